```python
import math
import jax, jax.numpy as jnp
from jax import lax
import numpy as np

D_MODEL = 1024
BATCH = 16
SEQ = 2048
DEPTH = 2

D_FF = 2816
HEAD_DIM_A = 64
N_HEADS_A = 8
N_KV_HEADS_A = 2
GROUP_A = N_HEADS_A // N_KV_HEADS_A
WINDOW = 128
BLOCK = 128
N_HEADS_B = 8
Q_LORA_RANK = 256
KV_LORA_RANK = 128
QK_NOPE_DIM = 64
QK_ROPE_DIM = 32
QK_DIM_B = QK_NOPE_DIM + QK_ROPE_DIM
V_DIM_B = 64
ROPE_BASE = 10000.0
WIDTH_A = N_HEADS_A * HEAD_DIM_A
WIDTH_B = N_HEADS_B * V_DIM_B
IN_SPLITS = (WIDTH_A, N_KV_HEADS_A * HEAD_DIM_A, N_KV_HEADS_A * HEAD_DIM_A,
             Q_LORA_RANK, KV_LORA_RANK, QK_ROPE_DIM, D_MODEL, D_MODEL)
IN_WIDTH = sum(IN_SPLITS)
IN_OFFSETS = tuple(int(o) for o in np.cumsum(IN_SPLITS)[:-1])
EPS = 1e-6
NEG = -1e30

kernel_name = "hybrid_swa_sink_mla_gated_macaron"


def rms_norm(x, g):
    xf = x.astype(jnp.float32)
    y = xf * lax.rsqrt(jnp.mean(xf * xf, axis=-1, keepdims=True) + EPS)
    return (y * g.astype(jnp.float32)).astype(x.dtype)


def swiglu(x, w_gate, w_up, w_down):
    return (jax.nn.silu(x @ w_gate) * (x @ w_up)) @ w_down


def apply_rope(x, cos, sin):
    half = x.shape[-1] // 2
    x1, x2 = x[..., :half], x[..., half:]
    return jnp.concatenate([x1 * cos - x2 * sin, x2 * cos + x1 * sin], axis=-1)


def alibi_slopes(n_heads):
    return jnp.exp2(-8.0 * (jnp.arange(n_heads, dtype=jnp.float32) + 1.0) / n_heads)


def swa_attention(q, k, v, positions, q_gain, k_gain, sinks):
    B, S = q.shape[0], q.shape[1]
    nb = S // BLOCK
    q = rms_norm(q, q_gain)
    k = rms_norm(k, k_gain)
    qb = q.reshape(B, nb, BLOCK, N_KV_HEADS_A, GROUP_A, HEAD_DIM_A)
    kb = k.reshape(B, nb, BLOCK, N_KV_HEADS_A, HEAD_DIM_A)
    vb = v.reshape(B, nb, BLOCK, N_KV_HEADS_A, HEAD_DIM_A)
    pad5 = ((0, 0), (1, 0), (0, 0), (0, 0), (0, 0))
    kk = jnp.concatenate([jnp.pad(kb, pad5)[:, :-1], kb], axis=2)
    vv = jnp.concatenate([jnp.pad(vb, pad5)[:, :-1], vb], axis=2)
    pb = positions.reshape(B, nb, BLOCK)
    pk = jnp.concatenate([jnp.pad(pb, ((0, 0), (1, 0), (0, 0)))[:, :-1], pb], axis=2)

    scale = HEAD_DIM_A ** -0.5
    s = jnp.einsum('bnqkgd,bnskd->bnkgqs', qb, kk).astype(jnp.float32) * scale
    dist = (pb[:, :, :, None] - pk[:, :, None, :]).astype(jnp.float32)
    slopes = alibi_slopes(N_HEADS_A).reshape(N_KV_HEADS_A, GROUP_A)
    s = s - slopes[None, None, :, :, None, None] * dist[:, :, None, None]

    qi = jnp.arange(BLOCK)[:, None] + BLOCK
    ki = jnp.arange(2 * BLOCK)[None, :]
    diff = qi - ki
    band = (diff >= 0) & (diff < WINDOW)
    has_prev = (jnp.arange(nb)[:, None, None] > 0) | (ki[None] >= BLOCK)
    valid = band[None] & has_prev
    s = jnp.where(valid[None, :, None, None], s, NEG)

    sink = sinks.astype(jnp.float32).reshape(N_KV_HEADS_A, GROUP_A)
    sink = jnp.broadcast_to(sink[None, None, :, :, None, None], s.shape[:-1] + (1,))
    p = jax.nn.softmax(jnp.concatenate([s, sink], axis=-1), axis=-1)[..., :-1]
    o = jnp.einsum('bnkgqs,bnskd->bnqkgd', p.astype(vv.dtype), vv)
    return o.reshape(B, S, WIDTH_A)


def mla_attention(c_q, c_kv, k_rope, cos, sin, q_lora_norm, w_uq, kv_lora_norm, w_ukv, q_gain, k_gain):
    B, S = c_q.shape[0], c_q.shape[1]
    nb = S // BLOCK
    q = (rms_norm(c_q, q_lora_norm) @ w_uq).reshape(B, S, N_HEADS_B, QK_DIM_B)
    kv = (rms_norm(c_kv, kv_lora_norm) @ w_ukv).reshape(B, S, N_HEADS_B, QK_NOPE_DIM + V_DIM_B)
    k_nope, v = kv[..., :QK_NOPE_DIM], kv[..., QK_NOPE_DIM:]
    k_r = jnp.broadcast_to(k_rope[:, :, None, :], (B, S, N_HEADS_B, QK_ROPE_DIM))
    k = jnp.concatenate([k_nope, k_r], axis=-1)
    q = rms_norm(q, q_gain)
    k = rms_norm(k, k_gain)
    q = jnp.concatenate([q[..., :QK_NOPE_DIM], apply_rope(q[..., QK_NOPE_DIM:], cos, sin)], axis=-1)
    k = jnp.concatenate([k[..., :QK_NOPE_DIM], apply_rope(k[..., QK_NOPE_DIM:], cos, sin)], axis=-1)

    scale = QK_DIM_B ** -0.5
    key_idx = jnp.arange(S)
    qb = jnp.moveaxis(q.reshape(B, nb, BLOCK, N_HEADS_B, QK_DIM_B), 1, 0)

    def one_block(args):
        q_blk, i = args
        s = jnp.einsum('bqhd,bshd->bhqs', q_blk, k).astype(jnp.float32) * scale
        t = i * BLOCK + jnp.arange(BLOCK)
        s = jnp.where(t[:, None] >= key_idx[None, :], s, NEG)
        p = jax.nn.softmax(s, axis=-1)
        return jnp.einsum('bhqs,bshd->bqhd', p.astype(v.dtype), v)

    o = lax.map(one_block, (qb, jnp.arange(nb)))
    return jnp.moveaxis(o, 0, 1).reshape(B, S, WIDTH_B)


def _fwd_setup_inputs(seed: int = 0) -> dict:
    key = jax.random.key(seed)
    ks = iter(jax.random.split(key, 32))
    L = DEPTH

    def w(shape, fan_in):
        return jax.random.normal(next(ks), shape, jnp.float32) * fan_in ** -0.5

    def gain(shape):
        return 1.0 + 0.02 * jax.random.normal(next(ks), shape, jnp.float32)

    x = jax.random.normal(next(ks), (BATCH, SEQ, D_MODEL), jnp.float32)
    start = jax.random.randint(next(ks), (BATCH, 1), 0, 4096, dtype=jnp.int32)
    positions = start + jnp.arange(SEQ, dtype=jnp.int32)[None, :]
    return {
        "x": x,
        "positions": positions,
        "ffn1_norm": gain((L, D_MODEL)),
        "ffn1_w_gate": w((L, D_MODEL, D_FF), D_MODEL),
        "ffn1_w_up": w((L, D_MODEL, D_FF), D_MODEL),
        "ffn1_w_down": w((L, D_FF, D_MODEL), D_FF),
        "mix_norm": gain((L, D_MODEL)),
        "w_in": w((L, D_MODEL, IN_WIDTH), D_MODEL),
        "swa_q_norm": gain((L, HEAD_DIM_A)),
        "swa_k_norm": gain((L, HEAD_DIM_A)),
        "swa_sinks": 0.5 * jax.random.normal(next(ks), (L, N_HEADS_A), jnp.float32),
        "mla_q_lora_norm": gain((L, Q_LORA_RANK)),
        "mla_w_uq": w((L, Q_LORA_RANK, N_HEADS_B * QK_DIM_B), Q_LORA_RANK),
        "mla_kv_lora_norm": gain((L, KV_LORA_RANK)),
        "mla_w_ukv": w((L, KV_LORA_RANK, N_HEADS_B * (QK_NOPE_DIM + V_DIM_B)), KV_LORA_RANK),
        "mla_q_norm": gain((L, QK_DIM_B)),
        "mla_k_norm": gain((L, QK_DIM_B)),
        "w_branch_a": w((L, WIDTH_A, D_MODEL), WIDTH_A),
        "w_branch_b": w((L, WIDTH_B, D_MODEL), WIDTH_B),
        "w_out": w((L, D_MODEL, D_MODEL), D_MODEL),
        "ffn2_norm": gain((L, D_MODEL)),
        "ffn2_w_gate": w((L, D_MODEL, D_FF), D_MODEL),
        "ffn2_w_up": w((L, D_MODEL, D_FF), D_MODEL),
        "ffn2_w_down": w((L, D_FF, D_MODEL), D_FF),
    }


def _fwd_reference(x, positions, ffn1_norm, ffn1_w_gate, ffn1_w_up, ffn1_w_down, mix_norm, w_in,
              swa_q_norm, swa_k_norm, swa_sinks, mla_q_lora_norm, mla_w_uq, mla_kv_lora_norm,
              mla_w_ukv, mla_q_norm, mla_k_norm, w_branch_a, w_branch_b, w_out,
              ffn2_norm, ffn2_w_gate, ffn2_w_up, ffn2_w_down):
    B, S, _ = x.shape
    half = QK_ROPE_DIM // 2
    inv_freq = ROPE_BASE ** (-jnp.arange(half, dtype=jnp.float32) / half)
    ang = positions.astype(jnp.float32)[..., None] * inv_freq
    cos = jnp.cos(ang)[:, :, None, :].astype(x.dtype)
    sin = jnp.sin(ang)[:, :, None, :].astype(x.dtype)
    k_rope_cos, k_rope_sin = cos[:, :, 0], sin[:, :, 0]

    for l in range(DEPTH):
        x = x + 0.5 * swiglu(rms_norm(x, ffn1_norm[l]), ffn1_w_gate[l], ffn1_w_up[l], ffn1_w_down[l])

        h = rms_norm(x, mix_norm[l])
        proj = h @ w_in[l]
        qa, ka, va, c_q, c_kv, k_r, g_a, g_b = jnp.split(proj, IN_OFFSETS, axis=-1)
        qa = qa.reshape(B, S, N_HEADS_A, HEAD_DIM_A)
        ka = ka.reshape(B, S, N_KV_HEADS_A, HEAD_DIM_A)
        va = va.reshape(B, S, N_KV_HEADS_A, HEAD_DIM_A)
        o_a = swa_attention(qa, ka, va, positions, swa_q_norm[l], swa_k_norm[l], swa_sinks[l])
        o_b = mla_attention(c_q, c_kv, k_r, cos, sin, mla_q_lora_norm[l], mla_w_uq[l],
                            mla_kv_lora_norm[l], mla_w_ukv[l], mla_q_norm[l], mla_k_norm[l])
        merged = jax.nn.sigmoid(g_a) * (o_a @ w_branch_a[l]) + jax.nn.sigmoid(g_b) * (o_b @ w_branch_b[l])
        x = x + merged @ w_out[l]

        x = x + 0.5 * swiglu(rms_norm(x, ffn2_norm[l]), ffn2_w_gate[l], ffn2_w_up[l], ffn2_w_down[l])
    return x


import jax as _jax
import jax.numpy as _jnp

TWIN_FORMAT = 'train_step'
FWD_PARAMS = ['x', 'positions', 'ffn1_norm', 'ffn1_w_gate', 'ffn1_w_up', 'ffn1_w_down', 'mix_norm', 'w_in', 'swa_q_norm', 'swa_k_norm', 'swa_sinks', 'mla_q_lora_norm', 'mla_w_uq', 'mla_kv_lora_norm', 'mla_w_ukv', 'mla_q_norm', 'mla_k_norm', 'w_branch_a', 'w_branch_b', 'w_out', 'ffn2_norm', 'ffn2_w_gate', 'ffn2_w_up', 'ffn2_w_down']
TWIN_WEIGHTS = ['ffn1_norm', 'ffn1_w_gate', 'ffn1_w_up', 'ffn1_w_down', 'mix_norm', 'w_in', 'swa_q_norm', 'swa_k_norm', 'swa_sinks', 'mla_q_lora_norm', 'mla_w_uq', 'mla_kv_lora_norm', 'mla_w_ukv', 'mla_q_norm', 'mla_k_norm', 'w_branch_a', 'w_branch_b', 'w_out', 'ffn2_norm', 'ffn2_w_gate', 'ffn2_w_up', 'ffn2_w_down']
TWIN_DIFF_INPUT = 'x'
TWIN_INPUTS = ['x', 'positions', 'ffn1_norm', 'ffn1_w_gate', 'ffn1_w_up', 'ffn1_w_down', 'mix_norm', 'w_in', 'swa_q_norm', 'swa_k_norm', 'swa_sinks', 'mla_q_lora_norm', 'mla_w_uq', 'mla_kv_lora_norm', 'mla_w_ukv', 'mla_q_norm', 'mla_k_norm', 'w_branch_a', 'w_branch_b', 'w_out', 'ffn2_norm', 'ffn2_w_gate', 'ffn2_w_up', 'ffn2_w_down', 'loss_target', 'm_ffn1_norm', 'm_ffn1_w_gate', 'm_ffn1_w_up', 'm_ffn1_w_down', 'm_mix_norm', 'm_w_in', 'm_swa_q_norm', 'm_swa_k_norm', 'm_swa_sinks', 'm_mla_q_lora_norm', 'm_mla_w_uq', 'm_mla_kv_lora_norm', 'm_mla_w_ukv', 'm_mla_q_norm', 'm_mla_k_norm', 'm_w_branch_a', 'm_w_branch_b', 'm_w_out', 'm_ffn2_norm', 'm_ffn2_w_gate', 'm_ffn2_w_up', 'm_ffn2_w_down', 'v_ffn1_norm', 'v_ffn1_w_gate', 'v_ffn1_w_up', 'v_ffn1_w_down', 'v_mix_norm', 'v_w_in', 'v_swa_q_norm', 'v_swa_k_norm', 'v_swa_sinks', 'v_mla_q_lora_norm', 'v_mla_w_uq', 'v_mla_kv_lora_norm', 'v_mla_w_ukv', 'v_mla_q_norm', 'v_mla_k_norm', 'v_w_branch_a', 'v_w_branch_b', 'v_w_out', 'v_ffn2_norm', 'v_ffn2_w_gate', 'v_ffn2_w_up', 'v_ffn2_w_down']
TWIN_OUTPUTS = ['loss', 'grad_x', 'grad_ffn1_norm', 'grad_ffn1_w_gate', 'grad_ffn1_w_up', 'grad_ffn1_w_down', 'grad_mix_norm', 'grad_w_in', 'grad_swa_q_norm', 'grad_swa_k_norm', 'grad_swa_sinks', 'grad_mla_q_lora_norm', 'grad_mla_w_uq', 'grad_mla_kv_lora_norm', 'grad_mla_w_ukv', 'grad_mla_q_norm', 'grad_mla_k_norm', 'grad_w_branch_a', 'grad_w_branch_b', 'grad_w_out', 'grad_ffn2_norm', 'grad_ffn2_w_gate', 'grad_ffn2_w_up', 'grad_ffn2_w_down', 'delta_ffn1_norm', 'delta_ffn1_w_gate', 'delta_ffn1_w_up', 'delta_ffn1_w_down', 'delta_mix_norm', 'delta_w_in', 'delta_swa_q_norm', 'delta_swa_k_norm', 'delta_swa_sinks', 'delta_mla_q_lora_norm', 'delta_mla_w_uq', 'delta_mla_kv_lora_norm', 'delta_mla_w_ukv', 'delta_mla_q_norm', 'delta_mla_k_norm', 'delta_w_branch_a', 'delta_w_branch_b', 'delta_w_out', 'delta_ffn2_norm', 'delta_ffn2_w_gate', 'delta_ffn2_w_up', 'delta_ffn2_w_down', 'new_m_ffn1_norm', 'new_m_ffn1_w_gate', 'new_m_ffn1_w_up', 'new_m_ffn1_w_down', 'new_m_mix_norm', 'new_m_w_in', 'new_m_swa_q_norm', 'new_m_swa_k_norm', 'new_m_swa_sinks', 'new_m_mla_q_lora_norm', 'new_m_mla_w_uq', 'new_m_mla_kv_lora_norm', 'new_m_mla_w_ukv', 'new_m_mla_q_norm', 'new_m_mla_k_norm', 'new_m_w_branch_a', 'new_m_w_branch_b', 'new_m_w_out', 'new_m_ffn2_norm', 'new_m_ffn2_w_gate', 'new_m_ffn2_w_up', 'new_m_ffn2_w_down', 'new_v_ffn1_norm', 'new_v_ffn1_w_gate', 'new_v_ffn1_w_up', 'new_v_ffn1_w_down', 'new_v_mix_norm', 'new_v_w_in', 'new_v_swa_q_norm', 'new_v_swa_k_norm', 'new_v_swa_sinks', 'new_v_mla_q_lora_norm', 'new_v_mla_w_uq', 'new_v_mla_kv_lora_norm', 'new_v_mla_w_ukv', 'new_v_mla_q_norm', 'new_v_mla_k_norm', 'new_v_w_branch_a', 'new_v_w_branch_b', 'new_v_w_out', 'new_v_ffn2_norm', 'new_v_ffn2_w_gate', 'new_v_ffn2_w_up', 'new_v_ffn2_w_down']
TWIN_LEAF_KINDS = {'loss': 'loss', 'grad_x': 'grad_x', 'grad_ffn1_norm': 'grad_w', 'grad_ffn1_w_gate': 'grad_w', 'grad_ffn1_w_up': 'grad_w', 'grad_ffn1_w_down': 'grad_w', 'grad_mix_norm': 'grad_w', 'grad_w_in': 'grad_w', 'grad_swa_q_norm': 'grad_w', 'grad_swa_k_norm': 'grad_w', 'grad_swa_sinks': 'grad_w', 'grad_mla_q_lora_norm': 'grad_w', 'grad_mla_w_uq': 'grad_w', 'grad_mla_kv_lora_norm': 'grad_w', 'grad_mla_w_ukv': 'grad_w', 'grad_mla_q_norm': 'grad_w', 'grad_mla_k_norm': 'grad_w', 'grad_w_branch_a': 'grad_w', 'grad_w_branch_b': 'grad_w', 'grad_w_out': 'grad_w', 'grad_ffn2_norm': 'grad_w', 'grad_ffn2_w_gate': 'grad_w', 'grad_ffn2_w_up': 'grad_w', 'grad_ffn2_w_down': 'grad_w', 'delta_ffn1_norm': 'delta_w', 'delta_ffn1_w_gate': 'delta_w', 'delta_ffn1_w_up': 'delta_w', 'delta_ffn1_w_down': 'delta_w', 'delta_mix_norm': 'delta_w', 'delta_w_in': 'delta_w', 'delta_swa_q_norm': 'delta_w', 'delta_swa_k_norm': 'delta_w', 'delta_swa_sinks': 'delta_w', 'delta_mla_q_lora_norm': 'delta_w', 'delta_mla_w_uq': 'delta_w', 'delta_mla_kv_lora_norm': 'delta_w', 'delta_mla_w_ukv': 'delta_w', 'delta_mla_q_norm': 'delta_w', 'delta_mla_k_norm': 'delta_w', 'delta_w_branch_a': 'delta_w', 'delta_w_branch_b': 'delta_w', 'delta_w_out': 'delta_w', 'delta_ffn2_norm': 'delta_w', 'delta_ffn2_w_gate': 'delta_w', 'delta_ffn2_w_up': 'delta_w', 'delta_ffn2_w_down': 'delta_w', 'new_m_ffn1_norm': 'new_m', 'new_m_ffn1_w_gate': 'new_m', 'new_m_ffn1_w_up': 'new_m', 'new_m_ffn1_w_down': 'new_m', 'new_m_mix_norm': 'new_m', 'new_m_w_in': 'new_m', 'new_m_swa_q_norm': 'new_m', 'new_m_swa_k_norm': 'new_m', 'new_m_swa_sinks': 'new_m', 'new_m_mla_q_lora_norm': 'new_m', 'new_m_mla_w_uq': 'new_m', 'new_m_mla_kv_lora_norm': 'new_m', 'new_m_mla_w_ukv': 'new_m', 'new_m_mla_q_norm': 'new_m', 'new_m_mla_k_norm': 'new_m', 'new_m_w_branch_a': 'new_m', 'new_m_w_branch_b': 'new_m', 'new_m_w_out': 'new_m', 'new_m_ffn2_norm': 'new_m', 'new_m_ffn2_w_gate': 'new_m', 'new_m_ffn2_w_up': 'new_m', 'new_m_ffn2_w_down': 'new_m', 'new_v_ffn1_norm': 'new_v', 'new_v_ffn1_w_gate': 'new_v', 'new_v_ffn1_w_up': 'new_v', 'new_v_ffn1_w_down': 'new_v', 'new_v_mix_norm': 'new_v', 'new_v_w_in': 'new_v', 'new_v_swa_q_norm': 'new_v', 'new_v_swa_k_norm': 'new_v', 'new_v_swa_sinks': 'new_v', 'new_v_mla_q_lora_norm': 'new_v', 'new_v_mla_w_uq': 'new_v', 'new_v_mla_kv_lora_norm': 'new_v', 'new_v_mla_w_ukv': 'new_v', 'new_v_mla_q_norm': 'new_v', 'new_v_mla_k_norm': 'new_v', 'new_v_w_branch_a': 'new_v', 'new_v_w_branch_b': 'new_v', 'new_v_w_out': 'new_v', 'new_v_ffn2_norm': 'new_v', 'new_v_ffn2_w_gate': 'new_v', 'new_v_ffn2_w_up': 'new_v', 'new_v_ffn2_w_down': 'new_v'}


def _forward(args):
    return _fwd_reference(*[args[k] for k in FWD_PARAMS])


def _output_shape():
    out = _jax.eval_shape(lambda: _forward(_fwd_setup_inputs(0)))
    return out.shape, out.dtype

N_MICROBATCH = 1
ADAM_LR = 0.001
ADAM_B1 = 0.9
ADAM_B2 = 0.999
ADAM_EPS = 1e-08
ADAM_WD = 0.01
ADAM_STEP = 10
PER_EXAMPLE_BATCH_AXIS = {'x': 0, 'positions': 0, 'loss_target': 0}
SHARED_INPUTS = []
_WEIGHT_DTYPES = {'ffn1_norm': _jnp.float32, 'ffn1_w_gate': _jnp.float32, 'ffn1_w_up': _jnp.float32, 'ffn1_w_down': _jnp.float32, 'mix_norm': _jnp.float32, 'w_in': _jnp.float32, 'swa_q_norm': _jnp.float32, 'swa_k_norm': _jnp.float32, 'swa_sinks': _jnp.float32, 'mla_q_lora_norm': _jnp.float32, 'mla_w_uq': _jnp.float32, 'mla_kv_lora_norm': _jnp.float32, 'mla_w_ukv': _jnp.float32, 'mla_q_norm': _jnp.float32, 'mla_k_norm': _jnp.float32, 'w_branch_a': _jnp.float32, 'w_branch_b': _jnp.float32, 'w_out': _jnp.float32, 'ffn2_norm': _jnp.float32, 'ffn2_w_gate': _jnp.float32, 'ffn2_w_up': _jnp.float32, 'ffn2_w_down': _jnp.float32}
MOMENT_SCALE = {'ffn1_norm': 6.167427e+00, 'ffn1_w_gate': 6.863114e-02, 'ffn1_w_up': 7.267202e-02, 'ffn1_w_down': 1.181147e-01, 'mix_norm': 6.014631e-01, 'w_in': 7.834467e-02, 'swa_q_norm': 5.340328e+00, 'swa_k_norm': 5.344671e+00, 'swa_sinks': 1.829074e+01, 'mla_q_lora_norm': 7.885552e-02, 'mla_w_uq': 4.438107e-02, 'mla_kv_lora_norm': 8.207601e-01, 'mla_w_ukv': 7.251921e-02, 'mla_q_norm': 5.353559e-01, 'mla_k_norm': 5.352562e-01, 'w_branch_a': 6.944170e-02, 'w_branch_b': 5.955295e-02, 'w_out': 8.532331e-02, 'ffn2_norm': 6.138542e+00, 'ffn2_w_gate': 6.215022e-02, 'ffn2_w_up': 7.062457e-02, 'ffn2_w_down': 1.141967e-01}


def _to_microbatches(a, axis):
    t = _jnp.moveaxis(a, axis, 0)
    t = t.reshape((N_MICROBATCH, t.shape[0] // N_MICROBATCH) + t.shape[1:])
    return _jnp.moveaxis(t, 1, axis + 1)


def setup_inputs(seed: int = 0) -> dict:
    inp = _fwd_setup_inputs(seed)
    key = _jax.random.fold_in(_jax.random.key(seed), 7919)
    shape, _ = _output_shape()
    out = dict(inp)
    out["loss_target"] = _jax.random.normal(_jax.random.fold_in(key, 0), shape, _jnp.float32)
    for i, name in enumerate(TWIN_WEIGHTS):
        w = inp[name].astype(_jnp.float32)
        if MOMENT_SCALE is None:
            s = _jnp.sqrt(_jnp.mean(_jnp.square(w)) + 1e-30)
        else:
            s = MOMENT_SCALE[name]
        km, kv = _jax.random.split(_jax.random.fold_in(key, i + 1))
        out[name] = w
        out["m_" + name] = s * _jax.random.normal(km, w.shape, _jnp.float32)
        out["v_" + name] = (s * s) * _jax.random.uniform(kv, w.shape, _jnp.float32, 0.5, 1.5)
    if N_MICROBATCH > 1:
        for name, axis in PER_EXAMPLE_BATCH_AXIS.items():
            out[name] = _to_microbatches(out[name], axis)
    return {'x': out['x'], 'positions': out['positions'], 'ffn1_norm': out['ffn1_norm'], 'ffn1_w_gate': out['ffn1_w_gate'], 'ffn1_w_up': out['ffn1_w_up'], 'ffn1_w_down': out['ffn1_w_down'], 'mix_norm': out['mix_norm'], 'w_in': out['w_in'], 'swa_q_norm': out['swa_q_norm'], 'swa_k_norm': out['swa_k_norm'], 'swa_sinks': out['swa_sinks'], 'mla_q_lora_norm': out['mla_q_lora_norm'], 'mla_w_uq': out['mla_w_uq'], 'mla_kv_lora_norm': out['mla_kv_lora_norm'], 'mla_w_ukv': out['mla_w_ukv'], 'mla_q_norm': out['mla_q_norm'], 'mla_k_norm': out['mla_k_norm'], 'w_branch_a': out['w_branch_a'], 'w_branch_b': out['w_branch_b'], 'w_out': out['w_out'], 'ffn2_norm': out['ffn2_norm'], 'ffn2_w_gate': out['ffn2_w_gate'], 'ffn2_w_up': out['ffn2_w_up'], 'ffn2_w_down': out['ffn2_w_down'], 'loss_target': out['loss_target'], 'm_ffn1_norm': out['m_ffn1_norm'], 'm_ffn1_w_gate': out['m_ffn1_w_gate'], 'm_ffn1_w_up': out['m_ffn1_w_up'], 'm_ffn1_w_down': out['m_ffn1_w_down'], 'm_mix_norm': out['m_mix_norm'], 'm_w_in': out['m_w_in'], 'm_swa_q_norm': out['m_swa_q_norm'], 'm_swa_k_norm': out['m_swa_k_norm'], 'm_swa_sinks': out['m_swa_sinks'], 'm_mla_q_lora_norm': out['m_mla_q_lora_norm'], 'm_mla_w_uq': out['m_mla_w_uq'], 'm_mla_kv_lora_norm': out['m_mla_kv_lora_norm'], 'm_mla_w_ukv': out['m_mla_w_ukv'], 'm_mla_q_norm': out['m_mla_q_norm'], 'm_mla_k_norm': out['m_mla_k_norm'], 'm_w_branch_a': out['m_w_branch_a'], 'm_w_branch_b': out['m_w_branch_b'], 'm_w_out': out['m_w_out'], 'm_ffn2_norm': out['m_ffn2_norm'], 'm_ffn2_w_gate': out['m_ffn2_w_gate'], 'm_ffn2_w_up': out['m_ffn2_w_up'], 'm_ffn2_w_down': out['m_ffn2_w_down'], 'v_ffn1_norm': out['v_ffn1_norm'], 'v_ffn1_w_gate': out['v_ffn1_w_gate'], 'v_ffn1_w_up': out['v_ffn1_w_up'], 'v_ffn1_w_down': out['v_ffn1_w_down'], 'v_mix_norm': out['v_mix_norm'], 'v_w_in': out['v_w_in'], 'v_swa_q_norm': out['v_swa_q_norm'], 'v_swa_k_norm': out['v_swa_k_norm'], 'v_swa_sinks': out['v_swa_sinks'], 'v_mla_q_lora_norm': out['v_mla_q_lora_norm'], 'v_mla_w_uq': out['v_mla_w_uq'], 'v_mla_kv_lora_norm': out['v_mla_kv_lora_norm'], 'v_mla_w_ukv': out['v_mla_w_ukv'], 'v_mla_q_norm': out['v_mla_q_norm'], 'v_mla_k_norm': out['v_mla_k_norm'], 'v_w_branch_a': out['v_w_branch_a'], 'v_w_branch_b': out['v_w_branch_b'], 'v_w_out': out['v_w_out'], 'v_ffn2_norm': out['v_ffn2_norm'], 'v_ffn2_w_gate': out['v_ffn2_w_gate'], 'v_ffn2_w_up': out['v_ffn2_w_up'], 'v_ffn2_w_down': out['v_ffn2_w_down']}


def _loss(weights, diff, rest, loss_target):
    with _jax.named_scope("forward"):
        args = {**rest, TWIN_DIFF_INPUT: diff, **{k: w.astype(_WEIGHT_DTYPES[k]) for k, w in weights.items()}}
        y = _forward(args)
    with _jax.named_scope("loss_head"):
        err = _jnp.square(y.astype(_jnp.float32) - loss_target)
        return 0.5 * _jnp.sum(_jnp.mean(err, axis=-1)) if err.ndim else 0.5 * err


def _adamw(w, g, m, v):
    m = ADAM_B1 * m + (1.0 - ADAM_B1) * g
    v = ADAM_B2 * v + (1.0 - ADAM_B2) * _jnp.square(g)
    m_hat = m / (1.0 - ADAM_B1 ** ADAM_STEP)
    v_hat = v / (1.0 - ADAM_B2 ** ADAM_STEP)
    delta = -ADAM_LR * (m_hat / (_jnp.sqrt(v_hat) + ADAM_EPS) + ADAM_WD * w)
    return delta, m, v


def reference(x, positions, ffn1_norm, ffn1_w_gate, ffn1_w_up, ffn1_w_down, mix_norm, w_in, swa_q_norm, swa_k_norm, swa_sinks, mla_q_lora_norm, mla_w_uq, mla_kv_lora_norm, mla_w_ukv, mla_q_norm, mla_k_norm, w_branch_a, w_branch_b, w_out, ffn2_norm, ffn2_w_gate, ffn2_w_up, ffn2_w_down, loss_target, m_ffn1_norm, m_ffn1_w_gate, m_ffn1_w_up, m_ffn1_w_down, m_mix_norm, m_w_in, m_swa_q_norm, m_swa_k_norm, m_swa_sinks, m_mla_q_lora_norm, m_mla_w_uq, m_mla_kv_lora_norm, m_mla_w_ukv, m_mla_q_norm, m_mla_k_norm, m_w_branch_a, m_w_branch_b, m_w_out, m_ffn2_norm, m_ffn2_w_gate, m_ffn2_w_up, m_ffn2_w_down, v_ffn1_norm, v_ffn1_w_gate, v_ffn1_w_up, v_ffn1_w_down, v_mix_norm, v_w_in, v_swa_q_norm, v_swa_k_norm, v_swa_sinks, v_mla_q_lora_norm, v_mla_w_uq, v_mla_kv_lora_norm, v_mla_w_ukv, v_mla_q_norm, v_mla_k_norm, v_w_branch_a, v_w_branch_b, v_w_out, v_ffn2_norm, v_ffn2_w_gate, v_ffn2_w_up, v_ffn2_w_down):
    given = dict(x=x, positions=positions, ffn1_norm=ffn1_norm, ffn1_w_gate=ffn1_w_gate, ffn1_w_up=ffn1_w_up, ffn1_w_down=ffn1_w_down, mix_norm=mix_norm, w_in=w_in, swa_q_norm=swa_q_norm, swa_k_norm=swa_k_norm, swa_sinks=swa_sinks, mla_q_lora_norm=mla_q_lora_norm, mla_w_uq=mla_w_uq, mla_kv_lora_norm=mla_kv_lora_norm, mla_w_ukv=mla_w_ukv, mla_q_norm=mla_q_norm, mla_k_norm=mla_k_norm, w_branch_a=w_branch_a, w_branch_b=w_branch_b, w_out=w_out, ffn2_norm=ffn2_norm, ffn2_w_gate=ffn2_w_gate, ffn2_w_up=ffn2_w_up, ffn2_w_down=ffn2_w_down, loss_target=loss_target, m_ffn1_norm=m_ffn1_norm, m_ffn1_w_gate=m_ffn1_w_gate, m_ffn1_w_up=m_ffn1_w_up, m_ffn1_w_down=m_ffn1_w_down, m_mix_norm=m_mix_norm, m_w_in=m_w_in, m_swa_q_norm=m_swa_q_norm, m_swa_k_norm=m_swa_k_norm, m_swa_sinks=m_swa_sinks, m_mla_q_lora_norm=m_mla_q_lora_norm, m_mla_w_uq=m_mla_w_uq, m_mla_kv_lora_norm=m_mla_kv_lora_norm, m_mla_w_ukv=m_mla_w_ukv, m_mla_q_norm=m_mla_q_norm, m_mla_k_norm=m_mla_k_norm, m_w_branch_a=m_w_branch_a, m_w_branch_b=m_w_branch_b, m_w_out=m_w_out, m_ffn2_norm=m_ffn2_norm, m_ffn2_w_gate=m_ffn2_w_gate, m_ffn2_w_up=m_ffn2_w_up, m_ffn2_w_down=m_ffn2_w_down, v_ffn1_norm=v_ffn1_norm, v_ffn1_w_gate=v_ffn1_w_gate, v_ffn1_w_up=v_ffn1_w_up, v_ffn1_w_down=v_ffn1_w_down, v_mix_norm=v_mix_norm, v_w_in=v_w_in, v_swa_q_norm=v_swa_q_norm, v_swa_k_norm=v_swa_k_norm, v_swa_sinks=v_swa_sinks, v_mla_q_lora_norm=v_mla_q_lora_norm, v_mla_w_uq=v_mla_w_uq, v_mla_kv_lora_norm=v_mla_kv_lora_norm, v_mla_w_ukv=v_mla_w_ukv, v_mla_q_norm=v_mla_q_norm, v_mla_k_norm=v_mla_k_norm, v_w_branch_a=v_w_branch_a, v_w_branch_b=v_w_branch_b, v_w_out=v_w_out, v_ffn2_norm=v_ffn2_norm, v_ffn2_w_gate=v_ffn2_w_gate, v_ffn2_w_up=v_ffn2_w_up, v_ffn2_w_down=v_ffn2_w_down)
    weights = {n: given[n] for n in TWIN_WEIGHTS}
    shared = {n: given[n] for n in SHARED_INPUTS}
    per_example = {n: given[n] for n in ['x', 'positions']}
    grad_fn = _jax.value_and_grad(_loss, argnums=(0, 1))

    def one_microbatch(ex, loss_target):
        ex = dict(ex)
        diff = ex.pop(TWIN_DIFF_INPUT)
        return grad_fn(weights, diff, {**shared, **ex}, loss_target)

    if N_MICROBATCH == 1:
        loss, (grad_w, grad_x) = one_microbatch(per_example, given["loss_target"])
    else:
        def body(carry, xs):
            loss_sum, grad_sum = carry
            l_k, (gw_k, gx_k) = one_microbatch(xs[0], xs[1])
            with _jax.named_scope("update"):
                return (loss_sum + l_k, _jax.tree.map(_jnp.add, grad_sum, gw_k)), gx_k

        init = (_jnp.zeros((), _jnp.float32), _jax.tree.map(_jnp.zeros_like, weights))
        (loss, grad_w), grad_x = _jax.lax.scan(body, init, (per_example, given["loss_target"]))
    with _jax.named_scope("update"):
        delta_w, new_m, new_v = {}, {}, {}
        for n in TWIN_WEIGHTS:
            delta_w[n], new_m[n], new_v[n] = _adamw(weights[n], grad_w[n], given["m_" + n], given["v_" + n])
    return (loss, grad_x, *[grad_w[n] for n in TWIN_WEIGHTS], *[delta_w[n] for n in TWIN_WEIGHTS],
            *[new_m[n] for n in TWIN_WEIGHTS], *[new_v[n] for n in TWIN_WEIGHTS])
```

```python
import functools
import math

import jax
import jax.numpy as jnp
from jax import lax
from jax.experimental import pallas as pl
from jax.experimental.pallas import tpu as pltpu

F32 = jnp.float32
BF16 = jnp.bfloat16

N_DEV = 8
DEPTH = 2
D_MODEL = 1024
D_FF = 2816
HEAD_DIM_A = 64
N_HEADS_A = 8
N_KV_HEADS_A = 2
GROUP_A = N_HEADS_A // N_KV_HEADS_A
BLOCK = 128
N_HEADS_B = 8
Q_LORA = 256
KV_LORA = 128
NOPE = 64
ROPE = 32
QK_B = NOPE + ROPE
V_B = 64
HEAD_PAD = 128
WIDTH_A = N_HEADS_A * HEAD_DIM_A
WIDTH_B = N_HEADS_B * V_B
WIDTH_BP = N_HEADS_B * HEAD_PAD
KV_A = N_KV_HEADS_A * HEAD_DIM_A
IN_WIDTH = WIDTH_A + 2 * KV_A + Q_LORA + KV_LORA + ROPE + 2 * D_MODEL
ROPE_BASE = 10000.0
EPS = 1e-6
NEG = -1e30

P_GA, P_GB, P_QA, P_CQ, P_KA, P_VA, P_CKV, P_KR = 0, 1024, 2048, 2560, 2816, 2944, 3072, 3200
PROJ_W = 3328

ADAM_LR, ADAM_B1, ADAM_B2, ADAM_EPS, ADAM_WD, ADAM_STEP = 0.001, 0.9, 0.999, 1e-08, 0.01, 10

VMEM_LIMIT = 56 * 1024 * 1024
LANES = 1024

BIG = (("ffn1_w_gate", True), ("ffn1_w_up", True), ("ffn1_w_down", False), ("w_in", True), ("mla_w_uq", True),
       ("mla_w_ukv", True), ("w_branch_a", True), ("w_branch_b", True), ("w_out", False),
       ("ffn2_w_gate", True), ("ffn2_w_up", True), ("ffn2_w_down", False))
SMALL = ("ffn1_norm", "mix_norm", "ffn2_norm", "swa_q_norm", "swa_k_norm", "swa_sinks", "mla_q_lora_norm",
         "mla_kv_lora_norm", "mla_q_norm", "mla_k_norm")
WEIGHTS = ("ffn1_norm", "ffn1_w_gate", "ffn1_w_up", "ffn1_w_down", "mix_norm", "w_in", "swa_q_norm", "swa_k_norm",
           "swa_sinks", "mla_q_lora_norm", "mla_w_uq", "mla_kv_lora_norm", "mla_w_ukv", "mla_q_norm", "mla_k_norm",
           "w_branch_a", "w_branch_b", "w_out", "ffn2_norm", "ffn2_w_gate", "ffn2_w_up", "ffn2_w_down")
FLAT_ROW_TILE = 384
SMALL_ROWS = 8


def _cp(*sem):
    return pltpu.CompilerParams(dimension_semantics=sem, vmem_limit_bytes=VMEM_LIMIT)


def _tile(n, prefs):
    for t in prefs:
        if n % t == 0:
            return t
    return n


def _dot(a, b, dims):
    return lax.dot_general(a, b, (dims, ((), ())), preferred_element_type=F32)


_NT = ((1,), (1,))
_NN = ((1,), (0,))
_TN = ((0,), (0,))


def _sigmoid(x):
    return 1.0 / (1.0 + jnp.exp(-x))


def _mm(pairs, mode, out_dtype, name, residual=None, alpha=1.0, tm=None, tn=None):
    a0, b0 = pairs[0]
    m = a0.shape[1] if mode == "tn" else a0.shape[0]
    n = b0.shape[0] if mode == "nt" else b0.shape[1]
    tm = tm or _tile(m, (512, 256, 128))
    tn = tn or _tile(n, (512, 256, 128))
    dims = {"nt": _NT, "nn": _NN, "tn": _TN}[mode]
    in_specs, args = [], []
    for a, b in pairs:
        assert a.dtype == BF16 and b.dtype == BF16, (name, a.dtype, b.dtype)
        if mode == "tn":
            in_specs.append(pl.BlockSpec((a.shape[0], tm), lambda i, j: (0, i)))
        else:
            in_specs.append(pl.BlockSpec((tm, a.shape[1]), lambda i, j: (i, 0)))
        if mode == "nt":
            in_specs.append(pl.BlockSpec((tn, b.shape[1]), lambda i, j: (j, 0)))
        else:
            in_specs.append(pl.BlockSpec((b.shape[0], tn), lambda i, j: (0, j)))
        args += [a, b]
    if residual is not None:
        in_specs.append(pl.BlockSpec((tm, tn), lambda i, j: (i, j)))
        args.append(residual)
    n_pairs = len(pairs)

    def body(*refs):
        o_ref = refs[-1]
        acc = None
        for p in range(n_pairs):
            d = _dot(refs[2 * p][...], refs[2 * p + 1][...], dims)
            acc = d if acc is None else acc + d
        if alpha != 1.0:
            acc = acc * alpha
        if residual is not None:
            acc = refs[2 * n_pairs][...] + acc
        o_ref[...] = acc.astype(out_dtype)

    return pl.pallas_call(
        body, name=name, grid=(m // tm, n // tn), in_specs=in_specs,
        out_specs=pl.BlockSpec((tm, tn), lambda i, j: (i, j)),
        out_shape=jax.ShapeDtypeStruct((m, n), out_dtype), compiler_params=_cp("parallel", "parallel"))(*args)


def _rms_fwd(x, g, name):
    t, d = x.shape
    tm = _tile(t, (512, 256, 128))

    def body(x_ref, g_ref, o_ref):
        xv = x_ref[...]
        r = lax.rsqrt(jnp.mean(xv * xv, axis=1, keepdims=True) + EPS)
        o_ref[...] = (xv * r * g_ref[...]).astype(BF16)

    return pl.pallas_call(
        body, name=name, grid=(t // tm,),
        in_specs=[pl.BlockSpec((tm, d), lambda i: (i, 0)), pl.BlockSpec((1, d), lambda i: (0, 0))],
        out_specs=pl.BlockSpec((tm, d), lambda i: (i, 0)),
        out_shape=jax.ShapeDtypeStruct((t, d), BF16), compiler_params=_cp("parallel"))(x, g)


def _rms_bwd(dn, x, g, res, name):
    t, d = x.shape
    tm = _tile(t, (512, 256, 128))

    def body(dn_ref, x_ref, g_ref, res_ref, dx_ref, dxb_ref, dg_ref):
        xv = x_ref[...]
        r = lax.rsqrt(jnp.mean(xv * xv, axis=1, keepdims=True) + EPS)
        xh = xv * r
        dnv = dn_ref[...]
        dxh = dnv * g_ref[...]
        dx = res_ref[...] + r * (dxh - xh * jnp.mean(dxh * xh, axis=1, keepdims=True))
        dx_ref[...] = dx
        dxb_ref[...] = dx.astype(BF16)

        @pl.when(pl.program_id(0) == 0)
        def _():
            dg_ref[...] = jnp.zeros_like(dg_ref)

        dg_ref[...] += jnp.sum(dnv * xh, axis=0, keepdims=True)

    row = pl.BlockSpec((tm, d), lambda i: (i, 0))
    one = pl.BlockSpec((1, d), lambda i: (0, 0))
    return pl.pallas_call(
        body, name=name, grid=(t // tm,), in_specs=[row, row, one, row], out_specs=[row, row, one],
        out_shape=[jax.ShapeDtypeStruct((t, d), F32), jax.ShapeDtypeStruct((t, d), BF16),
                   jax.ShapeDtypeStruct((1, d), F32)], compiler_params=_cp("arbitrary"))(dn, x, g, res)


def _ffn_up(n, wg_t, wu_t, name):
    t, d = n.shape
    f = wg_t.shape[0]
    tm, tn = _tile(t, (512, 256, 128)), 256

    def body(n_ref, wg_ref, wu_ref, a_ref, b_ref, h_ref):
        nv = n_ref[...]
        a = _dot(nv, wg_ref[...], _NT)
        b = _dot(nv, wu_ref[...], _NT)
        a_ref[...] = a.astype(BF16)
        b_ref[...] = b.astype(BF16)
        h_ref[...] = (a * _sigmoid(a) * b).astype(BF16)

    w_spec = pl.BlockSpec((tn, d), lambda i, j: (j, 0))
    o_spec = pl.BlockSpec((tm, tn), lambda i, j: (i, j))
    o_shape = jax.ShapeDtypeStruct((t, f), BF16)
    return pl.pallas_call(
        body, name=name, grid=(t // tm, f // tn),
        in_specs=[pl.BlockSpec((tm, d), lambda i, j: (i, 0)), w_spec, w_spec], out_specs=[o_spec] * 3,
        out_shape=[o_shape] * 3, compiler_params=_cp("parallel", "parallel"))(n, wg_t, wu_t)


def _ffn_down_bwd(dxb, wd, a, b, name):
    t, d = dxb.shape
    f = wd.shape[0]
    tm, tn = _tile(t, (512, 256, 128)), 256

    def body(dx_ref, wd_ref, a_ref, b_ref, da_ref, db_ref):
        dh = 0.5 * _dot(dx_ref[...], wd_ref[...], _NT)
        av = a_ref[...].astype(F32)
        bv = b_ref[...].astype(F32)
        sg = _sigmoid(av)
        da_ref[...] = (dh * bv * (sg * (1.0 + av * (1.0 - sg)))).astype(BF16)
        db_ref[...] = (dh * (av * sg)).astype(BF16)

    o_spec = pl.BlockSpec((tm, tn), lambda i, j: (i, j))
    o_shape = jax.ShapeDtypeStruct((t, f), BF16)
    return pl.pallas_call(
        body, name=name, grid=(t // tm, f // tn),
        in_specs=[pl.BlockSpec((tm, d), lambda i, j: (i, 0)), pl.BlockSpec((tn, d), lambda i, j: (j, 0)),
                  o_spec, o_spec],
        out_specs=[o_spec] * 2, out_shape=[o_shape] * 2,
        compiler_params=_cp("parallel", "parallel"))(dxb, wd, a, b)


def _loss_head(y, target):
    t, d = y.shape
    tm = _tile(t, (512, 256, 128))

    def body(y_ref, t_ref, dy_ref, dyb_ref, loss_ref, acc_ref):
        i = pl.program_id(0)
        e = y_ref[...] - t_ref[...]
        dy = e * (1.0 / d)
        dy_ref[...] = dy
        dyb_ref[...] = dy.astype(BF16)

        @pl.when(i == 0)
        def _():
            acc_ref[...] = jnp.zeros_like(acc_ref)

        acc_ref[...] += jnp.sum(e * e, axis=0, keepdims=True)

        @pl.when(i == pl.num_programs(0) - 1)
        def _():
            loss_ref[...] = jnp.sum(acc_ref[...], axis=1, keepdims=True) * (0.5 / d)

    row = pl.BlockSpec((tm, d), lambda i: (i, 0))
    return pl.pallas_call(
        body, name="loss_head", grid=(t // tm,), in_specs=[row, row],
        out_specs=[row, row, pl.BlockSpec((1, 1), lambda i: (0, 0))],
        out_shape=[jax.ShapeDtypeStruct((t, d), F32), jax.ShapeDtypeStruct((t, d), BF16),
                   jax.ShapeDtypeStruct((1, 1), F32)],
        scratch_shapes=[pltpu.VMEM((1, d), F32)], compiler_params=_cp("arbitrary"))(y, target)


def _merge_fwd(oa, ob, proj, wa_t, wb_t, name):
    t = oa.shape[0]
    d = wa_t.shape[0]
    tm, tn = _tile(t, (512, 256, 128)), 512
    nj = d // tn

    def body(oa_ref, ob_ref, ga_ref, gb_ref, wa_ref, wb_ref, mg_ref, ya_ref, yb_ref):
        ya = _dot(oa_ref[...], wa_ref[...], _NT)
        yb = _dot(ob_ref[...], wb_ref[...], _NT)
        mg_ref[...] = (_sigmoid(ga_ref[...]) * ya + _sigmoid(gb_ref[...]) * yb).astype(BF16)
        ya_ref[...] = ya.astype(BF16)
        yb_ref[...] = yb.astype(BF16)

    o_spec = pl.BlockSpec((tm, tn), lambda i, j: (i, j))
    o_shape = jax.ShapeDtypeStruct((t, d), BF16)
    return pl.pallas_call(
        body, name=name, grid=(t // tm, nj),
        in_specs=[pl.BlockSpec((tm, oa.shape[1]), lambda i, j: (i, 0)),
                  pl.BlockSpec((tm, ob.shape[1]), lambda i, j: (i, 0)),
                  pl.BlockSpec((tm, tn), lambda i, j: (i, P_GA // tn + j)),
                  pl.BlockSpec((tm, tn), lambda i, j: (i, P_GB // tn + j)),
                  pl.BlockSpec((tn, wa_t.shape[1]), lambda i, j: (j, 0)),
                  pl.BlockSpec((tn, wb_t.shape[1]), lambda i, j: (j, 0))],
        out_specs=[o_spec] * 3, out_shape=[o_shape] * 3,
        compiler_params=_cp("parallel", "parallel"))(oa, ob, proj, proj, wa_t, wb_t)


def _merge_bwd(dxb, wo, proj, ya, yb, name):
    t, d = dxb.shape
    tm, tn = _tile(t, (512, 256, 128)), 512

    def body(dx_ref, wo_ref, ga_ref, gb_ref, ya_ref, yb_ref, dya_ref, dyb_ref, dga_ref, dgb_ref):
        dm = _dot(dx_ref[...], wo_ref[...], _NT)
        sa = _sigmoid(ga_ref[...])
        sb = _sigmoid(gb_ref[...])
        dya_ref[...] = (dm * sa).astype(BF16)
        dyb_ref[...] = (dm * sb).astype(BF16)
        dga_ref[...] = (dm * ya_ref[...].astype(F32) * (sa * (1.0 - sa))).astype(BF16)
        dgb_ref[...] = (dm * yb_ref[...].astype(F32) * (sb * (1.0 - sb))).astype(BF16)

    o_spec = pl.BlockSpec((tm, tn), lambda i, j: (i, j))
    o_shape = jax.ShapeDtypeStruct((t, d), BF16)
    return pl.pallas_call(
        body, name=name, grid=(t // tm, d // tn),
        in_specs=[pl.BlockSpec((tm, d), lambda i, j: (i, 0)), pl.BlockSpec((tn, d), lambda i, j: (j, 0)),
                  pl.BlockSpec((tm, tn), lambda i, j: (i, P_GA // tn + j)),
                  pl.BlockSpec((tm, tn), lambda i, j: (i, P_GB // tn + j)), o_spec, o_spec],
        out_specs=[o_spec] * 4, out_shape=[o_shape] * 4,
        compiler_params=_cp("parallel", "parallel"))(dxb, wo, proj, proj, ya, yb)


def _swa_scores(qh, kk, dist, valid, slope):
    s = _dot(qh, kk, _NT) * (HEAD_DIM_A ** -0.5) - slope * dist
    return jnp.where(valid, s, NEG)


def _swa_common(i, pq_ref, pkp_ref, pkc_ref):
    pk = jnp.concatenate([pkp_ref[0], pkc_ref[0]], axis=1)
    dist = (pq_ref[...] - pk).astype(F32)
    row = lax.broadcasted_iota(jnp.int32, (BLOCK, 2 * BLOCK), 0)
    col = lax.broadcasted_iota(jnp.int32, (BLOCK, 2 * BLOCK), 1)
    diff = row + BLOCK - col
    valid = (diff >= 0) & (diff < BLOCK) & ((i > 0) | (col >= BLOCK))
    return dist, valid


def _head_norm(x, gain):
    r = lax.rsqrt(jnp.mean(x * x, axis=1, keepdims=True) + EPS)
    xh = x * r
    return xh * gain, xh, r


def _swa_specs(s_len, tq_blocks):
    nb = s_len // BLOCK

    def rowblk(b, i):
        return b * nb + i

    def prevblk(b, i):
        return b * nb + jnp.maximum(i - 1, 0)

    q_spec = pl.BlockSpec((BLOCK, WIDTH_A), lambda b, i: (rowblk(b, i), P_QA // WIDTH_A))
    kc_spec = pl.BlockSpec((BLOCK, KV_A), lambda b, i: (rowblk(b, i), P_KA // KV_A))
    kp_spec = pl.BlockSpec((BLOCK, KV_A), lambda b, i: (prevblk(b, i), P_KA // KV_A))
    vc_spec = pl.BlockSpec((BLOCK, KV_A), lambda b, i: (rowblk(b, i), P_VA // KV_A))
    vp_spec = pl.BlockSpec((BLOCK, KV_A), lambda b, i: (prevblk(b, i), P_VA // KV_A))
    pq_spec = pl.BlockSpec((BLOCK, 1), lambda b, i: (rowblk(b, i), 0))
    pkc_spec = pl.BlockSpec((1, 1, BLOCK), lambda b, i: (rowblk(b, i), 0, 0))
    pkp_spec = pl.BlockSpec((1, 1, BLOCK), lambda b, i: (prevblk(b, i), 0, 0))
    return nb, rowblk, [q_spec, kc_spec, kp_spec, vc_spec, vp_spec, pq_spec, pkc_spec, pkp_spec]


def _swa_fwd(proj, pos_col, pos_row, qg, kg, sinks, n_batch, s_len, name):
    t = proj.shape[0]
    nb, rowblk, specs = _swa_specs(s_len, None)
    small = lambda w: pl.BlockSpec((1, w), lambda b, i: (0, 0))

    def body(q_ref, kc_ref, kp_ref, vc_ref, vp_ref, pq_ref, pkc_ref, pkp_ref, qg_ref, kg_ref, sk_ref, o_ref):
        i = pl.program_id(1)
        dist, valid = _swa_common(i, pq_ref, pkp_ref, pkc_ref)
        for kv in range(N_KV_HEADS_A):
            ls = slice(kv * HEAD_DIM_A, (kv + 1) * HEAD_DIM_A)
            kraw = jnp.concatenate([kp_ref[:, ls], kc_ref[:, ls]], axis=0)
            kk = _head_norm(kraw, kg_ref[...])[0].astype(BF16)
            vv = jnp.concatenate([vp_ref[:, ls], vc_ref[:, ls]], axis=0).astype(BF16)
            for g in range(GROUP_A):
                h = kv * GROUP_A + g
                hs = slice(h * HEAD_DIM_A, (h + 1) * HEAD_DIM_A)
                qh = _head_norm(q_ref[:, hs], qg_ref[...])[0].astype(BF16)
                s = _swa_scores(qh, kk, dist, valid, 2.0 ** (-(h + 1)))
                sink = sk_ref[:, h:h + 1]
                m = jnp.maximum(jnp.max(s, axis=1, keepdims=True), sink)
                e = jnp.exp(s - m)
                den = jnp.sum(e, axis=1, keepdims=True) + jnp.exp(sink - m)
                p = (e / den).astype(BF16)
                o_ref[:, hs] = _dot(p, vv, _NN).astype(BF16)

    return pl.pallas_call(
        body, name=name, grid=(n_batch, nb),
        in_specs=specs + [small(HEAD_DIM_A), small(HEAD_DIM_A), small(HEAD_PAD)],
        out_specs=pl.BlockSpec((BLOCK, WIDTH_A), lambda b, i: (rowblk(b, i), 0)),
        out_shape=jax.ShapeDtypeStruct((t, WIDTH_A), BF16),
        compiler_params=_cp("parallel", "parallel"))(proj, proj, proj, proj, proj, pos_col, pos_row, pos_row,
                                                     qg, kg, sinks)


def _swa_bwd(proj, pos_col, pos_row, qg, kg, sinks, do, n_batch, s_len, name):
    t = proj.shape[0]
    nb, rowblk, specs = _swa_specs(s_len, None)
    small = lambda w: pl.BlockSpec((1, w), lambda b, i: (0, 0))
    scale = HEAD_DIM_A ** -0.5

    def body(q_ref, kc_ref, kp_ref, vc_ref, vp_ref, pq_ref, pkc_ref, pkp_ref, qg_ref, kg_ref, sk_ref, do_ref,
             dq_ref, dkc_ref, dkp_ref, dvc_ref, dvp_ref, dqg_ref, dsk_ref):
        b, i = pl.program_id(0), pl.program_id(1)

        @pl.when((b == 0) & (i == 0))
        def _():
            dqg_ref[...] = jnp.zeros_like(dqg_ref)
            dsk_ref[...] = jnp.zeros_like(dsk_ref)

        dist, valid = _swa_common(i, pq_ref, pkp_ref, pkc_ref)
        lane = lax.broadcasted_iota(jnp.int32, (1, HEAD_PAD), 1)
        dqg = jnp.zeros((1, HEAD_DIM_A), F32)
        dsk = jnp.zeros((1, HEAD_PAD), F32)
        for kv in range(N_KV_HEADS_A):
            ls = slice(kv * HEAD_DIM_A, (kv + 1) * HEAD_DIM_A)
            kraw = jnp.concatenate([kp_ref[:, ls], kc_ref[:, ls]], axis=0)
            kk = _head_norm(kraw, kg_ref[...])[0].astype(BF16)
            vv = jnp.concatenate([vp_ref[:, ls], vc_ref[:, ls]], axis=0).astype(BF16)
            dkk = jnp.zeros((2 * BLOCK, HEAD_DIM_A), F32)
            dvv = jnp.zeros((2 * BLOCK, HEAD_DIM_A), F32)
            for g in range(GROUP_A):
                h = kv * GROUP_A + g
                hs = slice(h * HEAD_DIM_A, (h + 1) * HEAD_DIM_A)
                qn, qxh, qr = _head_norm(q_ref[:, hs], qg_ref[...])
                qh = qn.astype(BF16)
                s = _swa_scores(qh, kk, dist, valid, 2.0 ** (-(h + 1)))
                sink = sk_ref[:, h:h + 1]
                m = jnp.maximum(jnp.max(s, axis=1, keepdims=True), sink)
                e = jnp.exp(s - m)
                es = jnp.exp(sink - m)
                den = jnp.sum(e, axis=1, keepdims=True) + es
                p = e / den
                doh = do_ref[:, hs]
                dp = _dot(doh, vv, _NT)
                delta = jnp.sum(p * dp, axis=1, keepdims=True)
                ds = (p * (dp - delta) * scale).astype(BF16)
                dsk = dsk + jnp.where(lane == h, -jnp.sum((es / den) * delta), 0.0)
                dvv = dvv + _dot(p.astype(BF16), doh, _TN)
                dkk = dkk + _dot(ds, qh, _TN)
                dqn = _dot(ds, kk, _NN)
                dqg = dqg + jnp.sum(dqn * qxh, axis=0, keepdims=True)
                dxh = dqn * qg_ref[...]
                dq_ref[:, hs] = (qr * (dxh - qxh * jnp.mean(dxh * qxh, axis=1, keepdims=True))).astype(BF16)
            dkp_ref[:, ls] = dkk[:BLOCK]
            dkc_ref[:, ls] = dkk[BLOCK:]
            dvp_ref[:, ls] = dvv[:BLOCK]
            dvc_ref[:, ls] = dvv[BLOCK:]
        dqg_ref[...] += dqg
        dsk_ref[...] += dsk

    kv_out = pl.BlockSpec((BLOCK, KV_A), lambda b, i: (rowblk(b, i), 0))
    kv_shape = jax.ShapeDtypeStruct((t, KV_A), F32)
    return pl.pallas_call(
        body, name=name, grid=(n_batch, nb),
        in_specs=specs + [small(HEAD_DIM_A), small(HEAD_DIM_A), small(HEAD_PAD),
                          pl.BlockSpec((BLOCK, WIDTH_A), lambda b, i: (rowblk(b, i), 0))],
        out_specs=[pl.BlockSpec((BLOCK, WIDTH_A), lambda b, i: (rowblk(b, i), 0)), kv_out, kv_out, kv_out, kv_out,
                   small(HEAD_DIM_A), small(HEAD_PAD)],
        out_shape=[jax.ShapeDtypeStruct((t, WIDTH_A), BF16), kv_shape, kv_shape, kv_shape, kv_shape,
                   jax.ShapeDtypeStruct((1, HEAD_DIM_A), F32), jax.ShapeDtypeStruct((1, HEAD_PAD), F32)],
        compiler_params=_cp("arbitrary", "arbitrary"))(proj, proj, proj, proj, proj, pos_col, pos_row, pos_row,
                                                       qg, kg, sinks, do)


def _swa_kv_bwd(proj, kg, dkc, dkp, dvc, dvp, n_batch, s_len, name):
    t = proj.shape[0]
    nb = s_len // BLOCK

    def rowblk(b, i):
        return b * nb + i

    def nextblk(b, i):
        return b * nb + jnp.minimum(i + 1, nb - 1)

    def body(k_ref, kg_ref, dkc_ref, dkp_ref, dvc_ref, dvp_ref, dk_ref, dv_ref, dkg_ref):
        b, i = pl.program_id(0), pl.program_id(1)

        @pl.when((b == 0) & (i == 0))
        def _():
            dkg_ref[...] = jnp.zeros_like(dkg_ref)

        has_next = (i < nb - 1).astype(F32)
        dkn = dkc_ref[...] + has_next * dkp_ref[...]
        dv_ref[...] = (dvc_ref[...] + has_next * dvp_ref[...]).astype(BF16)
        dkg = jnp.zeros((1, HEAD_DIM_A), F32)
        for kv in range(N_KV_HEADS_A):
            ls = slice(kv * HEAD_DIM_A, (kv + 1) * HEAD_DIM_A)
            _, kxh, kr = _head_norm(k_ref[:, ls], kg_ref[...])
            d = dkn[:, ls]
            dkg = dkg + jnp.sum(d * kxh, axis=0, keepdims=True)
            dxh = d * kg_ref[...]
            dk_ref[:, ls] = (kr * (dxh - kxh * jnp.mean(dxh * kxh, axis=1, keepdims=True))).astype(BF16)
        dkg_ref[...] += dkg

    cur = pl.BlockSpec((BLOCK, KV_A), lambda b, i: (rowblk(b, i), 0))
    nxt = pl.BlockSpec((BLOCK, KV_A), lambda b, i: (nextblk(b, i), 0))
    small = pl.BlockSpec((1, HEAD_DIM_A), lambda b, i: (0, 0))
    return pl.pallas_call(
        body, name=name, grid=(n_batch, nb),
        in_specs=[pl.BlockSpec((BLOCK, KV_A), lambda b, i: (rowblk(b, i), P_KA // KV_A)), small, cur, nxt, cur, nxt],
        out_specs=[cur, cur, small],
        out_shape=[jax.ShapeDtypeStruct((t, KV_A), BF16), jax.ShapeDtypeStruct((t, KV_A), BF16),
                   jax.ShapeDtypeStruct((1, HEAD_DIM_A), F32)],
        compiler_params=_cp("arbitrary", "arbitrary"))(proj, kg, dkc, dkp, dvc, dvp)


def _rope(u, c, sm, sp):
    return u * c + pltpu.roll(u, HEAD_PAD - ROPE // 2, 1) * sm + pltpu.roll(u, ROPE // 2, 1) * sp


def _rope_t(d, c, sm, sp):
    return d * c + pltpu.roll(d * sm, ROPE // 2, 1) + pltpu.roll(d * sp, HEAD_PAD - ROPE // 2, 1)


def _pad_norm(x, gain):
    r = lax.rsqrt(jnp.sum(x * x, axis=1, keepdims=True) * (1.0 / QK_B) + EPS)
    xh = x * r
    return xh * gain, xh, r


def _pad_norm_bwd(d, xh, r, gain):
    dxh = d * gain
    return r * (dxh - xh * (jnp.sum(dxh * xh, axis=1, keepdims=True) * (1.0 / QK_B)))


def _lora_norm(x, gain):
    r = lax.rsqrt(jnp.mean(x * x, axis=1, keepdims=True) + EPS)
    xh = x * r
    return xh * gain, xh, r


def _mla_in_specs(tm):
    row = lambda w, off: pl.BlockSpec((tm, w), lambda i: (i, off // w))
    one = lambda w: pl.BlockSpec((1, w), lambda i: (0, 0))
    full = lambda r, c: pl.BlockSpec((r, c), lambda i: (0, 0))
    tab = pl.BlockSpec((tm, HEAD_PAD), lambda i: (i, 0))
    return [row(Q_LORA, P_CQ), row(KV_LORA, P_CKV), row(HEAD_PAD, P_KR), tab, tab, tab,
            one(Q_LORA), one(KV_LORA), one(HEAD_PAD), one(HEAD_PAD),
            full(WIDTH_BP, Q_LORA), full(WIDTH_BP, KV_LORA), full(WIDTH_BP, KV_LORA)]


def _mla_pre(proj, tabs, gq, gkv, gqn, gkn, wuq, wk, wv, name):
    t = proj.shape[0]
    tm = _tile(t, (256, 128))

    def body(cq_ref, ckv_ref, kr_ref, c_ref, sm_ref, sp_ref, gq_ref, gkv_ref, gqn_ref, gkn_ref,
             wuq_ref, wk_ref, wv_ref, q_ref, k_ref, v_ref):
        cqn = _lora_norm(cq_ref[...], gq_ref[...])[0].astype(BF16)
        ckvn = _lora_norm(ckv_ref[...], gkv_ref[...])[0].astype(BF16)
        q_raw = _dot(cqn, wuq_ref[...], _NT)
        k_raw = _dot(ckvn, wk_ref[...], _NT)
        v_ref[...] = _dot(ckvn, wv_ref[...], _NT).astype(BF16)
        kr = pltpu.roll(kr_ref[...], NOPE, 1)
        c, sm, sp = c_ref[...], sm_ref[...], sp_ref[...]
        for h in range(N_HEADS_B):
            hs = slice(h * HEAD_PAD, (h + 1) * HEAD_PAD)
            q_ref[:, hs] = _rope(_pad_norm(q_raw[:, hs], gqn_ref[...])[0], c, sm, sp).astype(BF16)
            k_ref[:, hs] = _rope(_pad_norm(k_raw[:, hs] + kr, gkn_ref[...])[0], c, sm, sp).astype(BF16)

    o_spec = pl.BlockSpec((tm, WIDTH_BP), lambda i: (i, 0))
    o_shape = jax.ShapeDtypeStruct((t, WIDTH_BP), BF16)
    return pl.pallas_call(
        body, name=name, grid=(t // tm,), in_specs=_mla_in_specs(tm), out_specs=[o_spec] * 3,
        out_shape=[o_shape] * 3, compiler_params=_cp("parallel"))(
            proj, proj, proj, *tabs, gq, gkv, gqn, gkn, wuq, wk, wv)


def _mla_pre_bwd(proj, tabs, gq, gkv, gqn, gkn, wuq, wk, wv, dq, dk, dv, name):
    t = proj.shape[0]
    tm = _tile(t, (256, 128))

    def body(cq_ref, ckv_ref, kr_ref, c_ref, sm_ref, sp_ref, gq_ref, gkv_ref, gqn_ref, gkn_ref,
             wuq_ref, wk_ref, wv_ref, dq_ref, dk_ref, dv_ref,
             dcq_ref, dckv_ref, dkr_ref, dwuq_ref, dwk_ref, dwv_ref, dgq_ref, dgkv_ref, dgqn_ref, dgkn_ref,
             dqraw_ref, dkraw_ref):
        @pl.when(pl.program_id(0) == 0)
        def _():
            for r in (dwuq_ref, dwk_ref, dwv_ref, dgq_ref, dgkv_ref, dgqn_ref, dgkn_ref):
                r[...] = jnp.zeros_like(r)

        cqn_f, cq_xh, cq_r = _lora_norm(cq_ref[...], gq_ref[...])
        ckvn_f, ckv_xh, ckv_r = _lora_norm(ckv_ref[...], gkv_ref[...])
        cqn, ckvn = cqn_f.astype(BF16), ckvn_f.astype(BF16)
        q_raw = _dot(cqn, wuq_ref[...], _NT)
        k_raw = _dot(ckvn, wk_ref[...], _NT)
        kr = pltpu.roll(kr_ref[...], NOPE, 1)
        c, sm, sp = c_ref[...], sm_ref[...], sp_ref[...]
        dgqn = jnp.zeros((1, HEAD_PAD), F32)
        dgkn = jnp.zeros((1, HEAD_PAD), F32)
        dkr = jnp.zeros((tm, HEAD_PAD), F32)
        for h in range(N_HEADS_B):
            hs = slice(h * HEAD_PAD, (h + 1) * HEAD_PAD)
            _, xh, r = _pad_norm(q_raw[:, hs], gqn_ref[...])
            dn = _rope_t(dq_ref[:, hs], c, sm, sp)
            dgqn = dgqn + jnp.sum(dn * xh, axis=0, keepdims=True)
            dqraw_ref[:, hs] = _pad_norm_bwd(dn, xh, r, gqn_ref[...]).astype(BF16)
            _, xh, r = _pad_norm(k_raw[:, hs] + kr, gkn_ref[...])
            dn = _rope_t(dk_ref[:, hs], c, sm, sp)
            dgkn = dgkn + jnp.sum(dn * xh, axis=0, keepdims=True)
            dkc = _pad_norm_bwd(dn, xh, r, gkn_ref[...])
            dkraw_ref[:, hs] = dkc.astype(BF16)
            dkr = dkr + dkc
        dgqn_ref[...] += dgqn
        dgkn_ref[...] += dgkn
        lane = lax.broadcasted_iota(jnp.int32, (tm, HEAD_PAD), 1)
        dkr_ref[...] = jnp.where(lane < ROPE, pltpu.roll(dkr, HEAD_PAD - NOPE, 1), 0.0).astype(BF16)
        dqraw = dqraw_ref[...]
        dkraw = dkraw_ref[...]
        dvb = dv_ref[...].astype(BF16)
        dwuq_ref[...] += _dot(dqraw, cqn, _TN)
        dwk_ref[...] += _dot(dkraw, ckvn, _TN)
        dwv_ref[...] += _dot(dvb, ckvn, _TN)
        dcqn = _dot(dqraw, wuq_ref[...], _NN)
        dckvn = _dot(dkraw, wk_ref[...], _NN) + _dot(dvb, wv_ref[...], _NN)
        dgq_ref[...] += jnp.sum(dcqn * cq_xh, axis=0, keepdims=True)
        dgkv_ref[...] += jnp.sum(dckvn * ckv_xh, axis=0, keepdims=True)
        dxh = dcqn * gq_ref[...]
        dcq_ref[...] = (cq_r * (dxh - cq_xh * jnp.mean(dxh * cq_xh, axis=1, keepdims=True))).astype(BF16)
        dxh = dckvn * gkv_ref[...]
        dckv_ref[...] = (ckv_r * (dxh - ckv_xh * jnp.mean(dxh * ckv_xh, axis=1, keepdims=True))).astype(BF16)

    wide = pl.BlockSpec((tm, WIDTH_BP), lambda i: (i, 0))
    row = lambda w: pl.BlockSpec((tm, w), lambda i: (i, 0))
    full = lambda r, c: pl.BlockSpec((r, c), lambda i: (0, 0))
    return pl.pallas_call(
        body, name=name, grid=(t // tm,), in_specs=_mla_in_specs(tm) + [wide, wide, wide],
        out_specs=[row(Q_LORA), row(KV_LORA), row(HEAD_PAD), full(WIDTH_BP, Q_LORA), full(WIDTH_BP, KV_LORA),
                   full(WIDTH_BP, KV_LORA), full(1, Q_LORA), full(1, KV_LORA), full(1, HEAD_PAD), full(1, HEAD_PAD)],
        out_shape=[jax.ShapeDtypeStruct((t, Q_LORA), BF16), jax.ShapeDtypeStruct((t, KV_LORA), BF16),
                   jax.ShapeDtypeStruct((t, HEAD_PAD), BF16), jax.ShapeDtypeStruct((WIDTH_BP, Q_LORA), F32),
                   jax.ShapeDtypeStruct((WIDTH_BP, KV_LORA), F32), jax.ShapeDtypeStruct((WIDTH_BP, KV_LORA), F32),
                   jax.ShapeDtypeStruct((1, Q_LORA), F32), jax.ShapeDtypeStruct((1, KV_LORA), F32),
                   jax.ShapeDtypeStruct((1, HEAD_PAD), F32), jax.ShapeDtypeStruct((1, HEAD_PAD), F32)],
        scratch_shapes=[pltpu.VMEM((tm, WIDTH_BP), BF16), pltpu.VMEM((tm, WIDTH_BP), BF16)],
        compiler_params=_cp("arbitrary"))(proj, proj, proj, *tabs, gq, gkv, gqn, gkn, wuq, wk, wv, dq, dk, dv)


def _mla_flash_specs(s_len, tq):
    nq = s_len // tq
    q_spec = pl.BlockSpec((tq, HEAD_PAD), lambda b, h, i: (b * nq + i, h))
    kv_spec = pl.BlockSpec((s_len, HEAD_PAD), lambda b, h, i: (b, h))
    lse_spec = pl.BlockSpec((1, tq, 1), lambda b, h, i: (b * N_HEADS_B + h, i, 0))
    return nq, q_spec, kv_spec, lse_spec


def _causal(s, i, tq):
    row = lax.broadcasted_iota(jnp.int32, s.shape, 0) + i * tq
    col = lax.broadcasted_iota(jnp.int32, s.shape, 1)
    return jnp.where(row >= col, s, NEG)


def _mla_flash_fwd(q, k, v, n_batch, s_len, name):
    t = q.shape[0]
    tq = _tile(s_len, (256, 128))
    nq, q_spec, kv_spec, lse_spec = _mla_flash_specs(s_len, tq)

    def body(q_ref, k_ref, v_ref, o_ref, lse_ref):
        s = _causal(_dot(q_ref[...], k_ref[...], _NT) * (QK_B ** -0.5), pl.program_id(2), tq)
        m = jnp.max(s, axis=1, keepdims=True)
        e = jnp.exp(s - m)
        l = jnp.sum(e, axis=1, keepdims=True)
        o_ref[...] = _dot((e / l).astype(BF16), v_ref[...], _NN).astype(BF16)
        lse_ref[0] = m + jnp.log(l)

    return pl.pallas_call(
        body, name=name, grid=(n_batch, N_HEADS_B, nq), in_specs=[q_spec, kv_spec, kv_spec],
        out_specs=[q_spec, lse_spec],
        out_shape=[jax.ShapeDtypeStruct((t, WIDTH_BP), BF16),
                   jax.ShapeDtypeStruct((n_batch * N_HEADS_B, s_len, 1), F32)],
        compiler_params=_cp("parallel", "parallel", "parallel"))(q, k, v)


def _mla_flash_bwd(q, k, v, do, lse, n_batch, s_len, name):
    t = q.shape[0]
    tq = _tile(s_len, (256, 128))
    nq, q_spec, kv_spec, lse_spec = _mla_flash_specs(s_len, tq)
    scale = QK_B ** -0.5

    def body(q_ref, k_ref, v_ref, do_ref, lse_ref, dq_ref, dk_ref, dv_ref):
        i = pl.program_id(2)

        @pl.when(i == 0)
        def _():
            dk_ref[...] = jnp.zeros_like(dk_ref)
            dv_ref[...] = jnp.zeros_like(dv_ref)

        qv, kv, dov = q_ref[...], k_ref[...], do_ref[...]
        s = _causal(_dot(qv, kv, _NT) * scale, i, tq)
        p = jnp.exp(s - lse_ref[0])
        dp = _dot(dov, v_ref[...], _NT)
        delta = jnp.sum(p * dp, axis=1, keepdims=True)
        ds = (p * (dp - delta) * scale).astype(BF16)
        dq_ref[...] = _dot(ds, kv, _NN)
        dk_ref[...] += _dot(ds, qv, _TN)
        dv_ref[...] += _dot(p.astype(BF16), dov, _TN)

    f32_wide = jax.ShapeDtypeStruct((t, WIDTH_BP), F32)
    return pl.pallas_call(
        body, name=name, grid=(n_batch, N_HEADS_B, nq), in_specs=[q_spec, kv_spec, kv_spec, q_spec, lse_spec],
        out_specs=[q_spec, kv_spec, kv_spec], out_shape=[f32_wide] * 3,
        compiler_params=_cp("parallel", "parallel", "arbitrary"))(q, k, v, do, lse)


def _layer_mats(w):
    w_in = w["w_in"]
    o = [0]
    for n in (WIDTH_A, KV_A, KV_A, Q_LORA, KV_LORA, ROPE, D_MODEL, D_MODEL):
        o.append(o[-1] + n)
    qa, ka, va, cq, ckv, kr, ga, gb = (w_in[o[i]:o[i + 1]] for i in range(8))
    pad = jnp.zeros((PROJ_W - IN_WIDTH, w_in.shape[1]), w_in.dtype)
    w_in_p = jnp.concatenate([ga, gb, qa, cq, ka, va, ckv, kr, pad], axis=0)
    uq = w["mla_w_uq"].reshape(N_HEADS_B, QK_B, Q_LORA)
    uq = jnp.pad(uq, ((0, 0), (0, HEAD_PAD - QK_B), (0, 0))).reshape(WIDTH_BP, Q_LORA)
    ukv = w["mla_w_ukv"].reshape(N_HEADS_B, NOPE + V_B, KV_LORA)
    wk = jnp.pad(ukv[:, :NOPE], ((0, 0), (0, HEAD_PAD - NOPE), (0, 0))).reshape(WIDTH_BP, KV_LORA)
    wv = jnp.pad(ukv[:, NOPE:], ((0, 0), (0, HEAD_PAD - V_B), (0, 0))).reshape(WIDTH_BP, KV_LORA)
    wb = w["w_branch_b"].reshape(D_MODEL, N_HEADS_B, V_B)
    wb = jnp.pad(wb, ((0, 0), (0, 0), (0, HEAD_PAD - V_B))).reshape(D_MODEL, WIDTH_BP)
    out = dict(w)
    out.update(w_in=w_in_p, mla_w_uq=uq, wk=wk, wv=wv, w_branch_b=wb)
    return out


def _unlayer_grads(g):
    d = g["w_in"]
    ga, gb, qa, cq, ka, va, ckv, kr = (d[a:b] for a, b in (
        (P_GA, P_GA + D_MODEL), (P_GB, P_GB + D_MODEL), (P_QA, P_QA + WIDTH_A), (P_CQ, P_CQ + Q_LORA),
        (P_KA, P_KA + KV_A), (P_VA, P_VA + KV_A), (P_CKV, P_CKV + KV_LORA), (P_KR, P_KR + ROPE)))
    out = dict(g)
    out["w_in"] = jnp.concatenate([qa, ka, va, cq, ckv, kr, ga, gb], axis=0)
    out["mla_w_uq"] = g["mla_w_uq"].reshape(N_HEADS_B, HEAD_PAD, Q_LORA)[:, :QK_B].reshape(N_HEADS_B * QK_B, Q_LORA)
    dk = g.pop("wk").reshape(N_HEADS_B, HEAD_PAD, KV_LORA)[:, :NOPE]
    dv = g.pop("wv").reshape(N_HEADS_B, HEAD_PAD, KV_LORA)[:, :V_B]
    out["mla_w_ukv"] = jnp.concatenate([dk, dv], axis=1).reshape(N_HEADS_B * (NOPE + V_B), KV_LORA)
    out["w_branch_b"] = g["w_branch_b"].reshape(D_MODEL, N_HEADS_B, HEAD_PAD)[:, :, :V_B].reshape(D_MODEL, WIDTH_B)
    out.pop("wk", None)
    out.pop("wv", None)
    return out


def _pad_lanes(v, width):
    return jnp.pad(v.reshape(1, -1), ((0, 0), (0, width - v.shape[-1])))


def _rope_tables(positions):
    half = ROPE // 2
    inv_freq = ROPE_BASE ** (-jnp.arange(half, dtype=F32) / half)
    ang = positions.astype(F32).reshape(-1, 1) * inv_freq
    cos, sin = jnp.cos(ang), jnp.sin(ang)
    t = cos.shape[0]
    one, zero = jnp.ones((t, NOPE), F32), jnp.zeros((t, NOPE), F32)
    tail = jnp.zeros((t, HEAD_PAD - QK_B), F32)
    z16 = jnp.zeros((t, half), F32)
    c = jnp.concatenate([one, cos, cos, tail], axis=1)
    sm = jnp.concatenate([zero, -sin, z16, tail], axis=1)
    sp = jnp.concatenate([zero, z16, sin, tail], axis=1)
    return c, sm, sp


def _ffn_fwd(x, gain, wg_t, wu_t, wd, tag):
    n = _rms_fwd(x, gain, f"{tag}_norm")
    a, b, hmid = _ffn_up(n, wg_t, wu_t, f"{tag}_up")
    y = _mm([(hmid, wd)], "nn", F32, f"{tag}_down", residual=x, alpha=0.5)
    return y, (x, n, a, b, hmid)


def _ffn_bwd(dy, dyb, saved, gain, wg_t, wu_t, wd, tag):
    x, n, a, b, hmid = saved
    da, db = _ffn_down_bwd(dyb, wd, a, b, f"{tag}_down_bwd")
    g_wd = _mm([(hmid, dyb)], "tn", BF16, f"{tag}_dwd", alpha=0.5)
    g_wg = _mm([(da, n)], "tn", BF16, f"{tag}_dwg")
    g_wu = _mm([(db, n)], "tn", BF16, f"{tag}_dwu")
    dn = _mm([(da, wg_t), (db, wu_t)], "nn", F32, f"{tag}_dn")
    dx, dxb, g_gain = _rms_bwd(dn, x, gain, dy, f"{tag}_norm_bwd")
    return dx, dxb, g_wg, g_wu, g_wd, g_gain


def _local_step(x, positions, target, layers, smalls):
    n_batch, s_len, d = x.shape
    t = n_batch * s_len
    xt = x.reshape(t, d)
    tabs = _rope_tables(positions)
    pos_col = positions.reshape(t, 1)
    pos_row = positions.reshape(t // BLOCK, 1, BLOCK)
    saved = []
    for l in range(len(layers)):
        w, s = layers[l], smalls[l]
        g1, gm, g2 = (s[k].reshape(1, d) for k in ("ffn1_norm", "mix_norm", "ffn2_norm"))
        qg, kg = s["swa_q_norm"].reshape(1, -1), s["swa_k_norm"].reshape(1, -1)
        sinks = _pad_lanes(s["swa_sinks"], HEAD_PAD)
        gq, gkv = s["mla_q_lora_norm"].reshape(1, -1), s["mla_kv_lora_norm"].reshape(1, -1)
        gqn, gkn = _pad_lanes(s["mla_q_norm"], HEAD_PAD), _pad_lanes(s["mla_k_norm"], HEAD_PAD)
        x1, sv1 = _ffn_fwd(xt, g1, w["ffn1_w_gate"], w["ffn1_w_up"], w["ffn1_w_down"], f"l{l}_ffn1")
        h = _rms_fwd(x1, gm, f"l{l}_mix_norm")
        proj = _mm([(h, w["w_in"])], "nt", F32, f"l{l}_proj")
        oa = _swa_fwd(proj, pos_col, pos_row, qg, kg, sinks, n_batch, s_len, f"l{l}_swa")
        q, k, v = _mla_pre(proj, tabs, gq, gkv, gqn, gkn, w["mla_w_uq"], w["wk"], w["wv"], f"l{l}_mla_pre")
        ob, lse = _mla_flash_fwd(q, k, v, n_batch, s_len, f"l{l}_mla")
        merged, ya, yb = _merge_fwd(oa, ob, proj, w["w_branch_a"], w["w_branch_b"], f"l{l}_merge")
        x2 = _mm([(merged, w["w_out"])], "nn", F32, f"l{l}_out", residual=x1)
        x3, sv2 = _ffn_fwd(x2, g2, w["ffn2_w_gate"], w["ffn2_w_up"], w["ffn2_w_down"], f"l{l}_ffn2")
        saved.append((sv1, sv2, x1, h, proj, oa, q, k, v, ob, lse, merged, ya, yb,
                      (g1, gm, g2, qg, kg, sinks, gq, gkv, gqn, gkn)))
        xt = x3

    dy, dyb, loss = _loss_head(xt, target.reshape(t, d))

    big_grads, small_grads = [None] * len(layers), [None] * len(layers)
    for l in reversed(range(len(layers))):
        w = layers[l]
        sv1, sv2, x1, h, proj, oa, q, k, v, ob, lse, merged, ya, yb, gains = saved[l]
        g1, gm, g2, qg, kg, sinks, gq, gkv, gqn, gkn = gains
        bg, sg = {}, {}
        dy, dyb, bg["ffn2_w_gate"], bg["ffn2_w_up"], bg["ffn2_w_down"], sg["ffn2_norm"] = _ffn_bwd(
            dy, dyb, sv2, g2, w["ffn2_w_gate"], w["ffn2_w_up"], w["ffn2_w_down"], f"l{l}_ffn2")
        dya, dyb_, dga, dgb = _merge_bwd(dyb, w["w_out"], proj, ya, yb, f"l{l}_merge_bwd")
        bg["w_out"] = _mm([(merged, dyb)], "tn", BF16, f"l{l}_dwo")
        doa = _mm([(dya, w["w_branch_a"])], "nn", BF16, f"l{l}_doa")
        bg["w_branch_a"] = _mm([(dya, oa)], "tn", BF16, f"l{l}_dwa")
        dob = _mm([(dyb_, w["w_branch_b"])], "nn", BF16, f"l{l}_dob")
        bg["w_branch_b"] = _mm([(dyb_, ob)], "tn", BF16, f"l{l}_dwb")
        dqa, dkc, dkp, dvc, dvp, sg["swa_q_norm"], dsk = _swa_bwd(
            proj, pos_col, pos_row, qg, kg, sinks, doa, n_batch, s_len, f"l{l}_swa_bwd")
        sg["swa_sinks"] = dsk[:, :N_HEADS_A]
        dka, dva, sg["swa_k_norm"] = _swa_kv_bwd(proj, kg, dkc, dkp, dvc, dvp, n_batch, s_len, f"l{l}_swa_kv_bwd")
        dq, dk, dv = _mla_flash_bwd(q, k, v, dob, lse, n_batch, s_len, f"l{l}_mla_bwd")
        (dcq, dckv, dkr, g_uq, g_wk, g_wv, sg["mla_q_lora_norm"], sg["mla_kv_lora_norm"], dgqn, dgkn) = _mla_pre_bwd(
            proj, tabs, gq, gkv, gqn, gkn, w["mla_w_uq"], w["wk"], w["wv"], dq, dk, dv, f"l{l}_mla_pre_bwd")
        sg["mla_q_norm"], sg["mla_k_norm"] = dgqn[:, :QK_B], dgkn[:, :QK_B]
        bg["mla_w_uq"], bg["wk"], bg["wv"] = g_uq.astype(BF16), g_wk.astype(BF16), g_wv.astype(BF16)
        dproj = jnp.concatenate([dga, dgb, dqa, dcq, dka, dva, dckv, dkr], axis=1)
        bg["w_in"] = _mm([(dproj, h)], "tn", BF16, f"l{l}_dwin")
        dh = _mm([(dproj, w["w_in"])], "nn", F32, f"l{l}_dh")
        dy, dyb, sg["mix_norm"] = _rms_bwd(dh, x1, gm, dy, f"l{l}_mix_norm_bwd")
        dy, dyb, bg["ffn1_w_gate"], bg["ffn1_w_up"], bg["ffn1_w_down"], sg["ffn1_norm"] = _ffn_bwd(
            dy, dyb, sv1, g1, w["ffn1_w_gate"], w["ffn1_w_up"], w["ffn1_w_down"], f"l{l}_ffn1")
        big_grads[l], small_grads[l] = bg, sg
    return loss, dy.reshape(n_batch, s_len, d), big_grads, small_grads


def _round_up(n, m):
    return (n + m - 1) // m * m


def _flat_rows(shape, transposed):
    rows, k = (shape[1], shape[0]) if transposed else shape
    return _round_up(rows * k // LANES, 16), rows, k


def _flat_layout(shard_shapes):
    table, off = [], 0
    for l in range(DEPTH):
        for name, tr in BIG:
            pr, rows, k = _flat_rows(shard_shapes[name], tr)
            table.append((l, name, tr, off, pr, rows, k))
            off += pr
    return table, _round_up(off, FLAT_ROW_TILE)


def _pack_flat(params, table, total):
    parts, off = [], 0
    for l, name, tr, o, pr, rows, k in table:
        w = params[name][l]
        w = (w.T if tr else w).reshape(rows * k // LANES, LANES)
        parts.append(jnp.pad(w, ((0, pr - w.shape[0]), (0, 0))))
        off = o + pr
    if total > off:
        parts.append(jnp.zeros((total - off, LANES), parts[0].dtype))
    return jnp.concatenate(parts, axis=0)


def _unpack_flat(flat, table):
    out = {}
    for l, name, tr, o, pr, rows, k in table:
        w = flat[o:o + rows * k // LANES].reshape(rows, k)
        out.setdefault(name, []).append(w.T if tr else w)
    return {n: jnp.stack(v) for n, v in out.items()}


def _gathered_mats(gathered, table):
    layers = [dict() for _ in range(DEPTH)]
    for l, name, tr, o, pr, rows, k in table:
        layers[l][name] = gathered[:, o:o + rows * k // LANES].reshape(N_DEV * rows, k)
    return layers


def _pack_grads(grads, table, total):
    parts, off = [], 0
    for l, name, tr, o, pr, rows, k in table:
        g = grads[l][name].reshape(N_DEV, rows * k // LANES, LANES)
        parts.append(jnp.pad(g, ((0, 0), (0, pr - g.shape[1]), (0, 0))))
        off = o + pr
    if total > off:
        parts.append(jnp.zeros((N_DEV, total - off, LANES), BF16))
    return jnp.concatenate(parts, axis=1)


def _pack_small(params):
    parts = [params[n][l].reshape(-1).astype(F32) for l in range(DEPTH) for n in SMALL]
    v = jnp.concatenate(parts)
    return jnp.pad(v, (0, SMALL_ROWS * LANES - v.shape[0])).reshape(SMALL_ROWS, LANES)


def _unpack_small(flat, shapes):
    v, out, off = flat.reshape(-1), {}, 0
    for l in range(DEPTH):
        for n in SMALL:
            sz = math.prod(shapes[n][1:])
            out.setdefault(n, []).append(v[off:off + sz].reshape(shapes[n][1:]))
            off += sz
    return {n: jnp.stack(p) for n, p in out.items()}


_ANY = pl.BlockSpec(memory_space=pl.ANY)
_MESH = pl.DeviceIdType.MESH


def _place():
    return lax.axis_index("x"), lax.axis_index("y"), lax.axis_index("c")


def _all_gather(x_shard, name, vmem=False):
    spec = pl.BlockSpec(memory_space=pltpu.VMEM) if vmem else _ANY

    def body(x_ref, out_ref, send_sems, recv_sems, local_sem):
        x, y, c = _place()
        me, sibling = (x, y, c), (x, y, 1 - c)
        chips = [(1 - x, y), (x, 1 - y), (1 - x, 1 - y)]

        def rows(px, py, pc):
            return out_ref.at[4 * px + 2 * py + pc]

        def copy(k, block, to, src=None):
            return pltpu.make_async_remote_copy(
                src_ref=rows(*block) if src is None else src, dst_ref=rows(*block),
                send_sem=send_sems.at[k], recv_sem=recv_sems.at[k], device_id=to, device_id_type=_MESH)

        mine = pltpu.make_async_copy(x_ref, rows(*me), local_sem)
        mine.start()
        first = [copy(0, me, sibling, src=x_ref)]
        first += [copy(1 + j, me, (*chip, c), src=x_ref) for j, chip in enumerate(chips)]
        for cp in first:
            cp.start()
        passed = [copy(4 + j, (*chip, c), sibling) for j, chip in enumerate(chips)]
        for j, chip in enumerate(chips):
            copy(1 + j, (*chip, c), me).wait_recv()
            passed[j].start()
        copy(0, sibling, me).wait_recv()
        for j, chip in enumerate(chips):
            copy(4 + j, (*chip, 1 - c), me).wait_recv()
        for cp in first + passed:
            cp.wait_send()
        mine.wait()

    return pl.pallas_call(
        body, name=name, out_shape=jax.ShapeDtypeStruct((N_DEV,) + x_shard.shape, x_shard.dtype),
        in_specs=[spec], out_specs=spec,
        scratch_shapes=[pltpu.SemaphoreType.DMA((7,)), pltpu.SemaphoreType.DMA((7,)), pltpu.SemaphoreType.DMA],
    )(x_shard)


def _exchange_cores(g4, name):
    n_chip, _, r, w = g4.shape

    def body(g_ref, out_ref, send_sems, recv_sems):
        x, y, c = _place()
        copies = [pltpu.make_async_remote_copy(
            src_ref=g_ref.at[q, 1 - c], dst_ref=out_ref.at[q], send_sem=send_sems.at[q], recv_sem=recv_sems.at[q],
            device_id=(x, y, 1 - c), device_id_type=_MESH) for q in range(n_chip)]
        for cp in copies:
            cp.start()
        for cp in copies:
            cp.wait()

    return pl.pallas_call(
        body, name=name, out_shape=jax.ShapeDtypeStruct((n_chip, r, w), g4.dtype), in_specs=[_ANY], out_specs=_ANY,
        scratch_shapes=[pltpu.SemaphoreType.DMA((n_chip,)), pltpu.SemaphoreType.DMA((n_chip,))],
    )(g4)


def _exchange_chips(s1, name):
    _, r, w = s1.shape

    def body(s_ref, out_ref, send_sems, recv_sems):
        x, y, c = _place()
        copies = []
        for k, (tx, ty) in enumerate([(1 - x, y), (x, 1 - y), (1 - x, 1 - y)]):
            copies.append(pltpu.make_async_remote_copy(
                src_ref=s_ref.at[2 * tx + ty], dst_ref=out_ref.at[k], send_sem=send_sems.at[k],
                recv_sem=recv_sems.at[k], device_id=(tx, ty, c), device_id_type=_MESH))
        for cp in copies:
            cp.start()
        for cp in copies:
            cp.wait()

    return pl.pallas_call(
        body, name=name, out_shape=jax.ShapeDtypeStruct((3, r, w), s1.dtype), in_specs=[_ANY], out_specs=_ANY,
        scratch_shapes=[pltpu.SemaphoreType.DMA((3,)), pltpu.SemaphoreType.DMA((3,))],
    )(s1)


def _chip_sum(g4, recv, core, name):
    n_chip, _, r, w = g4.shape
    tr = FLAT_ROW_TILE

    def body(c_ref, a_ref, b_ref, o_ref):
        o_ref[...] = (a_ref[...].astype(F32) + b_ref[...].astype(F32)).astype(o_ref.dtype)

    grid_spec = pltpu.PrefetchScalarGridSpec(
        num_scalar_prefetch=1, grid=(n_chip, r // tr),
        in_specs=[pl.BlockSpec((None, None, tr, w), lambda q, i, c: (q, c[0], i, 0)),
                  pl.BlockSpec((None, tr, w), lambda q, i, c: (q, i, 0))],
        out_specs=pl.BlockSpec((None, tr, w), lambda q, i, c: (q, i, 0)))
    return pl.pallas_call(
        body, name=name, grid_spec=grid_spec, out_shape=jax.ShapeDtypeStruct((n_chip, r, w), g4.dtype),
        compiler_params=_cp("parallel", "parallel"))(core, g4, recv)


def _adam(w, g, m, v):
    m = ADAM_B1 * m + (1.0 - ADAM_B1) * g
    v = ADAM_B2 * v + (1.0 - ADAM_B2) * (g * g)
    m_hat = m / (1.0 - ADAM_B1 ** ADAM_STEP)
    v_hat = v / (1.0 - ADAM_B2 ** ADAM_STEP)
    delta = -ADAM_LR * (m_hat / (jnp.sqrt(v_hat) + ADAM_EPS) + ADAM_WD * w)
    return delta, m, v


def _adam_big(s1, r2, chip, w, m, v, name):
    r, lanes = w.shape
    tr = FLAT_ROW_TILE

    def body(c_ref, s_ref, r0_ref, r1_ref, r2_ref, w_ref, m_ref, v_ref, g_out, d_out, m_out, v_out):
        g = ((s_ref[...].astype(F32) + r0_ref[...].astype(F32)) + r1_ref[...].astype(F32)) + r2_ref[...].astype(F32)
        d, mn, vn = _adam(w_ref[...], g, m_ref[...], v_ref[...])
        g_out[...] = g
        d_out[...] = d
        m_out[...] = mn
        v_out[...] = vn

    row = pl.BlockSpec((tr, lanes), lambda i, c: (i, 0))
    rel = lambda k: pl.BlockSpec((None, tr, lanes), lambda i, c: (k, i, 0))
    grid_spec = pltpu.PrefetchScalarGridSpec(
        num_scalar_prefetch=1, grid=(r // tr,),
        in_specs=[pl.BlockSpec((None, tr, lanes), lambda i, c: (c[0], i, 0)), rel(0), rel(1), rel(2), row, row, row],
        out_specs=[row] * 4)
    return pl.pallas_call(
        body, name=name, grid_spec=grid_spec, out_shape=[jax.ShapeDtypeStruct((r, lanes), F32)] * 4,
        compiler_params=_cp("parallel"))(chip, s1, r2, r2, r2, w, m, v)


def _adam_small(parts, w, m, v, name):
    rows, lanes = w.shape

    def body(p_ref, w_ref, m_ref, v_ref, g_out, d_out, m_out, v_out):
        g = p_ref[0:rows, :]
        for dev in range(1, N_DEV):
            g = g + p_ref[dev * rows:(dev + 1) * rows, :]
        d, mn, vn = _adam(w_ref[...], g, m_ref[...], v_ref[...])
        g_out[...] = g
        d_out[...] = d
        m_out[...] = mn
        v_out[...] = vn

    return pl.pallas_call(
        body, name=name, out_shape=[jax.ShapeDtypeStruct((rows, lanes), F32)] * 4)(parts, w, m, v)


def kernel(x, positions, ffn1_norm, ffn1_w_gate, ffn1_w_up, ffn1_w_down, mix_norm, w_in, swa_q_norm, swa_k_norm, swa_sinks, mla_q_lora_norm, mla_w_uq, mla_kv_lora_norm, mla_w_ukv, mla_q_norm, mla_k_norm, w_branch_a, w_branch_b, w_out, ffn2_norm, ffn2_w_gate, ffn2_w_up, ffn2_w_down, loss_target, m_ffn1_norm, m_ffn1_w_gate, m_ffn1_w_up, m_ffn1_w_down, m_mix_norm, m_w_in, m_swa_q_norm, m_swa_k_norm, m_swa_sinks, m_mla_q_lora_norm, m_mla_w_uq, m_mla_kv_lora_norm, m_mla_w_ukv, m_mla_q_norm, m_mla_k_norm, m_w_branch_a, m_w_branch_b, m_w_out, m_ffn2_norm, m_ffn2_w_gate, m_ffn2_w_up, m_ffn2_w_down, v_ffn1_norm, v_ffn1_w_gate, v_ffn1_w_up, v_ffn1_w_down, v_mix_norm, v_w_in, v_swa_q_norm, v_swa_k_norm, v_swa_sinks, v_mla_q_lora_norm, v_mla_w_uq, v_mla_kv_lora_norm, v_mla_w_ukv, v_mla_q_norm, v_mla_k_norm, v_w_branch_a, v_w_branch_b, v_w_out, v_ffn2_norm, v_ffn2_w_gate, v_ffn2_w_up, v_ffn2_w_down):
    given = dict(locals())
    params = {n: given[n] for n in WEIGHTS}
    mom1 = {n: given["m_" + n] for n in WEIGHTS}
    mom2 = {n: given["v_" + n] for n in WEIGHTS}
    table, total = _flat_layout({n: params[n].shape[1:] for n, _ in BIG})

    w_flat = _pack_flat(params, table, total)
    gathered = _all_gather(w_flat.astype(BF16), "gather_weights")
    layers = [_layer_mats(w) for w in _gathered_mats(gathered, table)]
    smalls = [{n: params[n][l] for n in SMALL} for l in range(DEPTH)]

    loss, grad_x, big_grads, small_grads = _local_step(x, positions, loss_target, layers, smalls)
    loss = lax.psum(loss[0, 0], ("x", "y", "c"))

    cx, cy, cc = _place()
    core = jnp.reshape(cc, (1,)).astype(jnp.int32)
    chip = jnp.reshape(2 * cx + cy, (1,)).astype(jnp.int32)
    g_flat = _pack_grads([_unlayer_grads(g) for g in big_grads], table, total)
    g4 = g_flat.reshape(N_DEV // 2, 2, total, LANES)
    from_core = _exchange_cores(g4, "scatter_cores")
    s1 = _chip_sum(g4, from_core, core, "sum_cores")
    from_chips = _exchange_chips(s1, "scatter_chips")
    g_big, d_big, m_big, v_big = _adam_big(s1, from_chips, chip, w_flat, _pack_flat(mom1, table, total),
                                           _pack_flat(mom2, table, total), "adam_big")

    g_small = _pack_small({n: [small_grads[l][n] for l in range(DEPTH)] for n in SMALL})
    parts = _all_gather(g_small, "gather_small", vmem=True).reshape(N_DEV * SMALL_ROWS, LANES)
    small_out = _adam_small(parts, _pack_small(params), _pack_small(mom1), _pack_small(mom2), "adam_small")

    outs = []
    shapes = {n: params[n].shape for n in SMALL}
    for big, small in zip((g_big, d_big, m_big, v_big), small_out):
        tree = _unpack_flat(big, table)
        tree.update(_unpack_small(small, shapes))
        outs.append(tree)
    return (loss, grad_x, *[o[n] for o in outs for n in WEIGHTS])
```

```python
import math

import jax
import jax.numpy as jnp
from jax import lax
from jax.experimental import pallas as pl
from jax.experimental.pallas import tpu as pltpu

F32 = jnp.float32
BF16 = jnp.bfloat16

N_DEV = 8
DEPTH = 2
D_MODEL = 1024
D_FF = 2816
HEAD_DIM_A = 64
N_HEADS_A = 8
N_KV_HEADS_A = 2
GROUP_A = N_HEADS_A // N_KV_HEADS_A
BLOCK = 128
N_HEADS_B = 8
Q_LORA = 256
KV_LORA = 128
NOPE = 64
ROPE = 32
QK_B = NOPE + ROPE
V_B = 64
HEAD_PAD = 128
WIDTH_A = N_HEADS_A * HEAD_DIM_A
WIDTH_B = N_HEADS_B * V_B
WIDTH_BP = N_HEADS_B * HEAD_PAD
KV_A = N_KV_HEADS_A * HEAD_DIM_A
IN_WIDTH = WIDTH_A + 2 * KV_A + Q_LORA + KV_LORA + ROPE + 2 * D_MODEL
ROPE_BASE = 10000.0
EPS = 1e-6
NEG = -1e30
LOG2E = 1.4426950408889634

P_GA, P_GB, P_QA, P_CQ, P_KA, P_VA, P_CKV, P_KR = 0, 1024, 2048, 2560, 2816, 2944, 3072, 3200
PROJ_W = 3328
SWA_HEAD_ORDER = (0, 4, 1, 5, 2, 6, 3, 7)
SWA_HEAD_INVERSE = tuple(SWA_HEAD_ORDER.index(h) for h in range(N_HEADS_A))

ADAM_LR, ADAM_B1, ADAM_B2, ADAM_EPS, ADAM_WD, ADAM_STEP = 0.001, 0.9, 0.999, 1e-08, 0.01, 10

VMEM_LIMIT = 56 * 1024 * 1024
LANES = 1024

BIG = (("ffn1_w_gate", True), ("ffn1_w_up", True), ("ffn1_w_down", False), ("w_in", True), ("mla_w_uq", True),
       ("mla_w_ukv", True), ("w_branch_a", True), ("w_branch_b", True), ("w_out", False),
       ("ffn2_w_gate", True), ("ffn2_w_up", True), ("ffn2_w_down", False))
SMALL = ("ffn1_norm", "mix_norm", "ffn2_norm", "swa_q_norm", "swa_k_norm", "swa_sinks", "mla_q_lora_norm",
         "mla_kv_lora_norm", "mla_q_norm", "mla_k_norm")
WEIGHTS = ("ffn1_norm", "ffn1_w_gate", "ffn1_w_up", "ffn1_w_down", "mix_norm", "w_in", "swa_q_norm", "swa_k_norm",
           "swa_sinks", "mla_q_lora_norm", "mla_w_uq", "mla_kv_lora_norm", "mla_w_ukv", "mla_q_norm", "mla_k_norm",
           "w_branch_a", "w_branch_b", "w_out", "ffn2_norm", "ffn2_w_gate", "ffn2_w_up", "ffn2_w_down")
FLAT_ROW_TILE = 384
SMALL_ROWS = 8


def _cp(*sem):
    return pltpu.CompilerParams(dimension_semantics=sem, vmem_limit_bytes=VMEM_LIMIT)


def _tile(n, prefs):
    for t in prefs:
        if n % t == 0:
            return t
    return n


def _dot(a, b, dims):
    return lax.dot_general(a, b, (dims, ((), ())), preferred_element_type=F32)


_NT = ((1,), (1,))
_NN = ((1,), (0,))
_TN = ((0,), (0,))


def _sigmoid(x):
    return 0.5 * jnp.tanh(0.5 * x) + 0.5


def _chunks(n, width):
    return [(c, min(width, n - c)) for c in range(0, n, width)]


def _mm(pairs, mode, out_dtype, name, residual=None, alpha=1.0):
    for a, b in pairs:
        assert a.dtype == BF16 and b.dtype == BF16, (name, a.dtype, b.dtype)
    if mode == "tn":
        (a, b), = pairs
        return _mm_tokens(a, b, out_dtype, name, alpha)
    t = pairs[0][0].shape[0]
    n = pairs[0][1].shape[0] if mode == "nt" else pairs[0][1].shape[1]
    tm = _tile(t, (512, 256, 128))
    dims = _NT if mode == "nt" else _NN
    in_specs, args = [], []
    for a, w in pairs:
        in_specs.append(pl.BlockSpec((tm, a.shape[1]), lambda i: (i, 0)))
        in_specs.append(pl.BlockSpec(w.shape, lambda i: (0, 0)))
        args += [a, w]
    if residual is not None:
        in_specs.append(pl.BlockSpec((tm, n), lambda i: (i, 0)))
        args.append(residual)
    n_pairs = len(pairs)

    def body(*refs):
        o_ref = refs[-1]
        for c0, cw in _chunks(n, 512):
            acc = None
            for p in range(n_pairs):
                w_ref = refs[2 * p + 1]
                w = w_ref[c0:c0 + cw, :] if mode == "nt" else w_ref[:, c0:c0 + cw]
                d = _dot(refs[2 * p][...], w, dims)
                acc = d if acc is None else acc + d
            if alpha != 1.0:
                acc = acc * alpha
            if residual is not None:
                acc = refs[2 * n_pairs][:, c0:c0 + cw] + acc
            o_ref[:, c0:c0 + cw] = acc.astype(out_dtype)

    return pl.pallas_call(
        body, name=name, grid=(t // tm,), in_specs=in_specs, out_specs=pl.BlockSpec((tm, n), lambda i: (i, 0)),
        out_shape=jax.ShapeDtypeStruct((t, n), out_dtype), compiler_params=_cp("parallel"))(*args)


def _mm_tokens(a, b, out_dtype, name, alpha):
    t, m = a.shape
    n = b.shape[1]
    tk = _tile(t, (512, 256, 128))

    def body(a_ref, b_ref, o_ref, acc_ref):
        k = pl.program_id(0)

        @pl.when(k == 0)
        def _():
            acc_ref[...] = jnp.zeros_like(acc_ref)

        for c0, cw in _chunks(m, 512):
            acc_ref[c0:c0 + cw, :] += _dot(a_ref[:, c0:c0 + cw], b_ref[...], _TN)

        @pl.when(k == pl.num_programs(0) - 1)
        def _():
            o_ref[...] = (acc_ref[...] * alpha).astype(out_dtype)

    return pl.pallas_call(
        body, name=name, grid=(t // tk,),
        in_specs=[pl.BlockSpec((tk, m), lambda k: (k, 0)), pl.BlockSpec((tk, n), lambda k: (k, 0))],
        out_specs=pl.BlockSpec((m, n), lambda k: (0, 0)), out_shape=jax.ShapeDtypeStruct((m, n), out_dtype),
        scratch_shapes=[pltpu.VMEM((m, n), F32)], compiler_params=_cp("arbitrary"))(a, b)


def _rms_fwd(x, g, name):
    t, d = x.shape
    tm = _tile(t, (512, 256, 128))

    def body(x_ref, g_ref, o_ref):
        xv = x_ref[...]
        r = lax.rsqrt(jnp.mean(xv * xv, axis=1, keepdims=True) + EPS)
        o_ref[...] = (xv * r * g_ref[...]).astype(BF16)

    return pl.pallas_call(
        body, name=name, grid=(t // tm,),
        in_specs=[pl.BlockSpec((tm, d), lambda i: (i, 0)), pl.BlockSpec((1, d), lambda i: (0, 0))],
        out_specs=pl.BlockSpec((tm, d), lambda i: (i, 0)),
        out_shape=jax.ShapeDtypeStruct((t, d), BF16), compiler_params=_cp("parallel"))(x, g)


def _rms_bwd(dn, x, g, res, name):
    t, d = x.shape
    tm = _tile(t, (512, 256, 128))

    def body(dn_ref, x_ref, g_ref, res_ref, dx_ref, dxb_ref, dg_ref):
        xv = x_ref[...]
        r = lax.rsqrt(jnp.mean(xv * xv, axis=1, keepdims=True) + EPS)
        xh = xv * r
        dnv = dn_ref[...]
        dxh = dnv * g_ref[...]
        dx = res_ref[...] + r * (dxh - xh * jnp.mean(dxh * xh, axis=1, keepdims=True))
        dx_ref[...] = dx
        dxb_ref[...] = dx.astype(BF16)

        @pl.when(pl.program_id(0) == 0)
        def _():
            dg_ref[...] = jnp.zeros_like(dg_ref)

        dg_ref[...] += jnp.sum(dnv * xh, axis=0, keepdims=True)

    row = pl.BlockSpec((tm, d), lambda i: (i, 0))
    one = pl.BlockSpec((1, d), lambda i: (0, 0))
    return pl.pallas_call(
        body, name=name, grid=(t // tm,), in_specs=[row, row, one, row], out_specs=[row, row, one],
        out_shape=[jax.ShapeDtypeStruct((t, d), F32), jax.ShapeDtypeStruct((t, d), BF16),
                   jax.ShapeDtypeStruct((1, d), F32)], compiler_params=_cp("arbitrary"))(dn, x, g, res)


def _ffn_up(n, wg_t, wu_t, name):
    t, d = n.shape
    f = wg_t.shape[0]
    tm = _tile(t, (512, 256, 128))

    def body(n_ref, wg_ref, wu_ref, a_ref, b_ref, h_ref):
        nv = n_ref[...]
        for c0, cw in _chunks(f, 256):
            a = _dot(nv, wg_ref[c0:c0 + cw, :], _NT)
            b = _dot(nv, wu_ref[c0:c0 + cw, :], _NT)
            a_ref[:, c0:c0 + cw] = a.astype(BF16)
            b_ref[:, c0:c0 + cw] = b.astype(BF16)
            h_ref[:, c0:c0 + cw] = (a * _sigmoid(a) * b).astype(BF16)

    w_spec = pl.BlockSpec((f, d), lambda i: (0, 0))
    o_spec = pl.BlockSpec((tm, f), lambda i: (i, 0))
    o_shape = jax.ShapeDtypeStruct((t, f), BF16)
    return pl.pallas_call(
        body, name=name, grid=(t // tm,), in_specs=[pl.BlockSpec((tm, d), lambda i: (i, 0)), w_spec, w_spec],
        out_specs=[o_spec] * 3, out_shape=[o_shape] * 3, compiler_params=_cp("parallel"))(n, wg_t, wu_t)


def _ffn_down_bwd(dxb, wd, a, b, name):
    t, d = dxb.shape
    f = wd.shape[0]
    tm = _tile(t, (512, 256, 128))

    def body(dx_ref, wd_ref, a_ref, b_ref, da_ref, db_ref):
        dxv = dx_ref[...]
        for c0, cw in _chunks(f, 256):
            dh = 0.5 * _dot(dxv, wd_ref[c0:c0 + cw, :], _NT)
            av = a_ref[:, c0:c0 + cw].astype(F32)
            bv = b_ref[:, c0:c0 + cw].astype(F32)
            sg = _sigmoid(av)
            da_ref[:, c0:c0 + cw] = (dh * bv * (sg * (1.0 + av * (1.0 - sg)))).astype(BF16)
            db_ref[:, c0:c0 + cw] = (dh * (av * sg)).astype(BF16)

    o_spec = pl.BlockSpec((tm, f), lambda i: (i, 0))
    o_shape = jax.ShapeDtypeStruct((t, f), BF16)
    return pl.pallas_call(
        body, name=name, grid=(t // tm,),
        in_specs=[pl.BlockSpec((tm, d), lambda i: (i, 0)), pl.BlockSpec((f, d), lambda i: (0, 0)), o_spec, o_spec],
        out_specs=[o_spec] * 2, out_shape=[o_shape] * 2, compiler_params=_cp("parallel"))(dxb, wd, a, b)


def _loss_head(y, target):
    t, d = y.shape
    tm = _tile(t, (512, 256, 128))

    def body(y_ref, t_ref, dy_ref, dyb_ref, loss_ref, acc_ref):
        i = pl.program_id(0)
        e = y_ref[...] - t_ref[...]
        dy = e * (1.0 / d)
        dy_ref[...] = dy
        dyb_ref[...] = dy.astype(BF16)

        @pl.when(i == 0)
        def _():
            acc_ref[...] = jnp.zeros_like(acc_ref)

        acc_ref[...] += jnp.sum(e * e, axis=0, keepdims=True)

        @pl.when(i == pl.num_programs(0) - 1)
        def _():
            loss_ref[...] = jnp.sum(acc_ref[...], axis=1, keepdims=True) * (0.5 / d)

    row = pl.BlockSpec((tm, d), lambda i: (i, 0))
    return pl.pallas_call(
        body, name="loss_head", grid=(t // tm,), in_specs=[row, row],
        out_specs=[row, row, pl.BlockSpec((1, 1), lambda i: (0, 0))],
        out_shape=[jax.ShapeDtypeStruct((t, d), F32), jax.ShapeDtypeStruct((t, d), BF16),
                   jax.ShapeDtypeStruct((1, 1), F32)],
        scratch_shapes=[pltpu.VMEM((1, d), F32)], compiler_params=_cp("arbitrary"))(y, target)


def _merge_fwd(oa, ob, proj, wa_t, wb_t, name):
    t = oa.shape[0]
    d = wa_t.shape[0]
    tm = _tile(t, (512, 256, 128))

    def body(oa_ref, ob_ref, ga_ref, gb_ref, wa_ref, wb_ref, mg_ref, ya_ref, yb_ref):
        oav, obv = oa_ref[...], ob_ref[...]
        for c0, cw in _chunks(d, 512):
            cs = slice(c0, c0 + cw)
            ya = _dot(oav, wa_ref[cs, :], _NT)
            yb = _dot(obv, wb_ref[cs, :], _NT)
            mg_ref[:, cs] = (_sigmoid(ga_ref[:, cs]) * ya + _sigmoid(gb_ref[:, cs]) * yb).astype(BF16)
            ya_ref[:, cs] = ya.astype(BF16)
            yb_ref[:, cs] = yb.astype(BF16)

    o_spec = pl.BlockSpec((tm, d), lambda i: (i, 0))
    o_shape = jax.ShapeDtypeStruct((t, d), BF16)
    return pl.pallas_call(
        body, name=name, grid=(t // tm,),
        in_specs=[pl.BlockSpec((tm, oa.shape[1]), lambda i: (i, 0)), pl.BlockSpec((tm, ob.shape[1]), lambda i: (i, 0)),
                  pl.BlockSpec((tm, d), lambda i: (i, P_GA // d)), pl.BlockSpec((tm, d), lambda i: (i, P_GB // d)),
                  pl.BlockSpec(wa_t.shape, lambda i: (0, 0)), pl.BlockSpec(wb_t.shape, lambda i: (0, 0))],
        out_specs=[o_spec] * 3, out_shape=[o_shape] * 3,
        compiler_params=_cp("parallel"))(oa, ob, proj, proj, wa_t, wb_t)


def _merge_bwd(dxb, wo, proj, ya, yb, name):
    t, d = dxb.shape
    tm = _tile(t, (512, 256, 128))

    def body(dx_ref, wo_ref, ga_ref, gb_ref, ya_ref, yb_ref, dya_ref, dyb_ref, dga_ref, dgb_ref):
        dxv = dx_ref[...]
        for c0, cw in _chunks(d, 512):
            cs = slice(c0, c0 + cw)
            dm = _dot(dxv, wo_ref[cs, :], _NT)
            sa = _sigmoid(ga_ref[:, cs])
            sb = _sigmoid(gb_ref[:, cs])
            dya_ref[:, cs] = (dm * sa).astype(BF16)
            dyb_ref[:, cs] = (dm * sb).astype(BF16)
            dga_ref[:, cs] = (dm * ya_ref[:, cs].astype(F32) * (sa * (1.0 - sa))).astype(BF16)
            dgb_ref[:, cs] = (dm * yb_ref[:, cs].astype(F32) * (sb * (1.0 - sb))).astype(BF16)

    o_spec = pl.BlockSpec((tm, d), lambda i: (i, 0))
    o_shape = jax.ShapeDtypeStruct((t, d), BF16)
    return pl.pallas_call(
        body, name=name, grid=(t // tm,),
        in_specs=[o_spec, pl.BlockSpec((d, d), lambda i: (0, 0)),
                  pl.BlockSpec((tm, d), lambda i: (i, P_GA // d)), pl.BlockSpec((tm, d), lambda i: (i, P_GB // d)),
                  o_spec, o_spec],
        out_specs=[o_spec] * 4, out_shape=[o_shape] * 4,
        compiler_params=_cp("parallel"))(dxb, wo, proj, proj, ya, yb)


def _swa_common(i, pq_ref, pkp_ref, pkc_ref):
    pk = jnp.concatenate([pkp_ref[0], pkc_ref[0]], axis=1)
    dist = (pq_ref[...] - pk).astype(F32)
    row = lax.broadcasted_iota(jnp.int32, (BLOCK, 2 * BLOCK), 0)
    col = lax.broadcasted_iota(jnp.int32, (BLOCK, 2 * BLOCK), 1)
    diff = row + BLOCK - col
    valid = (diff >= 0) & (diff < BLOCK) & ((i > 0) | (col >= BLOCK))
    return jnp.concatenate([dist] * GROUP_A, axis=0), jnp.concatenate([valid] * GROUP_A, axis=0)


def _half_sum(x, lo):
    s_lo = jnp.sum(jnp.where(lo, x, 0.0), axis=1, keepdims=True)
    s_hi = jnp.sum(jnp.where(lo, 0.0, x), axis=1, keepdims=True)
    return jnp.where(lo, s_lo, s_hi)


def _norm2(x, gain2, lo):
    r = lax.rsqrt(_half_sum(x * x, lo) * (1.0 / HEAD_DIM_A) + EPS)
    xh = x * r
    return xh * gain2, xh, r


def _norm2_bwd(d, xh, r, gain2, lo):
    dxh = d * gain2
    return r * (dxh - xh * (_half_sum(dxh * xh, lo) * (1.0 / HEAD_DIM_A)))


def _swa_group(grp, qn, lo, sk_ref):
    mask = lo if grp == 0 else jnp.logical_not(lo)
    q4 = jnp.concatenate([jnp.where(mask, qn[j], 0.0) for j in range(GROUP_A)], axis=0).astype(BF16)
    heads = [grp * GROUP_A + j for j in range(GROUP_A)]
    slope = jnp.concatenate([jnp.full((BLOCK, 1), 2.0 ** (-(h + 1)), F32) for h in heads], axis=0)
    sink = jnp.concatenate([jnp.broadcast_to(sk_ref[:, h:h + 1], (BLOCK, 1)) for h in heads], axis=0)
    return mask, q4, slope, sink


def _swa_probs(q4, kk, dist4, valid4, slope, sink):
    s = _dot(q4, kk, _NT) * (HEAD_DIM_A ** -0.5) - slope * dist4
    s = jnp.where(valid4, s, NEG)
    m = jnp.maximum(jnp.max(s, axis=1, keepdims=True), sink)
    e = jnp.exp(s - m)
    es = jnp.exp(sink - m)
    inv = 1.0 / (jnp.sum(e, axis=1, keepdims=True) + es)
    return e * inv, es * inv


def _swa_specs(s_len):
    nb = s_len // BLOCK

    def rowblk(b, i):
        return b * nb + i

    def prevblk(b, i):
        return b * nb + jnp.maximum(i - 1, 0)

    q_spec = pl.BlockSpec((BLOCK, WIDTH_A), lambda b, i: (rowblk(b, i), P_QA // WIDTH_A))
    kc_spec = pl.BlockSpec((BLOCK, KV_A), lambda b, i: (rowblk(b, i), P_KA // KV_A))
    kp_spec = pl.BlockSpec((BLOCK, KV_A), lambda b, i: (prevblk(b, i), P_KA // KV_A))
    vc_spec = pl.BlockSpec((BLOCK, KV_A), lambda b, i: (rowblk(b, i), P_VA // KV_A))
    vp_spec = pl.BlockSpec((BLOCK, KV_A), lambda b, i: (prevblk(b, i), P_VA // KV_A))
    pq_spec = pl.BlockSpec((BLOCK, 1), lambda b, i: (rowblk(b, i), 0))
    pkc_spec = pl.BlockSpec((1, 1, BLOCK), lambda b, i: (rowblk(b, i), 0, 0))
    pkp_spec = pl.BlockSpec((1, 1, BLOCK), lambda b, i: (prevblk(b, i), 0, 0))
    return nb, rowblk, [q_spec, kc_spec, kp_spec, vc_spec, vp_spec, pq_spec, pkc_spec, pkp_spec]


def _swa_fwd(proj, pos_col, pos_row, qg2, kg2, sinks, n_batch, s_len, name):
    t = proj.shape[0]
    nb, rowblk, specs = _swa_specs(s_len)
    small = pl.BlockSpec((1, HEAD_PAD), lambda b, i: (0, 0))

    def body(q_ref, kc_ref, kp_ref, vc_ref, vp_ref, pq_ref, pkc_ref, pkp_ref, qg_ref, kg_ref, sk_ref, o_ref):
        dist4, valid4 = _swa_common(pl.program_id(1), pq_ref, pkp_ref, pkc_ref)
        lo = lax.broadcasted_iota(jnp.int32, (1, HEAD_PAD), 1) < HEAD_DIM_A
        kk = _norm2(jnp.concatenate([kp_ref[...], kc_ref[...]], axis=0), kg_ref[...], lo)[0].astype(BF16)
        vv = jnp.concatenate([vp_ref[...], vc_ref[...]], axis=0).astype(BF16)
        qn = [_norm2(q_ref[:, j * HEAD_PAD:(j + 1) * HEAD_PAD], qg_ref[...], lo)[0] for j in range(GROUP_A)]
        outs = []
        for grp in range(N_KV_HEADS_A):
            _, q4, slope, sink = _swa_group(grp, qn, lo, sk_ref)
            p, _ = _swa_probs(q4, kk, dist4, valid4, slope, sink)
            outs.append(_dot(p.astype(BF16), vv, _NN))
        for j in range(GROUP_A):
            rs = slice(j * BLOCK, (j + 1) * BLOCK)
            o_ref[:, j * HEAD_PAD:(j + 1) * HEAD_PAD] = jnp.where(lo, outs[0][rs], outs[1][rs]).astype(BF16)

    return pl.pallas_call(
        body, name=name, grid=(n_batch, nb), in_specs=specs + [small, small, small],
        out_specs=pl.BlockSpec((BLOCK, WIDTH_A), lambda b, i: (rowblk(b, i), 0)),
        out_shape=jax.ShapeDtypeStruct((t, WIDTH_A), BF16),
        compiler_params=_cp("parallel", "parallel"))(proj, proj, proj, proj, proj, pos_col, pos_row, pos_row,
                                                     qg2, kg2, sinks)


def _swa_bwd(proj, pos_col, pos_row, qg2, kg2, sinks, do, n_batch, s_len, name):
    t = proj.shape[0]
    nb, rowblk, specs = _swa_specs(s_len)
    small = pl.BlockSpec((1, HEAD_PAD), lambda b, i: (0, 0))
    scale = HEAD_DIM_A ** -0.5

    def body(q_ref, kc_ref, kp_ref, vc_ref, vp_ref, pq_ref, pkc_ref, pkp_ref, qg_ref, kg_ref, sk_ref, do_ref,
             dq_ref, dkc_ref, dkp_ref, dvc_ref, dvp_ref, dqg_ref, dsk_ref):
        b, i = pl.program_id(0), pl.program_id(1)

        @pl.when((b == 0) & (i == 0))
        def _():
            dqg_ref[...] = jnp.zeros_like(dqg_ref)
            dsk_ref[...] = jnp.zeros_like(dsk_ref)

        dist4, valid4 = _swa_common(i, pq_ref, pkp_ref, pkc_ref)
        lane = lax.broadcasted_iota(jnp.int32, (1, HEAD_PAD), 1)
        lo = lane < HEAD_DIM_A
        kk = _norm2(jnp.concatenate([kp_ref[...], kc_ref[...]], axis=0), kg_ref[...], lo)[0].astype(BF16)
        vv = jnp.concatenate([vp_ref[...], vc_ref[...]], axis=0).astype(BF16)
        qs = [_norm2(q_ref[:, j * HEAD_PAD:(j + 1) * HEAD_PAD], qg_ref[...], lo) for j in range(GROUP_A)]
        dos = [do_ref[:, j * HEAD_PAD:(j + 1) * HEAD_PAD] for j in range(GROUP_A)]
        dkk = jnp.zeros((2 * BLOCK, HEAD_PAD), F32)
        dvv = jnp.zeros((2 * BLOCK, HEAD_PAD), F32)
        dsk = jnp.zeros((1, HEAD_PAD), F32)
        dq4 = []
        for grp in range(N_KV_HEADS_A):
            mask, q4, slope, sink = _swa_group(grp, [q[0] for q in qs], lo, sk_ref)
            do4 = jnp.concatenate([jnp.where(mask, d, jnp.zeros_like(d)) for d in dos], axis=0)
            p, ps = _swa_probs(q4, kk, dist4, valid4, slope, sink)
            dp = _dot(do4, vv, _NT)
            delta = jnp.sum(p * dp, axis=1, keepdims=True)
            ds = (p * (dp - delta) * scale).astype(BF16)
            dsink = ps * delta
            for j in range(GROUP_A):
                dsk = dsk + jnp.where(lane == grp * GROUP_A + j, -jnp.sum(dsink[j * BLOCK:(j + 1) * BLOCK]), 0.0)
            dvv = dvv + _dot(p.astype(BF16), do4, _TN)
            dkk = dkk + _dot(ds, q4, _TN)
            dq4.append(_dot(ds, kk, _NN))
        dqg = jnp.zeros((1, HEAD_PAD), F32)
        for j in range(GROUP_A):
            rs = slice(j * BLOCK, (j + 1) * BLOCK)
            _, xh, r = qs[j]
            dqn = jnp.where(lo, dq4[0][rs], dq4[1][rs])
            dqg = dqg + jnp.sum(dqn * xh, axis=0, keepdims=True)
            dq_ref[:, j * HEAD_PAD:(j + 1) * HEAD_PAD] = _norm2_bwd(dqn, xh, r, qg_ref[...], lo).astype(BF16)
        dkp_ref[...] = dkk[:BLOCK]
        dkc_ref[...] = dkk[BLOCK:]
        dvp_ref[...] = dvv[:BLOCK]
        dvc_ref[...] = dvv[BLOCK:]
        dqg_ref[...] += dqg
        dsk_ref[...] += dsk

    kv_out = pl.BlockSpec((BLOCK, KV_A), lambda b, i: (rowblk(b, i), 0))
    kv_shape = jax.ShapeDtypeStruct((t, KV_A), F32)
    wide = pl.BlockSpec((BLOCK, WIDTH_A), lambda b, i: (rowblk(b, i), 0))
    return pl.pallas_call(
        body, name=name, grid=(n_batch, nb), in_specs=specs + [small, small, small, wide],
        out_specs=[wide, kv_out, kv_out, kv_out, kv_out, small, small],
        out_shape=[jax.ShapeDtypeStruct((t, WIDTH_A), BF16), kv_shape, kv_shape, kv_shape, kv_shape,
                   jax.ShapeDtypeStruct((1, HEAD_PAD), F32), jax.ShapeDtypeStruct((1, HEAD_PAD), F32)],
        compiler_params=_cp("arbitrary", "arbitrary"))(proj, proj, proj, proj, proj, pos_col, pos_row, pos_row,
                                                       qg2, kg2, sinks, do)


def _swa_kv_bwd(proj, kg2, dkc, dkp, dvc, dvp, n_batch, s_len, name):
    t = proj.shape[0]
    nb = s_len // BLOCK

    def rowblk(b, i):
        return b * nb + i

    def nextblk(b, i):
        return b * nb + jnp.minimum(i + 1, nb - 1)

    def body(k_ref, kg_ref, dkc_ref, dkp_ref, dvc_ref, dvp_ref, dk_ref, dv_ref, dkg_ref):
        b, i = pl.program_id(0), pl.program_id(1)

        @pl.when((b == 0) & (i == 0))
        def _():
            dkg_ref[...] = jnp.zeros_like(dkg_ref)

        lo = lax.broadcasted_iota(jnp.int32, (1, HEAD_PAD), 1) < HEAD_DIM_A
        has_next = (i < nb - 1).astype(F32)
        dkn = dkc_ref[...] + has_next * dkp_ref[...]
        dv_ref[...] = (dvc_ref[...] + has_next * dvp_ref[...]).astype(BF16)
        _, xh, r = _norm2(k_ref[...], kg_ref[...], lo)
        dkg_ref[...] += jnp.sum(dkn * xh, axis=0, keepdims=True)
        dk_ref[...] = _norm2_bwd(dkn, xh, r, kg_ref[...], lo).astype(BF16)

    cur = pl.BlockSpec((BLOCK, KV_A), lambda b, i: (rowblk(b, i), 0))
    nxt = pl.BlockSpec((BLOCK, KV_A), lambda b, i: (nextblk(b, i), 0))
    small = pl.BlockSpec((1, HEAD_PAD), lambda b, i: (0, 0))
    return pl.pallas_call(
        body, name=name, grid=(n_batch, nb),
        in_specs=[pl.BlockSpec((BLOCK, KV_A), lambda b, i: (rowblk(b, i), P_KA // KV_A)), small, cur, nxt, cur, nxt],
        out_specs=[cur, cur, small],
        out_shape=[jax.ShapeDtypeStruct((t, KV_A), BF16), jax.ShapeDtypeStruct((t, KV_A), BF16),
                   jax.ShapeDtypeStruct((1, HEAD_PAD), F32)],
        compiler_params=_cp("arbitrary", "arbitrary"))(proj, kg2, dkc, dkp, dvc, dvp)


def _rope(u, c, sm, sp):
    return u * c + pltpu.roll(u, HEAD_PAD - ROPE // 2, 1) * sm + pltpu.roll(u, ROPE // 2, 1) * sp


def _rope_t(d, c, sm, sp):
    return d * c + pltpu.roll(d * sm, ROPE // 2, 1) + pltpu.roll(d * sp, HEAD_PAD - ROPE // 2, 1)


def _pad_norm(x, gain):
    r = lax.rsqrt(jnp.sum(x * x, axis=1, keepdims=True) * (1.0 / QK_B) + EPS)
    xh = x * r
    return xh * gain, xh, r


def _pad_norm_bwd(d, xh, r, gain):
    dxh = d * gain
    return r * (dxh - xh * (jnp.sum(dxh * xh, axis=1, keepdims=True) * (1.0 / QK_B)))


def _lora_norm(x, gain):
    r = lax.rsqrt(jnp.mean(x * x, axis=1, keepdims=True) + EPS)
    xh = x * r
    return xh * gain, xh, r


def _mla_in_specs(tm):
    row = lambda w, off: pl.BlockSpec((tm, w), lambda i: (i, off // w))
    one = lambda w: pl.BlockSpec((1, w), lambda i: (0, 0))
    full = lambda r, c: pl.BlockSpec((r, c), lambda i: (0, 0))
    tab = pl.BlockSpec((tm, HEAD_PAD), lambda i: (i, 0))
    return [row(Q_LORA, P_CQ), row(KV_LORA, P_CKV), row(HEAD_PAD, P_KR), tab, tab, tab,
            one(Q_LORA), one(KV_LORA), one(HEAD_PAD), one(HEAD_PAD),
            full(WIDTH_BP, Q_LORA), full(WIDTH_BP, KV_LORA), full(WIDTH_BP, KV_LORA)]


def _mla_pre(proj, tabs, gq, gkv, gqn, gkn, wuq, wk, wv, name):
    t = proj.shape[0]
    tm = _tile(t, (256, 128))

    def body(cq_ref, ckv_ref, kr_ref, c_ref, sm_ref, sp_ref, gq_ref, gkv_ref, gqn_ref, gkn_ref,
             wuq_ref, wk_ref, wv_ref, q_ref, k_ref, v_ref):
        cqn = _lora_norm(cq_ref[...], gq_ref[...])[0].astype(BF16)
        ckvn = _lora_norm(ckv_ref[...], gkv_ref[...])[0].astype(BF16)
        q_raw = _dot(cqn, wuq_ref[...], _NT)
        k_raw = _dot(ckvn, wk_ref[...], _NT)
        v_ref[...] = _dot(ckvn, wv_ref[...], _NT).astype(BF16)
        kr = pltpu.roll(kr_ref[...], NOPE, 1)
        c, sm, sp = c_ref[...], sm_ref[...], sp_ref[...]
        for h in range(N_HEADS_B):
            hs = slice(h * HEAD_PAD, (h + 1) * HEAD_PAD)
            q_ref[:, hs] = _rope(_pad_norm(q_raw[:, hs], gqn_ref[...])[0], c, sm, sp).astype(BF16)
            k_ref[:, hs] = _rope(_pad_norm(k_raw[:, hs] + kr, gkn_ref[...])[0], c, sm, sp).astype(BF16)

    o_spec = pl.BlockSpec((tm, WIDTH_BP), lambda i: (i, 0))
    o_shape = jax.ShapeDtypeStruct((t, WIDTH_BP), BF16)
    return pl.pallas_call(
        body, name=name, grid=(t // tm,), in_specs=_mla_in_specs(tm), out_specs=[o_spec] * 3,
        out_shape=[o_shape] * 3, compiler_params=_cp("parallel"))(
            proj, proj, proj, *tabs, gq, gkv, gqn, gkn, wuq, wk, wv)


def _mla_pre_bwd(proj, tabs, gq, gkv, gqn, gkn, wuq, wk, wv, dq, dk, dv, name):
    t = proj.shape[0]
    tm = _tile(t, (256, 128))

    def body(cq_ref, ckv_ref, kr_ref, c_ref, sm_ref, sp_ref, gq_ref, gkv_ref, gqn_ref, gkn_ref,
             wuq_ref, wk_ref, wv_ref, dq_ref, dk_ref, dv_ref,
             dcq_ref, dckv_ref, dkr_ref, dwuq_ref, dwk_ref, dwv_ref, dgq_ref, dgkv_ref, dgqn_ref, dgkn_ref,
             dqraw_ref, dkraw_ref):
        @pl.when(pl.program_id(0) == 0)
        def _():
            for r in (dwuq_ref, dwk_ref, dwv_ref, dgq_ref, dgkv_ref, dgqn_ref, dgkn_ref):
                r[...] = jnp.zeros_like(r)

        cqn_f, cq_xh, cq_r = _lora_norm(cq_ref[...], gq_ref[...])
        ckvn_f, ckv_xh, ckv_r = _lora_norm(ckv_ref[...], gkv_ref[...])
        cqn, ckvn = cqn_f.astype(BF16), ckvn_f.astype(BF16)
        q_raw = _dot(cqn, wuq_ref[...], _NT)
        k_raw = _dot(ckvn, wk_ref[...], _NT)
        kr = pltpu.roll(kr_ref[...], NOPE, 1)
        c, sm, sp = c_ref[...], sm_ref[...], sp_ref[...]
        dgqn = jnp.zeros((1, HEAD_PAD), F32)
        dgkn = jnp.zeros((1, HEAD_PAD), F32)
        dkr = jnp.zeros((tm, HEAD_PAD), F32)
        for h in range(N_HEADS_B):
            hs = slice(h * HEAD_PAD, (h + 1) * HEAD_PAD)
            _, xh, r = _pad_norm(q_raw[:, hs], gqn_ref[...])
            dn = _rope_t(dq_ref[:, hs], c, sm, sp)
            dgqn = dgqn + jnp.sum(dn * xh, axis=0, keepdims=True)
            dqraw_ref[:, hs] = _pad_norm_bwd(dn, xh, r, gqn_ref[...]).astype(BF16)
            _, xh, r = _pad_norm(k_raw[:, hs] + kr, gkn_ref[...])
            dn = _rope_t(dk_ref[:, hs], c, sm, sp)
            dgkn = dgkn + jnp.sum(dn * xh, axis=0, keepdims=True)
            dkc = _pad_norm_bwd(dn, xh, r, gkn_ref[...])
            dkraw_ref[:, hs] = dkc.astype(BF16)
            dkr = dkr + dkc
        dgqn_ref[...] += dgqn
        dgkn_ref[...] += dgkn
        lane = lax.broadcasted_iota(jnp.int32, (tm, HEAD_PAD), 1)
        dkr_ref[...] = jnp.where(lane < ROPE, pltpu.roll(dkr, HEAD_PAD - NOPE, 1), 0.0).astype(BF16)
        dqraw = dqraw_ref[...]
        dkraw = dkraw_ref[...]
        dvb = dv_ref[...].astype(BF16)
        dwuq_ref[...] += _dot(dqraw, cqn, _TN)
        dwk_ref[...] += _dot(dkraw, ckvn, _TN)
        dwv_ref[...] += _dot(dvb, ckvn, _TN)
        dcqn = _dot(dqraw, wuq_ref[...], _NN)
        dckvn = _dot(dkraw, wk_ref[...], _NN) + _dot(dvb, wv_ref[...], _NN)
        dgq_ref[...] += jnp.sum(dcqn * cq_xh, axis=0, keepdims=True)
        dgkv_ref[...] += jnp.sum(dckvn * ckv_xh, axis=0, keepdims=True)
        dxh = dcqn * gq_ref[...]
        dcq_ref[...] = (cq_r * (dxh - cq_xh * jnp.mean(dxh * cq_xh, axis=1, keepdims=True))).astype(BF16)
        dxh = dckvn * gkv_ref[...]
        dckv_ref[...] = (ckv_r * (dxh - ckv_xh * jnp.mean(dxh * ckv_xh, axis=1, keepdims=True))).astype(BF16)

    wide = pl.BlockSpec((tm, WIDTH_BP), lambda i: (i, 0))
    row = lambda w: pl.BlockSpec((tm, w), lambda i: (i, 0))
    full = lambda r, c: pl.BlockSpec((r, c), lambda i: (0, 0))
    return pl.pallas_call(
        body, name=name, grid=(t // tm,), in_specs=_mla_in_specs(tm) + [wide, wide, wide],
        out_specs=[row(Q_LORA), row(KV_LORA), row(HEAD_PAD), full(WIDTH_BP, Q_LORA), full(WIDTH_BP, KV_LORA),
                   full(WIDTH_BP, KV_LORA), full(1, Q_LORA), full(1, KV_LORA), full(1, HEAD_PAD), full(1, HEAD_PAD)],
        out_shape=[jax.ShapeDtypeStruct((t, Q_LORA), BF16), jax.ShapeDtypeStruct((t, KV_LORA), BF16),
                   jax.ShapeDtypeStruct((t, HEAD_PAD), BF16), jax.ShapeDtypeStruct((WIDTH_BP, Q_LORA), F32),
                   jax.ShapeDtypeStruct((WIDTH_BP, KV_LORA), F32), jax.ShapeDtypeStruct((WIDTH_BP, KV_LORA), F32),
                   jax.ShapeDtypeStruct((1, Q_LORA), F32), jax.ShapeDtypeStruct((1, KV_LORA), F32),
                   jax.ShapeDtypeStruct((1, HEAD_PAD), F32), jax.ShapeDtypeStruct((1, HEAD_PAD), F32)],
        scratch_shapes=[pltpu.VMEM((tm, WIDTH_BP), BF16), pltpu.VMEM((tm, WIDTH_BP), BF16)],
        compiler_params=_cp("arbitrary"))(proj, proj, proj, *tabs, gq, gkv, gqn, gkn, wuq, wk, wv, dq, dk, dv)


def _mla_flash_specs(s_len, tq):
    nq = s_len // tq
    q_spec = pl.BlockSpec((tq, HEAD_PAD), lambda b, h, i: (b * nq + i, h))
    kv_spec = pl.BlockSpec((s_len, HEAD_PAD), lambda b, h, i: (b, h))
    lse_spec = pl.BlockSpec((1, tq, 1), lambda b, h, i: (b * N_HEADS_B + h, i, 0))
    return nq, q_spec, kv_spec, lse_spec


def _diag_mask(s):
    row = lax.broadcasted_iota(jnp.int32, s.shape, 0)
    col = lax.broadcasted_iota(jnp.int32, s.shape, 1)
    return jnp.where(row >= col, s, NEG)


def _mla_flash_fwd(q, k, v, n_batch, s_len, name):
    t = q.shape[0]
    tq = _tile(s_len, (256, 128))
    nq, q_spec, kv_spec, lse_spec = _mla_flash_specs(s_len, tq)
    c = (QK_B ** -0.5) * LOG2E

    def body(q_ref, k_ref, v_ref, o_ref, lse_ref):
        i = pl.program_id(2)
        qv = q_ref[...]

        def tile(kc, carry, diag):
            m, l, acc = carry
            ks = pl.multiple_of(kc * tq, tq)
            s = _dot(qv, k_ref[pl.ds(ks, tq), :], _NT)
            if diag:
                s = _diag_mask(s)
            m_new = jnp.maximum(m, jnp.max(s, axis=1, keepdims=True))
            alpha = jnp.exp2((m - m_new) * c)
            p = jnp.exp2((s - m_new) * c)
            l = alpha * l + jnp.sum(p, axis=1, keepdims=True)
            acc = alpha * acc + _dot(p.astype(BF16), v_ref[pl.ds(ks, tq), :], _NN)
            return m_new, l, acc

        init = (jnp.full((tq, 1), NEG, F32), jnp.zeros((tq, 1), F32), jnp.zeros((tq, HEAD_PAD), F32))
        carry = lax.fori_loop(0, i, lambda kc, cr: tile(kc, cr, False), init)
        m, l, acc = tile(i, carry, True)
        o_ref[...] = (acc * (1.0 / l)).astype(BF16)
        lse_ref[0] = m * c + jnp.log2(l)

    return pl.pallas_call(
        body, name=name, grid=(n_batch, N_HEADS_B, nq), in_specs=[q_spec, kv_spec, kv_spec],
        out_specs=[q_spec, lse_spec],
        out_shape=[jax.ShapeDtypeStruct((t, WIDTH_BP), BF16),
                   jax.ShapeDtypeStruct((n_batch * N_HEADS_B, s_len, 1), F32)],
        compiler_params=_cp("parallel", "parallel", "parallel"))(q, k, v)


def _mla_flash_bwd(q, k, v, o, do, lse2, n_batch, s_len, name):
    t = q.shape[0]
    tq = _tile(s_len, (256, 128))
    nq, q_spec, kv_spec, lse_spec = _mla_flash_specs(s_len, tq)
    scale = QK_B ** -0.5
    c = scale * LOG2E

    def body(q_ref, k_ref, v_ref, o_ref, do_ref, lse_ref, dq_ref, dk_ref, dv_ref):
        i = pl.program_id(2)

        @pl.when(i == 0)
        def _():
            dk_ref[...] = jnp.zeros_like(dk_ref)
            dv_ref[...] = jnp.zeros_like(dv_ref)

        qv, dov = q_ref[...], do_ref[...]
        lse = lse_ref[0]
        delta = jnp.sum(dov.astype(F32) * o_ref[...].astype(F32), axis=1, keepdims=True)

        def tile(kc, dq, diag):
            ks = pl.multiple_of(kc * tq, tq)
            kblk = k_ref[pl.ds(ks, tq), :]
            s = _dot(qv, kblk, _NT)
            if diag:
                s = _diag_mask(s)
            p = jnp.exp2(s * c - lse)
            dp = _dot(dov, v_ref[pl.ds(ks, tq), :], _NT)
            ds = (p * (dp - delta) * scale).astype(BF16)
            dk_ref[pl.ds(ks, tq), :] += _dot(ds, qv, _TN)
            dv_ref[pl.ds(ks, tq), :] += _dot(p.astype(BF16), dov, _TN)
            return dq + _dot(ds, kblk, _NN)

        dq = lax.fori_loop(0, i, lambda kc, d: tile(kc, d, False), jnp.zeros((tq, HEAD_PAD), F32))
        dq_ref[...] = tile(i, dq, True)

    f32_wide = jax.ShapeDtypeStruct((t, WIDTH_BP), F32)
    return pl.pallas_call(
        body, name=name, grid=(n_batch, N_HEADS_B, nq),
        in_specs=[q_spec, kv_spec, kv_spec, q_spec, q_spec, lse_spec],
        out_specs=[q_spec, kv_spec, kv_spec], out_shape=[f32_wide] * 3,
        compiler_params=_cp("parallel", "parallel", "arbitrary"))(q, k, v, o, do, lse2)


def _swa_heads(w, axis, order):
    heads = [lax.slice_in_dim(w, h * HEAD_DIM_A, (h + 1) * HEAD_DIM_A, axis=axis) for h in order]
    return jnp.concatenate(heads, axis=axis)


def _layer_mats(w):
    w_in = w["w_in"]
    o = [0]
    for n in (WIDTH_A, KV_A, KV_A, Q_LORA, KV_LORA, ROPE, D_MODEL, D_MODEL):
        o.append(o[-1] + n)
    qa, ka, va, cq, ckv, kr, ga, gb = (w_in[o[i]:o[i + 1]] for i in range(8))
    pad = jnp.zeros((PROJ_W - IN_WIDTH, w_in.shape[1]), w_in.dtype)
    w_in_p = jnp.concatenate([ga, gb, _swa_heads(qa, 0, SWA_HEAD_ORDER), cq, ka, va, ckv, kr, pad], axis=0)
    uq = w["mla_w_uq"].reshape(N_HEADS_B, QK_B, Q_LORA)
    uq = jnp.pad(uq, ((0, 0), (0, HEAD_PAD - QK_B), (0, 0))).reshape(WIDTH_BP, Q_LORA)
    ukv = w["mla_w_ukv"].reshape(N_HEADS_B, NOPE + V_B, KV_LORA)
    wk = jnp.pad(ukv[:, :NOPE], ((0, 0), (0, HEAD_PAD - NOPE), (0, 0))).reshape(WIDTH_BP, KV_LORA)
    wv = jnp.pad(ukv[:, NOPE:], ((0, 0), (0, HEAD_PAD - V_B), (0, 0))).reshape(WIDTH_BP, KV_LORA)
    wb = w["w_branch_b"].reshape(D_MODEL, N_HEADS_B, V_B)
    wb = jnp.pad(wb, ((0, 0), (0, 0), (0, HEAD_PAD - V_B))).reshape(D_MODEL, WIDTH_BP)
    out = dict(w)
    out.update(w_in=w_in_p, mla_w_uq=uq, wk=wk, wv=wv, w_branch_b=wb,
               w_branch_a=_swa_heads(w["w_branch_a"], 1, SWA_HEAD_ORDER))
    return out


def _unlayer_grads(g):
    d = g["w_in"]
    ga, gb, qa, cq, ka, va, ckv, kr = (d[a:b] for a, b in (
        (P_GA, P_GA + D_MODEL), (P_GB, P_GB + D_MODEL), (P_QA, P_QA + WIDTH_A), (P_CQ, P_CQ + Q_LORA),
        (P_KA, P_KA + KV_A), (P_VA, P_VA + KV_A), (P_CKV, P_CKV + KV_LORA), (P_KR, P_KR + ROPE)))
    out = {n: v for n, v in g.items() if n not in ("wk", "wv")}
    out["w_in"] = jnp.concatenate([_swa_heads(qa, 0, SWA_HEAD_INVERSE), ka, va, cq, ckv, kr, ga, gb], axis=0)
    out["mla_w_uq"] = g["mla_w_uq"].reshape(N_HEADS_B, HEAD_PAD, Q_LORA)[:, :QK_B].reshape(N_HEADS_B * QK_B, Q_LORA)
    dk = g["wk"].reshape(N_HEADS_B, HEAD_PAD, KV_LORA)[:, :NOPE]
    dv = g["wv"].reshape(N_HEADS_B, HEAD_PAD, KV_LORA)[:, :V_B]
    out["mla_w_ukv"] = jnp.concatenate([dk, dv], axis=1).reshape(N_HEADS_B * (NOPE + V_B), KV_LORA)
    out["w_branch_b"] = g["w_branch_b"].reshape(D_MODEL, N_HEADS_B, HEAD_PAD)[:, :, :V_B].reshape(D_MODEL, WIDTH_B)
    out["w_branch_a"] = _swa_heads(g["w_branch_a"], 1, SWA_HEAD_INVERSE)
    return out


def _pad_lanes(v, width):
    return jnp.pad(v.reshape(1, -1), ((0, 0), (0, width - v.shape[-1])))


def _rope_tables(positions):
    half = ROPE // 2
    inv_freq = ROPE_BASE ** (-jnp.arange(half, dtype=F32) / half)
    ang = positions.astype(F32).reshape(-1, 1) * inv_freq
    cos, sin = jnp.cos(ang), jnp.sin(ang)
    t = cos.shape[0]
    one, zero = jnp.ones((t, NOPE), F32), jnp.zeros((t, NOPE), F32)
    tail = jnp.zeros((t, HEAD_PAD - QK_B), F32)
    z16 = jnp.zeros((t, half), F32)
    c = jnp.concatenate([one, cos, cos, tail], axis=1)
    sm = jnp.concatenate([zero, -sin, z16, tail], axis=1)
    sp = jnp.concatenate([zero, z16, sin, tail], axis=1)
    return c, sm, sp


def _ffn_fwd(x, gain, wg_t, wu_t, wd, tag):
    n = _rms_fwd(x, gain, f"{tag}_norm")
    a, b, hmid = _ffn_up(n, wg_t, wu_t, f"{tag}_up")
    y = _mm([(hmid, wd)], "nn", F32, f"{tag}_down", residual=x, alpha=0.5)
    return y, (x, n, a, b, hmid)


def _ffn_bwd(dy, dyb, saved, gain, wg_t, wu_t, wd, tag):
    x, n, a, b, hmid = saved
    da, db = _ffn_down_bwd(dyb, wd, a, b, f"{tag}_down_bwd")
    g_wd = _mm([(hmid, dyb)], "tn", BF16, f"{tag}_dwd", alpha=0.5)
    g_wg = _mm([(da, n)], "tn", BF16, f"{tag}_dwg")
    g_wu = _mm([(db, n)], "tn", BF16, f"{tag}_dwu")
    dn = _mm([(da, wg_t), (db, wu_t)], "nn", F32, f"{tag}_dn")
    dx, dxb, g_gain = _rms_bwd(dn, x, gain, dy, f"{tag}_norm_bwd")
    return dx, dxb, g_wg, g_wu, g_wd, g_gain


def _fold_halves(d):
    return d[:, :HEAD_DIM_A] + d[:, HEAD_DIM_A:]


def _local_step(x, positions, target, layers, smalls):
    n_batch, s_len, d = x.shape
    t = n_batch * s_len
    xt = x.reshape(t, d)
    tabs = _rope_tables(positions)
    pos_col = positions.reshape(t, 1)
    pos_row = positions.reshape(t // BLOCK, 1, BLOCK)
    saved = []
    for l in range(len(layers)):
        w, s = layers[l], smalls[l]
        g1, gm, g2 = (s[k].reshape(1, d) for k in ("ffn1_norm", "mix_norm", "ffn2_norm"))
        qg2, kg2 = (jnp.tile(s[k].reshape(1, -1), (1, 2)) for k in ("swa_q_norm", "swa_k_norm"))
        sinks = _pad_lanes(s["swa_sinks"], HEAD_PAD)
        gq, gkv = s["mla_q_lora_norm"].reshape(1, -1), s["mla_kv_lora_norm"].reshape(1, -1)
        gqn, gkn = _pad_lanes(s["mla_q_norm"], HEAD_PAD), _pad_lanes(s["mla_k_norm"], HEAD_PAD)
        x1, sv1 = _ffn_fwd(xt, g1, w["ffn1_w_gate"], w["ffn1_w_up"], w["ffn1_w_down"], f"l{l}_ffn1")
        h = _rms_fwd(x1, gm, f"l{l}_mix_norm")
        proj = _mm([(h, w["w_in"])], "nt", F32, f"l{l}_proj")
        oa = _swa_fwd(proj, pos_col, pos_row, qg2, kg2, sinks, n_batch, s_len, f"l{l}_swa")
        q, k, v = _mla_pre(proj, tabs, gq, gkv, gqn, gkn, w["mla_w_uq"], w["wk"], w["wv"], f"l{l}_mla_pre")
        ob, lse = _mla_flash_fwd(q, k, v, n_batch, s_len, f"l{l}_mla")
        merged, ya, yb = _merge_fwd(oa, ob, proj, w["w_branch_a"], w["w_branch_b"], f"l{l}_merge")
        x2 = _mm([(merged, w["w_out"])], "nn", F32, f"l{l}_out", residual=x1)
        x3, sv2 = _ffn_fwd(x2, g2, w["ffn2_w_gate"], w["ffn2_w_up"], w["ffn2_w_down"], f"l{l}_ffn2")
        saved.append((sv1, sv2, x1, h, proj, oa, q, k, v, ob, lse, merged, ya, yb,
                      (g1, gm, g2, qg2, kg2, sinks, gq, gkv, gqn, gkn)))
        xt = x3

    dy, dyb, loss = _loss_head(xt, target.reshape(t, d))

    big_grads, small_grads = [None] * len(layers), [None] * len(layers)
    for l in reversed(range(len(layers))):
        w = layers[l]
        sv1, sv2, x1, h, proj, oa, q, k, v, ob, lse, merged, ya, yb, gains = saved[l]
        g1, gm, g2, qg2, kg2, sinks, gq, gkv, gqn, gkn = gains
        bg, sg = {}, {}
        dy, dyb, bg["ffn2_w_gate"], bg["ffn2_w_up"], bg["ffn2_w_down"], sg["ffn2_norm"] = _ffn_bwd(
            dy, dyb, sv2, g2, w["ffn2_w_gate"], w["ffn2_w_up"], w["ffn2_w_down"], f"l{l}_ffn2")
        dya, dyb_, dga, dgb = _merge_bwd(dyb, w["w_out"], proj, ya, yb, f"l{l}_merge_bwd")
        bg["w_out"] = _mm([(merged, dyb)], "tn", BF16, f"l{l}_dwo")
        doa = _mm([(dya, w["w_branch_a"])], "nn", BF16, f"l{l}_doa")
        bg["w_branch_a"] = _mm([(dya, oa)], "tn", BF16, f"l{l}_dwa")
        dob = _mm([(dyb_, w["w_branch_b"])], "nn", BF16, f"l{l}_dob")
        bg["w_branch_b"] = _mm([(dyb_, ob)], "tn", BF16, f"l{l}_dwb")
        dqa, dkc, dkp, dvc, dvp, dqg, dsk = _swa_bwd(
            proj, pos_col, pos_row, qg2, kg2, sinks, doa, n_batch, s_len, f"l{l}_swa_bwd")
        sg["swa_q_norm"], sg["swa_sinks"] = _fold_halves(dqg), dsk[:, :N_HEADS_A]
        dka, dva, dkg = _swa_kv_bwd(proj, kg2, dkc, dkp, dvc, dvp, n_batch, s_len, f"l{l}_swa_kv_bwd")
        sg["swa_k_norm"] = _fold_halves(dkg)
        dq, dk, dv = _mla_flash_bwd(q, k, v, ob, dob, lse, n_batch, s_len, f"l{l}_mla_bwd")
        (dcq, dckv, dkr, g_uq, g_wk, g_wv, sg["mla_q_lora_norm"], sg["mla_kv_lora_norm"], dgqn, dgkn) = _mla_pre_bwd(
            proj, tabs, gq, gkv, gqn, gkn, w["mla_w_uq"], w["wk"], w["wv"], dq, dk, dv, f"l{l}_mla_pre_bwd")
        sg["mla_q_norm"], sg["mla_k_norm"] = dgqn[:, :QK_B], dgkn[:, :QK_B]
        bg["mla_w_uq"], bg["wk"], bg["wv"] = g_uq.astype(BF16), g_wk.astype(BF16), g_wv.astype(BF16)
        dproj = jnp.concatenate([dga, dgb, dqa, dcq, dka, dva, dckv, dkr], axis=1)
        bg["w_in"] = _mm([(dproj, h)], "tn", BF16, f"l{l}_dwin")
        dh = _mm([(dproj, w["w_in"])], "nn", F32, f"l{l}_dh")
        dy, dyb, sg["mix_norm"] = _rms_bwd(dh, x1, gm, dy, f"l{l}_mix_norm_bwd")
        dy, dyb, bg["ffn1_w_gate"], bg["ffn1_w_up"], bg["ffn1_w_down"], sg["ffn1_norm"] = _ffn_bwd(
            dy, dyb, sv1, g1, w["ffn1_w_gate"], w["ffn1_w_up"], w["ffn1_w_down"], f"l{l}_ffn1")
        big_grads[l], small_grads[l] = bg, sg
    return loss, dy.reshape(n_batch, s_len, d), big_grads, small_grads


def _round_up(n, m):
    return (n + m - 1) // m * m


def _flat_rows(shape, transposed):
    rows, k = (shape[1], shape[0]) if transposed else shape
    return _round_up(rows * k // LANES, 16), rows, k


def _flat_layout(shard_shapes):
    table, off = [], 0
    for l in range(DEPTH):
        for name, tr in BIG:
            pr, rows, k = _flat_rows(shard_shapes[name], tr)
            table.append((l, name, tr, off, pr, rows, k))
            off += pr
    return table, _round_up(off, FLAT_ROW_TILE)


def _pack_flat(params, table, total):
    parts, off = [], 0
    for l, name, tr, o, pr, rows, k in table:
        w = params[name][l]
        w = (w.T if tr else w).reshape(rows * k // LANES, LANES)
        parts.append(jnp.pad(w, ((0, pr - w.shape[0]), (0, 0))))
        off = o + pr
    if total > off:
        parts.append(jnp.zeros((total - off, LANES), parts[0].dtype))
    return jnp.concatenate(parts, axis=0)


def _unpack_flat(flat, table):
    out = {}
    for l, name, tr, o, pr, rows, k in table:
        w = flat[o:o + rows * k // LANES].reshape(rows, k)
        out.setdefault(name, []).append(w.T if tr else w)
    return {n: jnp.stack(v) for n, v in out.items()}


def _gathered_mats(gathered, table):
    layers = [dict() for _ in range(DEPTH)]
    for l, name, tr, o, pr, rows, k in table:
        layers[l][name] = gathered[:, o:o + rows * k // LANES].reshape(N_DEV * rows, k)
    return layers


def _pack_grads(grads, table, total):
    parts, off = [], 0
    for l, name, tr, o, pr, rows, k in table:
        g = grads[l][name].reshape(N_DEV, rows * k // LANES, LANES)
        parts.append(jnp.pad(g, ((0, 0), (0, pr - g.shape[1]), (0, 0))))
        off = o + pr
    if total > off:
        parts.append(jnp.zeros((N_DEV, total - off, LANES), BF16))
    return jnp.concatenate(parts, axis=1)


def _pack_small(params):
    parts = [params[n][l].reshape(-1).astype(F32) for l in range(DEPTH) for n in SMALL]
    v = jnp.concatenate(parts)
    return jnp.pad(v, (0, SMALL_ROWS * LANES - v.shape[0])).reshape(SMALL_ROWS, LANES)


def _unpack_small(flat, shapes):
    v, out, off = flat.reshape(-1), {}, 0
    for l in range(DEPTH):
        for n in SMALL:
            sz = math.prod(shapes[n][1:])
            out.setdefault(n, []).append(v[off:off + sz].reshape(shapes[n][1:]))
            off += sz
    return {n: jnp.stack(p) for n, p in out.items()}


_ANY = pl.BlockSpec(memory_space=pl.ANY)
_MESH = pl.DeviceIdType.MESH


def _place():
    return lax.axis_index("x"), lax.axis_index("y"), lax.axis_index("c")


def _all_gather(x_shard, name, vmem=False):
    spec = pl.BlockSpec(memory_space=pltpu.VMEM) if vmem else _ANY

    def body(x_ref, out_ref, send_sems, recv_sems, local_sem):
        x, y, c = _place()
        me, sibling = (x, y, c), (x, y, 1 - c)
        chips = [(1 - x, y), (x, 1 - y), (1 - x, 1 - y)]

        def rows(px, py, pc):
            return out_ref.at[4 * px + 2 * py + pc]

        def copy(k, block, to, src=None):
            return pltpu.make_async_remote_copy(
                src_ref=rows(*block) if src is None else src, dst_ref=rows(*block),
                send_sem=send_sems.at[k], recv_sem=recv_sems.at[k], device_id=to, device_id_type=_MESH)

        mine = pltpu.make_async_copy(x_ref, rows(*me), local_sem)
        mine.start()
        first = [copy(0, me, sibling, src=x_ref)]
        first += [copy(1 + j, me, (*chip, c), src=x_ref) for j, chip in enumerate(chips)]
        for cp in first:
            cp.start()
        passed = [copy(4 + j, (*chip, c), sibling) for j, chip in enumerate(chips)]
        for j, chip in enumerate(chips):
            copy(1 + j, (*chip, c), me).wait_recv()
            passed[j].start()
        copy(0, sibling, me).wait_recv()
        for j, chip in enumerate(chips):
            copy(4 + j, (*chip, 1 - c), me).wait_recv()
        for cp in first + passed:
            cp.wait_send()
        mine.wait()

    return pl.pallas_call(
        body, name=name, out_shape=jax.ShapeDtypeStruct((N_DEV,) + x_shard.shape, x_shard.dtype),
        in_specs=[spec], out_specs=spec,
        scratch_shapes=[pltpu.SemaphoreType.DMA((7,)), pltpu.SemaphoreType.DMA((7,)), pltpu.SemaphoreType.DMA],
    )(x_shard)


def _exchange_cores(g4, name):
    n_chip, _, r, w = g4.shape

    def body(g_ref, out_ref, send_sems, recv_sems):
        x, y, c = _place()
        copies = [pltpu.make_async_remote_copy(
            src_ref=g_ref.at[q, 1 - c], dst_ref=out_ref.at[q], send_sem=send_sems.at[q], recv_sem=recv_sems.at[q],
            device_id=(x, y, 1 - c), device_id_type=_MESH) for q in range(n_chip)]
        for cp in copies:
            cp.start()
        for cp in copies:
            cp.wait()

    return pl.pallas_call(
        body, name=name, out_shape=jax.ShapeDtypeStruct((n_chip, r, w), g4.dtype), in_specs=[_ANY], out_specs=_ANY,
        scratch_shapes=[pltpu.SemaphoreType.DMA((n_chip,)), pltpu.SemaphoreType.DMA((n_chip,))],
    )(g4)


def _exchange_chips(s1, name):
    _, r, w = s1.shape

    def body(s_ref, out_ref, send_sems, recv_sems):
        x, y, c = _place()
        copies = []
        for k, (tx, ty) in enumerate([(1 - x, y), (x, 1 - y), (1 - x, 1 - y)]):
            copies.append(pltpu.make_async_remote_copy(
                src_ref=s_ref.at[2 * tx + ty], dst_ref=out_ref.at[k], send_sem=send_sems.at[k],
                recv_sem=recv_sems.at[k], device_id=(tx, ty, c), device_id_type=_MESH))
        for cp in copies:
            cp.start()
        for cp in copies:
            cp.wait()

    return pl.pallas_call(
        body, name=name, out_shape=jax.ShapeDtypeStruct((3, r, w), s1.dtype), in_specs=[_ANY], out_specs=_ANY,
        scratch_shapes=[pltpu.SemaphoreType.DMA((3,)), pltpu.SemaphoreType.DMA((3,))],
    )(s1)


def _chip_sum(g4, recv, core, name):
    n_chip, _, r, w = g4.shape
    tr = FLAT_ROW_TILE

    def body(c_ref, a_ref, b_ref, o_ref):
        o_ref[...] = (a_ref[...].astype(F32) + b_ref[...].astype(F32)).astype(o_ref.dtype)

    grid_spec = pltpu.PrefetchScalarGridSpec(
        num_scalar_prefetch=1, grid=(n_chip, r // tr),
        in_specs=[pl.BlockSpec((None, None, tr, w), lambda q, i, c: (q, c[0], i, 0)),
                  pl.BlockSpec((None, tr, w), lambda q, i, c: (q, i, 0))],
        out_specs=pl.BlockSpec((None, tr, w), lambda q, i, c: (q, i, 0)))
    return pl.pallas_call(
        body, name=name, grid_spec=grid_spec, out_shape=jax.ShapeDtypeStruct((n_chip, r, w), g4.dtype),
        compiler_params=_cp("parallel", "parallel"))(core, g4, recv)


def _adam(w, g, m, v):
    m = ADAM_B1 * m + (1.0 - ADAM_B1) * g
    v = ADAM_B2 * v + (1.0 - ADAM_B2) * (g * g)
    m_hat = m / (1.0 - ADAM_B1 ** ADAM_STEP)
    v_hat = v / (1.0 - ADAM_B2 ** ADAM_STEP)
    delta = -ADAM_LR * (m_hat / (jnp.sqrt(v_hat) + ADAM_EPS) + ADAM_WD * w)
    return delta, m, v


def _adam_big(s1, r2, chip, w, m, v, name):
    r, lanes = w.shape
    tr = FLAT_ROW_TILE

    def body(c_ref, s_ref, r0_ref, r1_ref, r2_ref, w_ref, m_ref, v_ref, g_out, d_out, m_out, v_out):
        g = ((s_ref[...].astype(F32) + r0_ref[...].astype(F32)) + r1_ref[...].astype(F32)) + r2_ref[...].astype(F32)
        d, mn, vn = _adam(w_ref[...], g, m_ref[...], v_ref[...])
        g_out[...] = g
        d_out[...] = d
        m_out[...] = mn
        v_out[...] = vn

    row = pl.BlockSpec((tr, lanes), lambda i, c: (i, 0))
    rel = lambda k: pl.BlockSpec((None, tr, lanes), lambda i, c: (k, i, 0))
    grid_spec = pltpu.PrefetchScalarGridSpec(
        num_scalar_prefetch=1, grid=(r // tr,),
        in_specs=[pl.BlockSpec((None, tr, lanes), lambda i, c: (c[0], i, 0)), rel(0), rel(1), rel(2), row, row, row],
        out_specs=[row] * 4)
    return pl.pallas_call(
        body, name=name, grid_spec=grid_spec, out_shape=[jax.ShapeDtypeStruct((r, lanes), F32)] * 4,
        compiler_params=_cp("parallel"))(chip, s1, r2, r2, r2, w, m, v)


def _adam_small(parts, w, m, v, name):
    rows, lanes = w.shape

    def body(p_ref, w_ref, m_ref, v_ref, g_out, d_out, m_out, v_out):
        g = p_ref[0:rows, :]
        for dev in range(1, N_DEV):
            g = g + p_ref[dev * rows:(dev + 1) * rows, :]
        d, mn, vn = _adam(w_ref[...], g, m_ref[...], v_ref[...])
        g_out[...] = g
        d_out[...] = d
        m_out[...] = mn
        v_out[...] = vn

    return pl.pallas_call(
        body, name=name, out_shape=[jax.ShapeDtypeStruct((rows, lanes), F32)] * 4)(parts, w, m, v)


def kernel(x, positions, ffn1_norm, ffn1_w_gate, ffn1_w_up, ffn1_w_down, mix_norm, w_in, swa_q_norm, swa_k_norm, swa_sinks, mla_q_lora_norm, mla_w_uq, mla_kv_lora_norm, mla_w_ukv, mla_q_norm, mla_k_norm, w_branch_a, w_branch_b, w_out, ffn2_norm, ffn2_w_gate, ffn2_w_up, ffn2_w_down, loss_target, m_ffn1_norm, m_ffn1_w_gate, m_ffn1_w_up, m_ffn1_w_down, m_mix_norm, m_w_in, m_swa_q_norm, m_swa_k_norm, m_swa_sinks, m_mla_q_lora_norm, m_mla_w_uq, m_mla_kv_lora_norm, m_mla_w_ukv, m_mla_q_norm, m_mla_k_norm, m_w_branch_a, m_w_branch_b, m_w_out, m_ffn2_norm, m_ffn2_w_gate, m_ffn2_w_up, m_ffn2_w_down, v_ffn1_norm, v_ffn1_w_gate, v_ffn1_w_up, v_ffn1_w_down, v_mix_norm, v_w_in, v_swa_q_norm, v_swa_k_norm, v_swa_sinks, v_mla_q_lora_norm, v_mla_w_uq, v_mla_kv_lora_norm, v_mla_w_ukv, v_mla_q_norm, v_mla_k_norm, v_w_branch_a, v_w_branch_b, v_w_out, v_ffn2_norm, v_ffn2_w_gate, v_ffn2_w_up, v_ffn2_w_down):
    given = dict(locals())
    params = {n: given[n] for n in WEIGHTS}
    mom1 = {n: given["m_" + n] for n in WEIGHTS}
    mom2 = {n: given["v_" + n] for n in WEIGHTS}
    table, total = _flat_layout({n: params[n].shape[1:] for n, _ in BIG})

    w_flat = _pack_flat(params, table, total)
    gathered = _all_gather(w_flat.astype(BF16), "gather_weights")
    layers = [_layer_mats(w) for w in _gathered_mats(gathered, table)]
    smalls = [{n: params[n][l] for n in SMALL} for l in range(DEPTH)]

    loss, grad_x, big_grads, small_grads = _local_step(x, positions, loss_target, layers, smalls)
    loss = lax.psum(loss[0, 0], ("x", "y", "c"))

    cx, cy, cc = _place()
    core = jnp.reshape(cc, (1,)).astype(jnp.int32)
    chip = jnp.reshape(2 * cx + cy, (1,)).astype(jnp.int32)
    g_flat = _pack_grads([_unlayer_grads(g) for g in big_grads], table, total)
    g4 = g_flat.reshape(N_DEV // 2, 2, total, LANES)
    from_core = _exchange_cores(g4, "scatter_cores")
    s1 = _chip_sum(g4, from_core, core, "sum_cores")
    from_chips = _exchange_chips(s1, "scatter_chips")
    g_big, d_big, m_big, v_big = _adam_big(s1, from_chips, chip, w_flat, _pack_flat(mom1, table, total),
                                           _pack_flat(mom2, table, total), "adam_big")

    g_small = _pack_small({n: [small_grads[l][n] for l in range(DEPTH)] for n in SMALL})
    parts = _all_gather(g_small, "gather_small", vmem=True).reshape(N_DEV * SMALL_ROWS, LANES)
    small_out = _adam_small(parts, _pack_small(params), _pack_small(mom1), _pack_small(mom2), "adam_small")

    outs = []
    shapes = {n: params[n].shape for n in SMALL}
    for big, small in zip((g_big, d_big, m_big, v_big), small_out):
        tree = _unpack_flat(big, table)
        tree.update(_unpack_small(small, shapes))
        outs.append(tree)
    return (loss, grad_x, *[o[n] for o in outs for n in WEIGHTS])
```

```python
import math

import jax
import jax.numpy as jnp
from jax import lax
from jax.experimental import pallas as pl
from jax.experimental.pallas import tpu as pltpu
from jax.experimental.pallas import tpu_sc as plsc

F32 = jnp.float32
BF16 = jnp.bfloat16

N_DEV = 8
DEPTH = 2
D_MODEL = 1024
D_FF = 2816
HEAD_DIM_A = 64
N_HEADS_A = 8
N_KV_HEADS_A = 2
GROUP_A = N_HEADS_A // N_KV_HEADS_A
BLOCK = 128
N_HEADS_B = 8
Q_LORA = 256
KV_LORA = 128
NOPE = 64
ROPE = 32
QK_B = NOPE + ROPE
V_B = 64
HEAD_PAD = 128
WIDTH_A = N_HEADS_A * HEAD_DIM_A
WIDTH_B = N_HEADS_B * V_B
WIDTH_BP = N_HEADS_B * HEAD_PAD
KV_A = N_KV_HEADS_A * HEAD_DIM_A
IN_WIDTH = WIDTH_A + 2 * KV_A + Q_LORA + KV_LORA + ROPE + 2 * D_MODEL
ROPE_BASE = 10000.0
EPS = 1e-6
NEG = -1e30
LOG2E = 1.4426950408889634

P_GA, P_GB, P_QA, P_CQ, P_KA, P_VA, P_CKV, P_KR = 0, 1024, 2048, 2560, 2816, 2944, 3072, 3200
PROJ_W = 3328
SWA_HEAD_ORDER = (0, 4, 1, 5, 2, 6, 3, 7)
SWA_HEAD_INVERSE = tuple(SWA_HEAD_ORDER.index(h) for h in range(N_HEADS_A))

ADAM_LR, ADAM_B1, ADAM_B2, ADAM_EPS, ADAM_WD, ADAM_STEP = 0.001, 0.9, 0.999, 1e-08, 0.01, 10

VMEM_LIMIT = 56 * 1024 * 1024
LANES = 1024

BIG = (("ffn1_w_gate", True), ("ffn1_w_up", True), ("ffn1_w_down", False), ("w_in", True), ("mla_w_uq", True),
       ("mla_w_ukv", True), ("w_branch_a", True), ("w_branch_b", True), ("w_out", False),
       ("ffn2_w_gate", True), ("ffn2_w_up", True), ("ffn2_w_down", False))
SMALL = ("ffn1_norm", "mix_norm", "ffn2_norm", "swa_q_norm", "swa_k_norm", "swa_sinks", "mla_q_lora_norm",
         "mla_kv_lora_norm", "mla_q_norm", "mla_k_norm")
WEIGHTS = ("ffn1_norm", "ffn1_w_gate", "ffn1_w_up", "ffn1_w_down", "mix_norm", "w_in", "swa_q_norm", "swa_k_norm",
           "swa_sinks", "mla_q_lora_norm", "mla_w_uq", "mla_kv_lora_norm", "mla_w_ukv", "mla_q_norm", "mla_k_norm",
           "w_branch_a", "w_branch_b", "w_out", "ffn2_norm", "ffn2_w_gate", "ffn2_w_up", "ffn2_w_down")
FLAT_ROW_TILE = 480
SMALL_ROWS = 8
SEQ_GATHER, SEQ_CORES, SEQ_CHIPS = 1, 2, 3


def _cp(*sem):
    return pltpu.CompilerParams(dimension_semantics=sem, vmem_limit_bytes=VMEM_LIMIT)


def _tile(n, prefs):
    for t in prefs:
        if n % t == 0:
            return t
    return n


def _dot(a, b, dims):
    return lax.dot_general(a, b, (dims, ((), ())), preferred_element_type=F32)


_NT = ((1,), (1,))
_NN = ((1,), (0,))
_TN = ((0,), (0,))


def _sigmoid(x):
    return 0.5 * jnp.tanh(0.5 * x) + 0.5


def _chunks(n, width):
    return [(c, min(width, n - c)) for c in range(0, n, width)]


def _mm(pairs, mode, out_dtype, name, residual=None, alpha=1.0):
    for a, b in pairs:
        assert a.dtype == BF16 and b.dtype == BF16, (name, a.dtype, b.dtype)
    if mode == "tn":
        (a, b), = pairs
        return _mm_tokens(a, b, out_dtype, name, alpha)
    t = pairs[0][0].shape[0]
    n = pairs[0][1].shape[0] if mode == "nt" else pairs[0][1].shape[1]
    tm = _tile(t, (512, 256, 128))
    dims = _NT if mode == "nt" else _NN
    in_specs, args = [], []
    for a, w in pairs:
        in_specs.append(pl.BlockSpec((tm, a.shape[1]), lambda i: (i, 0)))
        in_specs.append(pl.BlockSpec(w.shape, lambda i: (0, 0)))
        args += [a, w]
    if residual is not None:
        in_specs.append(pl.BlockSpec((tm, n), lambda i: (i, 0)))
        args.append(residual)
    n_pairs = len(pairs)

    def body(*refs):
        o_ref = refs[-1]
        for c0, cw in _chunks(n, 512):
            acc = None
            for p in range(n_pairs):
                w_ref = refs[2 * p + 1]
                w = w_ref[c0:c0 + cw, :] if mode == "nt" else w_ref[:, c0:c0 + cw]
                d = _dot(refs[2 * p][...], w, dims)
                acc = d if acc is None else acc + d
            if alpha != 1.0:
                acc = acc * alpha
            if residual is not None:
                acc = refs[2 * n_pairs][:, c0:c0 + cw] + acc
            o_ref[:, c0:c0 + cw] = acc.astype(out_dtype)

    return pl.pallas_call(
        body, name=name, grid=(t // tm,), in_specs=in_specs, out_specs=pl.BlockSpec((tm, n), lambda i: (i, 0)),
        out_shape=jax.ShapeDtypeStruct((t, n), out_dtype), compiler_params=_cp("parallel"))(*args)


def _mm_tokens(a, b, out_dtype, name, alpha):
    t, m = a.shape
    n = b.shape[1]
    tk = _tile(t, (512, 256, 128))

    def body(a_ref, b_ref, o_ref, acc_ref):
        k = pl.program_id(0)

        @pl.when(k == 0)
        def _():
            acc_ref[...] = jnp.zeros_like(acc_ref)

        for c0, cw in _chunks(m, 512):
            acc_ref[c0:c0 + cw, :] += _dot(a_ref[:, c0:c0 + cw], b_ref[...], _TN)

        @pl.when(k == pl.num_programs(0) - 1)
        def _():
            o_ref[...] = (acc_ref[...] * alpha).astype(out_dtype)

    return pl.pallas_call(
        body, name=name, grid=(t // tk,),
        in_specs=[pl.BlockSpec((tk, m), lambda k: (k, 0)), pl.BlockSpec((tk, n), lambda k: (k, 0))],
        out_specs=pl.BlockSpec((m, n), lambda k: (0, 0)), out_shape=jax.ShapeDtypeStruct((m, n), out_dtype),
        scratch_shapes=[pltpu.VMEM((m, n), F32)], compiler_params=_cp("arbitrary"))(a, b)


def _rms_fwd(x, g, name):
    t, d = x.shape
    tm = _tile(t, (512, 256, 128))

    def body(x_ref, g_ref, o_ref):
        xv = x_ref[...]
        r = lax.rsqrt(jnp.mean(xv * xv, axis=1, keepdims=True) + EPS)
        o_ref[...] = (xv * r * g_ref[...]).astype(BF16)

    return pl.pallas_call(
        body, name=name, grid=(t // tm,),
        in_specs=[pl.BlockSpec((tm, d), lambda i: (i, 0)), pl.BlockSpec((1, d), lambda i: (0, 0))],
        out_specs=pl.BlockSpec((tm, d), lambda i: (i, 0)),
        out_shape=jax.ShapeDtypeStruct((t, d), BF16), compiler_params=_cp("parallel"))(x, g)


def _rms_bwd(dn, x, g, res, name):
    t, d = x.shape
    tm = _tile(t, (512, 256, 128))

    def body(dn_ref, x_ref, g_ref, res_ref, dx_ref, dxb_ref, dg_ref):
        xv = x_ref[...]
        r = lax.rsqrt(jnp.mean(xv * xv, axis=1, keepdims=True) + EPS)
        xh = xv * r
        dnv = dn_ref[...]
        dxh = dnv * g_ref[...]
        dx = res_ref[...] + r * (dxh - xh * jnp.mean(dxh * xh, axis=1, keepdims=True))
        dx_ref[...] = dx
        dxb_ref[...] = dx.astype(BF16)

        @pl.when(pl.program_id(0) == 0)
        def _():
            dg_ref[...] = jnp.zeros_like(dg_ref)

        dg_ref[...] += jnp.sum(dnv * xh, axis=0, keepdims=True)

    row = pl.BlockSpec((tm, d), lambda i: (i, 0))
    one = pl.BlockSpec((1, d), lambda i: (0, 0))
    return pl.pallas_call(
        body, name=name, grid=(t // tm,), in_specs=[row, row, one, row], out_specs=[row, row, one],
        out_shape=[jax.ShapeDtypeStruct((t, d), F32), jax.ShapeDtypeStruct((t, d), BF16),
                   jax.ShapeDtypeStruct((1, d), F32)], compiler_params=_cp("arbitrary"))(dn, x, g, res)


def _ffn_up(n, wg_t, wu_t, name):
    t, d = n.shape
    f = wg_t.shape[0]
    tm = _tile(t, (512, 256, 128))

    def body(n_ref, wg_ref, wu_ref, a_ref, b_ref, h_ref):
        nv = n_ref[...]
        for c0, cw in _chunks(f, 256):
            a = _dot(nv, wg_ref[c0:c0 + cw, :], _NT)
            b = _dot(nv, wu_ref[c0:c0 + cw, :], _NT)
            a_ref[:, c0:c0 + cw] = a.astype(BF16)
            b_ref[:, c0:c0 + cw] = b.astype(BF16)
            h_ref[:, c0:c0 + cw] = (a * _sigmoid(a) * b).astype(BF16)

    w_spec = pl.BlockSpec((f, d), lambda i: (0, 0))
    o_spec = pl.BlockSpec((tm, f), lambda i: (i, 0))
    o_shape = jax.ShapeDtypeStruct((t, f), BF16)
    return pl.pallas_call(
        body, name=name, grid=(t // tm,), in_specs=[pl.BlockSpec((tm, d), lambda i: (i, 0)), w_spec, w_spec],
        out_specs=[o_spec] * 3, out_shape=[o_shape] * 3, compiler_params=_cp("parallel"))(n, wg_t, wu_t)


def _ffn_down_bwd(dxb, wd, a, b, name):
    t, d = dxb.shape
    f = wd.shape[0]
    tm = _tile(t, (512, 256, 128))

    def body(dx_ref, wd_ref, a_ref, b_ref, da_ref, db_ref):
        dxv = dx_ref[...]
        for c0, cw in _chunks(f, 256):
            dh = 0.5 * _dot(dxv, wd_ref[c0:c0 + cw, :], _NT)
            av = a_ref[:, c0:c0 + cw].astype(F32)
            bv = b_ref[:, c0:c0 + cw].astype(F32)
            sg = _sigmoid(av)
            da_ref[:, c0:c0 + cw] = (dh * bv * (sg * (1.0 + av * (1.0 - sg)))).astype(BF16)
            db_ref[:, c0:c0 + cw] = (dh * (av * sg)).astype(BF16)

    o_spec = pl.BlockSpec((tm, f), lambda i: (i, 0))
    o_shape = jax.ShapeDtypeStruct((t, f), BF16)
    return pl.pallas_call(
        body, name=name, grid=(t // tm,),
        in_specs=[pl.BlockSpec((tm, d), lambda i: (i, 0)), pl.BlockSpec((f, d), lambda i: (0, 0)), o_spec, o_spec],
        out_specs=[o_spec] * 2, out_shape=[o_shape] * 2, compiler_params=_cp("parallel"))(dxb, wd, a, b)


def _loss_head(y, target):
    t, d = y.shape
    tm = _tile(t, (512, 256, 128))

    def body(y_ref, t_ref, dy_ref, dyb_ref, loss_ref, acc_ref):
        i = pl.program_id(0)
        e = y_ref[...] - t_ref[...]
        dy = e * (1.0 / d)
        dy_ref[...] = dy
        dyb_ref[...] = dy.astype(BF16)

        @pl.when(i == 0)
        def _():
            acc_ref[...] = jnp.zeros_like(acc_ref)

        acc_ref[...] += jnp.sum(e * e, axis=0, keepdims=True)

        @pl.when(i == pl.num_programs(0) - 1)
        def _():
            loss_ref[...] = jnp.sum(acc_ref[...], axis=1, keepdims=True) * (0.5 / d)

    row = pl.BlockSpec((tm, d), lambda i: (i, 0))
    return pl.pallas_call(
        body, name="loss_head", grid=(t // tm,), in_specs=[row, row],
        out_specs=[row, row, pl.BlockSpec((1, 1), lambda i: (0, 0))],
        out_shape=[jax.ShapeDtypeStruct((t, d), F32), jax.ShapeDtypeStruct((t, d), BF16),
                   jax.ShapeDtypeStruct((1, 1), F32)],
        scratch_shapes=[pltpu.VMEM((1, d), F32)], compiler_params=_cp("arbitrary"))(y, target)


def _merge_fwd(oa, ob, proj, wa_t, wb_t, name):
    t = oa.shape[0]
    d = wa_t.shape[0]
    tm = _tile(t, (512, 256, 128))

    def body(oa_ref, ob_ref, ga_ref, gb_ref, wa_ref, wb_ref, mg_ref, ya_ref, yb_ref):
        oav, obv = oa_ref[...], ob_ref[...]
        for c0, cw in _chunks(d, 512):
            cs = slice(c0, c0 + cw)
            ya = _dot(oav, wa_ref[cs, :], _NT)
            yb = _dot(obv, wb_ref[cs, :], _NT)
            mg_ref[:, cs] = (_sigmoid(ga_ref[:, cs]) * ya + _sigmoid(gb_ref[:, cs]) * yb).astype(BF16)
            ya_ref[:, cs] = ya.astype(BF16)
            yb_ref[:, cs] = yb.astype(BF16)

    o_spec = pl.BlockSpec((tm, d), lambda i: (i, 0))
    o_shape = jax.ShapeDtypeStruct((t, d), BF16)
    return pl.pallas_call(
        body, name=name, grid=(t // tm,),
        in_specs=[pl.BlockSpec((tm, oa.shape[1]), lambda i: (i, 0)), pl.BlockSpec((tm, ob.shape[1]), lambda i: (i, 0)),
                  pl.BlockSpec((tm, d), lambda i: (i, P_GA // d)), pl.BlockSpec((tm, d), lambda i: (i, P_GB // d)),
                  pl.BlockSpec(wa_t.shape, lambda i: (0, 0)), pl.BlockSpec(wb_t.shape, lambda i: (0, 0))],
        out_specs=[o_spec] * 3, out_shape=[o_shape] * 3,
        compiler_params=_cp("parallel"))(oa, ob, proj, proj, wa_t, wb_t)


def _merge_bwd(dxb, wo, proj, ya, yb, name):
    t, d = dxb.shape
    tm = _tile(t, (512, 256, 128))

    def body(dx_ref, wo_ref, ga_ref, gb_ref, ya_ref, yb_ref, dya_ref, dyb_ref, dga_ref, dgb_ref):
        dxv = dx_ref[...]
        for c0, cw in _chunks(d, 512):
            cs = slice(c0, c0 + cw)
            dm = _dot(dxv, wo_ref[cs, :], _NT)
            sa = _sigmoid(ga_ref[:, cs])
            sb = _sigmoid(gb_ref[:, cs])
            dya_ref[:, cs] = (dm * sa).astype(BF16)
            dyb_ref[:, cs] = (dm * sb).astype(BF16)
            dga_ref[:, cs] = (dm * ya_ref[:, cs].astype(F32) * (sa * (1.0 - sa))).astype(BF16)
            dgb_ref[:, cs] = (dm * yb_ref[:, cs].astype(F32) * (sb * (1.0 - sb))).astype(BF16)

    o_spec = pl.BlockSpec((tm, d), lambda i: (i, 0))
    o_shape = jax.ShapeDtypeStruct((t, d), BF16)
    return pl.pallas_call(
        body, name=name, grid=(t // tm,),
        in_specs=[o_spec, pl.BlockSpec((d, d), lambda i: (0, 0)),
                  pl.BlockSpec((tm, d), lambda i: (i, P_GA // d)), pl.BlockSpec((tm, d), lambda i: (i, P_GB // d)),
                  o_spec, o_spec],
        out_specs=[o_spec] * 4, out_shape=[o_shape] * 4,
        compiler_params=_cp("parallel"))(dxb, wo, proj, proj, ya, yb)


def _swa_common(i, pq_ref, pkp_ref, pkc_ref):
    pk = jnp.concatenate([pkp_ref[0], pkc_ref[0]], axis=1)
    dist = (pq_ref[...] - pk).astype(F32)
    row = lax.broadcasted_iota(jnp.int32, (BLOCK, 2 * BLOCK), 0)
    col = lax.broadcasted_iota(jnp.int32, (BLOCK, 2 * BLOCK), 1)
    diff = row + BLOCK - col
    valid = (diff >= 0) & (diff < BLOCK) & ((i > 0) | (col >= BLOCK))
    return jnp.concatenate([dist] * GROUP_A, axis=0), jnp.concatenate([valid] * GROUP_A, axis=0)


def _half_sum(x, lo):
    s_lo = jnp.sum(jnp.where(lo, x, 0.0), axis=1, keepdims=True)
    s_hi = jnp.sum(jnp.where(lo, 0.0, x), axis=1, keepdims=True)
    return jnp.where(lo, s_lo, s_hi)


def _norm2(x, gain2, lo):
    r = lax.rsqrt(_half_sum(x * x, lo) * (1.0 / HEAD_DIM_A) + EPS)
    xh = x * r
    return xh * gain2, xh, r


def _norm2_bwd(d, xh, r, gain2, lo):
    dxh = d * gain2
    return r * (dxh - xh * (_half_sum(dxh * xh, lo) * (1.0 / HEAD_DIM_A)))


def _swa_group(grp, qn, lo, sk_ref):
    mask = lo if grp == 0 else jnp.logical_not(lo)
    q4 = jnp.concatenate([jnp.where(mask, qn[j], 0.0) for j in range(GROUP_A)], axis=0).astype(BF16)
    heads = [grp * GROUP_A + j for j in range(GROUP_A)]
    slope = jnp.concatenate([jnp.full((BLOCK, 1), 2.0 ** (-(h + 1)), F32) for h in heads], axis=0)
    sink = jnp.concatenate([jnp.broadcast_to(sk_ref[:, h:h + 1], (BLOCK, 1)) for h in heads], axis=0)
    return mask, q4, slope, sink


def _swa_probs(q4, kk, dist4, valid4, slope, sink):
    s = _dot(q4, kk, _NT) * (HEAD_DIM_A ** -0.5) - slope * dist4
    s = jnp.where(valid4, s, NEG)
    m = jnp.maximum(jnp.max(s, axis=1, keepdims=True), sink)
    e = jnp.exp(s - m)
    es = jnp.exp(sink - m)
    inv = 1.0 / (jnp.sum(e, axis=1, keepdims=True) + es)
    return e * inv, es * inv


def _swa_specs(s_len):
    nb = s_len // BLOCK

    def rowblk(b, i):
        return b * nb + i

    def prevblk(b, i):
        return b * nb + jnp.maximum(i - 1, 0)

    q_spec = pl.BlockSpec((BLOCK, WIDTH_A), lambda b, i: (rowblk(b, i), P_QA // WIDTH_A))
    kc_spec = pl.BlockSpec((BLOCK, KV_A), lambda b, i: (rowblk(b, i), P_KA // KV_A))
    kp_spec = pl.BlockSpec((BLOCK, KV_A), lambda b, i: (prevblk(b, i), P_KA // KV_A))
    vc_spec = pl.BlockSpec((BLOCK, KV_A), lambda b, i: (rowblk(b, i), P_VA // KV_A))
    vp_spec = pl.BlockSpec((BLOCK, KV_A), lambda b, i: (prevblk(b, i), P_VA // KV_A))
    pq_spec = pl.BlockSpec((BLOCK, 1), lambda b, i: (rowblk(b, i), 0))
    pkc_spec = pl.BlockSpec((1, 1, BLOCK), lambda b, i: (rowblk(b, i), 0, 0))
    pkp_spec = pl.BlockSpec((1, 1, BLOCK), lambda b, i: (prevblk(b, i), 0, 0))
    return nb, rowblk, [q_spec, kc_spec, kp_spec, vc_spec, vp_spec, pq_spec, pkc_spec, pkp_spec]


def _swa_fwd(proj, pos_col, pos_row, qg2, kg2, sinks, n_batch, s_len, name):
    t = proj.shape[0]
    nb, rowblk, specs = _swa_specs(s_len)
    small = pl.BlockSpec((1, HEAD_PAD), lambda b, i: (0, 0))

    def body(q_ref, kc_ref, kp_ref, vc_ref, vp_ref, pq_ref, pkc_ref, pkp_ref, qg_ref, kg_ref, sk_ref, o_ref):
        dist4, valid4 = _swa_common(pl.program_id(1), pq_ref, pkp_ref, pkc_ref)
        lo = lax.broadcasted_iota(jnp.int32, (1, HEAD_PAD), 1) < HEAD_DIM_A
        kk = _norm2(jnp.concatenate([kp_ref[...], kc_ref[...]], axis=0), kg_ref[...], lo)[0].astype(BF16)
        vv = jnp.concatenate([vp_ref[...], vc_ref[...]], axis=0).astype(BF16)
        qn = [_norm2(q_ref[:, j * HEAD_PAD:(j + 1) * HEAD_PAD], qg_ref[...], lo)[0] for j in range(GROUP_A)]
        outs = []
        for grp in range(N_KV_HEADS_A):
            _, q4, slope, sink = _swa_group(grp, qn, lo, sk_ref)
            p, _ = _swa_probs(q4, kk, dist4, valid4, slope, sink)
            outs.append(_dot(p.astype(BF16), vv, _NN))
        for j in range(GROUP_A):
            rs = slice(j * BLOCK, (j + 1) * BLOCK)
            o_ref[:, j * HEAD_PAD:(j + 1) * HEAD_PAD] = jnp.where(lo, outs[0][rs], outs[1][rs]).astype(BF16)

    return pl.pallas_call(
        body, name=name, grid=(n_batch, nb), in_specs=specs + [small, small, small],
        out_specs=pl.BlockSpec((BLOCK, WIDTH_A), lambda b, i: (rowblk(b, i), 0)),
        out_shape=jax.ShapeDtypeStruct((t, WIDTH_A), BF16),
        compiler_params=_cp("parallel", "parallel"))(proj, proj, proj, proj, proj, pos_col, pos_row, pos_row,
                                                     qg2, kg2, sinks)


def _swa_bwd(proj, pos_col, pos_row, qg2, kg2, sinks, do, n_batch, s_len, name):
    t = proj.shape[0]
    nb, rowblk, specs = _swa_specs(s_len)
    small = pl.BlockSpec((1, HEAD_PAD), lambda b, i: (0, 0))
    scale = HEAD_DIM_A ** -0.5

    def body(q_ref, kc_ref, kp_ref, vc_ref, vp_ref, pq_ref, pkc_ref, pkp_ref, qg_ref, kg_ref, sk_ref, do_ref,
             dq_ref, dkc_ref, dkp_ref, dvc_ref, dvp_ref, dqg_ref, dsk_ref):
        b, i = pl.program_id(0), pl.program_id(1)

        @pl.when((b == 0) & (i == 0))
        def _():
            dqg_ref[...] = jnp.zeros_like(dqg_ref)
            dsk_ref[...] = jnp.zeros_like(dsk_ref)

        dist4, valid4 = _swa_common(i, pq_ref, pkp_ref, pkc_ref)
        lane = lax.broadcasted_iota(jnp.int32, (1, HEAD_PAD), 1)
        lo = lane < HEAD_DIM_A
        kk = _norm2(jnp.concatenate([kp_ref[...], kc_ref[...]], axis=0), kg_ref[...], lo)[0].astype(BF16)
        vv = jnp.concatenate([vp_ref[...], vc_ref[...]], axis=0).astype(BF16)
        qs = [_norm2(q_ref[:, j * HEAD_PAD:(j + 1) * HEAD_PAD], qg_ref[...], lo) for j in range(GROUP_A)]
        dos = [do_ref[:, j * HEAD_PAD:(j + 1) * HEAD_PAD] for j in range(GROUP_A)]
        dkk = jnp.zeros((2 * BLOCK, HEAD_PAD), F32)
        dvv = jnp.zeros((2 * BLOCK, HEAD_PAD), F32)
        dsk = jnp.zeros((1, HEAD_PAD), F32)
        dq4 = []
        for grp in range(N_KV_HEADS_A):
            mask, q4, slope, sink = _swa_group(grp, [q[0] for q in qs], lo, sk_ref)
            do4 = jnp.concatenate([jnp.where(mask, d, jnp.zeros_like(d)) for d in dos], axis=0)
            p, ps = _swa_probs(q4, kk, dist4, valid4, slope, sink)
            dp = _dot(do4, vv, _NT)
            delta = jnp.sum(p * dp, axis=1, keepdims=True)
            ds = (p * (dp - delta) * scale).astype(BF16)
            dsink = ps * delta
            for j in range(GROUP_A):
                dsk = dsk + jnp.where(lane == grp * GROUP_A + j, -jnp.sum(dsink[j * BLOCK:(j + 1) * BLOCK]), 0.0)
            dvv = dvv + _dot(p.astype(BF16), do4, _TN)
            dkk = dkk + _dot(ds, q4, _TN)
            dq4.append(_dot(ds, kk, _NN))
        dqg = jnp.zeros((1, HEAD_PAD), F32)
        for j in range(GROUP_A):
            rs = slice(j * BLOCK, (j + 1) * BLOCK)
            _, xh, r = qs[j]
            dqn = jnp.where(lo, dq4[0][rs], dq4[1][rs])
            dqg = dqg + jnp.sum(dqn * xh, axis=0, keepdims=True)
            dq_ref[:, j * HEAD_PAD:(j + 1) * HEAD_PAD] = _norm2_bwd(dqn, xh, r, qg_ref[...], lo).astype(BF16)
        dkp_ref[...] = dkk[:BLOCK]
        dkc_ref[...] = dkk[BLOCK:]
        dvp_ref[...] = dvv[:BLOCK]
        dvc_ref[...] = dvv[BLOCK:]
        dqg_ref[...] += dqg
        dsk_ref[...] += dsk

    kv_out = pl.BlockSpec((BLOCK, KV_A), lambda b, i: (rowblk(b, i), 0))
    kv_shape = jax.ShapeDtypeStruct((t, KV_A), F32)
    wide = pl.BlockSpec((BLOCK, WIDTH_A), lambda b, i: (rowblk(b, i), 0))
    return pl.pallas_call(
        body, name=name, grid=(n_batch, nb), in_specs=specs + [small, small, small, wide],
        out_specs=[wide, kv_out, kv_out, kv_out, kv_out, small, small],
        out_shape=[jax.ShapeDtypeStruct((t, WIDTH_A), BF16), kv_shape, kv_shape, kv_shape, kv_shape,
                   jax.ShapeDtypeStruct((1, HEAD_PAD), F32), jax.ShapeDtypeStruct((1, HEAD_PAD), F32)],
        compiler_params=_cp("arbitrary", "arbitrary"))(proj, proj, proj, proj, proj, pos_col, pos_row, pos_row,
                                                       qg2, kg2, sinks, do)


def _swa_kv_bwd(proj, kg2, dkc, dkp, dvc, dvp, n_batch, s_len, name):
    t = proj.shape[0]
    nb = s_len // BLOCK

    def rowblk(b, i):
        return b * nb + i

    def nextblk(b, i):
        return b * nb + jnp.minimum(i + 1, nb - 1)

    def body(k_ref, kg_ref, dkc_ref, dkp_ref, dvc_ref, dvp_ref, dk_ref, dv_ref, dkg_ref):
        b, i = pl.program_id(0), pl.program_id(1)

        @pl.when((b == 0) & (i == 0))
        def _():
            dkg_ref[...] = jnp.zeros_like(dkg_ref)

        lo = lax.broadcasted_iota(jnp.int32, (1, HEAD_PAD), 1) < HEAD_DIM_A
        has_next = (i < nb - 1).astype(F32)
        dkn = dkc_ref[...] + has_next * dkp_ref[...]
        dv_ref[...] = (dvc_ref[...] + has_next * dvp_ref[...]).astype(BF16)
        _, xh, r = _norm2(k_ref[...], kg_ref[...], lo)
        dkg_ref[...] += jnp.sum(dkn * xh, axis=0, keepdims=True)
        dk_ref[...] = _norm2_bwd(dkn, xh, r, kg_ref[...], lo).astype(BF16)

    cur = pl.BlockSpec((BLOCK, KV_A), lambda b, i: (rowblk(b, i), 0))
    nxt = pl.BlockSpec((BLOCK, KV_A), lambda b, i: (nextblk(b, i), 0))
    small = pl.BlockSpec((1, HEAD_PAD), lambda b, i: (0, 0))
    return pl.pallas_call(
        body, name=name, grid=(n_batch, nb),
        in_specs=[pl.BlockSpec((BLOCK, KV_A), lambda b, i: (rowblk(b, i), P_KA // KV_A)), small, cur, nxt, cur, nxt],
        out_specs=[cur, cur, small],
        out_shape=[jax.ShapeDtypeStruct((t, KV_A), BF16), jax.ShapeDtypeStruct((t, KV_A), BF16),
                   jax.ShapeDtypeStruct((1, HEAD_PAD), F32)],
        compiler_params=_cp("arbitrary", "arbitrary"))(proj, kg2, dkc, dkp, dvc, dvp)


def _rope(u, c, sm, sp):
    return u * c + pltpu.roll(u, HEAD_PAD - ROPE // 2, 1) * sm + pltpu.roll(u, ROPE // 2, 1) * sp


def _rope_t(d, c, sm, sp):
    return d * c + pltpu.roll(d * sm, ROPE // 2, 1) + pltpu.roll(d * sp, HEAD_PAD - ROPE // 2, 1)


def _pad_norm(x, gain):
    r = lax.rsqrt(jnp.sum(x * x, axis=1, keepdims=True) * (1.0 / QK_B) + EPS)
    xh = x * r
    return xh * gain, xh, r


def _pad_norm_bwd(d, xh, r, gain):
    dxh = d * gain
    return r * (dxh - xh * (jnp.sum(dxh * xh, axis=1, keepdims=True) * (1.0 / QK_B)))


def _lora_norm(x, gain):
    r = lax.rsqrt(jnp.mean(x * x, axis=1, keepdims=True) + EPS)
    xh = x * r
    return xh * gain, xh, r


def _mla_in_specs(tm):
    row = lambda w, off: pl.BlockSpec((tm, w), lambda i: (i, off // w))
    one = lambda w: pl.BlockSpec((1, w), lambda i: (0, 0))
    full = lambda r, c: pl.BlockSpec((r, c), lambda i: (0, 0))
    tab = pl.BlockSpec((tm, HEAD_PAD), lambda i: (i, 0))
    return [row(Q_LORA, P_CQ), row(KV_LORA, P_CKV), row(HEAD_PAD, P_KR), tab, tab, tab,
            one(Q_LORA), one(KV_LORA), one(HEAD_PAD), one(HEAD_PAD),
            full(WIDTH_BP, Q_LORA), full(WIDTH_BP, KV_LORA), full(WIDTH_BP, KV_LORA)]


def _mla_pre(proj, tabs, gq, gkv, gqn, gkn, wuq, wk, wv, name):
    t = proj.shape[0]
    tm = _tile(t, (256, 128))

    def body(cq_ref, ckv_ref, kr_ref, c_ref, sm_ref, sp_ref, gq_ref, gkv_ref, gqn_ref, gkn_ref,
             wuq_ref, wk_ref, wv_ref, q_ref, k_ref, v_ref):
        cqn = _lora_norm(cq_ref[...], gq_ref[...])[0].astype(BF16)
        ckvn = _lora_norm(ckv_ref[...], gkv_ref[...])[0].astype(BF16)
        q_raw = _dot(cqn, wuq_ref[...], _NT)
        k_raw = _dot(ckvn, wk_ref[...], _NT)
        v_ref[...] = _dot(ckvn, wv_ref[...], _NT).astype(BF16)
        kr = pltpu.roll(kr_ref[...], NOPE, 1)
        c, sm, sp = c_ref[...], sm_ref[...], sp_ref[...]
        for h in range(N_HEADS_B):
            hs = slice(h * HEAD_PAD, (h + 1) * HEAD_PAD)
            q_ref[:, hs] = _rope(_pad_norm(q_raw[:, hs], gqn_ref[...])[0], c, sm, sp).astype(BF16)
            k_ref[:, hs] = _rope(_pad_norm(k_raw[:, hs] + kr, gkn_ref[...])[0], c, sm, sp).astype(BF16)

    o_spec = pl.BlockSpec((tm, WIDTH_BP), lambda i: (i, 0))
    o_shape = jax.ShapeDtypeStruct((t, WIDTH_BP), BF16)
    return pl.pallas_call(
        body, name=name, grid=(t // tm,), in_specs=_mla_in_specs(tm), out_specs=[o_spec] * 3,
        out_shape=[o_shape] * 3, compiler_params=_cp("parallel"))(
            proj, proj, proj, *tabs, gq, gkv, gqn, gkn, wuq, wk, wv)


def _mla_pre_bwd(proj, tabs, gq, gkv, gqn, gkn, wuq, wk, wv, dq, dk, dv, name):
    t = proj.shape[0]
    tm = _tile(t, (256, 128))

    def body(cq_ref, ckv_ref, kr_ref, c_ref, sm_ref, sp_ref, gq_ref, gkv_ref, gqn_ref, gkn_ref,
             wuq_ref, wk_ref, wv_ref, dq_ref, dk_ref, dv_ref,
             dcq_ref, dckv_ref, dkr_ref, dwuq_ref, dwk_ref, dwv_ref, dgq_ref, dgkv_ref, dgqn_ref, dgkn_ref,
             dqraw_ref, dkraw_ref):
        @pl.when(pl.program_id(0) == 0)
        def _():
            for r in (dwuq_ref, dwk_ref, dwv_ref, dgq_ref, dgkv_ref, dgqn_ref, dgkn_ref):
                r[...] = jnp.zeros_like(r)

        cqn_f, cq_xh, cq_r = _lora_norm(cq_ref[...], gq_ref[...])
        ckvn_f, ckv_xh, ckv_r = _lora_norm(ckv_ref[...], gkv_ref[...])
        cqn, ckvn = cqn_f.astype(BF16), ckvn_f.astype(BF16)
        q_raw = _dot(cqn, wuq_ref[...], _NT)
        k_raw = _dot(ckvn, wk_ref[...], _NT)
        kr = pltpu.roll(kr_ref[...], NOPE, 1)
        c, sm, sp = c_ref[...], sm_ref[...], sp_ref[...]
        dgqn = jnp.zeros((1, HEAD_PAD), F32)
        dgkn = jnp.zeros((1, HEAD_PAD), F32)
        dkr = jnp.zeros((tm, HEAD_PAD), F32)
        for h in range(N_HEADS_B):
            hs = slice(h * HEAD_PAD, (h + 1) * HEAD_PAD)
            _, xh, r = _pad_norm(q_raw[:, hs], gqn_ref[...])
            dn = _rope_t(dq_ref[:, hs], c, sm, sp)
            dgqn = dgqn + jnp.sum(dn * xh, axis=0, keepdims=True)
            dqraw_ref[:, hs] = _pad_norm_bwd(dn, xh, r, gqn_ref[...]).astype(BF16)
            _, xh, r = _pad_norm(k_raw[:, hs] + kr, gkn_ref[...])
            dn = _rope_t(dk_ref[:, hs], c, sm, sp)
            dgkn = dgkn + jnp.sum(dn * xh, axis=0, keepdims=True)
            dkc = _pad_norm_bwd(dn, xh, r, gkn_ref[...])
            dkraw_ref[:, hs] = dkc.astype(BF16)
            dkr = dkr + dkc
        dgqn_ref[...] += dgqn
        dgkn_ref[...] += dgkn
        lane = lax.broadcasted_iota(jnp.int32, (tm, HEAD_PAD), 1)
        dkr_ref[...] = jnp.where(lane < ROPE, pltpu.roll(dkr, HEAD_PAD - NOPE, 1), 0.0).astype(BF16)
        dqraw = dqraw_ref[...]
        dkraw = dkraw_ref[...]
        dvb = dv_ref[...].astype(BF16)
        dwuq_ref[...] += _dot(dqraw, cqn, _TN)
        dwk_ref[...] += _dot(dkraw, ckvn, _TN)
        dwv_ref[...] += _dot(dvb, ckvn, _TN)
        dcqn = _dot(dqraw, wuq_ref[...], _NN)
        dckvn = _dot(dkraw, wk_ref[...], _NN) + _dot(dvb, wv_ref[...], _NN)
        dgq_ref[...] += jnp.sum(dcqn * cq_xh, axis=0, keepdims=True)
        dgkv_ref[...] += jnp.sum(dckvn * ckv_xh, axis=0, keepdims=True)
        dxh = dcqn * gq_ref[...]
        dcq_ref[...] = (cq_r * (dxh - cq_xh * jnp.mean(dxh * cq_xh, axis=1, keepdims=True))).astype(BF16)
        dxh = dckvn * gkv_ref[...]
        dckv_ref[...] = (ckv_r * (dxh - ckv_xh * jnp.mean(dxh * ckv_xh, axis=1, keepdims=True))).astype(BF16)

    wide = pl.BlockSpec((tm, WIDTH_BP), lambda i: (i, 0))
    row = lambda w: pl.BlockSpec((tm, w), lambda i: (i, 0))
    full = lambda r, c: pl.BlockSpec((r, c), lambda i: (0, 0))
    return pl.pallas_call(
        body, name=name, grid=(t // tm,), in_specs=_mla_in_specs(tm) + [wide, wide, wide],
        out_specs=[row(Q_LORA), row(KV_LORA), row(HEAD_PAD), full(WIDTH_BP, Q_LORA), full(WIDTH_BP, KV_LORA),
                   full(WIDTH_BP, KV_LORA), full(1, Q_LORA), full(1, KV_LORA), full(1, HEAD_PAD), full(1, HEAD_PAD)],
        out_shape=[jax.ShapeDtypeStruct((t, Q_LORA), BF16), jax.ShapeDtypeStruct((t, KV_LORA), BF16),
                   jax.ShapeDtypeStruct((t, HEAD_PAD), BF16), jax.ShapeDtypeStruct((WIDTH_BP, Q_LORA), F32),
                   jax.ShapeDtypeStruct((WIDTH_BP, KV_LORA), F32), jax.ShapeDtypeStruct((WIDTH_BP, KV_LORA), F32),
                   jax.ShapeDtypeStruct((1, Q_LORA), F32), jax.ShapeDtypeStruct((1, KV_LORA), F32),
                   jax.ShapeDtypeStruct((1, HEAD_PAD), F32), jax.ShapeDtypeStruct((1, HEAD_PAD), F32)],
        scratch_shapes=[pltpu.VMEM((tm, WIDTH_BP), BF16), pltpu.VMEM((tm, WIDTH_BP), BF16)],
        compiler_params=_cp("arbitrary"))(proj, proj, proj, *tabs, gq, gkv, gqn, gkn, wuq, wk, wv, dq, dk, dv)


def _mla_flash_specs(s_len):
    bh_spec = pl.BlockSpec((s_len, HEAD_PAD), lambda b, h: (b, h))
    lse_spec = pl.BlockSpec((1, s_len, 1), lambda b, h: (b * N_HEADS_B + h, 0, 0))
    return bh_spec, lse_spec


def _diag_mask(s):
    row = lax.broadcasted_iota(jnp.int32, s.shape, 0)
    col = lax.broadcasted_iota(jnp.int32, s.shape, 1)
    return jnp.where(row >= col, s, NEG)


def _mla_flash_fwd(q, k, v, n_batch, s_len, name):
    t = q.shape[0]
    tq = _tile(s_len, (256, 128))
    bh_spec, lse_spec = _mla_flash_specs(s_len)
    c = (QK_B ** -0.5) * LOG2E

    def body(q_ref, k_ref, v_ref, o_ref, lse_ref):
        for i in range(s_len // tq):
            rows, below = slice(i * tq, (i + 1) * tq), slice(0, i * tq)
            qv = q_ref[rows, :]
            sd = _diag_mask(_dot(qv, k_ref[rows, :], _NT))
            m = jnp.max(sd, axis=1, keepdims=True)
            if i:
                sb = _dot(qv, k_ref[below, :], _NT)
                m = jnp.maximum(m, jnp.max(sb, axis=1, keepdims=True))
            pd = jnp.exp2((sd - m) * c)
            l = jnp.sum(pd, axis=1, keepdims=True)
            acc = _dot(pd.astype(BF16), v_ref[rows, :], _NN)
            if i:
                pb = jnp.exp2((sb - m) * c)
                l = l + jnp.sum(pb, axis=1, keepdims=True)
                acc = acc + _dot(pb.astype(BF16), v_ref[below, :], _NN)
            o_ref[rows, :] = (acc * (1.0 / l)).astype(BF16)
            lse_ref[0, rows, :] = m * c + jnp.log2(l)

    return pl.pallas_call(
        body, name=name, grid=(n_batch, N_HEADS_B), in_specs=[bh_spec, bh_spec, bh_spec],
        out_specs=[bh_spec, lse_spec],
        out_shape=[jax.ShapeDtypeStruct((t, WIDTH_BP), BF16),
                   jax.ShapeDtypeStruct((n_batch * N_HEADS_B, s_len, 1), F32)],
        compiler_params=_cp("parallel", "parallel"))(q, k, v)


def _mla_flash_bwd(q, k, v, o, do, lse2, n_batch, s_len, name):
    t = q.shape[0]
    tq = _tile(s_len, (256, 128))
    bh_spec, lse_spec = _mla_flash_specs(s_len)
    scale = QK_B ** -0.5
    c = scale * LOG2E

    def body(q_ref, k_ref, v_ref, o_ref, do_ref, lse_ref, dq_ref, dk_ref, dv_ref):
        dk_ref[...] = jnp.zeros_like(dk_ref)
        dv_ref[...] = jnp.zeros_like(dv_ref)
        for i in range(s_len // tq):
            rows = slice(i * tq, (i + 1) * tq)
            qv, dov = q_ref[rows, :], do_ref[rows, :]
            lse = lse_ref[0, rows, :]
            delta = jnp.sum(dov.astype(F32) * o_ref[rows, :].astype(F32), axis=1, keepdims=True)

            def part(ks, diag):
                kblk = k_ref[ks, :]
                s = _dot(qv, kblk, _NT)
                if diag:
                    s = _diag_mask(s)
                p = jnp.exp2(s * c - lse)
                dp = _dot(dov, v_ref[ks, :], _NT)
                ds = (p * (dp - delta) * scale).astype(BF16)
                dk_ref[ks, :] += _dot(ds, qv, _TN)
                dv_ref[ks, :] += _dot(p.astype(BF16), dov, _TN)
                return _dot(ds, kblk, _NN)

            dq = part(rows, True)
            if i:
                dq = dq + part(slice(0, i * tq), False)
            dq_ref[rows, :] = dq

    f32_wide = jax.ShapeDtypeStruct((t, WIDTH_BP), F32)
    return pl.pallas_call(
        body, name=name, grid=(n_batch, N_HEADS_B),
        in_specs=[bh_spec, bh_spec, bh_spec, bh_spec, bh_spec, lse_spec],
        out_specs=[bh_spec, bh_spec, bh_spec], out_shape=[f32_wide] * 3,
        compiler_params=_cp("parallel", "parallel"))(q, k, v, o, do, lse2)


def _swa_heads(w, axis, order):
    heads = [lax.slice_in_dim(w, h * HEAD_DIM_A, (h + 1) * HEAD_DIM_A, axis=axis) for h in order]
    return jnp.concatenate(heads, axis=axis)


def _layer_mats(w):
    w_in = w["w_in"]
    o = [0]
    for n in (WIDTH_A, KV_A, KV_A, Q_LORA, KV_LORA, ROPE, D_MODEL, D_MODEL):
        o.append(o[-1] + n)
    qa, ka, va, cq, ckv, kr, ga, gb = (w_in[o[i]:o[i + 1]] for i in range(8))
    pad = jnp.zeros((PROJ_W - IN_WIDTH, w_in.shape[1]), w_in.dtype)
    w_in_p = jnp.concatenate([ga, gb, _swa_heads(qa, 0, SWA_HEAD_ORDER), cq, ka, va, ckv, kr, pad], axis=0)
    uq = w["mla_w_uq"].reshape(N_HEADS_B, QK_B, Q_LORA)
    uq = jnp.pad(uq, ((0, 0), (0, HEAD_PAD - QK_B), (0, 0))).reshape(WIDTH_BP, Q_LORA)
    ukv = w["mla_w_ukv"].reshape(N_HEADS_B, NOPE + V_B, KV_LORA)
    wk = jnp.pad(ukv[:, :NOPE], ((0, 0), (0, HEAD_PAD - NOPE), (0, 0))).reshape(WIDTH_BP, KV_LORA)
    wv = jnp.pad(ukv[:, NOPE:], ((0, 0), (0, HEAD_PAD - V_B), (0, 0))).reshape(WIDTH_BP, KV_LORA)
    wb = w["w_branch_b"].reshape(D_MODEL, N_HEADS_B, V_B)
    wb = jnp.pad(wb, ((0, 0), (0, 0), (0, HEAD_PAD - V_B))).reshape(D_MODEL, WIDTH_BP)
    out = dict(w)
    out.update(w_in=w_in_p, mla_w_uq=uq, wk=wk, wv=wv, w_branch_b=wb,
               w_branch_a=_swa_heads(w["w_branch_a"], 1, SWA_HEAD_ORDER))
    return out


def _unlayer_grads(g):
    d = g["w_in"]
    ga, gb, qa, cq, ka, va, ckv, kr = (d[a:b] for a, b in (
        (P_GA, P_GA + D_MODEL), (P_GB, P_GB + D_MODEL), (P_QA, P_QA + WIDTH_A), (P_CQ, P_CQ + Q_LORA),
        (P_KA, P_KA + KV_A), (P_VA, P_VA + KV_A), (P_CKV, P_CKV + KV_LORA), (P_KR, P_KR + ROPE)))
    out = {n: v for n, v in g.items() if n not in ("wk", "wv")}
    out["w_in"] = jnp.concatenate([_swa_heads(qa, 0, SWA_HEAD_INVERSE), ka, va, cq, ckv, kr, ga, gb], axis=0)
    out["mla_w_uq"] = g["mla_w_uq"].reshape(N_HEADS_B, HEAD_PAD, Q_LORA)[:, :QK_B].reshape(N_HEADS_B * QK_B, Q_LORA)
    dk = g["wk"].reshape(N_HEADS_B, HEAD_PAD, KV_LORA)[:, :NOPE]
    dv = g["wv"].reshape(N_HEADS_B, HEAD_PAD, KV_LORA)[:, :V_B]
    out["mla_w_ukv"] = jnp.concatenate([dk, dv], axis=1).reshape(N_HEADS_B * (NOPE + V_B), KV_LORA)
    out["w_branch_b"] = g["w_branch_b"].reshape(D_MODEL, N_HEADS_B, HEAD_PAD)[:, :, :V_B].reshape(D_MODEL, WIDTH_B)
    out["w_branch_a"] = _swa_heads(g["w_branch_a"], 1, SWA_HEAD_INVERSE)
    return out


def _pad_lanes(v, width):
    return jnp.pad(v.reshape(1, -1), ((0, 0), (0, width - v.shape[-1])))


def _rope_tables(positions):
    half = ROPE // 2
    inv_freq = ROPE_BASE ** (-jnp.arange(half, dtype=F32) / half)
    ang = positions.astype(F32).reshape(-1, 1) * inv_freq
    cos, sin = jnp.cos(ang), jnp.sin(ang)
    t = cos.shape[0]
    one, zero = jnp.ones((t, NOPE), F32), jnp.zeros((t, NOPE), F32)
    tail = jnp.zeros((t, HEAD_PAD - QK_B), F32)
    z16 = jnp.zeros((t, half), F32)
    c = jnp.concatenate([one, cos, cos, tail], axis=1)
    sm = jnp.concatenate([zero, -sin, z16, tail], axis=1)
    sp = jnp.concatenate([zero, z16, sin, tail], axis=1)
    return c, sm, sp


def _ffn_fwd(x, gain, wg_t, wu_t, wd, tag):
    n = _rms_fwd(x, gain, f"{tag}_norm")
    a, b, hmid = _ffn_up(n, wg_t, wu_t, f"{tag}_up")
    y = _mm([(hmid, wd)], "nn", F32, f"{tag}_down", residual=x, alpha=0.5)
    return y, (x, n, a, b, hmid)


def _ffn_bwd(dy, dyb, saved, gain, wg_t, wu_t, wd, tag):
    x, n, a, b, hmid = saved
    da, db = _ffn_down_bwd(dyb, wd, a, b, f"{tag}_down_bwd")
    g_wd = _mm([(hmid, dyb)], "tn", BF16, f"{tag}_dwd", alpha=0.5)
    g_wg = _mm([(da, n)], "tn", BF16, f"{tag}_dwg")
    g_wu = _mm([(db, n)], "tn", BF16, f"{tag}_dwu")
    dn = _mm([(da, wg_t), (db, wu_t)], "nn", F32, f"{tag}_dn")
    dx, dxb, g_gain = _rms_bwd(dn, x, gain, dy, f"{tag}_norm_bwd")
    return dx, dxb, g_wg, g_wu, g_wd, g_gain


def _fold_halves(d):
    return d[:, :HEAD_DIM_A] + d[:, HEAD_DIM_A:]


def _local_step(x, positions, target, layers, smalls, on_mid=None, on_done=None):
    n_batch, s_len, d = x.shape
    t = n_batch * s_len
    xt = x.reshape(t, d)
    tabs = _rope_tables(positions)
    pos_col = positions.reshape(t, 1)
    pos_row = positions.reshape(t // BLOCK, 1, BLOCK)
    saved = []
    for l in range(len(layers)):
        w, s = layers[l], smalls[l]
        g1, gm, g2 = (s[k].reshape(1, d) for k in ("ffn1_norm", "mix_norm", "ffn2_norm"))
        qg2, kg2 = (jnp.tile(s[k].reshape(1, -1), (1, 2)) for k in ("swa_q_norm", "swa_k_norm"))
        sinks = _pad_lanes(s["swa_sinks"], HEAD_PAD)
        gq, gkv = s["mla_q_lora_norm"].reshape(1, -1), s["mla_kv_lora_norm"].reshape(1, -1)
        gqn, gkn = _pad_lanes(s["mla_q_norm"], HEAD_PAD), _pad_lanes(s["mla_k_norm"], HEAD_PAD)
        x1, sv1 = _ffn_fwd(xt, g1, w["ffn1_w_gate"], w["ffn1_w_up"], w["ffn1_w_down"], f"l{l}_ffn1")
        h = _rms_fwd(x1, gm, f"l{l}_mix_norm")
        proj = _mm([(h, w["w_in"])], "nt", F32, f"l{l}_proj")
        oa = _swa_fwd(proj, pos_col, pos_row, qg2, kg2, sinks, n_batch, s_len, f"l{l}_swa")
        q, k, v = _mla_pre(proj, tabs, gq, gkv, gqn, gkn, w["mla_w_uq"], w["wk"], w["wv"], f"l{l}_mla_pre")
        ob, lse = _mla_flash_fwd(q, k, v, n_batch, s_len, f"l{l}_mla")
        merged, ya, yb = _merge_fwd(oa, ob, proj, w["w_branch_a"], w["w_branch_b"], f"l{l}_merge")
        x2 = _mm([(merged, w["w_out"])], "nn", F32, f"l{l}_out", residual=x1)
        x3, sv2 = _ffn_fwd(x2, g2, w["ffn2_w_gate"], w["ffn2_w_up"], w["ffn2_w_down"], f"l{l}_ffn2")
        saved.append((sv1, sv2, x1, h, proj, oa, q, k, v, ob, lse, merged, ya, yb,
                      (g1, gm, g2, qg2, kg2, sinks, gq, gkv, gqn, gkn)))
        xt = x3

    dy, dyb, loss = _loss_head(xt, target.reshape(t, d))

    big_grads, small_grads = [None] * len(layers), [None] * len(layers)
    for l in reversed(range(len(layers))):
        w = layers[l]
        sv1, sv2, x1, h, proj, oa, q, k, v, ob, lse, merged, ya, yb, gains = saved[l]
        g1, gm, g2, qg2, kg2, sinks, gq, gkv, gqn, gkn = gains
        bg, sg = {}, {}
        dy, dyb, bg["ffn2_w_gate"], bg["ffn2_w_up"], bg["ffn2_w_down"], sg["ffn2_norm"] = _ffn_bwd(
            dy, dyb, sv2, g2, w["ffn2_w_gate"], w["ffn2_w_up"], w["ffn2_w_down"], f"l{l}_ffn2")
        dya, dyb_, dga, dgb = _merge_bwd(dyb, w["w_out"], proj, ya, yb, f"l{l}_merge_bwd")
        if on_mid is not None:
            on_mid(l, sg["ffn2_norm"])
        bg["w_out"] = _mm([(merged, dyb)], "tn", BF16, f"l{l}_dwo")
        doa = _mm([(dya, w["w_branch_a"])], "nn", BF16, f"l{l}_doa")
        bg["w_branch_a"] = _mm([(dya, oa)], "tn", BF16, f"l{l}_dwa")
        dob = _mm([(dyb_, w["w_branch_b"])], "nn", BF16, f"l{l}_dob")
        bg["w_branch_b"] = _mm([(dyb_, ob)], "tn", BF16, f"l{l}_dwb")
        dqa, dkc, dkp, dvc, dvp, dqg, dsk = _swa_bwd(
            proj, pos_col, pos_row, qg2, kg2, sinks, doa, n_batch, s_len, f"l{l}_swa_bwd")
        sg["swa_q_norm"], sg["swa_sinks"] = _fold_halves(dqg), dsk[:, :N_HEADS_A]
        dka, dva, dkg = _swa_kv_bwd(proj, kg2, dkc, dkp, dvc, dvp, n_batch, s_len, f"l{l}_swa_kv_bwd")
        sg["swa_k_norm"] = _fold_halves(dkg)
        dq, dk, dv = _mla_flash_bwd(q, k, v, ob, dob, lse, n_batch, s_len, f"l{l}_mla_bwd")
        (dcq, dckv, dkr, g_uq, g_wk, g_wv, sg["mla_q_lora_norm"], sg["mla_kv_lora_norm"], dgqn, dgkn) = _mla_pre_bwd(
            proj, tabs, gq, gkv, gqn, gkn, w["mla_w_uq"], w["wk"], w["wv"], dq, dk, dv, f"l{l}_mla_pre_bwd")
        sg["mla_q_norm"], sg["mla_k_norm"] = dgqn[:, :QK_B], dgkn[:, :QK_B]
        bg["mla_w_uq"], bg["wk"], bg["wv"] = g_uq.astype(BF16), g_wk.astype(BF16), g_wv.astype(BF16)
        dproj = jnp.concatenate([dga, dgb, dqa, dcq, dka, dva, dckv, dkr], axis=1)
        bg["w_in"] = _mm([(dproj, h)], "tn", BF16, f"l{l}_dwin")
        dh = _mm([(dproj, w["w_in"])], "nn", F32, f"l{l}_dh")
        dy, dyb, sg["mix_norm"] = _rms_bwd(dh, x1, gm, dy, f"l{l}_mix_norm_bwd")
        dy, dyb, bg["ffn1_w_gate"], bg["ffn1_w_up"], bg["ffn1_w_down"], sg["ffn1_norm"] = _ffn_bwd(
            dy, dyb, sv1, g1, w["ffn1_w_gate"], w["ffn1_w_up"], w["ffn1_w_down"], f"l{l}_ffn1")
        big_grads[l], small_grads[l] = bg, sg
        if on_done is not None:
            on_done(l, bg)
    return loss, dy.reshape(n_batch, s_len, d), big_grads, small_grads


def _round_up(n, m):
    return (n + m - 1) // m * m


def _flat_rows(shape, transposed):
    rows, k = (shape[1], shape[0]) if transposed else shape
    return _round_up(rows * k // LANES, 16), rows, k


def _flat_layout(shard_shapes):
    table, off = [], 0
    for name, tr in BIG:
        pr, rows, k = _flat_rows(shard_shapes[name], tr)
        table.append((name, tr, off, pr, rows, k))
        off += pr
    return table, _round_up(off, FLAT_ROW_TILE)


def _pack_flat(params, l, table, total):
    parts, off = [], 0
    for name, tr, o, pr, rows, k in table:
        w = params[name][l]
        w = (w.T if tr else w).reshape(rows * k // LANES, LANES)
        parts.append(jnp.pad(w, ((0, pr - w.shape[0]), (0, 0))))
        off = o + pr
    if total > off:
        parts.append(jnp.zeros((total - off, LANES), parts[0].dtype))
    return jnp.concatenate(parts, axis=0)


def _unpack_flat(flat, table):
    out = {}
    for name, tr, o, pr, rows, k in table:
        w = flat[o:o + rows * k // LANES].reshape(rows, k)
        out[name] = w.T if tr else w
    return out


def _gathered_mats(gathered, table):
    return {name: gathered[:, o:o + rows * k // LANES].reshape(N_DEV * rows, k)
            for name, tr, o, pr, rows, k in table}


def _pack_grads(grads, table, total):
    parts, off = [], 0
    for name, tr, o, pr, rows, k in table:
        g = grads[name].reshape(N_DEV, rows * k // LANES, LANES)
        parts.append(jnp.pad(g, ((0, 0), (0, pr - g.shape[1]), (0, 0))))
        off = o + pr
    if total > off:
        parts.append(jnp.zeros((N_DEV, total - off, LANES), BF16))
    return jnp.concatenate(parts, axis=1)


def _pack_small(params):
    parts = [params[n][l].reshape(-1).astype(F32) for l in range(DEPTH) for n in SMALL]
    v = jnp.concatenate(parts)
    return jnp.pad(v, (0, SMALL_ROWS * LANES - v.shape[0])).reshape(SMALL_ROWS, LANES)


def _unpack_small(flat, shapes):
    v, out, off = flat.reshape(-1), {}, 0
    for l in range(DEPTH):
        for n in SMALL:
            sz = math.prod(shapes[n][1:])
            out.setdefault(n, []).append(v[off:off + sz].reshape(shapes[n][1:]))
            off += sz
    return {n: jnp.stack(p) for n, p in out.items()}


_ANY = pl.BlockSpec(memory_space=pl.ANY)
_MESH = pl.DeviceIdType.MESH


def _place():
    return lax.axis_index("x"), lax.axis_index("y"), lax.axis_index("c")


def _handshake(peers):
    barrier = pltpu.get_barrier_semaphore()
    for peer in peers:
        pl.semaphore_signal(barrier, inc=1, device_id=peer, device_id_type=_MESH)
    pl.semaphore_wait(barrier, len(peers))


def _comm_call(body, out_shape, scratch, name, seq_id, spec=_ANY):
    if seq_id is None:
        return pl.pallas_call(body, name=name, out_shape=out_shape, in_specs=[spec], out_specs=spec,
                              scratch_shapes=scratch)
    return pl.kernel(body, out_type=out_shape, mesh=plsc.ScalarSubcoreMesh(axis_name="sequencer", num_cores=1),
                     scratch_types=scratch, name=name, compiler_params=pltpu.CompilerParams(collective_id=seq_id))


def _all_gather(x_shard, name, vmem=False, seq_id=None):
    spec = pl.BlockSpec(memory_space=pltpu.VMEM) if vmem else _ANY

    def body(x_ref, out_ref, send_sems, recv_sems, local_sem):
        x, y, c = _place()
        me, sibling = (x, y, c), (x, y, 1 - c)
        chips = [(1 - x, y), (x, 1 - y), (1 - x, 1 - y)]
        if seq_id is not None:
            _handshake([sibling] + [(*chip, c) for chip in chips])

        def rows(px, py, pc):
            return out_ref.at[4 * px + 2 * py + pc]

        def copy(k, block, to, src=None):
            return pltpu.make_async_remote_copy(
                src_ref=rows(*block) if src is None else src, dst_ref=rows(*block),
                send_sem=send_sems.at[k], recv_sem=recv_sems.at[k], device_id=to, device_id_type=_MESH)

        mine = pltpu.make_async_copy(x_ref, rows(*me), local_sem)
        mine.start()
        first = [copy(0, me, sibling, src=x_ref)]
        first += [copy(1 + j, me, (*chip, c), src=x_ref) for j, chip in enumerate(chips)]
        for cp in first:
            cp.start()
        passed = [copy(4 + j, (*chip, c), sibling) for j, chip in enumerate(chips)]
        for j, chip in enumerate(chips):
            copy(1 + j, (*chip, c), me).wait_recv()
            passed[j].start()
        copy(0, sibling, me).wait_recv()
        for j, chip in enumerate(chips):
            copy(4 + j, (*chip, 1 - c), me).wait_recv()
        for cp in first + passed:
            cp.wait_send()
        mine.wait()

    return _comm_call(
        body, jax.ShapeDtypeStruct((N_DEV,) + x_shard.shape, x_shard.dtype),
        [pltpu.SemaphoreType.DMA((7,)), pltpu.SemaphoreType.DMA((7,)), pltpu.SemaphoreType.DMA], name, seq_id,
        spec)(x_shard)


def _exchange_cores(g4, name, seq_id=None):
    n_chip, _, r, w = g4.shape

    def body(g_ref, out_ref, send_sems, recv_sems):
        x, y, c = _place()
        if seq_id is not None:
            _handshake([(x, y, 1 - c)])
        copies = [pltpu.make_async_remote_copy(
            src_ref=g_ref.at[q, 1 - c], dst_ref=out_ref.at[q], send_sem=send_sems.at[q], recv_sem=recv_sems.at[q],
            device_id=(x, y, 1 - c), device_id_type=_MESH) for q in range(n_chip)]
        for cp in copies:
            cp.start()
        for cp in copies:
            cp.wait()

    return _comm_call(
        body, jax.ShapeDtypeStruct((n_chip, r, w), g4.dtype),
        [pltpu.SemaphoreType.DMA((n_chip,)), pltpu.SemaphoreType.DMA((n_chip,))], name, seq_id)(g4)


def _exchange_chips(s1, name, seq_id=None):
    _, r, w = s1.shape

    def body(s_ref, out_ref, send_sems, recv_sems):
        x, y, c = _place()
        chips = [(1 - x, y), (x, 1 - y), (1 - x, 1 - y)]
        if seq_id is not None:
            _handshake([(*chip, c) for chip in chips])
        copies = []
        for k, (tx, ty) in enumerate(chips):
            copies.append(pltpu.make_async_remote_copy(
                src_ref=s_ref.at[2 * tx + ty], dst_ref=out_ref.at[k], send_sem=send_sems.at[k],
                recv_sem=recv_sems.at[k], device_id=(tx, ty, c), device_id_type=_MESH))
        for cp in copies:
            cp.start()
        for cp in copies:
            cp.wait()

    return _comm_call(
        body, jax.ShapeDtypeStruct((3, r, w), s1.dtype),
        [pltpu.SemaphoreType.DMA((3,)), pltpu.SemaphoreType.DMA((3,))], name, seq_id)(s1)


def _chip_sum(g4, recv, core, after, name):
    n_chip, _, r, w = g4.shape
    tr = FLAT_ROW_TILE

    def body(c_ref, a_ref, b_ref, after_ref, o_ref):
        o_ref[...] = (a_ref[...].astype(F32) + b_ref[...].astype(F32)).astype(o_ref.dtype)

    grid_spec = pltpu.PrefetchScalarGridSpec(
        num_scalar_prefetch=1, grid=(n_chip, r // tr),
        in_specs=[pl.BlockSpec((None, None, tr, w), lambda q, i, c: (q, c[0], i, 0)),
                  pl.BlockSpec((None, tr, w), lambda q, i, c: (q, i, 0)), _ANY],
        out_specs=pl.BlockSpec((None, tr, w), lambda q, i, c: (q, i, 0)))
    return pl.pallas_call(
        body, name=name, grid_spec=grid_spec, out_shape=jax.ShapeDtypeStruct((n_chip, r, w), g4.dtype),
        compiler_params=_cp("parallel", "parallel"))(core, g4, recv, after)


def _adam(w, g, m, v):
    m = ADAM_B1 * m + (1.0 - ADAM_B1) * g
    v = ADAM_B2 * v + (1.0 - ADAM_B2) * (g * g)
    m_hat = m / (1.0 - ADAM_B1 ** ADAM_STEP)
    v_hat = v / (1.0 - ADAM_B2 ** ADAM_STEP)
    delta = -ADAM_LR * (m_hat / (jnp.sqrt(v_hat) + ADAM_EPS) + ADAM_WD * w)
    return delta, m, v


def _adam_big(s1, r2, chip, w, m, v, name):
    r, lanes = w.shape
    tr = FLAT_ROW_TILE

    def body(c_ref, s_ref, r0_ref, r1_ref, r2_ref, w_ref, m_ref, v_ref, g_out, d_out, m_out, v_out):
        g = ((s_ref[...].astype(F32) + r0_ref[...].astype(F32)) + r1_ref[...].astype(F32)) + r2_ref[...].astype(F32)
        d, mn, vn = _adam(w_ref[...], g, m_ref[...], v_ref[...])
        g_out[...] = g
        d_out[...] = d
        m_out[...] = mn
        v_out[...] = vn

    row = pl.BlockSpec((tr, lanes), lambda i, c: (i, 0))
    rel = lambda k: pl.BlockSpec((None, tr, lanes), lambda i, c: (k, i, 0))
    grid_spec = pltpu.PrefetchScalarGridSpec(
        num_scalar_prefetch=1, grid=(r // tr,),
        in_specs=[pl.BlockSpec((None, tr, lanes), lambda i, c: (c[0], i, 0)), rel(0), rel(1), rel(2), row, row, row],
        out_specs=[row] * 4)
    return pl.pallas_call(
        body, name=name, grid_spec=grid_spec, out_shape=[jax.ShapeDtypeStruct((r, lanes), F32)] * 4,
        compiler_params=_cp("parallel"))(chip, s1, r2, r2, r2, w, m, v)


def _adam_small(parts, w, m, v, name):
    rows, lanes = w.shape

    def body(p_ref, w_ref, m_ref, v_ref, g_out, d_out, m_out, v_out):
        g = p_ref[0:rows, :]
        for dev in range(1, N_DEV):
            g = g + p_ref[dev * rows:(dev + 1) * rows, :]
        d, mn, vn = _adam(w_ref[...], g, m_ref[...], v_ref[...])
        g_out[...] = g
        d_out[...] = d
        m_out[...] = mn
        v_out[...] = vn

    return pl.pallas_call(
        body, name=name, out_shape=[jax.ShapeDtypeStruct((rows, lanes), F32)] * 4)(parts, w, m, v)


def kernel(x, positions, ffn1_norm, ffn1_w_gate, ffn1_w_up, ffn1_w_down, mix_norm, w_in, swa_q_norm, swa_k_norm, swa_sinks, mla_q_lora_norm, mla_w_uq, mla_kv_lora_norm, mla_w_ukv, mla_q_norm, mla_k_norm, w_branch_a, w_branch_b, w_out, ffn2_norm, ffn2_w_gate, ffn2_w_up, ffn2_w_down, loss_target, m_ffn1_norm, m_ffn1_w_gate, m_ffn1_w_up, m_ffn1_w_down, m_mix_norm, m_w_in, m_swa_q_norm, m_swa_k_norm, m_swa_sinks, m_mla_q_lora_norm, m_mla_w_uq, m_mla_kv_lora_norm, m_mla_w_ukv, m_mla_q_norm, m_mla_k_norm, m_w_branch_a, m_w_branch_b, m_w_out, m_ffn2_norm, m_ffn2_w_gate, m_ffn2_w_up, m_ffn2_w_down, v_ffn1_norm, v_ffn1_w_gate, v_ffn1_w_up, v_ffn1_w_down, v_mix_norm, v_w_in, v_swa_q_norm, v_swa_k_norm, v_swa_sinks, v_mla_q_lora_norm, v_mla_w_uq, v_mla_kv_lora_norm, v_mla_w_ukv, v_mla_q_norm, v_mla_k_norm, v_w_branch_a, v_w_branch_b, v_w_out, v_ffn2_norm, v_ffn2_w_gate, v_ffn2_w_up, v_ffn2_w_down):
    given = dict(locals())
    params = {n: given[n] for n in WEIGHTS}
    mom1 = {n: given["m_" + n] for n in WEIGHTS}
    mom2 = {n: given["v_" + n] for n in WEIGHTS}
    table, total = _flat_layout({n: params[n].shape[1:] for n, _ in BIG})

    cx, cy, cc = _place()
    core = jnp.reshape(cc, (1,)).astype(jnp.int32)
    chip = jnp.reshape(2 * cx + cy, (1,)).astype(jnp.int32)

    last = DEPTH - 1
    w_flat = [_pack_flat(params, l, table, total) for l in range(DEPTH)]
    gathered = [_all_gather(w_flat[l].astype(BF16), f"gather_l{l}", seq_id=SEQ_GATHER if l == last else None)
                for l in range(DEPTH)]
    layers = [_layer_mats(_gathered_mats(g, table)) for g in gathered]
    smalls = [{n: params[n][l] for n in SMALL} for l in range(DEPTH)]

    pending, big_out = {}, [None] * DEPTH

    def exchange_cores(l, grads):
        g_flat = _pack_grads(_unlayer_grads(grads), table, total)
        g4 = g_flat.reshape(N_DEV // 2, 2, total, LANES)
        pending[l] = (g4, _exchange_cores(g4, f"scatter_cores_l{l}", seq_id=SEQ_CORES if l == last else None))

    def exchange_chips(l, after):
        g4, from_core = pending.pop(l)
        s1 = _chip_sum(g4, from_core, core, after, f"sum_cores_l{l}")
        pending[l] = (s1, _exchange_chips(s1, f"scatter_chips_l{l}", seq_id=SEQ_CHIPS if l == last else None))

    def update(l):
        s1, from_chips = pending.pop(l)
        big_out[l] = _adam_big(s1, from_chips, chip, w_flat[l], _pack_flat(mom1, l, table, total),
                               _pack_flat(mom2, l, table, total), f"adam_l{l}")

    def on_mid(l, token):
        if l + 1 in pending:
            exchange_chips(l + 1, token)

    def on_done(l, grads):
        if l + 1 in pending:
            update(l + 1)
        exchange_cores(l, grads)

    loss, grad_x, _, small_grads = _local_step(x, positions, loss_target, layers, smalls, on_mid, on_done)
    loss = lax.psum(loss[0, 0], ("x", "y", "c"))
    exchange_chips(0, grad_x)
    update(0)

    g_small = _pack_small({n: [small_grads[l][n] for l in range(DEPTH)] for n in SMALL})
    parts = _all_gather(g_small, "gather_small", vmem=True).reshape(N_DEV * SMALL_ROWS, LANES)
    small_out = _adam_small(parts, _pack_small(params), _pack_small(mom1), _pack_small(mom2), "adam_small")

    outs = []
    shapes = {n: params[n].shape for n in SMALL}
    for kind, small in enumerate(small_out):
        per_layer = [_unpack_flat(big_out[l][kind], table) for l in range(DEPTH)]
        tree = {n: jnp.stack([per_layer[l][n] for l in range(DEPTH)]) for n, _ in BIG}
        tree.update(_unpack_small(small, shapes))
        outs.append(tree)
    return (loss, grad_x, *[o[n] for o in outs for n in WEIGHTS])
```

```python
import math

import jax
import jax.numpy as jnp
from jax import lax
from jax.experimental import pallas as pl
from jax.experimental.pallas import tpu as pltpu
from jax.experimental.pallas import tpu_sc as plsc

F32 = jnp.float32
BF16 = jnp.bfloat16

N_DEV = 8
DEPTH = 2
D_MODEL = 1024
D_FF = 2816
HEAD_DIM_A = 64
N_HEADS_A = 8
N_KV_HEADS_A = 2
GROUP_A = N_HEADS_A // N_KV_HEADS_A
BLOCK = 128
N_HEADS_B = 8
Q_LORA = 256
KV_LORA = 128
NOPE = 64
ROPE = 32
QK_B = NOPE + ROPE
V_B = 64
HEAD_PAD = 128
WIDTH_A = N_HEADS_A * HEAD_DIM_A
WIDTH_B = N_HEADS_B * V_B
WIDTH_BP = N_HEADS_B * HEAD_PAD
KV_A = N_KV_HEADS_A * HEAD_DIM_A
IN_WIDTH = WIDTH_A + 2 * KV_A + Q_LORA + KV_LORA + ROPE + 2 * D_MODEL
ROPE_BASE = 10000.0
EPS = 1e-6
NEG = -1e30
LOG2E = 1.4426950408889634

P_GA, P_GB, P_QA, P_CQ, P_KA, P_VA, P_CKV, P_KR = 0, 1024, 2048, 2560, 2816, 2944, 3072, 3200
PROJ_W = 3328
SWA_HEAD_ORDER = (0, 4, 1, 5, 2, 6, 3, 7)
SWA_HEAD_INVERSE = tuple(SWA_HEAD_ORDER.index(h) for h in range(N_HEADS_A))

ADAM_LR, ADAM_B1, ADAM_B2, ADAM_EPS, ADAM_WD, ADAM_STEP = 0.001, 0.9, 0.999, 1e-08, 0.01, 10

VMEM_LIMIT = 56 * 1024 * 1024
LANES = 1024

BIG = (("ffn1_w_gate", True), ("ffn1_w_up", True), ("ffn1_w_down", False), ("w_in", True), ("mla_w_uq", True),
       ("mla_w_ukv", True), ("w_branch_a", True), ("w_branch_b", True), ("w_out", False),
       ("ffn2_w_gate", True), ("ffn2_w_up", True), ("ffn2_w_down", False))
SMALL = ("ffn1_norm", "mix_norm", "ffn2_norm", "swa_q_norm", "swa_k_norm", "swa_sinks", "mla_q_lora_norm",
         "mla_kv_lora_norm", "mla_q_norm", "mla_k_norm")
WEIGHTS = ("ffn1_norm", "ffn1_w_gate", "ffn1_w_up", "ffn1_w_down", "mix_norm", "w_in", "swa_q_norm", "swa_k_norm",
           "swa_sinks", "mla_q_lora_norm", "mla_w_uq", "mla_kv_lora_norm", "mla_w_ukv", "mla_q_norm", "mla_k_norm",
           "w_branch_a", "w_branch_b", "w_out", "ffn2_norm", "ffn2_w_gate", "ffn2_w_up", "ffn2_w_down")
SMALL_ROWS = 8
FFN1 = ("ffn1_w_gate", "ffn1_w_up", "ffn1_w_down")
FFN2 = ("ffn2_w_gate", "ffn2_w_up", "ffn2_w_down")
MIXER = tuple(n for n, _ in BIG if n not in FFN1 + FFN2)
SEGMENTS = (
    (tuple((0, n) for n in FFN1), 352),
    (tuple((0, n) for n in MIXER), 240),
    (tuple((0, n) for n in FFN2) + tuple((1, n) for n in FFN1), 352),
    (tuple((1, n) for n in MIXER + FFN2), 480),
)
LINK_COST_SCALE = 64
SEQ_IDS = {(kind, s): 1 + 3 * k + s - 1 for k, kind in enumerate(("gather", "cores", "chips")) for s in (1, 2, 3)}


def _cp(*sem):
    return pltpu.CompilerParams(dimension_semantics=sem, vmem_limit_bytes=VMEM_LIMIT)


def _tile(n, prefs):
    for t in prefs:
        if n % t == 0:
            return t
    return n


def _dot(a, b, dims):
    return lax.dot_general(a, b, (dims, ((), ())), preferred_element_type=F32)


_NT = ((1,), (1,))
_NN = ((1,), (0,))
_TN = ((0,), (0,))


_ANY = pl.BlockSpec(memory_space=pl.ANY)


class _Order:
    token = None


def _tc_call(body, *, in_specs, **kw):
    def run(*args):
        token, n = _Order.token, len(args)
        if token is None:
            out = pl.pallas_call(body, in_specs=in_specs, **kw)(*args)
        else:
            def chained(*refs):
                return body(*refs[:n], *refs[n + 1:])
            out = pl.pallas_call(chained, in_specs=list(in_specs) + [_ANY], **kw)(*args, token)
        _Order.token = jax.tree.leaves(out)[0]
        return out
    return run


def _sigmoid(x):
    return 0.5 * jnp.tanh(0.5 * x) + 0.5


def _chunks(n, width):
    return [(c, min(width, n - c)) for c in range(0, n, width)]


def _mm(pairs, mode, out_dtype, name, residual=None, alpha=1.0):
    for a, b in pairs:
        assert a.dtype == BF16 and b.dtype == BF16, (name, a.dtype, b.dtype)
    if mode == "tn":
        (a, b), = pairs
        return _mm_tokens(a, b, out_dtype, name, alpha)
    t = pairs[0][0].shape[0]
    n = pairs[0][1].shape[0] if mode == "nt" else pairs[0][1].shape[1]
    tm = _tile(t, (512, 256, 128))
    dims = _NT if mode == "nt" else _NN
    in_specs, args = [], []
    for a, w in pairs:
        in_specs.append(pl.BlockSpec((tm, a.shape[1]), lambda i: (i, 0)))
        in_specs.append(pl.BlockSpec(w.shape, lambda i: (0, 0)))
        args += [a, w]
    if residual is not None:
        in_specs.append(pl.BlockSpec((tm, n), lambda i: (i, 0)))
        args.append(residual)
    n_pairs = len(pairs)

    def body(*refs):
        o_ref = refs[-1]
        for c0, cw in _chunks(n, 512):
            acc = None
            for p in range(n_pairs):
                w_ref = refs[2 * p + 1]
                w = w_ref[c0:c0 + cw, :] if mode == "nt" else w_ref[:, c0:c0 + cw]
                d = _dot(refs[2 * p][...], w, dims)
                acc = d if acc is None else acc + d
            if alpha != 1.0:
                acc = acc * alpha
            if residual is not None:
                acc = refs[2 * n_pairs][:, c0:c0 + cw] + acc
            o_ref[:, c0:c0 + cw] = acc.astype(out_dtype)

    return _tc_call(
        body, name=name, grid=(t // tm,), in_specs=in_specs, out_specs=pl.BlockSpec((tm, n), lambda i: (i, 0)),
        out_shape=jax.ShapeDtypeStruct((t, n), out_dtype), compiler_params=_cp("parallel"))(*args)


def _mm_tokens(a, b, out_dtype, name, alpha):
    t, m = a.shape
    n = b.shape[1]
    tk = _tile(t, (512, 256, 128))

    def body(a_ref, b_ref, o_ref, acc_ref):
        k = pl.program_id(0)

        @pl.when(k == 0)
        def _():
            acc_ref[...] = jnp.zeros_like(acc_ref)

        for c0, cw in _chunks(m, 512):
            acc_ref[c0:c0 + cw, :] += _dot(a_ref[:, c0:c0 + cw], b_ref[...], _TN)

        @pl.when(k == pl.num_programs(0) - 1)
        def _():
            o_ref[...] = (acc_ref[...] * alpha).astype(out_dtype)

    return _tc_call(
        body, name=name, grid=(t // tk,),
        in_specs=[pl.BlockSpec((tk, m), lambda k: (k, 0)), pl.BlockSpec((tk, n), lambda k: (k, 0))],
        out_specs=pl.BlockSpec((m, n), lambda k: (0, 0)), out_shape=jax.ShapeDtypeStruct((m, n), out_dtype),
        scratch_shapes=[pltpu.VMEM((m, n), F32)], compiler_params=_cp("arbitrary"))(a, b)


def _rms_fwd(x, g, name):
    t, d = x.shape
    tm = _tile(t, (512, 256, 128))

    def body(x_ref, g_ref, o_ref):
        xv = x_ref[...]
        r = lax.rsqrt(jnp.mean(xv * xv, axis=1, keepdims=True) + EPS)
        o_ref[...] = (xv * r * g_ref[...]).astype(BF16)

    return _tc_call(
        body, name=name, grid=(t // tm,),
        in_specs=[pl.BlockSpec((tm, d), lambda i: (i, 0)), pl.BlockSpec((1, d), lambda i: (0, 0))],
        out_specs=pl.BlockSpec((tm, d), lambda i: (i, 0)),
        out_shape=jax.ShapeDtypeStruct((t, d), BF16), compiler_params=_cp("parallel"))(x, g)


def _rms_bwd(dn, x, g, res, name):
    t, d = x.shape
    tm = _tile(t, (512, 256, 128))

    def body(dn_ref, x_ref, g_ref, res_ref, dx_ref, dxb_ref, dg_ref):
        xv = x_ref[...]
        r = lax.rsqrt(jnp.mean(xv * xv, axis=1, keepdims=True) + EPS)
        xh = xv * r
        dnv = dn_ref[...]
        dxh = dnv * g_ref[...]
        dx = res_ref[...] + r * (dxh - xh * jnp.mean(dxh * xh, axis=1, keepdims=True))
        dx_ref[...] = dx
        dxb_ref[...] = dx.astype(BF16)

        @pl.when(pl.program_id(0) == 0)
        def _():
            dg_ref[...] = jnp.zeros_like(dg_ref)

        dg_ref[...] += jnp.sum(dnv * xh, axis=0, keepdims=True)

    row = pl.BlockSpec((tm, d), lambda i: (i, 0))
    one = pl.BlockSpec((1, d), lambda i: (0, 0))
    return _tc_call(
        body, name=name, grid=(t // tm,), in_specs=[row, row, one, row], out_specs=[row, row, one],
        out_shape=[jax.ShapeDtypeStruct((t, d), F32), jax.ShapeDtypeStruct((t, d), BF16),
                   jax.ShapeDtypeStruct((1, d), F32)], compiler_params=_cp("arbitrary"))(dn, x, g, res)


def _ffn_up(n, wg_t, wu_t, name):
    t, d = n.shape
    f = wg_t.shape[0]
    tm = _tile(t, (512, 256, 128))

    def body(n_ref, wg_ref, wu_ref, a_ref, b_ref, h_ref):
        nv = n_ref[...]
        for c0, cw in _chunks(f, 256):
            a = _dot(nv, wg_ref[c0:c0 + cw, :], _NT)
            b = _dot(nv, wu_ref[c0:c0 + cw, :], _NT)
            a_ref[:, c0:c0 + cw] = a.astype(BF16)
            b_ref[:, c0:c0 + cw] = b.astype(BF16)
            h_ref[:, c0:c0 + cw] = (a * _sigmoid(a) * b).astype(BF16)

    w_spec = pl.BlockSpec((f, d), lambda i: (0, 0))
    o_spec = pl.BlockSpec((tm, f), lambda i: (i, 0))
    o_shape = jax.ShapeDtypeStruct((t, f), BF16)
    return _tc_call(
        body, name=name, grid=(t // tm,), in_specs=[pl.BlockSpec((tm, d), lambda i: (i, 0)), w_spec, w_spec],
        out_specs=[o_spec] * 3, out_shape=[o_shape] * 3, compiler_params=_cp("parallel"))(n, wg_t, wu_t)


def _ffn_down_bwd(dxb, wd, a, b, name):
    t, d = dxb.shape
    f = wd.shape[0]
    tm = _tile(t, (512, 256, 128))

    def body(dx_ref, wd_ref, a_ref, b_ref, da_ref, db_ref):
        dxv = dx_ref[...]
        for c0, cw in _chunks(f, 256):
            dh = 0.5 * _dot(dxv, wd_ref[c0:c0 + cw, :], _NT)
            av = a_ref[:, c0:c0 + cw].astype(F32)
            bv = b_ref[:, c0:c0 + cw].astype(F32)
            sg = _sigmoid(av)
            da_ref[:, c0:c0 + cw] = (dh * bv * (sg * (1.0 + av * (1.0 - sg)))).astype(BF16)
            db_ref[:, c0:c0 + cw] = (dh * (av * sg)).astype(BF16)

    o_spec = pl.BlockSpec((tm, f), lambda i: (i, 0))
    o_shape = jax.ShapeDtypeStruct((t, f), BF16)
    return _tc_call(
        body, name=name, grid=(t // tm,),
        in_specs=[pl.BlockSpec((tm, d), lambda i: (i, 0)), pl.BlockSpec((f, d), lambda i: (0, 0)), o_spec, o_spec],
        out_specs=[o_spec] * 2, out_shape=[o_shape] * 2, compiler_params=_cp("parallel"))(dxb, wd, a, b)


def _loss_head(y, target):
    t, d = y.shape
    tm = _tile(t, (512, 256, 128))

    def body(y_ref, t_ref, dy_ref, dyb_ref, loss_ref, acc_ref):
        i = pl.program_id(0)
        e = y_ref[...] - t_ref[...]
        dy = e * (1.0 / d)
        dy_ref[...] = dy
        dyb_ref[...] = dy.astype(BF16)

        @pl.when(i == 0)
        def _():
            acc_ref[...] = jnp.zeros_like(acc_ref)

        acc_ref[...] += jnp.sum(e * e, axis=0, keepdims=True)

        @pl.when(i == pl.num_programs(0) - 1)
        def _():
            loss_ref[...] = jnp.sum(acc_ref[...], axis=1, keepdims=True) * (0.5 / d)

    row = pl.BlockSpec((tm, d), lambda i: (i, 0))
    return _tc_call(
        body, name="loss_head", grid=(t // tm,), in_specs=[row, row],
        out_specs=[row, row, pl.BlockSpec((1, 1), lambda i: (0, 0))],
        out_shape=[jax.ShapeDtypeStruct((t, d), F32), jax.ShapeDtypeStruct((t, d), BF16),
                   jax.ShapeDtypeStruct((1, 1), F32)],
        scratch_shapes=[pltpu.VMEM((1, d), F32)], compiler_params=_cp("arbitrary"))(y, target)


def _merge_fwd(oa, ob, proj, wa_t, wb_t, name):
    t = oa.shape[0]
    d = wa_t.shape[0]
    tm = _tile(t, (512, 256, 128))

    def body(oa_ref, ob_ref, ga_ref, gb_ref, wa_ref, wb_ref, mg_ref, ya_ref, yb_ref):
        oav, obv = oa_ref[...], ob_ref[...]
        for c0, cw in _chunks(d, 512):
            cs = slice(c0, c0 + cw)
            ya = _dot(oav, wa_ref[cs, :], _NT)
            yb = _dot(obv, wb_ref[cs, :], _NT)
            mg_ref[:, cs] = (_sigmoid(ga_ref[:, cs]) * ya + _sigmoid(gb_ref[:, cs]) * yb).astype(BF16)
            ya_ref[:, cs] = ya.astype(BF16)
            yb_ref[:, cs] = yb.astype(BF16)

    o_spec = pl.BlockSpec((tm, d), lambda i: (i, 0))
    o_shape = jax.ShapeDtypeStruct((t, d), BF16)
    return _tc_call(
        body, name=name, grid=(t // tm,),
        in_specs=[pl.BlockSpec((tm, oa.shape[1]), lambda i: (i, 0)), pl.BlockSpec((tm, ob.shape[1]), lambda i: (i, 0)),
                  pl.BlockSpec((tm, d), lambda i: (i, P_GA // d)), pl.BlockSpec((tm, d), lambda i: (i, P_GB // d)),
                  pl.BlockSpec(wa_t.shape, lambda i: (0, 0)), pl.BlockSpec(wb_t.shape, lambda i: (0, 0))],
        out_specs=[o_spec] * 3, out_shape=[o_shape] * 3,
        compiler_params=_cp("parallel"))(oa, ob, proj, proj, wa_t, wb_t)


def _merge_bwd(dxb, wo, proj, ya, yb, name):
    t, d = dxb.shape
    tm = _tile(t, (512, 256, 128))

    def body(dx_ref, wo_ref, ga_ref, gb_ref, ya_ref, yb_ref, dya_ref, dyb_ref, dga_ref, dgb_ref):
        dxv = dx_ref[...]
        for c0, cw in _chunks(d, 512):
            cs = slice(c0, c0 + cw)
            dm = _dot(dxv, wo_ref[cs, :], _NT)
            sa = _sigmoid(ga_ref[:, cs])
            sb = _sigmoid(gb_ref[:, cs])
            dya_ref[:, cs] = (dm * sa).astype(BF16)
            dyb_ref[:, cs] = (dm * sb).astype(BF16)
            dga_ref[:, cs] = (dm * ya_ref[:, cs].astype(F32) * (sa * (1.0 - sa))).astype(BF16)
            dgb_ref[:, cs] = (dm * yb_ref[:, cs].astype(F32) * (sb * (1.0 - sb))).astype(BF16)

    o_spec = pl.BlockSpec((tm, d), lambda i: (i, 0))
    o_shape = jax.ShapeDtypeStruct((t, d), BF16)
    return _tc_call(
        body, name=name, grid=(t // tm,),
        in_specs=[o_spec, pl.BlockSpec((d, d), lambda i: (0, 0)),
                  pl.BlockSpec((tm, d), lambda i: (i, P_GA // d)), pl.BlockSpec((tm, d), lambda i: (i, P_GB // d)),
                  o_spec, o_spec],
        out_specs=[o_spec] * 4, out_shape=[o_shape] * 4,
        compiler_params=_cp("parallel"))(dxb, wo, proj, proj, ya, yb)


def _swa_common(i, pq_ref, pkp_ref, pkc_ref):
    pk = jnp.concatenate([pkp_ref[0], pkc_ref[0]], axis=1)
    dist = (pq_ref[...] - pk).astype(F32)
    row = lax.broadcasted_iota(jnp.int32, (BLOCK, 2 * BLOCK), 0)
    col = lax.broadcasted_iota(jnp.int32, (BLOCK, 2 * BLOCK), 1)
    diff = row + BLOCK - col
    valid = (diff >= 0) & (diff < BLOCK) & ((i > 0) | (col >= BLOCK))
    return jnp.concatenate([dist] * GROUP_A, axis=0), jnp.concatenate([valid] * GROUP_A, axis=0)


def _half_sum(x, lo):
    s_lo = jnp.sum(jnp.where(lo, x, 0.0), axis=1, keepdims=True)
    s_hi = jnp.sum(jnp.where(lo, 0.0, x), axis=1, keepdims=True)
    return jnp.where(lo, s_lo, s_hi)


def _norm2(x, gain2, lo):
    r = lax.rsqrt(_half_sum(x * x, lo) * (1.0 / HEAD_DIM_A) + EPS)
    xh = x * r
    return xh * gain2, xh, r


def _norm2_bwd(d, xh, r, gain2, lo):
    dxh = d * gain2
    return r * (dxh - xh * (_half_sum(dxh * xh, lo) * (1.0 / HEAD_DIM_A)))


def _swa_group(grp, qn, lo, sk_ref):
    mask = lo if grp == 0 else jnp.logical_not(lo)
    q4 = jnp.concatenate([jnp.where(mask, qn[j], 0.0) for j in range(GROUP_A)], axis=0).astype(BF16)
    heads = [grp * GROUP_A + j for j in range(GROUP_A)]
    slope = jnp.concatenate([jnp.full((BLOCK, 1), 2.0 ** (-(h + 1)), F32) for h in heads], axis=0)
    sink = jnp.concatenate([jnp.broadcast_to(sk_ref[:, h:h + 1], (BLOCK, 1)) for h in heads], axis=0)
    return mask, q4, slope, sink


def _swa_probs(q4, kk, dist4, valid4, slope, sink):
    s = _dot(q4, kk, _NT) * (HEAD_DIM_A ** -0.5) - slope * dist4
    s = jnp.where(valid4, s, NEG)
    m = jnp.maximum(jnp.max(s, axis=1, keepdims=True), sink)
    e = jnp.exp(s - m)
    es = jnp.exp(sink - m)
    inv = 1.0 / (jnp.sum(e, axis=1, keepdims=True) + es)
    return e * inv, es * inv


def _swa_specs(s_len):
    nb = s_len // BLOCK

    def rowblk(b, i):
        return b * nb + i

    def prevblk(b, i):
        return b * nb + jnp.maximum(i - 1, 0)

    q_spec = pl.BlockSpec((BLOCK, WIDTH_A), lambda b, i: (rowblk(b, i), P_QA // WIDTH_A))
    kc_spec = pl.BlockSpec((BLOCK, KV_A), lambda b, i: (rowblk(b, i), P_KA // KV_A))
    kp_spec = pl.BlockSpec((BLOCK, KV_A), lambda b, i: (prevblk(b, i), P_KA // KV_A))
    vc_spec = pl.BlockSpec((BLOCK, KV_A), lambda b, i: (rowblk(b, i), P_VA // KV_A))
    vp_spec = pl.BlockSpec((BLOCK, KV_A), lambda b, i: (prevblk(b, i), P_VA // KV_A))
    pq_spec = pl.BlockSpec((BLOCK, 1), lambda b, i: (rowblk(b, i), 0))
    pkc_spec = pl.BlockSpec((1, 1, BLOCK), lambda b, i: (rowblk(b, i), 0, 0))
    pkp_spec = pl.BlockSpec((1, 1, BLOCK), lambda b, i: (prevblk(b, i), 0, 0))
    return nb, rowblk, [q_spec, kc_spec, kp_spec, vc_spec, vp_spec, pq_spec, pkc_spec, pkp_spec]


def _swa_fwd(proj, pos_col, pos_row, qg2, kg2, sinks, n_batch, s_len, name):
    t = proj.shape[0]
    nb, rowblk, specs = _swa_specs(s_len)
    small = pl.BlockSpec((1, HEAD_PAD), lambda b, i: (0, 0))

    def body(q_ref, kc_ref, kp_ref, vc_ref, vp_ref, pq_ref, pkc_ref, pkp_ref, qg_ref, kg_ref, sk_ref, o_ref):
        dist4, valid4 = _swa_common(pl.program_id(1), pq_ref, pkp_ref, pkc_ref)
        lo = lax.broadcasted_iota(jnp.int32, (1, HEAD_PAD), 1) < HEAD_DIM_A
        kk = _norm2(jnp.concatenate([kp_ref[...], kc_ref[...]], axis=0), kg_ref[...], lo)[0].astype(BF16)
        vv = jnp.concatenate([vp_ref[...], vc_ref[...]], axis=0).astype(BF16)
        qn = [_norm2(q_ref[:, j * HEAD_PAD:(j + 1) * HEAD_PAD], qg_ref[...], lo)[0] for j in range(GROUP_A)]
        outs = []
        for grp in range(N_KV_HEADS_A):
            _, q4, slope, sink = _swa_group(grp, qn, lo, sk_ref)
            p, _ = _swa_probs(q4, kk, dist4, valid4, slope, sink)
            outs.append(_dot(p.astype(BF16), vv, _NN))
        for j in range(GROUP_A):
            rs = slice(j * BLOCK, (j + 1) * BLOCK)
            o_ref[:, j * HEAD_PAD:(j + 1) * HEAD_PAD] = jnp.where(lo, outs[0][rs], outs[1][rs]).astype(BF16)

    return _tc_call(
        body, name=name, grid=(n_batch, nb), in_specs=specs + [small, small, small],
        out_specs=pl.BlockSpec((BLOCK, WIDTH_A), lambda b, i: (rowblk(b, i), 0)),
        out_shape=jax.ShapeDtypeStruct((t, WIDTH_A), BF16),
        compiler_params=_cp("parallel", "parallel"))(proj, proj, proj, proj, proj, pos_col, pos_row, pos_row,
                                                     qg2, kg2, sinks)


def _swa_bwd(proj, pos_col, pos_row, qg2, kg2, sinks, do, n_batch, s_len, name):
    t = proj.shape[0]
    nb, rowblk, specs = _swa_specs(s_len)
    small = pl.BlockSpec((1, HEAD_PAD), lambda b, i: (0, 0))
    scale = HEAD_DIM_A ** -0.5

    def body(q_ref, kc_ref, kp_ref, vc_ref, vp_ref, pq_ref, pkc_ref, pkp_ref, qg_ref, kg_ref, sk_ref, do_ref,
             dq_ref, dkc_ref, dkp_ref, dvc_ref, dvp_ref, dqg_ref, dsk_ref):
        b, i = pl.program_id(0), pl.program_id(1)

        @pl.when((b == 0) & (i == 0))
        def _():
            dqg_ref[...] = jnp.zeros_like(dqg_ref)
            dsk_ref[...] = jnp.zeros_like(dsk_ref)

        dist4, valid4 = _swa_common(i, pq_ref, pkp_ref, pkc_ref)
        lane = lax.broadcasted_iota(jnp.int32, (1, HEAD_PAD), 1)
        lo = lane < HEAD_DIM_A
        kk = _norm2(jnp.concatenate([kp_ref[...], kc_ref[...]], axis=0), kg_ref[...], lo)[0].astype(BF16)
        vv = jnp.concatenate([vp_ref[...], vc_ref[...]], axis=0).astype(BF16)
        qs = [_norm2(q_ref[:, j * HEAD_PAD:(j + 1) * HEAD_PAD], qg_ref[...], lo) for j in range(GROUP_A)]
        dos = [do_ref[:, j * HEAD_PAD:(j + 1) * HEAD_PAD] for j in range(GROUP_A)]
        dkk = jnp.zeros((2 * BLOCK, HEAD_PAD), F32)
        dvv = jnp.zeros((2 * BLOCK, HEAD_PAD), F32)
        dsk = jnp.zeros((1, HEAD_PAD), F32)
        dq4 = []
        for grp in range(N_KV_HEADS_A):
            mask, q4, slope, sink = _swa_group(grp, [q[0] for q in qs], lo, sk_ref)
            do4 = jnp.concatenate([jnp.where(mask, d, jnp.zeros_like(d)) for d in dos], axis=0)
            p, ps = _swa_probs(q4, kk, dist4, valid4, slope, sink)
            dp = _dot(do4, vv, _NT)
            delta = jnp.sum(p * dp, axis=1, keepdims=True)
            ds = (p * (dp - delta) * scale).astype(BF16)
            dsink = ps * delta
            for j in range(GROUP_A):
                dsk = dsk + jnp.where(lane == grp * GROUP_A + j, -jnp.sum(dsink[j * BLOCK:(j + 1) * BLOCK]), 0.0)
            dvv = dvv + _dot(p.astype(BF16), do4, _TN)
            dkk = dkk + _dot(ds, q4, _TN)
            dq4.append(_dot(ds, kk, _NN))
        dqg = jnp.zeros((1, HEAD_PAD), F32)
        for j in range(GROUP_A):
            rs = slice(j * BLOCK, (j + 1) * BLOCK)
            _, xh, r = qs[j]
            dqn = jnp.where(lo, dq4[0][rs], dq4[1][rs])
            dqg = dqg + jnp.sum(dqn * xh, axis=0, keepdims=True)
            dq_ref[:, j * HEAD_PAD:(j + 1) * HEAD_PAD] = _norm2_bwd(dqn, xh, r, qg_ref[...], lo).astype(BF16)
        dkp_ref[...] = dkk[:BLOCK]
        dkc_ref[...] = dkk[BLOCK:]
        dvp_ref[...] = dvv[:BLOCK]
        dvc_ref[...] = dvv[BLOCK:]
        dqg_ref[...] += dqg
        dsk_ref[...] += dsk

    kv_out = pl.BlockSpec((BLOCK, KV_A), lambda b, i: (rowblk(b, i), 0))
    kv_shape = jax.ShapeDtypeStruct((t, KV_A), F32)
    wide = pl.BlockSpec((BLOCK, WIDTH_A), lambda b, i: (rowblk(b, i), 0))
    return _tc_call(
        body, name=name, grid=(n_batch, nb), in_specs=specs + [small, small, small, wide],
        out_specs=[wide, kv_out, kv_out, kv_out, kv_out, small, small],
        out_shape=[jax.ShapeDtypeStruct((t, WIDTH_A), BF16), kv_shape, kv_shape, kv_shape, kv_shape,
                   jax.ShapeDtypeStruct((1, HEAD_PAD), F32), jax.ShapeDtypeStruct((1, HEAD_PAD), F32)],
        compiler_params=_cp("arbitrary", "arbitrary"))(proj, proj, proj, proj, proj, pos_col, pos_row, pos_row,
                                                       qg2, kg2, sinks, do)


def _swa_kv_bwd(proj, kg2, dkc, dkp, dvc, dvp, n_batch, s_len, name):
    t = proj.shape[0]
    nb = s_len // BLOCK

    def rowblk(b, i):
        return b * nb + i

    def nextblk(b, i):
        return b * nb + jnp.minimum(i + 1, nb - 1)

    def body(k_ref, kg_ref, dkc_ref, dkp_ref, dvc_ref, dvp_ref, dk_ref, dv_ref, dkg_ref):
        b, i = pl.program_id(0), pl.program_id(1)

        @pl.when((b == 0) & (i == 0))
        def _():
            dkg_ref[...] = jnp.zeros_like(dkg_ref)

        lo = lax.broadcasted_iota(jnp.int32, (1, HEAD_PAD), 1) < HEAD_DIM_A
        has_next = (i < nb - 1).astype(F32)
        dkn = dkc_ref[...] + has_next * dkp_ref[...]
        dv_ref[...] = (dvc_ref[...] + has_next * dvp_ref[...]).astype(BF16)
        _, xh, r = _norm2(k_ref[...], kg_ref[...], lo)
        dkg_ref[...] += jnp.sum(dkn * xh, axis=0, keepdims=True)
        dk_ref[...] = _norm2_bwd(dkn, xh, r, kg_ref[...], lo).astype(BF16)

    cur = pl.BlockSpec((BLOCK, KV_A), lambda b, i: (rowblk(b, i), 0))
    nxt = pl.BlockSpec((BLOCK, KV_A), lambda b, i: (nextblk(b, i), 0))
    small = pl.BlockSpec((1, HEAD_PAD), lambda b, i: (0, 0))
    return _tc_call(
        body, name=name, grid=(n_batch, nb),
        in_specs=[pl.BlockSpec((BLOCK, KV_A), lambda b, i: (rowblk(b, i), P_KA // KV_A)), small, cur, nxt, cur, nxt],
        out_specs=[cur, cur, small],
        out_shape=[jax.ShapeDtypeStruct((t, KV_A), BF16), jax.ShapeDtypeStruct((t, KV_A), BF16),
                   jax.ShapeDtypeStruct((1, HEAD_PAD), F32)],
        compiler_params=_cp("arbitrary", "arbitrary"))(proj, kg2, dkc, dkp, dvc, dvp)


def _rope(u, c, sm, sp):
    return u * c + pltpu.roll(u, HEAD_PAD - ROPE // 2, 1) * sm + pltpu.roll(u, ROPE // 2, 1) * sp


def _rope_t(d, c, sm, sp):
    return d * c + pltpu.roll(d * sm, ROPE // 2, 1) + pltpu.roll(d * sp, HEAD_PAD - ROPE // 2, 1)


def _pad_norm(x, gain):
    r = lax.rsqrt(jnp.sum(x * x, axis=1, keepdims=True) * (1.0 / QK_B) + EPS)
    xh = x * r
    return xh * gain, xh, r


def _pad_norm_bwd(d, xh, r, gain):
    dxh = d * gain
    return r * (dxh - xh * (jnp.sum(dxh * xh, axis=1, keepdims=True) * (1.0 / QK_B)))


def _lora_norm(x, gain):
    r = lax.rsqrt(jnp.mean(x * x, axis=1, keepdims=True) + EPS)
    xh = x * r
    return xh * gain, xh, r


def _mla_in_specs(tm):
    row = lambda w, off: pl.BlockSpec((tm, w), lambda i: (i, off // w))
    one = lambda w: pl.BlockSpec((1, w), lambda i: (0, 0))
    full = lambda r, c: pl.BlockSpec((r, c), lambda i: (0, 0))
    tab = pl.BlockSpec((tm, HEAD_PAD), lambda i: (i, 0))
    return [row(Q_LORA, P_CQ), row(KV_LORA, P_CKV), row(HEAD_PAD, P_KR), tab, tab, tab,
            one(Q_LORA), one(KV_LORA), one(HEAD_PAD), one(HEAD_PAD),
            full(WIDTH_BP, Q_LORA), full(WIDTH_BP, KV_LORA), full(WIDTH_BP, KV_LORA)]


def _mla_pre(proj, tabs, gq, gkv, gqn, gkn, wuq, wk, wv, name):
    t = proj.shape[0]
    tm = _tile(t, (256, 128))

    def body(cq_ref, ckv_ref, kr_ref, c_ref, sm_ref, sp_ref, gq_ref, gkv_ref, gqn_ref, gkn_ref,
             wuq_ref, wk_ref, wv_ref, q_ref, k_ref, v_ref):
        cqn = _lora_norm(cq_ref[...], gq_ref[...])[0].astype(BF16)
        ckvn = _lora_norm(ckv_ref[...], gkv_ref[...])[0].astype(BF16)
        q_raw = _dot(cqn, wuq_ref[...], _NT)
        k_raw = _dot(ckvn, wk_ref[...], _NT)
        v_ref[...] = _dot(ckvn, wv_ref[...], _NT).astype(BF16)
        kr = pltpu.roll(kr_ref[...], NOPE, 1)
        c, sm, sp = c_ref[...], sm_ref[...], sp_ref[...]
        for h in range(N_HEADS_B):
            hs = slice(h * HEAD_PAD, (h + 1) * HEAD_PAD)
            q_ref[:, hs] = _rope(_pad_norm(q_raw[:, hs], gqn_ref[...])[0], c, sm, sp).astype(BF16)
            k_ref[:, hs] = _rope(_pad_norm(k_raw[:, hs] + kr, gkn_ref[...])[0], c, sm, sp).astype(BF16)

    o_spec = pl.BlockSpec((tm, WIDTH_BP), lambda i: (i, 0))
    o_shape = jax.ShapeDtypeStruct((t, WIDTH_BP), BF16)
    return _tc_call(
        body, name=name, grid=(t // tm,), in_specs=_mla_in_specs(tm), out_specs=[o_spec] * 3,
        out_shape=[o_shape] * 3, compiler_params=_cp("parallel"))(
            proj, proj, proj, *tabs, gq, gkv, gqn, gkn, wuq, wk, wv)


def _mla_pre_bwd(proj, tabs, gq, gkv, gqn, gkn, wuq, wk, wv, dq, dk, dv, name):
    t = proj.shape[0]
    tm = _tile(t, (256, 128))

    def body(cq_ref, ckv_ref, kr_ref, c_ref, sm_ref, sp_ref, gq_ref, gkv_ref, gqn_ref, gkn_ref,
             wuq_ref, wk_ref, wv_ref, dq_ref, dk_ref, dv_ref,
             dcq_ref, dckv_ref, dkr_ref, dwuq_ref, dwk_ref, dwv_ref, dgq_ref, dgkv_ref, dgqn_ref, dgkn_ref,
             dqraw_ref, dkraw_ref):
        @pl.when(pl.program_id(0) == 0)
        def _():
            for r in (dwuq_ref, dwk_ref, dwv_ref, dgq_ref, dgkv_ref, dgqn_ref, dgkn_ref):
                r[...] = jnp.zeros_like(r)

        cqn_f, cq_xh, cq_r = _lora_norm(cq_ref[...], gq_ref[...])
        ckvn_f, ckv_xh, ckv_r = _lora_norm(ckv_ref[...], gkv_ref[...])
        cqn, ckvn = cqn_f.astype(BF16), ckvn_f.astype(BF16)
        q_raw = _dot(cqn, wuq_ref[...], _NT)
        k_raw = _dot(ckvn, wk_ref[...], _NT)
        kr = pltpu.roll(kr_ref[...], NOPE, 1)
        c, sm, sp = c_ref[...], sm_ref[...], sp_ref[...]
        dgqn = jnp.zeros((1, HEAD_PAD), F32)
        dgkn = jnp.zeros((1, HEAD_PAD), F32)
        dkr = jnp.zeros((tm, HEAD_PAD), F32)
        for h in range(N_HEADS_B):
            hs = slice(h * HEAD_PAD, (h + 1) * HEAD_PAD)
            _, xh, r = _pad_norm(q_raw[:, hs], gqn_ref[...])
            dn = _rope_t(dq_ref[:, hs], c, sm, sp)
            dgqn = dgqn + jnp.sum(dn * xh, axis=0, keepdims=True)
            dqraw_ref[:, hs] = _pad_norm_bwd(dn, xh, r, gqn_ref[...]).astype(BF16)
            _, xh, r = _pad_norm(k_raw[:, hs] + kr, gkn_ref[...])
            dn = _rope_t(dk_ref[:, hs], c, sm, sp)
            dgkn = dgkn + jnp.sum(dn * xh, axis=0, keepdims=True)
            dkc = _pad_norm_bwd(dn, xh, r, gkn_ref[...])
            dkraw_ref[:, hs] = dkc.astype(BF16)
            dkr = dkr + dkc
        dgqn_ref[...] += dgqn
        dgkn_ref[...] += dgkn
        lane = lax.broadcasted_iota(jnp.int32, (tm, HEAD_PAD), 1)
        dkr_ref[...] = jnp.where(lane < ROPE, pltpu.roll(dkr, HEAD_PAD - NOPE, 1), 0.0).astype(BF16)
        dqraw = dqraw_ref[...]
        dkraw = dkraw_ref[...]
        dvb = dv_ref[...].astype(BF16)
        dwuq_ref[...] += _dot(dqraw, cqn, _TN)
        dwk_ref[...] += _dot(dkraw, ckvn, _TN)
        dwv_ref[...] += _dot(dvb, ckvn, _TN)
        dcqn = _dot(dqraw, wuq_ref[...], _NN)
        dckvn = _dot(dkraw, wk_ref[...], _NN) + _dot(dvb, wv_ref[...], _NN)
        dgq_ref[...] += jnp.sum(dcqn * cq_xh, axis=0, keepdims=True)
        dgkv_ref[...] += jnp.sum(dckvn * ckv_xh, axis=0, keepdims=True)
        dxh = dcqn * gq_ref[...]
        dcq_ref[...] = (cq_r * (dxh - cq_xh * jnp.mean(dxh * cq_xh, axis=1, keepdims=True))).astype(BF16)
        dxh = dckvn * gkv_ref[...]
        dckv_ref[...] = (ckv_r * (dxh - ckv_xh * jnp.mean(dxh * ckv_xh, axis=1, keepdims=True))).astype(BF16)

    wide = pl.BlockSpec((tm, WIDTH_BP), lambda i: (i, 0))
    row = lambda w: pl.BlockSpec((tm, w), lambda i: (i, 0))
    full = lambda r, c: pl.BlockSpec((r, c), lambda i: (0, 0))
    return _tc_call(
        body, name=name, grid=(t // tm,), in_specs=_mla_in_specs(tm) + [wide, wide, wide],
        out_specs=[row(Q_LORA), row(KV_LORA), row(HEAD_PAD), full(WIDTH_BP, Q_LORA), full(WIDTH_BP, KV_LORA),
                   full(WIDTH_BP, KV_LORA), full(1, Q_LORA), full(1, KV_LORA), full(1, HEAD_PAD), full(1, HEAD_PAD)],
        out_shape=[jax.ShapeDtypeStruct((t, Q_LORA), BF16), jax.ShapeDtypeStruct((t, KV_LORA), BF16),
                   jax.ShapeDtypeStruct((t, HEAD_PAD), BF16), jax.ShapeDtypeStruct((WIDTH_BP, Q_LORA), F32),
                   jax.ShapeDtypeStruct((WIDTH_BP, KV_LORA), F32), jax.ShapeDtypeStruct((WIDTH_BP, KV_LORA), F32),
                   jax.ShapeDtypeStruct((1, Q_LORA), F32), jax.ShapeDtypeStruct((1, KV_LORA), F32),
                   jax.ShapeDtypeStruct((1, HEAD_PAD), F32), jax.ShapeDtypeStruct((1, HEAD_PAD), F32)],
        scratch_shapes=[pltpu.VMEM((tm, WIDTH_BP), BF16), pltpu.VMEM((tm, WIDTH_BP), BF16)],
        compiler_params=_cp("arbitrary"))(proj, proj, proj, *tabs, gq, gkv, gqn, gkn, wuq, wk, wv, dq, dk, dv)


def _mla_flash_specs(s_len):
    bh_spec = pl.BlockSpec((s_len, HEAD_PAD), lambda b, h: (b, h))
    lse_spec = pl.BlockSpec((1, s_len, 1), lambda b, h: (b * N_HEADS_B + h, 0, 0))
    return bh_spec, lse_spec


def _diag_mask(s):
    row = lax.broadcasted_iota(jnp.int32, s.shape, 0)
    col = lax.broadcasted_iota(jnp.int32, s.shape, 1)
    return jnp.where(row >= col, s, NEG)


def _mla_flash_fwd(q, k, v, n_batch, s_len, name):
    t = q.shape[0]
    tq = _tile(s_len, (256, 128))
    bh_spec, lse_spec = _mla_flash_specs(s_len)
    c = (QK_B ** -0.5) * LOG2E

    def body(q_ref, k_ref, v_ref, o_ref, lse_ref):
        for i in range(s_len // tq):
            rows, below = slice(i * tq, (i + 1) * tq), slice(0, i * tq)
            qv = q_ref[rows, :]
            sd = _diag_mask(_dot(qv, k_ref[rows, :], _NT))
            m = jnp.max(sd, axis=1, keepdims=True)
            if i:
                sb = _dot(qv, k_ref[below, :], _NT)
                m = jnp.maximum(m, jnp.max(sb, axis=1, keepdims=True))
            pd = jnp.exp2((sd - m) * c)
            l = jnp.sum(pd, axis=1, keepdims=True)
            acc = _dot(pd.astype(BF16), v_ref[rows, :], _NN)
            if i:
                pb = jnp.exp2((sb - m) * c)
                l = l + jnp.sum(pb, axis=1, keepdims=True)
                acc = acc + _dot(pb.astype(BF16), v_ref[below, :], _NN)
            o_ref[rows, :] = (acc * (1.0 / l)).astype(BF16)
            lse_ref[0, rows, :] = m * c + jnp.log2(l)

    return _tc_call(
        body, name=name, grid=(n_batch, N_HEADS_B), in_specs=[bh_spec, bh_spec, bh_spec],
        out_specs=[bh_spec, lse_spec],
        out_shape=[jax.ShapeDtypeStruct((t, WIDTH_BP), BF16),
                   jax.ShapeDtypeStruct((n_batch * N_HEADS_B, s_len, 1), F32)],
        compiler_params=_cp("parallel", "parallel"))(q, k, v)


def _mla_flash_bwd(q, k, v, o, do, lse2, n_batch, s_len, name):
    t = q.shape[0]
    tq = _tile(s_len, (256, 128))
    bh_spec, lse_spec = _mla_flash_specs(s_len)
    scale = QK_B ** -0.5
    c = scale * LOG2E

    def body(q_ref, k_ref, v_ref, o_ref, do_ref, lse_ref, dq_ref, dk_ref, dv_ref):
        dk_ref[...] = jnp.zeros_like(dk_ref)
        dv_ref[...] = jnp.zeros_like(dv_ref)
        for i in range(s_len // tq):
            rows = slice(i * tq, (i + 1) * tq)
            qv, dov = q_ref[rows, :], do_ref[rows, :]
            lse = lse_ref[0, rows, :]
            delta = jnp.sum(dov.astype(F32) * o_ref[rows, :].astype(F32), axis=1, keepdims=True)

            def part(ks, diag):
                kblk = k_ref[ks, :]
                s = _dot(qv, kblk, _NT)
                if diag:
                    s = _diag_mask(s)
                p = jnp.exp2(s * c - lse)
                dp = _dot(dov, v_ref[ks, :], _NT)
                ds = (p * (dp - delta) * scale).astype(BF16)
                dk_ref[ks, :] += _dot(ds, qv, _TN)
                dv_ref[ks, :] += _dot(p.astype(BF16), dov, _TN)
                return _dot(ds, kblk, _NN)

            dq = part(rows, True)
            if i:
                dq = dq + part(slice(0, i * tq), False)
            dq_ref[rows, :] = dq

    f32_wide = jax.ShapeDtypeStruct((t, WIDTH_BP), F32)
    return _tc_call(
        body, name=name, grid=(n_batch, N_HEADS_B),
        in_specs=[bh_spec, bh_spec, bh_spec, bh_spec, bh_spec, lse_spec],
        out_specs=[bh_spec, bh_spec, bh_spec], out_shape=[f32_wide] * 3,
        compiler_params=_cp("parallel", "parallel"))(q, k, v, o, do, lse2)


def _swa_heads(w, axis, order):
    heads = [lax.slice_in_dim(w, h * HEAD_DIM_A, (h + 1) * HEAD_DIM_A, axis=axis) for h in order]
    return jnp.concatenate(heads, axis=axis)


class _LayerWeights:
    def __init__(self, build):
        self._build, self._mats = build, {}

    def __getitem__(self, name):
        if name not in self._mats:
            self._mats.update(self._build(name))
        return self._mats[name]


def _layer_mats(w):
    if "w_in" not in w:
        return dict(w)
    w_in = w["w_in"]
    o = [0]
    for n in (WIDTH_A, KV_A, KV_A, Q_LORA, KV_LORA, ROPE, D_MODEL, D_MODEL):
        o.append(o[-1] + n)
    qa, ka, va, cq, ckv, kr, ga, gb = (w_in[o[i]:o[i + 1]] for i in range(8))
    pad = jnp.zeros((PROJ_W - IN_WIDTH, w_in.shape[1]), w_in.dtype)
    w_in_p = jnp.concatenate([ga, gb, _swa_heads(qa, 0, SWA_HEAD_ORDER), cq, ka, va, ckv, kr, pad], axis=0)
    uq = w["mla_w_uq"].reshape(N_HEADS_B, QK_B, Q_LORA)
    uq = jnp.pad(uq, ((0, 0), (0, HEAD_PAD - QK_B), (0, 0))).reshape(WIDTH_BP, Q_LORA)
    ukv = w["mla_w_ukv"].reshape(N_HEADS_B, NOPE + V_B, KV_LORA)
    wk = jnp.pad(ukv[:, :NOPE], ((0, 0), (0, HEAD_PAD - NOPE), (0, 0))).reshape(WIDTH_BP, KV_LORA)
    wv = jnp.pad(ukv[:, NOPE:], ((0, 0), (0, HEAD_PAD - V_B), (0, 0))).reshape(WIDTH_BP, KV_LORA)
    wb = w["w_branch_b"].reshape(D_MODEL, N_HEADS_B, V_B)
    wb = jnp.pad(wb, ((0, 0), (0, 0), (0, HEAD_PAD - V_B))).reshape(D_MODEL, WIDTH_BP)
    out = dict(w)
    out.update(w_in=w_in_p, mla_w_uq=uq, wk=wk, wv=wv, w_branch_b=wb,
               w_branch_a=_swa_heads(w["w_branch_a"], 1, SWA_HEAD_ORDER))
    return out


def _unlayer_grads(g):
    if "w_in" not in g:
        return g
    d = g["w_in"]
    ga, gb, qa, cq, ka, va, ckv, kr = (d[a:b] for a, b in (
        (P_GA, P_GA + D_MODEL), (P_GB, P_GB + D_MODEL), (P_QA, P_QA + WIDTH_A), (P_CQ, P_CQ + Q_LORA),
        (P_KA, P_KA + KV_A), (P_VA, P_VA + KV_A), (P_CKV, P_CKV + KV_LORA), (P_KR, P_KR + ROPE)))
    out = {n: v for n, v in g.items() if n not in ("wk", "wv")}
    out["w_in"] = jnp.concatenate([_swa_heads(qa, 0, SWA_HEAD_INVERSE), ka, va, cq, ckv, kr, ga, gb], axis=0)
    out["mla_w_uq"] = g["mla_w_uq"].reshape(N_HEADS_B, HEAD_PAD, Q_LORA)[:, :QK_B].reshape(N_HEADS_B * QK_B, Q_LORA)
    dk = g["wk"].reshape(N_HEADS_B, HEAD_PAD, KV_LORA)[:, :NOPE]
    dv = g["wv"].reshape(N_HEADS_B, HEAD_PAD, KV_LORA)[:, :V_B]
    out["mla_w_ukv"] = jnp.concatenate([dk, dv], axis=1).reshape(N_HEADS_B * (NOPE + V_B), KV_LORA)
    out["w_branch_b"] = g["w_branch_b"].reshape(D_MODEL, N_HEADS_B, HEAD_PAD)[:, :, :V_B].reshape(D_MODEL, WIDTH_B)
    out["w_branch_a"] = _swa_heads(g["w_branch_a"], 1, SWA_HEAD_INVERSE)
    return out


def _pad_lanes(v, width):
    return jnp.pad(v.reshape(1, -1), ((0, 0), (0, width - v.shape[-1])))


def _rope_tables(positions):
    half = ROPE // 2
    inv_freq = ROPE_BASE ** (-jnp.arange(half, dtype=F32) / half)
    ang = positions.astype(F32).reshape(-1, 1) * inv_freq
    cos, sin = jnp.cos(ang), jnp.sin(ang)
    t = cos.shape[0]
    one, zero = jnp.ones((t, NOPE), F32), jnp.zeros((t, NOPE), F32)
    tail = jnp.zeros((t, HEAD_PAD - QK_B), F32)
    z16 = jnp.zeros((t, half), F32)
    c = jnp.concatenate([one, cos, cos, tail], axis=1)
    sm = jnp.concatenate([zero, -sin, z16, tail], axis=1)
    sp = jnp.concatenate([zero, z16, sin, tail], axis=1)
    return c, sm, sp


def _ffn_fwd(x, gain, wg_t, wu_t, wd, tag):
    n = _rms_fwd(x, gain, f"{tag}_norm")
    a, b, hmid = _ffn_up(n, wg_t, wu_t, f"{tag}_up")
    y = _mm([(hmid, wd)], "nn", F32, f"{tag}_down", residual=x, alpha=0.5)
    return y, (x, n, a, b, hmid)


def _ffn_bwd(dy, dyb, saved, gain, wg_t, wu_t, wd, tag, on_early=None):
    x, n, a, b, hmid = saved
    da, db = _ffn_down_bwd(dyb, wd, a, b, f"{tag}_down_bwd")
    g_wd = _mm([(hmid, dyb)], "tn", BF16, f"{tag}_dwd", alpha=0.5)
    if on_early is not None:
        on_early(g_wd)
    g_wg = _mm([(da, n)], "tn", BF16, f"{tag}_dwg")
    g_wu = _mm([(db, n)], "tn", BF16, f"{tag}_dwu")
    dn = _mm([(da, wg_t), (db, wu_t)], "nn", F32, f"{tag}_dn")
    dx, dxb, g_gain = _rms_bwd(dn, x, gain, dy, f"{tag}_norm_bwd")
    return dx, dxb, g_wg, g_wu, g_wd, g_gain


def _fold_halves(d):
    return d[:, :HEAD_DIM_A] + d[:, HEAD_DIM_A:]


def _local_step(x, positions, target, layers, smalls, at=None):
    at = at or (lambda point, l, token, grads: None)
    _Order.token = None
    n_batch, s_len, d = x.shape
    t = n_batch * s_len
    xt = x.reshape(t, d)
    tabs = _rope_tables(positions)
    pos_col = positions.reshape(t, 1)
    pos_row = positions.reshape(t // BLOCK, 1, BLOCK)
    saved = []
    get_layer = layers if callable(layers) else layers.__getitem__
    for l in range(len(smalls)):
        w, s = get_layer(l), smalls[l]
        g1, gm, g2 = (s[k].reshape(1, d) for k in ("ffn1_norm", "mix_norm", "ffn2_norm"))
        qg2, kg2 = (jnp.tile(s[k].reshape(1, -1), (1, 2)) for k in ("swa_q_norm", "swa_k_norm"))
        sinks = _pad_lanes(s["swa_sinks"], HEAD_PAD)
        gq, gkv = s["mla_q_lora_norm"].reshape(1, -1), s["mla_kv_lora_norm"].reshape(1, -1)
        gqn, gkn = _pad_lanes(s["mla_q_norm"], HEAD_PAD), _pad_lanes(s["mla_k_norm"], HEAD_PAD)
        x1, sv1 = _ffn_fwd(xt, g1, w["ffn1_w_gate"], w["ffn1_w_up"], w["ffn1_w_down"], f"l{l}_ffn1")
        h = _rms_fwd(x1, gm, f"l{l}_mix_norm")
        proj = _mm([(h, w["w_in"])], "nt", F32, f"l{l}_proj")
        at("proj", l, proj, None)
        oa = _swa_fwd(proj, pos_col, pos_row, qg2, kg2, sinks, n_batch, s_len, f"l{l}_swa")
        q, k, v = _mla_pre(proj, tabs, gq, gkv, gqn, gkn, w["mla_w_uq"], w["wk"], w["wv"], f"l{l}_mla_pre")
        ob, lse = _mla_flash_fwd(q, k, v, n_batch, s_len, f"l{l}_mla")
        merged, ya, yb = _merge_fwd(oa, ob, proj, w["w_branch_a"], w["w_branch_b"], f"l{l}_merge")
        x2 = _mm([(merged, w["w_out"])], "nn", F32, f"l{l}_out", residual=x1)
        at("out", l, x2, None)
        x3, sv2 = _ffn_fwd(x2, g2, w["ffn2_w_gate"], w["ffn2_w_up"], w["ffn2_w_down"], f"l{l}_ffn2")
        saved.append((w, sv1, sv2, x1, h, proj, oa, q, k, v, ob, lse, merged, ya, yb,
                      (g1, gm, g2, qg2, kg2, sinks, gq, gkv, gqn, gkn)))
        xt = x3

    dy, dyb, loss = _loss_head(xt, target.reshape(t, d))

    big_grads, small_grads = [None] * len(smalls), [None] * len(smalls)
    for l in reversed(range(len(smalls))):
        w, sv1, sv2, x1, h, proj, oa, q, k, v, ob, lse, merged, ya, yb, gains = saved[l]
        g1, gm, g2, qg2, kg2, sinks, gq, gkv, gqn, gkn = gains
        bg, sg = {}, {}
        dy, dyb, bg["ffn2_w_gate"], bg["ffn2_w_up"], bg["ffn2_w_down"], sg["ffn2_norm"] = _ffn_bwd(
            dy, dyb, sv2, g2, w["ffn2_w_gate"], w["ffn2_w_up"], w["ffn2_w_down"], f"l{l}_ffn2")
        dya, dyb_, dga, dgb = _merge_bwd(dyb, w["w_out"], proj, ya, yb, f"l{l}_merge_bwd")
        at("mixer", l, sg["ffn2_norm"], bg)
        bg["w_out"] = _mm([(merged, dyb)], "tn", BF16, f"l{l}_dwo")
        doa = _mm([(dya, w["w_branch_a"])], "nn", BF16, f"l{l}_doa")
        bg["w_branch_a"] = _mm([(dya, oa)], "tn", BF16, f"l{l}_dwa")
        dob = _mm([(dyb_, w["w_branch_b"])], "nn", BF16, f"l{l}_dob")
        bg["w_branch_b"] = _mm([(dyb_, ob)], "tn", BF16, f"l{l}_dwb")
        dqa, dkc, dkp, dvc, dvp, dqg, dsk = _swa_bwd(
            proj, pos_col, pos_row, qg2, kg2, sinks, doa, n_batch, s_len, f"l{l}_swa_bwd")
        sg["swa_q_norm"], sg["swa_sinks"] = _fold_halves(dqg), dsk[:, :N_HEADS_A]
        at("mixer_mid", l, dqa, bg)
        dka, dva, dkg = _swa_kv_bwd(proj, kg2, dkc, dkp, dvc, dvp, n_batch, s_len, f"l{l}_swa_kv_bwd")
        sg["swa_k_norm"] = _fold_halves(dkg)
        dq, dk, dv = _mla_flash_bwd(q, k, v, ob, dob, lse, n_batch, s_len, f"l{l}_mla_bwd")
        (dcq, dckv, dkr, g_uq, g_wk, g_wv, sg["mla_q_lora_norm"], sg["mla_kv_lora_norm"], dgqn, dgkn) = _mla_pre_bwd(
            proj, tabs, gq, gkv, gqn, gkn, w["mla_w_uq"], w["wk"], w["wv"], dq, dk, dv, f"l{l}_mla_pre_bwd")
        sg["mla_q_norm"], sg["mla_k_norm"] = dgqn[:, :QK_B], dgkn[:, :QK_B]
        bg["mla_w_uq"], bg["wk"], bg["wv"] = g_uq.astype(BF16), g_wk.astype(BF16), g_wv.astype(BF16)
        dproj = jnp.concatenate([dga, dgb, dqa, dcq, dka, dva, dckv, dkr], axis=1)
        bg["w_in"] = _mm([(dproj, h)], "tn", BF16, f"l{l}_dwin")
        dh = _mm([(dproj, w["w_in"])], "nn", F32, f"l{l}_dh")
        dy, dyb, sg["mix_norm"] = _rms_bwd(dh, x1, gm, dy, f"l{l}_mix_norm_bwd")
        at("ffn1", l, sg["mix_norm"], bg)
        dy, dyb, bg["ffn1_w_gate"], bg["ffn1_w_up"], bg["ffn1_w_down"], sg["ffn1_norm"] = _ffn_bwd(
            dy, dyb, sv1, g1, w["ffn1_w_gate"], w["ffn1_w_up"], w["ffn1_w_down"], f"l{l}_ffn1",
            on_early=lambda token, l=l, bg=bg: at("ffn1_early", l, token, bg))
        big_grads[l], small_grads[l] = bg, sg
        at("done", l, dy, bg)
    return loss, dy.reshape(n_batch, s_len, d), big_grads, small_grads


def _round_up(n, m):
    return (n + m - 1) // m * m


def _flat_rows(shape, transposed):
    rows, k = (shape[1], shape[0]) if transposed else shape
    return _round_up(rows * k // LANES, 16), rows, k


def _flat_layout(shard_shapes, members, row_tile):
    table, off = [], 0
    for l, name in members:
        tr = dict(BIG)[name]
        pr, rows, k = _flat_rows(shard_shapes[name], tr)
        table.append(((l, name), tr, off, pr, rows, k))
        off += pr
    return table, _round_up(off, row_tile)


def _pack_flat(params, table, total):
    parts, off = [], 0
    for (l, name), tr, o, pr, rows, k in table:
        w = params[name][l]
        w = (w.T if tr else w).reshape(rows * k // LANES, LANES)
        parts.append(jnp.pad(w, ((0, pr - w.shape[0]), (0, 0))))
        off = o + pr
    if total > off:
        parts.append(jnp.zeros((total - off, LANES), parts[0].dtype))
    return jnp.concatenate(parts, axis=0)


def _unpack_flat(flat, table):
    out = {}
    for key, tr, o, pr, rows, k in table:
        w = flat[o:o + rows * k // LANES].reshape(rows, k)
        out[key] = w.T if tr else w
    return out


def _gathered_mats(gathered, table, layer):
    return {name: gathered[:, o:o + rows * k // LANES].reshape(N_DEV * rows, k)
            for (l, name), tr, o, pr, rows, k in table if l == layer}


def _pack_grads(grads, table, total):
    parts, off = [], 0
    for key, tr, o, pr, rows, k in table:
        g = grads[key].reshape(N_DEV, rows * k // LANES, LANES)
        parts.append(jnp.pad(g, ((0, 0), (0, pr - g.shape[1]), (0, 0))))
        off = o + pr
    if total > off:
        parts.append(jnp.zeros((N_DEV, total - off, LANES), BF16))
    return jnp.concatenate(parts, axis=1)


def _pack_small(params):
    parts = [params[n][l].reshape(-1).astype(F32) for l in range(DEPTH) for n in SMALL]
    v = jnp.concatenate(parts)
    return jnp.pad(v, (0, SMALL_ROWS * LANES - v.shape[0])).reshape(SMALL_ROWS, LANES)


def _unpack_small(flat, shapes):
    v, out, off = flat.reshape(-1), {}, 0
    for l in range(DEPTH):
        for n in SMALL:
            sz = math.prod(shapes[n][1:])
            out.setdefault(n, []).append(v[off:off + sz].reshape(shapes[n][1:]))
            off += sz
    return {n: jnp.stack(p) for n, p in out.items()}


_MESH = pl.DeviceIdType.MESH


def _place():
    return lax.axis_index("x"), lax.axis_index("y"), lax.axis_index("c")


def _handshake(peers):
    barrier = pltpu.get_barrier_semaphore()
    for peer in peers:
        pl.semaphore_signal(barrier, inc=1, device_id=peer, device_id_type=_MESH)
    pl.semaphore_wait(barrier, len(peers))


def _comm_call(body, out_shape, scratch, name, seq_id, spec=_ANY, start_early=False):
    if seq_id is None:
        return pl.pallas_call(body, name=name, out_shape=out_shape, in_specs=[spec, _ANY], out_specs=spec,
                              scratch_shapes=scratch)
    nbytes = (LINK_COST_SCALE if start_early else 1) * math.prod(out_shape.shape) * out_shape.dtype.itemsize
    return pl.kernel(body, out_type=out_shape, mesh=plsc.ScalarSubcoreMesh(axis_name="sequencer", num_cores=1),
                     scratch_types=scratch, name=name, compiler_params=pltpu.CompilerParams(collective_id=seq_id),
                     cost_estimate=pl.CostEstimate(flops=0, transcendentals=0, bytes_accessed=nbytes))


def _all_gather(x_shard, name, vmem=False, seq_id=None, after=None):
    spec = pl.BlockSpec(memory_space=pltpu.VMEM) if vmem else _ANY

    def body(x_ref, after_ref, out_ref, send_sems, recv_sems, local_sem):
        x, y, c = _place()
        me, sibling = (x, y, c), (x, y, 1 - c)
        chips = [(1 - x, y), (x, 1 - y), (1 - x, 1 - y)]
        if seq_id is not None:
            _handshake([sibling] + [(*chip, c) for chip in chips])

        def rows(px, py, pc):
            return out_ref.at[4 * px + 2 * py + pc]

        def copy(k, block, to, src=None):
            return pltpu.make_async_remote_copy(
                src_ref=rows(*block) if src is None else src, dst_ref=rows(*block),
                send_sem=send_sems.at[k], recv_sem=recv_sems.at[k], device_id=to, device_id_type=_MESH)

        mine = pltpu.make_async_copy(x_ref, rows(*me), local_sem)
        mine.start()
        first = [copy(0, me, sibling, src=x_ref)]
        first += [copy(1 + j, me, (*chip, c), src=x_ref) for j, chip in enumerate(chips)]
        for cp in first:
            cp.start()
        passed = [copy(4 + j, (*chip, c), sibling) for j, chip in enumerate(chips)]
        for j, chip in enumerate(chips):
            copy(1 + j, (*chip, c), me).wait_recv()
            passed[j].start()
        copy(0, sibling, me).wait_recv()
        for j, chip in enumerate(chips):
            copy(4 + j, (*chip, 1 - c), me).wait_recv()
        for cp in first + passed:
            cp.wait_send()
        mine.wait()

    return _comm_call(
        body, jax.ShapeDtypeStruct((N_DEV,) + x_shard.shape, x_shard.dtype),
        [pltpu.SemaphoreType.DMA((7,)), pltpu.SemaphoreType.DMA((7,)), pltpu.SemaphoreType.DMA], name, seq_id,
        spec, start_early=True)(x_shard, x_shard if after is None else after)


def _exchange_cores(g4, name, seq_id=None):
    n_chip, _, r, w = g4.shape

    def body(g_ref, after_ref, out_ref, send_sems, recv_sems):
        x, y, c = _place()
        if seq_id is not None:
            _handshake([(x, y, 1 - c)])
        copies = [pltpu.make_async_remote_copy(
            src_ref=g_ref.at[q, 1 - c], dst_ref=out_ref.at[q], send_sem=send_sems.at[q], recv_sem=recv_sems.at[q],
            device_id=(x, y, 1 - c), device_id_type=_MESH) for q in range(n_chip)]
        for cp in copies:
            cp.start()
        for cp in copies:
            cp.wait()

    return _comm_call(
        body, jax.ShapeDtypeStruct((n_chip, r, w), g4.dtype),
        [pltpu.SemaphoreType.DMA((n_chip,)), pltpu.SemaphoreType.DMA((n_chip,))], name, seq_id)(g4, g4)


def _exchange_chips(s1, name, seq_id=None):
    _, r, w = s1.shape

    def body(s_ref, after_ref, out_ref, send_sems, recv_sems):
        x, y, c = _place()
        chips = [(1 - x, y), (x, 1 - y), (1 - x, 1 - y)]
        if seq_id is not None:
            _handshake([(*chip, c) for chip in chips])
        copies = []
        for k, (tx, ty) in enumerate(chips):
            copies.append(pltpu.make_async_remote_copy(
                src_ref=s_ref.at[2 * tx + ty], dst_ref=out_ref.at[k], send_sem=send_sems.at[k],
                recv_sem=recv_sems.at[k], device_id=(tx, ty, c), device_id_type=_MESH))
        for cp in copies:
            cp.start()
        for cp in copies:
            cp.wait()

    return _comm_call(
        body, jax.ShapeDtypeStruct((3, r, w), s1.dtype),
        [pltpu.SemaphoreType.DMA((3,)), pltpu.SemaphoreType.DMA((3,))], name, seq_id)(s1, s1)


def _chip_sum(g4, recv, core, after, name, tr):
    n_chip, _, r, w = g4.shape

    def body(c_ref, a_ref, b_ref, after_ref, o_ref):
        o_ref[...] = (a_ref[...].astype(F32) + b_ref[...].astype(F32)).astype(o_ref.dtype)

    grid_spec = pltpu.PrefetchScalarGridSpec(
        num_scalar_prefetch=1, grid=(n_chip, r // tr),
        in_specs=[pl.BlockSpec((None, None, tr, w), lambda q, i, c: (q, c[0], i, 0)),
                  pl.BlockSpec((None, tr, w), lambda q, i, c: (q, i, 0)), _ANY],
        out_specs=pl.BlockSpec((None, tr, w), lambda q, i, c: (q, i, 0)))
    return pl.pallas_call(
        body, name=name, grid_spec=grid_spec, out_shape=jax.ShapeDtypeStruct((n_chip, r, w), g4.dtype),
        compiler_params=_cp("parallel", "parallel"))(core, g4, recv, after)


def _adam(w, g, m, v):
    m = ADAM_B1 * m + (1.0 - ADAM_B1) * g
    v = ADAM_B2 * v + (1.0 - ADAM_B2) * (g * g)
    m_hat = m / (1.0 - ADAM_B1 ** ADAM_STEP)
    v_hat = v / (1.0 - ADAM_B2 ** ADAM_STEP)
    delta = -ADAM_LR * (m_hat / (jnp.sqrt(v_hat) + ADAM_EPS) + ADAM_WD * w)
    return delta, m, v


def _adam_big(s1, r2, chip, w, m, v, name, tr):
    r, lanes = w.shape

    def body(c_ref, s_ref, r0_ref, r1_ref, r2_ref, w_ref, m_ref, v_ref, g_out, d_out, m_out, v_out):
        g = ((s_ref[...].astype(F32) + r0_ref[...].astype(F32)) + r1_ref[...].astype(F32)) + r2_ref[...].astype(F32)
        d, mn, vn = _adam(w_ref[...], g, m_ref[...], v_ref[...])
        g_out[...] = g
        d_out[...] = d
        m_out[...] = mn
        v_out[...] = vn

    row = pl.BlockSpec((tr, lanes), lambda i, c: (i, 0))
    rel = lambda k: pl.BlockSpec((None, tr, lanes), lambda i, c: (k, i, 0))
    grid_spec = pltpu.PrefetchScalarGridSpec(
        num_scalar_prefetch=1, grid=(r // tr,),
        in_specs=[pl.BlockSpec((None, tr, lanes), lambda i, c: (c[0], i, 0)), rel(0), rel(1), rel(2), row, row, row],
        out_specs=[row] * 4)
    return pl.pallas_call(
        body, name=name, grid_spec=grid_spec, out_shape=[jax.ShapeDtypeStruct((r, lanes), F32)] * 4,
        compiler_params=_cp("parallel"))(chip, s1, r2, r2, r2, w, m, v)


def _adam_small(parts, w, m, v, name):
    rows, lanes = w.shape

    def body(p_ref, w_ref, m_ref, v_ref, g_out, d_out, m_out, v_out):
        g = p_ref[0:rows, :]
        for dev in range(1, N_DEV):
            g = g + p_ref[dev * rows:(dev + 1) * rows, :]
        d, mn, vn = _adam(w_ref[...], g, m_ref[...], v_ref[...])
        g_out[...] = g
        d_out[...] = d
        m_out[...] = mn
        v_out[...] = vn

    return pl.pallas_call(
        body, name=name, out_shape=[jax.ShapeDtypeStruct((rows, lanes), F32)] * 4)(parts, w, m, v)


def kernel(x, positions, ffn1_norm, ffn1_w_gate, ffn1_w_up, ffn1_w_down, mix_norm, w_in, swa_q_norm, swa_k_norm, swa_sinks, mla_q_lora_norm, mla_w_uq, mla_kv_lora_norm, mla_w_ukv, mla_q_norm, mla_k_norm, w_branch_a, w_branch_b, w_out, ffn2_norm, ffn2_w_gate, ffn2_w_up, ffn2_w_down, loss_target, m_ffn1_norm, m_ffn1_w_gate, m_ffn1_w_up, m_ffn1_w_down, m_mix_norm, m_w_in, m_swa_q_norm, m_swa_k_norm, m_swa_sinks, m_mla_q_lora_norm, m_mla_w_uq, m_mla_kv_lora_norm, m_mla_w_ukv, m_mla_q_norm, m_mla_k_norm, m_w_branch_a, m_w_branch_b, m_w_out, m_ffn2_norm, m_ffn2_w_gate, m_ffn2_w_up, m_ffn2_w_down, v_ffn1_norm, v_ffn1_w_gate, v_ffn1_w_up, v_ffn1_w_down, v_mix_norm, v_w_in, v_swa_q_norm, v_swa_k_norm, v_swa_sinks, v_mla_q_lora_norm, v_mla_w_uq, v_mla_kv_lora_norm, v_mla_w_ukv, v_mla_q_norm, v_mla_k_norm, v_w_branch_a, v_w_branch_b, v_w_out, v_ffn2_norm, v_ffn2_w_gate, v_ffn2_w_up, v_ffn2_w_down):
    given = dict(locals())
    params = {n: given[n] for n in WEIGHTS}
    mom1 = {n: given["m_" + n] for n in WEIGHTS}
    mom2 = {n: given["v_" + n] for n in WEIGHTS}
    shard_shapes = {n: params[n].shape[1:] for n, _ in BIG}
    segs = [(members, tile) + _flat_layout(shard_shapes, members, tile) for members, tile in SEGMENTS]
    n_seg = len(segs)

    def members_of(s, l):
        names = [n for sl, n in segs[s][0] if sl == l]
        return [n for n in names if n != "mla_w_ukv"] + (["wk", "wv"] if "mla_w_ukv" in names else [])

    cx, cy, cc = _place()
    core = jnp.reshape(cc, (1,)).astype(jnp.int32)
    chip = jnp.reshape(2 * cx + cy, (1,)).astype(jnp.int32)

    w_flat = [_pack_flat(params, table, total) for _, _, table, total in segs]
    gathered = [None] * n_seg

    def gather(s, after):
        gathered[s] = _all_gather(w_flat[s].astype(BF16), f"gather_s{s}", seq_id=SEQ_IDS["gather", s] if s else None,
                                  after=after)

    def get_layer(l):
        def build(name):
            for s in range(n_seg):
                if name in members_of(s, l):
                    return _layer_mats(_gathered_mats(gathered[s], segs[s][2], l))
            raise KeyError(name)
        return _LayerWeights(build)

    gather(0, None)
    gather(1, gathered[0])
    gather(2, gathered[0])
    gather(3, gathered[0])
    smalls = [{n: params[n][l] for n in SMALL} for l in range(DEPTH)]

    pending, big_out, layer_grads = {}, [None] * n_seg, {}

    def exchange_cores(s):
        _, _, table, total = segs[s]
        mine = {}
        for l in range(DEPTH):
            names = members_of(s, l)
            if names:
                natural = _unlayer_grads({n: layer_grads[l][n] for n in names})
                mine.update({(l, n): g for n, g in natural.items()})
        g_flat = _pack_grads(mine, table, total)
        g4 = g_flat.reshape(N_DEV // 2, 2, total, LANES)
        pending[s] = (g4, _exchange_cores(g4, f"scatter_cores_s{s}", seq_id=SEQ_IDS["cores", s] if s else None))

    def exchange_chips(s, after):
        g4, from_core = pending.pop(s)
        s1 = _chip_sum(g4, from_core, core, after, f"sum_cores_s{s}", segs[s][1])
        _Order.token = s1
        pending[s] = (s1, _exchange_chips(s1, f"scatter_chips_s{s}", seq_id=SEQ_IDS["chips", s] if s else None))

    def update(s):
        _, tile, table, total = segs[s]
        s1, from_chips = pending.pop(s)
        big_out[s] = _adam_big(s1, from_chips, chip, w_flat[s], _pack_flat(mom1, table, total),
                               _pack_flat(mom2, table, total), f"adam_s{s}", tile)

    def at(point, l, token, grads):
        if grads is not None:
            layer_grads[l] = grads
        if (point, l) == ("ffn1", 1):
            exchange_cores(3)
        elif (point, l) == ("ffn1_early", 1):
            exchange_chips(3, token)
        elif (point, l) == ("mixer", 0):
            exchange_cores(2)
        elif (point, l) == ("mixer_mid", 0):
            exchange_chips(2, token)
        elif (point, l) == ("ffn1", 0):
            exchange_cores(1)
        elif (point, l) == ("ffn1_early", 0):
            exchange_chips(1, token)
        elif (point, l) == ("done", 0):
            exchange_cores(0)

    loss, grad_x, _, small_grads = _local_step(x, positions, loss_target, get_layer, smalls, at)
    loss = lax.psum(loss[0, 0], ("x", "y", "c"))
    exchange_chips(0, grad_x)
    for s in reversed(range(n_seg)):
        update(s)

    g_small = _pack_small({n: [small_grads[l][n] for l in range(DEPTH)] for n in SMALL})
    parts = _all_gather(g_small, "gather_small", vmem=True).reshape(N_DEV * SMALL_ROWS, LANES)
    small_out = _adam_small(parts, _pack_small(params), _pack_small(mom1), _pack_small(mom2), "adam_small")

    outs = []
    shapes = {n: params[n].shape for n in SMALL}
    for kind, small in enumerate(small_out):
        pieces = {}
        for s in range(n_seg):
            pieces.update(_unpack_flat(big_out[s][kind], segs[s][2]))
        tree = {n: jnp.stack([pieces[l, n] for l in range(DEPTH)]) for n, _ in BIG}
        tree.update(_unpack_small(small, shapes))
        outs.append(tree)
    return (loss, grad_x, *[o[n] for o in outs for n in WEIGHTS])
```

```python
import math

import jax
import jax.numpy as jnp
from jax import lax
from jax.experimental import pallas as pl
from jax.experimental.pallas import tpu as pltpu
from jax.experimental.pallas import tpu_sc as plsc

F32 = jnp.float32
BF16 = jnp.bfloat16

N_DEV = 8
DEPTH = 2
D_MODEL = 1024
D_FF = 2816
HEAD_DIM_A = 64
N_HEADS_A = 8
N_KV_HEADS_A = 2
GROUP_A = N_HEADS_A // N_KV_HEADS_A
BLOCK = 128
N_HEADS_B = 8
Q_LORA = 256
KV_LORA = 128
NOPE = 64
ROPE = 32
QK_B = NOPE + ROPE
V_B = 64
HEAD_PAD = 128
WIDTH_A = N_HEADS_A * HEAD_DIM_A
WIDTH_B = N_HEADS_B * V_B
WIDTH_BP = N_HEADS_B * HEAD_PAD
KV_A = N_KV_HEADS_A * HEAD_DIM_A
IN_WIDTH = WIDTH_A + 2 * KV_A + Q_LORA + KV_LORA + ROPE + 2 * D_MODEL
ROPE_BASE = 10000.0
EPS = 1e-6
NEG = -1e30
LOG2E = 1.4426950408889634

P_GA, P_GB, P_QA, P_CQ, P_KA, P_VA, P_CKV, P_KR = 0, 1024, 2048, 2560, 2816, 2944, 3072, 3200
PROJ_W = 3328
SWA_HEAD_ORDER = (0, 4, 1, 5, 2, 6, 3, 7)
SWA_HEAD_INVERSE = tuple(SWA_HEAD_ORDER.index(h) for h in range(N_HEADS_A))

ADAM_LR, ADAM_B1, ADAM_B2, ADAM_EPS, ADAM_WD, ADAM_STEP = 0.001, 0.9, 0.999, 1e-08, 0.01, 10

VMEM_LIMIT = 56 * 1024 * 1024
LANES = 1024

BIG = (("ffn1_w_gate", True), ("ffn1_w_up", True), ("ffn1_w_down", False), ("w_in", True), ("mla_w_uq", True),
       ("mla_w_ukv", True), ("w_branch_a", True), ("w_branch_b", True), ("w_out", False),
       ("ffn2_w_gate", True), ("ffn2_w_up", True), ("ffn2_w_down", False))
SMALL = ("ffn1_norm", "mix_norm", "ffn2_norm", "swa_q_norm", "swa_k_norm", "swa_sinks", "mla_q_lora_norm",
         "mla_kv_lora_norm", "mla_q_norm", "mla_k_norm")
WEIGHTS = ("ffn1_norm", "ffn1_w_gate", "ffn1_w_up", "ffn1_w_down", "mix_norm", "w_in", "swa_q_norm", "swa_k_norm",
           "swa_sinks", "mla_q_lora_norm", "mla_w_uq", "mla_kv_lora_norm", "mla_w_ukv", "mla_q_norm", "mla_k_norm",
           "w_branch_a", "w_branch_b", "w_out", "ffn2_norm", "ffn2_w_gate", "ffn2_w_up", "ffn2_w_down")
SMALL_ROWS = 8
FFN1 = ("ffn1_w_gate", "ffn1_w_up", "ffn1_w_down")
FFN2 = ("ffn2_w_gate", "ffn2_w_up", "ffn2_w_down")
MIXER = tuple(n for n, _ in BIG if n not in FFN1 + FFN2)
GATHER_SEGMENTS = (
    (tuple((0, n) for n in FFN1[:2]), 352),
    (((0, FFN1[2]),), 352),
    (tuple((0, n) for n in MIXER), 240),
    (tuple((0, n) for n in FFN2) + tuple((1, n) for n in FFN1), 352),
    (tuple((1, n) for n in MIXER + FFN2), 480),
)
SCATTER_SEGMENTS = (
    (((0, FFN1[2]),), 352),
    (tuple((0, n) for n in FFN1[:2]), 352),
    (tuple((0, n) for n in MIXER), 240),
    (tuple((0, n) for n in FFN2) + tuple((1, n) for n in FFN1), 352),
    (tuple((1, n) for n in MIXER + FFN2), 480),
)
LINK_COST_SCALE = 64
SEQ_IDS = {(kind, s): 1 + 4 * k + s - 1 for k, kind in enumerate(("gather", "cores", "chips")) for s in (1, 2, 3, 4)}


def _cp(*sem):
    return pltpu.CompilerParams(dimension_semantics=sem, vmem_limit_bytes=VMEM_LIMIT)


def _tile(n, prefs):
    for t in prefs:
        if n % t == 0:
            return t
    return n


def _dot(a, b, dims):
    return lax.dot_general(a, b, (dims, ((), ())), preferred_element_type=F32)


_NT = ((1,), (1,))
_NN = ((1,), (0,))
_TN = ((0,), (0,))


_ANY = pl.BlockSpec(memory_space=pl.ANY)


class _Order:
    tokens = ()


def _tc_call(body, *, in_specs, **kw):
    def run(*args):
        tokens, n = _Order.tokens, len(args)
        if not tokens:
            out = pl.pallas_call(body, in_specs=in_specs, **kw)(*args)
        else:
            def chained(*refs):
                return body(*refs[:n], *refs[n + len(tokens):])
            out = pl.pallas_call(chained, in_specs=list(in_specs) + [_ANY] * len(tokens), **kw)(*args, *tokens)
        _Order.tokens = (jax.tree.leaves(out)[0],)
        return out
    return run


def _sigmoid(x):
    return 0.5 * jnp.tanh(0.5 * x) + 0.5


def _chunks(n, width):
    return [(c, min(width, n - c)) for c in range(0, n, width)]


def _mm(pairs, mode, out_dtype, name, residual=None, alpha=1.0):
    for a, b in pairs:
        assert a.dtype == BF16 and b.dtype == BF16, (name, a.dtype, b.dtype)
    if mode == "tn":
        (a, b), = pairs
        return _mm_tokens(a, b, out_dtype, name, alpha)
    t = pairs[0][0].shape[0]
    n = pairs[0][1].shape[0] if mode == "nt" else pairs[0][1].shape[1]
    tm = _tile(t, (512, 256, 128))
    dims = _NT if mode == "nt" else _NN
    in_specs, args = [], []
    for a, w in pairs:
        in_specs.append(pl.BlockSpec((tm, a.shape[1]), lambda i: (i, 0)))
        in_specs.append(pl.BlockSpec(w.shape, lambda i: (0, 0)))
        args += [a, w]
    if residual is not None:
        in_specs.append(pl.BlockSpec((tm, n), lambda i: (i, 0)))
        args.append(residual)
    n_pairs = len(pairs)

    def body(*refs):
        o_ref = refs[-1]
        for c0, cw in _chunks(n, 512):
            acc = None
            for p in range(n_pairs):
                w_ref = refs[2 * p + 1]
                w = w_ref[c0:c0 + cw, :] if mode == "nt" else w_ref[:, c0:c0 + cw]
                d = _dot(refs[2 * p][...], w, dims)
                acc = d if acc is None else acc + d
            if alpha != 1.0:
                acc = acc * alpha
            if residual is not None:
                acc = refs[2 * n_pairs][:, c0:c0 + cw] + acc
            o_ref[:, c0:c0 + cw] = acc.astype(out_dtype)

    return _tc_call(
        body, name=name, grid=(t // tm,), in_specs=in_specs, out_specs=pl.BlockSpec((tm, n), lambda i: (i, 0)),
        out_shape=jax.ShapeDtypeStruct((t, n), out_dtype), compiler_params=_cp("parallel"))(*args)


def _mm_tokens(a, b, out_dtype, name, alpha):
    t, m = a.shape
    n = b.shape[1]
    tk = _tile(t, (512, 256, 128))

    def body(a_ref, b_ref, o_ref, acc_ref):
        k = pl.program_id(0)

        @pl.when(k == 0)
        def _():
            acc_ref[...] = jnp.zeros_like(acc_ref)

        for c0, cw in _chunks(m, 512):
            acc_ref[c0:c0 + cw, :] += _dot(a_ref[:, c0:c0 + cw], b_ref[...], _TN)

        @pl.when(k == pl.num_programs(0) - 1)
        def _():
            o_ref[...] = (acc_ref[...] * alpha).astype(out_dtype)

    return _tc_call(
        body, name=name, grid=(t // tk,),
        in_specs=[pl.BlockSpec((tk, m), lambda k: (k, 0)), pl.BlockSpec((tk, n), lambda k: (k, 0))],
        out_specs=pl.BlockSpec((m, n), lambda k: (0, 0)), out_shape=jax.ShapeDtypeStruct((m, n), out_dtype),
        scratch_shapes=[pltpu.VMEM((m, n), F32)], compiler_params=_cp("arbitrary"))(a, b)


def _rms_mm(x, gain, w, name):
    t, d = x.shape
    n = w.shape[0]
    tm = _tile(t, (512, 256, 128))

    def body(x_ref, g_ref, w_ref, h_ref, o_ref):
        xv = x_ref[...]
        r = lax.rsqrt(jnp.mean(xv * xv, axis=1, keepdims=True) + EPS)
        hv = (xv * r * g_ref[...]).astype(BF16)
        h_ref[...] = hv
        for c0, cw in _chunks(n, 512):
            o_ref[:, c0:c0 + cw] = _dot(hv, w_ref[c0:c0 + cw, :], _NT)

    row = pl.BlockSpec((tm, d), lambda i: (i, 0))
    return _tc_call(
        body, name=name, grid=(t // tm,),
        in_specs=[row, pl.BlockSpec((1, d), lambda i: (0, 0)), pl.BlockSpec(w.shape, lambda i: (0, 0))],
        out_specs=[row, pl.BlockSpec((tm, n), lambda i: (i, 0))],
        out_shape=[jax.ShapeDtypeStruct((t, d), BF16), jax.ShapeDtypeStruct((t, n), F32)],
        compiler_params=_cp("parallel"))(x, gain, w)


def _mm_rms_bwd(pairs, x, gain, res, name):
    t, d = x.shape
    tm = _tile(t, (512, 256, 128))
    n_pairs = len(pairs)

    def body(*refs):
        x_ref, g_ref, res_ref, dx_ref, dxb_ref, dg_ref, dn_ref = refs[2 * n_pairs:]
        for c0, cw in _chunks(d, 512):
            acc = None
            for p in range(n_pairs):
                part = _dot(refs[2 * p][...], refs[2 * p + 1][:, c0:c0 + cw], _NN)
                acc = part if acc is None else acc + part
            dn_ref[:, c0:c0 + cw] = acc
        xv = x_ref[...]
        r = lax.rsqrt(jnp.mean(xv * xv, axis=1, keepdims=True) + EPS)
        xh = xv * r
        dnv = dn_ref[...]
        dxh = dnv * g_ref[...]
        dx = res_ref[...] + r * (dxh - xh * jnp.mean(dxh * xh, axis=1, keepdims=True))
        dx_ref[...] = dx
        dxb_ref[...] = dx.astype(BF16)

        @pl.when(pl.program_id(0) == 0)
        def _():
            dg_ref[...] = jnp.zeros_like(dg_ref)

        dg_ref[...] += jnp.sum(dnv * xh, axis=0, keepdims=True)

    in_specs, args = [], []
    for a, w in pairs:
        assert a.dtype == BF16 and w.dtype == BF16, (name, a.dtype, w.dtype)
        in_specs += [pl.BlockSpec((tm, a.shape[1]), lambda i: (i, 0)), pl.BlockSpec(w.shape, lambda i: (0, 0))]
        args += [a, w]
    row = pl.BlockSpec((tm, d), lambda i: (i, 0))
    one = pl.BlockSpec((1, d), lambda i: (0, 0))
    return _tc_call(
        body, name=name, grid=(t // tm,), in_specs=in_specs + [row, one, row], out_specs=[row, row, one],
        out_shape=[jax.ShapeDtypeStruct((t, d), F32), jax.ShapeDtypeStruct((t, d), BF16),
                   jax.ShapeDtypeStruct((1, d), F32)],
        scratch_shapes=[pltpu.VMEM((tm, d), F32)], compiler_params=_cp("arbitrary"))(*args, x, gain, res)


def _ffn_up(x, gain, wg_t, wu_t, name):
    t, d = x.shape
    f = wg_t.shape[0]
    tm = _tile(t, (512, 256, 128))

    def body(x_ref, g_ref, wg_ref, wu_ref, n_ref, a_ref, b_ref, h_ref):
        xv = x_ref[...]
        r = lax.rsqrt(jnp.mean(xv * xv, axis=1, keepdims=True) + EPS)
        nv = (xv * r * g_ref[...]).astype(BF16)
        n_ref[...] = nv
        for c0, cw in _chunks(f, 256):
            a = _dot(nv, wg_ref[c0:c0 + cw, :], _NT)
            b = _dot(nv, wu_ref[c0:c0 + cw, :], _NT)
            a_ref[:, c0:c0 + cw] = a.astype(BF16)
            b_ref[:, c0:c0 + cw] = b.astype(BF16)
            h_ref[:, c0:c0 + cw] = (a * _sigmoid(a) * b).astype(BF16)

    w_spec = pl.BlockSpec((f, d), lambda i: (0, 0))
    x_spec = pl.BlockSpec((tm, d), lambda i: (i, 0))
    o_spec = pl.BlockSpec((tm, f), lambda i: (i, 0))
    o_shape = jax.ShapeDtypeStruct((t, f), BF16)
    return _tc_call(
        body, name=name, grid=(t // tm,), in_specs=[x_spec, pl.BlockSpec((1, d), lambda i: (0, 0)), w_spec, w_spec],
        out_specs=[x_spec] + [o_spec] * 3, out_shape=[jax.ShapeDtypeStruct((t, d), BF16)] + [o_shape] * 3,
        compiler_params=_cp("parallel"))(x, gain, wg_t, wu_t)


def _ffn_down_bwd(dxb, wd, a, b, name):
    t, d = dxb.shape
    f = wd.shape[0]
    tm = _tile(t, (512, 256, 128))

    def body(dx_ref, wd_ref, a_ref, b_ref, da_ref, db_ref):
        dxv = dx_ref[...]
        for c0, cw in _chunks(f, 256):
            dh = 0.5 * _dot(dxv, wd_ref[c0:c0 + cw, :], _NT)
            av = a_ref[:, c0:c0 + cw].astype(F32)
            bv = b_ref[:, c0:c0 + cw].astype(F32)
            sg = _sigmoid(av)
            da_ref[:, c0:c0 + cw] = (dh * bv * (sg * (1.0 + av * (1.0 - sg)))).astype(BF16)
            db_ref[:, c0:c0 + cw] = (dh * (av * sg)).astype(BF16)

    o_spec = pl.BlockSpec((tm, f), lambda i: (i, 0))
    o_shape = jax.ShapeDtypeStruct((t, f), BF16)
    return _tc_call(
        body, name=name, grid=(t // tm,),
        in_specs=[pl.BlockSpec((tm, d), lambda i: (i, 0)), pl.BlockSpec((f, d), lambda i: (0, 0)), o_spec, o_spec],
        out_specs=[o_spec] * 2, out_shape=[o_shape] * 2, compiler_params=_cp("parallel"))(dxb, wd, a, b)


def _loss_head(y, target):
    t, d = y.shape
    tm = _tile(t, (512, 256, 128))

    def body(y_ref, t_ref, dy_ref, dyb_ref, loss_ref, acc_ref):
        i = pl.program_id(0)
        e = y_ref[...] - t_ref[...]
        dy = e * (1.0 / d)
        dy_ref[...] = dy
        dyb_ref[...] = dy.astype(BF16)

        @pl.when(i == 0)
        def _():
            acc_ref[...] = jnp.zeros_like(acc_ref)

        acc_ref[...] += jnp.sum(e * e, axis=0, keepdims=True)

        @pl.when(i == pl.num_programs(0) - 1)
        def _():
            loss_ref[...] = jnp.sum(acc_ref[...], axis=1, keepdims=True) * (0.5 / d)

    row = pl.BlockSpec((tm, d), lambda i: (i, 0))
    return _tc_call(
        body, name="loss_head", grid=(t // tm,), in_specs=[row, row],
        out_specs=[row, row, pl.BlockSpec((1, 1), lambda i: (0, 0))],
        out_shape=[jax.ShapeDtypeStruct((t, d), F32), jax.ShapeDtypeStruct((t, d), BF16),
                   jax.ShapeDtypeStruct((1, 1), F32)],
        scratch_shapes=[pltpu.VMEM((1, d), F32)], compiler_params=_cp("arbitrary"))(y, target)


def _merge_fwd(oa, ob, proj, wa_t, wb_t, name):
    t = oa.shape[0]
    d = wa_t.shape[0]
    tm = _tile(t, (512, 256, 128))

    def body(oa_ref, ob_ref, ga_ref, gb_ref, wa_ref, wb_ref, mg_ref, ya_ref, yb_ref):
        oav, obv = oa_ref[...], ob_ref[...]
        for c0, cw in _chunks(d, 512):
            cs = slice(c0, c0 + cw)
            ya = _dot(oav, wa_ref[cs, :], _NT)
            yb = _dot(obv, wb_ref[cs, :], _NT)
            mg_ref[:, cs] = (_sigmoid(ga_ref[:, cs]) * ya + _sigmoid(gb_ref[:, cs]) * yb).astype(BF16)
            ya_ref[:, cs] = ya.astype(BF16)
            yb_ref[:, cs] = yb.astype(BF16)

    o_spec = pl.BlockSpec((tm, d), lambda i: (i, 0))
    o_shape = jax.ShapeDtypeStruct((t, d), BF16)
    return _tc_call(
        body, name=name, grid=(t // tm,),
        in_specs=[pl.BlockSpec((tm, oa.shape[1]), lambda i: (i, 0)), pl.BlockSpec((tm, ob.shape[1]), lambda i: (i, 0)),
                  pl.BlockSpec((tm, d), lambda i: (i, P_GA // d)), pl.BlockSpec((tm, d), lambda i: (i, P_GB // d)),
                  pl.BlockSpec(wa_t.shape, lambda i: (0, 0)), pl.BlockSpec(wb_t.shape, lambda i: (0, 0))],
        out_specs=[o_spec] * 3, out_shape=[o_shape] * 3,
        compiler_params=_cp("parallel"))(oa, ob, proj, proj, wa_t, wb_t)


def _merge_bwd(dxb, wo, proj, ya, yb, name):
    t, d = dxb.shape
    tm = _tile(t, (512, 256, 128))

    def body(dx_ref, wo_ref, ga_ref, gb_ref, ya_ref, yb_ref, dya_ref, dyb_ref, dga_ref, dgb_ref):
        dxv = dx_ref[...]
        for c0, cw in _chunks(d, 512):
            cs = slice(c0, c0 + cw)
            dm = _dot(dxv, wo_ref[cs, :], _NT)
            sa = _sigmoid(ga_ref[:, cs])
            sb = _sigmoid(gb_ref[:, cs])
            dya_ref[:, cs] = (dm * sa).astype(BF16)
            dyb_ref[:, cs] = (dm * sb).astype(BF16)
            dga_ref[:, cs] = (dm * ya_ref[:, cs].astype(F32) * (sa * (1.0 - sa))).astype(BF16)
            dgb_ref[:, cs] = (dm * yb_ref[:, cs].astype(F32) * (sb * (1.0 - sb))).astype(BF16)

    o_spec = pl.BlockSpec((tm, d), lambda i: (i, 0))
    o_shape = jax.ShapeDtypeStruct((t, d), BF16)
    return _tc_call(
        body, name=name, grid=(t // tm,),
        in_specs=[o_spec, pl.BlockSpec((d, d), lambda i: (0, 0)),
                  pl.BlockSpec((tm, d), lambda i: (i, P_GA // d)), pl.BlockSpec((tm, d), lambda i: (i, P_GB // d)),
                  o_spec, o_spec],
        out_specs=[o_spec] * 4, out_shape=[o_shape] * 4,
        compiler_params=_cp("parallel"))(dxb, wo, proj, proj, ya, yb)


def _swa_common(i, pq_ref, pkp_ref, pkc_ref):
    pk = jnp.concatenate([pkp_ref[0], pkc_ref[0]], axis=1)
    dist = (pq_ref[...] - pk).astype(F32)
    row = lax.broadcasted_iota(jnp.int32, (BLOCK, 2 * BLOCK), 0)
    col = lax.broadcasted_iota(jnp.int32, (BLOCK, 2 * BLOCK), 1)
    diff = row + BLOCK - col
    valid = (diff >= 0) & (diff < BLOCK) & ((i > 0) | (col >= BLOCK))
    return jnp.concatenate([dist] * GROUP_A, axis=0), jnp.concatenate([valid] * GROUP_A, axis=0)


def _half_sum(x, lo):
    s_lo = jnp.sum(jnp.where(lo, x, 0.0), axis=1, keepdims=True)
    s_hi = jnp.sum(jnp.where(lo, 0.0, x), axis=1, keepdims=True)
    return jnp.where(lo, s_lo, s_hi)


def _norm2(x, gain2, lo):
    r = lax.rsqrt(_half_sum(x * x, lo) * (1.0 / HEAD_DIM_A) + EPS)
    xh = x * r
    return xh * gain2, xh, r


def _norm2_bwd(d, xh, r, gain2, lo):
    dxh = d * gain2
    return r * (dxh - xh * (_half_sum(dxh * xh, lo) * (1.0 / HEAD_DIM_A)))


def _swa_group(grp, qn, lo, sk_ref):
    mask = lo if grp == 0 else jnp.logical_not(lo)
    q4 = jnp.concatenate([jnp.where(mask, qn[j], 0.0) for j in range(GROUP_A)], axis=0).astype(BF16)
    heads = [grp * GROUP_A + j for j in range(GROUP_A)]
    slope = jnp.concatenate([jnp.full((BLOCK, 1), 2.0 ** (-(h + 1)), F32) for h in heads], axis=0)
    sink = jnp.concatenate([jnp.broadcast_to(sk_ref[:, h:h + 1], (BLOCK, 1)) for h in heads], axis=0)
    return mask, q4, slope, sink


def _swa_probs(q4, kk, dist4, valid4, slope, sink):
    s = _dot(q4, kk, _NT) * (HEAD_DIM_A ** -0.5) - slope * dist4
    s = jnp.where(valid4, s, NEG)
    m = jnp.maximum(jnp.max(s, axis=1, keepdims=True), sink)
    e = jnp.exp(s - m)
    es = jnp.exp(sink - m)
    inv = 1.0 / (jnp.sum(e, axis=1, keepdims=True) + es)
    return e * inv, es * inv


def _swa_specs(s_len):
    nb = s_len // BLOCK

    def rowblk(b, i):
        return b * nb + i

    def prevblk(b, i):
        return b * nb + jnp.maximum(i - 1, 0)

    q_spec = pl.BlockSpec((BLOCK, WIDTH_A), lambda b, i: (rowblk(b, i), P_QA // WIDTH_A))
    kc_spec = pl.BlockSpec((BLOCK, KV_A), lambda b, i: (rowblk(b, i), P_KA // KV_A))
    kp_spec = pl.BlockSpec((BLOCK, KV_A), lambda b, i: (prevblk(b, i), P_KA // KV_A))
    vc_spec = pl.BlockSpec((BLOCK, KV_A), lambda b, i: (rowblk(b, i), P_VA // KV_A))
    vp_spec = pl.BlockSpec((BLOCK, KV_A), lambda b, i: (prevblk(b, i), P_VA // KV_A))
    pq_spec = pl.BlockSpec((BLOCK, 1), lambda b, i: (rowblk(b, i), 0))
    pkc_spec = pl.BlockSpec((1, 1, BLOCK), lambda b, i: (rowblk(b, i), 0, 0))
    pkp_spec = pl.BlockSpec((1, 1, BLOCK), lambda b, i: (prevblk(b, i), 0, 0))
    return nb, rowblk, [q_spec, kc_spec, kp_spec, vc_spec, vp_spec, pq_spec, pkc_spec, pkp_spec]


def _swa_fwd(proj, pos_col, pos_row, qg2, kg2, sinks, n_batch, s_len, name):
    t = proj.shape[0]
    nb, rowblk, specs = _swa_specs(s_len)
    small = pl.BlockSpec((1, HEAD_PAD), lambda b, i: (0, 0))

    def body(q_ref, kc_ref, kp_ref, vc_ref, vp_ref, pq_ref, pkc_ref, pkp_ref, qg_ref, kg_ref, sk_ref, o_ref):
        dist4, valid4 = _swa_common(pl.program_id(1), pq_ref, pkp_ref, pkc_ref)
        lo = lax.broadcasted_iota(jnp.int32, (1, HEAD_PAD), 1) < HEAD_DIM_A
        kk = _norm2(jnp.concatenate([kp_ref[...], kc_ref[...]], axis=0), kg_ref[...], lo)[0].astype(BF16)
        vv = jnp.concatenate([vp_ref[...], vc_ref[...]], axis=0).astype(BF16)
        qn = [_norm2(q_ref[:, j * HEAD_PAD:(j + 1) * HEAD_PAD], qg_ref[...], lo)[0] for j in range(GROUP_A)]
        outs = []
        for grp in range(N_KV_HEADS_A):
            _, q4, slope, sink = _swa_group(grp, qn, lo, sk_ref)
            p, _ = _swa_probs(q4, kk, dist4, valid4, slope, sink)
            outs.append(_dot(p.astype(BF16), vv, _NN))
        for j in range(GROUP_A):
            rs = slice(j * BLOCK, (j + 1) * BLOCK)
            o_ref[:, j * HEAD_PAD:(j + 1) * HEAD_PAD] = jnp.where(lo, outs[0][rs], outs[1][rs]).astype(BF16)

    return _tc_call(
        body, name=name, grid=(n_batch, nb), in_specs=specs + [small, small, small],
        out_specs=pl.BlockSpec((BLOCK, WIDTH_A), lambda b, i: (rowblk(b, i), 0)),
        out_shape=jax.ShapeDtypeStruct((t, WIDTH_A), BF16),
        compiler_params=_cp("parallel", "parallel"))(proj, proj, proj, proj, proj, pos_col, pos_row, pos_row,
                                                     qg2, kg2, sinks)


def _swa_bwd(proj, pos_col, pos_row, qg2, kg2, sinks, do, n_batch, s_len, name):
    t = proj.shape[0]
    nb, rowblk, specs = _swa_specs(s_len)
    small = pl.BlockSpec((1, HEAD_PAD), lambda b, i: (0, 0))
    scale = HEAD_DIM_A ** -0.5

    def body(q_ref, kc_ref, kp_ref, vc_ref, vp_ref, pq_ref, pkc_ref, pkp_ref, qg_ref, kg_ref, sk_ref, do_ref,
             dq_ref, dkc_ref, dkp_ref, dvc_ref, dvp_ref, dqg_ref, dsk_ref):
        b, i = pl.program_id(0), pl.program_id(1)

        @pl.when((b == 0) & (i == 0))
        def _():
            dqg_ref[...] = jnp.zeros_like(dqg_ref)
            dsk_ref[...] = jnp.zeros_like(dsk_ref)

        dist4, valid4 = _swa_common(i, pq_ref, pkp_ref, pkc_ref)
        lane = lax.broadcasted_iota(jnp.int32, (1, HEAD_PAD), 1)
        lo = lane < HEAD_DIM_A
        kk = _norm2(jnp.concatenate([kp_ref[...], kc_ref[...]], axis=0), kg_ref[...], lo)[0].astype(BF16)
        vv = jnp.concatenate([vp_ref[...], vc_ref[...]], axis=0).astype(BF16)
        qs = [_norm2(q_ref[:, j * HEAD_PAD:(j + 1) * HEAD_PAD], qg_ref[...], lo) for j in range(GROUP_A)]
        dos = [do_ref[:, j * HEAD_PAD:(j + 1) * HEAD_PAD] for j in range(GROUP_A)]
        dkk = jnp.zeros((2 * BLOCK, HEAD_PAD), F32)
        dvv = jnp.zeros((2 * BLOCK, HEAD_PAD), F32)
        dsk = jnp.zeros((1, HEAD_PAD), F32)
        dq4 = []
        for grp in range(N_KV_HEADS_A):
            mask, q4, slope, sink = _swa_group(grp, [q[0] for q in qs], lo, sk_ref)
            do4 = jnp.concatenate([jnp.where(mask, d, jnp.zeros_like(d)) for d in dos], axis=0)
            p, ps = _swa_probs(q4, kk, dist4, valid4, slope, sink)
            dp = _dot(do4, vv, _NT)
            delta = jnp.sum(p * dp, axis=1, keepdims=True)
            ds = (p * (dp - delta) * scale).astype(BF16)
            dsink = ps * delta
            for j in range(GROUP_A):
                dsk = dsk + jnp.where(lane == grp * GROUP_A + j, -jnp.sum(dsink[j * BLOCK:(j + 1) * BLOCK]), 0.0)
            dvv = dvv + _dot(p.astype(BF16), do4, _TN)
            dkk = dkk + _dot(ds, q4, _TN)
            dq4.append(_dot(ds, kk, _NN))
        dqg = jnp.zeros((1, HEAD_PAD), F32)
        for j in range(GROUP_A):
            rs = slice(j * BLOCK, (j + 1) * BLOCK)
            _, xh, r = qs[j]
            dqn = jnp.where(lo, dq4[0][rs], dq4[1][rs])
            dqg = dqg + jnp.sum(dqn * xh, axis=0, keepdims=True)
            dq_ref[:, j * HEAD_PAD:(j + 1) * HEAD_PAD] = _norm2_bwd(dqn, xh, r, qg_ref[...], lo).astype(BF16)
        dkp_ref[...] = dkk[:BLOCK]
        dkc_ref[...] = dkk[BLOCK:]
        dvp_ref[...] = dvv[:BLOCK]
        dvc_ref[...] = dvv[BLOCK:]
        dqg_ref[...] += dqg
        dsk_ref[...] += dsk

    kv_out = pl.BlockSpec((BLOCK, KV_A), lambda b, i: (rowblk(b, i), 0))
    kv_shape = jax.ShapeDtypeStruct((t, KV_A), F32)
    wide = pl.BlockSpec((BLOCK, WIDTH_A), lambda b, i: (rowblk(b, i), 0))
    return _tc_call(
        body, name=name, grid=(n_batch, nb), in_specs=specs + [small, small, small, wide],
        out_specs=[wide, kv_out, kv_out, kv_out, kv_out, small, small],
        out_shape=[jax.ShapeDtypeStruct((t, WIDTH_A), BF16), kv_shape, kv_shape, kv_shape, kv_shape,
                   jax.ShapeDtypeStruct((1, HEAD_PAD), F32), jax.ShapeDtypeStruct((1, HEAD_PAD), F32)],
        compiler_params=_cp("arbitrary", "arbitrary"))(proj, proj, proj, proj, proj, pos_col, pos_row, pos_row,
                                                       qg2, kg2, sinks, do)


def _swa_kv_bwd(proj, kg2, dkc, dkp, dvc, dvp, n_batch, s_len, name):
    t = proj.shape[0]
    nb = s_len // BLOCK

    def rowblk(b, i):
        return b * nb + i

    def nextblk(b, i):
        return b * nb + jnp.minimum(i + 1, nb - 1)

    def body(k_ref, kg_ref, dkc_ref, dkp_ref, dvc_ref, dvp_ref, dk_ref, dv_ref, dkg_ref):
        b, i = pl.program_id(0), pl.program_id(1)

        @pl.when((b == 0) & (i == 0))
        def _():
            dkg_ref[...] = jnp.zeros_like(dkg_ref)

        lo = lax.broadcasted_iota(jnp.int32, (1, HEAD_PAD), 1) < HEAD_DIM_A
        has_next = (i < nb - 1).astype(F32)
        dkn = dkc_ref[...] + has_next * dkp_ref[...]
        dv_ref[...] = (dvc_ref[...] + has_next * dvp_ref[...]).astype(BF16)
        _, xh, r = _norm2(k_ref[...], kg_ref[...], lo)
        dkg_ref[...] += jnp.sum(dkn * xh, axis=0, keepdims=True)
        dk_ref[...] = _norm2_bwd(dkn, xh, r, kg_ref[...], lo).astype(BF16)

    cur = pl.BlockSpec((BLOCK, KV_A), lambda b, i: (rowblk(b, i), 0))
    nxt = pl.BlockSpec((BLOCK, KV_A), lambda b, i: (nextblk(b, i), 0))
    small = pl.BlockSpec((1, HEAD_PAD), lambda b, i: (0, 0))
    return _tc_call(
        body, name=name, grid=(n_batch, nb),
        in_specs=[pl.BlockSpec((BLOCK, KV_A), lambda b, i: (rowblk(b, i), P_KA // KV_A)), small, cur, nxt, cur, nxt],
        out_specs=[cur, cur, small],
        out_shape=[jax.ShapeDtypeStruct((t, KV_A), BF16), jax.ShapeDtypeStruct((t, KV_A), BF16),
                   jax.ShapeDtypeStruct((1, HEAD_PAD), F32)],
        compiler_params=_cp("arbitrary", "arbitrary"))(proj, kg2, dkc, dkp, dvc, dvp)


def _rope(u, c, sm, sp):
    return u * c + pltpu.roll(u, HEAD_PAD - ROPE // 2, 1) * sm + pltpu.roll(u, ROPE // 2, 1) * sp


def _rope_t(d, c, sm, sp):
    return d * c + pltpu.roll(d * sm, ROPE // 2, 1) + pltpu.roll(d * sp, HEAD_PAD - ROPE // 2, 1)


def _pad_norm(x, gain):
    r = lax.rsqrt(jnp.sum(x * x, axis=1, keepdims=True) * (1.0 / QK_B) + EPS)
    xh = x * r
    return xh * gain, xh, r


def _pad_norm_bwd(d, xh, r, gain):
    dxh = d * gain
    return r * (dxh - xh * (jnp.sum(dxh * xh, axis=1, keepdims=True) * (1.0 / QK_B)))


def _lora_norm(x, gain):
    r = lax.rsqrt(jnp.mean(x * x, axis=1, keepdims=True) + EPS)
    xh = x * r
    return xh * gain, xh, r


def _mla_in_specs(tm):
    row = lambda w, off: pl.BlockSpec((tm, w), lambda i: (i, off // w))
    one = lambda w: pl.BlockSpec((1, w), lambda i: (0, 0))
    full = lambda r, c: pl.BlockSpec((r, c), lambda i: (0, 0))
    tab = pl.BlockSpec((tm, HEAD_PAD), lambda i: (i, 0))
    return [row(Q_LORA, P_CQ), row(KV_LORA, P_CKV), row(HEAD_PAD, P_KR), tab, tab, tab,
            one(Q_LORA), one(KV_LORA), one(HEAD_PAD), one(HEAD_PAD),
            full(WIDTH_BP, Q_LORA), full(WIDTH_BP, KV_LORA), full(WIDTH_BP, KV_LORA)]


def _mla_pre(proj, tabs, gq, gkv, gqn, gkn, wuq, wk, wv, name):
    t = proj.shape[0]
    tm = _tile(t, (256, 128))

    def body(cq_ref, ckv_ref, kr_ref, c_ref, sm_ref, sp_ref, gq_ref, gkv_ref, gqn_ref, gkn_ref,
             wuq_ref, wk_ref, wv_ref, q_ref, k_ref, v_ref):
        cqn = _lora_norm(cq_ref[...], gq_ref[...])[0].astype(BF16)
        ckvn = _lora_norm(ckv_ref[...], gkv_ref[...])[0].astype(BF16)
        q_raw = _dot(cqn, wuq_ref[...], _NT)
        k_raw = _dot(ckvn, wk_ref[...], _NT)
        v_ref[...] = _dot(ckvn, wv_ref[...], _NT).astype(BF16)
        kr = pltpu.roll(kr_ref[...], NOPE, 1)
        c, sm, sp = c_ref[...], sm_ref[...], sp_ref[...]
        for h in range(N_HEADS_B):
            hs = slice(h * HEAD_PAD, (h + 1) * HEAD_PAD)
            q_ref[:, hs] = _rope(_pad_norm(q_raw[:, hs], gqn_ref[...])[0], c, sm, sp).astype(BF16)
            k_ref[:, hs] = _rope(_pad_norm(k_raw[:, hs] + kr, gkn_ref[...])[0], c, sm, sp).astype(BF16)

    o_spec = pl.BlockSpec((tm, WIDTH_BP), lambda i: (i, 0))
    o_shape = jax.ShapeDtypeStruct((t, WIDTH_BP), BF16)
    return _tc_call(
        body, name=name, grid=(t // tm,), in_specs=_mla_in_specs(tm), out_specs=[o_spec] * 3,
        out_shape=[o_shape] * 3, compiler_params=_cp("parallel"))(
            proj, proj, proj, *tabs, gq, gkv, gqn, gkn, wuq, wk, wv)


def _mla_pre_bwd(proj, tabs, gq, gkv, gqn, gkn, wuq, wk, wv, dq, dk, dv, name):
    t = proj.shape[0]
    tm = _tile(t, (256, 128))

    def body(cq_ref, ckv_ref, kr_ref, c_ref, sm_ref, sp_ref, gq_ref, gkv_ref, gqn_ref, gkn_ref,
             wuq_ref, wk_ref, wv_ref, dq_ref, dk_ref, dv_ref,
             dcq_ref, dckv_ref, dkr_ref, dwuq_ref, dwk_ref, dwv_ref, dgq_ref, dgkv_ref, dgqn_ref, dgkn_ref,
             dqraw_ref, dkraw_ref):
        @pl.when(pl.program_id(0) == 0)
        def _():
            for r in (dwuq_ref, dwk_ref, dwv_ref, dgq_ref, dgkv_ref, dgqn_ref, dgkn_ref):
                r[...] = jnp.zeros_like(r)

        cqn_f, cq_xh, cq_r = _lora_norm(cq_ref[...], gq_ref[...])
        ckvn_f, ckv_xh, ckv_r = _lora_norm(ckv_ref[...], gkv_ref[...])
        cqn, ckvn = cqn_f.astype(BF16), ckvn_f.astype(BF16)
        q_raw = _dot(cqn, wuq_ref[...], _NT)
        k_raw = _dot(ckvn, wk_ref[...], _NT)
        kr = pltpu.roll(kr_ref[...], NOPE, 1)
        c, sm, sp = c_ref[...], sm_ref[...], sp_ref[...]
        dgqn = jnp.zeros((1, HEAD_PAD), F32)
        dgkn = jnp.zeros((1, HEAD_PAD), F32)
        dkr = jnp.zeros((tm, HEAD_PAD), F32)
        for h in range(N_HEADS_B):
            hs = slice(h * HEAD_PAD, (h + 1) * HEAD_PAD)
            _, xh, r = _pad_norm(q_raw[:, hs], gqn_ref[...])
            dn = _rope_t(dq_ref[:, hs], c, sm, sp)
            dgqn = dgqn + jnp.sum(dn * xh, axis=0, keepdims=True)
            dqraw_ref[:, hs] = _pad_norm_bwd(dn, xh, r, gqn_ref[...]).astype(BF16)
            _, xh, r = _pad_norm(k_raw[:, hs] + kr, gkn_ref[...])
            dn = _rope_t(dk_ref[:, hs], c, sm, sp)
            dgkn = dgkn + jnp.sum(dn * xh, axis=0, keepdims=True)
            dkc = _pad_norm_bwd(dn, xh, r, gkn_ref[...])
            dkraw_ref[:, hs] = dkc.astype(BF16)
            dkr = dkr + dkc
        dgqn_ref[...] += dgqn
        dgkn_ref[...] += dgkn
        lane = lax.broadcasted_iota(jnp.int32, (tm, HEAD_PAD), 1)
        dkr_ref[...] = jnp.where(lane < ROPE, pltpu.roll(dkr, HEAD_PAD - NOPE, 1), 0.0).astype(BF16)
        dqraw = dqraw_ref[...]
        dkraw = dkraw_ref[...]
        dvb = dv_ref[...].astype(BF16)
        dwuq_ref[...] += _dot(dqraw, cqn, _TN)
        dwk_ref[...] += _dot(dkraw, ckvn, _TN)
        dwv_ref[...] += _dot(dvb, ckvn, _TN)
        dcqn = _dot(dqraw, wuq_ref[...], _NN)
        dckvn = _dot(dkraw, wk_ref[...], _NN) + _dot(dvb, wv_ref[...], _NN)
        dgq_ref[...] += jnp.sum(dcqn * cq_xh, axis=0, keepdims=True)
        dgkv_ref[...] += jnp.sum(dckvn * ckv_xh, axis=0, keepdims=True)
        dxh = dcqn * gq_ref[...]
        dcq_ref[...] = (cq_r * (dxh - cq_xh * jnp.mean(dxh * cq_xh, axis=1, keepdims=True))).astype(BF16)
        dxh = dckvn * gkv_ref[...]
        dckv_ref[...] = (ckv_r * (dxh - ckv_xh * jnp.mean(dxh * ckv_xh, axis=1, keepdims=True))).astype(BF16)

    wide = pl.BlockSpec((tm, WIDTH_BP), lambda i: (i, 0))
    row = lambda w: pl.BlockSpec((tm, w), lambda i: (i, 0))
    full = lambda r, c: pl.BlockSpec((r, c), lambda i: (0, 0))
    return _tc_call(
        body, name=name, grid=(t // tm,), in_specs=_mla_in_specs(tm) + [wide, wide, wide],
        out_specs=[row(Q_LORA), row(KV_LORA), row(HEAD_PAD), full(WIDTH_BP, Q_LORA), full(WIDTH_BP, KV_LORA),
                   full(WIDTH_BP, KV_LORA), full(1, Q_LORA), full(1, KV_LORA), full(1, HEAD_PAD), full(1, HEAD_PAD)],
        out_shape=[jax.ShapeDtypeStruct((t, Q_LORA), BF16), jax.ShapeDtypeStruct((t, KV_LORA), BF16),
                   jax.ShapeDtypeStruct((t, HEAD_PAD), BF16), jax.ShapeDtypeStruct((WIDTH_BP, Q_LORA), F32),
                   jax.ShapeDtypeStruct((WIDTH_BP, KV_LORA), F32), jax.ShapeDtypeStruct((WIDTH_BP, KV_LORA), F32),
                   jax.ShapeDtypeStruct((1, Q_LORA), F32), jax.ShapeDtypeStruct((1, KV_LORA), F32),
                   jax.ShapeDtypeStruct((1, HEAD_PAD), F32), jax.ShapeDtypeStruct((1, HEAD_PAD), F32)],
        scratch_shapes=[pltpu.VMEM((tm, WIDTH_BP), BF16), pltpu.VMEM((tm, WIDTH_BP), BF16)],
        compiler_params=_cp("arbitrary"))(proj, proj, proj, *tabs, gq, gkv, gqn, gkn, wuq, wk, wv, dq, dk, dv)


def _mla_flash_specs(s_len):
    bh_spec = pl.BlockSpec((s_len, HEAD_PAD), lambda b, h: (b, h))
    lse_spec = pl.BlockSpec((1, s_len, 1), lambda b, h: (b * N_HEADS_B + h, 0, 0))
    return bh_spec, lse_spec


def _diag_mask(s):
    row = lax.broadcasted_iota(jnp.int32, s.shape, 0)
    col = lax.broadcasted_iota(jnp.int32, s.shape, 1)
    return jnp.where(row >= col, s, NEG)


def _mla_flash_fwd(q, k, v, n_batch, s_len, name):
    t = q.shape[0]
    tq = _tile(s_len, (256, 128))
    bh_spec, lse_spec = _mla_flash_specs(s_len)
    c = (QK_B ** -0.5) * LOG2E

    def body(q_ref, k_ref, v_ref, o_ref, lse_ref):
        for i in range(s_len // tq):
            rows, below = slice(i * tq, (i + 1) * tq), slice(0, i * tq)
            qv = q_ref[rows, :]
            sd = _diag_mask(_dot(qv, k_ref[rows, :], _NT))
            m = jnp.max(sd, axis=1, keepdims=True)
            if i:
                sb = _dot(qv, k_ref[below, :], _NT)
                m = jnp.maximum(m, jnp.max(sb, axis=1, keepdims=True))
            pd = jnp.exp2((sd - m) * c)
            l = jnp.sum(pd, axis=1, keepdims=True)
            acc = _dot(pd.astype(BF16), v_ref[rows, :], _NN)
            if i:
                pb = jnp.exp2((sb - m) * c)
                l = l + jnp.sum(pb, axis=1, keepdims=True)
                acc = acc + _dot(pb.astype(BF16), v_ref[below, :], _NN)
            o_ref[rows, :] = (acc * (1.0 / l)).astype(BF16)
            lse_ref[0, rows, :] = m * c + jnp.log2(l)

    return _tc_call(
        body, name=name, grid=(n_batch, N_HEADS_B), in_specs=[bh_spec, bh_spec, bh_spec],
        out_specs=[bh_spec, lse_spec],
        out_shape=[jax.ShapeDtypeStruct((t, WIDTH_BP), BF16),
                   jax.ShapeDtypeStruct((n_batch * N_HEADS_B, s_len, 1), F32)],
        compiler_params=_cp("parallel", "parallel"))(q, k, v)


def _mla_flash_bwd(q, k, v, o, do, lse2, n_batch, s_len, name):
    t = q.shape[0]
    tq = _tile(s_len, (256, 128))
    bh_spec, lse_spec = _mla_flash_specs(s_len)
    scale = QK_B ** -0.5
    c = scale * LOG2E

    def body(q_ref, k_ref, v_ref, o_ref, do_ref, lse_ref, dq_ref, dk_ref, dv_ref):
        dk_ref[...] = jnp.zeros_like(dk_ref)
        dv_ref[...] = jnp.zeros_like(dv_ref)
        for i in range(s_len // tq):
            rows = slice(i * tq, (i + 1) * tq)
            qv, dov = q_ref[rows, :], do_ref[rows, :]
            lse = lse_ref[0, rows, :]
            delta = jnp.sum(dov.astype(F32) * o_ref[rows, :].astype(F32), axis=1, keepdims=True)

            def part(ks, diag):
                kblk = k_ref[ks, :]
                s = _dot(qv, kblk, _NT)
                if diag:
                    s = _diag_mask(s)
                p = jnp.exp2(s * c - lse)
                dp = _dot(dov, v_ref[ks, :], _NT)
                ds = (p * (dp - delta) * scale).astype(BF16)
                dk_ref[ks, :] += _dot(ds, qv, _TN)
                dv_ref[ks, :] += _dot(p.astype(BF16), dov, _TN)
                return _dot(ds, kblk, _NN)

            dq = part(rows, True)
            if i:
                dq = dq + part(slice(0, i * tq), False)
            dq_ref[rows, :] = dq

    f32_wide = jax.ShapeDtypeStruct((t, WIDTH_BP), F32)
    return _tc_call(
        body, name=name, grid=(n_batch, N_HEADS_B),
        in_specs=[bh_spec, bh_spec, bh_spec, bh_spec, bh_spec, lse_spec],
        out_specs=[bh_spec, bh_spec, bh_spec], out_shape=[f32_wide] * 3,
        compiler_params=_cp("parallel", "parallel"))(q, k, v, o, do, lse2)


def _swa_heads(w, axis, order):
    heads = [lax.slice_in_dim(w, h * HEAD_DIM_A, (h + 1) * HEAD_DIM_A, axis=axis) for h in order]
    return jnp.concatenate(heads, axis=axis)


class _LayerWeights:
    def __init__(self, build):
        self._build, self._mats = build, {}

    def __getitem__(self, name):
        if name not in self._mats:
            self._mats.update(self._build(name))
        return self._mats[name]


def _layer_mats(w):
    if "w_in" not in w:
        return dict(w)
    w_in = w["w_in"]
    o = [0]
    for n in (WIDTH_A, KV_A, KV_A, Q_LORA, KV_LORA, ROPE, D_MODEL, D_MODEL):
        o.append(o[-1] + n)
    qa, ka, va, cq, ckv, kr, ga, gb = (w_in[o[i]:o[i + 1]] for i in range(8))
    pad = jnp.zeros((PROJ_W - IN_WIDTH, w_in.shape[1]), w_in.dtype)
    w_in_p = jnp.concatenate([ga, gb, _swa_heads(qa, 0, SWA_HEAD_ORDER), cq, ka, va, ckv, kr, pad], axis=0)
    uq = w["mla_w_uq"].reshape(N_HEADS_B, QK_B, Q_LORA)
    uq = jnp.pad(uq, ((0, 0), (0, HEAD_PAD - QK_B), (0, 0))).reshape(WIDTH_BP, Q_LORA)
    ukv = w["mla_w_ukv"].reshape(N_HEADS_B, NOPE + V_B, KV_LORA)
    wk = jnp.pad(ukv[:, :NOPE], ((0, 0), (0, HEAD_PAD - NOPE), (0, 0))).reshape(WIDTH_BP, KV_LORA)
    wv = jnp.pad(ukv[:, NOPE:], ((0, 0), (0, HEAD_PAD - V_B), (0, 0))).reshape(WIDTH_BP, KV_LORA)
    wb = w["w_branch_b"].reshape(D_MODEL, N_HEADS_B, V_B)
    wb = jnp.pad(wb, ((0, 0), (0, 0), (0, HEAD_PAD - V_B))).reshape(D_MODEL, WIDTH_BP)
    out = dict(w)
    out.update(w_in=w_in_p, mla_w_uq=uq, wk=wk, wv=wv, w_branch_b=wb,
               w_branch_a=_swa_heads(w["w_branch_a"], 1, SWA_HEAD_ORDER))
    return out


def _unlayer_grads(g):
    if "w_in" not in g:
        return g
    d = g["w_in"]
    ga, gb, qa, cq, ka, va, ckv, kr = (d[a:b] for a, b in (
        (P_GA, P_GA + D_MODEL), (P_GB, P_GB + D_MODEL), (P_QA, P_QA + WIDTH_A), (P_CQ, P_CQ + Q_LORA),
        (P_KA, P_KA + KV_A), (P_VA, P_VA + KV_A), (P_CKV, P_CKV + KV_LORA), (P_KR, P_KR + ROPE)))
    out = {n: v for n, v in g.items() if n not in ("wk", "wv")}
    out["w_in"] = jnp.concatenate([_swa_heads(qa, 0, SWA_HEAD_INVERSE), ka, va, cq, ckv, kr, ga, gb], axis=0)
    out["mla_w_uq"] = g["mla_w_uq"].reshape(N_HEADS_B, HEAD_PAD, Q_LORA)[:, :QK_B].reshape(N_HEADS_B * QK_B, Q_LORA)
    dk = g["wk"].reshape(N_HEADS_B, HEAD_PAD, KV_LORA)[:, :NOPE]
    dv = g["wv"].reshape(N_HEADS_B, HEAD_PAD, KV_LORA)[:, :V_B]
    out["mla_w_ukv"] = jnp.concatenate([dk, dv], axis=1).reshape(N_HEADS_B * (NOPE + V_B), KV_LORA)
    out["w_branch_b"] = g["w_branch_b"].reshape(D_MODEL, N_HEADS_B, HEAD_PAD)[:, :, :V_B].reshape(D_MODEL, WIDTH_B)
    out["w_branch_a"] = _swa_heads(g["w_branch_a"], 1, SWA_HEAD_INVERSE)
    return out


def _pad_lanes(v, width):
    return jnp.pad(v.reshape(1, -1), ((0, 0), (0, width - v.shape[-1])))


def _rope_tables(positions):
    half = ROPE // 2
    inv_freq = ROPE_BASE ** (-jnp.arange(half, dtype=F32) / half)
    ang = positions.astype(F32).reshape(-1, 1) * inv_freq
    cos, sin = jnp.cos(ang), jnp.sin(ang)
    t = cos.shape[0]
    one, zero = jnp.ones((t, NOPE), F32), jnp.zeros((t, NOPE), F32)
    tail = jnp.zeros((t, HEAD_PAD - QK_B), F32)
    z16 = jnp.zeros((t, half), F32)
    c = jnp.concatenate([one, cos, cos, tail], axis=1)
    sm = jnp.concatenate([zero, -sin, z16, tail], axis=1)
    sp = jnp.concatenate([zero, z16, sin, tail], axis=1)
    return c, sm, sp


def _ffn_fwd(x, gain, wg_t, wu_t, wd, tag):
    n, a, b, hmid = _ffn_up(x, gain, wg_t, wu_t, f"{tag}_up")
    y = _mm([(hmid, wd)], "nn", F32, f"{tag}_down", residual=x, alpha=0.5)
    return y, (x, n, a, b, hmid)


def _ffn_bwd(dy, dyb, saved, gain, wg_t, wu_t, wd, tag, grads, names, hook):
    x, n, a, b, hmid = saved
    da, db = _ffn_down_bwd(dyb, wd, a, b, f"{tag}_down_bwd")
    grads[names[0]] = _mm([(da, n)], "tn", BF16, f"{tag}_dwg")
    grads[names[1]] = _mm([(db, n)], "tn", BF16, f"{tag}_dwu")
    hook("gu", grads[names[1]])
    dx, dxb, g_gain = _mm_rms_bwd([(da, wg_t), (db, wu_t)], x, gain, dy, f"{tag}_dn")
    hook("dn", dx)
    grads[names[2]] = _mm([(hmid, dyb)], "tn", BF16, f"{tag}_dwd", alpha=0.5)
    return dx, dxb, g_gain


def _fold_halves(d):
    return d[:, :HEAD_DIM_A] + d[:, HEAD_DIM_A:]


def _local_step(x, positions, target, layers, smalls, at=None):
    at = at or (lambda point, l, token, grads: None)
    _Order.tokens = ()
    n_batch, s_len, d = x.shape
    t = n_batch * s_len
    xt = x.reshape(t, d)
    tabs = _rope_tables(positions)
    pos_col = positions.reshape(t, 1)
    pos_row = positions.reshape(t // BLOCK, 1, BLOCK)
    saved = []
    get_layer = layers if callable(layers) else layers.__getitem__
    for l in range(len(smalls)):
        w, s = get_layer(l), smalls[l]
        g1, gm, g2 = (s[k].reshape(1, d) for k in ("ffn1_norm", "mix_norm", "ffn2_norm"))
        qg2, kg2 = (jnp.tile(s[k].reshape(1, -1), (1, 2)) for k in ("swa_q_norm", "swa_k_norm"))
        sinks = _pad_lanes(s["swa_sinks"], HEAD_PAD)
        gq, gkv = s["mla_q_lora_norm"].reshape(1, -1), s["mla_kv_lora_norm"].reshape(1, -1)
        gqn, gkn = _pad_lanes(s["mla_q_norm"], HEAD_PAD), _pad_lanes(s["mla_k_norm"], HEAD_PAD)
        x1, sv1 = _ffn_fwd(xt, g1, w["ffn1_w_gate"], w["ffn1_w_up"], w["ffn1_w_down"], f"l{l}_ffn1")
        h, proj = _rms_mm(x1, gm, w["w_in"], f"l{l}_proj")
        at("proj", l, proj, None)
        oa = _swa_fwd(proj, pos_col, pos_row, qg2, kg2, sinks, n_batch, s_len, f"l{l}_swa")
        q, k, v = _mla_pre(proj, tabs, gq, gkv, gqn, gkn, w["mla_w_uq"], w["wk"], w["wv"], f"l{l}_mla_pre")
        ob, lse = _mla_flash_fwd(q, k, v, n_batch, s_len, f"l{l}_mla")
        merged, ya, yb = _merge_fwd(oa, ob, proj, w["w_branch_a"], w["w_branch_b"], f"l{l}_merge")
        x2 = _mm([(merged, w["w_out"])], "nn", F32, f"l{l}_out", residual=x1)
        at("out", l, x2, None)
        x3, sv2 = _ffn_fwd(x2, g2, w["ffn2_w_gate"], w["ffn2_w_up"], w["ffn2_w_down"], f"l{l}_ffn2")
        saved.append((w, sv1, sv2, x1, h, proj, oa, q, k, v, ob, lse, merged, ya, yb,
                      (g1, gm, g2, qg2, kg2, sinks, gq, gkv, gqn, gkn)))
        xt = x3

    dy, dyb, loss = _loss_head(xt, target.reshape(t, d))

    big_grads, small_grads = [None] * len(smalls), [None] * len(smalls)
    for l in reversed(range(len(smalls))):
        w, sv1, sv2, x1, h, proj, oa, q, k, v, ob, lse, merged, ya, yb, gains = saved[l]
        g1, gm, g2, qg2, kg2, sinks, gq, gkv, gqn, gkn = gains
        bg, sg = {}, {}
        dy, dyb, sg["ffn2_norm"] = _ffn_bwd(
            dy, dyb, sv2, g2, w["ffn2_w_gate"], w["ffn2_w_up"], w["ffn2_w_down"], f"l{l}_ffn2", bg, FFN2,
            lambda point, token, l=l, bg=bg: at("ffn2_" + point, l, token, bg))
        dya, dyb_, dga, dgb = _merge_bwd(dyb, w["w_out"], proj, ya, yb, f"l{l}_merge_bwd")
        at("mixer", l, sg["ffn2_norm"], bg)
        bg["w_out"] = _mm([(merged, dyb)], "tn", BF16, f"l{l}_dwo")
        doa = _mm([(dya, w["w_branch_a"])], "nn", BF16, f"l{l}_doa")
        bg["w_branch_a"] = _mm([(dya, oa)], "tn", BF16, f"l{l}_dwa")
        dob = _mm([(dyb_, w["w_branch_b"])], "nn", BF16, f"l{l}_dob")
        bg["w_branch_b"] = _mm([(dyb_, ob)], "tn", BF16, f"l{l}_dwb")
        dqa, dkc, dkp, dvc, dvp, dqg, dsk = _swa_bwd(
            proj, pos_col, pos_row, qg2, kg2, sinks, doa, n_batch, s_len, f"l{l}_swa_bwd")
        sg["swa_q_norm"], sg["swa_sinks"] = _fold_halves(dqg), dsk[:, :N_HEADS_A]
        at("mixer_mid", l, dqa, bg)
        dka, dva, dkg = _swa_kv_bwd(proj, kg2, dkc, dkp, dvc, dvp, n_batch, s_len, f"l{l}_swa_kv_bwd")
        sg["swa_k_norm"] = _fold_halves(dkg)
        dq, dk, dv = _mla_flash_bwd(q, k, v, ob, dob, lse, n_batch, s_len, f"l{l}_mla_bwd")
        (dcq, dckv, dkr, g_uq, g_wk, g_wv, sg["mla_q_lora_norm"], sg["mla_kv_lora_norm"], dgqn, dgkn) = _mla_pre_bwd(
            proj, tabs, gq, gkv, gqn, gkn, w["mla_w_uq"], w["wk"], w["wv"], dq, dk, dv, f"l{l}_mla_pre_bwd")
        sg["mla_q_norm"], sg["mla_k_norm"] = dgqn[:, :QK_B], dgkn[:, :QK_B]
        bg["mla_w_uq"], bg["wk"], bg["wv"] = g_uq.astype(BF16), g_wk.astype(BF16), g_wv.astype(BF16)
        dproj = jnp.concatenate([dga, dgb, dqa, dcq, dka, dva, dckv, dkr], axis=1)
        bg["w_in"] = _mm([(dproj, h)], "tn", BF16, f"l{l}_dwin")
        dy, dyb, sg["mix_norm"] = _mm_rms_bwd([(dproj, w["w_in"])], x1, gm, dy, f"l{l}_dh")
        at("ffn1", l, sg["mix_norm"], bg)
        dy, dyb, sg["ffn1_norm"] = _ffn_bwd(
            dy, dyb, sv1, g1, w["ffn1_w_gate"], w["ffn1_w_up"], w["ffn1_w_down"], f"l{l}_ffn1", bg, FFN1,
            lambda point, token, l=l, bg=bg: at("ffn1_" + point, l, token, bg))
        big_grads[l], small_grads[l] = bg, sg
        at("done", l, dy, bg)
    return loss, dy.reshape(n_batch, s_len, d), big_grads, small_grads


def _round_up(n, m):
    return (n + m - 1) // m * m


def _flat_rows(shape, transposed):
    rows, k = (shape[1], shape[0]) if transposed else shape
    return _round_up(rows * k // LANES, 16), rows, k


def _flat_layout(shard_shapes, members, row_tile):
    table, off = [], 0
    for l, name in members:
        tr = dict(BIG)[name]
        pr, rows, k = _flat_rows(shard_shapes[name], tr)
        table.append(((l, name), tr, off, pr, rows, k))
        off += pr
    return table, _round_up(off, row_tile)


def _pack_flat(params, table, total):
    parts, off = [], 0
    for (l, name), tr, o, pr, rows, k in table:
        w = params[name][l]
        w = (w.T if tr else w).reshape(rows * k // LANES, LANES)
        parts.append(jnp.pad(w, ((0, pr - w.shape[0]), (0, 0))))
        off = o + pr
    if total > off:
        parts.append(jnp.zeros((total - off, LANES), parts[0].dtype))
    return jnp.concatenate(parts, axis=0)


def _unpack_flat(flat, table):
    return {key: flat[o:o + rows * k // LANES].reshape(rows, k) for key, tr, o, pr, rows, k in table}


def _gathered_mats(gathered, table, layer):
    return {name: gathered[:, o:o + rows * k // LANES].reshape(N_DEV * rows, k)
            for (l, name), tr, o, pr, rows, k in table if l == layer}


def _pack_grads(grads, table, total):
    parts, off = [], 0
    for key, tr, o, pr, rows, k in table:
        g = grads[key].reshape(N_DEV, rows * k // LANES, LANES)
        parts.append(jnp.pad(g, ((0, 0), (0, pr - g.shape[1]), (0, 0))))
        off = o + pr
    if total > off:
        parts.append(jnp.zeros((N_DEV, total - off, LANES), BF16))
    return jnp.concatenate(parts, axis=1)


def _pack_small(params, last=None):
    parts = [params[n][l].reshape(-1).astype(F32) for l in range(DEPTH) for n in SMALL]
    v = jnp.concatenate(parts)
    v = jnp.pad(v, (0, SMALL_ROWS * LANES - 1 - v.shape[0]))
    last = jnp.zeros((1,), F32) if last is None else last.reshape(1)
    return jnp.concatenate([v, last]).reshape(SMALL_ROWS, LANES)


def _unpack_small(flat, shapes):
    v, out, off = flat.reshape(-1), {}, 0
    for l in range(DEPTH):
        for n in SMALL:
            sz = math.prod(shapes[n][1:])
            out.setdefault(n, []).append(v[off:off + sz].reshape(shapes[n][1:]))
            off += sz
    return {n: jnp.stack(p) for n, p in out.items()}


_MESH = pl.DeviceIdType.MESH


def _place():
    return lax.axis_index("x"), lax.axis_index("y"), lax.axis_index("c")


def _handshake(peers):
    barrier = pltpu.get_barrier_semaphore()
    for peer in peers:
        pl.semaphore_signal(barrier, inc=1, device_id=peer, device_id_type=_MESH)
    pl.semaphore_wait(barrier, len(peers))


def _comm_call(body, out_shape, scratch, name, seq_id, spec=_ANY):
    if seq_id is None:
        return pl.pallas_call(body, name=name, out_shape=out_shape, in_specs=[spec, _ANY], out_specs=spec,
                              scratch_shapes=scratch)
    nbytes = LINK_COST_SCALE * math.prod(out_shape.shape) * out_shape.dtype.itemsize
    return pl.kernel(body, out_type=out_shape, mesh=plsc.ScalarSubcoreMesh(axis_name="sequencer", num_cores=1),
                     scratch_types=scratch, name=name, compiler_params=pltpu.CompilerParams(collective_id=seq_id),
                     cost_estimate=pl.CostEstimate(flops=0, transcendentals=0, bytes_accessed=nbytes))


def _all_gather(x_shard, name, vmem=False, seq_id=None, after=None):
    spec = pl.BlockSpec(memory_space=pltpu.VMEM) if vmem else _ANY

    def body(x_ref, after_ref, out_ref, send_sems, recv_sems, local_sem):
        x, y, c = _place()
        me, sibling = (x, y, c), (x, y, 1 - c)
        chips = [(1 - x, y), (x, 1 - y), (1 - x, 1 - y)]
        if seq_id is not None:
            _handshake([sibling] + [(*chip, c) for chip in chips])

        def rows(px, py, pc):
            return out_ref.at[4 * px + 2 * py + pc]

        def copy(k, block, to, src=None):
            return pltpu.make_async_remote_copy(
                src_ref=rows(*block) if src is None else src, dst_ref=rows(*block),
                send_sem=send_sems.at[k], recv_sem=recv_sems.at[k], device_id=to, device_id_type=_MESH)

        mine = pltpu.make_async_copy(x_ref, rows(*me), local_sem)
        mine.start()
        first = [copy(0, me, sibling, src=x_ref)]
        first += [copy(1 + j, me, (*chip, c), src=x_ref) for j, chip in enumerate(chips)]
        for cp in first:
            cp.start()
        passed = [copy(4 + j, (*chip, c), sibling) for j, chip in enumerate(chips)]
        for j, chip in enumerate(chips):
            copy(1 + j, (*chip, c), me).wait_recv()
            passed[j].start()
        copy(0, sibling, me).wait_recv()
        for j, chip in enumerate(chips):
            copy(4 + j, (*chip, 1 - c), me).wait_recv()
        for cp in first + passed:
            cp.wait_send()
        mine.wait()

    return _comm_call(
        body, jax.ShapeDtypeStruct((N_DEV,) + x_shard.shape, x_shard.dtype),
        [pltpu.SemaphoreType.DMA((7,)), pltpu.SemaphoreType.DMA((7,)), pltpu.SemaphoreType.DMA], name, seq_id,
        spec)(x_shard, x_shard if after is None else after)


def _exchange_cores(g4, name, seq_id=None, after=None):
    n_chip, _, r, w = g4.shape

    def body(g_ref, after_ref, out_ref, send_sems, recv_sems):
        x, y, c = _place()
        if seq_id is not None:
            _handshake([(x, y, 1 - c)])
        copies = [pltpu.make_async_remote_copy(
            src_ref=g_ref.at[q, 1 - c], dst_ref=out_ref.at[q], send_sem=send_sems.at[q], recv_sem=recv_sems.at[q],
            device_id=(x, y, 1 - c), device_id_type=_MESH) for q in range(n_chip)]
        for cp in copies:
            cp.start()
        for cp in copies:
            cp.wait()

    return _comm_call(
        body, jax.ShapeDtypeStruct((n_chip, r, w), g4.dtype),
        [pltpu.SemaphoreType.DMA((n_chip,)), pltpu.SemaphoreType.DMA((n_chip,))], name, seq_id)(g4, g4 if after is None else after)


def _exchange_chips(s1, name, seq_id=None):
    _, r, w = s1.shape

    def body(s_ref, after_ref, out_ref, send_sems, recv_sems):
        x, y, c = _place()
        chips = [(1 - x, y), (x, 1 - y), (1 - x, 1 - y)]
        if seq_id is not None:
            _handshake([(*chip, c) for chip in chips])
        copies = []
        for k, (tx, ty) in enumerate(chips):
            copies.append(pltpu.make_async_remote_copy(
                src_ref=s_ref.at[2 * tx + ty], dst_ref=out_ref.at[k], send_sem=send_sems.at[k],
                recv_sem=recv_sems.at[k], device_id=(tx, ty, c), device_id_type=_MESH))
        for cp in copies:
            cp.start()
        for cp in copies:
            cp.wait()

    return _comm_call(
        body, jax.ShapeDtypeStruct((3, r, w), s1.dtype),
        [pltpu.SemaphoreType.DMA((3,)), pltpu.SemaphoreType.DMA((3,))], name, seq_id)(s1, s1)


def _chip_sum(g4, recv, core, after, name, tr):
    n_chip, _, r, w = g4.shape

    def body(c_ref, a_ref, b_ref, after_ref, o_ref):
        o_ref[...] = (a_ref[...].astype(F32) + b_ref[...].astype(F32)).astype(o_ref.dtype)

    grid_spec = pltpu.PrefetchScalarGridSpec(
        num_scalar_prefetch=1, grid=(n_chip, r // tr),
        in_specs=[pl.BlockSpec((None, None, tr, w), lambda q, i, c: (q, c[0], i, 0)),
                  pl.BlockSpec((None, tr, w), lambda q, i, c: (q, i, 0)), _ANY],
        out_specs=pl.BlockSpec((None, tr, w), lambda q, i, c: (q, i, 0)))
    return pl.pallas_call(
        body, name=name, grid_spec=grid_spec, out_shape=jax.ShapeDtypeStruct((n_chip, r, w), g4.dtype),
        compiler_params=_cp("parallel", "parallel"))(core, g4, recv, after)


def _adam(w, g, m, v):
    m = ADAM_B1 * m + (1.0 - ADAM_B1) * g
    v = ADAM_B2 * v + (1.0 - ADAM_B2) * (g * g)
    m_hat = m / (1.0 - ADAM_B1 ** ADAM_STEP)
    v_hat = v / (1.0 - ADAM_B2 ** ADAM_STEP)
    delta = -ADAM_LR * (m_hat / (jnp.sqrt(v_hat) + ADAM_EPS) + ADAM_WD * w)
    return delta, m, v


def _grad_sum(s1, r2, chip, name, tr):
    _, r, lanes = s1.shape

    def body(c_ref, s_ref, r0_ref, r1_ref, r2_ref, g_out):
        g_out[...] = ((s_ref[...].astype(F32) + r0_ref[...].astype(F32)) + r1_ref[...].astype(F32)) + r2_ref[
            ...].astype(F32)

    row = pl.BlockSpec((tr, lanes), lambda i, c: (i, 0))
    rel = lambda k: pl.BlockSpec((None, tr, lanes), lambda i, c: (k, i, 0))
    grid_spec = pltpu.PrefetchScalarGridSpec(
        num_scalar_prefetch=1, grid=(r // tr,),
        in_specs=[pl.BlockSpec((None, tr, lanes), lambda i, c: (c[0], i, 0)), rel(0), rel(1), rel(2)], out_specs=row)
    return pl.pallas_call(
        body, name=name, grid_spec=grid_spec, out_shape=jax.ShapeDtypeStruct((r, lanes), F32),
        compiler_params=_cp("parallel"))(chip, s1, r2, r2, r2)


def _adam_big(w, g, m, v, name):
    depth, k, n = w.shape
    tk = k if k <= 512 else _tile(k, (256, 128))

    def body(w_ref, g_ref, m_ref, v_ref, d_out, m_out, v_out):
        d, mn, vn = _adam(w_ref[...], g_ref[...], m_ref[...], v_ref[...])
        d_out[...] = d
        m_out[...] = mn
        v_out[...] = vn

    blk = pl.BlockSpec((None, tk, n), lambda l, i: (l, i, 0))
    return pl.pallas_call(
        body, name=name, grid=(depth, k // tk), in_specs=[blk] * 4, out_specs=[blk] * 3,
        out_shape=[jax.ShapeDtypeStruct(w.shape, F32)] * 3, compiler_params=_cp("parallel", "parallel"))(w, g, m, v)


def _adam_small(parts, w, m, v, name):
    rows, lanes = w.shape

    def body(p_ref, w_ref, m_ref, v_ref, g_out, d_out, m_out, v_out):
        g = p_ref[0:rows, :]
        for dev in range(1, N_DEV):
            g = g + p_ref[dev * rows:(dev + 1) * rows, :]
        d, mn, vn = _adam(w_ref[...], g, m_ref[...], v_ref[...])
        g_out[...] = g
        d_out[...] = d
        m_out[...] = mn
        v_out[...] = vn

    return pl.pallas_call(
        body, name=name, out_shape=[jax.ShapeDtypeStruct((rows, lanes), F32)] * 4)(parts, w, m, v)


def kernel(x, positions, ffn1_norm, ffn1_w_gate, ffn1_w_up, ffn1_w_down, mix_norm, w_in, swa_q_norm, swa_k_norm, swa_sinks, mla_q_lora_norm, mla_w_uq, mla_kv_lora_norm, mla_w_ukv, mla_q_norm, mla_k_norm, w_branch_a, w_branch_b, w_out, ffn2_norm, ffn2_w_gate, ffn2_w_up, ffn2_w_down, loss_target, m_ffn1_norm, m_ffn1_w_gate, m_ffn1_w_up, m_ffn1_w_down, m_mix_norm, m_w_in, m_swa_q_norm, m_swa_k_norm, m_swa_sinks, m_mla_q_lora_norm, m_mla_w_uq, m_mla_kv_lora_norm, m_mla_w_ukv, m_mla_q_norm, m_mla_k_norm, m_w_branch_a, m_w_branch_b, m_w_out, m_ffn2_norm, m_ffn2_w_gate, m_ffn2_w_up, m_ffn2_w_down, v_ffn1_norm, v_ffn1_w_gate, v_ffn1_w_up, v_ffn1_w_down, v_mix_norm, v_w_in, v_swa_q_norm, v_swa_k_norm, v_swa_sinks, v_mla_q_lora_norm, v_mla_w_uq, v_mla_kv_lora_norm, v_mla_w_ukv, v_mla_q_norm, v_mla_k_norm, v_w_branch_a, v_w_branch_b, v_w_out, v_ffn2_norm, v_ffn2_w_gate, v_ffn2_w_up, v_ffn2_w_down):
    given = dict(locals())
    params = {n: given[n] for n in WEIGHTS}
    mom1 = {n: given["m_" + n] for n in WEIGHTS}
    mom2 = {n: given["v_" + n] for n in WEIGHTS}
    shard_shapes = {n: params[n].shape[1:] for n, _ in BIG}
    gsegs = [(members, tile) + _flat_layout(shard_shapes, members, tile) for members, tile in GATHER_SEGMENTS]
    rsegs = [(members, tile) + _flat_layout(shard_shapes, members, tile) for members, tile in SCATTER_SEGMENTS]

    def members_of(seg, l):
        names = [n for sl, n in seg[0] if sl == l]
        return [n for n in names if n != "mla_w_ukv"] + (["wk", "wv"] if "mla_w_ukv" in names else [])

    cx, cy, cc = _place()
    core = jnp.reshape(cc, (1,)).astype(jnp.int32)
    chip = jnp.reshape(2 * cx + cy, (1,)).astype(jnp.int32)

    gathered = []
    for s, (_, _, table, total) in enumerate(gsegs):
        w_flat = _pack_flat(params, table, total).astype(BF16)
        gathered.append(_all_gather(w_flat, f"gather_s{s}", seq_id=SEQ_IDS["gather", s] if s else None,
                                    after=gathered[0] if s else None))

    def get_layer(l):
        def build(name):
            for seg, g in zip(gsegs, gathered):
                if name in members_of(seg, l):
                    return _layer_mats(_gathered_mats(g, seg[2], l))
            raise KeyError(name)
        return _LayerWeights(build)

    smalls = [{n: params[n][l] for n in SMALL} for l in range(DEPTH)]

    pending, big_out, layer_grads = {}, [None] * len(rsegs), {}

    def exchange_cores(s):
        _, _, table, total = rsegs[s]
        mine = {}
        for l in range(DEPTH):
            names = members_of(rsegs[s], l)
            if names:
                natural = _unlayer_grads({n: layer_grads[l][n] for n in names})
                mine.update({(l, n): g for n, g in natural.items()})
        g_flat = _pack_grads(mine, table, total)
        g4 = g_flat.reshape(N_DEV // 2, 2, total, LANES)
        pending[s] = (g4, _exchange_cores(g4, f"scatter_cores_s{s}", seq_id=SEQ_IDS["cores", s] if s else None))

    def exchange_chips(s, after):
        g4, from_core = pending.pop(s)
        s1 = _chip_sum(g4, from_core, core, after, f"sum_cores_s{s}", rsegs[s][1])
        _Order.tokens = (s1,)
        pending[s] = (s1, _exchange_chips(s1, f"scatter_chips_s{s}", seq_id=SEQ_IDS["chips", s] if s else None))

    def finish(s):
        s1, from_chips = pending.pop(s)
        big_out[s] = _grad_sum(s1, from_chips, chip, f"grad_sum_s{s}", rsegs[s][1])

    plan = {("ffn1", 1): [("cores", 4)], ("ffn1_gu", 1): [("chips", 4)],
            ("mixer", 0): [("wait", 4), ("cores", 3)], ("mixer_mid", 0): [("chips", 3)],
            ("ffn1", 0): [("wait", 3), ("cores", 2)], ("ffn1_gu", 0): [("cores", 1), ("chips", 2)],
            ("ffn1_dn", 0): [("chips", 1)], ("done", 0): [("cores", 0)]}

    def at(point, l, token, grads):
        if grads is not None:
            layer_grads[l] = grads
        for what, s in plan.get((point, l), ()):
            if what == "cores":
                exchange_cores(s)
            elif what == "chips":
                exchange_chips(s, token)
            else:
                _Order.tokens += (pending[s][1],)

    loss, grad_x, _, small_grads = _local_step(x, positions, loss_target, get_layer, smalls, at)
    exchange_chips(0, grad_x)

    g_small = _pack_small({n: [small_grads[l][n] for l in range(DEPTH)] for n in SMALL}, loss)
    parts = _all_gather(g_small, "gather_small", vmem=True).reshape(N_DEV * SMALL_ROWS, LANES)
    small_out = _adam_small(parts, _pack_small(params), _pack_small(mom1), _pack_small(mom2), "adam_small")
    shapes = {n: params[n].shape for n in SMALL}
    outs = [_unpack_small(small, shapes) for small in small_out]
    loss = small_out[0].reshape(-1)[-1]

    pieces = {}
    for s in reversed(range(len(rsegs))):
        finish(s)
        pieces.update(_unpack_flat(big_out[s], rsegs[s][2]))
    for n, tr in BIG:
        view = (lambda a: jnp.swapaxes(a, 1, 2)) if tr else (lambda a: a)
        g = jnp.stack([pieces[l, n] for l in range(DEPTH)])
        updated = _adam_big(view(params[n]), g, view(mom1[n]), view(mom2[n]), f"adam_{n}")
        for tree, leaf in zip(outs, (g,) + tuple(updated)):
            tree[n] = view(leaf)
    return (loss, grad_x, *[o[n] for o in outs for n in WEIGHTS])
```

```python
import math

import jax
import jax.numpy as jnp
from jax import lax
from jax.experimental import pallas as pl
from jax.experimental.pallas import tpu as pltpu
from jax.experimental.pallas import tpu_sc as plsc

F32 = jnp.float32
BF16 = jnp.bfloat16

N_DEV = 8
DEPTH = 2
D_MODEL = 1024
D_FF = 2816
HEAD_DIM_A = 64
N_HEADS_A = 8
N_KV_HEADS_A = 2
GROUP_A = N_HEADS_A // N_KV_HEADS_A
BLOCK = 128
N_HEADS_B = 8
Q_LORA = 256
KV_LORA = 128
NOPE = 64
ROPE = 32
QK_B = NOPE + ROPE
V_B = 64
HEAD_PAD = 128
WIDTH_A = N_HEADS_A * HEAD_DIM_A
WIDTH_B = N_HEADS_B * V_B
WIDTH_BP = N_HEADS_B * HEAD_PAD
KV_A = N_KV_HEADS_A * HEAD_DIM_A
IN_WIDTH = WIDTH_A + 2 * KV_A + Q_LORA + KV_LORA + ROPE + 2 * D_MODEL
ROPE_BASE = 10000.0
EPS = 1e-6
NEG = -1e30
LOG2E = 1.4426950408889634

P_GA, P_GB, P_QA, P_CQ, P_KA, P_VA, P_CKV, P_KR = 0, 1024, 2048, 2560, 2816, 2944, 3072, 3200
PROJ_W = 3328
SWA_HEAD_ORDER = (0, 4, 1, 5, 2, 6, 3, 7)
SWA_HEAD_INVERSE = tuple(SWA_HEAD_ORDER.index(h) for h in range(N_HEADS_A))

ADAM_LR, ADAM_B1, ADAM_B2, ADAM_EPS, ADAM_WD, ADAM_STEP = 0.001, 0.9, 0.999, 1e-08, 0.01, 10

VMEM_LIMIT = 56 * 1024 * 1024
LANES = 1024

BIG = (("ffn1_w_gate", True), ("ffn1_w_up", True), ("ffn1_w_down", False), ("w_in", True), ("mla_w_uq", True),
       ("mla_w_ukv", True), ("w_branch_a", True), ("w_branch_b", True), ("w_out", False),
       ("ffn2_w_gate", True), ("ffn2_w_up", True), ("ffn2_w_down", False))
SMALL = ("ffn1_norm", "mix_norm", "ffn2_norm", "swa_q_norm", "swa_k_norm", "swa_sinks", "mla_q_lora_norm",
         "mla_kv_lora_norm", "mla_q_norm", "mla_k_norm")
WEIGHTS = ("ffn1_norm", "ffn1_w_gate", "ffn1_w_up", "ffn1_w_down", "mix_norm", "w_in", "swa_q_norm", "swa_k_norm",
           "swa_sinks", "mla_q_lora_norm", "mla_w_uq", "mla_kv_lora_norm", "mla_w_ukv", "mla_q_norm", "mla_k_norm",
           "w_branch_a", "w_branch_b", "w_out", "ffn2_norm", "ffn2_w_gate", "ffn2_w_up", "ffn2_w_down")
SMALL_ROWS = 8
FFN1 = ("ffn1_w_gate", "ffn1_w_up", "ffn1_w_down")
FFN2 = ("ffn2_w_gate", "ffn2_w_up", "ffn2_w_down")
MIXER = tuple(n for n, _ in BIG if n not in FFN1 + FFN2)
GATHER_SEGMENTS = (
    (tuple((0, n) for n in FFN1[:2]), 352),
    (((0, FFN1[2]),), 352),
    (tuple((0, n) for n in MIXER), 240),
    (tuple((0, n) for n in FFN2) + tuple((1, n) for n in FFN1), 352),
    (tuple((1, n) for n in MIXER + FFN2), 480),
)
SCATTER_SEGMENTS = (
    (((0, FFN1[2]),), 352),
    (tuple((0, n) for n in FFN1[:2]), 352),
    (tuple((0, n) for n in MIXER), 240),
    (tuple((0, n) for n in FFN2) + tuple((1, n) for n in FFN1), 352),
    (tuple((1, n) for n in MIXER + FFN2), 480),
)
LINK_COST_SCALE = 64
SEQ_IDS = {(kind, s): 1 + 4 * k + s - 1 for k, kind in enumerate(("gather", "cores", "chips")) for s in (1, 2, 3, 4)}


def _cp(*sem):
    return pltpu.CompilerParams(dimension_semantics=sem, vmem_limit_bytes=VMEM_LIMIT)


def _tile(n, prefs):
    for t in prefs:
        if n % t == 0:
            return t
    return n


def _dot(a, b, dims):
    return lax.dot_general(a, b, (dims, ((), ())), preferred_element_type=F32)


_NT = ((1,), (1,))
_NN = ((1,), (0,))
_TN = ((0,), (0,))


_ANY = pl.BlockSpec(memory_space=pl.ANY)


class _Order:
    tokens = ()


def _tc_call(body, *, in_specs, **kw):
    def run(*args):
        tokens, n = _Order.tokens, len(args)
        if not tokens:
            out = pl.pallas_call(body, in_specs=in_specs, **kw)(*args)
        else:
            def chained(*refs):
                return body(*refs[:n], *refs[n + len(tokens):])
            out = pl.pallas_call(chained, in_specs=list(in_specs) + [_ANY] * len(tokens), **kw)(*args, *tokens)
        _Order.tokens = (jax.tree.leaves(out)[0],)
        return out
    return run


def _sigmoid(x):
    return 0.5 * jnp.tanh(0.5 * x) + 0.5


def _chunks(n, width):
    return [(c, min(width, n - c)) for c in range(0, n, width)]


def _mm(pairs, mode, out_dtype, name, residual=None, alpha=1.0):
    for a, b in pairs:
        assert a.dtype == BF16 and b.dtype == BF16, (name, a.dtype, b.dtype)
    if mode == "tn":
        (a, b), = pairs
        return _mm_tokens(a, b, out_dtype, name, alpha)
    t = pairs[0][0].shape[0]
    n = pairs[0][1].shape[0] if mode == "nt" else pairs[0][1].shape[1]
    tm = _tile(t, (512, 256, 128))
    dims = _NT if mode == "nt" else _NN
    in_specs, args = [], []
    for a, w in pairs:
        in_specs.append(pl.BlockSpec((tm, a.shape[1]), lambda i: (i, 0)))
        in_specs.append(pl.BlockSpec(w.shape, lambda i: (0, 0)))
        args += [a, w]
    if residual is not None:
        in_specs.append(pl.BlockSpec((tm, n), lambda i: (i, 0)))
        args.append(residual)
    n_pairs = len(pairs)

    def body(*refs):
        o_ref = refs[-1]
        for c0, cw in _chunks(n, 512):
            acc = None
            for p in range(n_pairs):
                w_ref = refs[2 * p + 1]
                w = w_ref[c0:c0 + cw, :] if mode == "nt" else w_ref[:, c0:c0 + cw]
                d = _dot(refs[2 * p][...], w, dims)
                acc = d if acc is None else acc + d
            if alpha != 1.0:
                acc = acc * alpha
            if residual is not None:
                acc = refs[2 * n_pairs][:, c0:c0 + cw] + acc
            o_ref[:, c0:c0 + cw] = acc.astype(out_dtype)

    return _tc_call(
        body, name=name, grid=(t // tm,), in_specs=in_specs, out_specs=pl.BlockSpec((tm, n), lambda i: (i, 0)),
        out_shape=jax.ShapeDtypeStruct((t, n), out_dtype), compiler_params=_cp("parallel"))(*args)


def _mm_tokens(a, b, out_dtype, name, alpha):
    t, m = a.shape
    n = b.shape[1]
    tk = _tile(t, (512, 256, 128))

    def body(a_ref, b_ref, o_ref, acc_ref):
        k = pl.program_id(0)

        @pl.when(k == 0)
        def _():
            acc_ref[...] = jnp.zeros_like(acc_ref)

        for c0, cw in _chunks(m, 512):
            acc_ref[c0:c0 + cw, :] += _dot(a_ref[:, c0:c0 + cw], b_ref[...], _TN)

        @pl.when(k == pl.num_programs(0) - 1)
        def _():
            o_ref[...] = (acc_ref[...] * alpha).astype(out_dtype)

    return _tc_call(
        body, name=name, grid=(t // tk,),
        in_specs=[pl.BlockSpec((tk, m), lambda k: (k, 0)), pl.BlockSpec((tk, n), lambda k: (k, 0))],
        out_specs=pl.BlockSpec((m, n), lambda k: (0, 0)), out_shape=jax.ShapeDtypeStruct((m, n), out_dtype),
        scratch_shapes=[pltpu.VMEM((m, n), F32)], compiler_params=_cp("arbitrary"))(a, b)


def _rms_mm(x, gain, w, name):
    t, d = x.shape
    n = w.shape[0]
    tm = _tile(t, (512, 256, 128))

    def body(x_ref, g_ref, w_ref, h_ref, o_ref):
        xv = x_ref[...]
        r = lax.rsqrt(jnp.mean(xv * xv, axis=1, keepdims=True) + EPS)
        hv = (xv * r * g_ref[...]).astype(BF16)
        h_ref[...] = hv
        for c0, cw in _chunks(n, 512):
            o_ref[:, c0:c0 + cw] = _dot(hv, w_ref[c0:c0 + cw, :], _NT)

    row = pl.BlockSpec((tm, d), lambda i: (i, 0))
    return _tc_call(
        body, name=name, grid=(t // tm,),
        in_specs=[row, pl.BlockSpec((1, d), lambda i: (0, 0)), pl.BlockSpec(w.shape, lambda i: (0, 0))],
        out_specs=[row, pl.BlockSpec((tm, n), lambda i: (i, 0))],
        out_shape=[jax.ShapeDtypeStruct((t, d), BF16), jax.ShapeDtypeStruct((t, n), F32)],
        compiler_params=_cp("parallel"))(x, gain, w)


def _mm_rms_bwd(pairs, x, gain, res, name):
    t, d = x.shape
    tm = _tile(t, (512, 256, 128))
    n_pairs = len(pairs)

    def body(*refs):
        x_ref, g_ref, res_ref, dx_ref, dxb_ref, dg_ref, dn_ref = refs[2 * n_pairs:]
        for c0, cw in _chunks(d, 512):
            acc = None
            for p in range(n_pairs):
                part = _dot(refs[2 * p][...], refs[2 * p + 1][:, c0:c0 + cw], _NN)
                acc = part if acc is None else acc + part
            dn_ref[:, c0:c0 + cw] = acc
        xv = x_ref[...]
        r = lax.rsqrt(jnp.mean(xv * xv, axis=1, keepdims=True) + EPS)
        xh = xv * r
        dnv = dn_ref[...]
        dxh = dnv * g_ref[...]
        dx = res_ref[...] + r * (dxh - xh * jnp.mean(dxh * xh, axis=1, keepdims=True))
        dx_ref[...] = dx
        dxb_ref[...] = dx.astype(BF16)

        @pl.when(pl.program_id(0) == 0)
        def _():
            dg_ref[...] = jnp.zeros_like(dg_ref)

        dg_ref[...] += jnp.sum(dnv * xh, axis=0, keepdims=True)

    in_specs, args = [], []
    for a, w in pairs:
        assert a.dtype == BF16 and w.dtype == BF16, (name, a.dtype, w.dtype)
        in_specs += [pl.BlockSpec((tm, a.shape[1]), lambda i: (i, 0)), pl.BlockSpec(w.shape, lambda i: (0, 0))]
        args += [a, w]
    row = pl.BlockSpec((tm, d), lambda i: (i, 0))
    one = pl.BlockSpec((1, d), lambda i: (0, 0))
    return _tc_call(
        body, name=name, grid=(t // tm,), in_specs=in_specs + [row, one, row], out_specs=[row, row, one],
        out_shape=[jax.ShapeDtypeStruct((t, d), F32), jax.ShapeDtypeStruct((t, d), BF16),
                   jax.ShapeDtypeStruct((1, d), F32)],
        scratch_shapes=[pltpu.VMEM((tm, d), F32)], compiler_params=_cp("arbitrary"))(*args, x, gain, res)


def _ffn_up(x, gain, wg_t, wu_t, name):
    t, d = x.shape
    f = wg_t.shape[0]
    tm = _tile(t, (512, 256, 128))

    def body(x_ref, g_ref, wg_ref, wu_ref, n_ref, a_ref, b_ref, h_ref):
        xv = x_ref[...]
        r = lax.rsqrt(jnp.mean(xv * xv, axis=1, keepdims=True) + EPS)
        nv = (xv * r * g_ref[...]).astype(BF16)
        n_ref[...] = nv
        for c0, cw in _chunks(f, 256):
            a = _dot(nv, wg_ref[c0:c0 + cw, :], _NT)
            b = _dot(nv, wu_ref[c0:c0 + cw, :], _NT)
            a_ref[:, c0:c0 + cw] = a.astype(BF16)
            b_ref[:, c0:c0 + cw] = b.astype(BF16)
            h_ref[:, c0:c0 + cw] = (a * _sigmoid(a) * b).astype(BF16)

    w_spec = pl.BlockSpec((f, d), lambda i: (0, 0))
    x_spec = pl.BlockSpec((tm, d), lambda i: (i, 0))
    o_spec = pl.BlockSpec((tm, f), lambda i: (i, 0))
    o_shape = jax.ShapeDtypeStruct((t, f), BF16)
    return _tc_call(
        body, name=name, grid=(t // tm,), in_specs=[x_spec, pl.BlockSpec((1, d), lambda i: (0, 0)), w_spec, w_spec],
        out_specs=[x_spec] + [o_spec] * 3, out_shape=[jax.ShapeDtypeStruct((t, d), BF16)] + [o_shape] * 3,
        compiler_params=_cp("parallel"))(x, gain, wg_t, wu_t)


def _ffn_down_bwd(dxb, wd, a, b, name):
    t, d = dxb.shape
    f = wd.shape[0]
    tm = _tile(t, (512, 256, 128))

    def body(dx_ref, wd_ref, a_ref, b_ref, da_ref, db_ref):
        dxv = dx_ref[...]
        for c0, cw in _chunks(f, 256):
            dh = 0.5 * _dot(dxv, wd_ref[c0:c0 + cw, :], _NT)
            av = a_ref[:, c0:c0 + cw].astype(F32)
            bv = b_ref[:, c0:c0 + cw].astype(F32)
            sg = _sigmoid(av)
            da_ref[:, c0:c0 + cw] = (dh * bv * (sg * (1.0 + av * (1.0 - sg)))).astype(BF16)
            db_ref[:, c0:c0 + cw] = (dh * (av * sg)).astype(BF16)

    o_spec = pl.BlockSpec((tm, f), lambda i: (i, 0))
    o_shape = jax.ShapeDtypeStruct((t, f), BF16)
    return _tc_call(
        body, name=name, grid=(t // tm,),
        in_specs=[pl.BlockSpec((tm, d), lambda i: (i, 0)), pl.BlockSpec((f, d), lambda i: (0, 0)), o_spec, o_spec],
        out_specs=[o_spec] * 2, out_shape=[o_shape] * 2, compiler_params=_cp("parallel"))(dxb, wd, a, b)


def _loss_head(y, target):
    t, d = y.shape
    tm = _tile(t, (512, 256, 128))

    def body(y_ref, t_ref, dy_ref, dyb_ref, loss_ref, acc_ref):
        i = pl.program_id(0)
        e = y_ref[...] - t_ref[...]
        dy = e * (1.0 / d)
        dy_ref[...] = dy
        dyb_ref[...] = dy.astype(BF16)

        @pl.when(i == 0)
        def _():
            acc_ref[...] = jnp.zeros_like(acc_ref)

        acc_ref[...] += jnp.sum(e * e, axis=0, keepdims=True)

        @pl.when(i == pl.num_programs(0) - 1)
        def _():
            loss_ref[...] = jnp.sum(acc_ref[...], axis=1, keepdims=True) * (0.5 / d)

    row = pl.BlockSpec((tm, d), lambda i: (i, 0))
    return _tc_call(
        body, name="loss_head", grid=(t // tm,), in_specs=[row, row],
        out_specs=[row, row, pl.BlockSpec((1, 1), lambda i: (0, 0))],
        out_shape=[jax.ShapeDtypeStruct((t, d), F32), jax.ShapeDtypeStruct((t, d), BF16),
                   jax.ShapeDtypeStruct((1, 1), F32)],
        scratch_shapes=[pltpu.VMEM((1, d), F32)], compiler_params=_cp("arbitrary"))(y, target)


def _merge_fwd(oa, ob, proj, wa_t, wb_t, name):
    t = oa.shape[0]
    d = wa_t.shape[0]
    tm = _tile(t, (512, 256, 128))

    def body(oa_ref, ob_ref, ga_ref, gb_ref, wa_ref, wb_ref, mg_ref, ya_ref, yb_ref):
        oav, obv = oa_ref[...], ob_ref[...]
        for c0, cw in _chunks(d, 512):
            cs = slice(c0, c0 + cw)
            ya = _dot(oav, wa_ref[cs, :], _NT)
            yb = _dot(obv, wb_ref[cs, :], _NT)
            mg_ref[:, cs] = (_sigmoid(ga_ref[:, cs]) * ya + _sigmoid(gb_ref[:, cs]) * yb).astype(BF16)
            ya_ref[:, cs] = ya.astype(BF16)
            yb_ref[:, cs] = yb.astype(BF16)

    o_spec = pl.BlockSpec((tm, d), lambda i: (i, 0))
    o_shape = jax.ShapeDtypeStruct((t, d), BF16)
    return _tc_call(
        body, name=name, grid=(t // tm,),
        in_specs=[pl.BlockSpec((tm, oa.shape[1]), lambda i: (i, 0)), pl.BlockSpec((tm, ob.shape[1]), lambda i: (i, 0)),
                  pl.BlockSpec((tm, d), lambda i: (i, P_GA // d)), pl.BlockSpec((tm, d), lambda i: (i, P_GB // d)),
                  pl.BlockSpec(wa_t.shape, lambda i: (0, 0)), pl.BlockSpec(wb_t.shape, lambda i: (0, 0))],
        out_specs=[o_spec] * 3, out_shape=[o_shape] * 3,
        compiler_params=_cp("parallel"))(oa, ob, proj, proj, wa_t, wb_t)


def _merge_bwd(dxb, wo, proj, ya, yb, name):
    t, d = dxb.shape
    tm = _tile(t, (512, 256, 128))

    def body(dx_ref, wo_ref, ga_ref, gb_ref, ya_ref, yb_ref, dya_ref, dyb_ref, dga_ref, dgb_ref):
        dxv = dx_ref[...]
        for c0, cw in _chunks(d, 512):
            cs = slice(c0, c0 + cw)
            dm = _dot(dxv, wo_ref[cs, :], _NT)
            sa = _sigmoid(ga_ref[:, cs])
            sb = _sigmoid(gb_ref[:, cs])
            dya_ref[:, cs] = (dm * sa).astype(BF16)
            dyb_ref[:, cs] = (dm * sb).astype(BF16)
            dga_ref[:, cs] = (dm * ya_ref[:, cs].astype(F32) * (sa * (1.0 - sa))).astype(BF16)
            dgb_ref[:, cs] = (dm * yb_ref[:, cs].astype(F32) * (sb * (1.0 - sb))).astype(BF16)

    o_spec = pl.BlockSpec((tm, d), lambda i: (i, 0))
    o_shape = jax.ShapeDtypeStruct((t, d), BF16)
    return _tc_call(
        body, name=name, grid=(t // tm,),
        in_specs=[o_spec, pl.BlockSpec((d, d), lambda i: (0, 0)),
                  pl.BlockSpec((tm, d), lambda i: (i, P_GA // d)), pl.BlockSpec((tm, d), lambda i: (i, P_GB // d)),
                  o_spec, o_spec],
        out_specs=[o_spec] * 4, out_shape=[o_shape] * 4,
        compiler_params=_cp("parallel"))(dxb, wo, proj, proj, ya, yb)


def _swa_common(i, pq_ref, pkp_ref, pkc_ref):
    pk = jnp.concatenate([pkp_ref[0], pkc_ref[0]], axis=1)
    dist = (pq_ref[...] - pk).astype(F32)
    row = lax.broadcasted_iota(jnp.int32, (BLOCK, 2 * BLOCK), 0)
    col = lax.broadcasted_iota(jnp.int32, (BLOCK, 2 * BLOCK), 1)
    diff = row + BLOCK - col
    valid = (diff >= 0) & (diff < BLOCK) & ((i > 0) | (col >= BLOCK))
    return jnp.concatenate([dist] * N_HEADS_A, axis=0), jnp.concatenate([valid] * N_HEADS_A, axis=0)


def _half_sum(x, lo):
    s_lo = jnp.sum(jnp.where(lo, x, 0.0), axis=1, keepdims=True)
    s_hi = jnp.sum(jnp.where(lo, 0.0, x), axis=1, keepdims=True)
    return jnp.where(lo, s_lo, s_hi)


def _norm2(x, gain2, lo):
    r = lax.rsqrt(_half_sum(x * x, lo) * (1.0 / HEAD_DIM_A) + EPS)
    xh = x * r
    return xh * gain2, xh, r


def _norm2_bwd(d, xh, r, gain2, lo):
    dxh = d * gain2
    return r * (dxh - xh * (_half_sum(dxh * xh, lo) * (1.0 / HEAD_DIM_A)))


def _swa_stack(tiles, lo):
    zero = jnp.zeros_like(tiles[0])
    return jnp.concatenate([jnp.where(lo, t, zero) for t in tiles] + [jnp.where(lo, zero, t) for t in tiles], axis=0)


def _swa_unstack(x8, j, lo):
    return jnp.where(lo, x8[j * BLOCK:(j + 1) * BLOCK], x8[(GROUP_A + j) * BLOCK:(GROUP_A + j + 1) * BLOCK])


def _swa_head_columns(sk_ref):
    slope = jnp.concatenate([jnp.full((BLOCK, 1), 2.0 ** (-(h + 1)), F32) for h in range(N_HEADS_A)], axis=0)
    sink = jnp.concatenate([jnp.broadcast_to(sk_ref[:, h:h + 1], (BLOCK, 1)) for h in range(N_HEADS_A)], axis=0)
    return slope, sink


def _swa_probs(q4, kk, dist4, valid4, slope, sink):
    s = _dot(q4, kk, _NT) * (HEAD_DIM_A ** -0.5) - slope * dist4
    s = jnp.where(valid4, s, NEG)
    m = jnp.maximum(jnp.max(s, axis=1, keepdims=True), sink)
    e = jnp.exp(s - m)
    es = jnp.exp(sink - m)
    inv = 1.0 / (jnp.sum(e, axis=1, keepdims=True) + es)
    return e * inv, es * inv


def _swa_specs(s_len):
    nb = s_len // BLOCK

    def rowblk(b, i):
        return b * nb + i

    def prevblk(b, i):
        return b * nb + jnp.maximum(i - 1, 0)

    q_spec = pl.BlockSpec((BLOCK, WIDTH_A), lambda b, i: (rowblk(b, i), P_QA // WIDTH_A))
    kc_spec = pl.BlockSpec((BLOCK, KV_A), lambda b, i: (rowblk(b, i), P_KA // KV_A))
    kp_spec = pl.BlockSpec((BLOCK, KV_A), lambda b, i: (prevblk(b, i), P_KA // KV_A))
    vc_spec = pl.BlockSpec((BLOCK, KV_A), lambda b, i: (rowblk(b, i), P_VA // KV_A))
    vp_spec = pl.BlockSpec((BLOCK, KV_A), lambda b, i: (prevblk(b, i), P_VA // KV_A))
    pq_spec = pl.BlockSpec((BLOCK, 1), lambda b, i: (rowblk(b, i), 0))
    pkc_spec = pl.BlockSpec((1, 1, BLOCK), lambda b, i: (rowblk(b, i), 0, 0))
    pkp_spec = pl.BlockSpec((1, 1, BLOCK), lambda b, i: (prevblk(b, i), 0, 0))
    return nb, rowblk, [q_spec, kc_spec, kp_spec, vc_spec, vp_spec, pq_spec, pkc_spec, pkp_spec]


def _swa_fwd(proj, pos_col, pos_row, qg2, kg2, sinks, n_batch, s_len, name):
    t = proj.shape[0]
    nb, rowblk, specs = _swa_specs(s_len)
    small = pl.BlockSpec((1, HEAD_PAD), lambda b, i: (0, 0))

    def body(q_ref, kc_ref, kp_ref, vc_ref, vp_ref, pq_ref, pkc_ref, pkp_ref, qg_ref, kg_ref, sk_ref, o_ref):
        dist8, valid8 = _swa_common(pl.program_id(1), pq_ref, pkp_ref, pkc_ref)
        lo = lax.broadcasted_iota(jnp.int32, (1, HEAD_PAD), 1) < HEAD_DIM_A
        kk = _norm2(jnp.concatenate([kp_ref[...], kc_ref[...]], axis=0), kg_ref[...], lo)[0].astype(BF16)
        vv = jnp.concatenate([vp_ref[...], vc_ref[...]], axis=0).astype(BF16)
        qn = [_norm2(q_ref[:, j * HEAD_PAD:(j + 1) * HEAD_PAD], qg_ref[...], lo)[0] for j in range(GROUP_A)]
        slope, sink = _swa_head_columns(sk_ref)
        p, _ = _swa_probs(_swa_stack(qn, lo).astype(BF16), kk, dist8, valid8, slope, sink)
        o8 = _dot(p.astype(BF16), vv, _NN)
        for j in range(GROUP_A):
            o_ref[:, j * HEAD_PAD:(j + 1) * HEAD_PAD] = _swa_unstack(o8, j, lo).astype(BF16)

    return _tc_call(
        body, name=name, grid=(n_batch, nb), in_specs=specs + [small, small, small],
        out_specs=pl.BlockSpec((BLOCK, WIDTH_A), lambda b, i: (rowblk(b, i), 0)),
        out_shape=jax.ShapeDtypeStruct((t, WIDTH_A), BF16),
        compiler_params=_cp("parallel", "parallel"))(proj, proj, proj, proj, proj, pos_col, pos_row, pos_row,
                                                     qg2, kg2, sinks)


def _swa_bwd(proj, pos_col, pos_row, qg2, kg2, sinks, do, n_batch, s_len, name):
    t = proj.shape[0]
    nb, rowblk, specs = _swa_specs(s_len)
    small = pl.BlockSpec((1, HEAD_PAD), lambda b, i: (0, 0))
    scale = HEAD_DIM_A ** -0.5

    def body(q_ref, kc_ref, kp_ref, vc_ref, vp_ref, pq_ref, pkc_ref, pkp_ref, qg_ref, kg_ref, sk_ref, do_ref,
             dq_ref, dkc_ref, dkp_ref, dvc_ref, dvp_ref, dqg_ref, dsk_ref):
        b, i = pl.program_id(0), pl.program_id(1)

        @pl.when((b == 0) & (i == 0))
        def _():
            dqg_ref[...] = jnp.zeros_like(dqg_ref)
            dsk_ref[...] = jnp.zeros_like(dsk_ref)

        dist8, valid8 = _swa_common(i, pq_ref, pkp_ref, pkc_ref)
        lane = lax.broadcasted_iota(jnp.int32, (1, HEAD_PAD), 1)
        lo = lane < HEAD_DIM_A
        kk = _norm2(jnp.concatenate([kp_ref[...], kc_ref[...]], axis=0), kg_ref[...], lo)[0].astype(BF16)
        vv = jnp.concatenate([vp_ref[...], vc_ref[...]], axis=0).astype(BF16)
        qs = [_norm2(q_ref[:, j * HEAD_PAD:(j + 1) * HEAD_PAD], qg_ref[...], lo) for j in range(GROUP_A)]
        q8 = _swa_stack([q[0] for q in qs], lo).astype(BF16)
        do8 = _swa_stack([do_ref[:, j * HEAD_PAD:(j + 1) * HEAD_PAD] for j in range(GROUP_A)], lo)
        slope, sink = _swa_head_columns(sk_ref)
        p, ps = _swa_probs(q8, kk, dist8, valid8, slope, sink)
        dp = _dot(do8, vv, _NT)
        delta = jnp.sum(p * dp, axis=1, keepdims=True)
        ds = (p * (dp - delta) * scale).astype(BF16)
        dsink = ps * delta
        dsk = jnp.zeros((1, HEAD_PAD), F32)
        for h in range(N_HEADS_A):
            dsk = dsk + jnp.where(lane == h, -jnp.sum(dsink[h * BLOCK:(h + 1) * BLOCK]), 0.0)
        dvv = _dot(p.astype(BF16), do8, _TN)
        dkk = _dot(ds, q8, _TN)
        dq8 = _dot(ds, kk, _NN)
        dqg = jnp.zeros((1, HEAD_PAD), F32)
        for j in range(GROUP_A):
            _, xh, r = qs[j]
            dqn = _swa_unstack(dq8, j, lo)
            dqg = dqg + jnp.sum(dqn * xh, axis=0, keepdims=True)
            dq_ref[:, j * HEAD_PAD:(j + 1) * HEAD_PAD] = _norm2_bwd(dqn, xh, r, qg_ref[...], lo).astype(BF16)
        dkp_ref[...] = dkk[:BLOCK]
        dkc_ref[...] = dkk[BLOCK:]
        dvp_ref[...] = dvv[:BLOCK]
        dvc_ref[...] = dvv[BLOCK:]
        dqg_ref[...] += dqg
        dsk_ref[...] += dsk

    kv_out = pl.BlockSpec((BLOCK, KV_A), lambda b, i: (rowblk(b, i), 0))
    kv_shape = jax.ShapeDtypeStruct((t, KV_A), F32)
    wide = pl.BlockSpec((BLOCK, WIDTH_A), lambda b, i: (rowblk(b, i), 0))
    return _tc_call(
        body, name=name, grid=(n_batch, nb), in_specs=specs + [small, small, small, wide],
        out_specs=[wide, kv_out, kv_out, kv_out, kv_out, small, small],
        out_shape=[jax.ShapeDtypeStruct((t, WIDTH_A), BF16), kv_shape, kv_shape, kv_shape, kv_shape,
                   jax.ShapeDtypeStruct((1, HEAD_PAD), F32), jax.ShapeDtypeStruct((1, HEAD_PAD), F32)],
        compiler_params=_cp("arbitrary", "arbitrary"))(proj, proj, proj, proj, proj, pos_col, pos_row, pos_row,
                                                       qg2, kg2, sinks, do)


def _swa_kv_bwd(proj, kg2, dkc, dkp, dvc, dvp, n_batch, s_len, name):
    t = proj.shape[0]
    nb = s_len // BLOCK

    def rowblk(b, i):
        return b * nb + i

    def nextblk(b, i):
        return b * nb + jnp.minimum(i + 1, nb - 1)

    def body(k_ref, kg_ref, dkc_ref, dkp_ref, dvc_ref, dvp_ref, dk_ref, dv_ref, dkg_ref):
        b, i = pl.program_id(0), pl.program_id(1)

        @pl.when((b == 0) & (i == 0))
        def _():
            dkg_ref[...] = jnp.zeros_like(dkg_ref)

        lo = lax.broadcasted_iota(jnp.int32, (1, HEAD_PAD), 1) < HEAD_DIM_A
        has_next = (i < nb - 1).astype(F32)
        dkn = dkc_ref[...] + has_next * dkp_ref[...]
        dv_ref[...] = (dvc_ref[...] + has_next * dvp_ref[...]).astype(BF16)
        _, xh, r = _norm2(k_ref[...], kg_ref[...], lo)
        dkg_ref[...] += jnp.sum(dkn * xh, axis=0, keepdims=True)
        dk_ref[...] = _norm2_bwd(dkn, xh, r, kg_ref[...], lo).astype(BF16)

    cur = pl.BlockSpec((BLOCK, KV_A), lambda b, i: (rowblk(b, i), 0))
    nxt = pl.BlockSpec((BLOCK, KV_A), lambda b, i: (nextblk(b, i), 0))
    small = pl.BlockSpec((1, HEAD_PAD), lambda b, i: (0, 0))
    return _tc_call(
        body, name=name, grid=(n_batch, nb),
        in_specs=[pl.BlockSpec((BLOCK, KV_A), lambda b, i: (rowblk(b, i), P_KA // KV_A)), small, cur, nxt, cur, nxt],
        out_specs=[cur, cur, small],
        out_shape=[jax.ShapeDtypeStruct((t, KV_A), BF16), jax.ShapeDtypeStruct((t, KV_A), BF16),
                   jax.ShapeDtypeStruct((1, HEAD_PAD), F32)],
        compiler_params=_cp("arbitrary", "arbitrary"))(proj, kg2, dkc, dkp, dvc, dvp)


def _rope(u, c, sm, sp):
    return u * c + pltpu.roll(u, HEAD_PAD - ROPE // 2, 1) * sm + pltpu.roll(u, ROPE // 2, 1) * sp


def _rope_t(d, c, sm, sp):
    return d * c + pltpu.roll(d * sm, ROPE // 2, 1) + pltpu.roll(d * sp, HEAD_PAD - ROPE // 2, 1)


def _pad_norm(x, gain):
    r = lax.rsqrt(jnp.sum(x * x, axis=1, keepdims=True) * (1.0 / QK_B) + EPS)
    xh = x * r
    return xh * gain, xh, r


def _pad_norm_bwd(d, xh, r, gain):
    dxh = d * gain
    return r * (dxh - xh * (jnp.sum(dxh * xh, axis=1, keepdims=True) * (1.0 / QK_B)))


def _lora_norm(x, gain):
    r = lax.rsqrt(jnp.mean(x * x, axis=1, keepdims=True) + EPS)
    xh = x * r
    return xh * gain, xh, r


def _mla_in_specs(tm):
    row = lambda w, off: pl.BlockSpec((tm, w), lambda i: (i, off // w))
    one = lambda w: pl.BlockSpec((1, w), lambda i: (0, 0))
    full = lambda r, c: pl.BlockSpec((r, c), lambda i: (0, 0))
    tab = pl.BlockSpec((tm, HEAD_PAD), lambda i: (i, 0))
    return [row(Q_LORA, P_CQ), row(KV_LORA, P_CKV), row(HEAD_PAD, P_KR), tab, tab, tab,
            one(Q_LORA), one(KV_LORA), one(HEAD_PAD), one(HEAD_PAD),
            full(WIDTH_BP, Q_LORA), full(WIDTH_BP, KV_LORA), full(WIDTH_BP, KV_LORA)]


def _mla_pre(proj, tabs, gq, gkv, gqn, gkn, wuq, wk, wv, name):
    t = proj.shape[0]
    tm = _tile(t, (256, 128))

    def body(cq_ref, ckv_ref, kr_ref, c_ref, sm_ref, sp_ref, gq_ref, gkv_ref, gqn_ref, gkn_ref,
             wuq_ref, wk_ref, wv_ref, q_ref, k_ref, v_ref):
        cqn = _lora_norm(cq_ref[...], gq_ref[...])[0].astype(BF16)
        ckvn = _lora_norm(ckv_ref[...], gkv_ref[...])[0].astype(BF16)
        q_raw = _dot(cqn, wuq_ref[...], _NT)
        k_raw = _dot(ckvn, wk_ref[...], _NT)
        v_ref[...] = _dot(ckvn, wv_ref[...], _NT).astype(BF16)
        kr = pltpu.roll(kr_ref[...], NOPE, 1)
        c, sm, sp = c_ref[...], sm_ref[...], sp_ref[...]
        heads = [slice(h * HEAD_PAD, (h + 1) * HEAD_PAD) for h in range(N_HEADS_B)]
        raw = [q_raw[:, hs] for hs in heads] + [k_raw[:, hs] + kr for hs in heads]
        gains = [gqn_ref[...]] * N_HEADS_B + [gkn_ref[...]] * N_HEADS_B
        sq = [jnp.sum(x * x, axis=1, keepdims=True) for x in raw]
        normed = [x * lax.rsqrt(s * (1.0 / QK_B) + EPS) * g for x, s, g in zip(raw, sq, gains)]
        up = [pltpu.roll(x, HEAD_PAD - ROPE // 2, 1) for x in normed]
        down = [pltpu.roll(x, ROPE // 2, 1) for x in normed]
        roped = [(x * c + u * sm + d * sp).astype(BF16) for x, u, d in zip(normed, up, down)]
        for h, hs in enumerate(heads):
            q_ref[:, hs] = roped[h]
            k_ref[:, hs] = roped[N_HEADS_B + h]

    o_spec = pl.BlockSpec((tm, WIDTH_BP), lambda i: (i, 0))
    o_shape = jax.ShapeDtypeStruct((t, WIDTH_BP), BF16)
    return _tc_call(
        body, name=name, grid=(t // tm,), in_specs=_mla_in_specs(tm), out_specs=[o_spec] * 3,
        out_shape=[o_shape] * 3, compiler_params=_cp("parallel"))(
            proj, proj, proj, *tabs, gq, gkv, gqn, gkn, wuq, wk, wv)


def _mla_pre_bwd(proj, tabs, gq, gkv, gqn, gkn, wuq, wk, wv, dq, dk, dv, name):
    t = proj.shape[0]
    tm = _tile(t, (256, 128))

    def body(cq_ref, ckv_ref, kr_ref, c_ref, sm_ref, sp_ref, gq_ref, gkv_ref, gqn_ref, gkn_ref,
             wuq_ref, wk_ref, wv_ref, dq_ref, dk_ref, dv_ref,
             dcq_ref, dckv_ref, dkr_ref, dwuq_ref, dwk_ref, dwv_ref, dgq_ref, dgkv_ref, dgqn_ref, dgkn_ref,
             dqraw_ref, dkraw_ref):
        @pl.when(pl.program_id(0) == 0)
        def _():
            for r in (dwuq_ref, dwk_ref, dwv_ref, dgq_ref, dgkv_ref, dgqn_ref, dgkn_ref):
                r[...] = jnp.zeros_like(r)

        cqn_f, cq_xh, cq_r = _lora_norm(cq_ref[...], gq_ref[...])
        ckvn_f, ckv_xh, ckv_r = _lora_norm(ckv_ref[...], gkv_ref[...])
        cqn, ckvn = cqn_f.astype(BF16), ckvn_f.astype(BF16)
        q_raw = _dot(cqn, wuq_ref[...], _NT)
        k_raw = _dot(ckvn, wk_ref[...], _NT)
        kr = pltpu.roll(kr_ref[...], NOPE, 1)
        c, sm, sp = c_ref[...], sm_ref[...], sp_ref[...]
        heads = [slice(h * HEAD_PAD, (h + 1) * HEAD_PAD) for h in range(N_HEADS_B)]
        raw = [q_raw[:, hs] for hs in heads] + [k_raw[:, hs] + kr for hs in heads]
        d_out = [dq_ref[:, hs] for hs in heads] + [dk_ref[:, hs] for hs in heads]
        gains = [gqn_ref[...]] * N_HEADS_B + [gkn_ref[...]] * N_HEADS_B
        sq = [jnp.sum(x * x, axis=1, keepdims=True) for x in raw]
        rinv = [lax.rsqrt(s * (1.0 / QK_B) + EPS) for s in sq]
        xhat = [x * r for x, r in zip(raw, rinv)]
        down = [pltpu.roll(d * sm, ROPE // 2, 1) for d in d_out]
        up = [pltpu.roll(d * sp, HEAD_PAD - ROPE // 2, 1) for d in d_out]
        dn = [d * c + a + b for d, a, b in zip(d_out, down, up)]
        dgain = [jnp.sum(d * xh, axis=0, keepdims=True) for d, xh in zip(dn, xhat)]
        dxh = [d * g for d, g in zip(dn, gains)]
        inner = [jnp.sum(d * xh, axis=1, keepdims=True) * (1.0 / QK_B) for d, xh in zip(dxh, xhat)]
        d_raw = [r * (d - xh * s) for r, d, xh, s in zip(rinv, dxh, xhat, inner)]
        for h, hs in enumerate(heads):
            dqraw_ref[:, hs] = d_raw[h].astype(BF16)
            dkraw_ref[:, hs] = d_raw[N_HEADS_B + h].astype(BF16)
        dkr = sum(d_raw[N_HEADS_B + 1:], d_raw[N_HEADS_B])
        dgqn_ref[...] += sum(dgain[1:N_HEADS_B], dgain[0])
        dgkn_ref[...] += sum(dgain[N_HEADS_B + 1:], dgain[N_HEADS_B])
        lane = lax.broadcasted_iota(jnp.int32, (tm, HEAD_PAD), 1)
        dkr_ref[...] = jnp.where(lane < ROPE, pltpu.roll(dkr, HEAD_PAD - NOPE, 1), 0.0).astype(BF16)
        dqraw = dqraw_ref[...]
        dkraw = dkraw_ref[...]
        dvb = dv_ref[...].astype(BF16)
        dwuq_ref[...] += _dot(dqraw, cqn, _TN)
        dwk_ref[...] += _dot(dkraw, ckvn, _TN)
        dwv_ref[...] += _dot(dvb, ckvn, _TN)
        dcqn = _dot(dqraw, wuq_ref[...], _NN)
        dckvn = _dot(dkraw, wk_ref[...], _NN) + _dot(dvb, wv_ref[...], _NN)
        dgq_ref[...] += jnp.sum(dcqn * cq_xh, axis=0, keepdims=True)
        dgkv_ref[...] += jnp.sum(dckvn * ckv_xh, axis=0, keepdims=True)
        dxh = dcqn * gq_ref[...]
        dcq_ref[...] = (cq_r * (dxh - cq_xh * jnp.mean(dxh * cq_xh, axis=1, keepdims=True))).astype(BF16)
        dxh = dckvn * gkv_ref[...]
        dckv_ref[...] = (ckv_r * (dxh - ckv_xh * jnp.mean(dxh * ckv_xh, axis=1, keepdims=True))).astype(BF16)

    wide = pl.BlockSpec((tm, WIDTH_BP), lambda i: (i, 0))
    row = lambda w: pl.BlockSpec((tm, w), lambda i: (i, 0))
    full = lambda r, c: pl.BlockSpec((r, c), lambda i: (0, 0))
    return _tc_call(
        body, name=name, grid=(t // tm,), in_specs=_mla_in_specs(tm) + [wide, wide, wide],
        out_specs=[row(Q_LORA), row(KV_LORA), row(HEAD_PAD), full(WIDTH_BP, Q_LORA), full(WIDTH_BP, KV_LORA),
                   full(WIDTH_BP, KV_LORA), full(1, Q_LORA), full(1, KV_LORA), full(1, HEAD_PAD), full(1, HEAD_PAD)],
        out_shape=[jax.ShapeDtypeStruct((t, Q_LORA), BF16), jax.ShapeDtypeStruct((t, KV_LORA), BF16),
                   jax.ShapeDtypeStruct((t, HEAD_PAD), BF16), jax.ShapeDtypeStruct((WIDTH_BP, Q_LORA), F32),
                   jax.ShapeDtypeStruct((WIDTH_BP, KV_LORA), F32), jax.ShapeDtypeStruct((WIDTH_BP, KV_LORA), F32),
                   jax.ShapeDtypeStruct((1, Q_LORA), F32), jax.ShapeDtypeStruct((1, KV_LORA), F32),
                   jax.ShapeDtypeStruct((1, HEAD_PAD), F32), jax.ShapeDtypeStruct((1, HEAD_PAD), F32)],
        scratch_shapes=[pltpu.VMEM((tm, WIDTH_BP), BF16), pltpu.VMEM((tm, WIDTH_BP), BF16)],
        compiler_params=_cp("arbitrary"))(proj, proj, proj, *tabs, gq, gkv, gqn, gkn, wuq, wk, wv, dq, dk, dv)


def _mla_flash_specs(s_len):
    bh_spec = pl.BlockSpec((s_len, HEAD_PAD), lambda b, h: (b, h))
    lse_spec = pl.BlockSpec((1, s_len, 1), lambda b, h: (b * N_HEADS_B + h, 0, 0))
    return bh_spec, lse_spec


def _diag_mask(s):
    row = lax.broadcasted_iota(jnp.int32, s.shape, 0)
    col = lax.broadcasted_iota(jnp.int32, s.shape, 1)
    return jnp.where(row >= col, s, NEG)


def _mla_flash_fwd(q, k, v, n_batch, s_len, name):
    t = q.shape[0]
    tq = _tile(s_len, (256, 128))
    bh_spec, lse_spec = _mla_flash_specs(s_len)
    c = (QK_B ** -0.5) * LOG2E

    def body(q_ref, k_ref, v_ref, o_ref, lse_ref):
        nq = s_len // tq
        rows = [slice(i * tq, (i + 1) * tq) for i in range(nq)]
        below = [slice(0, i * tq) for i in range(nq)]
        qs = [q_ref[r, :] for r in rows]
        sd = [_diag_mask(_dot(qs[i], k_ref[rows[i], :], _NT)) for i in range(nq)]
        sb = [None] + [_dot(qs[i], k_ref[below[i], :], _NT) for i in range(1, nq)]
        m = [jnp.max(s, axis=1, keepdims=True) for s in sd]
        m = [m[0]] + [jnp.maximum(m[i], jnp.max(sb[i], axis=1, keepdims=True)) for i in range(1, nq)]
        pd = [jnp.exp2((sd[i] - m[i]) * c) for i in range(nq)]
        pb = [None] + [jnp.exp2((sb[i] - m[i]) * c) for i in range(1, nq)]
        l = [jnp.sum(p, axis=1, keepdims=True) for p in pd]
        l = [l[0]] + [l[i] + jnp.sum(pb[i], axis=1, keepdims=True) for i in range(1, nq)]
        acc = [_dot(pd[i].astype(BF16), v_ref[rows[i], :], _NN) for i in range(nq)]
        acc = [acc[0]] + [acc[i] + _dot(pb[i].astype(BF16), v_ref[below[i], :], _NN) for i in range(1, nq)]
        for i in range(nq):
            o_ref[rows[i], :] = (acc[i] * (1.0 / l[i])).astype(BF16)
            lse_ref[0, rows[i], :] = m[i] * c + jnp.log2(l[i])

    return _tc_call(
        body, name=name, grid=(n_batch, N_HEADS_B), in_specs=[bh_spec, bh_spec, bh_spec],
        out_specs=[bh_spec, lse_spec],
        out_shape=[jax.ShapeDtypeStruct((t, WIDTH_BP), BF16),
                   jax.ShapeDtypeStruct((n_batch * N_HEADS_B, s_len, 1), F32)],
        compiler_params=_cp("parallel", "parallel"))(q, k, v)


def _mla_flash_bwd(q, k, v, o, do, lse2, n_batch, s_len, name):
    t = q.shape[0]
    tq = _tile(s_len, (256, 128))
    bh_spec, lse_spec = _mla_flash_specs(s_len)
    scale = QK_B ** -0.5
    c = scale * LOG2E

    def body(q_ref, k_ref, v_ref, o_ref, do_ref, lse_ref, dq_ref, dk_ref, dv_ref):
        nq = s_len // tq
        rows = [slice(i * tq, (i + 1) * tq) for i in range(nq)]
        below = [slice(0, i * tq) for i in range(nq)]
        qs = [q_ref[r, :] for r in rows]
        dos = [do_ref[r, :] for r in rows]
        lse = [lse_ref[0, r, :] for r in rows]
        delta = [jnp.sum(dos[i].astype(F32) * o_ref[rows[i], :].astype(F32), axis=1, keepdims=True)
                 for i in range(nq)]

        def probs_and_ds(i, ks, diag):
            s = _dot(qs[i], k_ref[ks, :], _NT)
            if diag:
                s = _diag_mask(s)
            p = jnp.exp2(s * c - lse[i])
            dp = _dot(dos[i], v_ref[ks, :], _NT)
            return p.astype(BF16), (p * (dp - delta[i]) * scale).astype(BF16)

        diag = [probs_and_ds(i, rows[i], True) for i in range(nq)]
        rest = [None] + [probs_and_ds(i, below[i], False) for i in range(1, nq)]
        for i in range(nq):
            dq = _dot(diag[i][1], k_ref[rows[i], :], _NN)
            if i:
                dq = dq + _dot(rest[i][1], k_ref[below[i], :], _NN)
            dq_ref[rows[i], :] = dq
        for j in range(nq):
            later = slice(j * tq, s_len)
            p_j = jnp.concatenate([diag[j][0]] + [rest[i][0][:, rows[j]] for i in range(j + 1, nq)], axis=0)
            ds_j = jnp.concatenate([diag[j][1]] + [rest[i][1][:, rows[j]] for i in range(j + 1, nq)], axis=0)
            dk_ref[rows[j], :] = _dot(ds_j, q_ref[later, :], _TN)
            dv_ref[rows[j], :] = _dot(p_j, do_ref[later, :], _TN)

    f32_wide = jax.ShapeDtypeStruct((t, WIDTH_BP), F32)
    return _tc_call(
        body, name=name, grid=(n_batch, N_HEADS_B),
        in_specs=[bh_spec, bh_spec, bh_spec, bh_spec, bh_spec, lse_spec],
        out_specs=[bh_spec, bh_spec, bh_spec], out_shape=[f32_wide] * 3,
        compiler_params=_cp("parallel", "parallel"))(q, k, v, o, do, lse2)


def _swa_heads(w, axis, order):
    heads = [lax.slice_in_dim(w, h * HEAD_DIM_A, (h + 1) * HEAD_DIM_A, axis=axis) for h in order]
    return jnp.concatenate(heads, axis=axis)


class _LayerWeights:
    def __init__(self, build):
        self._build, self._mats = build, {}

    def __getitem__(self, name):
        if name not in self._mats:
            self._mats.update(self._build(name))
        return self._mats[name]


def _layer_mats(w):
    if "w_in" not in w:
        return dict(w)
    w_in = w["w_in"]
    o = [0]
    for n in (WIDTH_A, KV_A, KV_A, Q_LORA, KV_LORA, ROPE, D_MODEL, D_MODEL):
        o.append(o[-1] + n)
    qa, ka, va, cq, ckv, kr, ga, gb = (w_in[o[i]:o[i + 1]] for i in range(8))
    pad = jnp.zeros((PROJ_W - IN_WIDTH, w_in.shape[1]), w_in.dtype)
    w_in_p = jnp.concatenate([ga, gb, _swa_heads(qa, 0, SWA_HEAD_ORDER), cq, ka, va, ckv, kr, pad], axis=0)
    uq = w["mla_w_uq"].reshape(N_HEADS_B, QK_B, Q_LORA)
    uq = jnp.pad(uq, ((0, 0), (0, HEAD_PAD - QK_B), (0, 0))).reshape(WIDTH_BP, Q_LORA)
    ukv = w["mla_w_ukv"].reshape(N_HEADS_B, NOPE + V_B, KV_LORA)
    wk = jnp.pad(ukv[:, :NOPE], ((0, 0), (0, HEAD_PAD - NOPE), (0, 0))).reshape(WIDTH_BP, KV_LORA)
    wv = jnp.pad(ukv[:, NOPE:], ((0, 0), (0, HEAD_PAD - V_B), (0, 0))).reshape(WIDTH_BP, KV_LORA)
    wb = w["w_branch_b"].reshape(D_MODEL, N_HEADS_B, V_B)
    wb = jnp.pad(wb, ((0, 0), (0, 0), (0, HEAD_PAD - V_B))).reshape(D_MODEL, WIDTH_BP)
    out = dict(w)
    out.update(w_in=w_in_p, mla_w_uq=uq, wk=wk, wv=wv, w_branch_b=wb,
               w_branch_a=_swa_heads(w["w_branch_a"], 1, SWA_HEAD_ORDER))
    return out


def _unlayer_grads(g):
    if "w_in" not in g:
        return g
    d = g["w_in"]
    ga, gb, qa, cq, ka, va, ckv, kr = (d[a:b] for a, b in (
        (P_GA, P_GA + D_MODEL), (P_GB, P_GB + D_MODEL), (P_QA, P_QA + WIDTH_A), (P_CQ, P_CQ + Q_LORA),
        (P_KA, P_KA + KV_A), (P_VA, P_VA + KV_A), (P_CKV, P_CKV + KV_LORA), (P_KR, P_KR + ROPE)))
    out = {n: v for n, v in g.items() if n not in ("wk", "wv")}
    out["w_in"] = jnp.concatenate([_swa_heads(qa, 0, SWA_HEAD_INVERSE), ka, va, cq, ckv, kr, ga, gb], axis=0)
    out["mla_w_uq"] = g["mla_w_uq"].reshape(N_HEADS_B, HEAD_PAD, Q_LORA)[:, :QK_B].reshape(N_HEADS_B * QK_B, Q_LORA)
    dk = g["wk"].reshape(N_HEADS_B, HEAD_PAD, KV_LORA)[:, :NOPE]
    dv = g["wv"].reshape(N_HEADS_B, HEAD_PAD, KV_LORA)[:, :V_B]
    out["mla_w_ukv"] = jnp.concatenate([dk, dv], axis=1).reshape(N_HEADS_B * (NOPE + V_B), KV_LORA)
    out["w_branch_b"] = g["w_branch_b"].reshape(D_MODEL, N_HEADS_B, HEAD_PAD)[:, :, :V_B].reshape(D_MODEL, WIDTH_B)
    out["w_branch_a"] = _swa_heads(g["w_branch_a"], 1, SWA_HEAD_INVERSE)
    return out


def _pad_lanes(v, width):
    return jnp.pad(v.reshape(1, -1), ((0, 0), (0, width - v.shape[-1])))


def _rope_tables(positions):
    half = ROPE // 2
    inv_freq = ROPE_BASE ** (-jnp.arange(half, dtype=F32) / half)
    ang = positions.astype(F32).reshape(-1, 1) * inv_freq
    cos, sin = jnp.cos(ang), jnp.sin(ang)
    t = cos.shape[0]
    one, zero = jnp.ones((t, NOPE), F32), jnp.zeros((t, NOPE), F32)
    tail = jnp.zeros((t, HEAD_PAD - QK_B), F32)
    z16 = jnp.zeros((t, half), F32)
    c = jnp.concatenate([one, cos, cos, tail], axis=1)
    sm = jnp.concatenate([zero, -sin, z16, tail], axis=1)
    sp = jnp.concatenate([zero, z16, sin, tail], axis=1)
    return c, sm, sp


def _ffn_fwd(x, gain, wg_t, wu_t, wd, tag):
    n, a, b, hmid = _ffn_up(x, gain, wg_t, wu_t, f"{tag}_up")
    y = _mm([(hmid, wd)], "nn", F32, f"{tag}_down", residual=x, alpha=0.5)
    return y, (x, n, a, b, hmid)


def _ffn_bwd(dy, dyb, saved, gain, wg_t, wu_t, wd, tag, grads, names, hook):
    x, n, a, b, hmid = saved
    da, db = _ffn_down_bwd(dyb, wd, a, b, f"{tag}_down_bwd")
    grads[names[0]] = _mm([(da, n)], "tn", BF16, f"{tag}_dwg")
    grads[names[1]] = _mm([(db, n)], "tn", BF16, f"{tag}_dwu")
    hook("gu", grads[names[1]])
    dx, dxb, g_gain = _mm_rms_bwd([(da, wg_t), (db, wu_t)], x, gain, dy, f"{tag}_dn")
    hook("dn", dx)
    grads[names[2]] = _mm([(hmid, dyb)], "tn", BF16, f"{tag}_dwd", alpha=0.5)
    return dx, dxb, g_gain


def _fold_halves(d):
    return d[:, :HEAD_DIM_A] + d[:, HEAD_DIM_A:]


def _local_step(x, positions, target, layers, smalls, at=None):
    at = at or (lambda point, l, token, grads: None)
    _Order.tokens = ()
    n_batch, s_len, d = x.shape
    t = n_batch * s_len
    xt = x.reshape(t, d)
    tabs = _rope_tables(positions)
    pos_col = positions.reshape(t, 1)
    pos_row = positions.reshape(t // BLOCK, 1, BLOCK)
    saved = []
    get_layer = layers if callable(layers) else layers.__getitem__
    for l in range(len(smalls)):
        w, s = get_layer(l), smalls[l]
        g1, gm, g2 = (s[k].reshape(1, d) for k in ("ffn1_norm", "mix_norm", "ffn2_norm"))
        qg2, kg2 = (jnp.tile(s[k].reshape(1, -1), (1, 2)) for k in ("swa_q_norm", "swa_k_norm"))
        sinks = _pad_lanes(s["swa_sinks"], HEAD_PAD)
        gq, gkv = s["mla_q_lora_norm"].reshape(1, -1), s["mla_kv_lora_norm"].reshape(1, -1)
        gqn, gkn = _pad_lanes(s["mla_q_norm"], HEAD_PAD), _pad_lanes(s["mla_k_norm"], HEAD_PAD)
        x1, sv1 = _ffn_fwd(xt, g1, w["ffn1_w_gate"], w["ffn1_w_up"], w["ffn1_w_down"], f"l{l}_ffn1")
        h, proj = _rms_mm(x1, gm, w["w_in"], f"l{l}_proj")
        at("proj", l, proj, None)
        oa = _swa_fwd(proj, pos_col, pos_row, qg2, kg2, sinks, n_batch, s_len, f"l{l}_swa")
        q, k, v = _mla_pre(proj, tabs, gq, gkv, gqn, gkn, w["mla_w_uq"], w["wk"], w["wv"], f"l{l}_mla_pre")
        ob, lse = _mla_flash_fwd(q, k, v, n_batch, s_len, f"l{l}_mla")
        merged, ya, yb = _merge_fwd(oa, ob, proj, w["w_branch_a"], w["w_branch_b"], f"l{l}_merge")
        x2 = _mm([(merged, w["w_out"])], "nn", F32, f"l{l}_out", residual=x1)
        at("out", l, x2, None)
        x3, sv2 = _ffn_fwd(x2, g2, w["ffn2_w_gate"], w["ffn2_w_up"], w["ffn2_w_down"], f"l{l}_ffn2")
        saved.append((w, sv1, sv2, x1, h, proj, oa, q, k, v, ob, lse, merged, ya, yb,
                      (g1, gm, g2, qg2, kg2, sinks, gq, gkv, gqn, gkn)))
        xt = x3

    dy, dyb, loss = _loss_head(xt, target.reshape(t, d))

    big_grads, small_grads = [None] * len(smalls), [None] * len(smalls)
    for l in reversed(range(len(smalls))):
        w, sv1, sv2, x1, h, proj, oa, q, k, v, ob, lse, merged, ya, yb, gains = saved[l]
        g1, gm, g2, qg2, kg2, sinks, gq, gkv, gqn, gkn = gains
        bg, sg = {}, {}
        dy, dyb, sg["ffn2_norm"] = _ffn_bwd(
            dy, dyb, sv2, g2, w["ffn2_w_gate"], w["ffn2_w_up"], w["ffn2_w_down"], f"l{l}_ffn2", bg, FFN2,
            lambda point, token, l=l, bg=bg: at("ffn2_" + point, l, token, bg))
        dya, dyb_, dga, dgb = _merge_bwd(dyb, w["w_out"], proj, ya, yb, f"l{l}_merge_bwd")
        at("mixer", l, sg["ffn2_norm"], bg)
        bg["w_out"] = _mm([(merged, dyb)], "tn", BF16, f"l{l}_dwo")
        doa = _mm([(dya, w["w_branch_a"])], "nn", BF16, f"l{l}_doa")
        bg["w_branch_a"] = _mm([(dya, oa)], "tn", BF16, f"l{l}_dwa")
        dob = _mm([(dyb_, w["w_branch_b"])], "nn", BF16, f"l{l}_dob")
        bg["w_branch_b"] = _mm([(dyb_, ob)], "tn", BF16, f"l{l}_dwb")
        dqa, dkc, dkp, dvc, dvp, dqg, dsk = _swa_bwd(
            proj, pos_col, pos_row, qg2, kg2, sinks, doa, n_batch, s_len, f"l{l}_swa_bwd")
        sg["swa_q_norm"], sg["swa_sinks"] = _fold_halves(dqg), dsk[:, :N_HEADS_A]
        at("mixer_mid", l, dqa, bg)
        dka, dva, dkg = _swa_kv_bwd(proj, kg2, dkc, dkp, dvc, dvp, n_batch, s_len, f"l{l}_swa_kv_bwd")
        sg["swa_k_norm"] = _fold_halves(dkg)
        dq, dk, dv = _mla_flash_bwd(q, k, v, ob, dob, lse, n_batch, s_len, f"l{l}_mla_bwd")
        (dcq, dckv, dkr, g_uq, g_wk, g_wv, sg["mla_q_lora_norm"], sg["mla_kv_lora_norm"], dgqn, dgkn) = _mla_pre_bwd(
            proj, tabs, gq, gkv, gqn, gkn, w["mla_w_uq"], w["wk"], w["wv"], dq, dk, dv, f"l{l}_mla_pre_bwd")
        sg["mla_q_norm"], sg["mla_k_norm"] = dgqn[:, :QK_B], dgkn[:, :QK_B]
        bg["mla_w_uq"], bg["wk"], bg["wv"] = g_uq.astype(BF16), g_wk.astype(BF16), g_wv.astype(BF16)
        dproj = jnp.concatenate([dga, dgb, dqa, dcq, dka, dva, dckv, dkr], axis=1)
        bg["w_in"] = _mm([(dproj, h)], "tn", BF16, f"l{l}_dwin")
        dy, dyb, sg["mix_norm"] = _mm_rms_bwd([(dproj, w["w_in"])], x1, gm, dy, f"l{l}_dh")
        at("ffn1", l, sg["mix_norm"], bg)
        dy, dyb, sg["ffn1_norm"] = _ffn_bwd(
            dy, dyb, sv1, g1, w["ffn1_w_gate"], w["ffn1_w_up"], w["ffn1_w_down"], f"l{l}_ffn1", bg, FFN1,
            lambda point, token, l=l, bg=bg: at("ffn1_" + point, l, token, bg))
        big_grads[l], small_grads[l] = bg, sg
        at("done", l, dy, bg)
    return loss, dy.reshape(n_batch, s_len, d), big_grads, small_grads


def _round_up(n, m):
    return (n + m - 1) // m * m


def _flat_rows(shape, transposed):
    rows, k = (shape[1], shape[0]) if transposed else shape
    return _round_up(rows * k // LANES, 16), rows, k


def _flat_layout(shard_shapes, members, row_tile):
    table, off = [], 0
    for l, name in members:
        tr = dict(BIG)[name]
        pr, rows, k = _flat_rows(shard_shapes[name], tr)
        table.append(((l, name), tr, off, pr, rows, k))
        off += pr
    return table, _round_up(off, row_tile)


def _pack_flat(params, table, total):
    parts, off = [], 0
    for (l, name), tr, o, pr, rows, k in table:
        w = params[name][l]
        w = (w.T if tr else w).reshape(rows * k // LANES, LANES)
        parts.append(jnp.pad(w, ((0, pr - w.shape[0]), (0, 0))))
        off = o + pr
    if total > off:
        parts.append(jnp.zeros((total - off, LANES), parts[0].dtype))
    return jnp.concatenate(parts, axis=0)


def _unpack_flat(flat, table):
    return {key: flat[o:o + rows * k // LANES].reshape(rows, k) for key, tr, o, pr, rows, k in table}


def _gathered_mats(gathered, table, layer):
    return {name: gathered[:, o:o + rows * k // LANES].reshape(N_DEV * rows, k)
            for (l, name), tr, o, pr, rows, k in table if l == layer}


def _pack_grads(grads, table, total):
    parts, off = [], 0
    for key, tr, o, pr, rows, k in table:
        g = grads[key].reshape(N_DEV, rows * k // LANES, LANES)
        parts.append(jnp.pad(g, ((0, 0), (0, pr - g.shape[1]), (0, 0))))
        off = o + pr
    if total > off:
        parts.append(jnp.zeros((N_DEV, total - off, LANES), BF16))
    return jnp.concatenate(parts, axis=1)


def _pack_small(params, last=None):
    parts = [params[n][l].reshape(-1).astype(F32) for l in range(DEPTH) for n in SMALL]
    v = jnp.concatenate(parts)
    v = jnp.pad(v, (0, SMALL_ROWS * LANES - 1 - v.shape[0]))
    last = jnp.zeros((1,), F32) if last is None else last.reshape(1)
    return jnp.concatenate([v, last]).reshape(SMALL_ROWS, LANES)


def _unpack_small(flat, shapes):
    v, out, off = flat.reshape(-1), {}, 0
    for l in range(DEPTH):
        for n in SMALL:
            sz = math.prod(shapes[n][1:])
            out.setdefault(n, []).append(v[off:off + sz].reshape(shapes[n][1:]))
            off += sz
    return {n: jnp.stack(p) for n, p in out.items()}


_MESH = pl.DeviceIdType.MESH


def _place():
    return lax.axis_index("x"), lax.axis_index("y"), lax.axis_index("c")


def _handshake(peers):
    barrier = pltpu.get_barrier_semaphore()
    for peer in peers:
        pl.semaphore_signal(barrier, inc=1, device_id=peer, device_id_type=_MESH)
    pl.semaphore_wait(barrier, len(peers))


def _comm_call(body, out_shape, scratch, name, seq_id, spec=_ANY):
    if seq_id is None:
        return pl.pallas_call(body, name=name, out_shape=out_shape, in_specs=[spec, _ANY], out_specs=spec,
                              scratch_shapes=scratch)
    nbytes = LINK_COST_SCALE * math.prod(out_shape.shape) * out_shape.dtype.itemsize
    return pl.kernel(body, out_type=out_shape, mesh=plsc.ScalarSubcoreMesh(axis_name="sequencer", num_cores=1),
                     scratch_types=scratch, name=name, compiler_params=pltpu.CompilerParams(collective_id=seq_id),
                     cost_estimate=pl.CostEstimate(flops=0, transcendentals=0, bytes_accessed=nbytes))


def _all_gather(x_shard, name, vmem=False, seq_id=None, after=None):
    spec = pl.BlockSpec(memory_space=pltpu.VMEM) if vmem else _ANY

    def body(x_ref, after_ref, out_ref, send_sems, recv_sems, local_sem):
        x, y, c = _place()
        me, sibling = (x, y, c), (x, y, 1 - c)
        chips = [(1 - x, y), (x, 1 - y), (1 - x, 1 - y)]
        if seq_id is not None:
            _handshake([sibling] + [(*chip, c) for chip in chips])

        def rows(px, py, pc):
            return out_ref.at[4 * px + 2 * py + pc]

        def copy(k, block, to, src=None):
            return pltpu.make_async_remote_copy(
                src_ref=rows(*block) if src is None else src, dst_ref=rows(*block),
                send_sem=send_sems.at[k], recv_sem=recv_sems.at[k], device_id=to, device_id_type=_MESH)

        mine = pltpu.make_async_copy(x_ref, rows(*me), local_sem)
        mine.start()
        first = [copy(0, me, sibling, src=x_ref)]
        first += [copy(1 + j, me, (*chip, c), src=x_ref) for j, chip in enumerate(chips)]
        for cp in first:
            cp.start()
        passed = [copy(4 + j, (*chip, c), sibling) for j, chip in enumerate(chips)]
        for j, chip in enumerate(chips):
            copy(1 + j, (*chip, c), me).wait_recv()
            passed[j].start()
        copy(0, sibling, me).wait_recv()
        for j, chip in enumerate(chips):
            copy(4 + j, (*chip, 1 - c), me).wait_recv()
        for cp in first + passed:
            cp.wait_send()
        mine.wait()

    return _comm_call(
        body, jax.ShapeDtypeStruct((N_DEV,) + x_shard.shape, x_shard.dtype),
        [pltpu.SemaphoreType.DMA((7,)), pltpu.SemaphoreType.DMA((7,)), pltpu.SemaphoreType.DMA], name, seq_id,
        spec)(x_shard, x_shard if after is None else after)


def _exchange_cores(g4, name, seq_id=None, after=None):
    n_chip, _, r, w = g4.shape

    def body(g_ref, after_ref, out_ref, send_sems, recv_sems):
        x, y, c = _place()
        if seq_id is not None:
            _handshake([(x, y, 1 - c)])
        copies = [pltpu.make_async_remote_copy(
            src_ref=g_ref.at[q, 1 - c], dst_ref=out_ref.at[q], send_sem=send_sems.at[q], recv_sem=recv_sems.at[q],
            device_id=(x, y, 1 - c), device_id_type=_MESH) for q in range(n_chip)]
        for cp in copies:
            cp.start()
        for cp in copies:
            cp.wait()

    return _comm_call(
        body, jax.ShapeDtypeStruct((n_chip, r, w), g4.dtype),
        [pltpu.SemaphoreType.DMA((n_chip,)), pltpu.SemaphoreType.DMA((n_chip,))], name, seq_id)(g4, g4 if after is None else after)


def _exchange_chips(s1, name, seq_id=None):
    _, r, w = s1.shape

    def body(s_ref, after_ref, out_ref, send_sems, recv_sems):
        x, y, c = _place()
        chips = [(1 - x, y), (x, 1 - y), (1 - x, 1 - y)]
        if seq_id is not None:
            _handshake([(*chip, c) for chip in chips])
        copies = []
        for k, (tx, ty) in enumerate(chips):
            copies.append(pltpu.make_async_remote_copy(
                src_ref=s_ref.at[2 * tx + ty], dst_ref=out_ref.at[k], send_sem=send_sems.at[k],
                recv_sem=recv_sems.at[k], device_id=(tx, ty, c), device_id_type=_MESH))
        for cp in copies:
            cp.start()
        for cp in copies:
            cp.wait()

    return _comm_call(
        body, jax.ShapeDtypeStruct((3, r, w), s1.dtype),
        [pltpu.SemaphoreType.DMA((3,)), pltpu.SemaphoreType.DMA((3,))], name, seq_id)(s1, s1)


def _chip_sum(g4, recv, core, after, name, tr):
    n_chip, _, r, w = g4.shape

    def body(c_ref, a_ref, b_ref, after_ref, o_ref):
        o_ref[...] = (a_ref[...].astype(F32) + b_ref[...].astype(F32)).astype(o_ref.dtype)

    grid_spec = pltpu.PrefetchScalarGridSpec(
        num_scalar_prefetch=1, grid=(n_chip, r // tr),
        in_specs=[pl.BlockSpec((None, None, tr, w), lambda q, i, c: (q, c[0], i, 0)),
                  pl.BlockSpec((None, tr, w), lambda q, i, c: (q, i, 0)), _ANY],
        out_specs=pl.BlockSpec((None, tr, w), lambda q, i, c: (q, i, 0)))
    return pl.pallas_call(
        body, name=name, grid_spec=grid_spec, out_shape=jax.ShapeDtypeStruct((n_chip, r, w), g4.dtype),
        compiler_params=_cp("parallel", "parallel"))(core, g4, recv, after)


def _adam(w, g, m, v):
    m = ADAM_B1 * m + (1.0 - ADAM_B1) * g
    v = ADAM_B2 * v + (1.0 - ADAM_B2) * (g * g)
    m_hat = m / (1.0 - ADAM_B1 ** ADAM_STEP)
    v_hat = v / (1.0 - ADAM_B2 ** ADAM_STEP)
    delta = -ADAM_LR * (m_hat / (jnp.sqrt(v_hat) + ADAM_EPS) + ADAM_WD * w)
    return delta, m, v


def _grad_sum(s1, r2, chip, name, tr):
    _, r, lanes = s1.shape

    def body(c_ref, s_ref, r0_ref, r1_ref, r2_ref, g_out):
        g_out[...] = ((s_ref[...].astype(F32) + r0_ref[...].astype(F32)) + r1_ref[...].astype(F32)) + r2_ref[
            ...].astype(F32)

    row = pl.BlockSpec((tr, lanes), lambda i, c: (i, 0))
    rel = lambda k: pl.BlockSpec((None, tr, lanes), lambda i, c: (k, i, 0))
    grid_spec = pltpu.PrefetchScalarGridSpec(
        num_scalar_prefetch=1, grid=(r // tr,),
        in_specs=[pl.BlockSpec((None, tr, lanes), lambda i, c: (c[0], i, 0)), rel(0), rel(1), rel(2)], out_specs=row)
    return pl.pallas_call(
        body, name=name, grid_spec=grid_spec, out_shape=jax.ShapeDtypeStruct((r, lanes), F32),
        compiler_params=_cp("parallel"))(chip, s1, r2, r2, r2)


def _adam_big(w, g, m, v, name):
    depth, k, n = w.shape
    tk = k if k <= 512 else _tile(k, (256, 128))

    def body(w_ref, g_ref, m_ref, v_ref, d_out, m_out, v_out):
        d, mn, vn = _adam(w_ref[...], g_ref[...], m_ref[...], v_ref[...])
        d_out[...] = d
        m_out[...] = mn
        v_out[...] = vn

    blk = pl.BlockSpec((None, tk, n), lambda l, i: (l, i, 0))
    return pl.pallas_call(
        body, name=name, grid=(depth, k // tk), in_specs=[blk] * 4, out_specs=[blk] * 3,
        out_shape=[jax.ShapeDtypeStruct(w.shape, F32)] * 3, compiler_params=_cp("parallel", "parallel"))(w, g, m, v)


def _adam_small(parts, w, m, v, name):
    rows, lanes = w.shape

    def body(p_ref, w_ref, m_ref, v_ref, g_out, d_out, m_out, v_out):
        g = p_ref[0:rows, :]
        for dev in range(1, N_DEV):
            g = g + p_ref[dev * rows:(dev + 1) * rows, :]
        d, mn, vn = _adam(w_ref[...], g, m_ref[...], v_ref[...])
        g_out[...] = g
        d_out[...] = d
        m_out[...] = mn
        v_out[...] = vn

    return pl.pallas_call(
        body, name=name, out_shape=[jax.ShapeDtypeStruct((rows, lanes), F32)] * 4)(parts, w, m, v)


def kernel(x, positions, ffn1_norm, ffn1_w_gate, ffn1_w_up, ffn1_w_down, mix_norm, w_in, swa_q_norm, swa_k_norm, swa_sinks, mla_q_lora_norm, mla_w_uq, mla_kv_lora_norm, mla_w_ukv, mla_q_norm, mla_k_norm, w_branch_a, w_branch_b, w_out, ffn2_norm, ffn2_w_gate, ffn2_w_up, ffn2_w_down, loss_target, m_ffn1_norm, m_ffn1_w_gate, m_ffn1_w_up, m_ffn1_w_down, m_mix_norm, m_w_in, m_swa_q_norm, m_swa_k_norm, m_swa_sinks, m_mla_q_lora_norm, m_mla_w_uq, m_mla_kv_lora_norm, m_mla_w_ukv, m_mla_q_norm, m_mla_k_norm, m_w_branch_a, m_w_branch_b, m_w_out, m_ffn2_norm, m_ffn2_w_gate, m_ffn2_w_up, m_ffn2_w_down, v_ffn1_norm, v_ffn1_w_gate, v_ffn1_w_up, v_ffn1_w_down, v_mix_norm, v_w_in, v_swa_q_norm, v_swa_k_norm, v_swa_sinks, v_mla_q_lora_norm, v_mla_w_uq, v_mla_kv_lora_norm, v_mla_w_ukv, v_mla_q_norm, v_mla_k_norm, v_w_branch_a, v_w_branch_b, v_w_out, v_ffn2_norm, v_ffn2_w_gate, v_ffn2_w_up, v_ffn2_w_down):
    given = dict(locals())
    params = {n: given[n] for n in WEIGHTS}
    mom1 = {n: given["m_" + n] for n in WEIGHTS}
    mom2 = {n: given["v_" + n] for n in WEIGHTS}
    shard_shapes = {n: params[n].shape[1:] for n, _ in BIG}
    gsegs = [(members, tile) + _flat_layout(shard_shapes, members, tile) for members, tile in GATHER_SEGMENTS]
    rsegs = [(members, tile) + _flat_layout(shard_shapes, members, tile) for members, tile in SCATTER_SEGMENTS]

    def members_of(seg, l):
        names = [n for sl, n in seg[0] if sl == l]
        return [n for n in names if n != "mla_w_ukv"] + (["wk", "wv"] if "mla_w_ukv" in names else [])

    cx, cy, cc = _place()
    core = jnp.reshape(cc, (1,)).astype(jnp.int32)
    chip = jnp.reshape(2 * cx + cy, (1,)).astype(jnp.int32)

    gathered = []
    for s, (_, _, table, total) in enumerate(gsegs):
        w_flat = _pack_flat(params, table, total).astype(BF16)
        gathered.append(_all_gather(w_flat, f"gather_s{s}", seq_id=SEQ_IDS["gather", s] if s else None,
                                    after=gathered[0] if s else None))

    def get_layer(l):
        def build(name):
            for seg, g in zip(gsegs, gathered):
                if name in members_of(seg, l):
                    return _layer_mats(_gathered_mats(g, seg[2], l))
            raise KeyError(name)
        return _LayerWeights(build)

    smalls = [{n: params[n][l] for n in SMALL} for l in range(DEPTH)]

    pending, big_out, layer_grads = {}, [None] * len(rsegs), {}

    def exchange_cores(s):
        _, _, table, total = rsegs[s]
        mine = {}
        for l in range(DEPTH):
            names = members_of(rsegs[s], l)
            if names:
                natural = _unlayer_grads({n: layer_grads[l][n] for n in names})
                mine.update({(l, n): g for n, g in natural.items()})
        g_flat = _pack_grads(mine, table, total)
        g4 = g_flat.reshape(N_DEV // 2, 2, total, LANES)
        pending[s] = (g4, _exchange_cores(g4, f"scatter_cores_s{s}", seq_id=SEQ_IDS["cores", s] if s else None))

    def exchange_chips(s, after):
        g4, from_core = pending.pop(s)
        s1 = _chip_sum(g4, from_core, core, after, f"sum_cores_s{s}", rsegs[s][1])
        _Order.tokens = (s1,)
        pending[s] = (s1, _exchange_chips(s1, f"scatter_chips_s{s}", seq_id=SEQ_IDS["chips", s] if s else None))

    def finish(s):
        s1, from_chips = pending.pop(s)
        big_out[s] = _grad_sum(s1, from_chips, chip, f"grad_sum_s{s}", rsegs[s][1])

    plan = {("ffn1", 1): [("cores", 4)], ("ffn1_gu", 1): [("chips", 4)],
            ("mixer", 0): [("wait", 4), ("cores", 3)], ("mixer_mid", 0): [("chips", 3)],
            ("ffn1", 0): [("wait", 3), ("cores", 2)], ("ffn1_gu", 0): [("cores", 1), ("chips", 2)],
            ("ffn1_dn", 0): [("chips", 1)], ("done", 0): [("cores", 0)]}

    def at(point, l, token, grads):
        if grads is not None:
            layer_grads[l] = grads
        for what, s in plan.get((point, l), ()):
            if what == "cores":
                exchange_cores(s)
            elif what == "chips":
                exchange_chips(s, token)
            else:
                _Order.tokens += (pending[s][1],)

    loss, grad_x, _, small_grads = _local_step(x, positions, loss_target, get_layer, smalls, at)
    exchange_chips(0, grad_x)

    g_small = _pack_small({n: [small_grads[l][n] for l in range(DEPTH)] for n in SMALL}, loss)
    parts = _all_gather(g_small, "gather_small", vmem=True).reshape(N_DEV * SMALL_ROWS, LANES)
    small_out = _adam_small(parts, _pack_small(params), _pack_small(mom1), _pack_small(mom2), "adam_small")
    shapes = {n: params[n].shape for n in SMALL}
    outs = [_unpack_small(small, shapes) for small in small_out]
    loss = small_out[0].reshape(-1)[-1]

    pieces = {}
    for s in reversed(range(len(rsegs))):
        finish(s)
        pieces.update(_unpack_flat(big_out[s], rsegs[s][2]))
    for n, tr in BIG:
        view = (lambda a: jnp.swapaxes(a, 1, 2)) if tr else (lambda a: a)
        g = jnp.stack([pieces[l, n] for l in range(DEPTH)])
        updated = _adam_big(view(params[n]), g, view(mom1[n]), view(mom2[n]), f"adam_{n}")
        for tree, leaf in zip(outs, (g,) + tuple(updated)):
            tree[n] = view(leaf)
    return (loss, grad_x, *[o[n] for o in outs for n in WEIGHTS])
```

```python
import math

import jax
import jax.numpy as jnp
from jax import lax
from jax.experimental import pallas as pl
from jax.experimental.pallas import tpu as pltpu
from jax.experimental.pallas import tpu_sc as plsc

F32 = jnp.float32
BF16 = jnp.bfloat16

N_DEV = 8
DEPTH = 2
D_MODEL = 1024
D_FF = 2816
HEAD_DIM_A = 64
N_HEADS_A = 8
N_KV_HEADS_A = 2
GROUP_A = N_HEADS_A // N_KV_HEADS_A
BLOCK = 128
N_HEADS_B = 8
Q_LORA = 256
KV_LORA = 128
NOPE = 64
ROPE = 32
QK_B = NOPE + ROPE
V_B = 64
HEAD_PAD = 128
WIDTH_A = N_HEADS_A * HEAD_DIM_A
WIDTH_B = N_HEADS_B * V_B
WIDTH_BP = N_HEADS_B * HEAD_PAD
KV_A = N_KV_HEADS_A * HEAD_DIM_A
IN_WIDTH = WIDTH_A + 2 * KV_A + Q_LORA + KV_LORA + ROPE + 2 * D_MODEL
ROPE_BASE = 10000.0
EPS = 1e-6
NEG = -1e30
LOG2E = 1.4426950408889634

P_GA, P_GB, P_QA, P_CQ, P_KA, P_VA, P_CKV, P_KR = 0, 1024, 2048, 2560, 2816, 2944, 3072, 3200
PROJ_W = 3328
SWA_HEAD_ORDER = (0, 4, 1, 5, 2, 6, 3, 7)
SWA_HEAD_INVERSE = tuple(SWA_HEAD_ORDER.index(h) for h in range(N_HEADS_A))

ADAM_LR, ADAM_B1, ADAM_B2, ADAM_EPS, ADAM_WD, ADAM_STEP = 0.001, 0.9, 0.999, 1e-08, 0.01, 10

VMEM_LIMIT = 56 * 1024 * 1024
LANES = 1024

BIG = (("ffn1_w_gate", True), ("ffn1_w_up", True), ("ffn1_w_down", False), ("w_in", True), ("mla_w_uq", True),
       ("mla_w_ukv", True), ("w_branch_a", True), ("w_branch_b", True), ("w_out", False),
       ("ffn2_w_gate", True), ("ffn2_w_up", True), ("ffn2_w_down", False))
SMALL = ("ffn1_norm", "mix_norm", "ffn2_norm", "swa_q_norm", "swa_k_norm", "swa_sinks", "mla_q_lora_norm",
         "mla_kv_lora_norm", "mla_q_norm", "mla_k_norm")
WEIGHTS = ("ffn1_norm", "ffn1_w_gate", "ffn1_w_up", "ffn1_w_down", "mix_norm", "w_in", "swa_q_norm", "swa_k_norm",
           "swa_sinks", "mla_q_lora_norm", "mla_w_uq", "mla_kv_lora_norm", "mla_w_ukv", "mla_q_norm", "mla_k_norm",
           "w_branch_a", "w_branch_b", "w_out", "ffn2_norm", "ffn2_w_gate", "ffn2_w_up", "ffn2_w_down")
SMALL_ROWS = 8
FFN1 = ("ffn1_w_gate", "ffn1_w_up", "ffn1_w_down")
FFN2 = ("ffn2_w_gate", "ffn2_w_up", "ffn2_w_down")
MIXER = tuple(n for n, _ in BIG if n not in FFN1 + FFN2)
GATHER_SEGMENTS = (
    (tuple((0, n) for n in FFN1[:2]), 352),
    (((0, FFN1[2]),), 352),
    (tuple((0, n) for n in MIXER), 240),
    (tuple((0, n) for n in FFN2) + tuple((1, n) for n in FFN1), 352),
    (tuple((1, n) for n in MIXER + FFN2), 480),
)
SCATTER_SEGMENTS = (
    (((0, FFN1[2]),), 352),
    (tuple((0, n) for n in FFN1[:2]), 352),
    (tuple((0, n) for n in MIXER), 240),
    (tuple((0, n) for n in FFN2) + tuple((1, n) for n in FFN1), 352),
    (tuple((1, n) for n in MIXER + FFN2), 480),
)
LINK_COST_SCALE = 64
SEQ_IDS = {(kind, s): 1 + 4 * k + s - 1 for k, kind in enumerate(("gather", "cores", "chips")) for s in (1, 2, 3, 4)}


def _cp(*sem):
    return pltpu.CompilerParams(dimension_semantics=sem, vmem_limit_bytes=VMEM_LIMIT)


def _tile(n, prefs):
    for t in prefs:
        if n % t == 0:
            return t
    return n


def _dot(a, b, dims):
    return lax.dot_general(a, b, (dims, ((), ())), preferred_element_type=F32)


_NT = ((1,), (1,))
_NN = ((1,), (0,))
_TN = ((0,), (0,))


_ANY = pl.BlockSpec(memory_space=pl.ANY)


class _Order:
    tokens = ()


def _tc_call(body, *, in_specs, **kw):
    def run(*args):
        tokens, n = _Order.tokens, len(args)
        if not tokens:
            out = pl.pallas_call(body, in_specs=in_specs, **kw)(*args)
        else:
            def chained(*refs):
                return body(*refs[:n], *refs[n + len(tokens):])
            out = pl.pallas_call(chained, in_specs=list(in_specs) + [_ANY] * len(tokens), **kw)(*args, *tokens)
        _Order.tokens = (jax.tree.leaves(out)[0],)
        return out
    return run


def _sigmoid(x):
    return 0.5 * jnp.tanh(0.5 * x) + 0.5


def _chunks(n, width):
    return [(c, min(width, n - c)) for c in range(0, n, width)]


def _mm(pairs, mode, out_dtype, name, residual=None, alpha=1.0):
    for a, b in pairs:
        assert a.dtype == BF16 and b.dtype == BF16, (name, a.dtype, b.dtype)
    if mode == "tn":
        (a, b), = pairs
        return _mm_tokens(a, b, out_dtype, name, alpha)
    t = pairs[0][0].shape[0]
    n = pairs[0][1].shape[0] if mode == "nt" else pairs[0][1].shape[1]
    tm = _tile(t, (512, 256, 128))
    dims = _NT if mode == "nt" else _NN
    in_specs, args = [], []
    for a, w in pairs:
        in_specs.append(pl.BlockSpec((tm, a.shape[1]), lambda i: (i, 0)))
        in_specs.append(pl.BlockSpec(w.shape, lambda i: (0, 0)))
        args += [a, w]
    if residual is not None:
        in_specs.append(pl.BlockSpec((tm, n), lambda i: (i, 0)))
        args.append(residual)
    n_pairs = len(pairs)

    def body(*refs):
        o_ref = refs[-1]
        for c0, cw in _chunks(n, 512):
            acc = None
            for p in range(n_pairs):
                w_ref = refs[2 * p + 1]
                w = w_ref[c0:c0 + cw, :] if mode == "nt" else w_ref[:, c0:c0 + cw]
                d = _dot(refs[2 * p][...], w, dims)
                acc = d if acc is None else acc + d
            if alpha != 1.0:
                acc = acc * alpha
            if residual is not None:
                acc = refs[2 * n_pairs][:, c0:c0 + cw] + acc
            o_ref[:, c0:c0 + cw] = acc.astype(out_dtype)

    return _tc_call(
        body, name=name, grid=(t // tm,), in_specs=in_specs, out_specs=pl.BlockSpec((tm, n), lambda i: (i, 0)),
        out_shape=jax.ShapeDtypeStruct((t, n), out_dtype), compiler_params=_cp("parallel"))(*args)


def _mm_tokens(a, b, out_dtype, name, alpha):
    t, m = a.shape
    n = b.shape[1]
    tk = _tile(t, (512, 256, 128))

    def body(a_ref, b_ref, o_ref, acc_ref):
        k = pl.program_id(0)

        @pl.when(k == 0)
        def _():
            acc_ref[...] = jnp.zeros_like(acc_ref)

        for c0, cw in _chunks(m, 512):
            acc_ref[c0:c0 + cw, :] += _dot(a_ref[:, c0:c0 + cw], b_ref[...], _TN)

        @pl.when(k == pl.num_programs(0) - 1)
        def _():
            o_ref[...] = (acc_ref[...] * alpha).astype(out_dtype)

    return _tc_call(
        body, name=name, grid=(t // tk,),
        in_specs=[pl.BlockSpec((tk, m), lambda k: (k, 0)), pl.BlockSpec((tk, n), lambda k: (k, 0))],
        out_specs=pl.BlockSpec((m, n), lambda k: (0, 0)), out_shape=jax.ShapeDtypeStruct((m, n), out_dtype),
        scratch_shapes=[pltpu.VMEM((m, n), F32)], compiler_params=_cp("arbitrary"))(a, b)


def _rms_mm(x, gain, w, name):
    t, d = x.shape
    n = w.shape[0]
    tm = _tile(t, (512, 256, 128))

    def body(x_ref, g_ref, w_ref, h_ref, o_ref):
        xv = x_ref[...]
        r = lax.rsqrt(jnp.mean(xv * xv, axis=1, keepdims=True) + EPS)
        hv = (xv * r * g_ref[...]).astype(BF16)
        h_ref[...] = hv
        for c0, cw in _chunks(n, 512):
            o_ref[:, c0:c0 + cw] = _dot(hv, w_ref[c0:c0 + cw, :], _NT)

    row = pl.BlockSpec((tm, d), lambda i: (i, 0))
    return _tc_call(
        body, name=name, grid=(t // tm,),
        in_specs=[row, pl.BlockSpec((1, d), lambda i: (0, 0)), pl.BlockSpec(w.shape, lambda i: (0, 0))],
        out_specs=[row, pl.BlockSpec((tm, n), lambda i: (i, 0))],
        out_shape=[jax.ShapeDtypeStruct((t, d), BF16), jax.ShapeDtypeStruct((t, n), F32)],
        compiler_params=_cp("parallel"))(x, gain, w)


def _mm_rms_bwd(pairs, x, gain, res, name):
    t, d = x.shape
    tm = _tile(t, (512, 256, 128))
    n_pairs = len(pairs)

    def body(*refs):
        x_ref, g_ref, res_ref, dx_ref, dxb_ref, dg_ref, dn_ref = refs[2 * n_pairs:]
        for c0, cw in _chunks(d, 512):
            acc = None
            for p in range(n_pairs):
                part = _dot(refs[2 * p][...], refs[2 * p + 1][:, c0:c0 + cw], _NN)
                acc = part if acc is None else acc + part
            dn_ref[:, c0:c0 + cw] = acc
        xv = x_ref[...]
        r = lax.rsqrt(jnp.mean(xv * xv, axis=1, keepdims=True) + EPS)
        xh = xv * r
        dnv = dn_ref[...]
        dxh = dnv * g_ref[...]
        dx = res_ref[...] + r * (dxh - xh * jnp.mean(dxh * xh, axis=1, keepdims=True))
        dx_ref[...] = dx
        dxb_ref[...] = dx.astype(BF16)

        @pl.when(pl.program_id(0) == 0)
        def _():
            dg_ref[...] = jnp.zeros_like(dg_ref)

        dg_ref[...] += jnp.sum(dnv * xh, axis=0, keepdims=True)

    in_specs, args = [], []
    for a, w in pairs:
        assert a.dtype == BF16 and w.dtype == BF16, (name, a.dtype, w.dtype)
        in_specs += [pl.BlockSpec((tm, a.shape[1]), lambda i: (i, 0)), pl.BlockSpec(w.shape, lambda i: (0, 0))]
        args += [a, w]
    row = pl.BlockSpec((tm, d), lambda i: (i, 0))
    one = pl.BlockSpec((1, d), lambda i: (0, 0))
    return _tc_call(
        body, name=name, grid=(t // tm,), in_specs=in_specs + [row, one, row], out_specs=[row, row, one],
        out_shape=[jax.ShapeDtypeStruct((t, d), F32), jax.ShapeDtypeStruct((t, d), BF16),
                   jax.ShapeDtypeStruct((1, d), F32)],
        scratch_shapes=[pltpu.VMEM((tm, d), F32)], compiler_params=_cp("arbitrary"))(*args, x, gain, res)


def _ffn_up(x, gain, wg_t, wu_t, name):
    t, d = x.shape
    f = wg_t.shape[0]
    tm = _tile(t, (512, 256, 128))

    def body(x_ref, g_ref, wg_ref, wu_ref, n_ref, a_ref, b_ref, h_ref):
        xv = x_ref[...]
        r = lax.rsqrt(jnp.mean(xv * xv, axis=1, keepdims=True) + EPS)
        nv = (xv * r * g_ref[...]).astype(BF16)
        n_ref[...] = nv
        for c0, cw in _chunks(f, 256):
            a = _dot(nv, wg_ref[c0:c0 + cw, :], _NT)
            b = _dot(nv, wu_ref[c0:c0 + cw, :], _NT)
            a_ref[:, c0:c0 + cw] = a.astype(BF16)
            b_ref[:, c0:c0 + cw] = b.astype(BF16)
            h_ref[:, c0:c0 + cw] = (a * _sigmoid(a) * b).astype(BF16)

    w_spec = pl.BlockSpec((f, d), lambda i: (0, 0))
    x_spec = pl.BlockSpec((tm, d), lambda i: (i, 0))
    o_spec = pl.BlockSpec((tm, f), lambda i: (i, 0))
    o_shape = jax.ShapeDtypeStruct((t, f), BF16)
    return _tc_call(
        body, name=name, grid=(t // tm,), in_specs=[x_spec, pl.BlockSpec((1, d), lambda i: (0, 0)), w_spec, w_spec],
        out_specs=[x_spec] + [o_spec] * 3, out_shape=[jax.ShapeDtypeStruct((t, d), BF16)] + [o_shape] * 3,
        compiler_params=_cp("parallel"))(x, gain, wg_t, wu_t)


def _ffn_down_bwd(dxb, wd, a, b, name):
    t, d = dxb.shape
    f = wd.shape[0]
    tm = _tile(t, (512, 256, 128))

    def body(dx_ref, wd_ref, a_ref, b_ref, da_ref, db_ref):
        dxv = dx_ref[...]
        for c0, cw in _chunks(f, 256):
            dh = 0.5 * _dot(dxv, wd_ref[c0:c0 + cw, :], _NT)
            av = a_ref[:, c0:c0 + cw].astype(F32)
            bv = b_ref[:, c0:c0 + cw].astype(F32)
            sg = _sigmoid(av)
            da_ref[:, c0:c0 + cw] = (dh * bv * (sg * (1.0 + av * (1.0 - sg)))).astype(BF16)
            db_ref[:, c0:c0 + cw] = (dh * (av * sg)).astype(BF16)

    o_spec = pl.BlockSpec((tm, f), lambda i: (i, 0))
    o_shape = jax.ShapeDtypeStruct((t, f), BF16)
    return _tc_call(
        body, name=name, grid=(t // tm,),
        in_specs=[pl.BlockSpec((tm, d), lambda i: (i, 0)), pl.BlockSpec((f, d), lambda i: (0, 0)), o_spec, o_spec],
        out_specs=[o_spec] * 2, out_shape=[o_shape] * 2, compiler_params=_cp("parallel"))(dxb, wd, a, b)


def _loss_head(y, target):
    t, d = y.shape
    tm = _tile(t, (512, 256, 128))

    def body(y_ref, t_ref, dy_ref, dyb_ref, loss_ref, acc_ref):
        i = pl.program_id(0)
        e = y_ref[...] - t_ref[...]
        dy = e * (1.0 / d)
        dy_ref[...] = dy
        dyb_ref[...] = dy.astype(BF16)

        @pl.when(i == 0)
        def _():
            acc_ref[...] = jnp.zeros_like(acc_ref)

        acc_ref[...] += jnp.sum(e * e, axis=0, keepdims=True)

        @pl.when(i == pl.num_programs(0) - 1)
        def _():
            loss_ref[...] = jnp.sum(acc_ref[...], axis=1, keepdims=True) * (0.5 / d)

    row = pl.BlockSpec((tm, d), lambda i: (i, 0))
    return _tc_call(
        body, name="loss_head", grid=(t // tm,), in_specs=[row, row],
        out_specs=[row, row, pl.BlockSpec((1, 1), lambda i: (0, 0))],
        out_shape=[jax.ShapeDtypeStruct((t, d), F32), jax.ShapeDtypeStruct((t, d), BF16),
                   jax.ShapeDtypeStruct((1, 1), F32)],
        scratch_shapes=[pltpu.VMEM((1, d), F32)], compiler_params=_cp("arbitrary"))(y, target)


def _merge_fwd(oa, ob, proj, wa_t, wb_t, name):
    t = oa.shape[0]
    d = wa_t.shape[0]
    tm = _tile(t, (512, 256, 128))

    def body(oa_ref, ob_ref, ga_ref, gb_ref, wa_ref, wb_ref, mg_ref, ya_ref, yb_ref):
        oav, obv = oa_ref[...], ob_ref[...]
        for c0, cw in _chunks(d, 512):
            cs = slice(c0, c0 + cw)
            ya = _dot(oav, wa_ref[cs, :], _NT)
            yb = _dot(obv, wb_ref[cs, :], _NT)
            mg_ref[:, cs] = (_sigmoid(ga_ref[:, cs]) * ya + _sigmoid(gb_ref[:, cs]) * yb).astype(BF16)
            ya_ref[:, cs] = ya.astype(BF16)
            yb_ref[:, cs] = yb.astype(BF16)

    o_spec = pl.BlockSpec((tm, d), lambda i: (i, 0))
    o_shape = jax.ShapeDtypeStruct((t, d), BF16)
    return _tc_call(
        body, name=name, grid=(t // tm,),
        in_specs=[pl.BlockSpec((tm, oa.shape[1]), lambda i: (i, 0)), pl.BlockSpec((tm, ob.shape[1]), lambda i: (i, 0)),
                  pl.BlockSpec((tm, d), lambda i: (i, P_GA // d)), pl.BlockSpec((tm, d), lambda i: (i, P_GB // d)),
                  pl.BlockSpec(wa_t.shape, lambda i: (0, 0)), pl.BlockSpec(wb_t.shape, lambda i: (0, 0))],
        out_specs=[o_spec] * 3, out_shape=[o_shape] * 3,
        compiler_params=_cp("parallel"))(oa, ob, proj, proj, wa_t, wb_t)


def _merge_bwd(dxb, wo, proj, ya, yb, name):
    t, d = dxb.shape
    tm = _tile(t, (512, 256, 128))

    def body(dx_ref, wo_ref, ga_ref, gb_ref, ya_ref, yb_ref, dya_ref, dyb_ref, dga_ref, dgb_ref):
        dxv = dx_ref[...]
        for c0, cw in _chunks(d, 512):
            cs = slice(c0, c0 + cw)
            dm = _dot(dxv, wo_ref[cs, :], _NT)
            sa = _sigmoid(ga_ref[:, cs])
            sb = _sigmoid(gb_ref[:, cs])
            dya_ref[:, cs] = (dm * sa).astype(BF16)
            dyb_ref[:, cs] = (dm * sb).astype(BF16)
            dga_ref[:, cs] = (dm * ya_ref[:, cs].astype(F32) * (sa * (1.0 - sa))).astype(BF16)
            dgb_ref[:, cs] = (dm * yb_ref[:, cs].astype(F32) * (sb * (1.0 - sb))).astype(BF16)

    o_spec = pl.BlockSpec((tm, d), lambda i: (i, 0))
    o_shape = jax.ShapeDtypeStruct((t, d), BF16)
    return _tc_call(
        body, name=name, grid=(t // tm,),
        in_specs=[o_spec, pl.BlockSpec((d, d), lambda i: (0, 0)),
                  pl.BlockSpec((tm, d), lambda i: (i, P_GA // d)), pl.BlockSpec((tm, d), lambda i: (i, P_GB // d)),
                  o_spec, o_spec],
        out_specs=[o_spec] * 4, out_shape=[o_shape] * 4,
        compiler_params=_cp("parallel"))(dxb, wo, proj, proj, ya, yb)


def _swa_common(has_prev, pq, pk):
    dist = (pq - pk).astype(F32)
    row = lax.broadcasted_iota(jnp.int32, (BLOCK, 2 * BLOCK), 0)
    col = lax.broadcasted_iota(jnp.int32, (BLOCK, 2 * BLOCK), 1)
    diff = row + BLOCK - col
    valid = (diff >= 0) & (diff < BLOCK) & (has_prev | (col >= BLOCK))
    return jnp.concatenate([dist] * N_HEADS_A, axis=0), jnp.concatenate([valid] * N_HEADS_A, axis=0)


def _half_sum(x, lo):
    s_lo = jnp.sum(jnp.where(lo, x, 0.0), axis=1, keepdims=True)
    s_hi = jnp.sum(jnp.where(lo, 0.0, x), axis=1, keepdims=True)
    return jnp.where(lo, s_lo, s_hi)


def _norm2(x, gain2, lo):
    r = lax.rsqrt(_half_sum(x * x, lo) * (1.0 / HEAD_DIM_A) + EPS)
    xh = x * r
    return xh * gain2, xh, r


def _norm2_bwd(d, xh, r, gain2, lo):
    dxh = d * gain2
    return r * (dxh - xh * (_half_sum(dxh * xh, lo) * (1.0 / HEAD_DIM_A)))


def _swa_stack(tiles, lo):
    zero = jnp.zeros_like(tiles[0])
    return jnp.concatenate([jnp.where(lo, t, zero) for t in tiles] + [jnp.where(lo, zero, t) for t in tiles], axis=0)


def _swa_unstack(x8, j, lo):
    return jnp.where(lo, x8[j * BLOCK:(j + 1) * BLOCK], x8[(GROUP_A + j) * BLOCK:(GROUP_A + j + 1) * BLOCK])


def _swa_head_columns(sk_ref):
    slope = jnp.concatenate([jnp.full((BLOCK, 1), 2.0 ** (-(h + 1)), F32) for h in range(N_HEADS_A)], axis=0)
    sink = jnp.concatenate([jnp.broadcast_to(sk_ref[:, h:h + 1], (BLOCK, 1)) for h in range(N_HEADS_A)], axis=0)
    return slope, sink


def _swa_blocks_per_step(s_len):
    nb = s_len // BLOCK
    return 4 if nb % 4 == 0 and nb >= 8 else 2 if nb % 2 == 0 else 1


def _swa_specs(s_len):
    nb, qb = s_len // BLOCK, _swa_blocks_per_step(s_len)
    ns, rows = nb // qb, qb * BLOCK

    def step(b, j):
        return b * ns + j

    def prev(b, j):
        return b * nb + jnp.maximum(qb * j - 1, 0)

    q_spec = pl.BlockSpec((rows, WIDTH_A), lambda b, j: (step(b, j), P_QA // WIDTH_A))
    kc_spec = pl.BlockSpec((rows, KV_A), lambda b, j: (step(b, j), P_KA // KV_A))
    kp_spec = pl.BlockSpec((BLOCK, KV_A), lambda b, j: (prev(b, j), P_KA // KV_A))
    vc_spec = pl.BlockSpec((rows, KV_A), lambda b, j: (step(b, j), P_VA // KV_A))
    vp_spec = pl.BlockSpec((BLOCK, KV_A), lambda b, j: (prev(b, j), P_VA // KV_A))
    pq_spec = pl.BlockSpec((rows, 1), lambda b, j: (step(b, j), 0))
    pkc_spec = pl.BlockSpec((qb, 1, BLOCK), lambda b, j: (step(b, j), 0, 0))
    pkp_spec = pl.BlockSpec((1, 1, BLOCK), lambda b, j: (prev(b, j), 0, 0))
    return qb, ns, step, [q_spec, kc_spec, kp_spec, vc_spec, vp_spec, pq_spec, pkc_spec, pkp_spec]


def _swa_stage_probs(qb, q_ref, kc_ref, kp_ref, vc_ref, vp_ref, pq_ref, pkc_ref, pkp_ref, qg_ref, kg_ref, sk_ref, lo):
    first = pl.program_id(1) * qb
    kk_all = _norm2(jnp.concatenate([kp_ref[...], kc_ref[...]], axis=0), kg_ref[...], lo)[0].astype(BF16)
    vv_all = jnp.concatenate([vp_ref[...], vc_ref[...]], axis=0).astype(BF16)
    pk_all = jnp.concatenate([pkp_ref[0]] + [pkc_ref[s] for s in range(qb)], axis=1)
    rows = [slice(s * BLOCK, (s + 1) * BLOCK) for s in range(qb)]
    keys = [slice(s * BLOCK, (s + 2) * BLOCK) for s in range(qb)]
    masks = [_swa_common(first + s > 0, pq_ref[rows[s], :], pk_all[:, keys[s]]) for s in range(qb)]
    qs = [[_norm2(q_ref[r, j * HEAD_PAD:(j + 1) * HEAD_PAD], qg_ref[...], lo) for j in range(GROUP_A)] for r in rows]
    q8 = [_swa_stack([q[0] for q in tiles], lo).astype(BF16) for tiles in qs]
    kk = [kk_all[ks] for ks in keys]
    vv = [vv_all[ks] for ks in keys]
    slope, sink = _swa_head_columns(sk_ref)
    s = [_dot(q8[b], kk[b], _NT) * (HEAD_DIM_A ** -0.5) - slope * masks[b][0] for b in range(qb)]
    s = [jnp.where(masks[b][1], s[b], NEG) for b in range(qb)]
    m = [jnp.maximum(jnp.max(x, axis=1, keepdims=True), sink) for x in s]
    e = [jnp.exp(x - mx) for x, mx in zip(s, m)]
    es = [jnp.exp(sink - mx) for mx in m]
    inv = [1.0 / (jnp.sum(x, axis=1, keepdims=True) + y) for x, y in zip(e, es)]
    p = [x * i for x, i in zip(e, inv)]
    ps = [y * i for y, i in zip(es, inv)]
    return rows, qs, q8, kk, vv, p, ps


def _swa_fwd(proj, pos_col, pos_row, qg2, kg2, sinks, n_batch, s_len, name):
    t = proj.shape[0]
    qb, ns, step, specs = _swa_specs(s_len)
    small = pl.BlockSpec((1, HEAD_PAD), lambda b, j: (0, 0))

    def body(q_ref, kc_ref, kp_ref, vc_ref, vp_ref, pq_ref, pkc_ref, pkp_ref, qg_ref, kg_ref, sk_ref, o_ref):
        lo = lax.broadcasted_iota(jnp.int32, (1, HEAD_PAD), 1) < HEAD_DIM_A
        rows, _, _, _, vv, p, _ = _swa_stage_probs(qb, q_ref, kc_ref, kp_ref, vc_ref, vp_ref, pq_ref, pkc_ref, pkp_ref,
                                                   qg_ref, kg_ref, sk_ref, lo)
        o8 = [_dot(p[b].astype(BF16), vv[b], _NN) for b in range(qb)]
        for b in range(qb):
            for j in range(GROUP_A):
                o_ref[rows[b], j * HEAD_PAD:(j + 1) * HEAD_PAD] = _swa_unstack(o8[b], j, lo).astype(BF16)

    return _tc_call(
        body, name=name, grid=(n_batch, ns), in_specs=specs + [small, small, small],
        out_specs=pl.BlockSpec((qb * BLOCK, WIDTH_A), lambda b, j: (step(b, j), 0)),
        out_shape=jax.ShapeDtypeStruct((t, WIDTH_A), BF16),
        compiler_params=_cp("parallel", "parallel"))(proj, proj, proj, proj, proj, pos_col, pos_row, pos_row,
                                                     qg2, kg2, sinks)


def _swa_bwd(proj, pos_col, pos_row, qg2, kg2, sinks, do, n_batch, s_len, name):
    t = proj.shape[0]
    qb, ns, step, specs = _swa_specs(s_len)
    small = pl.BlockSpec((1, HEAD_PAD), lambda b, j: (0, 0))
    scale = HEAD_DIM_A ** -0.5

    def body(q_ref, kc_ref, kp_ref, vc_ref, vp_ref, pq_ref, pkc_ref, pkp_ref, qg_ref, kg_ref, sk_ref, do_ref,
             dq_ref, dkc_ref, dkp_ref, dvc_ref, dvp_ref, dqg_ref, dsk_ref):
        @pl.when((pl.program_id(0) == 0) & (pl.program_id(1) == 0))
        def _():
            dqg_ref[...] = jnp.zeros_like(dqg_ref)
            dsk_ref[...] = jnp.zeros_like(dsk_ref)

        lane = lax.broadcasted_iota(jnp.int32, (1, HEAD_PAD), 1)
        lo = lane < HEAD_DIM_A
        rows, qs, q8, kk, vv, p, ps = _swa_stage_probs(qb, q_ref, kc_ref, kp_ref, vc_ref, vp_ref, pq_ref, pkc_ref,
                                                       pkp_ref, qg_ref, kg_ref, sk_ref, lo)
        blocks = range(qb)
        do8 = [_swa_stack([do_ref[r, j * HEAD_PAD:(j + 1) * HEAD_PAD] for j in range(GROUP_A)], lo) for r in rows]
        dp = [_dot(do8[b], vv[b], _NT) for b in blocks]
        delta = [jnp.sum(p[b] * dp[b], axis=1, keepdims=True) for b in blocks]
        ds = [(p[b] * (dp[b] - delta[b]) * scale).astype(BF16) for b in blocks]
        dsink = [ps[b] * delta[b] for b in blocks]
        dvv = [_dot(p[b].astype(BF16), do8[b], _TN) for b in blocks]
        dkk = [_dot(ds[b], q8[b], _TN) for b in blocks]
        dq8 = [_dot(ds[b], kk[b], _NN) for b in blocks]
        dsk = jnp.zeros((1, HEAD_PAD), F32)
        dqg = jnp.zeros((1, HEAD_PAD), F32)
        for b in blocks:
            for h in range(N_HEADS_A):
                dsk = dsk + jnp.where(lane == h, -jnp.sum(dsink[b][h * BLOCK:(h + 1) * BLOCK]), 0.0)
            for j in range(GROUP_A):
                _, xh, r = qs[b][j]
                dqn = _swa_unstack(dq8[b], j, lo)
                dqg = dqg + jnp.sum(dqn * xh, axis=0, keepdims=True)
                dq_ref[rows[b], j * HEAD_PAD:(j + 1) * HEAD_PAD] = _norm2_bwd(dqn, xh, r, qg_ref[...], lo).astype(BF16)
            dkp_ref[rows[b], :] = dkk[b][:BLOCK]
            dkc_ref[rows[b], :] = dkk[b][BLOCK:]
            dvp_ref[rows[b], :] = dvv[b][:BLOCK]
            dvc_ref[rows[b], :] = dvv[b][BLOCK:]
        dqg_ref[...] += dqg
        dsk_ref[...] += dsk

    kv_out = pl.BlockSpec((qb * BLOCK, KV_A), lambda b, j: (step(b, j), 0))
    kv_shape = jax.ShapeDtypeStruct((t, KV_A), F32)
    wide = pl.BlockSpec((qb * BLOCK, WIDTH_A), lambda b, j: (step(b, j), 0))
    return _tc_call(
        body, name=name, grid=(n_batch, ns), in_specs=specs + [small, small, small, wide],
        out_specs=[wide, kv_out, kv_out, kv_out, kv_out, small, small],
        out_shape=[jax.ShapeDtypeStruct((t, WIDTH_A), BF16), kv_shape, kv_shape, kv_shape, kv_shape,
                   jax.ShapeDtypeStruct((1, HEAD_PAD), F32), jax.ShapeDtypeStruct((1, HEAD_PAD), F32)],
        compiler_params=_cp("arbitrary", "arbitrary"))(proj, proj, proj, proj, proj, pos_col, pos_row, pos_row,
                                                       qg2, kg2, sinks, do)


def _swa_kv_bwd(proj, kg2, dkc, dkp, dvc, dvp, n_batch, s_len, name):
    t = proj.shape[0]
    nb = s_len // BLOCK

    def rowblk(b, i):
        return b * nb + i

    def nextblk(b, i):
        return b * nb + jnp.minimum(i + 1, nb - 1)

    def body(k_ref, kg_ref, dkc_ref, dkp_ref, dvc_ref, dvp_ref, dk_ref, dv_ref, dkg_ref):
        b, i = pl.program_id(0), pl.program_id(1)

        @pl.when((b == 0) & (i == 0))
        def _():
            dkg_ref[...] = jnp.zeros_like(dkg_ref)

        lo = lax.broadcasted_iota(jnp.int32, (1, HEAD_PAD), 1) < HEAD_DIM_A
        has_next = (i < nb - 1).astype(F32)
        dkn = dkc_ref[...] + has_next * dkp_ref[...]
        dv_ref[...] = (dvc_ref[...] + has_next * dvp_ref[...]).astype(BF16)
        _, xh, r = _norm2(k_ref[...], kg_ref[...], lo)
        dkg_ref[...] += jnp.sum(dkn * xh, axis=0, keepdims=True)
        dk_ref[...] = _norm2_bwd(dkn, xh, r, kg_ref[...], lo).astype(BF16)

    cur = pl.BlockSpec((BLOCK, KV_A), lambda b, i: (rowblk(b, i), 0))
    nxt = pl.BlockSpec((BLOCK, KV_A), lambda b, i: (nextblk(b, i), 0))
    small = pl.BlockSpec((1, HEAD_PAD), lambda b, i: (0, 0))
    return _tc_call(
        body, name=name, grid=(n_batch, nb),
        in_specs=[pl.BlockSpec((BLOCK, KV_A), lambda b, i: (rowblk(b, i), P_KA // KV_A)), small, cur, nxt, cur, nxt],
        out_specs=[cur, cur, small],
        out_shape=[jax.ShapeDtypeStruct((t, KV_A), BF16), jax.ShapeDtypeStruct((t, KV_A), BF16),
                   jax.ShapeDtypeStruct((1, HEAD_PAD), F32)],
        compiler_params=_cp("arbitrary", "arbitrary"))(proj, kg2, dkc, dkp, dvc, dvp)


def _lora_norm(x, gain):
    r = lax.rsqrt(jnp.mean(x * x, axis=1, keepdims=True) + EPS)
    xh = x * r
    return xh * gain, xh, r


def _mla_in_specs(tm):
    row = lambda w, off: pl.BlockSpec((tm, w), lambda i: (i, off // w))
    one = lambda w: pl.BlockSpec((1, w), lambda i: (0, 0))
    full = lambda r, c: pl.BlockSpec((r, c), lambda i: (0, 0))
    tab = pl.BlockSpec((tm, HEAD_PAD), lambda i: (i, 0))
    return [row(Q_LORA, P_CQ), row(KV_LORA, P_CKV), row(HEAD_PAD, P_KR), tab, tab, tab,
            one(Q_LORA), one(KV_LORA), one(HEAD_PAD), one(HEAD_PAD),
            full(WIDTH_BP, Q_LORA), full(WIDTH_BP, KV_LORA), full(WIDTH_BP, KV_LORA)]


def _mla_pre(proj, tabs, gq, gkv, gqn, gkn, wuq, wk, wv, name):
    t = proj.shape[0]
    tm = _tile(t, (256, 128))

    def body(cq_ref, ckv_ref, kr_ref, c_ref, sm_ref, sp_ref, gq_ref, gkv_ref, gqn_ref, gkn_ref,
             wuq_ref, wk_ref, wv_ref, q_ref, k_ref, v_ref):
        cqn = _lora_norm(cq_ref[...], gq_ref[...])[0].astype(BF16)
        ckvn = _lora_norm(ckv_ref[...], gkv_ref[...])[0].astype(BF16)
        q_raw = _dot(cqn, wuq_ref[...], _NT)
        k_raw = _dot(ckvn, wk_ref[...], _NT)
        v_ref[...] = _dot(ckvn, wv_ref[...], _NT).astype(BF16)
        kr = pltpu.roll(kr_ref[...], NOPE, 1)
        c, sm, sp = c_ref[...], sm_ref[...], sp_ref[...]
        heads = [slice(h * HEAD_PAD, (h + 1) * HEAD_PAD) for h in range(N_HEADS_B)]
        raw = [q_raw[:, hs] for hs in heads] + [k_raw[:, hs] + kr for hs in heads]
        gains = [gqn_ref[...]] * N_HEADS_B + [gkn_ref[...]] * N_HEADS_B
        sq = [jnp.sum(x * x, axis=1, keepdims=True) for x in raw]
        normed = [x * lax.rsqrt(s * (1.0 / QK_B) + EPS) * g for x, s, g in zip(raw, sq, gains)]
        up = [pltpu.roll(x, HEAD_PAD - ROPE // 2, 1) for x in normed]
        down = [pltpu.roll(x, ROPE // 2, 1) for x in normed]
        roped = [(x * c + u * sm + d * sp).astype(BF16) for x, u, d in zip(normed, up, down)]
        for h, hs in enumerate(heads):
            q_ref[:, hs] = roped[h]
            k_ref[:, hs] = roped[N_HEADS_B + h]

    o_spec = pl.BlockSpec((tm, WIDTH_BP), lambda i: (i, 0))
    o_shape = jax.ShapeDtypeStruct((t, WIDTH_BP), BF16)
    return _tc_call(
        body, name=name, grid=(t // tm,), in_specs=_mla_in_specs(tm), out_specs=[o_spec] * 3,
        out_shape=[o_shape] * 3, compiler_params=_cp("parallel"))(
            proj, proj, proj, *tabs, gq, gkv, gqn, gkn, wuq, wk, wv)


def _mla_pre_bwd(proj, tabs, gq, gkv, gqn, gkn, wuq, wk, wv, dq, dk, dv, name):
    t = proj.shape[0]
    tm = _tile(t, (256, 128))

    def body(cq_ref, ckv_ref, kr_ref, c_ref, sm_ref, sp_ref, gq_ref, gkv_ref, gqn_ref, gkn_ref,
             wuq_ref, wk_ref, wv_ref, dq_ref, dk_ref, dv_ref,
             dcq_ref, dckv_ref, dkr_ref, dwuq_ref, dwk_ref, dwv_ref, dgq_ref, dgkv_ref, dgqn_ref, dgkn_ref,
             dqraw_ref, dkraw_ref):
        @pl.when(pl.program_id(0) == 0)
        def _():
            for r in (dwuq_ref, dwk_ref, dwv_ref, dgq_ref, dgkv_ref, dgqn_ref, dgkn_ref):
                r[...] = jnp.zeros_like(r)

        cqn_f, cq_xh, cq_r = _lora_norm(cq_ref[...], gq_ref[...])
        ckvn_f, ckv_xh, ckv_r = _lora_norm(ckv_ref[...], gkv_ref[...])
        cqn, ckvn = cqn_f.astype(BF16), ckvn_f.astype(BF16)
        q_raw = _dot(cqn, wuq_ref[...], _NT)
        k_raw = _dot(ckvn, wk_ref[...], _NT)
        kr = pltpu.roll(kr_ref[...], NOPE, 1)
        c, sm, sp = c_ref[...], sm_ref[...], sp_ref[...]
        heads = [slice(h * HEAD_PAD, (h + 1) * HEAD_PAD) for h in range(N_HEADS_B)]
        raw = [q_raw[:, hs] for hs in heads] + [k_raw[:, hs] + kr for hs in heads]
        d_out = [dq_ref[:, hs] for hs in heads] + [dk_ref[:, hs] for hs in heads]
        gains = [gqn_ref[...]] * N_HEADS_B + [gkn_ref[...]] * N_HEADS_B
        sq = [jnp.sum(x * x, axis=1, keepdims=True) for x in raw]
        rinv = [lax.rsqrt(s * (1.0 / QK_B) + EPS) for s in sq]
        xhat = [x * r for x, r in zip(raw, rinv)]
        down = [pltpu.roll(d * sm, ROPE // 2, 1) for d in d_out]
        up = [pltpu.roll(d * sp, HEAD_PAD - ROPE // 2, 1) for d in d_out]
        dn = [d * c + a + b for d, a, b in zip(d_out, down, up)]
        dgain = [jnp.sum(d * xh, axis=0, keepdims=True) for d, xh in zip(dn, xhat)]
        dxh = [d * g for d, g in zip(dn, gains)]
        inner = [jnp.sum(d * xh, axis=1, keepdims=True) * (1.0 / QK_B) for d, xh in zip(dxh, xhat)]
        d_raw = [r * (d - xh * s) for r, d, xh, s in zip(rinv, dxh, xhat, inner)]
        for h, hs in enumerate(heads):
            dqraw_ref[:, hs] = d_raw[h].astype(BF16)
            dkraw_ref[:, hs] = d_raw[N_HEADS_B + h].astype(BF16)
        dkr = sum(d_raw[N_HEADS_B + 1:], d_raw[N_HEADS_B])
        dgqn_ref[...] += sum(dgain[1:N_HEADS_B], dgain[0])
        dgkn_ref[...] += sum(dgain[N_HEADS_B + 1:], dgain[N_HEADS_B])
        lane = lax.broadcasted_iota(jnp.int32, (tm, HEAD_PAD), 1)
        dkr_ref[...] = jnp.where(lane < ROPE, pltpu.roll(dkr, HEAD_PAD - NOPE, 1), 0.0).astype(BF16)
        dqraw = dqraw_ref[...]
        dkraw = dkraw_ref[...]
        dvb = dv_ref[...].astype(BF16)
        dwuq_ref[...] += _dot(dqraw, cqn, _TN)
        dwk_ref[...] += _dot(dkraw, ckvn, _TN)
        dwv_ref[...] += _dot(dvb, ckvn, _TN)
        dcqn = _dot(dqraw, wuq_ref[...], _NN)
        dckvn = _dot(dkraw, wk_ref[...], _NN) + _dot(dvb, wv_ref[...], _NN)
        dgq_ref[...] += jnp.sum(dcqn * cq_xh, axis=0, keepdims=True)
        dgkv_ref[...] += jnp.sum(dckvn * ckv_xh, axis=0, keepdims=True)
        dxh = dcqn * gq_ref[...]
        dcq_ref[...] = (cq_r * (dxh - cq_xh * jnp.mean(dxh * cq_xh, axis=1, keepdims=True))).astype(BF16)
        dxh = dckvn * gkv_ref[...]
        dckv_ref[...] = (ckv_r * (dxh - ckv_xh * jnp.mean(dxh * ckv_xh, axis=1, keepdims=True))).astype(BF16)

    wide = pl.BlockSpec((tm, WIDTH_BP), lambda i: (i, 0))
    row = lambda w: pl.BlockSpec((tm, w), lambda i: (i, 0))
    full = lambda r, c: pl.BlockSpec((r, c), lambda i: (0, 0))
    return _tc_call(
        body, name=name, grid=(t // tm,), in_specs=_mla_in_specs(tm) + [wide, wide, wide],
        out_specs=[row(Q_LORA), row(KV_LORA), row(HEAD_PAD), full(WIDTH_BP, Q_LORA), full(WIDTH_BP, KV_LORA),
                   full(WIDTH_BP, KV_LORA), full(1, Q_LORA), full(1, KV_LORA), full(1, HEAD_PAD), full(1, HEAD_PAD)],
        out_shape=[jax.ShapeDtypeStruct((t, Q_LORA), BF16), jax.ShapeDtypeStruct((t, KV_LORA), BF16),
                   jax.ShapeDtypeStruct((t, HEAD_PAD), BF16), jax.ShapeDtypeStruct((WIDTH_BP, Q_LORA), F32),
                   jax.ShapeDtypeStruct((WIDTH_BP, KV_LORA), F32), jax.ShapeDtypeStruct((WIDTH_BP, KV_LORA), F32),
                   jax.ShapeDtypeStruct((1, Q_LORA), F32), jax.ShapeDtypeStruct((1, KV_LORA), F32),
                   jax.ShapeDtypeStruct((1, HEAD_PAD), F32), jax.ShapeDtypeStruct((1, HEAD_PAD), F32)],
        scratch_shapes=[pltpu.VMEM((tm, WIDTH_BP), BF16), pltpu.VMEM((tm, WIDTH_BP), BF16)],
        compiler_params=_cp("arbitrary"))(proj, proj, proj, *tabs, gq, gkv, gqn, gkn, wuq, wk, wv, dq, dk, dv)


def _mla_flash_specs(s_len):
    bh_spec = pl.BlockSpec((s_len, HEAD_PAD), lambda b, h: (b, h))
    lse_spec = pl.BlockSpec((1, s_len, 1), lambda b, h: (b * N_HEADS_B + h, 0, 0))
    return bh_spec, lse_spec


def _diag_mask(s):
    row = lax.broadcasted_iota(jnp.int32, s.shape, 0)
    col = lax.broadcasted_iota(jnp.int32, s.shape, 1)
    return jnp.where(row >= col, s, NEG)


def _mla_flash_fwd(q, k, v, n_batch, s_len, name):
    t = q.shape[0]
    tq = _tile(s_len, (256, 128))
    bh_spec, lse_spec = _mla_flash_specs(s_len)
    c = (QK_B ** -0.5) * LOG2E

    def body(q_ref, k_ref, v_ref, o_ref, lse_ref):
        nq = s_len // tq
        rows = [slice(i * tq, (i + 1) * tq) for i in range(nq)]
        below = [slice(0, i * tq) for i in range(nq)]
        qs = [q_ref[r, :] for r in rows]
        sd = [_diag_mask(_dot(qs[i], k_ref[rows[i], :], _NT)) for i in range(nq)]
        sb = [None] + [_dot(qs[i], k_ref[below[i], :], _NT) for i in range(1, nq)]
        m = [jnp.max(s, axis=1, keepdims=True) for s in sd]
        m = [m[0]] + [jnp.maximum(m[i], jnp.max(sb[i], axis=1, keepdims=True)) for i in range(1, nq)]
        pd = [jnp.exp2((sd[i] - m[i]) * c) for i in range(nq)]
        pb = [None] + [jnp.exp2((sb[i] - m[i]) * c) for i in range(1, nq)]
        l = [jnp.sum(p, axis=1, keepdims=True) for p in pd]
        l = [l[0]] + [l[i] + jnp.sum(pb[i], axis=1, keepdims=True) for i in range(1, nq)]
        acc = [_dot(pd[i].astype(BF16), v_ref[rows[i], :], _NN) for i in range(nq)]
        acc = [acc[0]] + [acc[i] + _dot(pb[i].astype(BF16), v_ref[below[i], :], _NN) for i in range(1, nq)]
        for i in range(nq):
            o_ref[rows[i], :] = (acc[i] * (1.0 / l[i])).astype(BF16)
            lse_ref[0, rows[i], :] = m[i] * c + jnp.log2(l[i])

    return _tc_call(
        body, name=name, grid=(n_batch, N_HEADS_B), in_specs=[bh_spec, bh_spec, bh_spec],
        out_specs=[bh_spec, lse_spec],
        out_shape=[jax.ShapeDtypeStruct((t, WIDTH_BP), BF16),
                   jax.ShapeDtypeStruct((n_batch * N_HEADS_B, s_len, 1), F32)],
        compiler_params=_cp("parallel", "parallel"))(q, k, v)


def _mla_flash_bwd(q, k, v, o, do, lse2, n_batch, s_len, name):
    t = q.shape[0]
    tq = _tile(s_len, (256, 128))
    bh_spec, lse_spec = _mla_flash_specs(s_len)
    scale = QK_B ** -0.5
    c = scale * LOG2E

    def body(q_ref, k_ref, v_ref, o_ref, do_ref, lse_ref, dq_ref, dk_ref, dv_ref):
        nq = s_len // tq
        rows = [slice(i * tq, (i + 1) * tq) for i in range(nq)]
        below = [slice(0, i * tq) for i in range(nq)]
        qs = [q_ref[r, :] for r in rows]
        dos = [do_ref[r, :] for r in rows]
        lse = [lse_ref[0, r, :] for r in rows]
        delta = [jnp.sum(dos[i].astype(F32) * o_ref[rows[i], :].astype(F32), axis=1, keepdims=True)
                 for i in range(nq)]

        def probs_and_ds(i, ks, diag):
            s = _dot(qs[i], k_ref[ks, :], _NT)
            if diag:
                s = _diag_mask(s)
            p = jnp.exp2(s * c - lse[i])
            dp = _dot(dos[i], v_ref[ks, :], _NT)
            return p.astype(BF16), (p * (dp - delta[i]) * scale).astype(BF16)

        diag = [probs_and_ds(i, rows[i], True) for i in range(nq)]
        rest = [None] + [probs_and_ds(i, below[i], False) for i in range(1, nq)]
        for i in range(nq):
            dq = _dot(diag[i][1], k_ref[rows[i], :], _NN)
            if i:
                dq = dq + _dot(rest[i][1], k_ref[below[i], :], _NN)
            dq_ref[rows[i], :] = dq
        for j in range(nq):
            later = slice(j * tq, s_len)
            p_j = jnp.concatenate([diag[j][0]] + [rest[i][0][:, rows[j]] for i in range(j + 1, nq)], axis=0)
            ds_j = jnp.concatenate([diag[j][1]] + [rest[i][1][:, rows[j]] for i in range(j + 1, nq)], axis=0)
            dk_ref[rows[j], :] = _dot(ds_j, q_ref[later, :], _TN)
            dv_ref[rows[j], :] = _dot(p_j, do_ref[later, :], _TN)

    f32_wide = jax.ShapeDtypeStruct((t, WIDTH_BP), F32)
    return _tc_call(
        body, name=name, grid=(n_batch, N_HEADS_B),
        in_specs=[bh_spec, bh_spec, bh_spec, bh_spec, bh_spec, lse_spec],
        out_specs=[bh_spec, bh_spec, bh_spec], out_shape=[f32_wide] * 3,
        compiler_params=_cp("parallel", "parallel"))(q, k, v, o, do, lse2)


def _swa_heads(w, axis, order):
    heads = [lax.slice_in_dim(w, h * HEAD_DIM_A, (h + 1) * HEAD_DIM_A, axis=axis) for h in order]
    return jnp.concatenate(heads, axis=axis)


class _LayerWeights:
    def __init__(self, build):
        self._build, self._mats = build, {}

    def __getitem__(self, name):
        if name not in self._mats:
            self._mats.update(self._build(name))
        return self._mats[name]


def _layer_mats(w):
    if "w_in" not in w:
        return dict(w)
    w_in = w["w_in"]
    o = [0]
    for n in (WIDTH_A, KV_A, KV_A, Q_LORA, KV_LORA, ROPE, D_MODEL, D_MODEL):
        o.append(o[-1] + n)
    qa, ka, va, cq, ckv, kr, ga, gb = (w_in[o[i]:o[i + 1]] for i in range(8))
    pad = jnp.zeros((PROJ_W - IN_WIDTH, w_in.shape[1]), w_in.dtype)
    w_in_p = jnp.concatenate([ga, gb, _swa_heads(qa, 0, SWA_HEAD_ORDER), cq, ka, va, ckv, kr, pad], axis=0)
    uq = w["mla_w_uq"].reshape(N_HEADS_B, QK_B, Q_LORA)
    uq = jnp.pad(uq, ((0, 0), (0, HEAD_PAD - QK_B), (0, 0))).reshape(WIDTH_BP, Q_LORA)
    ukv = w["mla_w_ukv"].reshape(N_HEADS_B, NOPE + V_B, KV_LORA)
    wk = jnp.pad(ukv[:, :NOPE], ((0, 0), (0, HEAD_PAD - NOPE), (0, 0))).reshape(WIDTH_BP, KV_LORA)
    wv = jnp.pad(ukv[:, NOPE:], ((0, 0), (0, HEAD_PAD - V_B), (0, 0))).reshape(WIDTH_BP, KV_LORA)
    wb = w["w_branch_b"].reshape(D_MODEL, N_HEADS_B, V_B)
    wb = jnp.pad(wb, ((0, 0), (0, 0), (0, HEAD_PAD - V_B))).reshape(D_MODEL, WIDTH_BP)
    out = dict(w)
    out.update(w_in=w_in_p, mla_w_uq=uq, wk=wk, wv=wv, w_branch_b=wb,
               w_branch_a=_swa_heads(w["w_branch_a"], 1, SWA_HEAD_ORDER))
    return out


def _unlayer_grads(g):
    if "w_in" not in g:
        return g
    d = g["w_in"]
    ga, gb, qa, cq, ka, va, ckv, kr = (d[a:b] for a, b in (
        (P_GA, P_GA + D_MODEL), (P_GB, P_GB + D_MODEL), (P_QA, P_QA + WIDTH_A), (P_CQ, P_CQ + Q_LORA),
        (P_KA, P_KA + KV_A), (P_VA, P_VA + KV_A), (P_CKV, P_CKV + KV_LORA), (P_KR, P_KR + ROPE)))
    out = {n: v for n, v in g.items() if n not in ("wk", "wv")}
    out["w_in"] = jnp.concatenate([_swa_heads(qa, 0, SWA_HEAD_INVERSE), ka, va, cq, ckv, kr, ga, gb], axis=0)
    out["mla_w_uq"] = g["mla_w_uq"].reshape(N_HEADS_B, HEAD_PAD, Q_LORA)[:, :QK_B].reshape(N_HEADS_B * QK_B, Q_LORA)
    dk = g["wk"].reshape(N_HEADS_B, HEAD_PAD, KV_LORA)[:, :NOPE]
    dv = g["wv"].reshape(N_HEADS_B, HEAD_PAD, KV_LORA)[:, :V_B]
    out["mla_w_ukv"] = jnp.concatenate([dk, dv], axis=1).reshape(N_HEADS_B * (NOPE + V_B), KV_LORA)
    out["w_branch_b"] = g["w_branch_b"].reshape(D_MODEL, N_HEADS_B, HEAD_PAD)[:, :, :V_B].reshape(D_MODEL, WIDTH_B)
    out["w_branch_a"] = _swa_heads(g["w_branch_a"], 1, SWA_HEAD_INVERSE)
    return out


def _pad_lanes(v, width):
    return jnp.pad(v.reshape(1, -1), ((0, 0), (0, width - v.shape[-1])))


def _rope_tables(positions):
    half = ROPE // 2
    inv_freq = ROPE_BASE ** (-jnp.arange(half, dtype=F32) / half)
    ang = positions.astype(F32).reshape(-1, 1) * inv_freq
    cos, sin = jnp.cos(ang), jnp.sin(ang)
    t = cos.shape[0]
    one, zero = jnp.ones((t, NOPE), F32), jnp.zeros((t, NOPE), F32)
    tail = jnp.zeros((t, HEAD_PAD - QK_B), F32)
    z16 = jnp.zeros((t, half), F32)
    c = jnp.concatenate([one, cos, cos, tail], axis=1)
    sm = jnp.concatenate([zero, -sin, z16, tail], axis=1)
    sp = jnp.concatenate([zero, z16, sin, tail], axis=1)
    return c, sm, sp


def _ffn_fwd(x, gain, wg_t, wu_t, wd, tag):
    n, a, b, hmid = _ffn_up(x, gain, wg_t, wu_t, f"{tag}_up")
    y = _mm([(hmid, wd)], "nn", F32, f"{tag}_down", residual=x, alpha=0.5)
    return y, (x, n, a, b, hmid)


def _ffn_bwd(dy, dyb, saved, gain, wg_t, wu_t, wd, tag, grads, names, hook):
    x, n, a, b, hmid = saved
    da, db = _ffn_down_bwd(dyb, wd, a, b, f"{tag}_down_bwd")
    grads[names[0]] = _mm([(da, n)], "tn", BF16, f"{tag}_dwg")
    grads[names[1]] = _mm([(db, n)], "tn", BF16, f"{tag}_dwu")
    hook("gu", grads[names[1]])
    dx, dxb, g_gain = _mm_rms_bwd([(da, wg_t), (db, wu_t)], x, gain, dy, f"{tag}_dn")
    hook("dn", dx)
    grads[names[2]] = _mm([(hmid, dyb)], "tn", BF16, f"{tag}_dwd", alpha=0.5)
    return dx, dxb, g_gain


def _fold_halves(d):
    return d[:, :HEAD_DIM_A] + d[:, HEAD_DIM_A:]


def _local_step(x, positions, target, layers, smalls, at=None):
    at = at or (lambda point, l, token, grads: None)
    _Order.tokens = ()
    n_batch, s_len, d = x.shape
    t = n_batch * s_len
    xt = x.reshape(t, d)
    tabs = _rope_tables(positions)
    pos_col = positions.reshape(t, 1)
    pos_row = positions.reshape(t // BLOCK, 1, BLOCK)
    saved = []
    get_layer = layers if callable(layers) else layers.__getitem__
    for l in range(len(smalls)):
        w, s = get_layer(l), smalls[l]
        g1, gm, g2 = (s[k].reshape(1, d) for k in ("ffn1_norm", "mix_norm", "ffn2_norm"))
        qg2, kg2 = (jnp.tile(s[k].reshape(1, -1), (1, 2)) for k in ("swa_q_norm", "swa_k_norm"))
        sinks = _pad_lanes(s["swa_sinks"], HEAD_PAD)
        gq, gkv = s["mla_q_lora_norm"].reshape(1, -1), s["mla_kv_lora_norm"].reshape(1, -1)
        gqn, gkn = _pad_lanes(s["mla_q_norm"], HEAD_PAD), _pad_lanes(s["mla_k_norm"], HEAD_PAD)
        x1, sv1 = _ffn_fwd(xt, g1, w["ffn1_w_gate"], w["ffn1_w_up"], w["ffn1_w_down"], f"l{l}_ffn1")
        h, proj = _rms_mm(x1, gm, w["w_in"], f"l{l}_proj")
        at("proj", l, proj, None)
        oa = _swa_fwd(proj, pos_col, pos_row, qg2, kg2, sinks, n_batch, s_len, f"l{l}_swa")
        q, k, v = _mla_pre(proj, tabs, gq, gkv, gqn, gkn, w["mla_w_uq"], w["wk"], w["wv"], f"l{l}_mla_pre")
        ob, lse = _mla_flash_fwd(q, k, v, n_batch, s_len, f"l{l}_mla")
        merged, ya, yb = _merge_fwd(oa, ob, proj, w["w_branch_a"], w["w_branch_b"], f"l{l}_merge")
        x2 = _mm([(merged, w["w_out"])], "nn", F32, f"l{l}_out", residual=x1)
        at("out", l, x2, None)
        x3, sv2 = _ffn_fwd(x2, g2, w["ffn2_w_gate"], w["ffn2_w_up"], w["ffn2_w_down"], f"l{l}_ffn2")
        saved.append((w, sv1, sv2, x1, h, proj, oa, q, k, v, ob, lse, merged, ya, yb,
                      (g1, gm, g2, qg2, kg2, sinks, gq, gkv, gqn, gkn)))
        xt = x3

    dy, dyb, loss = _loss_head(xt, target.reshape(t, d))

    big_grads, small_grads = [None] * len(smalls), [None] * len(smalls)
    for l in reversed(range(len(smalls))):
        w, sv1, sv2, x1, h, proj, oa, q, k, v, ob, lse, merged, ya, yb, gains = saved[l]
        g1, gm, g2, qg2, kg2, sinks, gq, gkv, gqn, gkn = gains
        bg, sg = {}, {}
        dy, dyb, sg["ffn2_norm"] = _ffn_bwd(
            dy, dyb, sv2, g2, w["ffn2_w_gate"], w["ffn2_w_up"], w["ffn2_w_down"], f"l{l}_ffn2", bg, FFN2,
            lambda point, token, l=l, bg=bg: at("ffn2_" + point, l, token, bg))
        dya, dyb_, dga, dgb = _merge_bwd(dyb, w["w_out"], proj, ya, yb, f"l{l}_merge_bwd")
        at("mixer", l, sg["ffn2_norm"], bg)
        bg["w_out"] = _mm([(merged, dyb)], "tn", BF16, f"l{l}_dwo")
        doa = _mm([(dya, w["w_branch_a"])], "nn", BF16, f"l{l}_doa")
        bg["w_branch_a"] = _mm([(dya, oa)], "tn", BF16, f"l{l}_dwa")
        dob = _mm([(dyb_, w["w_branch_b"])], "nn", BF16, f"l{l}_dob")
        bg["w_branch_b"] = _mm([(dyb_, ob)], "tn", BF16, f"l{l}_dwb")
        dqa, dkc, dkp, dvc, dvp, dqg, dsk = _swa_bwd(
            proj, pos_col, pos_row, qg2, kg2, sinks, doa, n_batch, s_len, f"l{l}_swa_bwd")
        sg["swa_q_norm"], sg["swa_sinks"] = _fold_halves(dqg), dsk[:, :N_HEADS_A]
        at("mixer_mid", l, dqa, bg)
        dka, dva, dkg = _swa_kv_bwd(proj, kg2, dkc, dkp, dvc, dvp, n_batch, s_len, f"l{l}_swa_kv_bwd")
        sg["swa_k_norm"] = _fold_halves(dkg)
        dq, dk, dv = _mla_flash_bwd(q, k, v, ob, dob, lse, n_batch, s_len, f"l{l}_mla_bwd")
        (dcq, dckv, dkr, g_uq, g_wk, g_wv, sg["mla_q_lora_norm"], sg["mla_kv_lora_norm"], dgqn, dgkn) = _mla_pre_bwd(
            proj, tabs, gq, gkv, gqn, gkn, w["mla_w_uq"], w["wk"], w["wv"], dq, dk, dv, f"l{l}_mla_pre_bwd")
        sg["mla_q_norm"], sg["mla_k_norm"] = dgqn[:, :QK_B], dgkn[:, :QK_B]
        bg["mla_w_uq"], bg["wk"], bg["wv"] = g_uq.astype(BF16), g_wk.astype(BF16), g_wv.astype(BF16)
        dproj = jnp.concatenate([dga, dgb, dqa, dcq, dka, dva, dckv, dkr], axis=1)
        bg["w_in"] = _mm([(dproj, h)], "tn", BF16, f"l{l}_dwin")
        dy, dyb, sg["mix_norm"] = _mm_rms_bwd([(dproj, w["w_in"])], x1, gm, dy, f"l{l}_dh")
        at("ffn1", l, sg["mix_norm"], bg)
        dy, dyb, sg["ffn1_norm"] = _ffn_bwd(
            dy, dyb, sv1, g1, w["ffn1_w_gate"], w["ffn1_w_up"], w["ffn1_w_down"], f"l{l}_ffn1", bg, FFN1,
            lambda point, token, l=l, bg=bg: at("ffn1_" + point, l, token, bg))
        big_grads[l], small_grads[l] = bg, sg
        at("done", l, dy, bg)
    return loss, dy.reshape(n_batch, s_len, d), big_grads, small_grads


def _round_up(n, m):
    return (n + m - 1) // m * m


def _flat_rows(shape, transposed):
    rows, k = (shape[1], shape[0]) if transposed else shape
    return _round_up(rows * k // LANES, 16), rows, k


def _flat_layout(shard_shapes, members, row_tile):
    table, off = [], 0
    for l, name in members:
        tr = dict(BIG)[name]
        pr, rows, k = _flat_rows(shard_shapes[name], tr)
        table.append(((l, name), tr, off, pr, rows, k))
        off += pr
    return table, _round_up(off, row_tile)


def _pack_flat(params, table, total):
    parts, off = [], 0
    for (l, name), tr, o, pr, rows, k in table:
        w = params[name][l]
        w = (w.T if tr else w).reshape(rows * k // LANES, LANES)
        parts.append(jnp.pad(w, ((0, pr - w.shape[0]), (0, 0))))
        off = o + pr
    if total > off:
        parts.append(jnp.zeros((total - off, LANES), parts[0].dtype))
    return jnp.concatenate(parts, axis=0)


def _unpack_flat(flat, table):
    return {key: flat[o:o + rows * k // LANES].reshape(rows, k) for key, tr, o, pr, rows, k in table}


def _gathered_mats(gathered, table, layer):
    return {name: gathered[:, o:o + rows * k // LANES].reshape(N_DEV * rows, k)
            for (l, name), tr, o, pr, rows, k in table if l == layer}


def _pack_grads(grads, table, total):
    parts, off = [], 0
    for key, tr, o, pr, rows, k in table:
        g = grads[key].reshape(N_DEV, rows * k // LANES, LANES)
        parts.append(jnp.pad(g, ((0, 0), (0, pr - g.shape[1]), (0, 0))))
        off = o + pr
    if total > off:
        parts.append(jnp.zeros((N_DEV, total - off, LANES), BF16))
    return jnp.concatenate(parts, axis=1)


def _pack_small(params, last=None):
    parts = [params[n][l].reshape(-1).astype(F32) for l in range(DEPTH) for n in SMALL]
    v = jnp.concatenate(parts)
    v = jnp.pad(v, (0, SMALL_ROWS * LANES - 1 - v.shape[0]))
    last = jnp.zeros((1,), F32) if last is None else last.reshape(1)
    return jnp.concatenate([v, last]).reshape(SMALL_ROWS, LANES)


def _unpack_small(flat, shapes):
    v, out, off = flat.reshape(-1), {}, 0
    for l in range(DEPTH):
        for n in SMALL:
            sz = math.prod(shapes[n][1:])
            out.setdefault(n, []).append(v[off:off + sz].reshape(shapes[n][1:]))
            off += sz
    return {n: jnp.stack(p) for n, p in out.items()}


_MESH = pl.DeviceIdType.MESH


def _place():
    return lax.axis_index("x"), lax.axis_index("y"), lax.axis_index("c")


def _handshake(peers):
    barrier = pltpu.get_barrier_semaphore()
    for peer in peers:
        pl.semaphore_signal(barrier, inc=1, device_id=peer, device_id_type=_MESH)
    pl.semaphore_wait(barrier, len(peers))


def _comm_call(body, out_shape, scratch, name, seq_id, spec=_ANY):
    if seq_id is None:
        return pl.pallas_call(body, name=name, out_shape=out_shape, in_specs=[spec, _ANY], out_specs=spec,
                              scratch_shapes=scratch)
    nbytes = LINK_COST_SCALE * math.prod(out_shape.shape) * out_shape.dtype.itemsize
    return pl.kernel(body, out_type=out_shape, mesh=plsc.ScalarSubcoreMesh(axis_name="sequencer", num_cores=1),
                     scratch_types=scratch, name=name, compiler_params=pltpu.CompilerParams(collective_id=seq_id),
                     cost_estimate=pl.CostEstimate(flops=0, transcendentals=0, bytes_accessed=nbytes))


def _all_gather(x_shard, name, vmem=False, seq_id=None, after=None):
    spec = pl.BlockSpec(memory_space=pltpu.VMEM) if vmem else _ANY

    def body(x_ref, after_ref, out_ref, send_sems, recv_sems, local_sem):
        x, y, c = _place()
        me, sibling = (x, y, c), (x, y, 1 - c)
        chips = [(1 - x, y), (x, 1 - y), (1 - x, 1 - y)]
        if seq_id is not None:
            _handshake([sibling] + [(*chip, c) for chip in chips])

        def rows(px, py, pc):
            return out_ref.at[4 * px + 2 * py + pc]

        def copy(k, block, to, src=None):
            return pltpu.make_async_remote_copy(
                src_ref=rows(*block) if src is None else src, dst_ref=rows(*block),
                send_sem=send_sems.at[k], recv_sem=recv_sems.at[k], device_id=to, device_id_type=_MESH)

        mine = pltpu.make_async_copy(x_ref, rows(*me), local_sem)
        mine.start()
        first = [copy(0, me, sibling, src=x_ref)]
        first += [copy(1 + j, me, (*chip, c), src=x_ref) for j, chip in enumerate(chips)]
        for cp in first:
            cp.start()
        passed = [copy(4 + j, (*chip, c), sibling) for j, chip in enumerate(chips)]
        for j, chip in enumerate(chips):
            copy(1 + j, (*chip, c), me).wait_recv()
            passed[j].start()
        copy(0, sibling, me).wait_recv()
        for j, chip in enumerate(chips):
            copy(4 + j, (*chip, 1 - c), me).wait_recv()
        for cp in first + passed:
            cp.wait_send()
        mine.wait()

    return _comm_call(
        body, jax.ShapeDtypeStruct((N_DEV,) + x_shard.shape, x_shard.dtype),
        [pltpu.SemaphoreType.DMA((7,)), pltpu.SemaphoreType.DMA((7,)), pltpu.SemaphoreType.DMA], name, seq_id,
        spec)(x_shard, x_shard if after is None else after)


def _exchange_cores(g4, name, seq_id=None, after=None):
    n_chip, _, r, w = g4.shape

    def body(g_ref, after_ref, out_ref, send_sems, recv_sems):
        x, y, c = _place()
        if seq_id is not None:
            _handshake([(x, y, 1 - c)])
        copies = [pltpu.make_async_remote_copy(
            src_ref=g_ref.at[q, 1 - c], dst_ref=out_ref.at[q], send_sem=send_sems.at[q], recv_sem=recv_sems.at[q],
            device_id=(x, y, 1 - c), device_id_type=_MESH) for q in range(n_chip)]
        for cp in copies:
            cp.start()
        for cp in copies:
            cp.wait()

    return _comm_call(
        body, jax.ShapeDtypeStruct((n_chip, r, w), g4.dtype),
        [pltpu.SemaphoreType.DMA((n_chip,)), pltpu.SemaphoreType.DMA((n_chip,))], name, seq_id)(g4, g4 if after is None else after)


def _exchange_chips(s1, name, seq_id=None):
    _, r, w = s1.shape

    def body(s_ref, after_ref, out_ref, send_sems, recv_sems):
        x, y, c = _place()
        chips = [(1 - x, y), (x, 1 - y), (1 - x, 1 - y)]
        if seq_id is not None:
            _handshake([(*chip, c) for chip in chips])
        copies = []
        for k, (tx, ty) in enumerate(chips):
            copies.append(pltpu.make_async_remote_copy(
                src_ref=s_ref.at[2 * tx + ty], dst_ref=out_ref.at[k], send_sem=send_sems.at[k],
                recv_sem=recv_sems.at[k], device_id=(tx, ty, c), device_id_type=_MESH))
        for cp in copies:
            cp.start()
        for cp in copies:
            cp.wait()

    return _comm_call(
        body, jax.ShapeDtypeStruct((3, r, w), s1.dtype),
        [pltpu.SemaphoreType.DMA((3,)), pltpu.SemaphoreType.DMA((3,))], name, seq_id)(s1, s1)


def _chip_sum(g4, recv, core, after, name, tr):
    n_chip, _, r, w = g4.shape

    def body(c_ref, a_ref, b_ref, after_ref, o_ref):
        o_ref[...] = (a_ref[...].astype(F32) + b_ref[...].astype(F32)).astype(o_ref.dtype)

    grid_spec = pltpu.PrefetchScalarGridSpec(
        num_scalar_prefetch=1, grid=(n_chip, r // tr),
        in_specs=[pl.BlockSpec((None, None, tr, w), lambda q, i, c: (q, c[0], i, 0)),
                  pl.BlockSpec((None, tr, w), lambda q, i, c: (q, i, 0)), _ANY],
        out_specs=pl.BlockSpec((None, tr, w), lambda q, i, c: (q, i, 0)))
    return pl.pallas_call(
        body, name=name, grid_spec=grid_spec, out_shape=jax.ShapeDtypeStruct((n_chip, r, w), g4.dtype),
        compiler_params=_cp("parallel", "parallel"))(core, g4, recv, after)


def _adam(w, g, m, v):
    m = ADAM_B1 * m + (1.0 - ADAM_B1) * g
    v = ADAM_B2 * v + (1.0 - ADAM_B2) * (g * g)
    m_hat = m / (1.0 - ADAM_B1 ** ADAM_STEP)
    v_hat = v / (1.0 - ADAM_B2 ** ADAM_STEP)
    delta = -ADAM_LR * (m_hat / (jnp.sqrt(v_hat) + ADAM_EPS) + ADAM_WD * w)
    return delta, m, v


def _grad_sum(s1, r2, chip, name, tr):
    _, r, lanes = s1.shape

    def body(c_ref, s_ref, r0_ref, r1_ref, r2_ref, g_out):
        g_out[...] = ((s_ref[...].astype(F32) + r0_ref[...].astype(F32)) + r1_ref[...].astype(F32)) + r2_ref[
            ...].astype(F32)

    row = pl.BlockSpec((tr, lanes), lambda i, c: (i, 0))
    rel = lambda k: pl.BlockSpec((None, tr, lanes), lambda i, c: (k, i, 0))
    grid_spec = pltpu.PrefetchScalarGridSpec(
        num_scalar_prefetch=1, grid=(r // tr,),
        in_specs=[pl.BlockSpec((None, tr, lanes), lambda i, c: (c[0], i, 0)), rel(0), rel(1), rel(2)], out_specs=row)
    return pl.pallas_call(
        body, name=name, grid_spec=grid_spec, out_shape=jax.ShapeDtypeStruct((r, lanes), F32),
        compiler_params=_cp("parallel"))(chip, s1, r2, r2, r2)


def _adam_big(w, g, m, v, name):
    depth, k, n = w.shape
    tk = k if k <= 512 else _tile(k, (256, 128))

    def body(w_ref, g_ref, m_ref, v_ref, d_out, m_out, v_out):
        d, mn, vn = _adam(w_ref[...], g_ref[...], m_ref[...], v_ref[...])
        d_out[...] = d
        m_out[...] = mn
        v_out[...] = vn

    blk = pl.BlockSpec((None, tk, n), lambda l, i: (l, i, 0))
    return pl.pallas_call(
        body, name=name, grid=(depth, k // tk), in_specs=[blk] * 4, out_specs=[blk] * 3,
        out_shape=[jax.ShapeDtypeStruct(w.shape, F32)] * 3, compiler_params=_cp("parallel", "parallel"))(w, g, m, v)


def _adam_small(parts, w, m, v, name):
    rows, lanes = w.shape

    def body(p_ref, w_ref, m_ref, v_ref, g_out, d_out, m_out, v_out):
        g = p_ref[0:rows, :]
        for dev in range(1, N_DEV):
            g = g + p_ref[dev * rows:(dev + 1) * rows, :]
        d, mn, vn = _adam(w_ref[...], g, m_ref[...], v_ref[...])
        g_out[...] = g
        d_out[...] = d
        m_out[...] = mn
        v_out[...] = vn

    return pl.pallas_call(
        body, name=name, out_shape=[jax.ShapeDtypeStruct((rows, lanes), F32)] * 4)(parts, w, m, v)


def kernel(x, positions, ffn1_norm, ffn1_w_gate, ffn1_w_up, ffn1_w_down, mix_norm, w_in, swa_q_norm, swa_k_norm, swa_sinks, mla_q_lora_norm, mla_w_uq, mla_kv_lora_norm, mla_w_ukv, mla_q_norm, mla_k_norm, w_branch_a, w_branch_b, w_out, ffn2_norm, ffn2_w_gate, ffn2_w_up, ffn2_w_down, loss_target, m_ffn1_norm, m_ffn1_w_gate, m_ffn1_w_up, m_ffn1_w_down, m_mix_norm, m_w_in, m_swa_q_norm, m_swa_k_norm, m_swa_sinks, m_mla_q_lora_norm, m_mla_w_uq, m_mla_kv_lora_norm, m_mla_w_ukv, m_mla_q_norm, m_mla_k_norm, m_w_branch_a, m_w_branch_b, m_w_out, m_ffn2_norm, m_ffn2_w_gate, m_ffn2_w_up, m_ffn2_w_down, v_ffn1_norm, v_ffn1_w_gate, v_ffn1_w_up, v_ffn1_w_down, v_mix_norm, v_w_in, v_swa_q_norm, v_swa_k_norm, v_swa_sinks, v_mla_q_lora_norm, v_mla_w_uq, v_mla_kv_lora_norm, v_mla_w_ukv, v_mla_q_norm, v_mla_k_norm, v_w_branch_a, v_w_branch_b, v_w_out, v_ffn2_norm, v_ffn2_w_gate, v_ffn2_w_up, v_ffn2_w_down):
    given = dict(locals())
    params = {n: given[n] for n in WEIGHTS}
    mom1 = {n: given["m_" + n] for n in WEIGHTS}
    mom2 = {n: given["v_" + n] for n in WEIGHTS}
    shard_shapes = {n: params[n].shape[1:] for n, _ in BIG}
    gsegs = [(members, tile) + _flat_layout(shard_shapes, members, tile) for members, tile in GATHER_SEGMENTS]
    rsegs = [(members, tile) + _flat_layout(shard_shapes, members, tile) for members, tile in SCATTER_SEGMENTS]

    def members_of(seg, l):
        names = [n for sl, n in seg[0] if sl == l]
        return [n for n in names if n != "mla_w_ukv"] + (["wk", "wv"] if "mla_w_ukv" in names else [])

    cx, cy, cc = _place()
    core = jnp.reshape(cc, (1,)).astype(jnp.int32)
    chip = jnp.reshape(2 * cx + cy, (1,)).astype(jnp.int32)

    gathered = []
    for s, (_, _, table, total) in enumerate(gsegs):
        w_flat = _pack_flat(params, table, total).astype(BF16)
        gathered.append(_all_gather(w_flat, f"gather_s{s}", seq_id=SEQ_IDS["gather", s] if s else None,
                                    after=gathered[0] if s else None))

    def get_layer(l):
        def build(name):
            for seg, g in zip(gsegs, gathered):
                if name in members_of(seg, l):
                    return _layer_mats(_gathered_mats(g, seg[2], l))
            raise KeyError(name)
        return _LayerWeights(build)

    smalls = [{n: params[n][l] for n in SMALL} for l in range(DEPTH)]

    pending, big_out, layer_grads = {}, [None] * len(rsegs), {}

    def exchange_cores(s):
        _, _, table, total = rsegs[s]
        mine = {}
        for l in range(DEPTH):
            names = members_of(rsegs[s], l)
            if names:
                natural = _unlayer_grads({n: layer_grads[l][n] for n in names})
                mine.update({(l, n): g for n, g in natural.items()})
        g_flat = _pack_grads(mine, table, total)
        g4 = g_flat.reshape(N_DEV // 2, 2, total, LANES)
        pending[s] = (g4, _exchange_cores(g4, f"scatter_cores_s{s}", seq_id=SEQ_IDS["cores", s] if s else None))

    def exchange_chips(s, after):
        g4, from_core = pending.pop(s)
        s1 = _chip_sum(g4, from_core, core, after, f"sum_cores_s{s}", rsegs[s][1])
        _Order.tokens = (s1,)
        pending[s] = (s1, _exchange_chips(s1, f"scatter_chips_s{s}", seq_id=SEQ_IDS["chips", s] if s else None))

    def finish(s):
        s1, from_chips = pending.pop(s)
        big_out[s] = _grad_sum(s1, from_chips, chip, f"grad_sum_s{s}", rsegs[s][1])

    plan = {("ffn1", 1): [("cores", 4)], ("ffn1_gu", 1): [("chips", 4)],
            ("mixer", 0): [("wait", 4), ("cores", 3)], ("mixer_mid", 0): [("chips", 3)],
            ("ffn1", 0): [("wait", 3), ("cores", 2)], ("ffn1_gu", 0): [("cores", 1), ("chips", 2)],
            ("ffn1_dn", 0): [("chips", 1)], ("done", 0): [("cores", 0)]}

    def at(point, l, token, grads):
        if grads is not None:
            layer_grads[l] = grads
        for what, s in plan.get((point, l), ()):
            if what == "cores":
                exchange_cores(s)
            elif what == "chips":
                exchange_chips(s, token)
            else:
                _Order.tokens += (pending[s][1],)

    loss, grad_x, _, small_grads = _local_step(x, positions, loss_target, get_layer, smalls, at)
    exchange_chips(0, grad_x)

    g_small = _pack_small({n: [small_grads[l][n] for l in range(DEPTH)] for n in SMALL}, loss)
    parts = _all_gather(g_small, "gather_small", vmem=True).reshape(N_DEV * SMALL_ROWS, LANES)
    small_out = _adam_small(parts, _pack_small(params), _pack_small(mom1), _pack_small(mom2), "adam_small")
    shapes = {n: params[n].shape for n in SMALL}
    outs = [_unpack_small(small, shapes) for small in small_out]
    loss = small_out[0].reshape(-1)[-1]

    pieces = {}
    for s in reversed(range(len(rsegs))):
        finish(s)
        pieces.update(_unpack_flat(big_out[s], rsegs[s][2]))
    for n, tr in BIG:
        view = (lambda a: jnp.swapaxes(a, 1, 2)) if tr else (lambda a: a)
        g = jnp.stack([pieces[l, n] for l in range(DEPTH)])
        updated = _adam_big(view(params[n]), g, view(mom1[n]), view(mom2[n]), f"adam_{n}")
        for tree, leaf in zip(outs, (g,) + tuple(updated)):
            tree[n] = view(leaf)
    return (loss, grad_x, *[o[n] for o in outs for n in WEIGHTS])
```

```python
import math

import jax
import jax.numpy as jnp
from jax import lax
from jax.experimental import pallas as pl
from jax.experimental.pallas import tpu as pltpu
from jax.experimental.pallas import tpu_sc as plsc

F32 = jnp.float32
BF16 = jnp.bfloat16

N_DEV = 8
DEPTH = 2
D_MODEL = 1024
D_FF = 2816
HEAD_DIM_A = 64
N_HEADS_A = 8
N_KV_HEADS_A = 2
GROUP_A = N_HEADS_A // N_KV_HEADS_A
BLOCK = 128
N_HEADS_B = 8
Q_LORA = 256
KV_LORA = 128
NOPE = 64
ROPE = 32
QK_B = NOPE + ROPE
V_B = 64
HEAD_PAD = 128
WIDTH_A = N_HEADS_A * HEAD_DIM_A
WIDTH_B = N_HEADS_B * V_B
WIDTH_BP = N_HEADS_B * HEAD_PAD
KV_A = N_KV_HEADS_A * HEAD_DIM_A
IN_WIDTH = WIDTH_A + 2 * KV_A + Q_LORA + KV_LORA + ROPE + 2 * D_MODEL
ROPE_BASE = 10000.0
EPS = 1e-6
NEG = -1e30
LOG2E = 1.4426950408889634

P_GA, P_GB, P_QA, P_CQ, P_KA, P_VA, P_CKV, P_KR = 0, 1024, 2048, 2560, 2816, 2944, 3072, 3200
PROJ_W = 3328
SWA_HEAD_ORDER = (0, 4, 1, 5, 2, 6, 3, 7)
SWA_HEAD_INVERSE = tuple(SWA_HEAD_ORDER.index(h) for h in range(N_HEADS_A))

ADAM_LR, ADAM_B1, ADAM_B2, ADAM_EPS, ADAM_WD, ADAM_STEP = 0.001, 0.9, 0.999, 1e-08, 0.01, 10

VMEM_LIMIT = 56 * 1024 * 1024
LANES = 1024

BIG = (("ffn1_w_gate", True), ("ffn1_w_up", True), ("ffn1_w_down", False), ("w_in", True), ("mla_w_uq", True),
       ("mla_w_ukv", True), ("w_branch_a", True), ("w_branch_b", True), ("w_out", False),
       ("ffn2_w_gate", True), ("ffn2_w_up", True), ("ffn2_w_down", False))
SMALL = ("ffn1_norm", "mix_norm", "ffn2_norm", "swa_q_norm", "swa_k_norm", "swa_sinks", "mla_q_lora_norm",
         "mla_kv_lora_norm", "mla_q_norm", "mla_k_norm")
WEIGHTS = ("ffn1_norm", "ffn1_w_gate", "ffn1_w_up", "ffn1_w_down", "mix_norm", "w_in", "swa_q_norm", "swa_k_norm",
           "swa_sinks", "mla_q_lora_norm", "mla_w_uq", "mla_kv_lora_norm", "mla_w_ukv", "mla_q_norm", "mla_k_norm",
           "w_branch_a", "w_branch_b", "w_out", "ffn2_norm", "ffn2_w_gate", "ffn2_w_up", "ffn2_w_down")
SMALL_ROWS = 8
FFN1 = ("ffn1_w_gate", "ffn1_w_up", "ffn1_w_down")
FFN2 = ("ffn2_w_gate", "ffn2_w_up", "ffn2_w_down")
MIXER = ("w_in", "mla_w_uq", "wk", "wv", "w_branch_a", "w_branch_b", "w_out")
GATHER_SEGMENTS = (
    (tuple((0, n) for n in FFN1[:2]), 352),
    (((0, FFN1[2]),), 352),
    (tuple((0, n) for n in MIXER), 400),
    (tuple((0, n) for n in FFN2) + tuple((1, n) for n in FFN1), 352),
    (tuple((1, n) for n in MIXER + FFN2), 464),
)
SCATTER_SEGMENTS = (
    (((0, FFN1[2]),), 352),
    (tuple((0, n) for n in FFN1[:2]), 352),
    (tuple((0, n) for n in MIXER), 400),
    (tuple((0, n) for n in FFN2) + tuple((1, n) for n in FFN1), 352),
    (tuple((1, n) for n in MIXER + FFN2), 464),
)
LINK_COST_SCALE = 64
SEQ_IDS = {(kind, s): 1 + 5 * k + s for k, kind in enumerate(("gather", "cores", "chips")) for s in range(5)}


def _cp(*sem):
    return pltpu.CompilerParams(dimension_semantics=sem, vmem_limit_bytes=VMEM_LIMIT)


def _tile(n, prefs):
    for t in prefs:
        if n % t == 0:
            return t
    return n


def _dot(a, b, dims):
    return lax.dot_general(a, b, (dims, ((), ())), preferred_element_type=F32)


_NT = ((1,), (1,))
_NN = ((1,), (0,))
_TN = ((0,), (0,))


_ANY = pl.BlockSpec(memory_space=pl.ANY)


class _Order:
    tokens = ()


def _tc_call(body, *, in_specs, **kw):
    def run(*args):
        tokens, n = _Order.tokens, len(args)
        if not tokens:
            out = pl.pallas_call(body, in_specs=in_specs, **kw)(*args)
        else:
            def chained(*refs):
                return body(*refs[:n], *refs[n + len(tokens):])
            out = pl.pallas_call(chained, in_specs=list(in_specs) + [_ANY] * len(tokens), **kw)(*args, *tokens)
        _Order.tokens = (jax.tree.leaves(out)[0],)
        return out
    return run


def _sigmoid(x):
    return 0.5 * jnp.tanh(0.5 * x) + 0.5


def _chunks(n, width):
    return [(c, min(width, n - c)) for c in range(0, n, width)]


def _mm(pairs, mode, out_dtype, name, residual=None, alpha=1.0):
    for a, b in pairs:
        for piece in (a if isinstance(a, tuple) else (a,)):
            assert piece.dtype == BF16 and b.dtype == BF16, (name, piece.dtype, b.dtype)
    if mode == "tn":
        (a, b), = pairs
        return _mm_tokens(a, b, out_dtype, name, alpha)
    t = pairs[0][0].shape[0]
    n = pairs[0][1].shape[0] if mode == "nt" else pairs[0][1].shape[1]
    tm = _tile(t, (512, 256, 128))
    dims = _NT if mode == "nt" else _NN
    in_specs, args = [], []
    for a, w in pairs:
        in_specs.append(pl.BlockSpec((tm, a.shape[1]), lambda i: (i, 0)))
        in_specs.append(pl.BlockSpec(w.shape, lambda i: (0, 0)))
        args += [a, w]
    if residual is not None:
        in_specs.append(pl.BlockSpec((tm, n), lambda i: (i, 0)))
        args.append(residual)
    n_pairs = len(pairs)

    def body(*refs):
        o_ref = refs[-1]
        for c0, cw in _chunks(n, 512):
            acc = None
            for p in range(n_pairs):
                w_ref = refs[2 * p + 1]
                w = w_ref[c0:c0 + cw, :] if mode == "nt" else w_ref[:, c0:c0 + cw]
                d = _dot(refs[2 * p][...], w, dims)
                acc = d if acc is None else acc + d
            if alpha != 1.0:
                acc = acc * alpha
            if residual is not None:
                acc = refs[2 * n_pairs][:, c0:c0 + cw] + acc
            o_ref[:, c0:c0 + cw] = acc.astype(out_dtype)

    return _tc_call(
        body, name=name, grid=(t // tm,), in_specs=in_specs, out_specs=pl.BlockSpec((tm, n), lambda i: (i, 0)),
        out_shape=jax.ShapeDtypeStruct((t, n), out_dtype), compiler_params=_cp("parallel"))(*args)


def _mm_tokens(a, b, out_dtype, name, alpha):
    pieces = a if isinstance(a, tuple) else (a,)
    t = b.shape[0]
    widths = [p.shape[1] for p in pieces]
    m, n = sum(widths), b.shape[1]
    tk = _tile(t, (512, 256, 128))
    n_pieces = len(pieces)

    def body(*refs):
        b_ref, o_ref, acc_ref = refs[n_pieces:]
        k = pl.program_id(0)

        @pl.when(k == 0)
        def _():
            acc_ref[...] = jnp.zeros_like(acc_ref)

        off = 0
        for a_ref, width in zip(refs[:n_pieces], widths):
            for c0, cw in _chunks(width, 512):
                acc_ref[off + c0:off + c0 + cw, :] += _dot(a_ref[:, c0:c0 + cw], b_ref[...], _TN)
            off += width

        @pl.when(k == pl.num_programs(0) - 1)
        def _():
            o_ref[...] = (acc_ref[...] * alpha).astype(out_dtype)

    return _tc_call(
        body, name=name, grid=(t // tk,),
        in_specs=[pl.BlockSpec((tk, w), lambda k: (k, 0)) for w in widths] + [pl.BlockSpec((tk, n), lambda k: (k, 0))],
        out_specs=pl.BlockSpec((m, n), lambda k: (0, 0)), out_shape=jax.ShapeDtypeStruct((m, n), out_dtype),
        scratch_shapes=[pltpu.VMEM((m, n), F32)], compiler_params=_cp("arbitrary"))(*pieces, b)


def _rms_mm(x, gain, w, name):
    t, d = x.shape
    n = w.shape[0]
    tm = _tile(t, (512, 256, 128))

    def body(x_ref, g_ref, w_ref, h_ref, o_ref):
        xv = x_ref[...]
        r = lax.rsqrt(jnp.mean(xv * xv, axis=1, keepdims=True) + EPS)
        hv = (xv * r * g_ref[...]).astype(BF16)
        h_ref[...] = hv
        for c0, cw in _chunks(n, 512):
            o_ref[:, c0:c0 + cw] = _dot(hv, w_ref[c0:c0 + cw, :], _NT)

    row = pl.BlockSpec((tm, d), lambda i: (i, 0))
    return _tc_call(
        body, name=name, grid=(t // tm,),
        in_specs=[row, pl.BlockSpec((1, d), lambda i: (0, 0)), pl.BlockSpec(w.shape, lambda i: (0, 0))],
        out_specs=[row, pl.BlockSpec((tm, n), lambda i: (i, 0))],
        out_shape=[jax.ShapeDtypeStruct((t, d), BF16), jax.ShapeDtypeStruct((t, n), F32)],
        compiler_params=_cp("parallel"))(x, gain, w)


def _mm_rms_bwd(pairs, x, gain, res, name):
    t, d = x.shape
    tm = _tile(t, (512, 256, 128))
    acts, weights, entries = [], [], []
    for a, w in pairs:
        k0 = 0
        for piece in (a if isinstance(a, tuple) else (a,)):
            assert piece.dtype == BF16 and w.dtype == BF16, (name, piece.dtype, w.dtype)
            entries.append((len(acts), len(weights), k0, piece.shape[1]))
            acts.append(piece)
            k0 += piece.shape[1]
        assert k0 == w.shape[0], (name, k0, w.shape)
        weights.append(w)
    n_acts = len(acts)

    def body(*refs):
        x_ref, g_ref, res_ref, dx_ref, dxb_ref, dg_ref, dn_ref = refs[n_acts + len(weights):]
        for c0, cw in _chunks(d, 512):
            acc = None
            for ai, wi, k0, kw in entries:
                part = _dot(refs[ai][...], refs[n_acts + wi][k0:k0 + kw, c0:c0 + cw], _NN)
                acc = part if acc is None else acc + part
            dn_ref[:, c0:c0 + cw] = acc
        xv = x_ref[...]
        r = lax.rsqrt(jnp.mean(xv * xv, axis=1, keepdims=True) + EPS)
        xh = xv * r
        dnv = dn_ref[...]
        dxh = dnv * g_ref[...]
        dx = res_ref[...] + r * (dxh - xh * jnp.mean(dxh * xh, axis=1, keepdims=True))
        dx_ref[...] = dx
        dxb_ref[...] = dx.astype(BF16)

        @pl.when(pl.program_id(0) == 0)
        def _():
            dg_ref[...] = jnp.zeros_like(dg_ref)

        dg_ref[...] += jnp.sum(dnv * xh, axis=0, keepdims=True)

    in_specs = [pl.BlockSpec((tm, a.shape[1]), lambda i: (i, 0)) for a in acts]
    in_specs += [pl.BlockSpec(w.shape, lambda i: (0, 0)) for w in weights]
    row = pl.BlockSpec((tm, d), lambda i: (i, 0))
    one = pl.BlockSpec((1, d), lambda i: (0, 0))
    return _tc_call(
        body, name=name, grid=(t // tm,), in_specs=in_specs + [row, one, row], out_specs=[row, row, one],
        out_shape=[jax.ShapeDtypeStruct((t, d), F32), jax.ShapeDtypeStruct((t, d), BF16),
                   jax.ShapeDtypeStruct((1, d), F32)],
        scratch_shapes=[pltpu.VMEM((tm, d), F32)],
        compiler_params=_cp("arbitrary"))(*acts, *weights, x, gain, res)


def _ffn_up(x, gain, wg_t, wu_t, name):
    t, d = x.shape
    f = wg_t.shape[0]
    tm = _tile(t, (512, 256, 128))

    def body(x_ref, g_ref, wg_ref, wu_ref, n_ref, a_ref, b_ref, h_ref):
        xv = x_ref[...]
        r = lax.rsqrt(jnp.mean(xv * xv, axis=1, keepdims=True) + EPS)
        nv = (xv * r * g_ref[...]).astype(BF16)
        n_ref[...] = nv
        for c0, cw in _chunks(f, 256):
            a = _dot(nv, wg_ref[c0:c0 + cw, :], _NT)
            b = _dot(nv, wu_ref[c0:c0 + cw, :], _NT)
            a_ref[:, c0:c0 + cw] = a.astype(BF16)
            b_ref[:, c0:c0 + cw] = b.astype(BF16)
            h_ref[:, c0:c0 + cw] = (a * _sigmoid(a) * b).astype(BF16)

    w_spec = pl.BlockSpec((f, d), lambda i: (0, 0))
    x_spec = pl.BlockSpec((tm, d), lambda i: (i, 0))
    o_spec = pl.BlockSpec((tm, f), lambda i: (i, 0))
    o_shape = jax.ShapeDtypeStruct((t, f), BF16)
    return _tc_call(
        body, name=name, grid=(t // tm,), in_specs=[x_spec, pl.BlockSpec((1, d), lambda i: (0, 0)), w_spec, w_spec],
        out_specs=[x_spec] + [o_spec] * 3, out_shape=[jax.ShapeDtypeStruct((t, d), BF16)] + [o_shape] * 3,
        compiler_params=_cp("parallel"))(x, gain, wg_t, wu_t)


def _ffn_down_bwd(dxb, wd, a, b, name):
    t, d = dxb.shape
    f = wd.shape[0]
    tm = _tile(t, (512, 256, 128))

    def body(dx_ref, wd_ref, a_ref, b_ref, da_ref, db_ref):
        dxv = dx_ref[...]
        for c0, cw in _chunks(f, 256):
            dh = 0.5 * _dot(dxv, wd_ref[c0:c0 + cw, :], _NT)
            av = a_ref[:, c0:c0 + cw].astype(F32)
            bv = b_ref[:, c0:c0 + cw].astype(F32)
            sg = _sigmoid(av)
            da_ref[:, c0:c0 + cw] = (dh * bv * (sg * (1.0 + av * (1.0 - sg)))).astype(BF16)
            db_ref[:, c0:c0 + cw] = (dh * (av * sg)).astype(BF16)

    o_spec = pl.BlockSpec((tm, f), lambda i: (i, 0))
    o_shape = jax.ShapeDtypeStruct((t, f), BF16)
    return _tc_call(
        body, name=name, grid=(t // tm,),
        in_specs=[pl.BlockSpec((tm, d), lambda i: (i, 0)), pl.BlockSpec((f, d), lambda i: (0, 0)), o_spec, o_spec],
        out_specs=[o_spec] * 2, out_shape=[o_shape] * 2, compiler_params=_cp("parallel"))(dxb, wd, a, b)


def _loss_head(y, target):
    t, d = y.shape
    tm = _tile(t, (512, 256, 128))

    def body(y_ref, t_ref, dy_ref, dyb_ref, loss_ref, acc_ref):
        i = pl.program_id(0)
        e = y_ref[...] - t_ref[...]
        dy = e * (1.0 / d)
        dy_ref[...] = dy
        dyb_ref[...] = dy.astype(BF16)

        @pl.when(i == 0)
        def _():
            acc_ref[...] = jnp.zeros_like(acc_ref)

        acc_ref[...] += jnp.sum(e * e, axis=0, keepdims=True)

        @pl.when(i == pl.num_programs(0) - 1)
        def _():
            loss_ref[...] = jnp.sum(acc_ref[...], axis=1, keepdims=True) * (0.5 / d)

    row = pl.BlockSpec((tm, d), lambda i: (i, 0))
    return _tc_call(
        body, name="loss_head", grid=(t // tm,), in_specs=[row, row],
        out_specs=[row, row, pl.BlockSpec((1, 1), lambda i: (0, 0))],
        out_shape=[jax.ShapeDtypeStruct((t, d), F32), jax.ShapeDtypeStruct((t, d), BF16),
                   jax.ShapeDtypeStruct((1, 1), F32)],
        scratch_shapes=[pltpu.VMEM((1, d), F32)], compiler_params=_cp("arbitrary"))(y, target)


def _merge_fwd(oa, ob, proj, wa_t, wb_t, name):
    t = oa.shape[0]
    d = wa_t.shape[0]
    tm = _tile(t, (512, 256, 128))

    def body(oa_ref, ob_ref, ga_ref, gb_ref, wa_ref, wb_ref, mg_ref, ya_ref, yb_ref):
        oav, obv = oa_ref[...], ob_ref[...]
        for c0, cw in _chunks(d, 512):
            cs = slice(c0, c0 + cw)
            ya = _dot(oav, wa_ref[cs, :], _NT)
            yb = _dot(obv, wb_ref[cs, :], _NT)
            mg_ref[:, cs] = (_sigmoid(ga_ref[:, cs]) * ya + _sigmoid(gb_ref[:, cs]) * yb).astype(BF16)
            ya_ref[:, cs] = ya.astype(BF16)
            yb_ref[:, cs] = yb.astype(BF16)

    o_spec = pl.BlockSpec((tm, d), lambda i: (i, 0))
    o_shape = jax.ShapeDtypeStruct((t, d), BF16)
    return _tc_call(
        body, name=name, grid=(t // tm,),
        in_specs=[pl.BlockSpec((tm, oa.shape[1]), lambda i: (i, 0)), pl.BlockSpec((tm, ob.shape[1]), lambda i: (i, 0)),
                  pl.BlockSpec((tm, d), lambda i: (i, P_GA // d)), pl.BlockSpec((tm, d), lambda i: (i, P_GB // d)),
                  pl.BlockSpec(wa_t.shape, lambda i: (0, 0)), pl.BlockSpec(wb_t.shape, lambda i: (0, 0))],
        out_specs=[o_spec] * 3, out_shape=[o_shape] * 3,
        compiler_params=_cp("parallel"))(oa, ob, proj, proj, wa_t, wb_t)


def _merge_bwd(dxb, wo, proj, ya, yb, name):
    t, d = dxb.shape
    tm = _tile(t, (512, 256, 128))

    def body(dx_ref, wo_ref, ga_ref, gb_ref, ya_ref, yb_ref, dya_ref, dyb_ref, dga_ref, dgb_ref):
        dxv = dx_ref[...]
        for c0, cw in _chunks(d, 512):
            cs = slice(c0, c0 + cw)
            dm = _dot(dxv, wo_ref[cs, :], _NT)
            sa = _sigmoid(ga_ref[:, cs])
            sb = _sigmoid(gb_ref[:, cs])
            dya_ref[:, cs] = (dm * sa).astype(BF16)
            dyb_ref[:, cs] = (dm * sb).astype(BF16)
            dga_ref[:, cs] = (dm * ya_ref[:, cs].astype(F32) * (sa * (1.0 - sa))).astype(BF16)
            dgb_ref[:, cs] = (dm * yb_ref[:, cs].astype(F32) * (sb * (1.0 - sb))).astype(BF16)

    o_spec = pl.BlockSpec((tm, d), lambda i: (i, 0))
    o_shape = jax.ShapeDtypeStruct((t, d), BF16)
    return _tc_call(
        body, name=name, grid=(t // tm,),
        in_specs=[o_spec, pl.BlockSpec((d, d), lambda i: (0, 0)),
                  pl.BlockSpec((tm, d), lambda i: (i, P_GA // d)), pl.BlockSpec((tm, d), lambda i: (i, P_GB // d)),
                  o_spec, o_spec],
        out_specs=[o_spec] * 4, out_shape=[o_shape] * 4,
        compiler_params=_cp("parallel"))(dxb, wo, proj, proj, ya, yb)


def _swa_common(has_prev, pq, pk):
    dist = (pq - pk).astype(F32)
    row = lax.broadcasted_iota(jnp.int32, (BLOCK, 2 * BLOCK), 0)
    col = lax.broadcasted_iota(jnp.int32, (BLOCK, 2 * BLOCK), 1)
    diff = row + BLOCK - col
    valid = (diff >= 0) & (diff < BLOCK) & (has_prev | (col >= BLOCK))
    return jnp.concatenate([dist] * N_HEADS_A, axis=0), jnp.concatenate([valid] * N_HEADS_A, axis=0)


def _half_sum(x, lo):
    s_lo = jnp.sum(jnp.where(lo, x, 0.0), axis=1, keepdims=True)
    s_hi = jnp.sum(jnp.where(lo, 0.0, x), axis=1, keepdims=True)
    return jnp.where(lo, s_lo, s_hi)


def _norm2(x, gain2, lo):
    r = lax.rsqrt(_half_sum(x * x, lo) * (1.0 / HEAD_DIM_A) + EPS)
    xh = x * r
    return xh * gain2, xh, r


def _norm2_bwd(d, xh, r, gain2, lo):
    dxh = d * gain2
    return r * (dxh - xh * (_half_sum(dxh * xh, lo) * (1.0 / HEAD_DIM_A)))


def _swa_stack(tiles, lo):
    zero = jnp.zeros_like(tiles[0])
    return jnp.concatenate([jnp.where(lo, t, zero) for t in tiles] + [jnp.where(lo, zero, t) for t in tiles], axis=0)


def _swa_unstack(x8, j, lo):
    return jnp.where(lo, x8[j * BLOCK:(j + 1) * BLOCK], x8[(GROUP_A + j) * BLOCK:(GROUP_A + j + 1) * BLOCK])


def _swa_head_columns(sk_ref):
    slope = jnp.concatenate([jnp.full((BLOCK, 1), 2.0 ** (-(h + 1)), F32) for h in range(N_HEADS_A)], axis=0)
    sink = jnp.concatenate([jnp.broadcast_to(sk_ref[:, h:h + 1], (BLOCK, 1)) for h in range(N_HEADS_A)], axis=0)
    return slope, sink


def _swa_blocks_per_step(s_len):
    nb = s_len // BLOCK
    return 4 if nb % 4 == 0 and nb >= 8 else 2 if nb % 2 == 0 else 1


def _swa_specs(s_len):
    nb, qb = s_len // BLOCK, _swa_blocks_per_step(s_len)
    ns, rows = nb // qb, qb * BLOCK

    def step(b, j):
        return b * ns + j

    def prev(b, j):
        return b * nb + jnp.maximum(qb * j - 1, 0)

    q_spec = pl.BlockSpec((rows, WIDTH_A), lambda b, j: (step(b, j), P_QA // WIDTH_A))
    kc_spec = pl.BlockSpec((rows, KV_A), lambda b, j: (step(b, j), P_KA // KV_A))
    kp_spec = pl.BlockSpec((BLOCK, KV_A), lambda b, j: (prev(b, j), P_KA // KV_A))
    vc_spec = pl.BlockSpec((rows, KV_A), lambda b, j: (step(b, j), P_VA // KV_A))
    vp_spec = pl.BlockSpec((BLOCK, KV_A), lambda b, j: (prev(b, j), P_VA // KV_A))
    pq_spec = pl.BlockSpec((rows, 1), lambda b, j: (step(b, j), 0))
    pkc_spec = pl.BlockSpec((qb, 1, BLOCK), lambda b, j: (step(b, j), 0, 0))
    pkp_spec = pl.BlockSpec((1, 1, BLOCK), lambda b, j: (prev(b, j), 0, 0))
    return qb, ns, step, [q_spec, kc_spec, kp_spec, vc_spec, vp_spec, pq_spec, pkc_spec, pkp_spec]


def _swa_stage_probs(qb, q_ref, kc_ref, kp_ref, vc_ref, vp_ref, pq_ref, pkc_ref, pkp_ref, qg_ref, kg_ref, sk_ref, lo):
    first = pl.program_id(1) * qb
    kk_all = _norm2(jnp.concatenate([kp_ref[...], kc_ref[...]], axis=0), kg_ref[...], lo)[0].astype(BF16)
    vv_all = jnp.concatenate([vp_ref[...], vc_ref[...]], axis=0).astype(BF16)
    pk_all = jnp.concatenate([pkp_ref[0]] + [pkc_ref[s] for s in range(qb)], axis=1)
    rows = [slice(s * BLOCK, (s + 1) * BLOCK) for s in range(qb)]
    keys = [slice(s * BLOCK, (s + 2) * BLOCK) for s in range(qb)]
    masks = [_swa_common(first + s > 0, pq_ref[rows[s], :], pk_all[:, keys[s]]) for s in range(qb)]
    qs = [[_norm2(q_ref[r, j * HEAD_PAD:(j + 1) * HEAD_PAD], qg_ref[...], lo) for j in range(GROUP_A)] for r in rows]
    q8 = [_swa_stack([q[0] for q in tiles], lo).astype(BF16) for tiles in qs]
    kk = [kk_all[ks] for ks in keys]
    vv = [vv_all[ks] for ks in keys]
    slope, sink = _swa_head_columns(sk_ref)
    s = [_dot(q8[b], kk[b], _NT) * (HEAD_DIM_A ** -0.5) - slope * masks[b][0] for b in range(qb)]
    s = [jnp.where(masks[b][1], s[b], NEG) for b in range(qb)]
    m = [jnp.maximum(jnp.max(x, axis=1, keepdims=True), sink) for x in s]
    e = [jnp.exp(x - mx) for x, mx in zip(s, m)]
    es = [jnp.exp(sink - mx) for mx in m]
    inv = [1.0 / (jnp.sum(x, axis=1, keepdims=True) + y) for x, y in zip(e, es)]
    p = [x * i for x, i in zip(e, inv)]
    ps = [y * i for y, i in zip(es, inv)]
    return rows, qs, q8, kk, vv, p, ps


def _swa_fwd(proj, pos_col, pos_row, qg2, kg2, sinks, n_batch, s_len, name):
    t = proj.shape[0]
    qb, ns, step, specs = _swa_specs(s_len)
    small = pl.BlockSpec((1, HEAD_PAD), lambda b, j: (0, 0))

    def body(q_ref, kc_ref, kp_ref, vc_ref, vp_ref, pq_ref, pkc_ref, pkp_ref, qg_ref, kg_ref, sk_ref, o_ref):
        lo = lax.broadcasted_iota(jnp.int32, (1, HEAD_PAD), 1) < HEAD_DIM_A
        rows, _, _, _, vv, p, _ = _swa_stage_probs(qb, q_ref, kc_ref, kp_ref, vc_ref, vp_ref, pq_ref, pkc_ref, pkp_ref,
                                                   qg_ref, kg_ref, sk_ref, lo)
        o8 = [_dot(p[b].astype(BF16), vv[b], _NN) for b in range(qb)]
        for b in range(qb):
            for j in range(GROUP_A):
                o_ref[rows[b], j * HEAD_PAD:(j + 1) * HEAD_PAD] = _swa_unstack(o8[b], j, lo).astype(BF16)

    return _tc_call(
        body, name=name, grid=(n_batch, ns), in_specs=specs + [small, small, small],
        out_specs=pl.BlockSpec((qb * BLOCK, WIDTH_A), lambda b, j: (step(b, j), 0)),
        out_shape=jax.ShapeDtypeStruct((t, WIDTH_A), BF16),
        compiler_params=_cp("parallel", "parallel"))(proj, proj, proj, proj, proj, pos_col, pos_row, pos_row,
                                                     qg2, kg2, sinks)


def _swa_bwd(proj, pos_col, pos_row, qg2, kg2, sinks, do, n_batch, s_len, name):
    t = proj.shape[0]
    qb, ns, step, specs = _swa_specs(s_len)
    small = pl.BlockSpec((1, HEAD_PAD), lambda b, j: (0, 0))
    scale = HEAD_DIM_A ** -0.5

    def body(q_ref, kc_ref, kp_ref, vc_ref, vp_ref, pq_ref, pkc_ref, pkp_ref, qg_ref, kg_ref, sk_ref, do_ref,
             dq_ref, dkc_ref, dkp_ref, dvc_ref, dvp_ref, dqg_ref, dsk_ref):
        @pl.when((pl.program_id(0) == 0) & (pl.program_id(1) == 0))
        def _():
            dqg_ref[...] = jnp.zeros_like(dqg_ref)
            dsk_ref[...] = jnp.zeros_like(dsk_ref)

        lane = lax.broadcasted_iota(jnp.int32, (1, HEAD_PAD), 1)
        lo = lane < HEAD_DIM_A
        rows, qs, q8, kk, vv, p, ps = _swa_stage_probs(qb, q_ref, kc_ref, kp_ref, vc_ref, vp_ref, pq_ref, pkc_ref,
                                                       pkp_ref, qg_ref, kg_ref, sk_ref, lo)
        blocks = range(qb)
        do8 = [_swa_stack([do_ref[r, j * HEAD_PAD:(j + 1) * HEAD_PAD] for j in range(GROUP_A)], lo) for r in rows]
        dp = [_dot(do8[b], vv[b], _NT) for b in blocks]
        delta = [jnp.sum(p[b] * dp[b], axis=1, keepdims=True) for b in blocks]
        ds = [(p[b] * (dp[b] - delta[b]) * scale).astype(BF16) for b in blocks]
        dsink = [ps[b] * delta[b] for b in blocks]
        dvv = [_dot(p[b].astype(BF16), do8[b], _TN) for b in blocks]
        dkk = [_dot(ds[b], q8[b], _TN) for b in blocks]
        dq8 = [_dot(ds[b], kk[b], _NN) for b in blocks]
        dsk = jnp.zeros((1, HEAD_PAD), F32)
        dqg = jnp.zeros((1, HEAD_PAD), F32)
        for b in blocks:
            for h in range(N_HEADS_A):
                dsk = dsk + jnp.where(lane == h, -jnp.sum(dsink[b][h * BLOCK:(h + 1) * BLOCK]), 0.0)
            for j in range(GROUP_A):
                _, xh, r = qs[b][j]
                dqn = _swa_unstack(dq8[b], j, lo)
                dqg = dqg + jnp.sum(dqn * xh, axis=0, keepdims=True)
                dq_ref[rows[b], j * HEAD_PAD:(j + 1) * HEAD_PAD] = _norm2_bwd(dqn, xh, r, qg_ref[...], lo).astype(BF16)
            dkp_ref[rows[b], :] = dkk[b][:BLOCK]
            dkc_ref[rows[b], :] = dkk[b][BLOCK:]
            dvp_ref[rows[b], :] = dvv[b][:BLOCK]
            dvc_ref[rows[b], :] = dvv[b][BLOCK:]
        dqg_ref[...] += dqg
        dsk_ref[...] += dsk

    kv_out = pl.BlockSpec((qb * BLOCK, KV_A), lambda b, j: (step(b, j), 0))
    kv_shape = jax.ShapeDtypeStruct((t, KV_A), F32)
    wide = pl.BlockSpec((qb * BLOCK, WIDTH_A), lambda b, j: (step(b, j), 0))
    return _tc_call(
        body, name=name, grid=(n_batch, ns), in_specs=specs + [small, small, small, wide],
        out_specs=[wide, kv_out, kv_out, kv_out, kv_out, small, small],
        out_shape=[jax.ShapeDtypeStruct((t, WIDTH_A), BF16), kv_shape, kv_shape, kv_shape, kv_shape,
                   jax.ShapeDtypeStruct((1, HEAD_PAD), F32), jax.ShapeDtypeStruct((1, HEAD_PAD), F32)],
        compiler_params=_cp("arbitrary", "arbitrary"))(proj, proj, proj, proj, proj, pos_col, pos_row, pos_row,
                                                       qg2, kg2, sinks, do)


def _swa_kv_bwd(proj, kg2, dkc, dkp, dvc, dvp, n_batch, s_len, name):
    t = proj.shape[0]
    nb = s_len // BLOCK

    def rowblk(b, i):
        return b * nb + i

    def nextblk(b, i):
        return b * nb + jnp.minimum(i + 1, nb - 1)

    def body(k_ref, kg_ref, dkc_ref, dkp_ref, dvc_ref, dvp_ref, dk_ref, dv_ref, dkg_ref):
        b, i = pl.program_id(0), pl.program_id(1)

        @pl.when((b == 0) & (i == 0))
        def _():
            dkg_ref[...] = jnp.zeros_like(dkg_ref)

        lo = lax.broadcasted_iota(jnp.int32, (1, HEAD_PAD), 1) < HEAD_DIM_A
        has_next = (i < nb - 1).astype(F32)
        dkn = dkc_ref[...] + has_next * dkp_ref[...]
        dv_ref[...] = (dvc_ref[...] + has_next * dvp_ref[...]).astype(BF16)
        _, xh, r = _norm2(k_ref[...], kg_ref[...], lo)
        dkg_ref[...] += jnp.sum(dkn * xh, axis=0, keepdims=True)
        dk_ref[...] = _norm2_bwd(dkn, xh, r, kg_ref[...], lo).astype(BF16)

    cur = pl.BlockSpec((BLOCK, KV_A), lambda b, i: (rowblk(b, i), 0))
    nxt = pl.BlockSpec((BLOCK, KV_A), lambda b, i: (nextblk(b, i), 0))
    small = pl.BlockSpec((1, HEAD_PAD), lambda b, i: (0, 0))
    return _tc_call(
        body, name=name, grid=(n_batch, nb),
        in_specs=[pl.BlockSpec((BLOCK, KV_A), lambda b, i: (rowblk(b, i), P_KA // KV_A)), small, cur, nxt, cur, nxt],
        out_specs=[cur, cur, small],
        out_shape=[jax.ShapeDtypeStruct((t, KV_A), BF16), jax.ShapeDtypeStruct((t, KV_A), BF16),
                   jax.ShapeDtypeStruct((1, HEAD_PAD), F32)],
        compiler_params=_cp("arbitrary", "arbitrary"))(proj, kg2, dkc, dkp, dvc, dvp)


def _lora_norm(x, gain):
    r = lax.rsqrt(jnp.mean(x * x, axis=1, keepdims=True) + EPS)
    xh = x * r
    return xh * gain, xh, r


def _mla_in_specs(tm):
    row = lambda w, off: pl.BlockSpec((tm, w), lambda i: (i, off // w))
    one = lambda w: pl.BlockSpec((1, w), lambda i: (0, 0))
    full = lambda r, c: pl.BlockSpec((r, c), lambda i: (0, 0))
    tab = pl.BlockSpec((tm, HEAD_PAD), lambda i: (i, 0))
    return [row(Q_LORA, P_CQ), row(KV_LORA, P_CKV), row(HEAD_PAD, P_KR), tab, tab, tab,
            one(Q_LORA), one(KV_LORA), one(HEAD_PAD), one(HEAD_PAD),
            full(WIDTH_BP, Q_LORA), full(WIDTH_BP, KV_LORA), full(WIDTH_BP, KV_LORA)]


def _mla_pre(proj, tabs, gq, gkv, gqn, gkn, wuq, wk, wv, name):
    t = proj.shape[0]
    tm = _tile(t, (512, 256, 128))

    def body(cq_ref, ckv_ref, kr_ref, c_ref, sm_ref, sp_ref, gq_ref, gkv_ref, gqn_ref, gkn_ref,
             wuq_ref, wk_ref, wv_ref, q_ref, k_ref, v_ref):
        cqn = _lora_norm(cq_ref[...], gq_ref[...])[0].astype(BF16)
        ckvn = _lora_norm(ckv_ref[...], gkv_ref[...])[0].astype(BF16)
        q_raw = _dot(cqn, wuq_ref[...], _NT)
        k_raw = _dot(ckvn, wk_ref[...], _NT)
        v_ref[...] = _dot(ckvn, wv_ref[...], _NT).astype(BF16)
        kr = pltpu.roll(kr_ref[...], NOPE, 1)
        c, sm, sp = c_ref[...], sm_ref[...], sp_ref[...]
        heads = [slice(h * HEAD_PAD, (h + 1) * HEAD_PAD) for h in range(N_HEADS_B)]
        raw = [q_raw[:, hs] for hs in heads] + [k_raw[:, hs] + kr for hs in heads]
        gains = [gqn_ref[...]] * N_HEADS_B + [gkn_ref[...]] * N_HEADS_B
        sq = [jnp.sum(x * x, axis=1, keepdims=True) for x in raw]
        normed = [x * lax.rsqrt(s * (1.0 / QK_B) + EPS) * g for x, s, g in zip(raw, sq, gains)]
        up = [pltpu.roll(x, HEAD_PAD - ROPE // 2, 1) for x in normed]
        down = [pltpu.roll(x, ROPE // 2, 1) for x in normed]
        roped = [(x * c + u * sm + d * sp).astype(BF16) for x, u, d in zip(normed, up, down)]
        for h, hs in enumerate(heads):
            q_ref[:, hs] = roped[h]
            k_ref[:, hs] = roped[N_HEADS_B + h]

    o_spec = pl.BlockSpec((tm, WIDTH_BP), lambda i: (i, 0))
    o_shape = jax.ShapeDtypeStruct((t, WIDTH_BP), BF16)
    return _tc_call(
        body, name=name, grid=(t // tm,), in_specs=_mla_in_specs(tm), out_specs=[o_spec] * 3,
        out_shape=[o_shape] * 3, compiler_params=_cp("parallel"))(
            proj, proj, proj, *tabs, gq, gkv, gqn, gkn, wuq, wk, wv)


def _mla_pre_bwd(proj, tabs, gq, gkv, gqn, gkn, wuq, wk, wv, dq, dk, dv, name):
    t = proj.shape[0]
    tm = _tile(t, (512, 256, 128))

    def body(cq_ref, ckv_ref, kr_ref, c_ref, sm_ref, sp_ref, gq_ref, gkv_ref, gqn_ref, gkn_ref,
             wuq_ref, wk_ref, wv_ref, dq_ref, dk_ref, dv_ref,
             dcq_ref, dckv_ref, dkr_ref, dwuq_ref, dwk_ref, dwv_ref, dgq_ref, dgkv_ref, dgqn_ref, dgkn_ref,
             dqraw_ref, dkraw_ref):
        @pl.when(pl.program_id(0) == 0)
        def _():
            for r in (dwuq_ref, dwk_ref, dwv_ref, dgq_ref, dgkv_ref, dgqn_ref, dgkn_ref):
                r[...] = jnp.zeros_like(r)

        cqn_f, cq_xh, cq_r = _lora_norm(cq_ref[...], gq_ref[...])
        ckvn_f, ckv_xh, ckv_r = _lora_norm(ckv_ref[...], gkv_ref[...])
        cqn, ckvn = cqn_f.astype(BF16), ckvn_f.astype(BF16)
        q_raw = _dot(cqn, wuq_ref[...], _NT)
        k_raw = _dot(ckvn, wk_ref[...], _NT)
        kr = pltpu.roll(kr_ref[...], NOPE, 1)
        c, sm, sp = c_ref[...], sm_ref[...], sp_ref[...]
        heads = [slice(h * HEAD_PAD, (h + 1) * HEAD_PAD) for h in range(N_HEADS_B)]
        raw = [q_raw[:, hs] for hs in heads] + [k_raw[:, hs] + kr for hs in heads]
        d_out = [dq_ref[:, hs] for hs in heads] + [dk_ref[:, hs] for hs in heads]
        gains = [gqn_ref[...]] * N_HEADS_B + [gkn_ref[...]] * N_HEADS_B
        sq = [jnp.sum(x * x, axis=1, keepdims=True) for x in raw]
        rinv = [lax.rsqrt(s * (1.0 / QK_B) + EPS) for s in sq]
        xhat = [x * r for x, r in zip(raw, rinv)]
        down = [pltpu.roll(d * sm, ROPE // 2, 1) for d in d_out]
        up = [pltpu.roll(d * sp, HEAD_PAD - ROPE // 2, 1) for d in d_out]
        dn = [d * c + a + b for d, a, b in zip(d_out, down, up)]
        dgain = [jnp.sum(d * xh, axis=0, keepdims=True) for d, xh in zip(dn, xhat)]
        dxh = [d * g for d, g in zip(dn, gains)]
        inner = [jnp.sum(d * xh, axis=1, keepdims=True) * (1.0 / QK_B) for d, xh in zip(dxh, xhat)]
        d_raw = [r * (d - xh * s) for r, d, xh, s in zip(rinv, dxh, xhat, inner)]
        for h, hs in enumerate(heads):
            dqraw_ref[:, hs] = d_raw[h].astype(BF16)
            dkraw_ref[:, hs] = d_raw[N_HEADS_B + h].astype(BF16)
        dkr = sum(d_raw[N_HEADS_B + 1:], d_raw[N_HEADS_B])
        dgqn_ref[...] += sum(dgain[1:N_HEADS_B], dgain[0])
        dgkn_ref[...] += sum(dgain[N_HEADS_B + 1:], dgain[N_HEADS_B])
        lane = lax.broadcasted_iota(jnp.int32, (tm, HEAD_PAD), 1)
        dkr_ref[...] = jnp.where(lane < ROPE, pltpu.roll(dkr, HEAD_PAD - NOPE, 1), 0.0).astype(BF16)
        dqraw = dqraw_ref[...]
        dkraw = dkraw_ref[...]
        dvb = dv_ref[...].astype(BF16)
        dwuq_ref[...] += _dot(dqraw, cqn, _TN)
        dwk_ref[...] += _dot(dkraw, ckvn, _TN)
        dwv_ref[...] += _dot(dvb, ckvn, _TN)
        dcqn = _dot(dqraw, wuq_ref[...], _NN)
        dckvn = _dot(dkraw, wk_ref[...], _NN) + _dot(dvb, wv_ref[...], _NN)
        dgq_ref[...] += jnp.sum(dcqn * cq_xh, axis=0, keepdims=True)
        dgkv_ref[...] += jnp.sum(dckvn * ckv_xh, axis=0, keepdims=True)
        dxh = dcqn * gq_ref[...]
        dcq_ref[...] = (cq_r * (dxh - cq_xh * jnp.mean(dxh * cq_xh, axis=1, keepdims=True))).astype(BF16)
        dxh = dckvn * gkv_ref[...]
        dckv_ref[...] = (ckv_r * (dxh - ckv_xh * jnp.mean(dxh * ckv_xh, axis=1, keepdims=True))).astype(BF16)

    wide = pl.BlockSpec((tm, WIDTH_BP), lambda i: (i, 0))
    row = lambda w: pl.BlockSpec((tm, w), lambda i: (i, 0))
    full = lambda r, c: pl.BlockSpec((r, c), lambda i: (0, 0))
    return _tc_call(
        body, name=name, grid=(t // tm,), in_specs=_mla_in_specs(tm) + [wide, wide, wide],
        out_specs=[row(Q_LORA), row(KV_LORA), row(HEAD_PAD), full(WIDTH_BP, Q_LORA), full(WIDTH_BP, KV_LORA),
                   full(WIDTH_BP, KV_LORA), full(1, Q_LORA), full(1, KV_LORA), full(1, HEAD_PAD), full(1, HEAD_PAD)],
        out_shape=[jax.ShapeDtypeStruct((t, Q_LORA), BF16), jax.ShapeDtypeStruct((t, KV_LORA), BF16),
                   jax.ShapeDtypeStruct((t, HEAD_PAD), BF16), jax.ShapeDtypeStruct((WIDTH_BP, Q_LORA), F32),
                   jax.ShapeDtypeStruct((WIDTH_BP, KV_LORA), F32), jax.ShapeDtypeStruct((WIDTH_BP, KV_LORA), F32),
                   jax.ShapeDtypeStruct((1, Q_LORA), F32), jax.ShapeDtypeStruct((1, KV_LORA), F32),
                   jax.ShapeDtypeStruct((1, HEAD_PAD), F32), jax.ShapeDtypeStruct((1, HEAD_PAD), F32)],
        scratch_shapes=[pltpu.VMEM((tm, WIDTH_BP), BF16), pltpu.VMEM((tm, WIDTH_BP), BF16)],
        compiler_params=_cp("arbitrary"))(proj, proj, proj, *tabs, gq, gkv, gqn, gkn, wuq, wk, wv, dq, dk, dv)


def _mla_flash_specs(s_len):
    bh_spec = pl.BlockSpec((s_len, HEAD_PAD), lambda b, h: (b, h))
    lse_spec = pl.BlockSpec((1, s_len, 1), lambda b, h: (b * N_HEADS_B + h, 0, 0))
    return bh_spec, lse_spec


def _diag_mask(s):
    row = lax.broadcasted_iota(jnp.int32, s.shape, 0)
    col = lax.broadcasted_iota(jnp.int32, s.shape, 1)
    return jnp.where(row >= col, s, NEG)


def _mla_flash_fwd(q, k, v, n_batch, s_len, name):
    t = q.shape[0]
    tq = _tile(s_len, (256, 128))
    bh_spec, lse_spec = _mla_flash_specs(s_len)
    c = (QK_B ** -0.5) * LOG2E

    def body(q_ref, k_ref, v_ref, o_ref, lse_ref):
        nq = s_len // tq
        rows = [slice(i * tq, (i + 1) * tq) for i in range(nq)]
        below = [slice(0, i * tq) for i in range(nq)]
        qs = [q_ref[r, :] for r in rows]
        sd = [_diag_mask(_dot(qs[i], k_ref[rows[i], :], _NT)) for i in range(nq)]
        sb = [None] + [_dot(qs[i], k_ref[below[i], :], _NT) for i in range(1, nq)]
        m = [jnp.max(s, axis=1, keepdims=True) for s in sd]
        m = [m[0]] + [jnp.maximum(m[i], jnp.max(sb[i], axis=1, keepdims=True)) for i in range(1, nq)]
        pd = [jnp.exp2((sd[i] - m[i]) * c) for i in range(nq)]
        pb = [None] + [jnp.exp2((sb[i] - m[i]) * c) for i in range(1, nq)]
        l = [jnp.sum(p, axis=1, keepdims=True) for p in pd]
        l = [l[0]] + [l[i] + jnp.sum(pb[i], axis=1, keepdims=True) for i in range(1, nq)]
        acc = [_dot(pd[i].astype(BF16), v_ref[rows[i], :], _NN) for i in range(nq)]
        acc = [acc[0]] + [acc[i] + _dot(pb[i].astype(BF16), v_ref[below[i], :], _NN) for i in range(1, nq)]
        for i in range(nq):
            o_ref[rows[i], :] = (acc[i] * (1.0 / l[i])).astype(BF16)
            lse_ref[0, rows[i], :] = m[i] * c + jnp.log2(l[i])

    return _tc_call(
        body, name=name, grid=(n_batch, N_HEADS_B), in_specs=[bh_spec, bh_spec, bh_spec],
        out_specs=[bh_spec, lse_spec],
        out_shape=[jax.ShapeDtypeStruct((t, WIDTH_BP), BF16),
                   jax.ShapeDtypeStruct((n_batch * N_HEADS_B, s_len, 1), F32)],
        compiler_params=_cp("parallel", "parallel"))(q, k, v)


def _mla_flash_bwd(q, k, v, o, do, lse2, n_batch, s_len, name):
    t = q.shape[0]
    tq = _tile(s_len, (256, 128))
    bh_spec, lse_spec = _mla_flash_specs(s_len)
    scale = QK_B ** -0.5
    c = scale * LOG2E

    def body(q_ref, k_ref, v_ref, o_ref, do_ref, lse_ref, dq_ref, dk_ref, dv_ref):
        nq = s_len // tq
        rows = [slice(i * tq, (i + 1) * tq) for i in range(nq)]
        below = [slice(0, i * tq) for i in range(nq)]
        qs = [q_ref[r, :] for r in rows]
        dos = [do_ref[r, :] for r in rows]
        lse = [lse_ref[0, r, :] for r in rows]
        delta = [jnp.sum(dos[i].astype(F32) * o_ref[rows[i], :].astype(F32), axis=1, keepdims=True)
                 for i in range(nq)]

        def probs_and_ds(i, ks, diag):
            s = _dot(qs[i], k_ref[ks, :], _NT)
            if diag:
                s = _diag_mask(s)
            p = jnp.exp2(s * c - lse[i])
            dp = _dot(dos[i], v_ref[ks, :], _NT)
            return p.astype(BF16), (p * (dp - delta[i]) * scale).astype(BF16)

        diag = [probs_and_ds(i, rows[i], True) for i in range(nq)]
        rest = [None] + [probs_and_ds(i, below[i], False) for i in range(1, nq)]
        for i in range(nq):
            dq = _dot(diag[i][1], k_ref[rows[i], :], _NN)
            if i:
                dq = dq + _dot(rest[i][1], k_ref[below[i], :], _NN)
            dq_ref[rows[i], :] = dq
        for j in range(nq):
            later = slice(j * tq, s_len)
            p_j = jnp.concatenate([diag[j][0]] + [rest[i][0][:, rows[j]] for i in range(j + 1, nq)], axis=0)
            ds_j = jnp.concatenate([diag[j][1]] + [rest[i][1][:, rows[j]] for i in range(j + 1, nq)], axis=0)
            dk_ref[rows[j], :] = _dot(ds_j, q_ref[later, :], _TN)
            dv_ref[rows[j], :] = _dot(p_j, do_ref[later, :], _TN)

    f32_wide = jax.ShapeDtypeStruct((t, WIDTH_BP), F32)
    return _tc_call(
        body, name=name, grid=(n_batch, N_HEADS_B),
        in_specs=[bh_spec, bh_spec, bh_spec, bh_spec, bh_spec, lse_spec],
        out_specs=[bh_spec, bh_spec, bh_spec], out_shape=[f32_wide] * 3,
        compiler_params=_cp("parallel", "parallel"))(q, k, v, o, do, lse2)


def _swa_heads(w, axis, order):
    heads = [lax.slice_in_dim(w, h * HEAD_DIM_A, (h + 1) * HEAD_DIM_A, axis=axis) for h in order]
    return jnp.concatenate(heads, axis=axis)


class _LayerWeights:
    def __init__(self, build):
        self._build, self._mats = build, {}

    def __getitem__(self, name):
        if name not in self._mats:
            self._mats.update(self._build(name))
        return self._mats[name]


PIECES_OF = {"mla_w_ukv": ("wk", "wv")}


def _store(name, w):
    t = w.T if dict(BIG)[name] else w
    if name == "mla_w_uq":
        return {name: jnp.pad(t, ((0, HEAD_PAD - QK_B), (0, 0)))}
    if name == "mla_w_ukv":
        pad = ((0, HEAD_PAD - NOPE), (0, 0))
        return {"wk": jnp.pad(t[:NOPE], pad), "wv": jnp.pad(t[NOPE:], pad)}
    if name == "w_branch_b":
        t3 = jnp.pad(t.reshape(t.shape[0], N_HEADS_B, V_B), ((0, 0), (0, 0), (0, HEAD_PAD - V_B)))
        return {name: t3.reshape(t.shape[0], WIDTH_BP)}
    if name == "w_branch_a":
        return {name: _swa_heads(t, 1, SWA_HEAD_ORDER)}
    return {name: t}


def _unstore(name, pieces):
    if name == "mla_w_uq":
        return pieces[name][:QK_B]
    if name == "mla_w_ukv":
        return jnp.concatenate([pieces["wk"][:NOPE], pieces["wv"][:V_B]], axis=0)
    if name == "w_branch_b":
        g = pieces[name]
        return g.reshape(g.shape[0], N_HEADS_B, HEAD_PAD)[:, :, :V_B].reshape(g.shape[0], WIDTH_B)
    if name == "w_branch_a":
        return _swa_heads(pieces[name], 1, SWA_HEAD_INVERSE)
    return pieces[name]


def _layer_mats(w):
    if "w_in" not in w:
        return dict(w)
    w_in = w["w_in"]
    o = [0]
    for n in (WIDTH_A, KV_A, KV_A, Q_LORA, KV_LORA, ROPE, D_MODEL, D_MODEL):
        o.append(o[-1] + n)
    qa, ka, va, cq, ckv, kr, ga, gb = (w_in[o[i]:o[i + 1]] for i in range(8))
    pad = jnp.zeros((PROJ_W - IN_WIDTH, w_in.shape[1]), w_in.dtype)
    out = dict(w)
    out["w_in"] = jnp.concatenate([ga, gb, _swa_heads(qa, 0, SWA_HEAD_ORDER), cq, ka, va, ckv, kr, pad], axis=0)
    return out


def _unlayer_w_in(d):
    ga, gb, qa, cq, ka, va, ckv, kr = (d[a:b] for a, b in (
        (P_GA, P_GA + D_MODEL), (P_GB, P_GB + D_MODEL), (P_QA, P_QA + WIDTH_A), (P_CQ, P_CQ + Q_LORA),
        (P_KA, P_KA + KV_A), (P_VA, P_VA + KV_A), (P_CKV, P_CKV + KV_LORA), (P_KR, P_KR + ROPE)))
    return jnp.concatenate([_swa_heads(qa, 0, SWA_HEAD_INVERSE), ka, va, cq, ckv, kr, ga, gb], axis=0)


def _pad_lanes(v, width):
    return jnp.pad(v.reshape(1, -1), ((0, 0), (0, width - v.shape[-1])))


def _rope_tables(positions):
    half = ROPE // 2
    inv_freq = ROPE_BASE ** (-jnp.arange(half, dtype=F32) / half)
    ang = positions.astype(F32).reshape(-1, 1) * inv_freq
    cos, sin = jnp.cos(ang), jnp.sin(ang)
    t = cos.shape[0]
    one, zero = jnp.ones((t, NOPE), F32), jnp.zeros((t, NOPE), F32)
    tail = jnp.zeros((t, HEAD_PAD - QK_B), F32)
    z16 = jnp.zeros((t, half), F32)
    c = jnp.concatenate([one, cos, cos, tail], axis=1)
    sm = jnp.concatenate([zero, -sin, z16, tail], axis=1)
    sp = jnp.concatenate([zero, z16, sin, tail], axis=1)
    return c, sm, sp


def _ffn_fwd(x, gain, wg_t, wu_t, wd, tag):
    n, a, b, hmid = _ffn_up(x, gain, wg_t, wu_t, f"{tag}_up")
    y = _mm([(hmid, wd)], "nn", F32, f"{tag}_down", residual=x, alpha=0.5)
    return y, (x, n, a, b, hmid)


def _ffn_bwd(dy, dyb, saved, gain, wg_t, wu_t, wd, tag, grads, names, hook):
    x, n, a, b, hmid = saved
    da, db = _ffn_down_bwd(dyb, wd, a, b, f"{tag}_down_bwd")
    grads[names[0]] = _mm([(da, n)], "tn", BF16, f"{tag}_dwg")
    grads[names[1]] = _mm([(db, n)], "tn", BF16, f"{tag}_dwu")
    hook("gu", grads[names[1]])
    dx, dxb, g_gain = _mm_rms_bwd([(da, wg_t), (db, wu_t)], x, gain, dy, f"{tag}_dn")
    hook("dn", dx)
    grads[names[2]] = _mm([(hmid, dyb)], "tn", BF16, f"{tag}_dwd", alpha=0.5)
    return dx, dxb, g_gain


def _fold_halves(d):
    return d[:, :HEAD_DIM_A] + d[:, HEAD_DIM_A:]


def _local_step(x, positions, target, layers, smalls, at=None):
    at = at or (lambda point, l, token, grads: None)
    _Order.tokens = ()
    n_batch, s_len, d = x.shape
    t = n_batch * s_len
    xt = x.reshape(t, d)
    tabs = _rope_tables(positions)
    pos_col = positions.reshape(t, 1)
    pos_row = positions.reshape(t // BLOCK, 1, BLOCK)
    saved = []
    get_layer = layers if callable(layers) else layers.__getitem__
    for l in range(len(smalls)):
        w, s = get_layer(l), smalls[l]
        g1, gm, g2 = (s[k].reshape(1, d) for k in ("ffn1_norm", "mix_norm", "ffn2_norm"))
        qg2, kg2 = (jnp.tile(s[k].reshape(1, -1), (1, 2)) for k in ("swa_q_norm", "swa_k_norm"))
        sinks = _pad_lanes(s["swa_sinks"], HEAD_PAD)
        gq, gkv = s["mla_q_lora_norm"].reshape(1, -1), s["mla_kv_lora_norm"].reshape(1, -1)
        gqn, gkn = _pad_lanes(s["mla_q_norm"], HEAD_PAD), _pad_lanes(s["mla_k_norm"], HEAD_PAD)
        x1, sv1 = _ffn_fwd(xt, g1, w["ffn1_w_gate"], w["ffn1_w_up"], w["ffn1_w_down"], f"l{l}_ffn1")
        h, proj = _rms_mm(x1, gm, w["w_in"], f"l{l}_proj")
        at("proj", l, proj, None)
        oa = _swa_fwd(proj, pos_col, pos_row, qg2, kg2, sinks, n_batch, s_len, f"l{l}_swa")
        q, k, v = _mla_pre(proj, tabs, gq, gkv, gqn, gkn, w["mla_w_uq"], w["wk"], w["wv"], f"l{l}_mla_pre")
        ob, lse = _mla_flash_fwd(q, k, v, n_batch, s_len, f"l{l}_mla")
        merged, ya, yb = _merge_fwd(oa, ob, proj, w["w_branch_a"], w["w_branch_b"], f"l{l}_merge")
        x2 = _mm([(merged, w["w_out"])], "nn", F32, f"l{l}_out", residual=x1)
        at("out", l, x2, None)
        x3, sv2 = _ffn_fwd(x2, g2, w["ffn2_w_gate"], w["ffn2_w_up"], w["ffn2_w_down"], f"l{l}_ffn2")
        saved.append((w, sv1, sv2, x1, h, proj, oa, q, k, v, ob, lse, merged, ya, yb,
                      (g1, gm, g2, qg2, kg2, sinks, gq, gkv, gqn, gkn)))
        xt = x3

    dy, dyb, loss = _loss_head(xt, target.reshape(t, d))

    big_grads, small_grads = [None] * len(smalls), [None] * len(smalls)
    for l in reversed(range(len(smalls))):
        w, sv1, sv2, x1, h, proj, oa, q, k, v, ob, lse, merged, ya, yb, gains = saved[l]
        g1, gm, g2, qg2, kg2, sinks, gq, gkv, gqn, gkn = gains
        bg, sg = {}, {}
        dy, dyb, sg["ffn2_norm"] = _ffn_bwd(
            dy, dyb, sv2, g2, w["ffn2_w_gate"], w["ffn2_w_up"], w["ffn2_w_down"], f"l{l}_ffn2", bg, FFN2,
            lambda point, token, l=l, bg=bg: at("ffn2_" + point, l, token, bg))
        dya, dyb_, dga, dgb = _merge_bwd(dyb, w["w_out"], proj, ya, yb, f"l{l}_merge_bwd")
        at("mixer", l, sg["ffn2_norm"], bg)
        bg["w_out"] = _mm([(merged, dyb)], "tn", BF16, f"l{l}_dwo")
        doa = _mm([(dya, w["w_branch_a"])], "nn", BF16, f"l{l}_doa")
        bg["w_branch_a"] = _mm([(dya, oa)], "tn", BF16, f"l{l}_dwa")
        dob = _mm([(dyb_, w["w_branch_b"])], "nn", BF16, f"l{l}_dob")
        bg["w_branch_b"] = _mm([(dyb_, ob)], "tn", BF16, f"l{l}_dwb")
        dqa, dkc, dkp, dvc, dvp, dqg, dsk = _swa_bwd(
            proj, pos_col, pos_row, qg2, kg2, sinks, doa, n_batch, s_len, f"l{l}_swa_bwd")
        sg["swa_q_norm"], sg["swa_sinks"] = _fold_halves(dqg), dsk[:, :N_HEADS_A]
        at("mixer_mid", l, dqa, bg)
        dka, dva, dkg = _swa_kv_bwd(proj, kg2, dkc, dkp, dvc, dvp, n_batch, s_len, f"l{l}_swa_kv_bwd")
        sg["swa_k_norm"] = _fold_halves(dkg)
        dq, dk, dv = _mla_flash_bwd(q, k, v, ob, dob, lse, n_batch, s_len, f"l{l}_mla_bwd")
        (dcq, dckv, dkr, g_uq, g_wk, g_wv, sg["mla_q_lora_norm"], sg["mla_kv_lora_norm"], dgqn, dgkn) = _mla_pre_bwd(
            proj, tabs, gq, gkv, gqn, gkn, w["mla_w_uq"], w["wk"], w["wv"], dq, dk, dv, f"l{l}_mla_pre_bwd")
        sg["mla_q_norm"], sg["mla_k_norm"] = dgqn[:, :QK_B], dgkn[:, :QK_B]
        bg["mla_w_uq"], bg["wk"], bg["wv"] = g_uq.astype(BF16), g_wk.astype(BF16), g_wv.astype(BF16)
        dproj = (dga, dgb, dqa, dcq, dka, dva, dckv, dkr)
        bg["w_in"] = _mm([(dproj, h)], "tn", BF16, f"l{l}_dwin")
        dy, dyb, sg["mix_norm"] = _mm_rms_bwd([(dproj, w["w_in"])], x1, gm, dy, f"l{l}_dh")
        at("ffn1", l, sg["mix_norm"], bg)
        dy, dyb, sg["ffn1_norm"] = _ffn_bwd(
            dy, dyb, sv1, g1, w["ffn1_w_gate"], w["ffn1_w_up"], w["ffn1_w_down"], f"l{l}_ffn1", bg, FFN1,
            lambda point, token, l=l, bg=bg: at("ffn1_" + point, l, token, bg))
        big_grads[l], small_grads[l] = bg, sg
        at("done", l, dy, bg)
    return loss, dy.reshape(n_batch, s_len, d), big_grads, small_grads


def _round_up(n, m):
    return (n + m - 1) // m * m


def _flat_layout(piece_shapes, members, row_tile):
    table, off = [], 0
    for l, piece in members:
        rows, k = piece_shapes[piece]
        pr = _round_up(rows * k // LANES, 16)
        table.append(((l, piece), off, pr, rows, k))
        off += pr
    return table, _round_up(off, row_tile)


def _pack_flat(stored, table, total):
    parts, off = [], 0
    for key, o, pr, rows, k in table:
        w = stored[key].reshape(rows * k // LANES, LANES)
        parts.append(jnp.pad(w, ((0, pr - w.shape[0]), (0, 0))))
        off = o + pr
    if total > off:
        parts.append(jnp.zeros((total - off, LANES), parts[0].dtype))
    return jnp.concatenate(parts, axis=0)


def _unpack_flat(flat, table):
    return {key: flat[o:o + rows * k // LANES].reshape(rows, k) for key, o, pr, rows, k in table}


def _gathered_mats(gathered, table, layer):
    return {piece: gathered[:, o:o + rows * k // LANES].reshape(N_DEV * rows, k)
            for (l, piece), o, pr, rows, k in table if l == layer}


def _pack_grads(grads, table, total):
    parts, off = [], 0
    for key, o, pr, rows, k in table:
        g = grads[key].reshape(N_DEV, rows * k // LANES, LANES)
        parts.append(jnp.pad(g, ((0, 0), (0, pr - g.shape[1]), (0, 0))))
        off = o + pr
    if total > off:
        parts.append(jnp.zeros((N_DEV, total - off, LANES), BF16))
    return jnp.concatenate(parts, axis=1)


def _pack_small(params, last=None):
    parts = [params[n][l].reshape(-1).astype(F32) for l in range(DEPTH) for n in SMALL]
    v = jnp.concatenate(parts)
    v = jnp.pad(v, (0, SMALL_ROWS * LANES - 1 - v.shape[0]))
    last = jnp.zeros((1,), F32) if last is None else last.reshape(1)
    return jnp.concatenate([v, last]).reshape(SMALL_ROWS, LANES)


def _unpack_small(flat, shapes):
    v, out, off = flat.reshape(-1), {}, 0
    for l in range(DEPTH):
        for n in SMALL:
            sz = math.prod(shapes[n][1:])
            out.setdefault(n, []).append(v[off:off + sz].reshape(shapes[n][1:]))
            off += sz
    return {n: jnp.stack(p) for n, p in out.items()}


_MESH = pl.DeviceIdType.MESH


def _place():
    return lax.axis_index("x"), lax.axis_index("y"), lax.axis_index("c")


def _handshake(peers):
    barrier = pltpu.get_barrier_semaphore()
    for peer in peers:
        pl.semaphore_signal(barrier, inc=1, device_id=peer, device_id_type=_MESH)
    pl.semaphore_wait(barrier, len(peers))


def _comm_call(body, out_shape, scratch, name, seq_id, spec=_ANY):
    if seq_id is None:
        return pl.pallas_call(body, name=name, out_shape=out_shape, in_specs=[spec, _ANY], out_specs=spec,
                              scratch_shapes=scratch)
    nbytes = LINK_COST_SCALE * math.prod(out_shape.shape) * out_shape.dtype.itemsize
    return pl.kernel(body, out_type=out_shape, mesh=plsc.ScalarSubcoreMesh(axis_name="sequencer", num_cores=1),
                     scratch_types=scratch, name=name, compiler_params=pltpu.CompilerParams(collective_id=seq_id),
                     cost_estimate=pl.CostEstimate(flops=0, transcendentals=0, bytes_accessed=nbytes))


def _all_gather(x_shard, name, vmem=False, seq_id=None, after=None):
    spec = pl.BlockSpec(memory_space=pltpu.VMEM) if vmem else _ANY

    def body(x_ref, after_ref, out_ref, send_sems, recv_sems, local_sem):
        x, y, c = _place()
        me, sibling = (x, y, c), (x, y, 1 - c)
        chips = [(1 - x, y), (x, 1 - y), (1 - x, 1 - y)]
        if seq_id is not None:
            _handshake([sibling] + [(*chip, c) for chip in chips])

        def rows(px, py, pc):
            return out_ref.at[4 * px + 2 * py + pc]

        def copy(k, block, to, src=None):
            return pltpu.make_async_remote_copy(
                src_ref=rows(*block) if src is None else src, dst_ref=rows(*block),
                send_sem=send_sems.at[k], recv_sem=recv_sems.at[k], device_id=to, device_id_type=_MESH)

        mine = pltpu.make_async_copy(x_ref, rows(*me), local_sem)
        mine.start()
        first = [copy(0, me, sibling, src=x_ref)]
        first += [copy(1 + j, me, (*chip, c), src=x_ref) for j, chip in enumerate(chips)]
        for cp in first:
            cp.start()
        passed = [copy(4 + j, (*chip, c), sibling) for j, chip in enumerate(chips)]
        for j, chip in enumerate(chips):
            copy(1 + j, (*chip, c), me).wait_recv()
            passed[j].start()
        copy(0, sibling, me).wait_recv()
        for j, chip in enumerate(chips):
            copy(4 + j, (*chip, 1 - c), me).wait_recv()
        for cp in first + passed:
            cp.wait_send()
        mine.wait()

    return _comm_call(
        body, jax.ShapeDtypeStruct((N_DEV,) + x_shard.shape, x_shard.dtype),
        [pltpu.SemaphoreType.DMA((7,)), pltpu.SemaphoreType.DMA((7,)), pltpu.SemaphoreType.DMA], name, seq_id,
        spec)(x_shard, x_shard if after is None else after)


def _exchange_cores(g4, name, seq_id=None, after=None):
    n_chip, _, r, w = g4.shape

    def body(g_ref, after_ref, out_ref, send_sems, recv_sems):
        x, y, c = _place()
        if seq_id is not None:
            _handshake([(x, y, 1 - c)])
        copies = [pltpu.make_async_remote_copy(
            src_ref=g_ref.at[q, 1 - c], dst_ref=out_ref.at[q], send_sem=send_sems.at[q], recv_sem=recv_sems.at[q],
            device_id=(x, y, 1 - c), device_id_type=_MESH) for q in range(n_chip)]
        for cp in copies:
            cp.start()
        for cp in copies:
            cp.wait()

    return _comm_call(
        body, jax.ShapeDtypeStruct((n_chip, r, w), g4.dtype),
        [pltpu.SemaphoreType.DMA((n_chip,)), pltpu.SemaphoreType.DMA((n_chip,))], name, seq_id)(g4, g4 if after is None else after)


def _exchange_chips(s1, name, seq_id=None):
    _, r, w = s1.shape

    def body(s_ref, after_ref, out_ref, send_sems, recv_sems):
        x, y, c = _place()
        chips = [(1 - x, y), (x, 1 - y), (1 - x, 1 - y)]
        if seq_id is not None:
            _handshake([(*chip, c) for chip in chips])
        copies = []
        for k, (tx, ty) in enumerate(chips):
            copies.append(pltpu.make_async_remote_copy(
                src_ref=s_ref.at[2 * tx + ty], dst_ref=out_ref.at[k], send_sem=send_sems.at[k],
                recv_sem=recv_sems.at[k], device_id=(tx, ty, c), device_id_type=_MESH))
        for cp in copies:
            cp.start()
        for cp in copies:
            cp.wait()

    return _comm_call(
        body, jax.ShapeDtypeStruct((3, r, w), s1.dtype),
        [pltpu.SemaphoreType.DMA((3,)), pltpu.SemaphoreType.DMA((3,))], name, seq_id)(s1, s1)


def _chip_sum(g4, recv, core, after, name, tr):
    n_chip, _, r, w = g4.shape

    def body(c_ref, a_ref, b_ref, after_ref, o_ref):
        o_ref[...] = (a_ref[...].astype(F32) + b_ref[...].astype(F32)).astype(o_ref.dtype)

    grid_spec = pltpu.PrefetchScalarGridSpec(
        num_scalar_prefetch=1, grid=(n_chip, r // tr),
        in_specs=[pl.BlockSpec((None, None, tr, w), lambda q, i, c: (q, c[0], i, 0)),
                  pl.BlockSpec((None, tr, w), lambda q, i, c: (q, i, 0)), _ANY],
        out_specs=pl.BlockSpec((None, tr, w), lambda q, i, c: (q, i, 0)))
    return pl.pallas_call(
        body, name=name, grid_spec=grid_spec, out_shape=jax.ShapeDtypeStruct((n_chip, r, w), g4.dtype),
        compiler_params=_cp("parallel", "parallel"))(core, g4, recv, after)


def _adam(w, g, m, v):
    m = ADAM_B1 * m + (1.0 - ADAM_B1) * g
    v = ADAM_B2 * v + (1.0 - ADAM_B2) * (g * g)
    m_hat = m / (1.0 - ADAM_B1 ** ADAM_STEP)
    v_hat = v / (1.0 - ADAM_B2 ** ADAM_STEP)
    delta = -ADAM_LR * (m_hat / (jnp.sqrt(v_hat) + ADAM_EPS) + ADAM_WD * w)
    return delta, m, v


def _grad_sum(s1, r2, chip, name, tr):
    _, r, lanes = s1.shape

    def body(c_ref, s_ref, r0_ref, r1_ref, r2_ref, g_out):
        g_out[...] = ((s_ref[...].astype(F32) + r0_ref[...].astype(F32)) + r1_ref[...].astype(F32)) + r2_ref[
            ...].astype(F32)

    row = pl.BlockSpec((tr, lanes), lambda i, c: (i, 0))
    rel = lambda k: pl.BlockSpec((None, tr, lanes), lambda i, c: (k, i, 0))
    grid_spec = pltpu.PrefetchScalarGridSpec(
        num_scalar_prefetch=1, grid=(r // tr,),
        in_specs=[pl.BlockSpec((None, tr, lanes), lambda i, c: (c[0], i, 0)), rel(0), rel(1), rel(2)], out_specs=row)
    return pl.pallas_call(
        body, name=name, grid_spec=grid_spec, out_shape=jax.ShapeDtypeStruct((r, lanes), F32),
        compiler_params=_cp("parallel"))(chip, s1, r2, r2, r2)


def _adam_big(w, g, m, v, name):
    depth, k, n = w.shape
    tk = k if k <= 512 else _tile(k, (256, 128))

    def body(w_ref, g_ref, m_ref, v_ref, d_out, m_out, v_out):
        d, mn, vn = _adam(w_ref[...], g_ref[...], m_ref[...], v_ref[...])
        d_out[...] = d
        m_out[...] = mn
        v_out[...] = vn

    blk = pl.BlockSpec((None, tk, n), lambda l, i: (l, i, 0))
    return pl.pallas_call(
        body, name=name, grid=(depth, k // tk), in_specs=[blk] * 4, out_specs=[blk] * 3,
        out_shape=[jax.ShapeDtypeStruct(w.shape, F32)] * 3, compiler_params=_cp("parallel", "parallel"))(w, g, m, v)


def _adam_small(parts, w, m, v, name):
    rows, lanes = w.shape

    def body(p_ref, w_ref, m_ref, v_ref, g_out, d_out, m_out, v_out):
        g = p_ref[0:rows, :]
        for dev in range(1, N_DEV):
            g = g + p_ref[dev * rows:(dev + 1) * rows, :]
        d, mn, vn = _adam(w_ref[...], g, m_ref[...], v_ref[...])
        g_out[...] = g
        d_out[...] = d
        m_out[...] = mn
        v_out[...] = vn

    return pl.pallas_call(
        body, name=name, out_shape=[jax.ShapeDtypeStruct((rows, lanes), F32)] * 4)(parts, w, m, v)


def kernel(x, positions, ffn1_norm, ffn1_w_gate, ffn1_w_up, ffn1_w_down, mix_norm, w_in, swa_q_norm, swa_k_norm, swa_sinks, mla_q_lora_norm, mla_w_uq, mla_kv_lora_norm, mla_w_ukv, mla_q_norm, mla_k_norm, w_branch_a, w_branch_b, w_out, ffn2_norm, ffn2_w_gate, ffn2_w_up, ffn2_w_down, loss_target, m_ffn1_norm, m_ffn1_w_gate, m_ffn1_w_up, m_ffn1_w_down, m_mix_norm, m_w_in, m_swa_q_norm, m_swa_k_norm, m_swa_sinks, m_mla_q_lora_norm, m_mla_w_uq, m_mla_kv_lora_norm, m_mla_w_ukv, m_mla_q_norm, m_mla_k_norm, m_w_branch_a, m_w_branch_b, m_w_out, m_ffn2_norm, m_ffn2_w_gate, m_ffn2_w_up, m_ffn2_w_down, v_ffn1_norm, v_ffn1_w_gate, v_ffn1_w_up, v_ffn1_w_down, v_mix_norm, v_w_in, v_swa_q_norm, v_swa_k_norm, v_swa_sinks, v_mla_q_lora_norm, v_mla_w_uq, v_mla_kv_lora_norm, v_mla_w_ukv, v_mla_q_norm, v_mla_k_norm, v_w_branch_a, v_w_branch_b, v_w_out, v_ffn2_norm, v_ffn2_w_gate, v_ffn2_w_up, v_ffn2_w_down):
    given = dict(locals())
    params = {n: given[n] for n in WEIGHTS}
    mom1 = {n: given["m_" + n] for n in WEIGHTS}
    mom2 = {n: given["v_" + n] for n in WEIGHTS}
    assert N_HEADS_B == N_DEV
    stored = {(l, piece): w for l in range(DEPTH) for n, _ in BIG for piece, w in _store(n, params[n][l]).items()}
    piece_shapes = {piece: w.shape for (l, piece), w in stored.items() if l == 0}
    gsegs = [(members, tile) + _flat_layout(piece_shapes, members, tile) for members, tile in GATHER_SEGMENTS]
    rsegs = [(members, tile) + _flat_layout(piece_shapes, members, tile) for members, tile in SCATTER_SEGMENTS]

    def members_of(seg, l):
        return [n for sl, n in seg[0] if sl == l]

    cx, cy, cc = _place()
    core = jnp.reshape(cc, (1,)).astype(jnp.int32)
    chip = jnp.reshape(2 * cx + cy, (1,)).astype(jnp.int32)

    gathered = []
    for s, (_, _, table, total) in enumerate(gsegs):
        w_flat = _pack_flat(stored, table, total).astype(BF16)
        gathered.append(_all_gather(w_flat, f"gather_s{s}", seq_id=SEQ_IDS["gather", s]))

    def get_layer(l):
        def build(name):
            for seg, g in zip(gsegs, gathered):
                if name in members_of(seg, l):
                    return _layer_mats(_gathered_mats(g, seg[2], l))
            raise KeyError(name)
        return _LayerWeights(build)

    smalls = [{n: params[n][l] for n in SMALL} for l in range(DEPTH)]

    pending, big_out, layer_grads = {}, [None] * len(rsegs), {}

    def exchange_cores(s):
        _, _, table, total = rsegs[s]
        mine = {(l, n): _unlayer_w_in(layer_grads[l][n]) if n == "w_in" else layer_grads[l][n] for l, n in rsegs[s][0]}
        g_flat = _pack_grads(mine, table, total)
        g4 = g_flat.reshape(N_DEV // 2, 2, total, LANES)
        pending[s] = (g4, _exchange_cores(g4, f"scatter_cores_s{s}", seq_id=SEQ_IDS["cores", s]))

    def exchange_chips(s, after):
        g4, from_core = pending.pop(s)
        s1 = _chip_sum(g4, from_core, core, after, f"sum_cores_s{s}", rsegs[s][1])
        _Order.tokens = (s1,)
        pending[s] = (s1, _exchange_chips(s1, f"scatter_chips_s{s}", seq_id=SEQ_IDS["chips", s]))

    def finish(s):
        s1, from_chips = pending.pop(s)
        big_out[s] = _grad_sum(s1, from_chips, chip, f"grad_sum_s{s}", rsegs[s][1])

    plan = {("ffn1", 1): [("cores", 4)], ("ffn1_gu", 1): [("chips", 4)],
            ("mixer", 0): [("wait", 4), ("cores", 3)], ("mixer_mid", 0): [("chips", 3)],
            ("ffn1", 0): [("wait", 3), ("cores", 2)], ("ffn1_gu", 0): [("cores", 1), ("chips", 2)],
            ("ffn1_dn", 0): [("chips", 1)], ("done", 0): [("cores", 0)]}

    def at(point, l, token, grads):
        if grads is not None:
            layer_grads[l] = grads
        for what, s in plan.get((point, l), ()):
            if what == "cores":
                exchange_cores(s)
            elif what == "chips":
                exchange_chips(s, token)
            else:
                _Order.tokens += (pending[s][1],)

    loss, grad_x, _, small_grads = _local_step(x, positions, loss_target, get_layer, smalls, at)
    exchange_chips(0, grad_x)

    g_small = _pack_small({n: [small_grads[l][n] for l in range(DEPTH)] for n in SMALL}, loss)
    parts = _all_gather(g_small, "gather_small", vmem=True).reshape(N_DEV * SMALL_ROWS, LANES)
    small_out = _adam_small(parts, _pack_small(params), _pack_small(mom1), _pack_small(mom2), "adam_small")
    shapes = {n: params[n].shape for n in SMALL}
    outs = [_unpack_small(small, shapes) for small in small_out]
    loss = small_out[0].reshape(-1)[-1]

    pieces = {}
    for s in reversed(range(len(rsegs))):
        finish(s)
        pieces.update(_unpack_flat(big_out[s], rsegs[s][2]))
    last = SCATTER_SEGMENTS[0][0][0][1]
    for n, tr in sorted(BIG, key=lambda entry: entry[0] == last):
        view = (lambda a: jnp.swapaxes(a, 1, 2)) if tr else (lambda a: a)
        g = jnp.stack([_unstore(n, {p: pieces[l, p] for p in PIECES_OF.get(n, (n,))}) for l in range(DEPTH)])
        updated = _adam_big(view(params[n]), g, view(mom1[n]), view(mom2[n]), f"adam_{n}")
        for tree, leaf in zip(outs, (g,) + tuple(updated)):
            tree[n] = view(leaf)
    return (loss, grad_x, *[o[n] for o in outs for n in WEIGHTS])
```

```python
import math

import jax
import jax.numpy as jnp
from jax import lax
from jax.experimental import pallas as pl
from jax.experimental.pallas import tpu as pltpu
from jax.experimental.pallas import tpu_sc as plsc

F32 = jnp.float32
BF16 = jnp.bfloat16

N_DEV = 8
DEPTH = 2
D_MODEL = 1024
D_FF = 2816
HEAD_DIM_A = 64
N_HEADS_A = 8
N_KV_HEADS_A = 2
GROUP_A = N_HEADS_A // N_KV_HEADS_A
BLOCK = 128
N_HEADS_B = 8
Q_LORA = 256
KV_LORA = 128
NOPE = 64
ROPE = 32
QK_B = NOPE + ROPE
V_B = 64
HEAD_PAD = 128
WIDTH_A = N_HEADS_A * HEAD_DIM_A
WIDTH_B = N_HEADS_B * V_B
WIDTH_BP = N_HEADS_B * HEAD_PAD
KV_A = N_KV_HEADS_A * HEAD_DIM_A
IN_WIDTH = WIDTH_A + 2 * KV_A + Q_LORA + KV_LORA + ROPE + 2 * D_MODEL
ROPE_BASE = 10000.0
EPS = 1e-6
NEG = -1e30
LOG2E = 1.4426950408889634

P_GA, P_GB, P_QA, P_CQ, P_KA, P_VA, P_CKV, P_KR = 0, 1024, 2048, 2560, 2816, 2944, 3072, 3200
PROJ_W = 3328
SWA_HEAD_ORDER = (0, 4, 1, 5, 2, 6, 3, 7)
SWA_HEAD_INVERSE = tuple(SWA_HEAD_ORDER.index(h) for h in range(N_HEADS_A))

ADAM_LR, ADAM_B1, ADAM_B2, ADAM_EPS, ADAM_WD, ADAM_STEP = 0.001, 0.9, 0.999, 1e-08, 0.01, 10

VMEM_LIMIT = 56 * 1024 * 1024
LANES = 1024

BIG = (("ffn1_w_gate", True), ("ffn1_w_up", True), ("ffn1_w_down", False), ("w_in", True), ("mla_w_uq", True),
       ("mla_w_ukv", True), ("w_branch_a", True), ("w_branch_b", True), ("w_out", False),
       ("ffn2_w_gate", True), ("ffn2_w_up", True), ("ffn2_w_down", False))
SMALL = ("ffn1_norm", "mix_norm", "ffn2_norm", "swa_q_norm", "swa_k_norm", "swa_sinks", "mla_q_lora_norm",
         "mla_kv_lora_norm", "mla_q_norm", "mla_k_norm")
WEIGHTS = ("ffn1_norm", "ffn1_w_gate", "ffn1_w_up", "ffn1_w_down", "mix_norm", "w_in", "swa_q_norm", "swa_k_norm",
           "swa_sinks", "mla_q_lora_norm", "mla_w_uq", "mla_kv_lora_norm", "mla_w_ukv", "mla_q_norm", "mla_k_norm",
           "w_branch_a", "w_branch_b", "w_out", "ffn2_norm", "ffn2_w_gate", "ffn2_w_up", "ffn2_w_down")
SMALL_ROWS = 8
FFN1 = ("ffn1_w_gate", "ffn1_w_up", "ffn1_w_down")
FFN2 = ("ffn2_w_gate", "ffn2_w_up", "ffn2_w_down")
MIXER = ("w_in", "mla_w_uq", "wk", "wv", "w_branch_a", "w_branch_b", "w_out")
GATHER_SEGMENTS = (
    (tuple((0, n) for n in FFN1[:2]), 352),
    (((0, FFN1[2]),), 352),
    (tuple((0, n) for n in MIXER), 400),
    (tuple((0, n) for n in FFN2) + tuple((1, n) for n in FFN1), 352),
    (tuple((1, n) for n in MIXER + FFN2), 464),
)
SCATTER_SEGMENTS = (
    (((0, FFN1[2]),), 352),
    (tuple((0, n) for n in FFN1[:2]), 352),
    (tuple((0, n) for n in MIXER), 400),
    (tuple((0, n) for n in FFN2) + tuple((1, n) for n in FFN1), 352),
    (tuple((1, n) for n in MIXER + FFN2), 464),
)
LINK_COST_SCALE = 64
SEQ_IDS = {(kind, s): 1 + 5 * k + s for k, kind in enumerate(("gather", "cores", "chips")) for s in range(5)}


def _cp(*sem):
    return pltpu.CompilerParams(dimension_semantics=sem, vmem_limit_bytes=VMEM_LIMIT)


def _tile(n, prefs):
    for t in prefs:
        if n % t == 0:
            return t
    return n


def _dot(a, b, dims):
    return lax.dot_general(a, b, (dims, ((), ())), preferred_element_type=F32)


_NT = ((1,), (1,))
_NN = ((1,), (0,))
_TN = ((0,), (0,))


_ANY = pl.BlockSpec(memory_space=pl.ANY)


class _Order:
    tokens = ()


def _tc_call(body, *, in_specs, **kw):
    def run(*args):
        tokens, n = _Order.tokens, len(args)
        if not tokens:
            out = pl.pallas_call(body, in_specs=in_specs, **kw)(*args)
        else:
            def chained(*refs):
                return body(*refs[:n], *refs[n + len(tokens):])
            out = pl.pallas_call(chained, in_specs=list(in_specs) + [_ANY] * len(tokens), **kw)(*args, *tokens)
        _Order.tokens = (jax.tree.leaves(out)[0],)
        return out
    return run


def _sigmoid(x):
    return 0.5 * jnp.tanh(0.5 * x) + 0.5


def _chunks(n, width):
    return [(c, min(width, n - c)) for c in range(0, n, width)]


def _mm(pairs, mode, out_dtype, name, residual=None, alpha=1.0):
    for a, b in pairs:
        for piece in (a if isinstance(a, tuple) else (a,)):
            assert piece.dtype == BF16 and b.dtype == BF16, (name, piece.dtype, b.dtype)
    if mode == "tn":
        (a, b), = pairs
        return _mm_tokens(a, b, out_dtype, name, alpha)
    t = pairs[0][0].shape[0]
    n = pairs[0][1].shape[0] if mode == "nt" else pairs[0][1].shape[1]
    tm = _tile(t, (512, 256, 128))
    dims = _NT if mode == "nt" else _NN
    in_specs, args = [], []
    for a, w in pairs:
        in_specs.append(pl.BlockSpec((tm, a.shape[1]), lambda i: (i, 0)))
        in_specs.append(pl.BlockSpec(w.shape, lambda i: (0, 0)))
        args += [a, w]
    if residual is not None:
        in_specs.append(pl.BlockSpec((tm, n), lambda i: (i, 0)))
        args.append(residual)
    n_pairs = len(pairs)

    def body(*refs):
        o_ref = refs[-1]
        for c0, cw in _chunks(n, 512):
            acc = None
            for p in range(n_pairs):
                w_ref = refs[2 * p + 1]
                w = w_ref[c0:c0 + cw, :] if mode == "nt" else w_ref[:, c0:c0 + cw]
                d = _dot(refs[2 * p][...], w, dims)
                acc = d if acc is None else acc + d
            if alpha != 1.0:
                acc = acc * alpha
            if residual is not None:
                acc = refs[2 * n_pairs][:, c0:c0 + cw] + acc
            o_ref[:, c0:c0 + cw] = acc.astype(out_dtype)

    return _tc_call(
        body, name=name, grid=(t // tm,), in_specs=in_specs, out_specs=pl.BlockSpec((tm, n), lambda i: (i, 0)),
        out_shape=jax.ShapeDtypeStruct((t, n), out_dtype), compiler_params=_cp("parallel"))(*args)


def _mm_tokens(a, b, out_dtype, name, alpha):
    pieces = a if isinstance(a, tuple) else (a,)
    t = b.shape[0]
    widths = [p.shape[1] for p in pieces]
    m, n = sum(widths), b.shape[1]
    tk = _tile(t, (512, 256, 128))
    n_pieces = len(pieces)

    def body(*refs):
        b_ref, o_ref, acc_ref = refs[n_pieces:]
        k = pl.program_id(0)

        @pl.when(k == 0)
        def _():
            acc_ref[...] = jnp.zeros_like(acc_ref)

        off = 0
        for a_ref, width in zip(refs[:n_pieces], widths):
            for c0, cw in _chunks(width, 512):
                acc_ref[off + c0:off + c0 + cw, :] += _dot(a_ref[:, c0:c0 + cw], b_ref[...], _TN)
            off += width

        @pl.when(k == pl.num_programs(0) - 1)
        def _():
            o_ref[...] = (acc_ref[...] * alpha).astype(out_dtype)

    return _tc_call(
        body, name=name, grid=(t // tk,),
        in_specs=[pl.BlockSpec((tk, w), lambda k: (k, 0)) for w in widths] + [pl.BlockSpec((tk, n), lambda k: (k, 0))],
        out_specs=pl.BlockSpec((m, n), lambda k: (0, 0)), out_shape=jax.ShapeDtypeStruct((m, n), out_dtype),
        scratch_shapes=[pltpu.VMEM((m, n), F32)], compiler_params=_cp("arbitrary"))(*pieces, b)


def _rms_mm(x, gain, w, name):
    t, d = x.shape
    n = w.shape[0]
    tm = _tile(t, (512, 256, 128))

    def body(x_ref, g_ref, w_ref, h_ref, o_ref):
        xv = x_ref[...]
        r = lax.rsqrt(jnp.mean(xv * xv, axis=1, keepdims=True) + EPS)
        hv = (xv * r * g_ref[...]).astype(BF16)
        h_ref[...] = hv
        for c0, cw in _chunks(n, 512):
            o_ref[:, c0:c0 + cw] = _dot(hv, w_ref[c0:c0 + cw, :], _NT)

    row = pl.BlockSpec((tm, d), lambda i: (i, 0))
    return _tc_call(
        body, name=name, grid=(t // tm,),
        in_specs=[row, pl.BlockSpec((1, d), lambda i: (0, 0)), pl.BlockSpec(w.shape, lambda i: (0, 0))],
        out_specs=[row, pl.BlockSpec((tm, n), lambda i: (i, 0))],
        out_shape=[jax.ShapeDtypeStruct((t, d), BF16), jax.ShapeDtypeStruct((t, n), F32)],
        compiler_params=_cp("parallel"))(x, gain, w)


def _mm_rms_bwd(pairs, x, gain, res, name):
    t, d = x.shape
    tm = _tile(t, (512, 256, 128))
    acts, weights, entries = [], [], []
    for a, w in pairs:
        k0 = 0
        for piece in (a if isinstance(a, tuple) else (a,)):
            assert piece.dtype == BF16 and w.dtype == BF16, (name, piece.dtype, w.dtype)
            entries.append((len(acts), len(weights), k0, piece.shape[1]))
            acts.append(piece)
            k0 += piece.shape[1]
        assert k0 == w.shape[0], (name, k0, w.shape)
        weights.append(w)
    n_acts = len(acts)

    def body(*refs):
        x_ref, g_ref, res_ref, dx_ref, dxb_ref, dg_ref, dn_ref = refs[n_acts + len(weights):]
        for c0, cw in _chunks(d, 512):
            acc = None
            for ai, wi, k0, kw in entries:
                part = _dot(refs[ai][...], refs[n_acts + wi][k0:k0 + kw, c0:c0 + cw], _NN)
                acc = part if acc is None else acc + part
            dn_ref[:, c0:c0 + cw] = acc
        xv = x_ref[...]
        r = lax.rsqrt(jnp.mean(xv * xv, axis=1, keepdims=True) + EPS)
        xh = xv * r
        dnv = dn_ref[...]
        dxh = dnv * g_ref[...]
        dx = res_ref[...] + r * (dxh - xh * jnp.mean(dxh * xh, axis=1, keepdims=True))
        dx_ref[...] = dx
        dxb_ref[...] = dx.astype(BF16)

        @pl.when(pl.program_id(0) == 0)
        def _():
            dg_ref[...] = jnp.zeros_like(dg_ref)

        dg_ref[...] += jnp.sum(dnv * xh, axis=0, keepdims=True)

    in_specs = [pl.BlockSpec((tm, a.shape[1]), lambda i: (i, 0)) for a in acts]
    in_specs += [pl.BlockSpec(w.shape, lambda i: (0, 0)) for w in weights]
    row = pl.BlockSpec((tm, d), lambda i: (i, 0))
    one = pl.BlockSpec((1, d), lambda i: (0, 0))
    return _tc_call(
        body, name=name, grid=(t // tm,), in_specs=in_specs + [row, one, row], out_specs=[row, row, one],
        out_shape=[jax.ShapeDtypeStruct((t, d), F32), jax.ShapeDtypeStruct((t, d), BF16),
                   jax.ShapeDtypeStruct((1, d), F32)],
        scratch_shapes=[pltpu.VMEM((tm, d), F32)],
        compiler_params=_cp("arbitrary"))(*acts, *weights, x, gain, res)


def _ffn_up(x, gain, wg_t, wu_t, name):
    t, d = x.shape
    f = wg_t.shape[0]
    tm = _tile(t, (512, 256, 128))

    def body(x_ref, g_ref, wg_ref, wu_ref, n_ref, a_ref, b_ref, h_ref):
        xv = x_ref[...]
        r = lax.rsqrt(jnp.mean(xv * xv, axis=1, keepdims=True) + EPS)
        nv = (xv * r * g_ref[...]).astype(BF16)
        n_ref[...] = nv
        for c0, cw in _chunks(f, 256):
            a = _dot(nv, wg_ref[c0:c0 + cw, :], _NT)
            b = _dot(nv, wu_ref[c0:c0 + cw, :], _NT)
            a_ref[:, c0:c0 + cw] = a.astype(BF16)
            b_ref[:, c0:c0 + cw] = b.astype(BF16)
            h_ref[:, c0:c0 + cw] = (a * _sigmoid(a) * b).astype(BF16)

    w_spec = pl.BlockSpec((f, d), lambda i: (0, 0))
    x_spec = pl.BlockSpec((tm, d), lambda i: (i, 0))
    o_spec = pl.BlockSpec((tm, f), lambda i: (i, 0))
    o_shape = jax.ShapeDtypeStruct((t, f), BF16)
    return _tc_call(
        body, name=name, grid=(t // tm,), in_specs=[x_spec, pl.BlockSpec((1, d), lambda i: (0, 0)), w_spec, w_spec],
        out_specs=[x_spec] + [o_spec] * 3, out_shape=[jax.ShapeDtypeStruct((t, d), BF16)] + [o_shape] * 3,
        compiler_params=_cp("parallel"))(x, gain, wg_t, wu_t)


def _ffn_down_bwd(dxb, wd, a, b, name):
    t, d = dxb.shape
    f = wd.shape[0]
    tm = _tile(t, (512, 256, 128))

    def body(dx_ref, wd_ref, a_ref, b_ref, da_ref, db_ref):
        dxv = dx_ref[...]
        for c0, cw in _chunks(f, 256):
            dh = 0.5 * _dot(dxv, wd_ref[c0:c0 + cw, :], _NT)
            av = a_ref[:, c0:c0 + cw].astype(F32)
            bv = b_ref[:, c0:c0 + cw].astype(F32)
            sg = _sigmoid(av)
            da_ref[:, c0:c0 + cw] = (dh * bv * (sg * (1.0 + av * (1.0 - sg)))).astype(BF16)
            db_ref[:, c0:c0 + cw] = (dh * (av * sg)).astype(BF16)

    o_spec = pl.BlockSpec((tm, f), lambda i: (i, 0))
    o_shape = jax.ShapeDtypeStruct((t, f), BF16)
    return _tc_call(
        body, name=name, grid=(t // tm,),
        in_specs=[pl.BlockSpec((tm, d), lambda i: (i, 0)), pl.BlockSpec((f, d), lambda i: (0, 0)), o_spec, o_spec],
        out_specs=[o_spec] * 2, out_shape=[o_shape] * 2, compiler_params=_cp("parallel"))(dxb, wd, a, b)


def _loss_head(y, target):
    t, d = y.shape
    tm = _tile(t, (512, 256, 128))

    def body(y_ref, t_ref, dy_ref, dyb_ref, loss_ref, acc_ref):
        i = pl.program_id(0)
        e = y_ref[...] - t_ref[...]
        dy = e * (1.0 / d)
        dy_ref[...] = dy
        dyb_ref[...] = dy.astype(BF16)

        @pl.when(i == 0)
        def _():
            acc_ref[...] = jnp.zeros_like(acc_ref)

        acc_ref[...] += jnp.sum(e * e, axis=0, keepdims=True)

        @pl.when(i == pl.num_programs(0) - 1)
        def _():
            loss_ref[...] = jnp.sum(acc_ref[...], axis=1, keepdims=True) * (0.5 / d)

    row = pl.BlockSpec((tm, d), lambda i: (i, 0))
    return _tc_call(
        body, name="loss_head", grid=(t // tm,), in_specs=[row, row],
        out_specs=[row, row, pl.BlockSpec((1, 1), lambda i: (0, 0))],
        out_shape=[jax.ShapeDtypeStruct((t, d), F32), jax.ShapeDtypeStruct((t, d), BF16),
                   jax.ShapeDtypeStruct((1, 1), F32)],
        scratch_shapes=[pltpu.VMEM((1, d), F32)], compiler_params=_cp("arbitrary"))(y, target)


def _merge_fwd(oa, ob, proj, wa_t, wb_t, name):
    t = oa.shape[0]
    d = wa_t.shape[0]
    tm = _tile(t, (512, 256, 128))

    def body(oa_ref, ob_ref, ga_ref, gb_ref, wa_ref, wb_ref, mg_ref, ya_ref, yb_ref):
        oav, obv = oa_ref[...], ob_ref[...]
        for c0, cw in _chunks(d, 512):
            cs = slice(c0, c0 + cw)
            ya = _dot(oav, wa_ref[cs, :], _NT)
            yb = _dot(obv, wb_ref[cs, :], _NT)
            mg_ref[:, cs] = (_sigmoid(ga_ref[:, cs]) * ya + _sigmoid(gb_ref[:, cs]) * yb).astype(BF16)
            ya_ref[:, cs] = ya.astype(BF16)
            yb_ref[:, cs] = yb.astype(BF16)

    o_spec = pl.BlockSpec((tm, d), lambda i: (i, 0))
    o_shape = jax.ShapeDtypeStruct((t, d), BF16)
    return _tc_call(
        body, name=name, grid=(t // tm,),
        in_specs=[pl.BlockSpec((tm, oa.shape[1]), lambda i: (i, 0)), pl.BlockSpec((tm, ob.shape[1]), lambda i: (i, 0)),
                  pl.BlockSpec((tm, d), lambda i: (i, P_GA // d)), pl.BlockSpec((tm, d), lambda i: (i, P_GB // d)),
                  pl.BlockSpec(wa_t.shape, lambda i: (0, 0)), pl.BlockSpec(wb_t.shape, lambda i: (0, 0))],
        out_specs=[o_spec] * 3, out_shape=[o_shape] * 3,
        compiler_params=_cp("parallel"))(oa, ob, proj, proj, wa_t, wb_t)


def _merge_bwd(dxb, wo, proj, ya, yb, name):
    t, d = dxb.shape
    tm = _tile(t, (512, 256, 128))

    def body(dx_ref, wo_ref, ga_ref, gb_ref, ya_ref, yb_ref, dya_ref, dyb_ref, dga_ref, dgb_ref):
        dxv = dx_ref[...]
        for c0, cw in _chunks(d, 512):
            cs = slice(c0, c0 + cw)
            dm = _dot(dxv, wo_ref[cs, :], _NT)
            sa = _sigmoid(ga_ref[:, cs])
            sb = _sigmoid(gb_ref[:, cs])
            dya_ref[:, cs] = (dm * sa).astype(BF16)
            dyb_ref[:, cs] = (dm * sb).astype(BF16)
            dga_ref[:, cs] = (dm * ya_ref[:, cs].astype(F32) * (sa * (1.0 - sa))).astype(BF16)
            dgb_ref[:, cs] = (dm * yb_ref[:, cs].astype(F32) * (sb * (1.0 - sb))).astype(BF16)

    o_spec = pl.BlockSpec((tm, d), lambda i: (i, 0))
    o_shape = jax.ShapeDtypeStruct((t, d), BF16)
    return _tc_call(
        body, name=name, grid=(t // tm,),
        in_specs=[o_spec, pl.BlockSpec((d, d), lambda i: (0, 0)),
                  pl.BlockSpec((tm, d), lambda i: (i, P_GA // d)), pl.BlockSpec((tm, d), lambda i: (i, P_GB // d)),
                  o_spec, o_spec],
        out_specs=[o_spec] * 4, out_shape=[o_shape] * 4,
        compiler_params=_cp("parallel"))(dxb, wo, proj, proj, ya, yb)


def _swa_common(has_prev, pq, pk):
    dist = (pq - pk).astype(F32)
    row = lax.broadcasted_iota(jnp.int32, (BLOCK, 2 * BLOCK), 0)
    col = lax.broadcasted_iota(jnp.int32, (BLOCK, 2 * BLOCK), 1)
    diff = row + BLOCK - col
    valid = (diff >= 0) & (diff < BLOCK) & (has_prev | (col >= BLOCK))
    return jnp.concatenate([dist] * N_HEADS_A, axis=0), jnp.concatenate([valid] * N_HEADS_A, axis=0)


def _half_sum(x, lo):
    s_lo = jnp.sum(jnp.where(lo, x, 0.0), axis=1, keepdims=True)
    s_hi = jnp.sum(jnp.where(lo, 0.0, x), axis=1, keepdims=True)
    return jnp.where(lo, s_lo, s_hi)


def _norm2(x, gain2, lo):
    r = lax.rsqrt(_half_sum(x * x, lo) * (1.0 / HEAD_DIM_A) + EPS)
    xh = x * r
    return xh * gain2, xh, r


def _norm2_bwd(d, xh, r, gain2, lo):
    dxh = d * gain2
    return r * (dxh - xh * (_half_sum(dxh * xh, lo) * (1.0 / HEAD_DIM_A)))


def _swa_stack(tiles, lo):
    zero = jnp.zeros_like(tiles[0])
    return jnp.concatenate([jnp.where(lo, t, zero) for t in tiles] + [jnp.where(lo, zero, t) for t in tiles], axis=0)


def _swa_unstack(x8, j, lo):
    return jnp.where(lo, x8[j * BLOCK:(j + 1) * BLOCK], x8[(GROUP_A + j) * BLOCK:(GROUP_A + j + 1) * BLOCK])


def _swa_head_columns(sk_ref):
    slope = jnp.concatenate([jnp.full((BLOCK, 1), 2.0 ** (-(h + 1)), F32) for h in range(N_HEADS_A)], axis=0)
    sink = jnp.concatenate([jnp.broadcast_to(sk_ref[:, h:h + 1], (BLOCK, 1)) for h in range(N_HEADS_A)], axis=0)
    return slope, sink


def _swa_blocks_per_step(s_len):
    nb = s_len // BLOCK
    return 4 if nb % 4 == 0 and nb >= 8 else 2 if nb % 2 == 0 else 1


def _swa_specs(s_len):
    nb, qb = s_len // BLOCK, _swa_blocks_per_step(s_len)
    ns, rows = nb // qb, qb * BLOCK

    def step(b, j):
        return b * ns + j

    def prev(b, j):
        return b * nb + jnp.maximum(qb * j - 1, 0)

    q_spec = pl.BlockSpec((rows, WIDTH_A), lambda b, j: (step(b, j), P_QA // WIDTH_A))
    kc_spec = pl.BlockSpec((rows, KV_A), lambda b, j: (step(b, j), P_KA // KV_A))
    kp_spec = pl.BlockSpec((BLOCK, KV_A), lambda b, j: (prev(b, j), P_KA // KV_A))
    vc_spec = pl.BlockSpec((rows, KV_A), lambda b, j: (step(b, j), P_VA // KV_A))
    vp_spec = pl.BlockSpec((BLOCK, KV_A), lambda b, j: (prev(b, j), P_VA // KV_A))
    pq_spec = pl.BlockSpec((rows, 1), lambda b, j: (step(b, j), 0))
    pkc_spec = pl.BlockSpec((qb, 1, BLOCK), lambda b, j: (step(b, j), 0, 0))
    pkp_spec = pl.BlockSpec((1, 1, BLOCK), lambda b, j: (prev(b, j), 0, 0))
    return qb, ns, step, [q_spec, kc_spec, kp_spec, vc_spec, vp_spec, pq_spec, pkc_spec, pkp_spec]


def _swa_stage_probs(qb, q_ref, kc_ref, kp_ref, vc_ref, vp_ref, pq_ref, pkc_ref, pkp_ref, qg_ref, kg_ref, sk_ref, lo):
    first = pl.program_id(1) * qb
    kk_all = _norm2(jnp.concatenate([kp_ref[...], kc_ref[...]], axis=0), kg_ref[...], lo)[0].astype(BF16)
    vv_all = jnp.concatenate([vp_ref[...], vc_ref[...]], axis=0).astype(BF16)
    pk_all = jnp.concatenate([pkp_ref[0]] + [pkc_ref[s] for s in range(qb)], axis=1)
    rows = [slice(s * BLOCK, (s + 1) * BLOCK) for s in range(qb)]
    keys = [slice(s * BLOCK, (s + 2) * BLOCK) for s in range(qb)]
    masks = [_swa_common(first + s > 0, pq_ref[rows[s], :], pk_all[:, keys[s]]) for s in range(qb)]
    qs = [[_norm2(q_ref[r, j * HEAD_PAD:(j + 1) * HEAD_PAD], qg_ref[...], lo) for j in range(GROUP_A)] for r in rows]
    q8 = [_swa_stack([q[0] for q in tiles], lo).astype(BF16) for tiles in qs]
    kk = [kk_all[ks] for ks in keys]
    vv = [vv_all[ks] for ks in keys]
    slope, sink = _swa_head_columns(sk_ref)
    s = [_dot(q8[b], kk[b], _NT) * (HEAD_DIM_A ** -0.5) - slope * masks[b][0] for b in range(qb)]
    s = [jnp.where(masks[b][1], s[b], NEG) for b in range(qb)]
    m = [jnp.maximum(jnp.max(x, axis=1, keepdims=True), sink) for x in s]
    e = [jnp.exp(x - mx) for x, mx in zip(s, m)]
    es = [jnp.exp(sink - mx) for mx in m]
    inv = [1.0 / (jnp.sum(x, axis=1, keepdims=True) + y) for x, y in zip(e, es)]
    p = [x * i for x, i in zip(e, inv)]
    ps = [y * i for y, i in zip(es, inv)]
    return rows, qs, q8, kk, vv, p, ps


def _swa_fwd(proj, pos_col, pos_row, qg2, kg2, sinks, n_batch, s_len, name):
    t = proj.shape[0]
    qb, ns, step, specs = _swa_specs(s_len)
    small = pl.BlockSpec((1, HEAD_PAD), lambda b, j: (0, 0))

    def body(q_ref, kc_ref, kp_ref, vc_ref, vp_ref, pq_ref, pkc_ref, pkp_ref, qg_ref, kg_ref, sk_ref, o_ref):
        lo = lax.broadcasted_iota(jnp.int32, (1, HEAD_PAD), 1) < HEAD_DIM_A
        rows, _, _, _, vv, p, _ = _swa_stage_probs(qb, q_ref, kc_ref, kp_ref, vc_ref, vp_ref, pq_ref, pkc_ref, pkp_ref,
                                                   qg_ref, kg_ref, sk_ref, lo)
        o8 = [_dot(p[b].astype(BF16), vv[b], _NN) for b in range(qb)]
        for b in range(qb):
            for j in range(GROUP_A):
                o_ref[rows[b], j * HEAD_PAD:(j + 1) * HEAD_PAD] = _swa_unstack(o8[b], j, lo).astype(BF16)

    return _tc_call(
        body, name=name, grid=(n_batch, ns), in_specs=specs + [small, small, small],
        out_specs=pl.BlockSpec((qb * BLOCK, WIDTH_A), lambda b, j: (step(b, j), 0)),
        out_shape=jax.ShapeDtypeStruct((t, WIDTH_A), BF16),
        compiler_params=_cp("parallel", "parallel"))(proj, proj, proj, proj, proj, pos_col, pos_row, pos_row,
                                                     qg2, kg2, sinks)


def _swa_bwd(proj, pos_col, pos_row, qg2, kg2, sinks, do, n_batch, s_len, name):
    t = proj.shape[0]
    qb, ns, step, specs = _swa_specs(s_len)
    small = pl.BlockSpec((1, HEAD_PAD), lambda b, j: (0, 0))
    scale = HEAD_DIM_A ** -0.5

    def body(q_ref, kc_ref, kp_ref, vc_ref, vp_ref, pq_ref, pkc_ref, pkp_ref, qg_ref, kg_ref, sk_ref, do_ref,
             dq_ref, dkc_ref, dkp_ref, dvc_ref, dvp_ref, dqg_ref, dsk_ref):
        @pl.when((pl.program_id(0) == 0) & (pl.program_id(1) == 0))
        def _():
            dqg_ref[...] = jnp.zeros_like(dqg_ref)
            dsk_ref[...] = jnp.zeros_like(dsk_ref)

        lane = lax.broadcasted_iota(jnp.int32, (1, HEAD_PAD), 1)
        lo = lane < HEAD_DIM_A
        rows, qs, q8, kk, vv, p, ps = _swa_stage_probs(qb, q_ref, kc_ref, kp_ref, vc_ref, vp_ref, pq_ref, pkc_ref,
                                                       pkp_ref, qg_ref, kg_ref, sk_ref, lo)
        blocks = range(qb)
        do8 = [_swa_stack([do_ref[r, j * HEAD_PAD:(j + 1) * HEAD_PAD] for j in range(GROUP_A)], lo) for r in rows]
        dp = [_dot(do8[b], vv[b], _NT) for b in blocks]
        delta = [jnp.sum(p[b] * dp[b], axis=1, keepdims=True) for b in blocks]
        ds = [(p[b] * (dp[b] - delta[b]) * scale).astype(BF16) for b in blocks]
        dsink = [ps[b] * delta[b] for b in blocks]
        dvv = [_dot(p[b].astype(BF16), do8[b], _TN) for b in blocks]
        dkk = [_dot(ds[b], q8[b], _TN) for b in blocks]
        dq8 = [_dot(ds[b], kk[b], _NN) for b in blocks]
        dsk = jnp.zeros((1, HEAD_PAD), F32)
        dqg = jnp.zeros((1, HEAD_PAD), F32)
        for b in blocks:
            for h in range(N_HEADS_A):
                dsk = dsk + jnp.where(lane == h, -jnp.sum(dsink[b][h * BLOCK:(h + 1) * BLOCK]), 0.0)
            for j in range(GROUP_A):
                _, xh, r = qs[b][j]
                dqn = _swa_unstack(dq8[b], j, lo)
                dqg = dqg + jnp.sum(dqn * xh, axis=0, keepdims=True)
                dq_ref[rows[b], j * HEAD_PAD:(j + 1) * HEAD_PAD] = _norm2_bwd(dqn, xh, r, qg_ref[...], lo).astype(BF16)
            dkp_ref[rows[b], :] = dkk[b][:BLOCK]
            dkc_ref[rows[b], :] = dkk[b][BLOCK:]
            dvp_ref[rows[b], :] = dvv[b][:BLOCK]
            dvc_ref[rows[b], :] = dvv[b][BLOCK:]
        dqg_ref[...] += dqg
        dsk_ref[...] += dsk

    kv_out = pl.BlockSpec((qb * BLOCK, KV_A), lambda b, j: (step(b, j), 0))
    kv_shape = jax.ShapeDtypeStruct((t, KV_A), F32)
    wide = pl.BlockSpec((qb * BLOCK, WIDTH_A), lambda b, j: (step(b, j), 0))
    return _tc_call(
        body, name=name, grid=(n_batch, ns), in_specs=specs + [small, small, small, wide],
        out_specs=[wide, kv_out, kv_out, kv_out, kv_out, small, small],
        out_shape=[jax.ShapeDtypeStruct((t, WIDTH_A), BF16), kv_shape, kv_shape, kv_shape, kv_shape,
                   jax.ShapeDtypeStruct((1, HEAD_PAD), F32), jax.ShapeDtypeStruct((1, HEAD_PAD), F32)],
        compiler_params=_cp("arbitrary", "arbitrary"))(proj, proj, proj, proj, proj, pos_col, pos_row, pos_row,
                                                       qg2, kg2, sinks, do)


def _swa_kv_bwd(proj, kg2, dkc, dkp, dvc, dvp, n_batch, s_len, name):
    t = proj.shape[0]
    nb, qb = s_len // BLOCK, _swa_blocks_per_step(s_len)
    ns, rows = nb // qb, qb * BLOCK

    def step(b, j):
        return b * ns + j

    def edge(b, j):
        return b * nb + jnp.minimum(qb * (j + 1), nb - 1)

    def body(k_ref, kg_ref, dkc_ref, dkp_ref, dkp_edge, dvc_ref, dvp_ref, dvp_edge, dk_ref, dv_ref, dkg_ref):
        @pl.when((pl.program_id(0) == 0) & (pl.program_id(1) == 0))
        def _():
            dkg_ref[...] = jnp.zeros_like(dkg_ref)

        lo = lax.broadcasted_iota(jnp.int32, (1, HEAD_PAD), 1) < HEAD_DIM_A
        has_next = (pl.program_id(1) < ns - 1).astype(F32)

        def from_next(p_ref, edge_ref):
            return jnp.concatenate([p_ref[BLOCK:, :], has_next * edge_ref[...]], axis=0)

        dkn = dkc_ref[...] + from_next(dkp_ref, dkp_edge)
        dv_ref[...] = (dvc_ref[...] + from_next(dvp_ref, dvp_edge)).astype(BF16)
        _, xh, r = _norm2(k_ref[...], kg_ref[...], lo)
        dkg_ref[...] += jnp.sum(dkn * xh, axis=0, keepdims=True)
        dk_ref[...] = _norm2_bwd(dkn, xh, r, kg_ref[...], lo).astype(BF16)

    cur = pl.BlockSpec((rows, KV_A), lambda b, j: (step(b, j), 0))
    nxt = pl.BlockSpec((BLOCK, KV_A), lambda b, j: (edge(b, j), 0))
    small = pl.BlockSpec((1, HEAD_PAD), lambda b, j: (0, 0))
    return _tc_call(
        body, name=name, grid=(n_batch, ns),
        in_specs=[pl.BlockSpec((rows, KV_A), lambda b, j: (step(b, j), P_KA // KV_A)), small, cur, cur, nxt, cur, cur,
                  nxt],
        out_specs=[cur, cur, small],
        out_shape=[jax.ShapeDtypeStruct((t, KV_A), BF16), jax.ShapeDtypeStruct((t, KV_A), BF16),
                   jax.ShapeDtypeStruct((1, HEAD_PAD), F32)],
        compiler_params=_cp("arbitrary", "arbitrary"))(proj, kg2, dkc, dkp, dkp, dvc, dvp, dvp)


def _lora_norm(x, gain):
    r = lax.rsqrt(jnp.mean(x * x, axis=1, keepdims=True) + EPS)
    xh = x * r
    return xh * gain, xh, r


def _mla_in_specs(tm):
    row = lambda w, off: pl.BlockSpec((tm, w), lambda i: (i, off // w))
    one = lambda w: pl.BlockSpec((1, w), lambda i: (0, 0))
    full = lambda r, c: pl.BlockSpec((r, c), lambda i: (0, 0))
    tab = pl.BlockSpec((tm, HEAD_PAD), lambda i: (i, 0))
    return [row(Q_LORA, P_CQ), row(KV_LORA, P_CKV), row(HEAD_PAD, P_KR), tab, tab, tab,
            one(Q_LORA), one(KV_LORA), one(HEAD_PAD), one(HEAD_PAD),
            full(WIDTH_BP, Q_LORA), full(WIDTH_BP, KV_LORA), full(WIDTH_BP, KV_LORA)]


def _mla_pre(proj, tabs, gq, gkv, gqn, gkn, wuq, wk, wv, name):
    t = proj.shape[0]
    tm = _tile(t, (512, 256, 128))

    def body(cq_ref, ckv_ref, kr_ref, c_ref, sm_ref, sp_ref, gq_ref, gkv_ref, gqn_ref, gkn_ref,
             wuq_ref, wk_ref, wv_ref, q_ref, k_ref, v_ref):
        cqn = _lora_norm(cq_ref[...], gq_ref[...])[0].astype(BF16)
        ckvn = _lora_norm(ckv_ref[...], gkv_ref[...])[0].astype(BF16)
        q_raw = _dot(cqn, wuq_ref[...], _NT)
        k_raw = _dot(ckvn, wk_ref[...], _NT)
        v_ref[...] = _dot(ckvn, wv_ref[...], _NT).astype(BF16)
        kr = pltpu.roll(kr_ref[...], NOPE, 1)
        c, sm, sp = c_ref[...], sm_ref[...], sp_ref[...]
        heads = [slice(h * HEAD_PAD, (h + 1) * HEAD_PAD) for h in range(N_HEADS_B)]
        raw = [q_raw[:, hs] for hs in heads] + [k_raw[:, hs] + kr for hs in heads]
        gains = [gqn_ref[...]] * N_HEADS_B + [gkn_ref[...]] * N_HEADS_B
        sq = [jnp.sum(x * x, axis=1, keepdims=True) for x in raw]
        normed = [x * lax.rsqrt(s * (1.0 / QK_B) + EPS) * g for x, s, g in zip(raw, sq, gains)]
        up = [pltpu.roll(x, HEAD_PAD - ROPE // 2, 1) for x in normed]
        down = [pltpu.roll(x, ROPE // 2, 1) for x in normed]
        roped = [(x * c + u * sm + d * sp).astype(BF16) for x, u, d in zip(normed, up, down)]
        for h, hs in enumerate(heads):
            q_ref[:, hs] = roped[h]
            k_ref[:, hs] = roped[N_HEADS_B + h]

    o_spec = pl.BlockSpec((tm, WIDTH_BP), lambda i: (i, 0))
    o_shape = jax.ShapeDtypeStruct((t, WIDTH_BP), BF16)
    return _tc_call(
        body, name=name, grid=(t // tm,), in_specs=_mla_in_specs(tm), out_specs=[o_spec] * 3,
        out_shape=[o_shape] * 3, compiler_params=_cp("parallel"))(
            proj, proj, proj, *tabs, gq, gkv, gqn, gkn, wuq, wk, wv)


def _mla_pre_bwd(proj, tabs, gq, gkv, gqn, gkn, wuq, wk, wv, dq, dk, dv, name):
    t = proj.shape[0]
    tm = _tile(t, (512, 256, 128))

    def body(cq_ref, ckv_ref, kr_ref, c_ref, sm_ref, sp_ref, gq_ref, gkv_ref, gqn_ref, gkn_ref,
             wuq_ref, wk_ref, wv_ref, dq_ref, dk_ref, dv_ref,
             dcq_ref, dckv_ref, dkr_ref, dwuq_ref, dwk_ref, dwv_ref, dgq_ref, dgkv_ref, dgqn_ref, dgkn_ref,
             dqraw_ref, dkraw_ref):
        @pl.when(pl.program_id(0) == 0)
        def _():
            for r in (dwuq_ref, dwk_ref, dwv_ref, dgq_ref, dgkv_ref, dgqn_ref, dgkn_ref):
                r[...] = jnp.zeros_like(r)

        cqn_f, cq_xh, cq_r = _lora_norm(cq_ref[...], gq_ref[...])
        ckvn_f, ckv_xh, ckv_r = _lora_norm(ckv_ref[...], gkv_ref[...])
        cqn, ckvn = cqn_f.astype(BF16), ckvn_f.astype(BF16)
        q_raw = _dot(cqn, wuq_ref[...], _NT)
        k_raw = _dot(ckvn, wk_ref[...], _NT)
        kr = pltpu.roll(kr_ref[...], NOPE, 1)
        c, sm, sp = c_ref[...], sm_ref[...], sp_ref[...]
        heads = [slice(h * HEAD_PAD, (h + 1) * HEAD_PAD) for h in range(N_HEADS_B)]
        raw = [q_raw[:, hs] for hs in heads] + [k_raw[:, hs] + kr for hs in heads]
        d_out = [dq_ref[:, hs] for hs in heads] + [dk_ref[:, hs] for hs in heads]
        gains = [gqn_ref[...]] * N_HEADS_B + [gkn_ref[...]] * N_HEADS_B
        sq = [jnp.sum(x * x, axis=1, keepdims=True) for x in raw]
        rinv = [lax.rsqrt(s * (1.0 / QK_B) + EPS) for s in sq]
        xhat = [x * r for x, r in zip(raw, rinv)]
        down = [pltpu.roll(d * sm, ROPE // 2, 1) for d in d_out]
        up = [pltpu.roll(d * sp, HEAD_PAD - ROPE // 2, 1) for d in d_out]
        dn = [d * c + a + b for d, a, b in zip(d_out, down, up)]
        dgain = [jnp.sum(d * xh, axis=0, keepdims=True) for d, xh in zip(dn, xhat)]
        dxh = [d * g for d, g in zip(dn, gains)]
        inner = [jnp.sum(d * xh, axis=1, keepdims=True) * (1.0 / QK_B) for d, xh in zip(dxh, xhat)]
        d_raw = [r * (d - xh * s) for r, d, xh, s in zip(rinv, dxh, xhat, inner)]
        for h, hs in enumerate(heads):
            dqraw_ref[:, hs] = d_raw[h].astype(BF16)
            dkraw_ref[:, hs] = d_raw[N_HEADS_B + h].astype(BF16)
        dkr = sum(d_raw[N_HEADS_B + 1:], d_raw[N_HEADS_B])
        dgqn_ref[...] += sum(dgain[1:N_HEADS_B], dgain[0])
        dgkn_ref[...] += sum(dgain[N_HEADS_B + 1:], dgain[N_HEADS_B])
        lane = lax.broadcasted_iota(jnp.int32, (tm, HEAD_PAD), 1)
        dkr_ref[...] = jnp.where(lane < ROPE, pltpu.roll(dkr, HEAD_PAD - NOPE, 1), 0.0).astype(BF16)
        dqraw = dqraw_ref[...]
        dkraw = dkraw_ref[...]
        dvb = dv_ref[...].astype(BF16)
        dwuq_ref[...] += _dot(dqraw, cqn, _TN)
        dwk_ref[...] += _dot(dkraw, ckvn, _TN)
        dwv_ref[...] += _dot(dvb, ckvn, _TN)
        dcqn = _dot(dqraw, wuq_ref[...], _NN)
        dckvn = _dot(dkraw, wk_ref[...], _NN) + _dot(dvb, wv_ref[...], _NN)
        dgq_ref[...] += jnp.sum(dcqn * cq_xh, axis=0, keepdims=True)
        dgkv_ref[...] += jnp.sum(dckvn * ckv_xh, axis=0, keepdims=True)
        dxh = dcqn * gq_ref[...]
        dcq_ref[...] = (cq_r * (dxh - cq_xh * jnp.mean(dxh * cq_xh, axis=1, keepdims=True))).astype(BF16)
        dxh = dckvn * gkv_ref[...]
        dckv_ref[...] = (ckv_r * (dxh - ckv_xh * jnp.mean(dxh * ckv_xh, axis=1, keepdims=True))).astype(BF16)

    wide = pl.BlockSpec((tm, WIDTH_BP), lambda i: (i, 0))
    row = lambda w: pl.BlockSpec((tm, w), lambda i: (i, 0))
    full = lambda r, c: pl.BlockSpec((r, c), lambda i: (0, 0))
    return _tc_call(
        body, name=name, grid=(t // tm,), in_specs=_mla_in_specs(tm) + [wide, wide, wide],
        out_specs=[row(Q_LORA), row(KV_LORA), row(HEAD_PAD), full(WIDTH_BP, Q_LORA), full(WIDTH_BP, KV_LORA),
                   full(WIDTH_BP, KV_LORA), full(1, Q_LORA), full(1, KV_LORA), full(1, HEAD_PAD), full(1, HEAD_PAD)],
        out_shape=[jax.ShapeDtypeStruct((t, Q_LORA), BF16), jax.ShapeDtypeStruct((t, KV_LORA), BF16),
                   jax.ShapeDtypeStruct((t, HEAD_PAD), BF16), jax.ShapeDtypeStruct((WIDTH_BP, Q_LORA), F32),
                   jax.ShapeDtypeStruct((WIDTH_BP, KV_LORA), F32), jax.ShapeDtypeStruct((WIDTH_BP, KV_LORA), F32),
                   jax.ShapeDtypeStruct((1, Q_LORA), F32), jax.ShapeDtypeStruct((1, KV_LORA), F32),
                   jax.ShapeDtypeStruct((1, HEAD_PAD), F32), jax.ShapeDtypeStruct((1, HEAD_PAD), F32)],
        scratch_shapes=[pltpu.VMEM((tm, WIDTH_BP), BF16), pltpu.VMEM((tm, WIDTH_BP), BF16)],
        compiler_params=_cp("arbitrary"))(proj, proj, proj, *tabs, gq, gkv, gqn, gkn, wuq, wk, wv, dq, dk, dv)


def _mla_flash_specs(s_len):
    bh_spec = pl.BlockSpec((s_len, HEAD_PAD), lambda b, h: (b, h))
    lse_spec = pl.BlockSpec((1, s_len, 1), lambda b, h: (b * N_HEADS_B + h, 0, 0))
    return bh_spec, lse_spec


def _diag_mask(s):
    row = lax.broadcasted_iota(jnp.int32, s.shape, 0)
    col = lax.broadcasted_iota(jnp.int32, s.shape, 1)
    return jnp.where(row >= col, s, NEG)


def _mla_flash_fwd(q, k, v, n_batch, s_len, name):
    t = q.shape[0]
    tq = _tile(s_len, (256, 128))
    bh_spec, lse_spec = _mla_flash_specs(s_len)
    c = (QK_B ** -0.5) * LOG2E

    def body(q_ref, k_ref, v_ref, o_ref, lse_ref):
        nq = s_len // tq
        rows = [slice(i * tq, (i + 1) * tq) for i in range(nq)]
        below = [slice(0, i * tq) for i in range(nq)]
        qs = [q_ref[r, :] for r in rows]
        sd = [_diag_mask(_dot(qs[i], k_ref[rows[i], :], _NT)) for i in range(nq)]
        sb = [None] + [_dot(qs[i], k_ref[below[i], :], _NT) for i in range(1, nq)]
        m = [jnp.max(s, axis=1, keepdims=True) for s in sd]
        m = [m[0]] + [jnp.maximum(m[i], jnp.max(sb[i], axis=1, keepdims=True)) for i in range(1, nq)]
        pd = [jnp.exp2((sd[i] - m[i]) * c) for i in range(nq)]
        pb = [None] + [jnp.exp2((sb[i] - m[i]) * c) for i in range(1, nq)]
        l = [jnp.sum(p, axis=1, keepdims=True) for p in pd]
        l = [l[0]] + [l[i] + jnp.sum(pb[i], axis=1, keepdims=True) for i in range(1, nq)]
        acc = [_dot(pd[i].astype(BF16), v_ref[rows[i], :], _NN) for i in range(nq)]
        acc = [acc[0]] + [acc[i] + _dot(pb[i].astype(BF16), v_ref[below[i], :], _NN) for i in range(1, nq)]
        for i in range(nq):
            o_ref[rows[i], :] = (acc[i] * (1.0 / l[i])).astype(BF16)
            lse_ref[0, rows[i], :] = m[i] * c + jnp.log2(l[i])

    return _tc_call(
        body, name=name, grid=(n_batch, N_HEADS_B), in_specs=[bh_spec, bh_spec, bh_spec],
        out_specs=[bh_spec, lse_spec],
        out_shape=[jax.ShapeDtypeStruct((t, WIDTH_BP), BF16),
                   jax.ShapeDtypeStruct((n_batch * N_HEADS_B, s_len, 1), F32)],
        compiler_params=_cp("parallel", "parallel"))(q, k, v)


def _mla_flash_bwd(q, k, v, o, do, lse2, n_batch, s_len, name):
    t = q.shape[0]
    tq = _tile(s_len, (256, 128))
    bh_spec, lse_spec = _mla_flash_specs(s_len)
    scale = QK_B ** -0.5
    c = scale * LOG2E

    def body(q_ref, k_ref, v_ref, o_ref, do_ref, lse_ref, dq_ref, dk_ref, dv_ref):
        nq = s_len // tq
        rows = [slice(i * tq, (i + 1) * tq) for i in range(nq)]
        below = [slice(0, i * tq) for i in range(nq)]
        qs = [q_ref[r, :] for r in rows]
        dos = [do_ref[r, :] for r in rows]
        lse = [lse_ref[0, r, :] for r in rows]
        delta = [jnp.sum(dos[i].astype(F32) * o_ref[rows[i], :].astype(F32), axis=1, keepdims=True)
                 for i in range(nq)]

        def probs_and_ds(i, ks, diag):
            s = _dot(qs[i], k_ref[ks, :], _NT)
            if diag:
                s = _diag_mask(s)
            p = jnp.exp2(s * c - lse[i])
            dp = _dot(dos[i], v_ref[ks, :], _NT)
            return p.astype(BF16), (p * (dp - delta[i]) * scale).astype(BF16)

        diag = [probs_and_ds(i, rows[i], True) for i in range(nq)]
        rest = [None] + [probs_and_ds(i, below[i], False) for i in range(1, nq)]
        for i in range(nq):
            dq = _dot(diag[i][1], k_ref[rows[i], :], _NN)
            if i:
                dq = dq + _dot(rest[i][1], k_ref[below[i], :], _NN)
            dq_ref[rows[i], :] = dq
        for j in range(nq):
            later = slice(j * tq, s_len)
            p_j = jnp.concatenate([diag[j][0]] + [rest[i][0][:, rows[j]] for i in range(j + 1, nq)], axis=0)
            ds_j = jnp.concatenate([diag[j][1]] + [rest[i][1][:, rows[j]] for i in range(j + 1, nq)], axis=0)
            dk_ref[rows[j], :] = _dot(ds_j, q_ref[later, :], _TN)
            dv_ref[rows[j], :] = _dot(p_j, do_ref[later, :], _TN)

    f32_wide = jax.ShapeDtypeStruct((t, WIDTH_BP), F32)
    return _tc_call(
        body, name=name, grid=(n_batch, N_HEADS_B),
        in_specs=[bh_spec, bh_spec, bh_spec, bh_spec, bh_spec, lse_spec],
        out_specs=[bh_spec, bh_spec, bh_spec], out_shape=[f32_wide] * 3,
        compiler_params=_cp("parallel", "parallel"))(q, k, v, o, do, lse2)


def _swa_heads(w, axis, order):
    heads = [lax.slice_in_dim(w, h * HEAD_DIM_A, (h + 1) * HEAD_DIM_A, axis=axis) for h in order]
    return jnp.concatenate(heads, axis=axis)


class _LayerWeights:
    def __init__(self, build):
        self._build, self._mats = build, {}

    def __getitem__(self, name):
        if name not in self._mats:
            self._mats.update(self._build(name))
        return self._mats[name]


PIECES_OF = {"mla_w_ukv": ("wk", "wv")}


def _store(name, w):
    t = w.T if dict(BIG)[name] else w
    if name == "mla_w_uq":
        return {name: jnp.pad(t, ((0, HEAD_PAD - QK_B), (0, 0)))}
    if name == "mla_w_ukv":
        pad = ((0, HEAD_PAD - NOPE), (0, 0))
        return {"wk": jnp.pad(t[:NOPE], pad), "wv": jnp.pad(t[NOPE:], pad)}
    if name == "w_branch_b":
        t3 = jnp.pad(t.reshape(t.shape[0], N_HEADS_B, V_B), ((0, 0), (0, 0), (0, HEAD_PAD - V_B)))
        return {name: t3.reshape(t.shape[0], WIDTH_BP)}
    if name == "w_branch_a":
        return {name: _swa_heads(t, 1, SWA_HEAD_ORDER)}
    return {name: t}


def _unstore(name, pieces):
    if name == "mla_w_uq":
        return pieces[name][:QK_B]
    if name == "mla_w_ukv":
        return jnp.concatenate([pieces["wk"][:NOPE], pieces["wv"][:V_B]], axis=0)
    if name == "w_branch_b":
        g = pieces[name]
        return g.reshape(g.shape[0], N_HEADS_B, HEAD_PAD)[:, :, :V_B].reshape(g.shape[0], WIDTH_B)
    if name == "w_branch_a":
        return _swa_heads(pieces[name], 1, SWA_HEAD_INVERSE)
    return pieces[name]


def _layer_mats(w):
    if "w_in" not in w:
        return dict(w)
    w_in = w["w_in"]
    o = [0]
    for n in (WIDTH_A, KV_A, KV_A, Q_LORA, KV_LORA, ROPE, D_MODEL, D_MODEL):
        o.append(o[-1] + n)
    qa, ka, va, cq, ckv, kr, ga, gb = (w_in[o[i]:o[i + 1]] for i in range(8))
    pad = jnp.zeros((PROJ_W - IN_WIDTH, w_in.shape[1]), w_in.dtype)
    out = dict(w)
    out["w_in"] = jnp.concatenate([ga, gb, _swa_heads(qa, 0, SWA_HEAD_ORDER), cq, ka, va, ckv, kr, pad], axis=0)
    return out


def _unlayer_w_in(d):
    ga, gb, qa, cq, ka, va, ckv, kr = (d[a:b] for a, b in (
        (P_GA, P_GA + D_MODEL), (P_GB, P_GB + D_MODEL), (P_QA, P_QA + WIDTH_A), (P_CQ, P_CQ + Q_LORA),
        (P_KA, P_KA + KV_A), (P_VA, P_VA + KV_A), (P_CKV, P_CKV + KV_LORA), (P_KR, P_KR + ROPE)))
    return jnp.concatenate([_swa_heads(qa, 0, SWA_HEAD_INVERSE), ka, va, cq, ckv, kr, ga, gb], axis=0)


def _pad_lanes(v, width):
    return jnp.pad(v.reshape(1, -1), ((0, 0), (0, width - v.shape[-1])))


def _rope_tables(positions):
    half = ROPE // 2
    inv_freq = ROPE_BASE ** (-jnp.arange(half, dtype=F32) / half)
    ang = positions.astype(F32).reshape(-1, 1) * inv_freq
    cos, sin = jnp.cos(ang), jnp.sin(ang)
    t = cos.shape[0]
    one, zero = jnp.ones((t, NOPE), F32), jnp.zeros((t, NOPE), F32)
    tail = jnp.zeros((t, HEAD_PAD - QK_B), F32)
    z16 = jnp.zeros((t, half), F32)
    c = jnp.concatenate([one, cos, cos, tail], axis=1)
    sm = jnp.concatenate([zero, -sin, z16, tail], axis=1)
    sp = jnp.concatenate([zero, z16, sin, tail], axis=1)
    return c, sm, sp


def _ffn_fwd(x, gain, wg_t, wu_t, wd, tag):
    n, a, b, hmid = _ffn_up(x, gain, wg_t, wu_t, f"{tag}_up")
    y = _mm([(hmid, wd)], "nn", F32, f"{tag}_down", residual=x, alpha=0.5)
    return y, (x, n, a, b, hmid)


def _ffn_bwd(dy, dyb, saved, gain, wg_t, wu_t, wd, tag, grads, names, hook):
    x, n, a, b, hmid = saved
    da, db = _ffn_down_bwd(dyb, wd, a, b, f"{tag}_down_bwd")
    grads[names[0]] = _mm([(da, n)], "tn", BF16, f"{tag}_dwg")
    grads[names[1]] = _mm([(db, n)], "tn", BF16, f"{tag}_dwu")
    hook("gu", grads[names[1]])
    dx, dxb, g_gain = _mm_rms_bwd([(da, wg_t), (db, wu_t)], x, gain, dy, f"{tag}_dn")
    hook("dn", dx)
    grads[names[2]] = _mm([(hmid, dyb)], "tn", BF16, f"{tag}_dwd", alpha=0.5)
    return dx, dxb, g_gain


def _fold_halves(d):
    return d[:, :HEAD_DIM_A] + d[:, HEAD_DIM_A:]


def _local_step(x, positions, target, layers, smalls, at=None):
    at = at or (lambda point, l, token, grads: None)
    _Order.tokens = ()
    n_batch, s_len, d = x.shape
    t = n_batch * s_len
    xt = x.reshape(t, d)
    tabs = _rope_tables(positions)
    pos_col = positions.reshape(t, 1)
    pos_row = positions.reshape(t // BLOCK, 1, BLOCK)
    saved = []
    get_layer = layers if callable(layers) else layers.__getitem__
    for l in range(len(smalls)):
        w, s = get_layer(l), smalls[l]
        g1, gm, g2 = (s[k].reshape(1, d) for k in ("ffn1_norm", "mix_norm", "ffn2_norm"))
        qg2, kg2 = (jnp.tile(s[k].reshape(1, -1), (1, 2)) for k in ("swa_q_norm", "swa_k_norm"))
        sinks = _pad_lanes(s["swa_sinks"], HEAD_PAD)
        gq, gkv = s["mla_q_lora_norm"].reshape(1, -1), s["mla_kv_lora_norm"].reshape(1, -1)
        gqn, gkn = _pad_lanes(s["mla_q_norm"], HEAD_PAD), _pad_lanes(s["mla_k_norm"], HEAD_PAD)
        x1, sv1 = _ffn_fwd(xt, g1, w["ffn1_w_gate"], w["ffn1_w_up"], w["ffn1_w_down"], f"l{l}_ffn1")
        h, proj = _rms_mm(x1, gm, w["w_in"], f"l{l}_proj")
        at("proj", l, proj, None)
        oa = _swa_fwd(proj, pos_col, pos_row, qg2, kg2, sinks, n_batch, s_len, f"l{l}_swa")
        q, k, v = _mla_pre(proj, tabs, gq, gkv, gqn, gkn, w["mla_w_uq"], w["wk"], w["wv"], f"l{l}_mla_pre")
        ob, lse = _mla_flash_fwd(q, k, v, n_batch, s_len, f"l{l}_mla")
        merged, ya, yb = _merge_fwd(oa, ob, proj, w["w_branch_a"], w["w_branch_b"], f"l{l}_merge")
        x2 = _mm([(merged, w["w_out"])], "nn", F32, f"l{l}_out", residual=x1)
        at("out", l, x2, None)
        x3, sv2 = _ffn_fwd(x2, g2, w["ffn2_w_gate"], w["ffn2_w_up"], w["ffn2_w_down"], f"l{l}_ffn2")
        saved.append((w, sv1, sv2, x1, h, proj, oa, q, k, v, ob, lse, merged, ya, yb,
                      (g1, gm, g2, qg2, kg2, sinks, gq, gkv, gqn, gkn)))
        xt = x3

    dy, dyb, loss = _loss_head(xt, target.reshape(t, d))

    big_grads, small_grads = [None] * len(smalls), [None] * len(smalls)
    for l in reversed(range(len(smalls))):
        w, sv1, sv2, x1, h, proj, oa, q, k, v, ob, lse, merged, ya, yb, gains = saved[l]
        g1, gm, g2, qg2, kg2, sinks, gq, gkv, gqn, gkn = gains
        bg, sg = {}, {}
        dy, dyb, sg["ffn2_norm"] = _ffn_bwd(
            dy, dyb, sv2, g2, w["ffn2_w_gate"], w["ffn2_w_up"], w["ffn2_w_down"], f"l{l}_ffn2", bg, FFN2,
            lambda point, token, l=l, bg=bg: at("ffn2_" + point, l, token, bg))
        dya, dyb_, dga, dgb = _merge_bwd(dyb, w["w_out"], proj, ya, yb, f"l{l}_merge_bwd")
        at("mixer", l, sg["ffn2_norm"], bg)
        bg["w_out"] = _mm([(merged, dyb)], "tn", BF16, f"l{l}_dwo")
        doa = _mm([(dya, w["w_branch_a"])], "nn", BF16, f"l{l}_doa")
        bg["w_branch_a"] = _mm([(dya, oa)], "tn", BF16, f"l{l}_dwa")
        dob = _mm([(dyb_, w["w_branch_b"])], "nn", BF16, f"l{l}_dob")
        bg["w_branch_b"] = _mm([(dyb_, ob)], "tn", BF16, f"l{l}_dwb")
        dqa, dkc, dkp, dvc, dvp, dqg, dsk = _swa_bwd(
            proj, pos_col, pos_row, qg2, kg2, sinks, doa, n_batch, s_len, f"l{l}_swa_bwd")
        sg["swa_q_norm"], sg["swa_sinks"] = _fold_halves(dqg), dsk[:, :N_HEADS_A]
        at("mixer_mid", l, dqa, bg)
        dka, dva, dkg = _swa_kv_bwd(proj, kg2, dkc, dkp, dvc, dvp, n_batch, s_len, f"l{l}_swa_kv_bwd")
        sg["swa_k_norm"] = _fold_halves(dkg)
        dq, dk, dv = _mla_flash_bwd(q, k, v, ob, dob, lse, n_batch, s_len, f"l{l}_mla_bwd")
        (dcq, dckv, dkr, g_uq, g_wk, g_wv, sg["mla_q_lora_norm"], sg["mla_kv_lora_norm"], dgqn, dgkn) = _mla_pre_bwd(
            proj, tabs, gq, gkv, gqn, gkn, w["mla_w_uq"], w["wk"], w["wv"], dq, dk, dv, f"l{l}_mla_pre_bwd")
        sg["mla_q_norm"], sg["mla_k_norm"] = dgqn[:, :QK_B], dgkn[:, :QK_B]
        bg["mla_w_uq"], bg["wk"], bg["wv"] = g_uq.astype(BF16), g_wk.astype(BF16), g_wv.astype(BF16)
        dproj = (dga, dgb, dqa, dcq, dka, dva, dckv, dkr)
        bg["w_in"] = _mm([(dproj, h)], "tn", BF16, f"l{l}_dwin")
        dy, dyb, sg["mix_norm"] = _mm_rms_bwd([(dproj, w["w_in"])], x1, gm, dy, f"l{l}_dh")
        at("ffn1", l, sg["mix_norm"], bg)
        dy, dyb, sg["ffn1_norm"] = _ffn_bwd(
            dy, dyb, sv1, g1, w["ffn1_w_gate"], w["ffn1_w_up"], w["ffn1_w_down"], f"l{l}_ffn1", bg, FFN1,
            lambda point, token, l=l, bg=bg: at("ffn1_" + point, l, token, bg))
        big_grads[l], small_grads[l] = bg, sg
        at("done", l, dy, bg)
    return loss, dy.reshape(n_batch, s_len, d), big_grads, small_grads


def _round_up(n, m):
    return (n + m - 1) // m * m


def _flat_layout(piece_shapes, members, row_tile):
    table, off = [], 0
    for l, piece in members:
        rows, k = piece_shapes[piece]
        pr = _round_up(rows * k // LANES, 16)
        table.append(((l, piece), off, pr, rows, k))
        off += pr
    return table, _round_up(off, row_tile)


def _pack_flat(stored, table, total):
    parts, off = [], 0
    for key, o, pr, rows, k in table:
        w = stored[key].reshape(rows * k // LANES, LANES)
        parts.append(jnp.pad(w, ((0, pr - w.shape[0]), (0, 0))))
        off = o + pr
    if total > off:
        parts.append(jnp.zeros((total - off, LANES), parts[0].dtype))
    return jnp.concatenate(parts, axis=0)


def _unpack_flat(flat, table):
    return {key: flat[o:o + rows * k // LANES].reshape(rows, k) for key, o, pr, rows, k in table}


def _gathered_mats(gathered, table, layer):
    return {piece: gathered[:, o:o + rows * k // LANES].reshape(N_DEV * rows, k)
            for (l, piece), o, pr, rows, k in table if l == layer}


def _pack_grads(grads, table, total):
    parts, off = [], 0
    for key, o, pr, rows, k in table:
        g = grads[key].reshape(N_DEV, rows * k // LANES, LANES)
        parts.append(jnp.pad(g, ((0, 0), (0, pr - g.shape[1]), (0, 0))))
        off = o + pr
    if total > off:
        parts.append(jnp.zeros((N_DEV, total - off, LANES), BF16))
    return jnp.concatenate(parts, axis=1)


def _pack_small(params, last=None):
    parts = [params[n][l].reshape(-1).astype(F32) for l in range(DEPTH) for n in SMALL]
    v = jnp.concatenate(parts)
    v = jnp.pad(v, (0, SMALL_ROWS * LANES - 1 - v.shape[0]))
    last = jnp.zeros((1,), F32) if last is None else last.reshape(1)
    return jnp.concatenate([v, last]).reshape(SMALL_ROWS, LANES)


def _unpack_small(flat, shapes):
    v, out, off = flat.reshape(-1), {}, 0
    for l in range(DEPTH):
        for n in SMALL:
            sz = math.prod(shapes[n][1:])
            out.setdefault(n, []).append(v[off:off + sz].reshape(shapes[n][1:]))
            off += sz
    return {n: jnp.stack(p) for n, p in out.items()}


_MESH = pl.DeviceIdType.MESH


def _place():
    return lax.axis_index("x"), lax.axis_index("y"), lax.axis_index("c")


def _handshake(peers):
    barrier = pltpu.get_barrier_semaphore()
    for peer in peers:
        pl.semaphore_signal(barrier, inc=1, device_id=peer, device_id_type=_MESH)
    pl.semaphore_wait(barrier, len(peers))


def _comm_call(body, out_shape, scratch, name, seq_id, spec=_ANY):
    if seq_id is None:
        return pl.pallas_call(body, name=name, out_shape=out_shape, in_specs=[spec, _ANY], out_specs=spec,
                              scratch_shapes=scratch)
    nbytes = LINK_COST_SCALE * math.prod(out_shape.shape) * out_shape.dtype.itemsize
    return pl.kernel(body, out_type=out_shape, mesh=plsc.ScalarSubcoreMesh(axis_name="sequencer", num_cores=1),
                     scratch_types=scratch, name=name, compiler_params=pltpu.CompilerParams(collective_id=seq_id),
                     cost_estimate=pl.CostEstimate(flops=0, transcendentals=0, bytes_accessed=nbytes))


def _all_gather(x_shard, name, vmem=False, seq_id=None, after=None):
    spec = pl.BlockSpec(memory_space=pltpu.VMEM) if vmem else _ANY

    def body(x_ref, after_ref, out_ref, send_sems, recv_sems, local_sem):
        x, y, c = _place()
        me, sibling = (x, y, c), (x, y, 1 - c)
        chips = [(1 - x, y), (x, 1 - y), (1 - x, 1 - y)]
        if seq_id is not None:
            _handshake([sibling] + [(*chip, c) for chip in chips])

        def rows(px, py, pc):
            return out_ref.at[4 * px + 2 * py + pc]

        def copy(k, block, to, src=None):
            return pltpu.make_async_remote_copy(
                src_ref=rows(*block) if src is None else src, dst_ref=rows(*block),
                send_sem=send_sems.at[k], recv_sem=recv_sems.at[k], device_id=to, device_id_type=_MESH)

        mine = pltpu.make_async_copy(x_ref, rows(*me), local_sem)
        mine.start()
        first = [copy(0, me, sibling, src=x_ref)]
        first += [copy(1 + j, me, (*chip, c), src=x_ref) for j, chip in enumerate(chips)]
        for cp in first:
            cp.start()
        passed = [copy(4 + j, (*chip, c), sibling) for j, chip in enumerate(chips)]
        for j, chip in enumerate(chips):
            copy(1 + j, (*chip, c), me).wait_recv()
            passed[j].start()
        copy(0, sibling, me).wait_recv()
        for j, chip in enumerate(chips):
            copy(4 + j, (*chip, 1 - c), me).wait_recv()
        for cp in first + passed:
            cp.wait_send()
        mine.wait()

    return _comm_call(
        body, jax.ShapeDtypeStruct((N_DEV,) + x_shard.shape, x_shard.dtype),
        [pltpu.SemaphoreType.DMA((7,)), pltpu.SemaphoreType.DMA((7,)), pltpu.SemaphoreType.DMA], name, seq_id,
        spec)(x_shard, x_shard if after is None else after)


def _exchange_cores(g4, name, seq_id=None, after=None):
    n_chip, _, r, w = g4.shape

    def body(g_ref, after_ref, out_ref, send_sems, recv_sems):
        x, y, c = _place()
        if seq_id is not None:
            _handshake([(x, y, 1 - c)])
        copies = [pltpu.make_async_remote_copy(
            src_ref=g_ref.at[q, 1 - c], dst_ref=out_ref.at[q], send_sem=send_sems.at[q], recv_sem=recv_sems.at[q],
            device_id=(x, y, 1 - c), device_id_type=_MESH) for q in range(n_chip)]
        for cp in copies:
            cp.start()
        for cp in copies:
            cp.wait()

    return _comm_call(
        body, jax.ShapeDtypeStruct((n_chip, r, w), g4.dtype),
        [pltpu.SemaphoreType.DMA((n_chip,)), pltpu.SemaphoreType.DMA((n_chip,))], name, seq_id)(g4, g4 if after is None else after)


def _exchange_chips(s1, name, seq_id=None):
    _, r, w = s1.shape

    def body(s_ref, after_ref, out_ref, send_sems, recv_sems):
        x, y, c = _place()
        chips = [(1 - x, y), (x, 1 - y), (1 - x, 1 - y)]
        if seq_id is not None:
            _handshake([(*chip, c) for chip in chips])
        copies = []
        for k, (tx, ty) in enumerate(chips):
            copies.append(pltpu.make_async_remote_copy(
                src_ref=s_ref.at[2 * tx + ty], dst_ref=out_ref.at[k], send_sem=send_sems.at[k],
                recv_sem=recv_sems.at[k], device_id=(tx, ty, c), device_id_type=_MESH))
        for cp in copies:
            cp.start()
        for cp in copies:
            cp.wait()

    return _comm_call(
        body, jax.ShapeDtypeStruct((3, r, w), s1.dtype),
        [pltpu.SemaphoreType.DMA((3,)), pltpu.SemaphoreType.DMA((3,))], name, seq_id)(s1, s1)


def _chip_sum(g4, recv, core, after, name, tr):
    n_chip, _, r, w = g4.shape

    def body(c_ref, a_ref, b_ref, after_ref, o_ref):
        o_ref[...] = (a_ref[...].astype(F32) + b_ref[...].astype(F32)).astype(o_ref.dtype)

    grid_spec = pltpu.PrefetchScalarGridSpec(
        num_scalar_prefetch=1, grid=(n_chip, r // tr),
        in_specs=[pl.BlockSpec((None, None, tr, w), lambda q, i, c: (q, c[0], i, 0)),
                  pl.BlockSpec((None, tr, w), lambda q, i, c: (q, i, 0)), _ANY],
        out_specs=pl.BlockSpec((None, tr, w), lambda q, i, c: (q, i, 0)))
    return pl.pallas_call(
        body, name=name, grid_spec=grid_spec, out_shape=jax.ShapeDtypeStruct((n_chip, r, w), g4.dtype),
        compiler_params=_cp("parallel", "parallel"))(core, g4, recv, after)


def _adam(w, g, m, v):
    m = ADAM_B1 * m + (1.0 - ADAM_B1) * g
    v = ADAM_B2 * v + (1.0 - ADAM_B2) * (g * g)
    m_hat = m / (1.0 - ADAM_B1 ** ADAM_STEP)
    v_hat = v / (1.0 - ADAM_B2 ** ADAM_STEP)
    delta = -ADAM_LR * (m_hat / (jnp.sqrt(v_hat) + ADAM_EPS) + ADAM_WD * w)
    return delta, m, v


def _grad_sum(s1, r2, chip, name, tr):
    _, r, lanes = s1.shape

    def body(c_ref, s_ref, r0_ref, r1_ref, r2_ref, g_out):
        g_out[...] = ((s_ref[...].astype(F32) + r0_ref[...].astype(F32)) + r1_ref[...].astype(F32)) + r2_ref[
            ...].astype(F32)

    row = pl.BlockSpec((tr, lanes), lambda i, c: (i, 0))
    rel = lambda k: pl.BlockSpec((None, tr, lanes), lambda i, c: (k, i, 0))
    grid_spec = pltpu.PrefetchScalarGridSpec(
        num_scalar_prefetch=1, grid=(r // tr,),
        in_specs=[pl.BlockSpec((None, tr, lanes), lambda i, c: (c[0], i, 0)), rel(0), rel(1), rel(2)], out_specs=row)
    return pl.pallas_call(
        body, name=name, grid_spec=grid_spec, out_shape=jax.ShapeDtypeStruct((r, lanes), F32),
        compiler_params=_cp("parallel"))(chip, s1, r2, r2, r2)


def _adam_big(w, g, m, v, name):
    depth, k, n = w.shape
    tk = k if k <= 512 else _tile(k, (256, 128))

    def body(w_ref, g_ref, m_ref, v_ref, d_out, m_out, v_out):
        d, mn, vn = _adam(w_ref[...], g_ref[...], m_ref[...], v_ref[...])
        d_out[...] = d
        m_out[...] = mn
        v_out[...] = vn

    blk = pl.BlockSpec((None, tk, n), lambda l, i: (l, i, 0))
    return pl.pallas_call(
        body, name=name, grid=(depth, k // tk), in_specs=[blk] * 4, out_specs=[blk] * 3,
        out_shape=[jax.ShapeDtypeStruct(w.shape, F32)] * 3, compiler_params=_cp("parallel", "parallel"))(w, g, m, v)


def _adam_small(parts, w, m, v, name):
    rows, lanes = w.shape

    def body(p_ref, w_ref, m_ref, v_ref, g_out, d_out, m_out, v_out):
        g = p_ref[0:rows, :]
        for dev in range(1, N_DEV):
            g = g + p_ref[dev * rows:(dev + 1) * rows, :]
        d, mn, vn = _adam(w_ref[...], g, m_ref[...], v_ref[...])
        g_out[...] = g
        d_out[...] = d
        m_out[...] = mn
        v_out[...] = vn

    return pl.pallas_call(
        body, name=name, out_shape=[jax.ShapeDtypeStruct((rows, lanes), F32)] * 4)(parts, w, m, v)


def kernel(x, positions, ffn1_norm, ffn1_w_gate, ffn1_w_up, ffn1_w_down, mix_norm, w_in, swa_q_norm, swa_k_norm, swa_sinks, mla_q_lora_norm, mla_w_uq, mla_kv_lora_norm, mla_w_ukv, mla_q_norm, mla_k_norm, w_branch_a, w_branch_b, w_out, ffn2_norm, ffn2_w_gate, ffn2_w_up, ffn2_w_down, loss_target, m_ffn1_norm, m_ffn1_w_gate, m_ffn1_w_up, m_ffn1_w_down, m_mix_norm, m_w_in, m_swa_q_norm, m_swa_k_norm, m_swa_sinks, m_mla_q_lora_norm, m_mla_w_uq, m_mla_kv_lora_norm, m_mla_w_ukv, m_mla_q_norm, m_mla_k_norm, m_w_branch_a, m_w_branch_b, m_w_out, m_ffn2_norm, m_ffn2_w_gate, m_ffn2_w_up, m_ffn2_w_down, v_ffn1_norm, v_ffn1_w_gate, v_ffn1_w_up, v_ffn1_w_down, v_mix_norm, v_w_in, v_swa_q_norm, v_swa_k_norm, v_swa_sinks, v_mla_q_lora_norm, v_mla_w_uq, v_mla_kv_lora_norm, v_mla_w_ukv, v_mla_q_norm, v_mla_k_norm, v_w_branch_a, v_w_branch_b, v_w_out, v_ffn2_norm, v_ffn2_w_gate, v_ffn2_w_up, v_ffn2_w_down):
    given = dict(locals())
    params = {n: given[n] for n in WEIGHTS}
    mom1 = {n: given["m_" + n] for n in WEIGHTS}
    mom2 = {n: given["v_" + n] for n in WEIGHTS}
    assert N_HEADS_B == N_DEV
    stored = {(l, piece): w for l in range(DEPTH) for n, _ in BIG for piece, w in _store(n, params[n][l]).items()}
    piece_shapes = {piece: w.shape for (l, piece), w in stored.items() if l == 0}
    gsegs = [(members, tile) + _flat_layout(piece_shapes, members, tile) for members, tile in GATHER_SEGMENTS]
    rsegs = [(members, tile) + _flat_layout(piece_shapes, members, tile) for members, tile in SCATTER_SEGMENTS]

    def members_of(seg, l):
        return [n for sl, n in seg[0] if sl == l]

    cx, cy, cc = _place()
    core = jnp.reshape(cc, (1,)).astype(jnp.int32)
    chip = jnp.reshape(2 * cx + cy, (1,)).astype(jnp.int32)

    gathered = []
    for s, (_, _, table, total) in enumerate(gsegs):
        w_flat = _pack_flat(stored, table, total).astype(BF16)
        gathered.append(_all_gather(w_flat, f"gather_s{s}", seq_id=SEQ_IDS["gather", s]))

    def get_layer(l):
        def build(name):
            for seg, g in zip(gsegs, gathered):
                if name in members_of(seg, l):
                    return _layer_mats(_gathered_mats(g, seg[2], l))
            raise KeyError(name)
        return _LayerWeights(build)

    smalls = [{n: params[n][l] for n in SMALL} for l in range(DEPTH)]

    pending, big_out, layer_grads = {}, [None] * len(rsegs), {}

    def exchange_cores(s):
        _, _, table, total = rsegs[s]
        mine = {(l, n): _unlayer_w_in(layer_grads[l][n]) if n == "w_in" else layer_grads[l][n] for l, n in rsegs[s][0]}
        g_flat = _pack_grads(mine, table, total)
        g4 = g_flat.reshape(N_DEV // 2, 2, total, LANES)
        pending[s] = (g4, _exchange_cores(g4, f"scatter_cores_s{s}", seq_id=SEQ_IDS["cores", s]))

    def exchange_chips(s, after):
        g4, from_core = pending.pop(s)
        s1 = _chip_sum(g4, from_core, core, after, f"sum_cores_s{s}", rsegs[s][1])
        _Order.tokens = (s1,)
        pending[s] = (s1, _exchange_chips(s1, f"scatter_chips_s{s}", seq_id=SEQ_IDS["chips", s]))

    def finish(s):
        s1, from_chips = pending.pop(s)
        big_out[s] = _grad_sum(s1, from_chips, chip, f"grad_sum_s{s}", rsegs[s][1])

    plan = {("ffn1", 1): [("cores", 4)], ("ffn1_gu", 1): [("chips", 4)],
            ("mixer", 0): [("wait", 4), ("cores", 3)], ("mixer_mid", 0): [("chips", 3)],
            ("ffn1", 0): [("wait", 3), ("cores", 2)], ("ffn1_gu", 0): [("cores", 1), ("chips", 2)],
            ("ffn1_dn", 0): [("chips", 1)], ("done", 0): [("cores", 0)]}

    def at(point, l, token, grads):
        if grads is not None:
            layer_grads[l] = grads
        for what, s in plan.get((point, l), ()):
            if what == "cores":
                exchange_cores(s)
            elif what == "chips":
                exchange_chips(s, token)
            else:
                _Order.tokens += (pending[s][1],)

    loss, grad_x, _, small_grads = _local_step(x, positions, loss_target, get_layer, smalls, at)
    exchange_chips(0, grad_x)

    g_small = _pack_small({n: [small_grads[l][n] for l in range(DEPTH)] for n in SMALL}, loss)
    parts = _all_gather(g_small, "gather_small", vmem=True).reshape(N_DEV * SMALL_ROWS, LANES)
    small_out = _adam_small(parts, _pack_small(params), _pack_small(mom1), _pack_small(mom2), "adam_small")
    shapes = {n: params[n].shape for n in SMALL}
    outs = [_unpack_small(small, shapes) for small in small_out]
    loss = small_out[0].reshape(-1)[-1]

    pieces = {}
    for s in reversed(range(len(rsegs))):
        finish(s)
        pieces.update(_unpack_flat(big_out[s], rsegs[s][2]))
    last = SCATTER_SEGMENTS[0][0][0][1]
    for n, tr in sorted(BIG, key=lambda entry: entry[0] == last):
        view = (lambda a: jnp.swapaxes(a, 1, 2)) if tr else (lambda a: a)
        g = jnp.stack([_unstore(n, {p: pieces[l, p] for p in PIECES_OF.get(n, (n,))}) for l in range(DEPTH)])
        updated = _adam_big(view(params[n]), g, view(mom1[n]), view(mom2[n]), f"adam_{n}")
        for tree, leaf in zip(outs, (g,) + tuple(updated)):
            tree[n] = view(leaf)
    return (loss, grad_x, *[o[n] for o in outs for n in WEIGHTS])
```

```python
import math

import jax
import jax.numpy as jnp
from jax import lax
from jax.experimental import pallas as pl
from jax.experimental.pallas import tpu as pltpu
from jax.experimental.pallas import tpu_sc as plsc

F32 = jnp.float32
BF16 = jnp.bfloat16

N_DEV = 8
DEPTH = 2
D_MODEL = 1024
D_FF = 2816
HEAD_DIM_A = 64
N_HEADS_A = 8
N_KV_HEADS_A = 2
GROUP_A = N_HEADS_A // N_KV_HEADS_A
BLOCK = 128
N_HEADS_B = 8
Q_LORA = 256
KV_LORA = 128
NOPE = 64
ROPE = 32
QK_B = NOPE + ROPE
V_B = 64
HEAD_PAD = 128
WIDTH_A = N_HEADS_A * HEAD_DIM_A
WIDTH_B = N_HEADS_B * V_B
WIDTH_BP = N_HEADS_B * HEAD_PAD
KV_A = N_KV_HEADS_A * HEAD_DIM_A
IN_WIDTH = WIDTH_A + 2 * KV_A + Q_LORA + KV_LORA + ROPE + 2 * D_MODEL
ROPE_BASE = 10000.0
EPS = 1e-6
NEG = -1e30
LOG2E = 1.4426950408889634

P_GA, P_GB, P_QA, P_CQ, P_KA, P_VA, P_CKV, P_KR = 0, 1024, 2048, 2560, 2816, 2944, 3072, 3200
PROJ_W = 3328
SWA_HEAD_ORDER = (0, 4, 1, 5, 2, 6, 3, 7)
SWA_HEAD_INVERSE = tuple(SWA_HEAD_ORDER.index(h) for h in range(N_HEADS_A))

ADAM_LR, ADAM_B1, ADAM_B2, ADAM_EPS, ADAM_WD, ADAM_STEP = 0.001, 0.9, 0.999, 1e-08, 0.01, 10

VMEM_LIMIT = 56 * 1024 * 1024
LANES = 1024

BIG = (("ffn1_w_gate", True), ("ffn1_w_up", True), ("ffn1_w_down", False), ("w_in", True), ("mla_w_uq", True),
       ("mla_w_ukv", True), ("w_branch_a", True), ("w_branch_b", True), ("w_out", False),
       ("ffn2_w_gate", True), ("ffn2_w_up", True), ("ffn2_w_down", False))
SMALL = ("ffn1_norm", "mix_norm", "ffn2_norm", "swa_q_norm", "swa_k_norm", "swa_sinks", "mla_q_lora_norm",
         "mla_kv_lora_norm", "mla_q_norm", "mla_k_norm")
WEIGHTS = ("ffn1_norm", "ffn1_w_gate", "ffn1_w_up", "ffn1_w_down", "mix_norm", "w_in", "swa_q_norm", "swa_k_norm",
           "swa_sinks", "mla_q_lora_norm", "mla_w_uq", "mla_kv_lora_norm", "mla_w_ukv", "mla_q_norm", "mla_k_norm",
           "w_branch_a", "w_branch_b", "w_out", "ffn2_norm", "ffn2_w_gate", "ffn2_w_up", "ffn2_w_down")
SMALL_ROWS = 8
FFN1 = ("ffn1_w_gate", "ffn1_w_up", "ffn1_w_down")
FFN2 = ("ffn2_w_gate", "ffn2_w_up", "ffn2_w_down")
MIXER = ("w_in", "mla_w_uq", "wk", "wv", "w_branch_a", "w_branch_b", "w_out")
GATHER_SEGMENTS = (
    (tuple((0, n) for n in FFN1[:2]), 352),
    (((0, FFN1[2]),), 352),
    (tuple((0, n) for n in MIXER), 400),
    (tuple((0, n) for n in FFN2) + tuple((1, n) for n in FFN1), 352),
    (tuple((1, n) for n in MIXER + FFN2), 464),
)
SCATTER_SEGMENTS = (
    (((0, FFN1[2]),), 352),
    (tuple((0, n) for n in FFN1[:2]), 352),
    (tuple((0, n) for n in MIXER), 400),
    (tuple((0, n) for n in FFN2) + tuple((1, n) for n in FFN1), 352),
    (tuple((1, n) for n in MIXER + FFN2), 464),
)
LINK_COST_SCALE = 64
SEQ_IDS = {(kind, s): 1 + 5 * k + s for k, kind in enumerate(("gather", "cores", "chips")) for s in range(5)}


def _cp(*sem):
    return pltpu.CompilerParams(dimension_semantics=sem, vmem_limit_bytes=VMEM_LIMIT)


def _tile(n, prefs):
    for t in prefs:
        if n % t == 0:
            return t
    return n


def _dot(a, b, dims):
    return lax.dot_general(a, b, (dims, ((), ())), preferred_element_type=F32)


_NT = ((1,), (1,))
_NN = ((1,), (0,))
_TN = ((0,), (0,))


_ANY = pl.BlockSpec(memory_space=pl.ANY)


class _Order:
    tokens = ()


def _tc_call(body, *, in_specs, **kw):
    def run(*args):
        tokens, n = _Order.tokens, len(args)
        if not tokens:
            out = pl.pallas_call(body, in_specs=in_specs, **kw)(*args)
        else:
            def chained(*refs):
                return body(*refs[:n], *refs[n + len(tokens):])
            out = pl.pallas_call(chained, in_specs=list(in_specs) + [_ANY] * len(tokens), **kw)(*args, *tokens)
        _Order.tokens = (jax.tree.leaves(out)[0],)
        return out
    return run


def _sigmoid(x):
    return 0.5 * jnp.tanh(0.5 * x) + 0.5


def _chunks(n, width):
    return [(c, min(width, n - c)) for c in range(0, n, width)]


def _mm(pairs, mode, out_dtype, name, residual=None, alpha=1.0):
    for a, b in pairs:
        for piece in (a if isinstance(a, tuple) else (a,)):
            assert piece.dtype == BF16 and b.dtype == BF16, (name, piece.dtype, b.dtype)
    if mode == "tn":
        (a, b), = pairs
        return _mm_tokens(a, b, out_dtype, name, alpha)
    t = pairs[0][0].shape[0]
    n = pairs[0][1].shape[0] if mode == "nt" else pairs[0][1].shape[1]
    tm = _tile(t, (512, 256, 128))
    dims = _NT if mode == "nt" else _NN
    in_specs, args = [], []
    for a, w in pairs:
        in_specs.append(pl.BlockSpec((tm, a.shape[1]), lambda i: (i, 0)))
        in_specs.append(pl.BlockSpec(w.shape, lambda i: (0, 0)))
        args += [a, w]
    if residual is not None:
        in_specs.append(pl.BlockSpec((tm, n), lambda i: (i, 0)))
        args.append(residual)
    n_pairs = len(pairs)

    def body(*refs):
        o_ref = refs[-1]
        for c0, cw in _chunks(n, 512):
            acc = None
            for p in range(n_pairs):
                w_ref = refs[2 * p + 1]
                w = w_ref[c0:c0 + cw, :] if mode == "nt" else w_ref[:, c0:c0 + cw]
                d = _dot(refs[2 * p][...], w, dims)
                acc = d if acc is None else acc + d
            if alpha != 1.0:
                acc = acc * alpha
            if residual is not None:
                acc = refs[2 * n_pairs][:, c0:c0 + cw] + acc
            o_ref[:, c0:c0 + cw] = acc.astype(out_dtype)

    return _tc_call(
        body, name=name, grid=(t // tm,), in_specs=in_specs, out_specs=pl.BlockSpec((tm, n), lambda i: (i, 0)),
        out_shape=jax.ShapeDtypeStruct((t, n), out_dtype), compiler_params=_cp("parallel"))(*args)


def _mm_tokens(a, b, out_dtype, name, alpha):
    pieces = a if isinstance(a, tuple) else (a,)
    t = b.shape[0]
    widths = [p.shape[1] for p in pieces]
    m, n = sum(widths), b.shape[1]
    tk = _tile(t, (512, 256, 128))
    n_pieces = len(pieces)

    def body(*refs):
        b_ref, o_ref, acc_ref = refs[n_pieces:]
        k = pl.program_id(0)

        @pl.when(k == 0)
        def _():
            acc_ref[...] = jnp.zeros_like(acc_ref)

        off = 0
        for a_ref, width in zip(refs[:n_pieces], widths):
            for c0, cw in _chunks(width, 512):
                acc_ref[off + c0:off + c0 + cw, :] += _dot(a_ref[:, c0:c0 + cw], b_ref[...], _TN)
            off += width

        @pl.when(k == pl.num_programs(0) - 1)
        def _():
            o_ref[...] = (acc_ref[...] * alpha).astype(out_dtype)

    return _tc_call(
        body, name=name, grid=(t // tk,),
        in_specs=[pl.BlockSpec((tk, w), lambda k: (k, 0)) for w in widths] + [pl.BlockSpec((tk, n), lambda k: (k, 0))],
        out_specs=pl.BlockSpec((m, n), lambda k: (0, 0)), out_shape=jax.ShapeDtypeStruct((m, n), out_dtype),
        scratch_shapes=[pltpu.VMEM((m, n), F32)], compiler_params=_cp("arbitrary"))(*pieces, b)


def _rms_mm(x, gain, w, name):
    t, d = x.shape
    n = w.shape[0]
    tm = _tile(t, (512, 256, 128))

    def body(x_ref, g_ref, w_ref, h_ref, o_ref):
        xv = x_ref[...]
        r = lax.rsqrt(jnp.mean(xv * xv, axis=1, keepdims=True) + EPS)
        hv = (xv * r * g_ref[...]).astype(BF16)
        h_ref[...] = hv
        for c0, cw in _chunks(n, 512):
            o_ref[:, c0:c0 + cw] = _dot(hv, w_ref[c0:c0 + cw, :], _NT)

    row = pl.BlockSpec((tm, d), lambda i: (i, 0))
    return _tc_call(
        body, name=name, grid=(t // tm,),
        in_specs=[row, pl.BlockSpec((1, d), lambda i: (0, 0)), pl.BlockSpec(w.shape, lambda i: (0, 0))],
        out_specs=[row, pl.BlockSpec((tm, n), lambda i: (i, 0))],
        out_shape=[jax.ShapeDtypeStruct((t, d), BF16), jax.ShapeDtypeStruct((t, n), F32)],
        compiler_params=_cp("parallel"))(x, gain, w)


def _mm_rms_bwd(pairs, x, gain, res, name):
    t, d = x.shape
    tm = _tile(t, (512, 256, 128))
    acts, weights, entries = [], [], []
    for a, w in pairs:
        k0 = 0
        for piece in (a if isinstance(a, tuple) else (a,)):
            assert piece.dtype == BF16 and w.dtype == BF16, (name, piece.dtype, w.dtype)
            entries.append((len(acts), len(weights), k0, piece.shape[1]))
            acts.append(piece)
            k0 += piece.shape[1]
        assert k0 == w.shape[0], (name, k0, w.shape)
        weights.append(w)
    n_acts = len(acts)

    def body(*refs):
        x_ref, g_ref, res_ref, dx_ref, dxb_ref, dg_ref, dn_ref = refs[n_acts + len(weights):]
        for c0, cw in _chunks(d, 512):
            acc = None
            for ai, wi, k0, kw in entries:
                part = _dot(refs[ai][...], refs[n_acts + wi][k0:k0 + kw, c0:c0 + cw], _NN)
                acc = part if acc is None else acc + part
            dn_ref[:, c0:c0 + cw] = acc
        xv = x_ref[...]
        r = lax.rsqrt(jnp.mean(xv * xv, axis=1, keepdims=True) + EPS)
        xh = xv * r
        dnv = dn_ref[...]
        dxh = dnv * g_ref[...]
        dx = res_ref[...] + r * (dxh - xh * jnp.mean(dxh * xh, axis=1, keepdims=True))
        dx_ref[...] = dx
        dxb_ref[...] = dx.astype(BF16)

        @pl.when(pl.program_id(0) == 0)
        def _():
            dg_ref[...] = jnp.zeros_like(dg_ref)

        dg_ref[...] += jnp.sum(dnv * xh, axis=0, keepdims=True)

    in_specs = [pl.BlockSpec((tm, a.shape[1]), lambda i: (i, 0)) for a in acts]
    in_specs += [pl.BlockSpec(w.shape, lambda i: (0, 0)) for w in weights]
    row = pl.BlockSpec((tm, d), lambda i: (i, 0))
    one = pl.BlockSpec((1, d), lambda i: (0, 0))
    return _tc_call(
        body, name=name, grid=(t // tm,), in_specs=in_specs + [row, one, row], out_specs=[row, row, one],
        out_shape=[jax.ShapeDtypeStruct((t, d), F32), jax.ShapeDtypeStruct((t, d), BF16),
                   jax.ShapeDtypeStruct((1, d), F32)],
        scratch_shapes=[pltpu.VMEM((tm, d), F32)],
        compiler_params=_cp("arbitrary"))(*acts, *weights, x, gain, res)


def _ffn_up(x, gain, wg_t, wu_t, name):
    t, d = x.shape
    f = wg_t.shape[0]
    tm = _tile(t, (512, 256, 128))

    def body(x_ref, g_ref, wg_ref, wu_ref, n_ref, a_ref, b_ref, h_ref):
        xv = x_ref[...]
        r = lax.rsqrt(jnp.mean(xv * xv, axis=1, keepdims=True) + EPS)
        nv = (xv * r * g_ref[...]).astype(BF16)
        n_ref[...] = nv
        for c0, cw in _chunks(f, 256):
            a = _dot(nv, wg_ref[c0:c0 + cw, :], _NT)
            b = _dot(nv, wu_ref[c0:c0 + cw, :], _NT)
            a_ref[:, c0:c0 + cw] = a.astype(BF16)
            b_ref[:, c0:c0 + cw] = b.astype(BF16)
            h_ref[:, c0:c0 + cw] = (a * _sigmoid(a) * b).astype(BF16)

    w_spec = pl.BlockSpec((f, d), lambda i: (0, 0))
    x_spec = pl.BlockSpec((tm, d), lambda i: (i, 0))
    o_spec = pl.BlockSpec((tm, f), lambda i: (i, 0))
    o_shape = jax.ShapeDtypeStruct((t, f), BF16)
    return _tc_call(
        body, name=name, grid=(t // tm,), in_specs=[x_spec, pl.BlockSpec((1, d), lambda i: (0, 0)), w_spec, w_spec],
        out_specs=[x_spec] + [o_spec] * 3, out_shape=[jax.ShapeDtypeStruct((t, d), BF16)] + [o_shape] * 3,
        compiler_params=_cp("parallel"))(x, gain, wg_t, wu_t)


def _ffn_down_bwd(dxb, wd, a, b, name):
    t, d = dxb.shape
    f = wd.shape[0]
    tm = _tile(t, (512, 256, 128))

    def body(dx_ref, wd_ref, a_ref, b_ref, da_ref, db_ref):
        dxv = dx_ref[...]
        for c0, cw in _chunks(f, 256):
            dh = 0.5 * _dot(dxv, wd_ref[c0:c0 + cw, :], _NT)
            av = a_ref[:, c0:c0 + cw].astype(F32)
            bv = b_ref[:, c0:c0 + cw].astype(F32)
            sg = _sigmoid(av)
            da_ref[:, c0:c0 + cw] = (dh * bv * (sg * (1.0 + av * (1.0 - sg)))).astype(BF16)
            db_ref[:, c0:c0 + cw] = (dh * (av * sg)).astype(BF16)

    o_spec = pl.BlockSpec((tm, f), lambda i: (i, 0))
    o_shape = jax.ShapeDtypeStruct((t, f), BF16)
    return _tc_call(
        body, name=name, grid=(t // tm,),
        in_specs=[pl.BlockSpec((tm, d), lambda i: (i, 0)), pl.BlockSpec((f, d), lambda i: (0, 0)), o_spec, o_spec],
        out_specs=[o_spec] * 2, out_shape=[o_shape] * 2, compiler_params=_cp("parallel"))(dxb, wd, a, b)


def _loss_head(y, target):
    t, d = y.shape
    tm = _tile(t, (512, 256, 128))

    def body(y_ref, t_ref, dy_ref, dyb_ref, loss_ref, acc_ref):
        i = pl.program_id(0)
        e = y_ref[...] - t_ref[...]
        dy = e * (1.0 / d)
        dy_ref[...] = dy
        dyb_ref[...] = dy.astype(BF16)

        @pl.when(i == 0)
        def _():
            acc_ref[...] = jnp.zeros_like(acc_ref)

        acc_ref[...] += jnp.sum(e * e, axis=0, keepdims=True)

        @pl.when(i == pl.num_programs(0) - 1)
        def _():
            loss_ref[...] = jnp.sum(acc_ref[...], axis=1, keepdims=True) * (0.5 / d)

    row = pl.BlockSpec((tm, d), lambda i: (i, 0))
    return _tc_call(
        body, name="loss_head", grid=(t // tm,), in_specs=[row, row],
        out_specs=[row, row, pl.BlockSpec((1, 1), lambda i: (0, 0))],
        out_shape=[jax.ShapeDtypeStruct((t, d), F32), jax.ShapeDtypeStruct((t, d), BF16),
                   jax.ShapeDtypeStruct((1, 1), F32)],
        scratch_shapes=[pltpu.VMEM((1, d), F32)], compiler_params=_cp("arbitrary"))(y, target)


def _merge_fwd(oa, ob, proj, wa_t, wb_t, name):
    t = oa.shape[0]
    d = wa_t.shape[0]
    tm = _tile(t, (512, 256, 128))

    def body(oa_ref, ob_ref, ga_ref, gb_ref, wa_ref, wb_ref, mg_ref, ya_ref, yb_ref):
        oav, obv = oa_ref[...], ob_ref[...]
        for c0, cw in _chunks(d, 512):
            cs = slice(c0, c0 + cw)
            ya = _dot(oav, wa_ref[cs, :], _NT)
            yb = _dot(obv, wb_ref[cs, :], _NT)
            mg_ref[:, cs] = (_sigmoid(ga_ref[:, cs]) * ya + _sigmoid(gb_ref[:, cs]) * yb).astype(BF16)
            ya_ref[:, cs] = ya.astype(BF16)
            yb_ref[:, cs] = yb.astype(BF16)

    o_spec = pl.BlockSpec((tm, d), lambda i: (i, 0))
    o_shape = jax.ShapeDtypeStruct((t, d), BF16)
    return _tc_call(
        body, name=name, grid=(t // tm,),
        in_specs=[pl.BlockSpec((tm, oa.shape[1]), lambda i: (i, 0)), pl.BlockSpec((tm, ob.shape[1]), lambda i: (i, 0)),
                  pl.BlockSpec((tm, d), lambda i: (i, P_GA // d)), pl.BlockSpec((tm, d), lambda i: (i, P_GB // d)),
                  pl.BlockSpec(wa_t.shape, lambda i: (0, 0)), pl.BlockSpec(wb_t.shape, lambda i: (0, 0))],
        out_specs=[o_spec] * 3, out_shape=[o_shape] * 3,
        compiler_params=_cp("parallel"))(oa, ob, proj, proj, wa_t, wb_t)


def _merge_bwd(dxb, wo, proj, ya, yb, wa_t, wb_t, name):
    t, d = dxb.shape
    tm = _tile(t, (512, 256, 128))
    wa_in, wb_in = wa_t.shape[1], wb_t.shape[1]

    def body(dx_ref, wo_ref, ga_ref, gb_ref, ya_ref, yb_ref, wa_ref, wb_ref,
             dya_ref, dyb_ref, dga_ref, dgb_ref, doa_ref, dob_ref):
        dxv = dx_ref[...]
        doa = jnp.zeros((tm, wa_in), F32)
        dob = jnp.zeros((tm, wb_in), F32)
        for c0, cw in _chunks(d, 512):
            cs = slice(c0, c0 + cw)
            dm = _dot(dxv, wo_ref[cs, :], _NT)
            sa = _sigmoid(ga_ref[:, cs])
            sb = _sigmoid(gb_ref[:, cs])
            dya = (dm * sa).astype(BF16)
            dyb = (dm * sb).astype(BF16)
            dya_ref[:, cs] = dya
            dyb_ref[:, cs] = dyb
            dga_ref[:, cs] = (dm * ya_ref[:, cs].astype(F32) * (sa * (1.0 - sa))).astype(BF16)
            dgb_ref[:, cs] = (dm * yb_ref[:, cs].astype(F32) * (sb * (1.0 - sb))).astype(BF16)
            doa = doa + _dot(dya, wa_ref[cs, :], _NN)
            dob = dob + _dot(dyb, wb_ref[cs, :], _NN)
        doa_ref[...] = doa.astype(BF16)
        dob_ref[...] = dob.astype(BF16)

    o_spec = pl.BlockSpec((tm, d), lambda i: (i, 0))
    o_shape = jax.ShapeDtypeStruct((t, d), BF16)
    return _tc_call(
        body, name=name, grid=(t // tm,),
        in_specs=[o_spec, pl.BlockSpec((d, d), lambda i: (0, 0)),
                  pl.BlockSpec((tm, d), lambda i: (i, P_GA // d)), pl.BlockSpec((tm, d), lambda i: (i, P_GB // d)),
                  o_spec, o_spec, pl.BlockSpec(wa_t.shape, lambda i: (0, 0)), pl.BlockSpec(wb_t.shape, lambda i: (0, 0))],
        out_specs=[o_spec] * 4 + [pl.BlockSpec((tm, wa_in), lambda i: (i, 0)), pl.BlockSpec((tm, wb_in), lambda i: (i, 0))],
        out_shape=[o_shape] * 4 + [jax.ShapeDtypeStruct((t, wa_in), BF16), jax.ShapeDtypeStruct((t, wb_in), BF16)],
        compiler_params=_cp("parallel"))(dxb, wo, proj, proj, ya, yb, wa_t, wb_t)


def _swa_common(has_prev, pq, pk):
    dist = (pq - pk).astype(F32)
    row = lax.broadcasted_iota(jnp.int32, (BLOCK, 2 * BLOCK), 0)
    col = lax.broadcasted_iota(jnp.int32, (BLOCK, 2 * BLOCK), 1)
    diff = row + BLOCK - col
    valid = (diff >= 0) & (diff < BLOCK) & (has_prev | (col >= BLOCK))
    return jnp.concatenate([dist] * N_HEADS_A, axis=0), jnp.concatenate([valid] * N_HEADS_A, axis=0)


def _half_sum(x, lo):
    s_lo = jnp.sum(jnp.where(lo, x, 0.0), axis=1, keepdims=True)
    s_hi = jnp.sum(jnp.where(lo, 0.0, x), axis=1, keepdims=True)
    return jnp.where(lo, s_lo, s_hi)


def _norm2(x, gain2, lo):
    r = lax.rsqrt(_half_sum(x * x, lo) * (1.0 / HEAD_DIM_A) + EPS)
    xh = x * r
    return xh * gain2, xh, r


def _norm2_bwd(d, xh, r, gain2, lo):
    dxh = d * gain2
    return r * (dxh - xh * (_half_sum(dxh * xh, lo) * (1.0 / HEAD_DIM_A)))


def _swa_stack(tiles, lo):
    zero = jnp.zeros_like(tiles[0])
    return jnp.concatenate([jnp.where(lo, t, zero) for t in tiles] + [jnp.where(lo, zero, t) for t in tiles], axis=0)


def _swa_unstack(x8, j, lo):
    return jnp.where(lo, x8[j * BLOCK:(j + 1) * BLOCK], x8[(GROUP_A + j) * BLOCK:(GROUP_A + j + 1) * BLOCK])


def _swa_head_columns(sk_ref):
    slope = jnp.concatenate([jnp.full((BLOCK, 1), 2.0 ** (-(h + 1)), F32) for h in range(N_HEADS_A)], axis=0)
    sink = jnp.concatenate([jnp.broadcast_to(sk_ref[:, h:h + 1], (BLOCK, 1)) for h in range(N_HEADS_A)], axis=0)
    return slope, sink


def _swa_blocks_per_step(s_len):
    nb = s_len // BLOCK
    return 4 if nb % 4 == 0 and nb >= 8 else 2 if nb % 2 == 0 else 1


def _swa_specs(s_len):
    nb, qb = s_len // BLOCK, _swa_blocks_per_step(s_len)
    ns, rows = nb // qb, qb * BLOCK

    def step(b, j):
        return b * ns + j

    def prev(b, j):
        return b * nb + jnp.maximum(qb * j - 1, 0)

    q_spec = pl.BlockSpec((rows, WIDTH_A), lambda b, j: (step(b, j), P_QA // WIDTH_A))
    kc_spec = pl.BlockSpec((rows, KV_A), lambda b, j: (step(b, j), P_KA // KV_A))
    kp_spec = pl.BlockSpec((BLOCK, KV_A), lambda b, j: (prev(b, j), P_KA // KV_A))
    vc_spec = pl.BlockSpec((rows, KV_A), lambda b, j: (step(b, j), P_VA // KV_A))
    vp_spec = pl.BlockSpec((BLOCK, KV_A), lambda b, j: (prev(b, j), P_VA // KV_A))
    pq_spec = pl.BlockSpec((rows, 1), lambda b, j: (step(b, j), 0))
    pkc_spec = pl.BlockSpec((qb, 1, BLOCK), lambda b, j: (step(b, j), 0, 0))
    pkp_spec = pl.BlockSpec((1, 1, BLOCK), lambda b, j: (prev(b, j), 0, 0))
    return qb, ns, step, [q_spec, kc_spec, kp_spec, vc_spec, vp_spec, pq_spec, pkc_spec, pkp_spec]


def _swa_stage_probs(qb, q_ref, kc_ref, kp_ref, vc_ref, vp_ref, pq_ref, pkc_ref, pkp_ref, qg_ref, kg_ref, sk_ref, lo):
    first = pl.program_id(1) * qb
    kk_all = _norm2(jnp.concatenate([kp_ref[...], kc_ref[...]], axis=0), kg_ref[...], lo)[0].astype(BF16)
    vv_all = jnp.concatenate([vp_ref[...], vc_ref[...]], axis=0).astype(BF16)
    pk_all = jnp.concatenate([pkp_ref[0]] + [pkc_ref[s] for s in range(qb)], axis=1)
    rows = [slice(s * BLOCK, (s + 1) * BLOCK) for s in range(qb)]
    keys = [slice(s * BLOCK, (s + 2) * BLOCK) for s in range(qb)]
    masks = [_swa_common(first + s > 0, pq_ref[rows[s], :], pk_all[:, keys[s]]) for s in range(qb)]
    qs = [[_norm2(q_ref[r, j * HEAD_PAD:(j + 1) * HEAD_PAD], qg_ref[...], lo) for j in range(GROUP_A)] for r in rows]
    q8 = [_swa_stack([q[0] for q in tiles], lo).astype(BF16) for tiles in qs]
    kk = [kk_all[ks] for ks in keys]
    vv = [vv_all[ks] for ks in keys]
    slope, sink = _swa_head_columns(sk_ref)
    s = [_dot(q8[b], kk[b], _NT) * (HEAD_DIM_A ** -0.5) - slope * masks[b][0] for b in range(qb)]
    s = [jnp.where(masks[b][1], s[b], NEG) for b in range(qb)]
    m = [jnp.maximum(jnp.max(x, axis=1, keepdims=True), sink) for x in s]
    e = [jnp.exp(x - mx) for x, mx in zip(s, m)]
    es = [jnp.exp(sink - mx) for mx in m]
    inv = [1.0 / (jnp.sum(x, axis=1, keepdims=True) + y) for x, y in zip(e, es)]
    p = [x * i for x, i in zip(e, inv)]
    ps = [y * i for y, i in zip(es, inv)]
    return rows, qs, q8, kk, vv, p, ps


def _swa_fwd(proj, pos_col, pos_row, qg2, kg2, sinks, n_batch, s_len, name):
    t = proj.shape[0]
    qb, ns, step, specs = _swa_specs(s_len)
    small = pl.BlockSpec((1, HEAD_PAD), lambda b, j: (0, 0))

    def body(q_ref, kc_ref, kp_ref, vc_ref, vp_ref, pq_ref, pkc_ref, pkp_ref, qg_ref, kg_ref, sk_ref, o_ref):
        lo = lax.broadcasted_iota(jnp.int32, (1, HEAD_PAD), 1) < HEAD_DIM_A
        rows, _, _, _, vv, p, _ = _swa_stage_probs(qb, q_ref, kc_ref, kp_ref, vc_ref, vp_ref, pq_ref, pkc_ref, pkp_ref,
                                                   qg_ref, kg_ref, sk_ref, lo)
        o8 = [_dot(p[b].astype(BF16), vv[b], _NN) for b in range(qb)]
        for b in range(qb):
            for j in range(GROUP_A):
                o_ref[rows[b], j * HEAD_PAD:(j + 1) * HEAD_PAD] = _swa_unstack(o8[b], j, lo).astype(BF16)

    return _tc_call(
        body, name=name, grid=(n_batch, ns), in_specs=specs + [small, small, small],
        out_specs=pl.BlockSpec((qb * BLOCK, WIDTH_A), lambda b, j: (step(b, j), 0)),
        out_shape=jax.ShapeDtypeStruct((t, WIDTH_A), BF16),
        compiler_params=_cp("parallel", "parallel"))(proj, proj, proj, proj, proj, pos_col, pos_row, pos_row,
                                                     qg2, kg2, sinks)


def _swa_bwd(proj, pos_col, pos_row, qg2, kg2, sinks, do, n_batch, s_len, name):
    t = proj.shape[0]
    qb, ns, step, specs = _swa_specs(s_len)
    small = pl.BlockSpec((1, HEAD_PAD), lambda b, j: (0, 0))
    scale = HEAD_DIM_A ** -0.5

    def body(q_ref, kc_ref, kp_ref, vc_ref, vp_ref, pq_ref, pkc_ref, pkp_ref, qg_ref, kg_ref, sk_ref, do_ref,
             dq_ref, dkc_ref, dkp_ref, dvc_ref, dvp_ref, dqg_ref, dsk_ref):
        @pl.when((pl.program_id(0) == 0) & (pl.program_id(1) == 0))
        def _():
            dqg_ref[...] = jnp.zeros_like(dqg_ref)
            dsk_ref[...] = jnp.zeros_like(dsk_ref)

        lane = lax.broadcasted_iota(jnp.int32, (1, HEAD_PAD), 1)
        lo = lane < HEAD_DIM_A
        rows, qs, q8, kk, vv, p, ps = _swa_stage_probs(qb, q_ref, kc_ref, kp_ref, vc_ref, vp_ref, pq_ref, pkc_ref,
                                                       pkp_ref, qg_ref, kg_ref, sk_ref, lo)
        blocks = range(qb)
        do8 = [_swa_stack([do_ref[r, j * HEAD_PAD:(j + 1) * HEAD_PAD] for j in range(GROUP_A)], lo) for r in rows]
        dp = [_dot(do8[b], vv[b], _NT) for b in blocks]
        delta = [jnp.sum(p[b] * dp[b], axis=1, keepdims=True) for b in blocks]
        ds = [(p[b] * (dp[b] - delta[b]) * scale).astype(BF16) for b in blocks]
        dsink = [ps[b] * delta[b] for b in blocks]
        dvv = [_dot(p[b].astype(BF16), do8[b], _TN) for b in blocks]
        dkk = [_dot(ds[b], q8[b], _TN) for b in blocks]
        dq8 = [_dot(ds[b], kk[b], _NN) for b in blocks]
        dsk = jnp.zeros((1, HEAD_PAD), F32)
        dqg = jnp.zeros((1, HEAD_PAD), F32)
        for b in blocks:
            for h in range(N_HEADS_A):
                dsk = dsk + jnp.where(lane == h, -jnp.sum(dsink[b][h * BLOCK:(h + 1) * BLOCK]), 0.0)
            for j in range(GROUP_A):
                _, xh, r = qs[b][j]
                dqn = _swa_unstack(dq8[b], j, lo)
                dqg = dqg + jnp.sum(dqn * xh, axis=0, keepdims=True)
                dq_ref[rows[b], j * HEAD_PAD:(j + 1) * HEAD_PAD] = _norm2_bwd(dqn, xh, r, qg_ref[...], lo).astype(BF16)
            dkp_ref[rows[b], :] = dkk[b][:BLOCK]
            dkc_ref[rows[b], :] = dkk[b][BLOCK:]
            dvp_ref[rows[b], :] = dvv[b][:BLOCK]
            dvc_ref[rows[b], :] = dvv[b][BLOCK:]
        dqg_ref[...] += dqg
        dsk_ref[...] += dsk

    kv_out = pl.BlockSpec((qb * BLOCK, KV_A), lambda b, j: (step(b, j), 0))
    kv_shape = jax.ShapeDtypeStruct((t, KV_A), F32)
    wide = pl.BlockSpec((qb * BLOCK, WIDTH_A), lambda b, j: (step(b, j), 0))
    return _tc_call(
        body, name=name, grid=(n_batch, ns), in_specs=specs + [small, small, small, wide],
        out_specs=[wide, kv_out, kv_out, kv_out, kv_out, small, small],
        out_shape=[jax.ShapeDtypeStruct((t, WIDTH_A), BF16), kv_shape, kv_shape, kv_shape, kv_shape,
                   jax.ShapeDtypeStruct((1, HEAD_PAD), F32), jax.ShapeDtypeStruct((1, HEAD_PAD), F32)],
        compiler_params=_cp("arbitrary", "arbitrary"))(proj, proj, proj, proj, proj, pos_col, pos_row, pos_row,
                                                       qg2, kg2, sinks, do)


def _swa_kv_bwd(proj, kg2, dkc, dkp, dvc, dvp, n_batch, s_len, name):
    t = proj.shape[0]
    nb, qb = s_len // BLOCK, _swa_blocks_per_step(s_len)
    ns, rows = nb // qb, qb * BLOCK

    def step(b, j):
        return b * ns + j

    def edge(b, j):
        return b * nb + jnp.minimum(qb * (j + 1), nb - 1)

    def body(k_ref, kg_ref, dkc_ref, dkp_ref, dkp_edge, dvc_ref, dvp_ref, dvp_edge, dk_ref, dv_ref, dkg_ref):
        @pl.when((pl.program_id(0) == 0) & (pl.program_id(1) == 0))
        def _():
            dkg_ref[...] = jnp.zeros_like(dkg_ref)

        lo = lax.broadcasted_iota(jnp.int32, (1, HEAD_PAD), 1) < HEAD_DIM_A
        has_next = (pl.program_id(1) < ns - 1).astype(F32)

        def from_next(p_ref, edge_ref):
            return jnp.concatenate([p_ref[BLOCK:, :], has_next * edge_ref[...]], axis=0)

        dkn = dkc_ref[...] + from_next(dkp_ref, dkp_edge)
        dv_ref[...] = (dvc_ref[...] + from_next(dvp_ref, dvp_edge)).astype(BF16)
        _, xh, r = _norm2(k_ref[...], kg_ref[...], lo)
        dkg_ref[...] += jnp.sum(dkn * xh, axis=0, keepdims=True)
        dk_ref[...] = _norm2_bwd(dkn, xh, r, kg_ref[...], lo).astype(BF16)

    cur = pl.BlockSpec((rows, KV_A), lambda b, j: (step(b, j), 0))
    nxt = pl.BlockSpec((BLOCK, KV_A), lambda b, j: (edge(b, j), 0))
    small = pl.BlockSpec((1, HEAD_PAD), lambda b, j: (0, 0))
    return _tc_call(
        body, name=name, grid=(n_batch, ns),
        in_specs=[pl.BlockSpec((rows, KV_A), lambda b, j: (step(b, j), P_KA // KV_A)), small, cur, cur, nxt, cur, cur,
                  nxt],
        out_specs=[cur, cur, small],
        out_shape=[jax.ShapeDtypeStruct((t, KV_A), BF16), jax.ShapeDtypeStruct((t, KV_A), BF16),
                   jax.ShapeDtypeStruct((1, HEAD_PAD), F32)],
        compiler_params=_cp("arbitrary", "arbitrary"))(proj, kg2, dkc, dkp, dkp, dvc, dvp, dvp)


def _lora_norm(x, gain):
    r = lax.rsqrt(jnp.mean(x * x, axis=1, keepdims=True) + EPS)
    xh = x * r
    return xh * gain, xh, r


def _mla_in_specs(tm):
    row = lambda w, off: pl.BlockSpec((tm, w), lambda i: (i, off // w))
    one = lambda w: pl.BlockSpec((1, w), lambda i: (0, 0))
    full = lambda r, c: pl.BlockSpec((r, c), lambda i: (0, 0))
    tab = pl.BlockSpec((tm, HEAD_PAD), lambda i: (i, 0))
    return [row(Q_LORA, P_CQ), row(KV_LORA, P_CKV), row(HEAD_PAD, P_KR), tab, tab, tab,
            one(Q_LORA), one(KV_LORA), one(HEAD_PAD), one(HEAD_PAD),
            full(WIDTH_BP, Q_LORA), full(WIDTH_BP, KV_LORA), full(WIDTH_BP, KV_LORA)]


def _mla_pre(proj, tabs, gq, gkv, gqn, gkn, wuq, wk, wv, name):
    t = proj.shape[0]
    tm = _tile(t, (512, 256, 128))

    def body(cq_ref, ckv_ref, kr_ref, c_ref, sm_ref, sp_ref, gq_ref, gkv_ref, gqn_ref, gkn_ref,
             wuq_ref, wk_ref, wv_ref, q_ref, k_ref, v_ref):
        cqn = _lora_norm(cq_ref[...], gq_ref[...])[0].astype(BF16)
        ckvn = _lora_norm(ckv_ref[...], gkv_ref[...])[0].astype(BF16)
        q_raw = _dot(cqn, wuq_ref[...], _NT)
        k_raw = _dot(ckvn, wk_ref[...], _NT)
        v_ref[...] = _dot(ckvn, wv_ref[...], _NT).astype(BF16)
        kr = pltpu.roll(kr_ref[...], NOPE, 1)
        c, sm, sp = c_ref[...], sm_ref[...], sp_ref[...]
        heads = [slice(h * HEAD_PAD, (h + 1) * HEAD_PAD) for h in range(N_HEADS_B)]
        raw = [q_raw[:, hs] for hs in heads] + [k_raw[:, hs] + kr for hs in heads]
        gains = [gqn_ref[...]] * N_HEADS_B + [gkn_ref[...]] * N_HEADS_B
        sq = [jnp.sum(x * x, axis=1, keepdims=True) for x in raw]
        normed = [x * lax.rsqrt(s * (1.0 / QK_B) + EPS) * g for x, s, g in zip(raw, sq, gains)]
        up = [pltpu.roll(x, HEAD_PAD - ROPE // 2, 1) for x in normed]
        down = [pltpu.roll(x, ROPE // 2, 1) for x in normed]
        roped = [(x * c + u * sm + d * sp).astype(BF16) for x, u, d in zip(normed, up, down)]
        for h, hs in enumerate(heads):
            q_ref[:, hs] = roped[h]
            k_ref[:, hs] = roped[N_HEADS_B + h]

    o_spec = pl.BlockSpec((tm, WIDTH_BP), lambda i: (i, 0))
    o_shape = jax.ShapeDtypeStruct((t, WIDTH_BP), BF16)
    return _tc_call(
        body, name=name, grid=(t // tm,), in_specs=_mla_in_specs(tm), out_specs=[o_spec] * 3,
        out_shape=[o_shape] * 3, compiler_params=_cp("parallel"))(
            proj, proj, proj, *tabs, gq, gkv, gqn, gkn, wuq, wk, wv)


def _mla_pre_bwd(proj, tabs, gq, gkv, gqn, gkn, wuq, wk, wv, dq, dk, dv, name):
    t = proj.shape[0]
    tm = _tile(t, (512, 256, 128))

    def body(cq_ref, ckv_ref, kr_ref, c_ref, sm_ref, sp_ref, gq_ref, gkv_ref, gqn_ref, gkn_ref,
             wuq_ref, wk_ref, wv_ref, dq_ref, dk_ref, dv_ref,
             dcq_ref, dckv_ref, dkr_ref, dwuq_ref, dwk_ref, dwv_ref, dgq_ref, dgkv_ref, dgqn_ref, dgkn_ref,
             dqraw_ref, dkraw_ref):
        @pl.when(pl.program_id(0) == 0)
        def _():
            for r in (dwuq_ref, dwk_ref, dwv_ref, dgq_ref, dgkv_ref, dgqn_ref, dgkn_ref):
                r[...] = jnp.zeros_like(r)

        cqn_f, cq_xh, cq_r = _lora_norm(cq_ref[...], gq_ref[...])
        ckvn_f, ckv_xh, ckv_r = _lora_norm(ckv_ref[...], gkv_ref[...])
        cqn, ckvn = cqn_f.astype(BF16), ckvn_f.astype(BF16)
        q_raw = _dot(cqn, wuq_ref[...], _NT)
        k_raw = _dot(ckvn, wk_ref[...], _NT)
        kr = pltpu.roll(kr_ref[...], NOPE, 1)
        c, sm, sp = c_ref[...], sm_ref[...], sp_ref[...]
        heads = [slice(h * HEAD_PAD, (h + 1) * HEAD_PAD) for h in range(N_HEADS_B)]
        raw = [q_raw[:, hs] for hs in heads] + [k_raw[:, hs] + kr for hs in heads]
        d_out = [dq_ref[:, hs] for hs in heads] + [dk_ref[:, hs] for hs in heads]
        gains = [gqn_ref[...]] * N_HEADS_B + [gkn_ref[...]] * N_HEADS_B
        sq = [jnp.sum(x * x, axis=1, keepdims=True) for x in raw]
        rinv = [lax.rsqrt(s * (1.0 / QK_B) + EPS) for s in sq]
        xhat = [x * r for x, r in zip(raw, rinv)]
        down = [pltpu.roll(d * sm, ROPE // 2, 1) for d in d_out]
        up = [pltpu.roll(d * sp, HEAD_PAD - ROPE // 2, 1) for d in d_out]
        dn = [d * c + a + b for d, a, b in zip(d_out, down, up)]
        dgain = [jnp.sum(d * xh, axis=0, keepdims=True) for d, xh in zip(dn, xhat)]
        dxh = [d * g for d, g in zip(dn, gains)]
        inner = [jnp.sum(d * xh, axis=1, keepdims=True) * (1.0 / QK_B) for d, xh in zip(dxh, xhat)]
        d_raw = [r * (d - xh * s) for r, d, xh, s in zip(rinv, dxh, xhat, inner)]
        for h, hs in enumerate(heads):
            dqraw_ref[:, hs] = d_raw[h].astype(BF16)
            dkraw_ref[:, hs] = d_raw[N_HEADS_B + h].astype(BF16)
        dkr = sum(d_raw[N_HEADS_B + 1:], d_raw[N_HEADS_B])
        dgqn_ref[...] += sum(dgain[1:N_HEADS_B], dgain[0])
        dgkn_ref[...] += sum(dgain[N_HEADS_B + 1:], dgain[N_HEADS_B])
        lane = lax.broadcasted_iota(jnp.int32, (tm, HEAD_PAD), 1)
        dkr_ref[...] = jnp.where(lane < ROPE, pltpu.roll(dkr, HEAD_PAD - NOPE, 1), 0.0).astype(BF16)
        dqraw = dqraw_ref[...]
        dkraw = dkraw_ref[...]
        dvb = dv_ref[...].astype(BF16)
        dwuq_ref[...] += _dot(dqraw, cqn, _TN)
        dwk_ref[...] += _dot(dkraw, ckvn, _TN)
        dwv_ref[...] += _dot(dvb, ckvn, _TN)
        dcqn = _dot(dqraw, wuq_ref[...], _NN)
        dckvn = _dot(dkraw, wk_ref[...], _NN) + _dot(dvb, wv_ref[...], _NN)
        dgq_ref[...] += jnp.sum(dcqn * cq_xh, axis=0, keepdims=True)
        dgkv_ref[...] += jnp.sum(dckvn * ckv_xh, axis=0, keepdims=True)
        dxh = dcqn * gq_ref[...]
        dcq_ref[...] = (cq_r * (dxh - cq_xh * jnp.mean(dxh * cq_xh, axis=1, keepdims=True))).astype(BF16)
        dxh = dckvn * gkv_ref[...]
        dckv_ref[...] = (ckv_r * (dxh - ckv_xh * jnp.mean(dxh * ckv_xh, axis=1, keepdims=True))).astype(BF16)

    wide = pl.BlockSpec((tm, WIDTH_BP), lambda i: (i, 0))
    row = lambda w: pl.BlockSpec((tm, w), lambda i: (i, 0))
    full = lambda r, c: pl.BlockSpec((r, c), lambda i: (0, 0))
    return _tc_call(
        body, name=name, grid=(t // tm,), in_specs=_mla_in_specs(tm) + [wide, wide, wide],
        out_specs=[row(Q_LORA), row(KV_LORA), row(HEAD_PAD), full(WIDTH_BP, Q_LORA), full(WIDTH_BP, KV_LORA),
                   full(WIDTH_BP, KV_LORA), full(1, Q_LORA), full(1, KV_LORA), full(1, HEAD_PAD), full(1, HEAD_PAD)],
        out_shape=[jax.ShapeDtypeStruct((t, Q_LORA), BF16), jax.ShapeDtypeStruct((t, KV_LORA), BF16),
                   jax.ShapeDtypeStruct((t, HEAD_PAD), BF16), jax.ShapeDtypeStruct((WIDTH_BP, Q_LORA), F32),
                   jax.ShapeDtypeStruct((WIDTH_BP, KV_LORA), F32), jax.ShapeDtypeStruct((WIDTH_BP, KV_LORA), F32),
                   jax.ShapeDtypeStruct((1, Q_LORA), F32), jax.ShapeDtypeStruct((1, KV_LORA), F32),
                   jax.ShapeDtypeStruct((1, HEAD_PAD), F32), jax.ShapeDtypeStruct((1, HEAD_PAD), F32)],
        scratch_shapes=[pltpu.VMEM((tm, WIDTH_BP), BF16), pltpu.VMEM((tm, WIDTH_BP), BF16)],
        compiler_params=_cp("arbitrary"))(proj, proj, proj, *tabs, gq, gkv, gqn, gkn, wuq, wk, wv, dq, dk, dv)


def _mla_flash_specs(s_len):
    bh_spec = pl.BlockSpec((s_len, HEAD_PAD), lambda b, h: (b, h))
    lse_spec = pl.BlockSpec((1, s_len, 1), lambda b, h: (b * N_HEADS_B + h, 0, 0))
    return bh_spec, lse_spec


def _diag_mask(s):
    row = lax.broadcasted_iota(jnp.int32, s.shape, 0)
    col = lax.broadcasted_iota(jnp.int32, s.shape, 1)
    return jnp.where(row >= col, s, NEG)


def _mla_flash_fwd(q, k, v, n_batch, s_len, name):
    t = q.shape[0]
    tq = _tile(s_len, (256, 128))
    bh_spec, lse_spec = _mla_flash_specs(s_len)
    c = (QK_B ** -0.5) * LOG2E

    def body(q_ref, k_ref, v_ref, o_ref, lse_ref):
        nq = s_len // tq
        rows = [slice(i * tq, (i + 1) * tq) for i in range(nq)]
        below = [slice(0, i * tq) for i in range(nq)]
        qs = [q_ref[r, :] for r in rows]
        sd = [_diag_mask(_dot(qs[i], k_ref[rows[i], :], _NT)) for i in range(nq)]
        sb = [None] + [_dot(qs[i], k_ref[below[i], :], _NT) for i in range(1, nq)]
        m = [jnp.max(s, axis=1, keepdims=True) for s in sd]
        m = [m[0]] + [jnp.maximum(m[i], jnp.max(sb[i], axis=1, keepdims=True)) for i in range(1, nq)]
        pd = [jnp.exp2((sd[i] - m[i]) * c) for i in range(nq)]
        pb = [None] + [jnp.exp2((sb[i] - m[i]) * c) for i in range(1, nq)]
        l = [jnp.sum(p, axis=1, keepdims=True) for p in pd]
        l = [l[0]] + [l[i] + jnp.sum(pb[i], axis=1, keepdims=True) for i in range(1, nq)]
        acc = [_dot(pd[i].astype(BF16), v_ref[rows[i], :], _NN) for i in range(nq)]
        acc = [acc[0]] + [acc[i] + _dot(pb[i].astype(BF16), v_ref[below[i], :], _NN) for i in range(1, nq)]
        for i in range(nq):
            o_ref[rows[i], :] = (acc[i] * (1.0 / l[i])).astype(BF16)
            lse_ref[0, rows[i], :] = m[i] * c + jnp.log2(l[i])

    return _tc_call(
        body, name=name, grid=(n_batch, N_HEADS_B), in_specs=[bh_spec, bh_spec, bh_spec],
        out_specs=[bh_spec, lse_spec],
        out_shape=[jax.ShapeDtypeStruct((t, WIDTH_BP), BF16),
                   jax.ShapeDtypeStruct((n_batch * N_HEADS_B, s_len, 1), F32)],
        compiler_params=_cp("parallel", "parallel"))(q, k, v)


def _mla_flash_bwd(q, k, v, o, do, lse2, n_batch, s_len, name):
    t = q.shape[0]
    tq = _tile(s_len, (256, 128))
    bh_spec, lse_spec = _mla_flash_specs(s_len)
    scale = QK_B ** -0.5
    c = scale * LOG2E

    def body(q_ref, k_ref, v_ref, o_ref, do_ref, lse_ref, dq_ref, dk_ref, dv_ref):
        nq = s_len // tq
        rows = [slice(i * tq, (i + 1) * tq) for i in range(nq)]
        below = [slice(0, i * tq) for i in range(nq)]
        qs = [q_ref[r, :] for r in rows]
        dos = [do_ref[r, :] for r in rows]
        lse = [lse_ref[0, r, :] for r in rows]
        delta = [jnp.sum(dos[i].astype(F32) * o_ref[rows[i], :].astype(F32), axis=1, keepdims=True)
                 for i in range(nq)]

        def probs_and_ds(i, ks, diag):
            s = _dot(qs[i], k_ref[ks, :], _NT)
            if diag:
                s = _diag_mask(s)
            p = jnp.exp2(s * c - lse[i])
            dp = _dot(dos[i], v_ref[ks, :], _NT)
            return p.astype(BF16), (p * (dp - delta[i]) * scale).astype(BF16)

        diag = [probs_and_ds(i, rows[i], True) for i in range(nq)]
        rest = [None] + [probs_and_ds(i, below[i], False) for i in range(1, nq)]
        for i in range(nq):
            dq = _dot(diag[i][1], k_ref[rows[i], :], _NN)
            if i:
                dq = dq + _dot(rest[i][1], k_ref[below[i], :], _NN)
            dq_ref[rows[i], :] = dq
        for j in range(nq):
            later = slice(j * tq, s_len)
            p_j = jnp.concatenate([diag[j][0]] + [rest[i][0][:, rows[j]] for i in range(j + 1, nq)], axis=0)
            ds_j = jnp.concatenate([diag[j][1]] + [rest[i][1][:, rows[j]] for i in range(j + 1, nq)], axis=0)
            dk_ref[rows[j], :] = _dot(ds_j, q_ref[later, :], _TN)
            dv_ref[rows[j], :] = _dot(p_j, do_ref[later, :], _TN)

    f32_wide = jax.ShapeDtypeStruct((t, WIDTH_BP), F32)
    return _tc_call(
        body, name=name, grid=(n_batch, N_HEADS_B),
        in_specs=[bh_spec, bh_spec, bh_spec, bh_spec, bh_spec, lse_spec],
        out_specs=[bh_spec, bh_spec, bh_spec], out_shape=[f32_wide] * 3,
        compiler_params=_cp("parallel", "parallel"))(q, k, v, o, do, lse2)


def _swa_heads(w, axis, order):
    heads = [lax.slice_in_dim(w, h * HEAD_DIM_A, (h + 1) * HEAD_DIM_A, axis=axis) for h in order]
    return jnp.concatenate(heads, axis=axis)


class _LayerWeights:
    def __init__(self, build):
        self._build, self._mats = build, {}

    def __getitem__(self, name):
        if name not in self._mats:
            self._mats.update(self._build(name))
        return self._mats[name]


PIECES_OF = {"mla_w_ukv": ("wk", "wv")}


def _store(name, w):
    t = w.T if dict(BIG)[name] else w
    if name == "mla_w_uq":
        return {name: jnp.pad(t, ((0, HEAD_PAD - QK_B), (0, 0)))}
    if name == "mla_w_ukv":
        pad = ((0, HEAD_PAD - NOPE), (0, 0))
        return {"wk": jnp.pad(t[:NOPE], pad), "wv": jnp.pad(t[NOPE:], pad)}
    if name == "w_branch_b":
        t3 = jnp.pad(t.reshape(t.shape[0], N_HEADS_B, V_B), ((0, 0), (0, 0), (0, HEAD_PAD - V_B)))
        return {name: t3.reshape(t.shape[0], WIDTH_BP)}
    if name == "w_branch_a":
        return {name: _swa_heads(t, 1, SWA_HEAD_ORDER)}
    return {name: t}


def _unstore(name, pieces):
    if name == "mla_w_uq":
        return pieces[name][:QK_B]
    if name == "mla_w_ukv":
        return jnp.concatenate([pieces["wk"][:NOPE], pieces["wv"][:V_B]], axis=0)
    if name == "w_branch_b":
        g = pieces[name]
        return g.reshape(g.shape[0], N_HEADS_B, HEAD_PAD)[:, :, :V_B].reshape(g.shape[0], WIDTH_B)
    if name == "w_branch_a":
        return _swa_heads(pieces[name], 1, SWA_HEAD_INVERSE)
    return pieces[name]


def _layer_mats(w):
    if "w_in" not in w:
        return dict(w)
    w_in = w["w_in"]
    o = [0]
    for n in (WIDTH_A, KV_A, KV_A, Q_LORA, KV_LORA, ROPE, D_MODEL, D_MODEL):
        o.append(o[-1] + n)
    qa, ka, va, cq, ckv, kr, ga, gb = (w_in[o[i]:o[i + 1]] for i in range(8))
    pad = jnp.zeros((PROJ_W - IN_WIDTH, w_in.shape[1]), w_in.dtype)
    out = dict(w)
    out["w_in"] = jnp.concatenate([ga, gb, _swa_heads(qa, 0, SWA_HEAD_ORDER), cq, ka, va, ckv, kr, pad], axis=0)
    return out


def _unlayer_w_in(d):
    ga, gb, qa, cq, ka, va, ckv, kr = (d[a:b] for a, b in (
        (P_GA, P_GA + D_MODEL), (P_GB, P_GB + D_MODEL), (P_QA, P_QA + WIDTH_A), (P_CQ, P_CQ + Q_LORA),
        (P_KA, P_KA + KV_A), (P_VA, P_VA + KV_A), (P_CKV, P_CKV + KV_LORA), (P_KR, P_KR + ROPE)))
    return jnp.concatenate([_swa_heads(qa, 0, SWA_HEAD_INVERSE), ka, va, cq, ckv, kr, ga, gb], axis=0)


def _pad_lanes(v, width):
    return jnp.pad(v.reshape(1, -1), ((0, 0), (0, width - v.shape[-1])))


def _rope_tables(positions):
    half = ROPE // 2
    inv_freq = ROPE_BASE ** (-jnp.arange(half, dtype=F32) / half)
    ang = positions.astype(F32).reshape(-1, 1) * inv_freq
    cos, sin = jnp.cos(ang), jnp.sin(ang)
    t = cos.shape[0]
    one, zero = jnp.ones((t, NOPE), F32), jnp.zeros((t, NOPE), F32)
    tail = jnp.zeros((t, HEAD_PAD - QK_B), F32)
    z16 = jnp.zeros((t, half), F32)
    c = jnp.concatenate([one, cos, cos, tail], axis=1)
    sm = jnp.concatenate([zero, -sin, z16, tail], axis=1)
    sp = jnp.concatenate([zero, z16, sin, tail], axis=1)
    return c, sm, sp


def _ffn_fwd(x, gain, wg_t, wu_t, wd, tag):
    n, a, b, hmid = _ffn_up(x, gain, wg_t, wu_t, f"{tag}_up")
    y = _mm([(hmid, wd)], "nn", F32, f"{tag}_down", residual=x, alpha=0.5)
    return y, (x, n, a, b, hmid)


def _ffn_bwd(dy, dyb, saved, gain, wg_t, wu_t, wd, tag, grads, names, hook):
    x, n, a, b, hmid = saved
    da, db = _ffn_down_bwd(dyb, wd, a, b, f"{tag}_down_bwd")
    grads[names[0]] = _mm([(da, n)], "tn", BF16, f"{tag}_dwg")
    grads[names[1]] = _mm([(db, n)], "tn", BF16, f"{tag}_dwu")
    hook("gu", grads[names[1]])
    dx, dxb, g_gain = _mm_rms_bwd([(da, wg_t), (db, wu_t)], x, gain, dy, f"{tag}_dn")
    hook("dn", dx)
    grads[names[2]] = _mm([(hmid, dyb)], "tn", BF16, f"{tag}_dwd", alpha=0.5)
    return dx, dxb, g_gain


def _fold_halves(d):
    return d[:, :HEAD_DIM_A] + d[:, HEAD_DIM_A:]


def _local_step(x, positions, target, layers, smalls, at=None):
    at = at or (lambda point, l, token, grads: None)
    _Order.tokens = ()
    n_batch, s_len, d = x.shape
    t = n_batch * s_len
    xt = x.reshape(t, d)
    tabs = _rope_tables(positions)
    pos_col = positions.reshape(t, 1)
    pos_row = positions.reshape(t // BLOCK, 1, BLOCK)
    saved = []
    get_layer = layers if callable(layers) else layers.__getitem__
    for l in range(len(smalls)):
        w, s = get_layer(l), smalls[l]
        g1, gm, g2 = (s[k].reshape(1, d) for k in ("ffn1_norm", "mix_norm", "ffn2_norm"))
        qg2, kg2 = (jnp.tile(s[k].reshape(1, -1), (1, 2)) for k in ("swa_q_norm", "swa_k_norm"))
        sinks = _pad_lanes(s["swa_sinks"], HEAD_PAD)
        gq, gkv = s["mla_q_lora_norm"].reshape(1, -1), s["mla_kv_lora_norm"].reshape(1, -1)
        gqn, gkn = _pad_lanes(s["mla_q_norm"], HEAD_PAD), _pad_lanes(s["mla_k_norm"], HEAD_PAD)
        x1, sv1 = _ffn_fwd(xt, g1, w["ffn1_w_gate"], w["ffn1_w_up"], w["ffn1_w_down"], f"l{l}_ffn1")
        h, proj = _rms_mm(x1, gm, w["w_in"], f"l{l}_proj")
        at("proj", l, proj, None)
        oa = _swa_fwd(proj, pos_col, pos_row, qg2, kg2, sinks, n_batch, s_len, f"l{l}_swa")
        q, k, v = _mla_pre(proj, tabs, gq, gkv, gqn, gkn, w["mla_w_uq"], w["wk"], w["wv"], f"l{l}_mla_pre")
        ob, lse = _mla_flash_fwd(q, k, v, n_batch, s_len, f"l{l}_mla")
        merged, ya, yb = _merge_fwd(oa, ob, proj, w["w_branch_a"], w["w_branch_b"], f"l{l}_merge")
        x2 = _mm([(merged, w["w_out"])], "nn", F32, f"l{l}_out", residual=x1)
        at("out", l, x2, None)
        x3, sv2 = _ffn_fwd(x2, g2, w["ffn2_w_gate"], w["ffn2_w_up"], w["ffn2_w_down"], f"l{l}_ffn2")
        saved.append((w, sv1, sv2, x1, h, proj, oa, q, k, v, ob, lse, merged, ya, yb,
                      (g1, gm, g2, qg2, kg2, sinks, gq, gkv, gqn, gkn)))
        xt = x3

    dy, dyb, loss = _loss_head(xt, target.reshape(t, d))

    big_grads, small_grads = [None] * len(smalls), [None] * len(smalls)
    for l in reversed(range(len(smalls))):
        w, sv1, sv2, x1, h, proj, oa, q, k, v, ob, lse, merged, ya, yb, gains = saved[l]
        g1, gm, g2, qg2, kg2, sinks, gq, gkv, gqn, gkn = gains
        bg, sg = {}, {}
        dy, dyb, sg["ffn2_norm"] = _ffn_bwd(
            dy, dyb, sv2, g2, w["ffn2_w_gate"], w["ffn2_w_up"], w["ffn2_w_down"], f"l{l}_ffn2", bg, FFN2,
            lambda point, token, l=l, bg=bg: at("ffn2_" + point, l, token, bg))
        dya, dyb_, dga, dgb, doa, dob = _merge_bwd(dyb, w["w_out"], proj, ya, yb, w["w_branch_a"], w["w_branch_b"],
                                                   f"l{l}_merge_bwd")
        at("mixer", l, sg["ffn2_norm"], bg)
        bg["w_out"] = _mm([(merged, dyb)], "tn", BF16, f"l{l}_dwo")
        bg["w_branch_a"] = _mm([(dya, oa)], "tn", BF16, f"l{l}_dwa")
        bg["w_branch_b"] = _mm([(dyb_, ob)], "tn", BF16, f"l{l}_dwb")
        dqa, dkc, dkp, dvc, dvp, dqg, dsk = _swa_bwd(
            proj, pos_col, pos_row, qg2, kg2, sinks, doa, n_batch, s_len, f"l{l}_swa_bwd")
        sg["swa_q_norm"], sg["swa_sinks"] = _fold_halves(dqg), dsk[:, :N_HEADS_A]
        at("mixer_mid", l, dqa, bg)
        dka, dva, dkg = _swa_kv_bwd(proj, kg2, dkc, dkp, dvc, dvp, n_batch, s_len, f"l{l}_swa_kv_bwd")
        sg["swa_k_norm"] = _fold_halves(dkg)
        dq, dk, dv = _mla_flash_bwd(q, k, v, ob, dob, lse, n_batch, s_len, f"l{l}_mla_bwd")
        (dcq, dckv, dkr, g_uq, g_wk, g_wv, sg["mla_q_lora_norm"], sg["mla_kv_lora_norm"], dgqn, dgkn) = _mla_pre_bwd(
            proj, tabs, gq, gkv, gqn, gkn, w["mla_w_uq"], w["wk"], w["wv"], dq, dk, dv, f"l{l}_mla_pre_bwd")
        sg["mla_q_norm"], sg["mla_k_norm"] = dgqn[:, :QK_B], dgkn[:, :QK_B]
        bg["mla_w_uq"], bg["wk"], bg["wv"] = g_uq.astype(BF16), g_wk.astype(BF16), g_wv.astype(BF16)
        dproj = (dga, dgb, dqa, dcq, dka, dva, dckv, dkr)
        bg["w_in"] = _mm([(dproj, h)], "tn", BF16, f"l{l}_dwin")
        dy, dyb, sg["mix_norm"] = _mm_rms_bwd([(dproj, w["w_in"])], x1, gm, dy, f"l{l}_dh")
        at("ffn1", l, sg["mix_norm"], bg)
        dy, dyb, sg["ffn1_norm"] = _ffn_bwd(
            dy, dyb, sv1, g1, w["ffn1_w_gate"], w["ffn1_w_up"], w["ffn1_w_down"], f"l{l}_ffn1", bg, FFN1,
            lambda point, token, l=l, bg=bg: at("ffn1_" + point, l, token, bg))
        big_grads[l], small_grads[l] = bg, sg
        at("done", l, dy, bg)
    return loss, dy.reshape(n_batch, s_len, d), big_grads, small_grads


def _round_up(n, m):
    return (n + m - 1) // m * m


def _flat_layout(piece_shapes, members, row_tile):
    table, off = [], 0
    for l, piece in members:
        rows, k = piece_shapes[piece]
        pr = _round_up(rows * k // LANES, 16)
        table.append(((l, piece), off, pr, rows, k))
        off += pr
    return table, _round_up(off, row_tile)


def _pack_flat(stored, table, total):
    parts, off = [], 0
    for key, o, pr, rows, k in table:
        w = stored[key].reshape(rows * k // LANES, LANES)
        parts.append(jnp.pad(w, ((0, pr - w.shape[0]), (0, 0))))
        off = o + pr
    if total > off:
        parts.append(jnp.zeros((total - off, LANES), parts[0].dtype))
    return jnp.concatenate(parts, axis=0)


def _unpack_flat(flat, table):
    return {key: flat[o:o + rows * k // LANES].reshape(rows, k) for key, o, pr, rows, k in table}


def _gathered_mats(gathered, table, layer):
    return {piece: gathered[:, o:o + rows * k // LANES].reshape(N_DEV * rows, k)
            for (l, piece), o, pr, rows, k in table if l == layer}


def _pack_grads(grads, table, total):
    parts, off = [], 0
    for key, o, pr, rows, k in table:
        g = grads[key].reshape(N_DEV, rows * k // LANES, LANES)
        parts.append(jnp.pad(g, ((0, 0), (0, pr - g.shape[1]), (0, 0))))
        off = o + pr
    if total > off:
        parts.append(jnp.zeros((N_DEV, total - off, LANES), BF16))
    return jnp.concatenate(parts, axis=1)


def _pack_small(params, last=None):
    parts = [params[n][l].reshape(-1).astype(F32) for l in range(DEPTH) for n in SMALL]
    v = jnp.concatenate(parts)
    v = jnp.pad(v, (0, SMALL_ROWS * LANES - 1 - v.shape[0]))
    last = jnp.zeros((1,), F32) if last is None else last.reshape(1)
    return jnp.concatenate([v, last]).reshape(SMALL_ROWS, LANES)


def _unpack_small(flat, shapes):
    v, out, off = flat.reshape(-1), {}, 0
    for l in range(DEPTH):
        for n in SMALL:
            sz = math.prod(shapes[n][1:])
            out.setdefault(n, []).append(v[off:off + sz].reshape(shapes[n][1:]))
            off += sz
    return {n: jnp.stack(p) for n, p in out.items()}


_MESH = pl.DeviceIdType.MESH


def _place():
    return lax.axis_index("x"), lax.axis_index("y"), lax.axis_index("c")


def _handshake(peers):
    barrier = pltpu.get_barrier_semaphore()
    for peer in peers:
        pl.semaphore_signal(barrier, inc=1, device_id=peer, device_id_type=_MESH)
    pl.semaphore_wait(barrier, len(peers))


def _comm_call(body, out_shape, scratch, name, seq_id, spec=_ANY):
    if seq_id is None:
        return pl.pallas_call(body, name=name, out_shape=out_shape, in_specs=[spec, _ANY], out_specs=spec,
                              scratch_shapes=scratch)
    nbytes = LINK_COST_SCALE * math.prod(out_shape.shape) * out_shape.dtype.itemsize
    return pl.kernel(body, out_type=out_shape, mesh=plsc.ScalarSubcoreMesh(axis_name="sequencer", num_cores=1),
                     scratch_types=scratch, name=name, compiler_params=pltpu.CompilerParams(collective_id=seq_id),
                     cost_estimate=pl.CostEstimate(flops=0, transcendentals=0, bytes_accessed=nbytes))


def _all_gather(x_shard, name, vmem=False, seq_id=None, after=None):
    spec = pl.BlockSpec(memory_space=pltpu.VMEM) if vmem else _ANY

    def body(x_ref, after_ref, out_ref, send_sems, recv_sems, local_sem):
        x, y, c = _place()
        me, sibling = (x, y, c), (x, y, 1 - c)
        chips = [(1 - x, y), (x, 1 - y), (1 - x, 1 - y)]
        if seq_id is not None:
            _handshake([sibling] + [(*chip, c) for chip in chips])

        def rows(px, py, pc):
            return out_ref.at[4 * px + 2 * py + pc]

        def copy(k, block, to, src=None):
            return pltpu.make_async_remote_copy(
                src_ref=rows(*block) if src is None else src, dst_ref=rows(*block),
                send_sem=send_sems.at[k], recv_sem=recv_sems.at[k], device_id=to, device_id_type=_MESH)

        mine = pltpu.make_async_copy(x_ref, rows(*me), local_sem)
        mine.start()
        first = [copy(0, me, sibling, src=x_ref)]
        first += [copy(1 + j, me, (*chip, c), src=x_ref) for j, chip in enumerate(chips)]
        for cp in first:
            cp.start()
        passed = [copy(4 + j, (*chip, c), sibling) for j, chip in enumerate(chips)]
        for j, chip in enumerate(chips):
            copy(1 + j, (*chip, c), me).wait_recv()
            passed[j].start()
        copy(0, sibling, me).wait_recv()
        for j, chip in enumerate(chips):
            copy(4 + j, (*chip, 1 - c), me).wait_recv()
        for cp in first + passed:
            cp.wait_send()
        mine.wait()

    return _comm_call(
        body, jax.ShapeDtypeStruct((N_DEV,) + x_shard.shape, x_shard.dtype),
        [pltpu.SemaphoreType.DMA((7,)), pltpu.SemaphoreType.DMA((7,)), pltpu.SemaphoreType.DMA], name, seq_id,
        spec)(x_shard, x_shard if after is None else after)


def _exchange_cores(g4, name, seq_id=None, after=None):
    n_chip, _, r, w = g4.shape

    def body(g_ref, after_ref, out_ref, send_sems, recv_sems):
        x, y, c = _place()
        if seq_id is not None:
            _handshake([(x, y, 1 - c)])
        copies = [pltpu.make_async_remote_copy(
            src_ref=g_ref.at[q, 1 - c], dst_ref=out_ref.at[q], send_sem=send_sems.at[q], recv_sem=recv_sems.at[q],
            device_id=(x, y, 1 - c), device_id_type=_MESH) for q in range(n_chip)]
        for cp in copies:
            cp.start()
        for cp in copies:
            cp.wait()

    return _comm_call(
        body, jax.ShapeDtypeStruct((n_chip, r, w), g4.dtype),
        [pltpu.SemaphoreType.DMA((n_chip,)), pltpu.SemaphoreType.DMA((n_chip,))], name, seq_id)(g4, g4 if after is None else after)


def _exchange_chips(s1, name, seq_id=None):
    _, r, w = s1.shape

    def body(s_ref, after_ref, out_ref, send_sems, recv_sems):
        x, y, c = _place()
        chips = [(1 - x, y), (x, 1 - y), (1 - x, 1 - y)]
        if seq_id is not None:
            _handshake([(*chip, c) for chip in chips])
        copies = []
        for k, (tx, ty) in enumerate(chips):
            copies.append(pltpu.make_async_remote_copy(
                src_ref=s_ref.at[2 * tx + ty], dst_ref=out_ref.at[k], send_sem=send_sems.at[k],
                recv_sem=recv_sems.at[k], device_id=(tx, ty, c), device_id_type=_MESH))
        for cp in copies:
            cp.start()
        for cp in copies:
            cp.wait()

    return _comm_call(
        body, jax.ShapeDtypeStruct((3, r, w), s1.dtype),
        [pltpu.SemaphoreType.DMA((3,)), pltpu.SemaphoreType.DMA((3,))], name, seq_id)(s1, s1)


def _chip_sum(g4, recv, core, after, name, tr):
    n_chip, _, r, w = g4.shape

    def body(c_ref, a_ref, b_ref, after_ref, o_ref):
        o_ref[...] = (a_ref[...].astype(F32) + b_ref[...].astype(F32)).astype(o_ref.dtype)

    grid_spec = pltpu.PrefetchScalarGridSpec(
        num_scalar_prefetch=1, grid=(n_chip, r // tr),
        in_specs=[pl.BlockSpec((None, None, tr, w), lambda q, i, c: (q, c[0], i, 0)),
                  pl.BlockSpec((None, tr, w), lambda q, i, c: (q, i, 0)), _ANY],
        out_specs=pl.BlockSpec((None, tr, w), lambda q, i, c: (q, i, 0)))
    return pl.pallas_call(
        body, name=name, grid_spec=grid_spec, out_shape=jax.ShapeDtypeStruct((n_chip, r, w), g4.dtype),
        compiler_params=_cp("parallel", "parallel"))(core, g4, recv, after)


def _adam(w, g, m, v):
    m = ADAM_B1 * m + (1.0 - ADAM_B1) * g
    v = ADAM_B2 * v + (1.0 - ADAM_B2) * (g * g)
    m_hat = m / (1.0 - ADAM_B1 ** ADAM_STEP)
    v_hat = v / (1.0 - ADAM_B2 ** ADAM_STEP)
    delta = -ADAM_LR * (m_hat / (jnp.sqrt(v_hat) + ADAM_EPS) + ADAM_WD * w)
    return delta, m, v


def _grad_sum(s1, r2, chip, name, tr):
    _, r, lanes = s1.shape

    def body(c_ref, s_ref, r0_ref, r1_ref, r2_ref, g_out):
        g_out[...] = ((s_ref[...].astype(F32) + r0_ref[...].astype(F32)) + r1_ref[...].astype(F32)) + r2_ref[
            ...].astype(F32)

    row = pl.BlockSpec((tr, lanes), lambda i, c: (i, 0))
    rel = lambda k: pl.BlockSpec((None, tr, lanes), lambda i, c: (k, i, 0))
    grid_spec = pltpu.PrefetchScalarGridSpec(
        num_scalar_prefetch=1, grid=(r // tr,),
        in_specs=[pl.BlockSpec((None, tr, lanes), lambda i, c: (c[0], i, 0)), rel(0), rel(1), rel(2)], out_specs=row)
    return pl.pallas_call(
        body, name=name, grid_spec=grid_spec, out_shape=jax.ShapeDtypeStruct((r, lanes), F32),
        compiler_params=_cp("parallel"))(chip, s1, r2, r2, r2)


def _adam_big(w, g, m, v, name):
    depth, k, n = w.shape
    tk = k if k <= 512 else _tile(k, (256, 128))

    def body(w_ref, g_ref, m_ref, v_ref, d_out, m_out, v_out):
        d, mn, vn = _adam(w_ref[...], g_ref[...], m_ref[...], v_ref[...])
        d_out[...] = d
        m_out[...] = mn
        v_out[...] = vn

    blk = pl.BlockSpec((None, tk, n), lambda l, i: (l, i, 0))
    return pl.pallas_call(
        body, name=name, grid=(depth, k // tk), in_specs=[blk] * 4, out_specs=[blk] * 3,
        out_shape=[jax.ShapeDtypeStruct(w.shape, F32)] * 3, compiler_params=_cp("parallel", "parallel"))(w, g, m, v)


def _adam_small(parts, w, m, v, name):
    rows, lanes = w.shape

    def body(p_ref, w_ref, m_ref, v_ref, g_out, d_out, m_out, v_out):
        g = p_ref[0:rows, :]
        for dev in range(1, N_DEV):
            g = g + p_ref[dev * rows:(dev + 1) * rows, :]
        d, mn, vn = _adam(w_ref[...], g, m_ref[...], v_ref[...])
        g_out[...] = g
        d_out[...] = d
        m_out[...] = mn
        v_out[...] = vn

    return pl.pallas_call(
        body, name=name, out_shape=[jax.ShapeDtypeStruct((rows, lanes), F32)] * 4)(parts, w, m, v)


def kernel(x, positions, ffn1_norm, ffn1_w_gate, ffn1_w_up, ffn1_w_down, mix_norm, w_in, swa_q_norm, swa_k_norm, swa_sinks, mla_q_lora_norm, mla_w_uq, mla_kv_lora_norm, mla_w_ukv, mla_q_norm, mla_k_norm, w_branch_a, w_branch_b, w_out, ffn2_norm, ffn2_w_gate, ffn2_w_up, ffn2_w_down, loss_target, m_ffn1_norm, m_ffn1_w_gate, m_ffn1_w_up, m_ffn1_w_down, m_mix_norm, m_w_in, m_swa_q_norm, m_swa_k_norm, m_swa_sinks, m_mla_q_lora_norm, m_mla_w_uq, m_mla_kv_lora_norm, m_mla_w_ukv, m_mla_q_norm, m_mla_k_norm, m_w_branch_a, m_w_branch_b, m_w_out, m_ffn2_norm, m_ffn2_w_gate, m_ffn2_w_up, m_ffn2_w_down, v_ffn1_norm, v_ffn1_w_gate, v_ffn1_w_up, v_ffn1_w_down, v_mix_norm, v_w_in, v_swa_q_norm, v_swa_k_norm, v_swa_sinks, v_mla_q_lora_norm, v_mla_w_uq, v_mla_kv_lora_norm, v_mla_w_ukv, v_mla_q_norm, v_mla_k_norm, v_w_branch_a, v_w_branch_b, v_w_out, v_ffn2_norm, v_ffn2_w_gate, v_ffn2_w_up, v_ffn2_w_down):
    given = dict(locals())
    params = {n: given[n] for n in WEIGHTS}
    mom1 = {n: given["m_" + n] for n in WEIGHTS}
    mom2 = {n: given["v_" + n] for n in WEIGHTS}
    assert N_HEADS_B == N_DEV
    stored = {(l, piece): w for l in range(DEPTH) for n, _ in BIG for piece, w in _store(n, params[n][l]).items()}
    piece_shapes = {piece: w.shape for (l, piece), w in stored.items() if l == 0}
    gsegs = [(members, tile) + _flat_layout(piece_shapes, members, tile) for members, tile in GATHER_SEGMENTS]
    rsegs = [(members, tile) + _flat_layout(piece_shapes, members, tile) for members, tile in SCATTER_SEGMENTS]

    def members_of(seg, l):
        return [n for sl, n in seg[0] if sl == l]

    cx, cy, cc = _place()
    core = jnp.reshape(cc, (1,)).astype(jnp.int32)
    chip = jnp.reshape(2 * cx + cy, (1,)).astype(jnp.int32)

    gathered = []
    for s, (_, _, table, total) in enumerate(gsegs):
        w_flat = _pack_flat(stored, table, total).astype(BF16)
        gathered.append(_all_gather(w_flat, f"gather_s{s}", seq_id=SEQ_IDS["gather", s]))

    def get_layer(l):
        def build(name):
            for seg, g in zip(gsegs, gathered):
                if name in members_of(seg, l):
                    return _layer_mats(_gathered_mats(g, seg[2], l))
            raise KeyError(name)
        return _LayerWeights(build)

    smalls = [{n: params[n][l] for n in SMALL} for l in range(DEPTH)]

    pending, big_out, layer_grads = {}, [None] * len(rsegs), {}

    def exchange_cores(s):
        _, _, table, total = rsegs[s]
        mine = {(l, n): _unlayer_w_in(layer_grads[l][n]) if n == "w_in" else layer_grads[l][n] for l, n in rsegs[s][0]}
        g_flat = _pack_grads(mine, table, total)
        g4 = g_flat.reshape(N_DEV // 2, 2, total, LANES)
        pending[s] = (g4, _exchange_cores(g4, f"scatter_cores_s{s}", seq_id=SEQ_IDS["cores", s]))

    def exchange_chips(s, after):
        g4, from_core = pending.pop(s)
        s1 = _chip_sum(g4, from_core, core, after, f"sum_cores_s{s}", rsegs[s][1])
        _Order.tokens = (s1,)
        pending[s] = (s1, _exchange_chips(s1, f"scatter_chips_s{s}", seq_id=SEQ_IDS["chips", s]))

    def finish(s):
        s1, from_chips = pending.pop(s)
        big_out[s] = _grad_sum(s1, from_chips, chip, f"grad_sum_s{s}", rsegs[s][1])

    plan = {("ffn1", 1): [("cores", 4)], ("ffn1_gu", 1): [("chips", 4)],
            ("mixer", 0): [("wait", 4), ("cores", 3)], ("mixer_mid", 0): [("chips", 3)],
            ("ffn1", 0): [("wait", 3), ("cores", 2)], ("ffn1_gu", 0): [("cores", 1), ("chips", 2)],
            ("ffn1_dn", 0): [("chips", 1)], ("done", 0): [("cores", 0)]}

    def at(point, l, token, grads):
        if grads is not None:
            layer_grads[l] = grads
        for what, s in plan.get((point, l), ()):
            if what == "cores":
                exchange_cores(s)
            elif what == "chips":
                exchange_chips(s, token)
            else:
                _Order.tokens += (pending[s][1],)

    loss, grad_x, _, small_grads = _local_step(x, positions, loss_target, get_layer, smalls, at)
    exchange_chips(0, grad_x)

    g_small = _pack_small({n: [small_grads[l][n] for l in range(DEPTH)] for n in SMALL}, loss)
    parts = _all_gather(g_small, "gather_small", vmem=True).reshape(N_DEV * SMALL_ROWS, LANES)
    small_out = _adam_small(parts, _pack_small(params), _pack_small(mom1), _pack_small(mom2), "adam_small")
    shapes = {n: params[n].shape for n in SMALL}
    outs = [_unpack_small(small, shapes) for small in small_out]
    loss = small_out[0].reshape(-1)[-1]

    pieces = {}
    for s in reversed(range(len(rsegs))):
        finish(s)
        pieces.update(_unpack_flat(big_out[s], rsegs[s][2]))
    last = SCATTER_SEGMENTS[0][0][0][1]
    for n, tr in sorted(BIG, key=lambda entry: entry[0] == last):
        view = (lambda a: jnp.swapaxes(a, 1, 2)) if tr else (lambda a: a)
        g = jnp.stack([_unstore(n, {p: pieces[l, p] for p in PIECES_OF.get(n, (n,))}) for l in range(DEPTH)])
        updated = _adam_big(view(params[n]), g, view(mom1[n]), view(mom2[n]), f"adam_{n}")
        for tree, leaf in zip(outs, (g,) + tuple(updated)):
            tree[n] = view(leaf)
    return (loss, grad_x, *[o[n] for o in outs for n in WEIGHTS])
```

```python
import math

import jax
import jax.numpy as jnp
from jax import lax
from jax.experimental import pallas as pl
from jax.experimental.pallas import tpu as pltpu
from jax.experimental.pallas import tpu_sc as plsc

F32 = jnp.float32
BF16 = jnp.bfloat16

N_DEV = 8
DEPTH = 2
D_MODEL = 1024
D_FF = 2816
HEAD_DIM_A = 64
N_HEADS_A = 8
N_KV_HEADS_A = 2
GROUP_A = N_HEADS_A // N_KV_HEADS_A
BLOCK = 128
N_HEADS_B = 8
Q_LORA = 256
KV_LORA = 128
NOPE = 64
ROPE = 32
QK_B = NOPE + ROPE
V_B = 64
HEAD_PAD = 128
WIDTH_A = N_HEADS_A * HEAD_DIM_A
WIDTH_B = N_HEADS_B * V_B
WIDTH_BP = N_HEADS_B * HEAD_PAD
KV_A = N_KV_HEADS_A * HEAD_DIM_A
IN_WIDTH = WIDTH_A + 2 * KV_A + Q_LORA + KV_LORA + ROPE + 2 * D_MODEL
ROPE_BASE = 10000.0
EPS = 1e-6
NEG = -1e30
LOG2E = 1.4426950408889634

P_GA, P_GB, P_QA, P_CQ, P_KA, P_VA, P_CKV, P_KR = 0, 1024, 2048, 2560, 2816, 2944, 3072, 3200
PROJ_W = 3328
SWA_HEAD_ORDER = (0, 4, 1, 5, 2, 6, 3, 7)
SWA_HEAD_INVERSE = tuple(SWA_HEAD_ORDER.index(h) for h in range(N_HEADS_A))

ADAM_LR, ADAM_B1, ADAM_B2, ADAM_EPS, ADAM_WD, ADAM_STEP = 0.001, 0.9, 0.999, 1e-08, 0.01, 10

VMEM_LIMIT = 56 * 1024 * 1024
LANES = 1024

BIG = (("ffn1_w_gate", True), ("ffn1_w_up", True), ("ffn1_w_down", False), ("w_in", True), ("mla_w_uq", True),
       ("mla_w_ukv", True), ("w_branch_a", True), ("w_branch_b", True), ("w_out", False),
       ("ffn2_w_gate", True), ("ffn2_w_up", True), ("ffn2_w_down", False))
SMALL = ("ffn1_norm", "mix_norm", "ffn2_norm", "swa_q_norm", "swa_k_norm", "swa_sinks", "mla_q_lora_norm",
         "mla_kv_lora_norm", "mla_q_norm", "mla_k_norm")
WEIGHTS = ("ffn1_norm", "ffn1_w_gate", "ffn1_w_up", "ffn1_w_down", "mix_norm", "w_in", "swa_q_norm", "swa_k_norm",
           "swa_sinks", "mla_q_lora_norm", "mla_w_uq", "mla_kv_lora_norm", "mla_w_ukv", "mla_q_norm", "mla_k_norm",
           "w_branch_a", "w_branch_b", "w_out", "ffn2_norm", "ffn2_w_gate", "ffn2_w_up", "ffn2_w_down")
SMALL_ROWS = 8
FFN1 = ("ffn1_w_gate", "ffn1_w_up", "ffn1_w_down")
FFN2 = ("ffn2_w_gate", "ffn2_w_up", "ffn2_w_down")
MIXER = ("w_in", "mla_w_uq", "wk", "wv", "w_branch_a", "w_branch_b", "w_out")
GATHER_SEGMENTS = (
    (tuple((0, n) for n in FFN1[:2]), 352),
    (((0, FFN1[2]),), 352),
    (tuple((0, n) for n in MIXER), 400),
    (tuple((0, n) for n in FFN2) + tuple((1, n) for n in FFN1), 352),
    (tuple((1, n) for n in MIXER + FFN2), 464),
)
SCATTER_SEGMENTS = (
    (((0, FFN1[2]),), 352),
    (tuple((0, n) for n in FFN1[:2]), 352),
    (tuple((0, n) for n in MIXER), 400),
    (tuple((0, n) for n in FFN2) + tuple((1, n) for n in FFN1), 352),
    (tuple((1, n) for n in MIXER + FFN2), 464),
)
LINK_COST_SCALE = 64
SEQ_IDS = {(kind, s): 1 + 5 * k + s for k, kind in enumerate(("gather", "cores", "chips")) for s in range(5)}


def _cp(*sem):
    return pltpu.CompilerParams(dimension_semantics=sem, vmem_limit_bytes=VMEM_LIMIT)


def _tile(n, prefs):
    for t in prefs:
        if n % t == 0:
            return t
    return n


def _dot(a, b, dims):
    return lax.dot_general(a, b, (dims, ((), ())), preferred_element_type=F32)


_NT = ((1,), (1,))
_NN = ((1,), (0,))
_TN = ((0,), (0,))


_ANY = pl.BlockSpec(memory_space=pl.ANY)


class _Order:
    tokens = ()


def _tc_call(body, *, in_specs, **kw):
    def run(*args):
        tokens, n = _Order.tokens, len(args)
        if not tokens:
            out = pl.pallas_call(body, in_specs=in_specs, **kw)(*args)
        else:
            def chained(*refs):
                return body(*refs[:n], *refs[n + len(tokens):])
            out = pl.pallas_call(chained, in_specs=list(in_specs) + [_ANY] * len(tokens), **kw)(*args, *tokens)
        _Order.tokens = (jax.tree.leaves(out)[0],)
        return out
    return run


def _sigmoid(x):
    return 0.5 * jnp.tanh(0.5 * x) + 0.5


def _chunks(n, width):
    return [(c, min(width, n - c)) for c in range(0, n, width)]


def _mm(pairs, mode, out_dtype, name, residual=None, alpha=1.0):
    for a, b in pairs:
        for piece in (a if isinstance(a, tuple) else (a,)):
            assert piece.dtype == BF16 and b.dtype == BF16, (name, piece.dtype, b.dtype)
    if mode == "tn":
        (a, b), = pairs
        return _mm_tokens(a, b, out_dtype, name, alpha)
    t = pairs[0][0].shape[0]
    n = pairs[0][1].shape[0] if mode == "nt" else pairs[0][1].shape[1]
    tm = _tile(t, (512, 256, 128))
    dims = _NT if mode == "nt" else _NN
    in_specs, args = [], []
    for a, w in pairs:
        in_specs.append(pl.BlockSpec((tm, a.shape[1]), lambda i: (i, 0)))
        in_specs.append(pl.BlockSpec(w.shape, lambda i: (0, 0)))
        args += [a, w]
    if residual is not None:
        in_specs.append(pl.BlockSpec((tm, n), lambda i: (i, 0)))
        args.append(residual)
    n_pairs = len(pairs)

    def body(*refs):
        o_ref = refs[-1]
        for c0, cw in _chunks(n, 512):
            acc = None
            for p in range(n_pairs):
                w_ref = refs[2 * p + 1]
                w = w_ref[c0:c0 + cw, :] if mode == "nt" else w_ref[:, c0:c0 + cw]
                d = _dot(refs[2 * p][...], w, dims)
                acc = d if acc is None else acc + d
            if alpha != 1.0:
                acc = acc * alpha
            if residual is not None:
                acc = refs[2 * n_pairs][:, c0:c0 + cw] + acc
            o_ref[:, c0:c0 + cw] = acc.astype(out_dtype)

    return _tc_call(
        body, name=name, grid=(t // tm,), in_specs=in_specs, out_specs=pl.BlockSpec((tm, n), lambda i: (i, 0)),
        out_shape=jax.ShapeDtypeStruct((t, n), out_dtype), compiler_params=_cp("parallel"))(*args)


def _mm_tokens(a, b, out_dtype, name, alpha):
    pieces = a if isinstance(a, tuple) else (a,)
    t = b.shape[0]
    widths = [p.shape[1] for p in pieces]
    m, n = sum(widths), b.shape[1]
    tk = _tile(t, (512, 256, 128))
    n_pieces = len(pieces)

    def body(*refs):
        b_ref, o_ref, acc_ref = refs[n_pieces:]
        k = pl.program_id(0)

        @pl.when(k == 0)
        def _():
            acc_ref[...] = jnp.zeros_like(acc_ref)

        off = 0
        for a_ref, width in zip(refs[:n_pieces], widths):
            for c0, cw in _chunks(width, 512):
                acc_ref[off + c0:off + c0 + cw, :] += _dot(a_ref[:, c0:c0 + cw], b_ref[...], _TN)
            off += width

        @pl.when(k == pl.num_programs(0) - 1)
        def _():
            o_ref[...] = (acc_ref[...] * alpha).astype(out_dtype)

    return _tc_call(
        body, name=name, grid=(t // tk,),
        in_specs=[pl.BlockSpec((tk, w), lambda k: (k, 0)) for w in widths] + [pl.BlockSpec((tk, n), lambda k: (k, 0))],
        out_specs=pl.BlockSpec((m, n), lambda k: (0, 0)), out_shape=jax.ShapeDtypeStruct((m, n), out_dtype),
        scratch_shapes=[pltpu.VMEM((m, n), F32)], compiler_params=_cp("arbitrary"))(*pieces, b)


def _rms_mm(x, gain, w, name):
    t, d = x.shape
    n = w.shape[0]
    tm = _tile(t, (512, 256, 128))

    def body(x_ref, g_ref, w_ref, h_ref, o_ref):
        xv = x_ref[...]
        r = lax.rsqrt(jnp.mean(xv * xv, axis=1, keepdims=True) + EPS)
        hv = (xv * r * g_ref[...]).astype(BF16)
        h_ref[...] = hv
        for c0, cw in _chunks(n, 512):
            o_ref[:, c0:c0 + cw] = _dot(hv, w_ref[c0:c0 + cw, :], _NT)

    row = pl.BlockSpec((tm, d), lambda i: (i, 0))
    return _tc_call(
        body, name=name, grid=(t // tm,),
        in_specs=[row, pl.BlockSpec((1, d), lambda i: (0, 0)), pl.BlockSpec(w.shape, lambda i: (0, 0))],
        out_specs=[row, pl.BlockSpec((tm, n), lambda i: (i, 0))],
        out_shape=[jax.ShapeDtypeStruct((t, d), BF16), jax.ShapeDtypeStruct((t, n), F32)],
        compiler_params=_cp("parallel"))(x, gain, w)


def _mm_rms_bwd(pairs, x, gain, res, name):
    t, d = x.shape
    tm = _tile(t, (512, 256, 128))
    acts, weights, entries = [], [], []
    for a, w in pairs:
        k0 = 0
        for piece in (a if isinstance(a, tuple) else (a,)):
            assert piece.dtype == BF16 and w.dtype == BF16, (name, piece.dtype, w.dtype)
            entries.append((len(acts), len(weights), k0, piece.shape[1]))
            acts.append(piece)
            k0 += piece.shape[1]
        assert k0 == w.shape[0], (name, k0, w.shape)
        weights.append(w)
    n_acts = len(acts)

    def body(*refs):
        x_ref, g_ref, res_ref, dx_ref, dxb_ref, dg_ref, dn_ref = refs[n_acts + len(weights):]
        for c0, cw in _chunks(d, 512):
            acc = None
            for ai, wi, k0, kw in entries:
                part = _dot(refs[ai][...], refs[n_acts + wi][k0:k0 + kw, c0:c0 + cw], _NN)
                acc = part if acc is None else acc + part
            dn_ref[:, c0:c0 + cw] = acc
        xv = x_ref[...]
        r = lax.rsqrt(jnp.mean(xv * xv, axis=1, keepdims=True) + EPS)
        xh = xv * r
        dnv = dn_ref[...]
        dxh = dnv * g_ref[...]
        dx = res_ref[...] + r * (dxh - xh * jnp.mean(dxh * xh, axis=1, keepdims=True))
        dx_ref[...] = dx
        dxb_ref[...] = dx.astype(BF16)

        @pl.when(pl.program_id(0) == 0)
        def _():
            dg_ref[...] = jnp.zeros_like(dg_ref)

        dg_ref[...] += jnp.sum(dnv * xh, axis=0, keepdims=True)

    in_specs = [pl.BlockSpec((tm, a.shape[1]), lambda i: (i, 0)) for a in acts]
    in_specs += [pl.BlockSpec(w.shape, lambda i: (0, 0)) for w in weights]
    row = pl.BlockSpec((tm, d), lambda i: (i, 0))
    one = pl.BlockSpec((1, d), lambda i: (0, 0))
    return _tc_call(
        body, name=name, grid=(t // tm,), in_specs=in_specs + [row, one, row], out_specs=[row, row, one],
        out_shape=[jax.ShapeDtypeStruct((t, d), F32), jax.ShapeDtypeStruct((t, d), BF16),
                   jax.ShapeDtypeStruct((1, d), F32)],
        scratch_shapes=[pltpu.VMEM((tm, d), F32)],
        compiler_params=_cp("arbitrary"))(*acts, *weights, x, gain, res)


def _ffn_up(x, gain, wg_t, wu_t, name):
    t, d = x.shape
    f = wg_t.shape[0]
    tm = _tile(t, (512, 256, 128))

    def body(x_ref, g_ref, wg_ref, wu_ref, n_ref, a_ref, b_ref, h_ref):
        xv = x_ref[...]
        r = lax.rsqrt(jnp.mean(xv * xv, axis=1, keepdims=True) + EPS)
        nv = (xv * r * g_ref[...]).astype(BF16)
        n_ref[...] = nv
        for c0, cw in _chunks(f, 256):
            a = _dot(nv, wg_ref[c0:c0 + cw, :], _NT)
            b = _dot(nv, wu_ref[c0:c0 + cw, :], _NT)
            a_ref[:, c0:c0 + cw] = a.astype(BF16)
            b_ref[:, c0:c0 + cw] = b.astype(BF16)
            h_ref[:, c0:c0 + cw] = (a * _sigmoid(a) * b).astype(BF16)

    w_spec = pl.BlockSpec((f, d), lambda i: (0, 0))
    x_spec = pl.BlockSpec((tm, d), lambda i: (i, 0))
    o_spec = pl.BlockSpec((tm, f), lambda i: (i, 0))
    o_shape = jax.ShapeDtypeStruct((t, f), BF16)
    return _tc_call(
        body, name=name, grid=(t // tm,), in_specs=[x_spec, pl.BlockSpec((1, d), lambda i: (0, 0)), w_spec, w_spec],
        out_specs=[x_spec] + [o_spec] * 3, out_shape=[jax.ShapeDtypeStruct((t, d), BF16)] + [o_shape] * 3,
        compiler_params=_cp("parallel"))(x, gain, wg_t, wu_t)


def _ffn_down_bwd(dxb, wd, a, b, name):
    t, d = dxb.shape
    f = wd.shape[0]
    tm = _tile(t, (512, 256, 128))

    def body(dx_ref, wd_ref, a_ref, b_ref, da_ref, db_ref):
        dxv = dx_ref[...]
        for c0, cw in _chunks(f, 256):
            dh = 0.5 * _dot(dxv, wd_ref[c0:c0 + cw, :], _NT)
            av = a_ref[:, c0:c0 + cw].astype(F32)
            bv = b_ref[:, c0:c0 + cw].astype(F32)
            sg = _sigmoid(av)
            da_ref[:, c0:c0 + cw] = (dh * bv * (sg * (1.0 + av * (1.0 - sg)))).astype(BF16)
            db_ref[:, c0:c0 + cw] = (dh * (av * sg)).astype(BF16)

    o_spec = pl.BlockSpec((tm, f), lambda i: (i, 0))
    o_shape = jax.ShapeDtypeStruct((t, f), BF16)
    return _tc_call(
        body, name=name, grid=(t // tm,),
        in_specs=[pl.BlockSpec((tm, d), lambda i: (i, 0)), pl.BlockSpec((f, d), lambda i: (0, 0)), o_spec, o_spec],
        out_specs=[o_spec] * 2, out_shape=[o_shape] * 2, compiler_params=_cp("parallel"))(dxb, wd, a, b)


def _loss_head(y, target):
    t, d = y.shape
    tm = _tile(t, (512, 256, 128))

    def body(y_ref, t_ref, dy_ref, dyb_ref, loss_ref, acc_ref):
        i = pl.program_id(0)
        e = y_ref[...] - t_ref[...]
        dy = e * (1.0 / d)
        dy_ref[...] = dy
        dyb_ref[...] = dy.astype(BF16)

        @pl.when(i == 0)
        def _():
            acc_ref[...] = jnp.zeros_like(acc_ref)

        acc_ref[...] += jnp.sum(e * e, axis=0, keepdims=True)

        @pl.when(i == pl.num_programs(0) - 1)
        def _():
            loss_ref[...] = jnp.sum(acc_ref[...], axis=1, keepdims=True) * (0.5 / d)

    row = pl.BlockSpec((tm, d), lambda i: (i, 0))
    return _tc_call(
        body, name="loss_head", grid=(t // tm,), in_specs=[row, row],
        out_specs=[row, row, pl.BlockSpec((1, 1), lambda i: (0, 0))],
        out_shape=[jax.ShapeDtypeStruct((t, d), F32), jax.ShapeDtypeStruct((t, d), BF16),
                   jax.ShapeDtypeStruct((1, 1), F32)],
        scratch_shapes=[pltpu.VMEM((1, d), F32)], compiler_params=_cp("arbitrary"))(y, target)


def _merge_fwd(oa, ob, proj, wa_t, wb_t, name):
    t = oa.shape[0]
    d = wa_t.shape[0]
    tm = _tile(t, (512, 256, 128))

    def body(oa_ref, ob_ref, ga_ref, gb_ref, wa_ref, wb_ref, mg_ref, ya_ref, yb_ref):
        oav, obv = oa_ref[...], ob_ref[...]
        for c0, cw in _chunks(d, 512):
            cs = slice(c0, c0 + cw)
            ya = _dot(oav, wa_ref[cs, :], _NT)
            yb = _dot(obv, wb_ref[cs, :], _NT)
            mg_ref[:, cs] = (_sigmoid(ga_ref[:, cs]) * ya + _sigmoid(gb_ref[:, cs]) * yb).astype(BF16)
            ya_ref[:, cs] = ya.astype(BF16)
            yb_ref[:, cs] = yb.astype(BF16)

    o_spec = pl.BlockSpec((tm, d), lambda i: (i, 0))
    o_shape = jax.ShapeDtypeStruct((t, d), BF16)
    return _tc_call(
        body, name=name, grid=(t // tm,),
        in_specs=[pl.BlockSpec((tm, oa.shape[1]), lambda i: (i, 0)), pl.BlockSpec((tm, ob.shape[1]), lambda i: (i, 0)),
                  pl.BlockSpec((tm, d), lambda i: (i, P_GA // d)), pl.BlockSpec((tm, d), lambda i: (i, P_GB // d)),
                  pl.BlockSpec(wa_t.shape, lambda i: (0, 0)), pl.BlockSpec(wb_t.shape, lambda i: (0, 0))],
        out_specs=[o_spec] * 3, out_shape=[o_shape] * 3,
        compiler_params=_cp("parallel"))(oa, ob, proj, proj, wa_t, wb_t)


def _merge_bwd(dxb, wo, proj, ya, yb, wa_t, wb_t, name):
    t, d = dxb.shape
    tm = _tile(t, (512, 256, 128))
    wa_in, wb_in = wa_t.shape[1], wb_t.shape[1]

    def body(dx_ref, wo_ref, ga_ref, gb_ref, ya_ref, yb_ref, wa_ref, wb_ref,
             dya_ref, dyb_ref, dga_ref, dgb_ref, doa_ref, dob_ref):
        dxv = dx_ref[...]
        doa = jnp.zeros((tm, wa_in), F32)
        dob = jnp.zeros((tm, wb_in), F32)
        for c0, cw in _chunks(d, 512):
            cs = slice(c0, c0 + cw)
            dm = _dot(dxv, wo_ref[cs, :], _NT)
            sa = _sigmoid(ga_ref[:, cs])
            sb = _sigmoid(gb_ref[:, cs])
            dya = (dm * sa).astype(BF16)
            dyb = (dm * sb).astype(BF16)
            dya_ref[:, cs] = dya
            dyb_ref[:, cs] = dyb
            dga_ref[:, cs] = (dm * ya_ref[:, cs].astype(F32) * (sa * (1.0 - sa))).astype(BF16)
            dgb_ref[:, cs] = (dm * yb_ref[:, cs].astype(F32) * (sb * (1.0 - sb))).astype(BF16)
            doa = doa + _dot(dya, wa_ref[cs, :], _NN)
            dob = dob + _dot(dyb, wb_ref[cs, :], _NN)
        doa_ref[...] = doa.astype(BF16)
        dob_ref[...] = dob.astype(BF16)

    o_spec = pl.BlockSpec((tm, d), lambda i: (i, 0))
    o_shape = jax.ShapeDtypeStruct((t, d), BF16)
    return _tc_call(
        body, name=name, grid=(t // tm,),
        in_specs=[o_spec, pl.BlockSpec((d, d), lambda i: (0, 0)),
                  pl.BlockSpec((tm, d), lambda i: (i, P_GA // d)), pl.BlockSpec((tm, d), lambda i: (i, P_GB // d)),
                  o_spec, o_spec, pl.BlockSpec(wa_t.shape, lambda i: (0, 0)), pl.BlockSpec(wb_t.shape, lambda i: (0, 0))],
        out_specs=[o_spec] * 4 + [pl.BlockSpec((tm, wa_in), lambda i: (i, 0)), pl.BlockSpec((tm, wb_in), lambda i: (i, 0))],
        out_shape=[o_shape] * 4 + [jax.ShapeDtypeStruct((t, wa_in), BF16), jax.ShapeDtypeStruct((t, wb_in), BF16)],
        compiler_params=_cp("parallel"))(dxb, wo, proj, proj, ya, yb, wa_t, wb_t)


def _swa_common(has_prev, pq, pk):
    dist = (pq - pk).astype(F32)
    row = lax.broadcasted_iota(jnp.int32, (BLOCK, 2 * BLOCK), 0)
    col = lax.broadcasted_iota(jnp.int32, (BLOCK, 2 * BLOCK), 1)
    diff = row + BLOCK - col
    valid = (diff >= 0) & (diff < BLOCK) & (has_prev | (col >= BLOCK))
    return jnp.concatenate([dist] * N_HEADS_A, axis=0), jnp.concatenate([valid] * N_HEADS_A, axis=0)


def _half_sum(x, lo):
    s_lo = jnp.sum(jnp.where(lo, x, 0.0), axis=1, keepdims=True)
    s_hi = jnp.sum(jnp.where(lo, 0.0, x), axis=1, keepdims=True)
    return jnp.where(lo, s_lo, s_hi)


def _norm2(x, gain2, lo):
    r = lax.rsqrt(_half_sum(x * x, lo) * (1.0 / HEAD_DIM_A) + EPS)
    xh = x * r
    return xh * gain2, xh, r


def _norm2_bwd(d, xh, r, gain2, lo):
    dxh = d * gain2
    return r * (dxh - xh * (_half_sum(dxh * xh, lo) * (1.0 / HEAD_DIM_A)))


def _swa_stack(tiles, lo):
    zero = jnp.zeros_like(tiles[0])
    return jnp.concatenate([jnp.where(lo, t, zero) for t in tiles] + [jnp.where(lo, zero, t) for t in tiles], axis=0)


def _swa_unstack(x8, j, lo):
    return jnp.where(lo, x8[j * BLOCK:(j + 1) * BLOCK], x8[(GROUP_A + j) * BLOCK:(GROUP_A + j + 1) * BLOCK])


def _swa_head_columns(sk_ref):
    slope = jnp.concatenate([jnp.full((BLOCK, 1), 2.0 ** (-(h + 1)), F32) for h in range(N_HEADS_A)], axis=0)
    sink = jnp.concatenate([jnp.broadcast_to(sk_ref[:, h:h + 1], (BLOCK, 1)) for h in range(N_HEADS_A)], axis=0)
    return slope, sink


def _swa_blocks_per_step(s_len):
    nb = s_len // BLOCK
    return 4 if nb % 4 == 0 and nb >= 8 else 2 if nb % 2 == 0 else 1


def _swa_specs(s_len):
    nb, qb = s_len // BLOCK, _swa_blocks_per_step(s_len)
    ns, rows = nb // qb, qb * BLOCK

    def step(b, j):
        return b * ns + j

    def prev(b, j):
        return b * nb + jnp.maximum(qb * j - 1, 0)

    q_spec = pl.BlockSpec((rows, WIDTH_A), lambda b, j: (step(b, j), P_QA // WIDTH_A))
    kc_spec = pl.BlockSpec((rows, KV_A), lambda b, j: (step(b, j), P_KA // KV_A))
    kp_spec = pl.BlockSpec((BLOCK, KV_A), lambda b, j: (prev(b, j), P_KA // KV_A))
    vc_spec = pl.BlockSpec((rows, KV_A), lambda b, j: (step(b, j), P_VA // KV_A))
    vp_spec = pl.BlockSpec((BLOCK, KV_A), lambda b, j: (prev(b, j), P_VA // KV_A))
    pq_spec = pl.BlockSpec((rows, 1), lambda b, j: (step(b, j), 0))
    pkc_spec = pl.BlockSpec((qb, 1, BLOCK), lambda b, j: (step(b, j), 0, 0))
    pkp_spec = pl.BlockSpec((1, 1, BLOCK), lambda b, j: (prev(b, j), 0, 0))
    return qb, ns, step, [q_spec, kc_spec, kp_spec, vc_spec, vp_spec, pq_spec, pkc_spec, pkp_spec]


def _swa_stage_probs(qb, q_ref, kc_ref, kp_ref, vc_ref, vp_ref, pq_ref, pkc_ref, pkp_ref, qg_ref, kg_ref, sk_ref, lo):
    first = pl.program_id(1) * qb
    kk_all = _norm2(jnp.concatenate([kp_ref[...], kc_ref[...]], axis=0), kg_ref[...], lo)[0].astype(BF16)
    vv_all = jnp.concatenate([vp_ref[...], vc_ref[...]], axis=0).astype(BF16)
    pk_all = jnp.concatenate([pkp_ref[0]] + [pkc_ref[s] for s in range(qb)], axis=1)
    rows = [slice(s * BLOCK, (s + 1) * BLOCK) for s in range(qb)]
    keys = [slice(s * BLOCK, (s + 2) * BLOCK) for s in range(qb)]
    masks = [_swa_common(first + s > 0, pq_ref[rows[s], :], pk_all[:, keys[s]]) for s in range(qb)]
    qs = [[_norm2(q_ref[r, j * HEAD_PAD:(j + 1) * HEAD_PAD], qg_ref[...], lo) for j in range(GROUP_A)] for r in rows]
    q8 = [_swa_stack([q[0] for q in tiles], lo).astype(BF16) for tiles in qs]
    kk = [kk_all[ks] for ks in keys]
    vv = [vv_all[ks] for ks in keys]
    slope, sink = _swa_head_columns(sk_ref)
    s = [_dot(q8[b], kk[b], _NT) * (HEAD_DIM_A ** -0.5) - slope * masks[b][0] for b in range(qb)]
    s = [jnp.where(masks[b][1], s[b], NEG) for b in range(qb)]
    m = [jnp.maximum(jnp.max(x, axis=1, keepdims=True), sink) for x in s]
    e = [jnp.exp(x - mx) for x, mx in zip(s, m)]
    es = [jnp.exp(sink - mx) for mx in m]
    inv = [1.0 / (jnp.sum(x, axis=1, keepdims=True) + y) for x, y in zip(e, es)]
    p = [x * i for x, i in zip(e, inv)]
    ps = [y * i for y, i in zip(es, inv)]
    return rows, qs, q8, kk, vv, p, ps


def _swa_fwd(proj, pos_col, pos_row, qg2, kg2, sinks, n_batch, s_len, name):
    t = proj.shape[0]
    qb, ns, step, specs = _swa_specs(s_len)
    small = pl.BlockSpec((1, HEAD_PAD), lambda b, j: (0, 0))

    def body(q_ref, kc_ref, kp_ref, vc_ref, vp_ref, pq_ref, pkc_ref, pkp_ref, qg_ref, kg_ref, sk_ref, o_ref):
        lo = lax.broadcasted_iota(jnp.int32, (1, HEAD_PAD), 1) < HEAD_DIM_A
        rows, _, _, _, vv, p, _ = _swa_stage_probs(qb, q_ref, kc_ref, kp_ref, vc_ref, vp_ref, pq_ref, pkc_ref, pkp_ref,
                                                   qg_ref, kg_ref, sk_ref, lo)
        o8 = [_dot(p[b].astype(BF16), vv[b], _NN) for b in range(qb)]
        for b in range(qb):
            for j in range(GROUP_A):
                o_ref[rows[b], j * HEAD_PAD:(j + 1) * HEAD_PAD] = _swa_unstack(o8[b], j, lo).astype(BF16)

    return _tc_call(
        body, name=name, grid=(n_batch, ns), in_specs=specs + [small, small, small],
        out_specs=pl.BlockSpec((qb * BLOCK, WIDTH_A), lambda b, j: (step(b, j), 0)),
        out_shape=jax.ShapeDtypeStruct((t, WIDTH_A), BF16),
        compiler_params=_cp("parallel", "parallel"))(proj, proj, proj, proj, proj, pos_col, pos_row, pos_row,
                                                     qg2, kg2, sinks)


def _swa_bwd(proj, pos_col, pos_row, qg2, kg2, sinks, do, n_batch, s_len, name):
    t = proj.shape[0]
    qb, ns, step, specs = _swa_specs(s_len)
    small = pl.BlockSpec((1, HEAD_PAD), lambda b, j: (0, 0))
    scale = HEAD_DIM_A ** -0.5

    def body(q_ref, kc_ref, kp_ref, vc_ref, vp_ref, pq_ref, pkc_ref, pkp_ref, qg_ref, kg_ref, sk_ref, do_ref,
             dq_ref, dkc_ref, dkp_ref, dvc_ref, dvp_ref, dqg_ref, dsk_ref):
        @pl.when((pl.program_id(0) == 0) & (pl.program_id(1) == 0))
        def _():
            dqg_ref[...] = jnp.zeros_like(dqg_ref)
            dsk_ref[...] = jnp.zeros_like(dsk_ref)

        lane = lax.broadcasted_iota(jnp.int32, (1, HEAD_PAD), 1)
        lo = lane < HEAD_DIM_A
        rows, qs, q8, kk, vv, p, ps = _swa_stage_probs(qb, q_ref, kc_ref, kp_ref, vc_ref, vp_ref, pq_ref, pkc_ref,
                                                       pkp_ref, qg_ref, kg_ref, sk_ref, lo)
        blocks = range(qb)
        do8 = [_swa_stack([do_ref[r, j * HEAD_PAD:(j + 1) * HEAD_PAD] for j in range(GROUP_A)], lo) for r in rows]
        dp = [_dot(do8[b], vv[b], _NT) for b in blocks]
        delta = [jnp.sum(p[b] * dp[b], axis=1, keepdims=True) for b in blocks]
        ds = [(p[b] * (dp[b] - delta[b]) * scale).astype(BF16) for b in blocks]
        dsink = [ps[b] * delta[b] for b in blocks]
        dvv = [_dot(p[b].astype(BF16), do8[b], _TN) for b in blocks]
        dkk = [_dot(ds[b], q8[b], _TN) for b in blocks]
        dq8 = [_dot(ds[b], kk[b], _NN) for b in blocks]
        dsk = jnp.zeros((1, HEAD_PAD), F32)
        dqg = jnp.zeros((1, HEAD_PAD), F32)
        for b in blocks:
            for h in range(N_HEADS_A):
                dsk = dsk + jnp.where(lane == h, -jnp.sum(dsink[b][h * BLOCK:(h + 1) * BLOCK]), 0.0)
            for j in range(GROUP_A):
                _, xh, r = qs[b][j]
                dqn = _swa_unstack(dq8[b], j, lo)
                dqg = dqg + jnp.sum(dqn * xh, axis=0, keepdims=True)
                dq_ref[rows[b], j * HEAD_PAD:(j + 1) * HEAD_PAD] = _norm2_bwd(dqn, xh, r, qg_ref[...], lo).astype(BF16)
            dkp_ref[rows[b], :] = dkk[b][:BLOCK]
            dkc_ref[rows[b], :] = dkk[b][BLOCK:]
            dvp_ref[rows[b], :] = dvv[b][:BLOCK]
            dvc_ref[rows[b], :] = dvv[b][BLOCK:]
        dqg_ref[...] += dqg
        dsk_ref[...] += dsk

    kv_out = pl.BlockSpec((qb * BLOCK, KV_A), lambda b, j: (step(b, j), 0))
    kv_shape = jax.ShapeDtypeStruct((t, KV_A), F32)
    wide = pl.BlockSpec((qb * BLOCK, WIDTH_A), lambda b, j: (step(b, j), 0))
    return _tc_call(
        body, name=name, grid=(n_batch, ns), in_specs=specs + [small, small, small, wide],
        out_specs=[wide, kv_out, kv_out, kv_out, kv_out, small, small],
        out_shape=[jax.ShapeDtypeStruct((t, WIDTH_A), BF16), kv_shape, kv_shape, kv_shape, kv_shape,
                   jax.ShapeDtypeStruct((1, HEAD_PAD), F32), jax.ShapeDtypeStruct((1, HEAD_PAD), F32)],
        compiler_params=_cp("arbitrary", "arbitrary"))(proj, proj, proj, proj, proj, pos_col, pos_row, pos_row,
                                                       qg2, kg2, sinks, do)


def _swa_kv_bwd(proj, kg2, dkc, dkp, dvc, dvp, n_batch, s_len, name):
    t = proj.shape[0]
    nb, qb = s_len // BLOCK, _swa_blocks_per_step(s_len)
    ns, rows = nb // qb, qb * BLOCK

    def step(b, j):
        return b * ns + j

    def edge(b, j):
        return b * nb + jnp.minimum(qb * (j + 1), nb - 1)

    def body(k_ref, kg_ref, dkc_ref, dkp_ref, dkp_edge, dvc_ref, dvp_ref, dvp_edge, dk_ref, dv_ref, dkg_ref):
        @pl.when((pl.program_id(0) == 0) & (pl.program_id(1) == 0))
        def _():
            dkg_ref[...] = jnp.zeros_like(dkg_ref)

        lo = lax.broadcasted_iota(jnp.int32, (1, HEAD_PAD), 1) < HEAD_DIM_A
        has_next = (pl.program_id(1) < ns - 1).astype(F32)

        def from_next(p_ref, edge_ref):
            return jnp.concatenate([p_ref[BLOCK:, :], has_next * edge_ref[...]], axis=0)

        dkn = dkc_ref[...] + from_next(dkp_ref, dkp_edge)
        dv_ref[...] = (dvc_ref[...] + from_next(dvp_ref, dvp_edge)).astype(BF16)
        _, xh, r = _norm2(k_ref[...], kg_ref[...], lo)
        dkg_ref[...] += jnp.sum(dkn * xh, axis=0, keepdims=True)
        dk_ref[...] = _norm2_bwd(dkn, xh, r, kg_ref[...], lo).astype(BF16)

    cur = pl.BlockSpec((rows, KV_A), lambda b, j: (step(b, j), 0))
    nxt = pl.BlockSpec((BLOCK, KV_A), lambda b, j: (edge(b, j), 0))
    small = pl.BlockSpec((1, HEAD_PAD), lambda b, j: (0, 0))
    return _tc_call(
        body, name=name, grid=(n_batch, ns),
        in_specs=[pl.BlockSpec((rows, KV_A), lambda b, j: (step(b, j), P_KA // KV_A)), small, cur, cur, nxt, cur, cur,
                  nxt],
        out_specs=[cur, cur, small],
        out_shape=[jax.ShapeDtypeStruct((t, KV_A), BF16), jax.ShapeDtypeStruct((t, KV_A), BF16),
                   jax.ShapeDtypeStruct((1, HEAD_PAD), F32)],
        compiler_params=_cp("arbitrary", "arbitrary"))(proj, kg2, dkc, dkp, dkp, dvc, dvp, dvp)


def _lora_norm(x, gain):
    r = lax.rsqrt(jnp.mean(x * x, axis=1, keepdims=True) + EPS)
    xh = x * r
    return xh * gain, xh, r


def _mla_in_specs(tm):
    row = lambda w, off: pl.BlockSpec((tm, w), lambda i: (i, off // w))
    one = lambda w: pl.BlockSpec((1, w), lambda i: (0, 0))
    full = lambda r, c: pl.BlockSpec((r, c), lambda i: (0, 0))
    tab = pl.BlockSpec((tm, HEAD_PAD), lambda i: (i, 0))
    return [row(Q_LORA, P_CQ), row(KV_LORA, P_CKV), row(HEAD_PAD, P_KR), tab, tab, tab,
            one(Q_LORA), one(KV_LORA), one(HEAD_PAD), one(HEAD_PAD),
            full(WIDTH_BP, Q_LORA), full(WIDTH_BP, KV_LORA), full(WIDTH_BP, KV_LORA)]


def _mla_pre(proj, tabs, gq, gkv, gqn, gkn, wuq, wk, wv, name):
    t = proj.shape[0]
    tm = _tile(t, (512, 256, 128))

    def body(cq_ref, ckv_ref, kr_ref, c_ref, sm_ref, sp_ref, gq_ref, gkv_ref, gqn_ref, gkn_ref,
             wuq_ref, wk_ref, wv_ref, q_ref, k_ref, v_ref):
        cqn = _lora_norm(cq_ref[...], gq_ref[...])[0].astype(BF16)
        ckvn = _lora_norm(ckv_ref[...], gkv_ref[...])[0].astype(BF16)
        q_raw = _dot(cqn, wuq_ref[...], _NT)
        k_raw = _dot(ckvn, wk_ref[...], _NT)
        v_ref[...] = _dot(ckvn, wv_ref[...], _NT).astype(BF16)
        kr = pltpu.roll(kr_ref[...], NOPE, 1)
        c, sm, sp = c_ref[...], sm_ref[...], sp_ref[...]
        heads = [slice(h * HEAD_PAD, (h + 1) * HEAD_PAD) for h in range(N_HEADS_B)]
        raw = [q_raw[:, hs] for hs in heads] + [k_raw[:, hs] + kr for hs in heads]
        gains = [gqn_ref[...]] * N_HEADS_B + [gkn_ref[...]] * N_HEADS_B
        sq = [jnp.sum(x * x, axis=1, keepdims=True) for x in raw]
        normed = [x * lax.rsqrt(s * (1.0 / QK_B) + EPS) * g for x, s, g in zip(raw, sq, gains)]
        up = [pltpu.roll(x, HEAD_PAD - ROPE // 2, 1) for x in normed]
        down = [pltpu.roll(x, ROPE // 2, 1) for x in normed]
        roped = [(x * c + u * sm + d * sp).astype(BF16) for x, u, d in zip(normed, up, down)]
        for h, hs in enumerate(heads):
            q_ref[:, hs] = roped[h]
            k_ref[:, hs] = roped[N_HEADS_B + h]

    o_spec = pl.BlockSpec((tm, WIDTH_BP), lambda i: (i, 0))
    o_shape = jax.ShapeDtypeStruct((t, WIDTH_BP), BF16)
    return _tc_call(
        body, name=name, grid=(t // tm,), in_specs=_mla_in_specs(tm), out_specs=[o_spec] * 3,
        out_shape=[o_shape] * 3, compiler_params=_cp("parallel"))(
            proj, proj, proj, *tabs, gq, gkv, gqn, gkn, wuq, wk, wv)


def _mla_pre_bwd(proj, tabs, gq, gkv, gqn, gkn, wuq, wk, wv, dq, dk, dv, name):
    t = proj.shape[0]
    tm = _tile(t, (512, 256, 128))

    def body(cq_ref, ckv_ref, kr_ref, c_ref, sm_ref, sp_ref, gq_ref, gkv_ref, gqn_ref, gkn_ref,
             wuq_ref, wk_ref, wv_ref, dq_ref, dk_ref, dv_ref,
             dcq_ref, dckv_ref, dkr_ref, dwuq_ref, dwk_ref, dwv_ref, dgq_ref, dgkv_ref, dgqn_ref, dgkn_ref,
             dqraw_ref, dkraw_ref):
        @pl.when(pl.program_id(0) == 0)
        def _():
            for r in (dwuq_ref, dwk_ref, dwv_ref, dgq_ref, dgkv_ref, dgqn_ref, dgkn_ref):
                r[...] = jnp.zeros_like(r)

        cqn_f, cq_xh, cq_r = _lora_norm(cq_ref[...], gq_ref[...])
        ckvn_f, ckv_xh, ckv_r = _lora_norm(ckv_ref[...], gkv_ref[...])
        cqn, ckvn = cqn_f.astype(BF16), ckvn_f.astype(BF16)
        q_raw = _dot(cqn, wuq_ref[...], _NT)
        k_raw = _dot(ckvn, wk_ref[...], _NT)
        kr = pltpu.roll(kr_ref[...], NOPE, 1)
        c, sm, sp = c_ref[...], sm_ref[...], sp_ref[...]
        heads = [slice(h * HEAD_PAD, (h + 1) * HEAD_PAD) for h in range(N_HEADS_B)]
        raw = [q_raw[:, hs] for hs in heads] + [k_raw[:, hs] + kr for hs in heads]
        d_out = [dq_ref[:, hs] for hs in heads] + [dk_ref[:, hs] for hs in heads]
        gains = [gqn_ref[...]] * N_HEADS_B + [gkn_ref[...]] * N_HEADS_B
        sq = [jnp.sum(x * x, axis=1, keepdims=True) for x in raw]
        rinv = [lax.rsqrt(s * (1.0 / QK_B) + EPS) for s in sq]
        xhat = [x * r for x, r in zip(raw, rinv)]
        down = [pltpu.roll(d * sm, ROPE // 2, 1) for d in d_out]
        up = [pltpu.roll(d * sp, HEAD_PAD - ROPE // 2, 1) for d in d_out]
        dn = [d * c + a + b for d, a, b in zip(d_out, down, up)]
        dgain = [jnp.sum(d * xh, axis=0, keepdims=True) for d, xh in zip(dn, xhat)]
        dxh = [d * g for d, g in zip(dn, gains)]
        inner = [jnp.sum(d * xh, axis=1, keepdims=True) * (1.0 / QK_B) for d, xh in zip(dxh, xhat)]
        d_raw = [r * (d - xh * s) for r, d, xh, s in zip(rinv, dxh, xhat, inner)]
        for h, hs in enumerate(heads):
            dqraw_ref[:, hs] = d_raw[h].astype(BF16)
            dkraw_ref[:, hs] = d_raw[N_HEADS_B + h].astype(BF16)
        dkr = sum(d_raw[N_HEADS_B + 1:], d_raw[N_HEADS_B])
        dgqn_ref[...] += sum(dgain[1:N_HEADS_B], dgain[0])
        dgkn_ref[...] += sum(dgain[N_HEADS_B + 1:], dgain[N_HEADS_B])
        lane = lax.broadcasted_iota(jnp.int32, (tm, HEAD_PAD), 1)
        dkr_ref[...] = jnp.where(lane < ROPE, pltpu.roll(dkr, HEAD_PAD - NOPE, 1), 0.0).astype(BF16)
        dqraw = dqraw_ref[...]
        dkraw = dkraw_ref[...]
        dvb = dv_ref[...].astype(BF16)
        dwuq_ref[...] += _dot(dqraw, cqn, _TN)
        dwk_ref[...] += _dot(dkraw, ckvn, _TN)
        dwv_ref[...] += _dot(dvb, ckvn, _TN)
        dcqn = _dot(dqraw, wuq_ref[...], _NN)
        dckvn = _dot(dkraw, wk_ref[...], _NN) + _dot(dvb, wv_ref[...], _NN)
        dgq_ref[...] += jnp.sum(dcqn * cq_xh, axis=0, keepdims=True)
        dgkv_ref[...] += jnp.sum(dckvn * ckv_xh, axis=0, keepdims=True)
        dxh = dcqn * gq_ref[...]
        dcq_ref[...] = (cq_r * (dxh - cq_xh * jnp.mean(dxh * cq_xh, axis=1, keepdims=True))).astype(BF16)
        dxh = dckvn * gkv_ref[...]
        dckv_ref[...] = (ckv_r * (dxh - ckv_xh * jnp.mean(dxh * ckv_xh, axis=1, keepdims=True))).astype(BF16)

    wide = pl.BlockSpec((tm, WIDTH_BP), lambda i: (i, 0))
    row = lambda w: pl.BlockSpec((tm, w), lambda i: (i, 0))
    full = lambda r, c: pl.BlockSpec((r, c), lambda i: (0, 0))
    return _tc_call(
        body, name=name, grid=(t // tm,), in_specs=_mla_in_specs(tm) + [wide, wide, wide],
        out_specs=[row(Q_LORA), row(KV_LORA), row(HEAD_PAD), full(WIDTH_BP, Q_LORA), full(WIDTH_BP, KV_LORA),
                   full(WIDTH_BP, KV_LORA), full(1, Q_LORA), full(1, KV_LORA), full(1, HEAD_PAD), full(1, HEAD_PAD)],
        out_shape=[jax.ShapeDtypeStruct((t, Q_LORA), BF16), jax.ShapeDtypeStruct((t, KV_LORA), BF16),
                   jax.ShapeDtypeStruct((t, HEAD_PAD), BF16), jax.ShapeDtypeStruct((WIDTH_BP, Q_LORA), F32),
                   jax.ShapeDtypeStruct((WIDTH_BP, KV_LORA), F32), jax.ShapeDtypeStruct((WIDTH_BP, KV_LORA), F32),
                   jax.ShapeDtypeStruct((1, Q_LORA), F32), jax.ShapeDtypeStruct((1, KV_LORA), F32),
                   jax.ShapeDtypeStruct((1, HEAD_PAD), F32), jax.ShapeDtypeStruct((1, HEAD_PAD), F32)],
        scratch_shapes=[pltpu.VMEM((tm, WIDTH_BP), BF16), pltpu.VMEM((tm, WIDTH_BP), BF16)],
        compiler_params=_cp("arbitrary"))(proj, proj, proj, *tabs, gq, gkv, gqn, gkn, wuq, wk, wv, dq, dk, dv)


def _mla_flash_specs(s_len):
    bh_spec = pl.BlockSpec((s_len, HEAD_PAD), lambda b, h: (b, h))
    lse_spec = pl.BlockSpec((1, s_len, 1), lambda b, h: (b * N_HEADS_B + h, 0, 0))
    return bh_spec, lse_spec


def _diag_mask(s):
    row = lax.broadcasted_iota(jnp.int32, s.shape, 0)
    col = lax.broadcasted_iota(jnp.int32, s.shape, 1)
    return jnp.where(row >= col, s, NEG)


def _mla_flash_fwd(q, k, v, n_batch, s_len, name):
    t = q.shape[0]
    tq = _tile(s_len, (256, 128))
    bh_spec, lse_spec = _mla_flash_specs(s_len)
    c = (QK_B ** -0.5) * LOG2E

    def body(q_ref, k_ref, v_ref, o_ref, lse_ref):
        nq = s_len // tq
        rows = [slice(i * tq, (i + 1) * tq) for i in range(nq)]
        below = [slice(0, i * tq) for i in range(nq)]
        qs = [q_ref[r, :] for r in rows]
        sd = [_diag_mask(_dot(qs[i], k_ref[rows[i], :], _NT)) for i in range(nq)]
        sb = [None] + [_dot(qs[i], k_ref[below[i], :], _NT) for i in range(1, nq)]
        m = [jnp.max(s, axis=1, keepdims=True) for s in sd]
        m = [m[0]] + [jnp.maximum(m[i], jnp.max(sb[i], axis=1, keepdims=True)) for i in range(1, nq)]
        pd = [jnp.exp2((sd[i] - m[i]) * c) for i in range(nq)]
        pb = [None] + [jnp.exp2((sb[i] - m[i]) * c) for i in range(1, nq)]
        l = [jnp.sum(p, axis=1, keepdims=True) for p in pd]
        l = [l[0]] + [l[i] + jnp.sum(pb[i], axis=1, keepdims=True) for i in range(1, nq)]
        acc = [_dot(pd[i].astype(BF16), v_ref[rows[i], :], _NN) for i in range(nq)]
        acc = [acc[0]] + [acc[i] + _dot(pb[i].astype(BF16), v_ref[below[i], :], _NN) for i in range(1, nq)]
        for i in range(nq):
            o_ref[rows[i], :] = (acc[i] * (1.0 / l[i])).astype(BF16)
            lse_ref[0, rows[i], :] = m[i] * c + jnp.log2(l[i])

    return _tc_call(
        body, name=name, grid=(n_batch, N_HEADS_B), in_specs=[bh_spec, bh_spec, bh_spec],
        out_specs=[bh_spec, lse_spec],
        out_shape=[jax.ShapeDtypeStruct((t, WIDTH_BP), BF16),
                   jax.ShapeDtypeStruct((n_batch * N_HEADS_B, s_len, 1), F32)],
        compiler_params=_cp("parallel", "parallel"))(q, k, v)


def _mla_flash_bwd(q, k, v, o, do, lse2, n_batch, s_len, name):
    t = q.shape[0]
    tq = _tile(s_len, (256, 128))
    bh_spec, lse_spec = _mla_flash_specs(s_len)
    scale = QK_B ** -0.5
    c = scale * LOG2E

    def body(q_ref, k_ref, v_ref, o_ref, do_ref, lse_ref, dq_ref, dk_ref, dv_ref):
        nq = s_len // tq
        rows = [slice(i * tq, (i + 1) * tq) for i in range(nq)]
        below = [slice(0, i * tq) for i in range(nq)]
        qs = [q_ref[r, :] for r in rows]
        dos = [do_ref[r, :] for r in rows]
        lse = [lse_ref[0, r, :] for r in rows]
        delta = [jnp.sum(dos[i].astype(F32) * o_ref[rows[i], :].astype(F32), axis=1, keepdims=True)
                 for i in range(nq)]

        def probs_and_ds(i, ks, diag):
            s = _dot(qs[i], k_ref[ks, :], _NT)
            if diag:
                s = _diag_mask(s)
            p = jnp.exp2(s * c - lse[i])
            dp = _dot(dos[i], v_ref[ks, :], _NT)
            return p.astype(BF16), (p * (dp - delta[i]) * scale).astype(BF16)

        diag = [probs_and_ds(i, rows[i], True) for i in range(nq)]
        rest = [None] + [probs_and_ds(i, below[i], False) for i in range(1, nq)]
        for i in range(nq):
            dq = _dot(diag[i][1], k_ref[rows[i], :], _NN)
            if i:
                dq = dq + _dot(rest[i][1], k_ref[below[i], :], _NN)
            dq_ref[rows[i], :] = dq
        for j in range(nq):
            later = slice(j * tq, s_len)
            p_j = jnp.concatenate([diag[j][0]] + [rest[i][0][:, rows[j]] for i in range(j + 1, nq)], axis=0)
            ds_j = jnp.concatenate([diag[j][1]] + [rest[i][1][:, rows[j]] for i in range(j + 1, nq)], axis=0)
            dk_ref[rows[j], :] = _dot(ds_j, q_ref[later, :], _TN)
            dv_ref[rows[j], :] = _dot(p_j, do_ref[later, :], _TN)

    f32_wide = jax.ShapeDtypeStruct((t, WIDTH_BP), F32)
    return _tc_call(
        body, name=name, grid=(n_batch, N_HEADS_B),
        in_specs=[bh_spec, bh_spec, bh_spec, bh_spec, bh_spec, lse_spec],
        out_specs=[bh_spec, bh_spec, bh_spec], out_shape=[f32_wide] * 3,
        compiler_params=_cp("parallel", "parallel"))(q, k, v, o, do, lse2)


def _swa_heads(w, axis, order):
    heads = [lax.slice_in_dim(w, h * HEAD_DIM_A, (h + 1) * HEAD_DIM_A, axis=axis) for h in order]
    return jnp.concatenate(heads, axis=axis)


class _LayerWeights:
    def __init__(self, build):
        self._build, self._mats = build, {}

    def __getitem__(self, name):
        if name not in self._mats:
            self._mats.update(self._build(name))
        return self._mats[name]


PIECES_OF = {"mla_w_ukv": ("wk", "wv")}


def _store(name, w):
    t = w.T if dict(BIG)[name] else w
    if name == "mla_w_uq":
        return {name: jnp.pad(t, ((0, HEAD_PAD - QK_B), (0, 0)))}
    if name == "mla_w_ukv":
        pad = ((0, HEAD_PAD - NOPE), (0, 0))
        return {"wk": jnp.pad(t[:NOPE], pad), "wv": jnp.pad(t[NOPE:], pad)}
    if name == "w_branch_b":
        t3 = jnp.pad(t.reshape(t.shape[0], N_HEADS_B, V_B), ((0, 0), (0, 0), (0, HEAD_PAD - V_B)))
        return {name: t3.reshape(t.shape[0], WIDTH_BP)}
    if name == "w_branch_a":
        return {name: _swa_heads(t, 1, SWA_HEAD_ORDER)}
    return {name: t}


def _unstore(name, pieces):
    if name == "mla_w_uq":
        return pieces[name][:QK_B]
    if name == "mla_w_ukv":
        return jnp.concatenate([pieces["wk"][:NOPE], pieces["wv"][:V_B]], axis=0)
    if name == "w_branch_b":
        g = pieces[name]
        return g.reshape(g.shape[0], N_HEADS_B, HEAD_PAD)[:, :, :V_B].reshape(g.shape[0], WIDTH_B)
    if name == "w_branch_a":
        return _swa_heads(pieces[name], 1, SWA_HEAD_INVERSE)
    return pieces[name]


def _layer_mats(w):
    if "w_in" not in w:
        return dict(w)
    w_in = w["w_in"]
    o = [0]
    for n in (WIDTH_A, KV_A, KV_A, Q_LORA, KV_LORA, ROPE, D_MODEL, D_MODEL):
        o.append(o[-1] + n)
    qa, ka, va, cq, ckv, kr, ga, gb = (w_in[o[i]:o[i + 1]] for i in range(8))
    pad = jnp.zeros((PROJ_W - IN_WIDTH, w_in.shape[1]), w_in.dtype)
    out = dict(w)
    out["w_in"] = jnp.concatenate([ga, gb, _swa_heads(qa, 0, SWA_HEAD_ORDER), cq, ka, va, ckv, kr, pad], axis=0)
    return out


def _unlayer_w_in(d):
    ga, gb, qa, cq, ka, va, ckv, kr = (d[a:b] for a, b in (
        (P_GA, P_GA + D_MODEL), (P_GB, P_GB + D_MODEL), (P_QA, P_QA + WIDTH_A), (P_CQ, P_CQ + Q_LORA),
        (P_KA, P_KA + KV_A), (P_VA, P_VA + KV_A), (P_CKV, P_CKV + KV_LORA), (P_KR, P_KR + ROPE)))
    return jnp.concatenate([_swa_heads(qa, 0, SWA_HEAD_INVERSE), ka, va, cq, ckv, kr, ga, gb], axis=0)


def _pad_lanes(v, width):
    return jnp.pad(v.reshape(1, -1), ((0, 0), (0, width - v.shape[-1])))


def _rope_tables(positions):
    half = ROPE // 2
    inv_freq = ROPE_BASE ** (-jnp.arange(half, dtype=F32) / half)
    ang = positions.astype(F32).reshape(-1, 1) * inv_freq
    cos, sin = jnp.cos(ang), jnp.sin(ang)
    t = cos.shape[0]
    one, zero = jnp.ones((t, NOPE), F32), jnp.zeros((t, NOPE), F32)
    tail = jnp.zeros((t, HEAD_PAD - QK_B), F32)
    z16 = jnp.zeros((t, half), F32)
    c = jnp.concatenate([one, cos, cos, tail], axis=1)
    sm = jnp.concatenate([zero, -sin, z16, tail], axis=1)
    sp = jnp.concatenate([zero, z16, sin, tail], axis=1)
    return c, sm, sp


def _ffn_fwd(x, gain, wg_t, wu_t, wd, tag):
    n, a, b, hmid = _ffn_up(x, gain, wg_t, wu_t, f"{tag}_up")
    y = _mm([(hmid, wd)], "nn", F32, f"{tag}_down", residual=x, alpha=0.5)
    return y, (x, n, a, b, hmid)


def _ffn_bwd(dy, dyb, saved, gain, wg_t, wu_t, wd, tag, grads, names, hook):
    x, n, a, b, hmid = saved
    da, db = _ffn_down_bwd(dyb, wd, a, b, f"{tag}_down_bwd")
    grads[names[0]] = _mm([(da, n)], "tn", BF16, f"{tag}_dwg")
    grads[names[1]] = _mm([(db, n)], "tn", BF16, f"{tag}_dwu")
    hook("gu", grads[names[1]])
    dx, dxb, g_gain = _mm_rms_bwd([(da, wg_t), (db, wu_t)], x, gain, dy, f"{tag}_dn")
    hook("dn", dx)
    grads[names[2]] = _mm([(hmid, dyb)], "tn", BF16, f"{tag}_dwd", alpha=0.5)
    return dx, dxb, g_gain


def _fold_halves(d):
    return d[:, :HEAD_DIM_A] + d[:, HEAD_DIM_A:]


def _local_step(x, positions, target, layers, smalls, at=None):
    at = at or (lambda point, l, token, grads: None)
    _Order.tokens = ()
    n_batch, s_len, d = x.shape
    t = n_batch * s_len
    xt = x.reshape(t, d)
    tabs = _rope_tables(positions)
    pos_col = positions.reshape(t, 1)
    pos_row = positions.reshape(t // BLOCK, 1, BLOCK)
    saved = []
    get_layer = layers if callable(layers) else layers.__getitem__
    for l in range(len(smalls)):
        w, s = get_layer(l), smalls[l]
        g1, gm, g2 = (s[k].reshape(1, d) for k in ("ffn1_norm", "mix_norm", "ffn2_norm"))
        qg2, kg2 = (jnp.tile(s[k].reshape(1, -1), (1, 2)) for k in ("swa_q_norm", "swa_k_norm"))
        sinks = _pad_lanes(s["swa_sinks"], HEAD_PAD)
        gq, gkv = s["mla_q_lora_norm"].reshape(1, -1), s["mla_kv_lora_norm"].reshape(1, -1)
        gqn, gkn = _pad_lanes(s["mla_q_norm"], HEAD_PAD), _pad_lanes(s["mla_k_norm"], HEAD_PAD)
        x1, sv1 = _ffn_fwd(xt, g1, w["ffn1_w_gate"], w["ffn1_w_up"], w["ffn1_w_down"], f"l{l}_ffn1")
        h, proj = _rms_mm(x1, gm, w["w_in"], f"l{l}_proj")
        at("proj", l, proj, None)
        oa = _swa_fwd(proj, pos_col, pos_row, qg2, kg2, sinks, n_batch, s_len, f"l{l}_swa")
        q, k, v = _mla_pre(proj, tabs, gq, gkv, gqn, gkn, w["mla_w_uq"], w["wk"], w["wv"], f"l{l}_mla_pre")
        ob, lse = _mla_flash_fwd(q, k, v, n_batch, s_len, f"l{l}_mla")
        merged, ya, yb = _merge_fwd(oa, ob, proj, w["w_branch_a"], w["w_branch_b"], f"l{l}_merge")
        x2 = _mm([(merged, w["w_out"])], "nn", F32, f"l{l}_out", residual=x1)
        at("out", l, x2, None)
        x3, sv2 = _ffn_fwd(x2, g2, w["ffn2_w_gate"], w["ffn2_w_up"], w["ffn2_w_down"], f"l{l}_ffn2")
        saved.append((w, sv1, sv2, x1, h, proj, oa, q, k, v, ob, lse, merged, ya, yb,
                      (g1, gm, g2, qg2, kg2, sinks, gq, gkv, gqn, gkn)))
        xt = x3

    dy, dyb, loss = _loss_head(xt, target.reshape(t, d))

    big_grads, small_grads = [None] * len(smalls), [None] * len(smalls)
    for l in reversed(range(len(smalls))):
        w, sv1, sv2, x1, h, proj, oa, q, k, v, ob, lse, merged, ya, yb, gains = saved[l]
        g1, gm, g2, qg2, kg2, sinks, gq, gkv, gqn, gkn = gains
        bg, sg = {}, {}
        dy, dyb, sg["ffn2_norm"] = _ffn_bwd(
            dy, dyb, sv2, g2, w["ffn2_w_gate"], w["ffn2_w_up"], w["ffn2_w_down"], f"l{l}_ffn2", bg, FFN2,
            lambda point, token, l=l, bg=bg: at("ffn2_" + point, l, token, bg))
        dya, dyb_, dga, dgb, doa, dob = _merge_bwd(dyb, w["w_out"], proj, ya, yb, w["w_branch_a"], w["w_branch_b"],
                                                   f"l{l}_merge_bwd")
        at("mixer", l, sg["ffn2_norm"], bg)
        bg["w_out"] = _mm([(merged, dyb)], "tn", BF16, f"l{l}_dwo")
        bg["w_branch_a"] = _mm([(dya, oa)], "tn", BF16, f"l{l}_dwa")
        bg["w_branch_b"] = _mm([(dyb_, ob)], "tn", BF16, f"l{l}_dwb")
        dqa, dkc, dkp, dvc, dvp, dqg, dsk = _swa_bwd(
            proj, pos_col, pos_row, qg2, kg2, sinks, doa, n_batch, s_len, f"l{l}_swa_bwd")
        sg["swa_q_norm"], sg["swa_sinks"] = _fold_halves(dqg), dsk[:, :N_HEADS_A]
        at("mixer_mid", l, dqa, bg)
        dka, dva, dkg = _swa_kv_bwd(proj, kg2, dkc, dkp, dvc, dvp, n_batch, s_len, f"l{l}_swa_kv_bwd")
        sg["swa_k_norm"] = _fold_halves(dkg)
        dq, dk, dv = _mla_flash_bwd(q, k, v, ob, dob, lse, n_batch, s_len, f"l{l}_mla_bwd")
        (dcq, dckv, dkr, g_uq, g_wk, g_wv, sg["mla_q_lora_norm"], sg["mla_kv_lora_norm"], dgqn, dgkn) = _mla_pre_bwd(
            proj, tabs, gq, gkv, gqn, gkn, w["mla_w_uq"], w["wk"], w["wv"], dq, dk, dv, f"l{l}_mla_pre_bwd")
        sg["mla_q_norm"], sg["mla_k_norm"] = dgqn[:, :QK_B], dgkn[:, :QK_B]
        bg["mla_w_uq"], bg["wk"], bg["wv"] = g_uq.astype(BF16), g_wk.astype(BF16), g_wv.astype(BF16)
        dproj = (dga, dgb, dqa, dcq, dka, dva, dckv, dkr)
        bg["w_in"] = _mm([(dproj, h)], "tn", BF16, f"l{l}_dwin")
        dy, dyb, sg["mix_norm"] = _mm_rms_bwd([(dproj, w["w_in"])], x1, gm, dy, f"l{l}_dh")
        at("ffn1", l, sg["mix_norm"], bg)
        dy, dyb, sg["ffn1_norm"] = _ffn_bwd(
            dy, dyb, sv1, g1, w["ffn1_w_gate"], w["ffn1_w_up"], w["ffn1_w_down"], f"l{l}_ffn1", bg, FFN1,
            lambda point, token, l=l, bg=bg: at("ffn1_" + point, l, token, bg))
        big_grads[l], small_grads[l] = bg, sg
        at("done", l, dy, bg)
    return loss, dy.reshape(n_batch, s_len, d), big_grads, small_grads


def _round_up(n, m):
    return (n + m - 1) // m * m


def _flat_layout(piece_shapes, members, row_tile):
    table, off = [], 0
    for l, piece in members:
        rows, k = piece_shapes[piece]
        pr = _round_up(rows * k // LANES, 16)
        table.append(((l, piece), off, pr, rows, k))
        off += pr
    return table, _round_up(off, row_tile)


def _pack_flat(stored, table, total):
    parts, off = [], 0
    for key, o, pr, rows, k in table:
        w = stored[key].reshape(rows * k // LANES, LANES)
        parts.append(jnp.pad(w, ((0, pr - w.shape[0]), (0, 0))))
        off = o + pr
    if total > off:
        parts.append(jnp.zeros((total - off, LANES), parts[0].dtype))
    return jnp.concatenate(parts, axis=0)


def _unpack_flat(flat, table):
    return {key: flat[o:o + rows * k // LANES].reshape(rows, k) for key, o, pr, rows, k in table}


def _gathered_mats(gathered, table, layer):
    return {piece: gathered[:, o:o + rows * k // LANES].reshape(N_DEV * rows, k)
            for (l, piece), o, pr, rows, k in table if l == layer}


def _pack_grads(grads, table, total):
    parts, off = [], 0
    for key, o, pr, rows, k in table:
        g = grads[key].reshape(N_DEV, rows * k // LANES, LANES)
        parts.append(jnp.pad(g, ((0, 0), (0, pr - g.shape[1]), (0, 0))))
        off = o + pr
    if total > off:
        parts.append(jnp.zeros((N_DEV, total - off, LANES), BF16))
    return jnp.concatenate(parts, axis=1)


def _pack_small(params, last=None):
    parts = [params[n][l].reshape(-1).astype(F32) for l in range(DEPTH) for n in SMALL]
    v = jnp.concatenate(parts)
    v = jnp.pad(v, (0, SMALL_ROWS * LANES - 1 - v.shape[0]))
    last = jnp.zeros((1,), F32) if last is None else last.reshape(1)
    return jnp.concatenate([v, last]).reshape(SMALL_ROWS, LANES)


def _unpack_small(flat, shapes):
    v, out, off = flat.reshape(-1), {}, 0
    for l in range(DEPTH):
        for n in SMALL:
            sz = math.prod(shapes[n][1:])
            out.setdefault(n, []).append(v[off:off + sz].reshape(shapes[n][1:]))
            off += sz
    return {n: jnp.stack(p) for n, p in out.items()}


_MESH = pl.DeviceIdType.MESH


def _place():
    return lax.axis_index("x"), lax.axis_index("y"), lax.axis_index("c")


def _handshake(peers):
    barrier = pltpu.get_barrier_semaphore()
    for peer in peers:
        pl.semaphore_signal(barrier, inc=1, device_id=peer, device_id_type=_MESH)
    pl.semaphore_wait(barrier, len(peers))


def _comm_call(body, out_shape, scratch, name, seq_id, spec=_ANY):
    if seq_id is None:
        return pl.pallas_call(body, name=name, out_shape=out_shape, in_specs=[spec, _ANY], out_specs=spec,
                              scratch_shapes=scratch)
    nbytes = LINK_COST_SCALE * math.prod(out_shape.shape) * out_shape.dtype.itemsize
    return pl.kernel(body, out_type=out_shape, mesh=plsc.ScalarSubcoreMesh(axis_name="sequencer", num_cores=1),
                     scratch_types=scratch, name=name, compiler_params=pltpu.CompilerParams(collective_id=seq_id),
                     cost_estimate=pl.CostEstimate(flops=0, transcendentals=0, bytes_accessed=nbytes))


def _all_gather(x_shard, name, vmem=False, seq_id=None, after=None):
    spec = pl.BlockSpec(memory_space=pltpu.VMEM) if vmem else _ANY

    def body(x_ref, after_ref, out_ref, send_sems, recv_sems, local_sem):
        x, y, c = _place()
        me, sibling = (x, y, c), (x, y, 1 - c)
        chips = [(1 - x, y), (x, 1 - y), (1 - x, 1 - y)]
        if seq_id is not None:
            _handshake([sibling] + [(*chip, c) for chip in chips])

        def rows(px, py, pc):
            return out_ref.at[4 * px + 2 * py + pc]

        def copy(k, block, to, src=None):
            return pltpu.make_async_remote_copy(
                src_ref=rows(*block) if src is None else src, dst_ref=rows(*block),
                send_sem=send_sems.at[k], recv_sem=recv_sems.at[k], device_id=to, device_id_type=_MESH)

        mine = pltpu.make_async_copy(x_ref, rows(*me), local_sem)
        mine.start()
        first = [copy(0, me, sibling, src=x_ref)]
        first += [copy(1 + j, me, (*chip, c), src=x_ref) for j, chip in enumerate(chips)]
        for cp in first:
            cp.start()
        passed = [copy(4 + j, (*chip, c), sibling) for j, chip in enumerate(chips)]
        for j, chip in enumerate(chips):
            copy(1 + j, (*chip, c), me).wait_recv()
            passed[j].start()
        copy(0, sibling, me).wait_recv()
        for j, chip in enumerate(chips):
            copy(4 + j, (*chip, 1 - c), me).wait_recv()
        for cp in first + passed:
            cp.wait_send()
        mine.wait()

    return _comm_call(
        body, jax.ShapeDtypeStruct((N_DEV,) + x_shard.shape, x_shard.dtype),
        [pltpu.SemaphoreType.DMA((7,)), pltpu.SemaphoreType.DMA((7,)), pltpu.SemaphoreType.DMA], name, seq_id,
        spec)(x_shard, x_shard if after is None else after)


def _all_gather_halves(x_shard, name, seq_id):
    r, w = x_shard.shape
    h = r // 2
    assert r % 32 == 0, r

    def body(x_ref, after_ref, out_ref, send_sems, recv_sems, local_sem):
        x, y, c = _place()
        sibling = (x, y, 1 - c)
        n1x, n1y = x + c * (1 - 2 * x), y + (1 - c) * (1 - 2 * y)
        n2x, n2y = x + (1 - c) * (1 - 2 * x), y + c * (1 - 2 * y)
        _handshake([sibling, (n1x, n1y, c), (n2x, n2y, c)])
        mine = pl.ds((1 - c) * h, h)
        other = pl.ds(c * h, h)

        def pair(px, py, rows):
            return out_ref.at[pl.ds(4 * px + 2 * py, 2), rows]

        def copy(k, ref, to, src=None):
            return pltpu.make_async_remote_copy(
                src_ref=ref if src is None else src, dst_ref=ref, send_sem=send_sems.at[k], recv_sem=recv_sems.at[k],
                device_id=to, device_id_type=_MESH)

        own = pltpu.make_async_copy(x_ref, out_ref.at[4 * x + 2 * y + c], local_sem)
        own.start()
        trade = copy(0, out_ref.at[4 * x + 2 * y + c], sibling, src=x_ref)
        trade.start()
        own.wait()
        copy(0, out_ref.at[4 * x + 2 * y + 1 - c], sibling).wait_recv()
        first = copy(1, pair(x, y, mine), (n1x, n1y, c))
        first.start()
        copy(1, pair(n1x, n1y, mine), (n1x, n1y, c)).wait_recv()
        second = [copy(2, pair(x, y, mine), (n2x, n2y, c)), copy(3, pair(n1x, n1y, mine), (n2x, n2y, c))]
        for cp in second:
            cp.start()
        copy(2, pair(n2x, n2y, mine), (n2x, n2y, c)).wait_recv()
        copy(3, pair(1 - x, 1 - y, mine), (n2x, n2y, c)).wait_recv()
        remote = [(n1x, n1y), (n2x, n2y), (1 - x, 1 - y)]
        last = [copy(4 + j, pair(px, py, mine), sibling) for j, (px, py) in enumerate(remote)]
        for cp in last:
            cp.start()
        for j, (px, py) in enumerate([(n2x, n2y), (n1x, n1y), (1 - x, 1 - y)]):
            copy(4 + j, pair(px, py, other), sibling).wait_recv()
        for cp in [trade, first] + second + last:
            cp.wait_send()

    return _comm_call(
        body, jax.ShapeDtypeStruct((N_DEV, r, w), x_shard.dtype),
        [pltpu.SemaphoreType.DMA((7,)), pltpu.SemaphoreType.DMA((7,)), pltpu.SemaphoreType.DMA], name, seq_id,
    )(x_shard, x_shard)


def _exchange_cores(g4, name, seq_id=None, after=None):
    n_chip, _, r, w = g4.shape

    def body(g_ref, after_ref, out_ref, send_sems, recv_sems):
        x, y, c = _place()
        if seq_id is not None:
            _handshake([(x, y, 1 - c)])
        copies = [pltpu.make_async_remote_copy(
            src_ref=g_ref.at[q, 1 - c], dst_ref=out_ref.at[q], send_sem=send_sems.at[q], recv_sem=recv_sems.at[q],
            device_id=(x, y, 1 - c), device_id_type=_MESH) for q in range(n_chip)]
        for cp in copies:
            cp.start()
        for cp in copies:
            cp.wait()

    return _comm_call(
        body, jax.ShapeDtypeStruct((n_chip, r, w), g4.dtype),
        [pltpu.SemaphoreType.DMA((n_chip,)), pltpu.SemaphoreType.DMA((n_chip,))], name, seq_id)(g4, g4 if after is None else after)


def _exchange_chips(s1, name, seq_id=None):
    _, r, w = s1.shape

    def body(s_ref, after_ref, out_ref, send_sems, recv_sems):
        x, y, c = _place()
        chips = [(1 - x, y), (x, 1 - y), (1 - x, 1 - y)]
        if seq_id is not None:
            _handshake([(*chip, c) for chip in chips])
        copies = []
        for k, (tx, ty) in enumerate(chips):
            copies.append(pltpu.make_async_remote_copy(
                src_ref=s_ref.at[2 * tx + ty], dst_ref=out_ref.at[k], send_sem=send_sems.at[k],
                recv_sem=recv_sems.at[k], device_id=(tx, ty, c), device_id_type=_MESH))
        for cp in copies:
            cp.start()
        for cp in copies:
            cp.wait()

    return _comm_call(
        body, jax.ShapeDtypeStruct((3, r, w), s1.dtype),
        [pltpu.SemaphoreType.DMA((3,)), pltpu.SemaphoreType.DMA((3,))], name, seq_id)(s1, s1)


def _chip_sum(g4, recv, core, after, name, tr):
    n_chip, _, r, w = g4.shape

    def body(c_ref, a_ref, b_ref, after_ref, o_ref):
        o_ref[...] = (a_ref[...].astype(F32) + b_ref[...].astype(F32)).astype(o_ref.dtype)

    grid_spec = pltpu.PrefetchScalarGridSpec(
        num_scalar_prefetch=1, grid=(n_chip, r // tr),
        in_specs=[pl.BlockSpec((None, None, tr, w), lambda q, i, c: (q, c[0], i, 0)),
                  pl.BlockSpec((None, tr, w), lambda q, i, c: (q, i, 0)), _ANY],
        out_specs=pl.BlockSpec((None, tr, w), lambda q, i, c: (q, i, 0)))
    return pl.pallas_call(
        body, name=name, grid_spec=grid_spec, out_shape=jax.ShapeDtypeStruct((n_chip, r, w), g4.dtype),
        compiler_params=_cp("parallel", "parallel"))(core, g4, recv, after)


def _adam(w, g, m, v):
    m = ADAM_B1 * m + (1.0 - ADAM_B1) * g
    v = ADAM_B2 * v + (1.0 - ADAM_B2) * (g * g)
    m_hat = m / (1.0 - ADAM_B1 ** ADAM_STEP)
    v_hat = v / (1.0 - ADAM_B2 ** ADAM_STEP)
    delta = -ADAM_LR * (m_hat / (jnp.sqrt(v_hat) + ADAM_EPS) + ADAM_WD * w)
    return delta, m, v


def _grad_sum(s1, r2, chip, name, tr):
    _, r, lanes = s1.shape

    def body(c_ref, s_ref, r0_ref, r1_ref, r2_ref, g_out):
        g_out[...] = ((s_ref[...].astype(F32) + r0_ref[...].astype(F32)) + r1_ref[...].astype(F32)) + r2_ref[
            ...].astype(F32)

    row = pl.BlockSpec((tr, lanes), lambda i, c: (i, 0))
    rel = lambda k: pl.BlockSpec((None, tr, lanes), lambda i, c: (k, i, 0))
    grid_spec = pltpu.PrefetchScalarGridSpec(
        num_scalar_prefetch=1, grid=(r // tr,),
        in_specs=[pl.BlockSpec((None, tr, lanes), lambda i, c: (c[0], i, 0)), rel(0), rel(1), rel(2)], out_specs=row)
    return pl.pallas_call(
        body, name=name, grid_spec=grid_spec, out_shape=jax.ShapeDtypeStruct((r, lanes), F32),
        compiler_params=_cp("parallel"))(chip, s1, r2, r2, r2)


def _adam_big(w, g, m, v, name):
    depth, k, n = w.shape
    tk = k if k <= 512 else _tile(k, (256, 128))

    def body(w_ref, g_ref, m_ref, v_ref, d_out, m_out, v_out):
        d, mn, vn = _adam(w_ref[...], g_ref[...], m_ref[...], v_ref[...])
        d_out[...] = d
        m_out[...] = mn
        v_out[...] = vn

    blk = pl.BlockSpec((None, tk, n), lambda l, i: (l, i, 0))
    return pl.pallas_call(
        body, name=name, grid=(depth, k // tk), in_specs=[blk] * 4, out_specs=[blk] * 3,
        out_shape=[jax.ShapeDtypeStruct(w.shape, F32)] * 3, compiler_params=_cp("parallel", "parallel"))(w, g, m, v)


def _adam_small(parts, w, m, v, name):
    rows, lanes = w.shape

    def body(p_ref, w_ref, m_ref, v_ref, g_out, d_out, m_out, v_out):
        g = p_ref[0:rows, :]
        for dev in range(1, N_DEV):
            g = g + p_ref[dev * rows:(dev + 1) * rows, :]
        d, mn, vn = _adam(w_ref[...], g, m_ref[...], v_ref[...])
        g_out[...] = g
        d_out[...] = d
        m_out[...] = mn
        v_out[...] = vn

    return pl.pallas_call(
        body, name=name, out_shape=[jax.ShapeDtypeStruct((rows, lanes), F32)] * 4)(parts, w, m, v)


def kernel(x, positions, ffn1_norm, ffn1_w_gate, ffn1_w_up, ffn1_w_down, mix_norm, w_in, swa_q_norm, swa_k_norm, swa_sinks, mla_q_lora_norm, mla_w_uq, mla_kv_lora_norm, mla_w_ukv, mla_q_norm, mla_k_norm, w_branch_a, w_branch_b, w_out, ffn2_norm, ffn2_w_gate, ffn2_w_up, ffn2_w_down, loss_target, m_ffn1_norm, m_ffn1_w_gate, m_ffn1_w_up, m_ffn1_w_down, m_mix_norm, m_w_in, m_swa_q_norm, m_swa_k_norm, m_swa_sinks, m_mla_q_lora_norm, m_mla_w_uq, m_mla_kv_lora_norm, m_mla_w_ukv, m_mla_q_norm, m_mla_k_norm, m_w_branch_a, m_w_branch_b, m_w_out, m_ffn2_norm, m_ffn2_w_gate, m_ffn2_w_up, m_ffn2_w_down, v_ffn1_norm, v_ffn1_w_gate, v_ffn1_w_up, v_ffn1_w_down, v_mix_norm, v_w_in, v_swa_q_norm, v_swa_k_norm, v_swa_sinks, v_mla_q_lora_norm, v_mla_w_uq, v_mla_kv_lora_norm, v_mla_w_ukv, v_mla_q_norm, v_mla_k_norm, v_w_branch_a, v_w_branch_b, v_w_out, v_ffn2_norm, v_ffn2_w_gate, v_ffn2_w_up, v_ffn2_w_down):
    given = dict(locals())
    params = {n: given[n] for n in WEIGHTS}
    mom1 = {n: given["m_" + n] for n in WEIGHTS}
    mom2 = {n: given["v_" + n] for n in WEIGHTS}
    assert N_HEADS_B == N_DEV
    stored = {(l, piece): w for l in range(DEPTH) for n, _ in BIG for piece, w in _store(n, params[n][l]).items()}
    piece_shapes = {piece: w.shape for (l, piece), w in stored.items() if l == 0}
    gsegs = [(members, tile) + _flat_layout(piece_shapes, members, tile) for members, tile in GATHER_SEGMENTS]
    rsegs = [(members, tile) + _flat_layout(piece_shapes, members, tile) for members, tile in SCATTER_SEGMENTS]

    def members_of(seg, l):
        return [n for sl, n in seg[0] if sl == l]

    cx, cy, cc = _place()
    core = jnp.reshape(cc, (1,)).astype(jnp.int32)
    chip = jnp.reshape(2 * cx + cy, (1,)).astype(jnp.int32)

    gathered = []
    for s, (_, _, table, total) in enumerate(gsegs):
        w_flat = _pack_flat(stored, table, total).astype(BF16)
        gathered.append(_all_gather_halves(w_flat, f"gather_s{s}", SEQ_IDS["gather", s]))

    def get_layer(l):
        def build(name):
            for seg, g in zip(gsegs, gathered):
                if name in members_of(seg, l):
                    return _layer_mats(_gathered_mats(g, seg[2], l))
            raise KeyError(name)
        return _LayerWeights(build)

    smalls = [{n: params[n][l] for n in SMALL} for l in range(DEPTH)]

    pending, big_out, layer_grads = {}, [None] * len(rsegs), {}

    def exchange_cores(s):
        _, _, table, total = rsegs[s]
        mine = {(l, n): _unlayer_w_in(layer_grads[l][n]) if n == "w_in" else layer_grads[l][n] for l, n in rsegs[s][0]}
        g_flat = _pack_grads(mine, table, total)
        g4 = g_flat.reshape(N_DEV // 2, 2, total, LANES)
        pending[s] = (g4, _exchange_cores(g4, f"scatter_cores_s{s}", seq_id=SEQ_IDS["cores", s]))

    def exchange_chips(s, after):
        g4, from_core = pending.pop(s)
        s1 = _chip_sum(g4, from_core, core, after, f"sum_cores_s{s}", rsegs[s][1])
        _Order.tokens = (s1,)
        pending[s] = (s1, _exchange_chips(s1, f"scatter_chips_s{s}", seq_id=SEQ_IDS["chips", s]))

    def finish(s):
        s1, from_chips = pending.pop(s)
        big_out[s] = _grad_sum(s1, from_chips, chip, f"grad_sum_s{s}", rsegs[s][1])

    plan = {("ffn1", 1): [("cores", 4)], ("ffn1_gu", 1): [("chips", 4)],
            ("mixer", 0): [("wait", 4), ("cores", 3)], ("mixer_mid", 0): [("chips", 3)],
            ("ffn1", 0): [("wait", 3), ("cores", 2)], ("ffn1_gu", 0): [("cores", 1), ("chips", 2)],
            ("ffn1_dn", 0): [("chips", 1)], ("done", 0): [("cores", 0)]}

    def at(point, l, token, grads):
        if grads is not None:
            layer_grads[l] = grads
        for what, s in plan.get((point, l), ()):
            if what == "cores":
                exchange_cores(s)
            elif what == "chips":
                exchange_chips(s, token)
            else:
                _Order.tokens += (pending[s][1],)

    loss, grad_x, _, small_grads = _local_step(x, positions, loss_target, get_layer, smalls, at)
    exchange_chips(0, grad_x)

    g_small = _pack_small({n: [small_grads[l][n] for l in range(DEPTH)] for n in SMALL}, loss)
    parts = _all_gather(g_small, "gather_small", vmem=True).reshape(N_DEV * SMALL_ROWS, LANES)
    small_out = _adam_small(parts, _pack_small(params), _pack_small(mom1), _pack_small(mom2), "adam_small")
    shapes = {n: params[n].shape for n in SMALL}
    outs = [_unpack_small(small, shapes) for small in small_out]
    loss = small_out[0].reshape(-1)[-1]

    pieces = {}
    for s in reversed(range(len(rsegs))):
        finish(s)
        pieces.update(_unpack_flat(big_out[s], rsegs[s][2]))
    last = SCATTER_SEGMENTS[0][0][0][1]
    for n, tr in sorted(BIG, key=lambda entry: entry[0] == last):
        view = (lambda a: jnp.swapaxes(a, 1, 2)) if tr else (lambda a: a)
        g = jnp.stack([_unstore(n, {p: pieces[l, p] for p in PIECES_OF.get(n, (n,))}) for l in range(DEPTH)])
        updated = _adam_big(view(params[n]), g, view(mom1[n]), view(mom2[n]), f"adam_{n}")
        for tree, leaf in zip(outs, (g,) + tuple(updated)):
            tree[n] = view(leaf)
    return (loss, grad_x, *[o[n] for o in outs for n in WEIGHTS])
```

```python
import math

import jax
import jax.numpy as jnp
from jax import lax
from jax.experimental import pallas as pl
from jax.experimental.pallas import tpu as pltpu
from jax.experimental.pallas import tpu_sc as plsc

F32 = jnp.float32
BF16 = jnp.bfloat16

N_DEV = 8
DEPTH = 2
D_MODEL = 1024
D_FF = 2816
HEAD_DIM_A = 64
N_HEADS_A = 8
N_KV_HEADS_A = 2
GROUP_A = N_HEADS_A // N_KV_HEADS_A
BLOCK = 128
N_HEADS_B = 8
Q_LORA = 256
KV_LORA = 128
NOPE = 64
ROPE = 32
QK_B = NOPE + ROPE
V_B = 64
HEAD_PAD = 128
WIDTH_A = N_HEADS_A * HEAD_DIM_A
WIDTH_B = N_HEADS_B * V_B
WIDTH_BP = N_HEADS_B * HEAD_PAD
KV_A = N_KV_HEADS_A * HEAD_DIM_A
IN_WIDTH = WIDTH_A + 2 * KV_A + Q_LORA + KV_LORA + ROPE + 2 * D_MODEL
ROPE_BASE = 10000.0
EPS = 1e-6
NEG = -1e30
LOG2E = 1.4426950408889634

P_GA, P_GB, P_QA, P_CQ, P_KA, P_VA, P_CKV, P_KR = 0, 1024, 2048, 2560, 2816, 2944, 3072, 3200
PROJ_W = 3328
SWA_HEAD_ORDER = (0, 4, 1, 5, 2, 6, 3, 7)
SWA_HEAD_INVERSE = tuple(SWA_HEAD_ORDER.index(h) for h in range(N_HEADS_A))

ADAM_LR, ADAM_B1, ADAM_B2, ADAM_EPS, ADAM_WD, ADAM_STEP = 0.001, 0.9, 0.999, 1e-08, 0.01, 10

VMEM_LIMIT = 56 * 1024 * 1024
LANES = 1024

BIG = (("ffn1_w_gate", True), ("ffn1_w_up", True), ("ffn1_w_down", False), ("w_in", True), ("mla_w_uq", True),
       ("mla_w_ukv", True), ("w_branch_a", True), ("w_branch_b", True), ("w_out", False),
       ("ffn2_w_gate", True), ("ffn2_w_up", True), ("ffn2_w_down", False))
SMALL = ("ffn1_norm", "mix_norm", "ffn2_norm", "swa_q_norm", "swa_k_norm", "swa_sinks", "mla_q_lora_norm",
         "mla_kv_lora_norm", "mla_q_norm", "mla_k_norm")
WEIGHTS = ("ffn1_norm", "ffn1_w_gate", "ffn1_w_up", "ffn1_w_down", "mix_norm", "w_in", "swa_q_norm", "swa_k_norm",
           "swa_sinks", "mla_q_lora_norm", "mla_w_uq", "mla_kv_lora_norm", "mla_w_ukv", "mla_q_norm", "mla_k_norm",
           "w_branch_a", "w_branch_b", "w_out", "ffn2_norm", "ffn2_w_gate", "ffn2_w_up", "ffn2_w_down")
SMALL_ROWS = 8
FFN1 = ("ffn1_w_gate", "ffn1_w_up", "ffn1_w_down")
FFN2 = ("ffn2_w_gate", "ffn2_w_up", "ffn2_w_down")
MIXER = ("w_in", "mla_w_uq", "wk", "wv", "w_branch_a", "w_branch_b", "w_out")
GATHER_SEGMENTS = (
    (tuple((0, n) for n in FFN1[:2]), 352),
    (((0, FFN1[2]),), 352),
    (tuple((0, n) for n in MIXER), 400),
    (tuple((0, n) for n in FFN2) + tuple((1, n) for n in FFN1), 352),
    (tuple((1, n) for n in MIXER + FFN2), 464),
)
SCATTER_SEGMENTS = (
    (((0, FFN1[2]),), 352),
    (tuple((0, n) for n in FFN1[:2]), 352),
    (tuple((0, n) for n in MIXER), 400),
    (tuple((0, n) for n in FFN2) + tuple((1, n) for n in FFN1), 352),
    (tuple((1, n) for n in MIXER + FFN2), 464),
)
LINK_COST_SCALE = 64
GATHER_COST_SCALE = 8
SEQ_IDS = {(kind, s): 1 + 5 * k + s for k, kind in enumerate(("gather", "cores", "chips")) for s in range(5)}


def _cp(*sem):
    return pltpu.CompilerParams(dimension_semantics=sem, vmem_limit_bytes=VMEM_LIMIT)


def _tile(n, prefs):
    for t in prefs:
        if n % t == 0:
            return t
    return n


def _dot(a, b, dims):
    return lax.dot_general(a, b, (dims, ((), ())), preferred_element_type=F32)


_NT = ((1,), (1,))
_NN = ((1,), (0,))
_TN = ((0,), (0,))


_ANY = pl.BlockSpec(memory_space=pl.ANY)


class _Order:
    tokens = ()


def _tc_call(body, *, in_specs, **kw):
    def run(*args):
        tokens, n = _Order.tokens, len(args)
        if not tokens:
            out = pl.pallas_call(body, in_specs=in_specs, **kw)(*args)
        else:
            def chained(*refs):
                return body(*refs[:n], *refs[n + len(tokens):])
            out = pl.pallas_call(chained, in_specs=list(in_specs) + [_ANY] * len(tokens), **kw)(*args, *tokens)
        _Order.tokens = (jax.tree.leaves(out)[0],)
        return out
    return run


def _sigmoid(x):
    return 0.5 * jnp.tanh(0.5 * x) + 0.5


def _chunks(n, width):
    return [(c, min(width, n - c)) for c in range(0, n, width)]


def _mm(pairs, mode, out_dtype, name, residual=None, alpha=1.0):
    for a, b in pairs:
        for piece in (a if isinstance(a, tuple) else (a,)):
            assert piece.dtype == BF16 and b.dtype == BF16, (name, piece.dtype, b.dtype)
    if mode == "tn":
        (a, b), = pairs
        return _mm_tokens(a, b, out_dtype, name, alpha)
    t = pairs[0][0].shape[0]
    n = pairs[0][1].shape[0] if mode == "nt" else pairs[0][1].shape[1]
    tm = _tile(t, (512, 256, 128))
    dims = _NT if mode == "nt" else _NN
    in_specs, args = [], []
    for a, w in pairs:
        in_specs.append(pl.BlockSpec((tm, a.shape[1]), lambda i: (i, 0)))
        in_specs.append(pl.BlockSpec(w.shape, lambda i: (0, 0)))
        args += [a, w]
    if residual is not None:
        in_specs.append(pl.BlockSpec((tm, n), lambda i: (i, 0)))
        args.append(residual)
    n_pairs = len(pairs)

    def body(*refs):
        o_ref = refs[-1]
        for c0, cw in _chunks(n, 512):
            acc = None
            for p in range(n_pairs):
                w_ref = refs[2 * p + 1]
                w = w_ref[c0:c0 + cw, :] if mode == "nt" else w_ref[:, c0:c0 + cw]
                d = _dot(refs[2 * p][...], w, dims)
                acc = d if acc is None else acc + d
            if alpha != 1.0:
                acc = acc * alpha
            if residual is not None:
                acc = refs[2 * n_pairs][:, c0:c0 + cw] + acc
            o_ref[:, c0:c0 + cw] = acc.astype(out_dtype)

    return _tc_call(
        body, name=name, grid=(t // tm,), in_specs=in_specs, out_specs=pl.BlockSpec((tm, n), lambda i: (i, 0)),
        out_shape=jax.ShapeDtypeStruct((t, n), out_dtype), compiler_params=_cp("parallel"))(*args)


def _mm_tokens(a, b, out_dtype, name, alpha):
    pieces = a if isinstance(a, tuple) else (a,)
    t = b.shape[0]
    widths = [p.shape[1] for p in pieces]
    m, n = sum(widths), b.shape[1]
    tk = _tile(t, (512, 256, 128))
    n_pieces = len(pieces)

    def body(*refs):
        b_ref, o_ref, acc_ref = refs[n_pieces:]
        k = pl.program_id(0)

        @pl.when(k == 0)
        def _():
            acc_ref[...] = jnp.zeros_like(acc_ref)

        off = 0
        for a_ref, width in zip(refs[:n_pieces], widths):
            for c0, cw in _chunks(width, 512):
                acc_ref[off + c0:off + c0 + cw, :] += _dot(a_ref[:, c0:c0 + cw], b_ref[...], _TN)
            off += width

        @pl.when(k == pl.num_programs(0) - 1)
        def _():
            o_ref[...] = (acc_ref[...] * alpha).astype(out_dtype)

    return _tc_call(
        body, name=name, grid=(t // tk,),
        in_specs=[pl.BlockSpec((tk, w), lambda k: (k, 0)) for w in widths] + [pl.BlockSpec((tk, n), lambda k: (k, 0))],
        out_specs=pl.BlockSpec((m, n), lambda k: (0, 0)), out_shape=jax.ShapeDtypeStruct((m, n), out_dtype),
        scratch_shapes=[pltpu.VMEM((m, n), F32)], compiler_params=_cp("arbitrary"))(*pieces, b)


def _rms_mm(x, gain, w, name):
    t, d = x.shape
    n = w.shape[0]
    tm = _tile(t, (512, 256, 128))

    def body(x_ref, g_ref, w_ref, h_ref, o_ref):
        xv = x_ref[...]
        r = lax.rsqrt(jnp.mean(xv * xv, axis=1, keepdims=True) + EPS)
        hv = (xv * r * g_ref[...]).astype(BF16)
        h_ref[...] = hv
        for c0, cw in _chunks(n, 512):
            o_ref[:, c0:c0 + cw] = _dot(hv, w_ref[c0:c0 + cw, :], _NT)

    row = pl.BlockSpec((tm, d), lambda i: (i, 0))
    return _tc_call(
        body, name=name, grid=(t // tm,),
        in_specs=[row, pl.BlockSpec((1, d), lambda i: (0, 0)), pl.BlockSpec(w.shape, lambda i: (0, 0))],
        out_specs=[row, pl.BlockSpec((tm, n), lambda i: (i, 0))],
        out_shape=[jax.ShapeDtypeStruct((t, d), BF16), jax.ShapeDtypeStruct((t, n), F32)],
        compiler_params=_cp("parallel"))(x, gain, w)


def _mm_rms_bwd(pairs, x, gain, res, name):
    t, d = x.shape
    tm = _tile(t, (512, 256, 128))
    acts, weights, entries = [], [], []
    for a, w in pairs:
        k0 = 0
        for piece in (a if isinstance(a, tuple) else (a,)):
            assert piece.dtype == BF16 and w.dtype == BF16, (name, piece.dtype, w.dtype)
            entries.append((len(acts), len(weights), k0, piece.shape[1]))
            acts.append(piece)
            k0 += piece.shape[1]
        assert k0 == w.shape[0], (name, k0, w.shape)
        weights.append(w)
    n_acts = len(acts)

    def body(*refs):
        x_ref, g_ref, res_ref, dx_ref, dxb_ref, dg_ref, dn_ref = refs[n_acts + len(weights):]
        for c0, cw in _chunks(d, 512):
            acc = None
            for ai, wi, k0, kw in entries:
                part = _dot(refs[ai][...], refs[n_acts + wi][k0:k0 + kw, c0:c0 + cw], _NN)
                acc = part if acc is None else acc + part
            dn_ref[:, c0:c0 + cw] = acc
        xv = x_ref[...]
        r = lax.rsqrt(jnp.mean(xv * xv, axis=1, keepdims=True) + EPS)
        xh = xv * r
        dnv = dn_ref[...]
        dxh = dnv * g_ref[...]
        dx = res_ref[...] + r * (dxh - xh * jnp.mean(dxh * xh, axis=1, keepdims=True))
        dx_ref[...] = dx
        dxb_ref[...] = dx.astype(BF16)

        @pl.when(pl.program_id(0) == 0)
        def _():
            dg_ref[...] = jnp.zeros_like(dg_ref)

        dg_ref[...] += jnp.sum(dnv * xh, axis=0, keepdims=True)

    in_specs = [pl.BlockSpec((tm, a.shape[1]), lambda i: (i, 0)) for a in acts]
    in_specs += [pl.BlockSpec(w.shape, lambda i: (0, 0)) for w in weights]
    row = pl.BlockSpec((tm, d), lambda i: (i, 0))
    one = pl.BlockSpec((1, d), lambda i: (0, 0))
    return _tc_call(
        body, name=name, grid=(t // tm,), in_specs=in_specs + [row, one, row], out_specs=[row, row, one],
        out_shape=[jax.ShapeDtypeStruct((t, d), F32), jax.ShapeDtypeStruct((t, d), BF16),
                   jax.ShapeDtypeStruct((1, d), F32)],
        scratch_shapes=[pltpu.VMEM((tm, d), F32)],
        compiler_params=_cp("arbitrary"))(*acts, *weights, x, gain, res)


def _ffn_up(x, gain, wg_t, wu_t, name):
    t, d = x.shape
    f = wg_t.shape[0]
    tm = _tile(t, (512, 256, 128))

    def body(x_ref, g_ref, wg_ref, wu_ref, n_ref, a_ref, b_ref, h_ref):
        xv = x_ref[...]
        r = lax.rsqrt(jnp.mean(xv * xv, axis=1, keepdims=True) + EPS)
        nv = (xv * r * g_ref[...]).astype(BF16)
        n_ref[...] = nv
        for c0, cw in _chunks(f, 256):
            a = _dot(nv, wg_ref[c0:c0 + cw, :], _NT)
            b = _dot(nv, wu_ref[c0:c0 + cw, :], _NT)
            a_ref[:, c0:c0 + cw] = a.astype(BF16)
            b_ref[:, c0:c0 + cw] = b.astype(BF16)
            h_ref[:, c0:c0 + cw] = (a * _sigmoid(a) * b).astype(BF16)

    w_spec = pl.BlockSpec((f, d), lambda i: (0, 0))
    x_spec = pl.BlockSpec((tm, d), lambda i: (i, 0))
    o_spec = pl.BlockSpec((tm, f), lambda i: (i, 0))
    o_shape = jax.ShapeDtypeStruct((t, f), BF16)
    return _tc_call(
        body, name=name, grid=(t // tm,), in_specs=[x_spec, pl.BlockSpec((1, d), lambda i: (0, 0)), w_spec, w_spec],
        out_specs=[x_spec] + [o_spec] * 3, out_shape=[jax.ShapeDtypeStruct((t, d), BF16)] + [o_shape] * 3,
        compiler_params=_cp("parallel"))(x, gain, wg_t, wu_t)


def _ffn_down_bwd(dxb, wd, a, b, name):
    t, d = dxb.shape
    f = wd.shape[0]
    tm = _tile(t, (512, 256, 128))

    def body(dx_ref, wd_ref, a_ref, b_ref, da_ref, db_ref):
        dxv = dx_ref[...]
        for c0, cw in _chunks(f, 256):
            dh = 0.5 * _dot(dxv, wd_ref[c0:c0 + cw, :], _NT)
            av = a_ref[:, c0:c0 + cw].astype(F32)
            bv = b_ref[:, c0:c0 + cw].astype(F32)
            sg = _sigmoid(av)
            da_ref[:, c0:c0 + cw] = (dh * bv * (sg * (1.0 + av * (1.0 - sg)))).astype(BF16)
            db_ref[:, c0:c0 + cw] = (dh * (av * sg)).astype(BF16)

    o_spec = pl.BlockSpec((tm, f), lambda i: (i, 0))
    o_shape = jax.ShapeDtypeStruct((t, f), BF16)
    return _tc_call(
        body, name=name, grid=(t // tm,),
        in_specs=[pl.BlockSpec((tm, d), lambda i: (i, 0)), pl.BlockSpec((f, d), lambda i: (0, 0)), o_spec, o_spec],
        out_specs=[o_spec] * 2, out_shape=[o_shape] * 2, compiler_params=_cp("parallel"))(dxb, wd, a, b)


def _loss_head(y, target):
    t, d = y.shape
    tm = _tile(t, (512, 256, 128))

    def body(y_ref, t_ref, dy_ref, dyb_ref, loss_ref, acc_ref):
        i = pl.program_id(0)
        e = y_ref[...] - t_ref[...]
        dy = e * (1.0 / d)
        dy_ref[...] = dy
        dyb_ref[...] = dy.astype(BF16)

        @pl.when(i == 0)
        def _():
            acc_ref[...] = jnp.zeros_like(acc_ref)

        acc_ref[...] += jnp.sum(e * e, axis=0, keepdims=True)

        @pl.when(i == pl.num_programs(0) - 1)
        def _():
            loss_ref[...] = jnp.sum(acc_ref[...], axis=1, keepdims=True) * (0.5 / d)

    row = pl.BlockSpec((tm, d), lambda i: (i, 0))
    return _tc_call(
        body, name="loss_head", grid=(t // tm,), in_specs=[row, row],
        out_specs=[row, row, pl.BlockSpec((1, 1), lambda i: (0, 0))],
        out_shape=[jax.ShapeDtypeStruct((t, d), F32), jax.ShapeDtypeStruct((t, d), BF16),
                   jax.ShapeDtypeStruct((1, 1), F32)],
        scratch_shapes=[pltpu.VMEM((1, d), F32)], compiler_params=_cp("arbitrary"))(y, target)


def _merge_fwd(oa, ob, proj, wa_t, wb_t, name):
    t = oa.shape[0]
    d = wa_t.shape[0]
    tm = _tile(t, (512, 256, 128))

    def body(oa_ref, ob_ref, ga_ref, gb_ref, wa_ref, wb_ref, mg_ref, ya_ref, yb_ref):
        oav, obv = oa_ref[...], ob_ref[...]
        for c0, cw in _chunks(d, 512):
            cs = slice(c0, c0 + cw)
            ya = _dot(oav, wa_ref[cs, :], _NT)
            yb = _dot(obv, wb_ref[cs, :], _NT)
            mg_ref[:, cs] = (_sigmoid(ga_ref[:, cs]) * ya + _sigmoid(gb_ref[:, cs]) * yb).astype(BF16)
            ya_ref[:, cs] = ya.astype(BF16)
            yb_ref[:, cs] = yb.astype(BF16)

    o_spec = pl.BlockSpec((tm, d), lambda i: (i, 0))
    o_shape = jax.ShapeDtypeStruct((t, d), BF16)
    return _tc_call(
        body, name=name, grid=(t // tm,),
        in_specs=[pl.BlockSpec((tm, oa.shape[1]), lambda i: (i, 0)), pl.BlockSpec((tm, ob.shape[1]), lambda i: (i, 0)),
                  pl.BlockSpec((tm, d), lambda i: (i, P_GA // d)), pl.BlockSpec((tm, d), lambda i: (i, P_GB // d)),
                  pl.BlockSpec(wa_t.shape, lambda i: (0, 0)), pl.BlockSpec(wb_t.shape, lambda i: (0, 0))],
        out_specs=[o_spec] * 3, out_shape=[o_shape] * 3,
        compiler_params=_cp("parallel"))(oa, ob, proj, proj, wa_t, wb_t)


def _merge_bwd(dxb, wo, proj, ya, yb, wa_t, wb_t, name):
    t, d = dxb.shape
    tm = _tile(t, (512, 256, 128))
    wa_in, wb_in = wa_t.shape[1], wb_t.shape[1]

    def body(dx_ref, wo_ref, ga_ref, gb_ref, ya_ref, yb_ref, wa_ref, wb_ref,
             dya_ref, dyb_ref, dga_ref, dgb_ref, doa_ref, dob_ref):
        dxv = dx_ref[...]
        doa = jnp.zeros((tm, wa_in), F32)
        dob = jnp.zeros((tm, wb_in), F32)
        for c0, cw in _chunks(d, 512):
            cs = slice(c0, c0 + cw)
            dm = _dot(dxv, wo_ref[cs, :], _NT)
            sa = _sigmoid(ga_ref[:, cs])
            sb = _sigmoid(gb_ref[:, cs])
            dya = (dm * sa).astype(BF16)
            dyb = (dm * sb).astype(BF16)
            dya_ref[:, cs] = dya
            dyb_ref[:, cs] = dyb
            dga_ref[:, cs] = (dm * ya_ref[:, cs].astype(F32) * (sa * (1.0 - sa))).astype(BF16)
            dgb_ref[:, cs] = (dm * yb_ref[:, cs].astype(F32) * (sb * (1.0 - sb))).astype(BF16)
            doa = doa + _dot(dya, wa_ref[cs, :], _NN)
            dob = dob + _dot(dyb, wb_ref[cs, :], _NN)
        doa_ref[...] = doa.astype(BF16)
        dob_ref[...] = dob.astype(BF16)

    o_spec = pl.BlockSpec((tm, d), lambda i: (i, 0))
    o_shape = jax.ShapeDtypeStruct((t, d), BF16)
    return _tc_call(
        body, name=name, grid=(t // tm,),
        in_specs=[o_spec, pl.BlockSpec((d, d), lambda i: (0, 0)),
                  pl.BlockSpec((tm, d), lambda i: (i, P_GA // d)), pl.BlockSpec((tm, d), lambda i: (i, P_GB // d)),
                  o_spec, o_spec, pl.BlockSpec(wa_t.shape, lambda i: (0, 0)), pl.BlockSpec(wb_t.shape, lambda i: (0, 0))],
        out_specs=[o_spec] * 4 + [pl.BlockSpec((tm, wa_in), lambda i: (i, 0)), pl.BlockSpec((tm, wb_in), lambda i: (i, 0))],
        out_shape=[o_shape] * 4 + [jax.ShapeDtypeStruct((t, wa_in), BF16), jax.ShapeDtypeStruct((t, wb_in), BF16)],
        compiler_params=_cp("parallel"))(dxb, wo, proj, proj, ya, yb, wa_t, wb_t)


def _swa_common(has_prev, pq, pk):
    dist = (pq - pk).astype(F32)
    row = lax.broadcasted_iota(jnp.int32, (BLOCK, 2 * BLOCK), 0)
    col = lax.broadcasted_iota(jnp.int32, (BLOCK, 2 * BLOCK), 1)
    diff = row + BLOCK - col
    valid = (diff >= 0) & (diff < BLOCK) & (has_prev | (col >= BLOCK))
    return jnp.concatenate([dist] * N_HEADS_A, axis=0), jnp.concatenate([valid] * N_HEADS_A, axis=0)


def _half_sum(x, lo):
    s_lo = jnp.sum(jnp.where(lo, x, 0.0), axis=1, keepdims=True)
    s_hi = jnp.sum(jnp.where(lo, 0.0, x), axis=1, keepdims=True)
    return jnp.where(lo, s_lo, s_hi)


def _norm2(x, gain2, lo):
    r = lax.rsqrt(_half_sum(x * x, lo) * (1.0 / HEAD_DIM_A) + EPS)
    xh = x * r
    return xh * gain2, xh, r


def _norm2_bwd(d, xh, r, gain2, lo):
    dxh = d * gain2
    return r * (dxh - xh * (_half_sum(dxh * xh, lo) * (1.0 / HEAD_DIM_A)))


def _swa_stack(tiles, lo):
    zero = jnp.zeros_like(tiles[0])
    return jnp.concatenate([jnp.where(lo, t, zero) for t in tiles] + [jnp.where(lo, zero, t) for t in tiles], axis=0)


def _swa_unstack(x8, j, lo):
    return jnp.where(lo, x8[j * BLOCK:(j + 1) * BLOCK], x8[(GROUP_A + j) * BLOCK:(GROUP_A + j + 1) * BLOCK])


def _swa_head_columns(sk_ref):
    slope = jnp.concatenate([jnp.full((BLOCK, 1), 2.0 ** (-(h + 1)), F32) for h in range(N_HEADS_A)], axis=0)
    sink = jnp.concatenate([jnp.broadcast_to(sk_ref[:, h:h + 1], (BLOCK, 1)) for h in range(N_HEADS_A)], axis=0)
    return slope, sink


def _swa_blocks_per_step(s_len):
    nb = s_len // BLOCK
    return 4 if nb % 4 == 0 and nb >= 8 else 2 if nb % 2 == 0 else 1


def _swa_specs(s_len):
    nb, qb = s_len // BLOCK, _swa_blocks_per_step(s_len)
    ns, rows = nb // qb, qb * BLOCK

    def step(b, j):
        return b * ns + j

    def prev(b, j):
        return b * nb + jnp.maximum(qb * j - 1, 0)

    q_spec = pl.BlockSpec((rows, WIDTH_A), lambda b, j: (step(b, j), P_QA // WIDTH_A))
    kc_spec = pl.BlockSpec((rows, KV_A), lambda b, j: (step(b, j), P_KA // KV_A))
    kp_spec = pl.BlockSpec((BLOCK, KV_A), lambda b, j: (prev(b, j), P_KA // KV_A))
    vc_spec = pl.BlockSpec((rows, KV_A), lambda b, j: (step(b, j), P_VA // KV_A))
    vp_spec = pl.BlockSpec((BLOCK, KV_A), lambda b, j: (prev(b, j), P_VA // KV_A))
    pq_spec = pl.BlockSpec((rows, 1), lambda b, j: (step(b, j), 0))
    pkc_spec = pl.BlockSpec((qb, 1, BLOCK), lambda b, j: (step(b, j), 0, 0))
    pkp_spec = pl.BlockSpec((1, 1, BLOCK), lambda b, j: (prev(b, j), 0, 0))
    return qb, ns, step, [q_spec, kc_spec, kp_spec, vc_spec, vp_spec, pq_spec, pkc_spec, pkp_spec]


def _swa_stage_probs(qb, q_ref, kc_ref, kp_ref, vc_ref, vp_ref, pq_ref, pkc_ref, pkp_ref, qg_ref, kg_ref, sk_ref, lo):
    first = pl.program_id(1) * qb
    kk_all = _norm2(jnp.concatenate([kp_ref[...], kc_ref[...]], axis=0), kg_ref[...], lo)[0].astype(BF16)
    vv_all = jnp.concatenate([vp_ref[...], vc_ref[...]], axis=0).astype(BF16)
    pk_all = jnp.concatenate([pkp_ref[0]] + [pkc_ref[s] for s in range(qb)], axis=1)
    rows = [slice(s * BLOCK, (s + 1) * BLOCK) for s in range(qb)]
    keys = [slice(s * BLOCK, (s + 2) * BLOCK) for s in range(qb)]
    masks = [_swa_common(first + s > 0, pq_ref[rows[s], :], pk_all[:, keys[s]]) for s in range(qb)]
    qs = [[_norm2(q_ref[r, j * HEAD_PAD:(j + 1) * HEAD_PAD], qg_ref[...], lo) for j in range(GROUP_A)] for r in rows]
    q8 = [_swa_stack([q[0] for q in tiles], lo).astype(BF16) for tiles in qs]
    kk = [kk_all[ks] for ks in keys]
    vv = [vv_all[ks] for ks in keys]
    slope, sink = _swa_head_columns(sk_ref)
    s = [_dot(q8[b], kk[b], _NT) * (HEAD_DIM_A ** -0.5) - slope * masks[b][0] for b in range(qb)]
    s = [jnp.where(masks[b][1], s[b], NEG) for b in range(qb)]
    m = [jnp.maximum(jnp.max(x, axis=1, keepdims=True), sink) for x in s]
    e = [jnp.exp(x - mx) for x, mx in zip(s, m)]
    es = [jnp.exp(sink - mx) for mx in m]
    inv = [1.0 / (jnp.sum(x, axis=1, keepdims=True) + y) for x, y in zip(e, es)]
    p = [x * i for x, i in zip(e, inv)]
    ps = [y * i for y, i in zip(es, inv)]
    return rows, qs, q8, kk, vv, p, ps


def _swa_fwd(proj, pos_col, pos_row, qg2, kg2, sinks, n_batch, s_len, name):
    t = proj.shape[0]
    qb, ns, step, specs = _swa_specs(s_len)
    small = pl.BlockSpec((1, HEAD_PAD), lambda b, j: (0, 0))

    def body(q_ref, kc_ref, kp_ref, vc_ref, vp_ref, pq_ref, pkc_ref, pkp_ref, qg_ref, kg_ref, sk_ref, o_ref):
        lo = lax.broadcasted_iota(jnp.int32, (1, HEAD_PAD), 1) < HEAD_DIM_A
        rows, _, _, _, vv, p, _ = _swa_stage_probs(qb, q_ref, kc_ref, kp_ref, vc_ref, vp_ref, pq_ref, pkc_ref, pkp_ref,
                                                   qg_ref, kg_ref, sk_ref, lo)
        o8 = [_dot(p[b].astype(BF16), vv[b], _NN) for b in range(qb)]
        for b in range(qb):
            for j in range(GROUP_A):
                o_ref[rows[b], j * HEAD_PAD:(j + 1) * HEAD_PAD] = _swa_unstack(o8[b], j, lo).astype(BF16)

    return _tc_call(
        body, name=name, grid=(n_batch, ns), in_specs=specs + [small, small, small],
        out_specs=pl.BlockSpec((qb * BLOCK, WIDTH_A), lambda b, j: (step(b, j), 0)),
        out_shape=jax.ShapeDtypeStruct((t, WIDTH_A), BF16),
        compiler_params=_cp("parallel", "parallel"))(proj, proj, proj, proj, proj, pos_col, pos_row, pos_row,
                                                     qg2, kg2, sinks)


def _swa_bwd(proj, pos_col, pos_row, qg2, kg2, sinks, do, n_batch, s_len, name):
    t = proj.shape[0]
    qb, ns, step, specs = _swa_specs(s_len)
    small = pl.BlockSpec((1, HEAD_PAD), lambda b, j: (0, 0))
    scale = HEAD_DIM_A ** -0.5

    def body(q_ref, kc_ref, kp_ref, vc_ref, vp_ref, pq_ref, pkc_ref, pkp_ref, qg_ref, kg_ref, sk_ref, do_ref,
             dq_ref, dkc_ref, dkp_ref, dvc_ref, dvp_ref, dqg_ref, dsk_ref):
        @pl.when((pl.program_id(0) == 0) & (pl.program_id(1) == 0))
        def _():
            dqg_ref[...] = jnp.zeros_like(dqg_ref)
            dsk_ref[...] = jnp.zeros_like(dsk_ref)

        lane = lax.broadcasted_iota(jnp.int32, (1, HEAD_PAD), 1)
        lo = lane < HEAD_DIM_A
        rows, qs, q8, kk, vv, p, ps = _swa_stage_probs(qb, q_ref, kc_ref, kp_ref, vc_ref, vp_ref, pq_ref, pkc_ref,
                                                       pkp_ref, qg_ref, kg_ref, sk_ref, lo)
        blocks = range(qb)
        do8 = [_swa_stack([do_ref[r, j * HEAD_PAD:(j + 1) * HEAD_PAD] for j in range(GROUP_A)], lo) for r in rows]
        dp = [_dot(do8[b], vv[b], _NT) for b in blocks]
        delta = [jnp.sum(p[b] * dp[b], axis=1, keepdims=True) for b in blocks]
        ds = [(p[b] * (dp[b] - delta[b]) * scale).astype(BF16) for b in blocks]
        dsink = [ps[b] * delta[b] for b in blocks]
        dvv = [_dot(p[b].astype(BF16), do8[b], _TN) for b in blocks]
        dkk = [_dot(ds[b], q8[b], _TN) for b in blocks]
        dq8 = [_dot(ds[b], kk[b], _NN) for b in blocks]
        dsk = jnp.zeros((1, HEAD_PAD), F32)
        dqg = jnp.zeros((1, HEAD_PAD), F32)
        for b in blocks:
            for h in range(N_HEADS_A):
                dsk = dsk + jnp.where(lane == h, -jnp.sum(dsink[b][h * BLOCK:(h + 1) * BLOCK]), 0.0)
            for j in range(GROUP_A):
                _, xh, r = qs[b][j]
                dqn = _swa_unstack(dq8[b], j, lo)
                dqg = dqg + jnp.sum(dqn * xh, axis=0, keepdims=True)
                dq_ref[rows[b], j * HEAD_PAD:(j + 1) * HEAD_PAD] = _norm2_bwd(dqn, xh, r, qg_ref[...], lo).astype(BF16)
            dkp_ref[rows[b], :] = dkk[b][:BLOCK]
            dkc_ref[rows[b], :] = dkk[b][BLOCK:]
            dvp_ref[rows[b], :] = dvv[b][:BLOCK]
            dvc_ref[rows[b], :] = dvv[b][BLOCK:]
        dqg_ref[...] += dqg
        dsk_ref[...] += dsk

    kv_out = pl.BlockSpec((qb * BLOCK, KV_A), lambda b, j: (step(b, j), 0))
    kv_shape = jax.ShapeDtypeStruct((t, KV_A), F32)
    wide = pl.BlockSpec((qb * BLOCK, WIDTH_A), lambda b, j: (step(b, j), 0))
    return _tc_call(
        body, name=name, grid=(n_batch, ns), in_specs=specs + [small, small, small, wide],
        out_specs=[wide, kv_out, kv_out, kv_out, kv_out, small, small],
        out_shape=[jax.ShapeDtypeStruct((t, WIDTH_A), BF16), kv_shape, kv_shape, kv_shape, kv_shape,
                   jax.ShapeDtypeStruct((1, HEAD_PAD), F32), jax.ShapeDtypeStruct((1, HEAD_PAD), F32)],
        compiler_params=_cp("arbitrary", "arbitrary"))(proj, proj, proj, proj, proj, pos_col, pos_row, pos_row,
                                                       qg2, kg2, sinks, do)


def _swa_kv_bwd(proj, kg2, dkc, dkp, dvc, dvp, n_batch, s_len, name):
    t = proj.shape[0]
    nb, qb = s_len // BLOCK, _swa_blocks_per_step(s_len)
    ns, rows = nb // qb, qb * BLOCK

    def step(b, j):
        return b * ns + j

    def edge(b, j):
        return b * nb + jnp.minimum(qb * (j + 1), nb - 1)

    def body(k_ref, kg_ref, dkc_ref, dkp_ref, dkp_edge, dvc_ref, dvp_ref, dvp_edge, dk_ref, dv_ref, dkg_ref):
        @pl.when((pl.program_id(0) == 0) & (pl.program_id(1) == 0))
        def _():
            dkg_ref[...] = jnp.zeros_like(dkg_ref)

        lo = lax.broadcasted_iota(jnp.int32, (1, HEAD_PAD), 1) < HEAD_DIM_A
        has_next = (pl.program_id(1) < ns - 1).astype(F32)

        def from_next(p_ref, edge_ref):
            return jnp.concatenate([p_ref[BLOCK:, :], has_next * edge_ref[...]], axis=0)

        dkn = dkc_ref[...] + from_next(dkp_ref, dkp_edge)
        dv_ref[...] = (dvc_ref[...] + from_next(dvp_ref, dvp_edge)).astype(BF16)
        _, xh, r = _norm2(k_ref[...], kg_ref[...], lo)
        dkg_ref[...] += jnp.sum(dkn * xh, axis=0, keepdims=True)
        dk_ref[...] = _norm2_bwd(dkn, xh, r, kg_ref[...], lo).astype(BF16)

    cur = pl.BlockSpec((rows, KV_A), lambda b, j: (step(b, j), 0))
    nxt = pl.BlockSpec((BLOCK, KV_A), lambda b, j: (edge(b, j), 0))
    small = pl.BlockSpec((1, HEAD_PAD), lambda b, j: (0, 0))
    return _tc_call(
        body, name=name, grid=(n_batch, ns),
        in_specs=[pl.BlockSpec((rows, KV_A), lambda b, j: (step(b, j), P_KA // KV_A)), small, cur, cur, nxt, cur, cur,
                  nxt],
        out_specs=[cur, cur, small],
        out_shape=[jax.ShapeDtypeStruct((t, KV_A), BF16), jax.ShapeDtypeStruct((t, KV_A), BF16),
                   jax.ShapeDtypeStruct((1, HEAD_PAD), F32)],
        compiler_params=_cp("arbitrary", "arbitrary"))(proj, kg2, dkc, dkp, dkp, dvc, dvp, dvp)


def _lora_norm(x, gain):
    r = lax.rsqrt(jnp.mean(x * x, axis=1, keepdims=True) + EPS)
    xh = x * r
    return xh * gain, xh, r


def _mla_in_specs(tm):
    row = lambda w, off: pl.BlockSpec((tm, w), lambda i: (i, off // w))
    one = lambda w: pl.BlockSpec((1, w), lambda i: (0, 0))
    full = lambda r, c: pl.BlockSpec((r, c), lambda i: (0, 0))
    tab = pl.BlockSpec((tm, HEAD_PAD), lambda i: (i, 0))
    return [row(Q_LORA, P_CQ), row(KV_LORA, P_CKV), row(HEAD_PAD, P_KR), tab, tab, tab,
            one(Q_LORA), one(KV_LORA), one(HEAD_PAD), one(HEAD_PAD),
            full(WIDTH_BP, Q_LORA), full(WIDTH_BP, KV_LORA), full(WIDTH_BP, KV_LORA)]


def _mla_pre(proj, tabs, gq, gkv, gqn, gkn, wuq, wk, wv, name):
    t = proj.shape[0]
    tm = _tile(t, (512, 256, 128))

    def body(cq_ref, ckv_ref, kr_ref, c_ref, sm_ref, sp_ref, gq_ref, gkv_ref, gqn_ref, gkn_ref,
             wuq_ref, wk_ref, wv_ref, q_ref, k_ref, v_ref):
        cqn = _lora_norm(cq_ref[...], gq_ref[...])[0].astype(BF16)
        ckvn = _lora_norm(ckv_ref[...], gkv_ref[...])[0].astype(BF16)
        q_raw = _dot(cqn, wuq_ref[...], _NT)
        k_raw = _dot(ckvn, wk_ref[...], _NT)
        v_ref[...] = _dot(ckvn, wv_ref[...], _NT).astype(BF16)
        kr = pltpu.roll(kr_ref[...], NOPE, 1)
        c, sm, sp = c_ref[...], sm_ref[...], sp_ref[...]
        heads = [slice(h * HEAD_PAD, (h + 1) * HEAD_PAD) for h in range(N_HEADS_B)]
        raw = [q_raw[:, hs] for hs in heads] + [k_raw[:, hs] + kr for hs in heads]
        gains = [gqn_ref[...]] * N_HEADS_B + [gkn_ref[...]] * N_HEADS_B
        sq = [jnp.sum(x * x, axis=1, keepdims=True) for x in raw]
        normed = [x * lax.rsqrt(s * (1.0 / QK_B) + EPS) * g for x, s, g in zip(raw, sq, gains)]
        up = [pltpu.roll(x, HEAD_PAD - ROPE // 2, 1) for x in normed]
        down = [pltpu.roll(x, ROPE // 2, 1) for x in normed]
        roped = [(x * c + u * sm + d * sp).astype(BF16) for x, u, d in zip(normed, up, down)]
        for h, hs in enumerate(heads):
            q_ref[:, hs] = roped[h]
            k_ref[:, hs] = roped[N_HEADS_B + h]

    o_spec = pl.BlockSpec((tm, WIDTH_BP), lambda i: (i, 0))
    o_shape = jax.ShapeDtypeStruct((t, WIDTH_BP), BF16)
    return _tc_call(
        body, name=name, grid=(t // tm,), in_specs=_mla_in_specs(tm), out_specs=[o_spec] * 3,
        out_shape=[o_shape] * 3, compiler_params=_cp("parallel"))(
            proj, proj, proj, *tabs, gq, gkv, gqn, gkn, wuq, wk, wv)


def _mla_pre_bwd(proj, tabs, gq, gkv, gqn, gkn, wuq, wk, wv, dq, dk, dv, name):
    t = proj.shape[0]
    tm = _tile(t, (512, 256, 128))

    def body(cq_ref, ckv_ref, kr_ref, c_ref, sm_ref, sp_ref, gq_ref, gkv_ref, gqn_ref, gkn_ref,
             wuq_ref, wk_ref, wv_ref, dq_ref, dk_ref, dv_ref,
             dcq_ref, dckv_ref, dkr_ref, dwuq_ref, dwk_ref, dwv_ref, dgq_ref, dgkv_ref, dgqn_ref, dgkn_ref,
             dqraw_ref, dkraw_ref):
        @pl.when(pl.program_id(0) == 0)
        def _():
            for r in (dwuq_ref, dwk_ref, dwv_ref, dgq_ref, dgkv_ref, dgqn_ref, dgkn_ref):
                r[...] = jnp.zeros_like(r)

        cqn_f, cq_xh, cq_r = _lora_norm(cq_ref[...], gq_ref[...])
        ckvn_f, ckv_xh, ckv_r = _lora_norm(ckv_ref[...], gkv_ref[...])
        cqn, ckvn = cqn_f.astype(BF16), ckvn_f.astype(BF16)
        q_raw = _dot(cqn, wuq_ref[...], _NT)
        k_raw = _dot(ckvn, wk_ref[...], _NT)
        kr = pltpu.roll(kr_ref[...], NOPE, 1)
        c, sm, sp = c_ref[...], sm_ref[...], sp_ref[...]
        heads = [slice(h * HEAD_PAD, (h + 1) * HEAD_PAD) for h in range(N_HEADS_B)]
        raw = [q_raw[:, hs] for hs in heads] + [k_raw[:, hs] + kr for hs in heads]
        d_out = [dq_ref[:, hs] for hs in heads] + [dk_ref[:, hs] for hs in heads]
        gains = [gqn_ref[...]] * N_HEADS_B + [gkn_ref[...]] * N_HEADS_B
        sq = [jnp.sum(x * x, axis=1, keepdims=True) for x in raw]
        rinv = [lax.rsqrt(s * (1.0 / QK_B) + EPS) for s in sq]
        xhat = [x * r for x, r in zip(raw, rinv)]
        down = [pltpu.roll(d * sm, ROPE // 2, 1) for d in d_out]
        up = [pltpu.roll(d * sp, HEAD_PAD - ROPE // 2, 1) for d in d_out]
        dn = [d * c + a + b for d, a, b in zip(d_out, down, up)]
        dgain = [jnp.sum(d * xh, axis=0, keepdims=True) for d, xh in zip(dn, xhat)]
        dxh = [d * g for d, g in zip(dn, gains)]
        inner = [jnp.sum(d * xh, axis=1, keepdims=True) * (1.0 / QK_B) for d, xh in zip(dxh, xhat)]
        d_raw = [r * (d - xh * s) for r, d, xh, s in zip(rinv, dxh, xhat, inner)]
        for h, hs in enumerate(heads):
            dqraw_ref[:, hs] = d_raw[h].astype(BF16)
            dkraw_ref[:, hs] = d_raw[N_HEADS_B + h].astype(BF16)
        dkr = sum(d_raw[N_HEADS_B + 1:], d_raw[N_HEADS_B])
        dgqn_ref[...] += sum(dgain[1:N_HEADS_B], dgain[0])
        dgkn_ref[...] += sum(dgain[N_HEADS_B + 1:], dgain[N_HEADS_B])
        lane = lax.broadcasted_iota(jnp.int32, (tm, HEAD_PAD), 1)
        dkr_ref[...] = jnp.where(lane < ROPE, pltpu.roll(dkr, HEAD_PAD - NOPE, 1), 0.0).astype(BF16)
        dqraw = dqraw_ref[...]
        dkraw = dkraw_ref[...]
        dvb = dv_ref[...].astype(BF16)
        dwuq_ref[...] += _dot(dqraw, cqn, _TN)
        dwk_ref[...] += _dot(dkraw, ckvn, _TN)
        dwv_ref[...] += _dot(dvb, ckvn, _TN)
        dcqn = _dot(dqraw, wuq_ref[...], _NN)
        dckvn = _dot(dkraw, wk_ref[...], _NN) + _dot(dvb, wv_ref[...], _NN)
        dgq_ref[...] += jnp.sum(dcqn * cq_xh, axis=0, keepdims=True)
        dgkv_ref[...] += jnp.sum(dckvn * ckv_xh, axis=0, keepdims=True)
        dxh = dcqn * gq_ref[...]
        dcq_ref[...] = (cq_r * (dxh - cq_xh * jnp.mean(dxh * cq_xh, axis=1, keepdims=True))).astype(BF16)
        dxh = dckvn * gkv_ref[...]
        dckv_ref[...] = (ckv_r * (dxh - ckv_xh * jnp.mean(dxh * ckv_xh, axis=1, keepdims=True))).astype(BF16)

    wide = pl.BlockSpec((tm, WIDTH_BP), lambda i: (i, 0))
    row = lambda w: pl.BlockSpec((tm, w), lambda i: (i, 0))
    full = lambda r, c: pl.BlockSpec((r, c), lambda i: (0, 0))
    return _tc_call(
        body, name=name, grid=(t // tm,), in_specs=_mla_in_specs(tm) + [wide, wide, wide],
        out_specs=[row(Q_LORA), row(KV_LORA), row(HEAD_PAD), full(WIDTH_BP, Q_LORA), full(WIDTH_BP, KV_LORA),
                   full(WIDTH_BP, KV_LORA), full(1, Q_LORA), full(1, KV_LORA), full(1, HEAD_PAD), full(1, HEAD_PAD)],
        out_shape=[jax.ShapeDtypeStruct((t, Q_LORA), BF16), jax.ShapeDtypeStruct((t, KV_LORA), BF16),
                   jax.ShapeDtypeStruct((t, HEAD_PAD), BF16), jax.ShapeDtypeStruct((WIDTH_BP, Q_LORA), F32),
                   jax.ShapeDtypeStruct((WIDTH_BP, KV_LORA), F32), jax.ShapeDtypeStruct((WIDTH_BP, KV_LORA), F32),
                   jax.ShapeDtypeStruct((1, Q_LORA), F32), jax.ShapeDtypeStruct((1, KV_LORA), F32),
                   jax.ShapeDtypeStruct((1, HEAD_PAD), F32), jax.ShapeDtypeStruct((1, HEAD_PAD), F32)],
        scratch_shapes=[pltpu.VMEM((tm, WIDTH_BP), BF16), pltpu.VMEM((tm, WIDTH_BP), BF16)],
        compiler_params=_cp("arbitrary"))(proj, proj, proj, *tabs, gq, gkv, gqn, gkn, wuq, wk, wv, dq, dk, dv)


def _mla_flash_specs(s_len):
    bh_spec = pl.BlockSpec((s_len, HEAD_PAD), lambda b, h: (b, h))
    lse_spec = pl.BlockSpec((1, s_len, 1), lambda b, h: (b * N_HEADS_B + h, 0, 0))
    return bh_spec, lse_spec


def _diag_mask(s):
    row = lax.broadcasted_iota(jnp.int32, s.shape, 0)
    col = lax.broadcasted_iota(jnp.int32, s.shape, 1)
    return jnp.where(row >= col, s, NEG)


def _mla_flash_fwd(q, k, v, n_batch, s_len, name):
    t = q.shape[0]
    tq = _tile(s_len, (256, 128))
    bh_spec, lse_spec = _mla_flash_specs(s_len)
    c = (QK_B ** -0.5) * LOG2E

    def body(q_ref, k_ref, v_ref, o_ref, lse_ref):
        nq = s_len // tq
        rows = [slice(i * tq, (i + 1) * tq) for i in range(nq)]
        below = [slice(0, i * tq) for i in range(nq)]
        qs = [q_ref[r, :] for r in rows]
        sd = [_diag_mask(_dot(qs[i], k_ref[rows[i], :], _NT)) for i in range(nq)]
        sb = [None] + [_dot(qs[i], k_ref[below[i], :], _NT) for i in range(1, nq)]
        m = [jnp.max(s, axis=1, keepdims=True) for s in sd]
        m = [m[0]] + [jnp.maximum(m[i], jnp.max(sb[i], axis=1, keepdims=True)) for i in range(1, nq)]
        pd = [jnp.exp2((sd[i] - m[i]) * c) for i in range(nq)]
        pb = [None] + [jnp.exp2((sb[i] - m[i]) * c) for i in range(1, nq)]
        l = [jnp.sum(p, axis=1, keepdims=True) for p in pd]
        l = [l[0]] + [l[i] + jnp.sum(pb[i], axis=1, keepdims=True) for i in range(1, nq)]
        acc = [_dot(pd[i].astype(BF16), v_ref[rows[i], :], _NN) for i in range(nq)]
        acc = [acc[0]] + [acc[i] + _dot(pb[i].astype(BF16), v_ref[below[i], :], _NN) for i in range(1, nq)]
        for i in range(nq):
            o_ref[rows[i], :] = (acc[i] * (1.0 / l[i])).astype(BF16)
            lse_ref[0, rows[i], :] = m[i] * c + jnp.log2(l[i])

    return _tc_call(
        body, name=name, grid=(n_batch, N_HEADS_B), in_specs=[bh_spec, bh_spec, bh_spec],
        out_specs=[bh_spec, lse_spec],
        out_shape=[jax.ShapeDtypeStruct((t, WIDTH_BP), BF16),
                   jax.ShapeDtypeStruct((n_batch * N_HEADS_B, s_len, 1), F32)],
        compiler_params=_cp("parallel", "parallel"))(q, k, v)


def _mla_flash_bwd(q, k, v, o, do, lse2, n_batch, s_len, name):
    t = q.shape[0]
    tq = _tile(s_len, (256, 128))
    bh_spec, lse_spec = _mla_flash_specs(s_len)
    scale = QK_B ** -0.5
    c = scale * LOG2E

    def body(q_ref, k_ref, v_ref, o_ref, do_ref, lse_ref, dq_ref, dk_ref, dv_ref):
        nq = s_len // tq
        rows = [slice(i * tq, (i + 1) * tq) for i in range(nq)]
        below = [slice(0, i * tq) for i in range(nq)]
        qs = [q_ref[r, :] for r in rows]
        dos = [do_ref[r, :] for r in rows]
        lse = [lse_ref[0, r, :] for r in rows]
        delta = [jnp.sum(dos[i].astype(F32) * o_ref[rows[i], :].astype(F32), axis=1, keepdims=True)
                 for i in range(nq)]

        def probs_and_ds(i, ks, diag):
            s = _dot(qs[i], k_ref[ks, :], _NT)
            if diag:
                s = _diag_mask(s)
            p = jnp.exp2(s * c - lse[i])
            dp = _dot(dos[i], v_ref[ks, :], _NT)
            return p.astype(BF16), (p * (dp - delta[i]) * scale).astype(BF16)

        diag = [probs_and_ds(i, rows[i], True) for i in range(nq)]
        rest = [None] + [probs_and_ds(i, below[i], False) for i in range(1, nq)]
        for i in range(nq):
            dq = _dot(diag[i][1], k_ref[rows[i], :], _NN)
            if i:
                dq = dq + _dot(rest[i][1], k_ref[below[i], :], _NN)
            dq_ref[rows[i], :] = dq
        for j in range(nq):
            later = slice(j * tq, s_len)
            p_j = jnp.concatenate([diag[j][0]] + [rest[i][0][:, rows[j]] for i in range(j + 1, nq)], axis=0)
            ds_j = jnp.concatenate([diag[j][1]] + [rest[i][1][:, rows[j]] for i in range(j + 1, nq)], axis=0)
            dk_ref[rows[j], :] = _dot(ds_j, q_ref[later, :], _TN)
            dv_ref[rows[j], :] = _dot(p_j, do_ref[later, :], _TN)

    f32_wide = jax.ShapeDtypeStruct((t, WIDTH_BP), F32)
    return _tc_call(
        body, name=name, grid=(n_batch, N_HEADS_B),
        in_specs=[bh_spec, bh_spec, bh_spec, bh_spec, bh_spec, lse_spec],
        out_specs=[bh_spec, bh_spec, bh_spec], out_shape=[f32_wide] * 3,
        compiler_params=_cp("parallel", "parallel"))(q, k, v, o, do, lse2)


def _swa_heads(w, axis, order):
    heads = [lax.slice_in_dim(w, h * HEAD_DIM_A, (h + 1) * HEAD_DIM_A, axis=axis) for h in order]
    return jnp.concatenate(heads, axis=axis)


class _LayerWeights:
    def __init__(self, build):
        self._build, self._mats = build, {}

    def __getitem__(self, name):
        if name not in self._mats:
            self._mats.update(self._build(name))
        return self._mats[name]


PIECES_OF = {"mla_w_ukv": ("wk", "wv")}


def _store(name, w):
    t = w.T if dict(BIG)[name] else w
    if name == "mla_w_uq":
        return {name: jnp.pad(t, ((0, HEAD_PAD - QK_B), (0, 0)))}
    if name == "mla_w_ukv":
        pad = ((0, HEAD_PAD - NOPE), (0, 0))
        return {"wk": jnp.pad(t[:NOPE], pad), "wv": jnp.pad(t[NOPE:], pad)}
    if name == "w_branch_b":
        t3 = jnp.pad(t.reshape(t.shape[0], N_HEADS_B, V_B), ((0, 0), (0, 0), (0, HEAD_PAD - V_B)))
        return {name: t3.reshape(t.shape[0], WIDTH_BP)}
    if name == "w_branch_a":
        return {name: _swa_heads(t, 1, SWA_HEAD_ORDER)}
    return {name: t}


def _unstore(name, pieces):
    if name == "mla_w_uq":
        return pieces[name][:QK_B]
    if name == "mla_w_ukv":
        return jnp.concatenate([pieces["wk"][:NOPE], pieces["wv"][:V_B]], axis=0)
    if name == "w_branch_b":
        g = pieces[name]
        return g.reshape(g.shape[0], N_HEADS_B, HEAD_PAD)[:, :, :V_B].reshape(g.shape[0], WIDTH_B)
    if name == "w_branch_a":
        return _swa_heads(pieces[name], 1, SWA_HEAD_INVERSE)
    return pieces[name]


def _layer_mats(w):
    if "w_in" not in w:
        return dict(w)
    w_in = w["w_in"]
    o = [0]
    for n in (WIDTH_A, KV_A, KV_A, Q_LORA, KV_LORA, ROPE, D_MODEL, D_MODEL):
        o.append(o[-1] + n)
    qa, ka, va, cq, ckv, kr, ga, gb = (w_in[o[i]:o[i + 1]] for i in range(8))
    pad = jnp.zeros((PROJ_W - IN_WIDTH, w_in.shape[1]), w_in.dtype)
    out = dict(w)
    out["w_in"] = jnp.concatenate([ga, gb, _swa_heads(qa, 0, SWA_HEAD_ORDER), cq, ka, va, ckv, kr, pad], axis=0)
    return out


def _unlayer_w_in(d):
    ga, gb, qa, cq, ka, va, ckv, kr = (d[a:b] for a, b in (
        (P_GA, P_GA + D_MODEL), (P_GB, P_GB + D_MODEL), (P_QA, P_QA + WIDTH_A), (P_CQ, P_CQ + Q_LORA),
        (P_KA, P_KA + KV_A), (P_VA, P_VA + KV_A), (P_CKV, P_CKV + KV_LORA), (P_KR, P_KR + ROPE)))
    return jnp.concatenate([_swa_heads(qa, 0, SWA_HEAD_INVERSE), ka, va, cq, ckv, kr, ga, gb], axis=0)


def _pad_lanes(v, width):
    return jnp.pad(v.reshape(1, -1), ((0, 0), (0, width - v.shape[-1])))


def _rope_tables(positions):
    half = ROPE // 2
    inv_freq = ROPE_BASE ** (-jnp.arange(half, dtype=F32) / half)
    ang = positions.astype(F32).reshape(-1, 1) * inv_freq
    cos, sin = jnp.cos(ang), jnp.sin(ang)
    t = cos.shape[0]
    one, zero = jnp.ones((t, NOPE), F32), jnp.zeros((t, NOPE), F32)
    tail = jnp.zeros((t, HEAD_PAD - QK_B), F32)
    z16 = jnp.zeros((t, half), F32)
    c = jnp.concatenate([one, cos, cos, tail], axis=1)
    sm = jnp.concatenate([zero, -sin, z16, tail], axis=1)
    sp = jnp.concatenate([zero, z16, sin, tail], axis=1)
    return c, sm, sp


def _ffn_fwd(x, gain, wg_t, wu_t, wd, tag):
    n, a, b, hmid = _ffn_up(x, gain, wg_t, wu_t, f"{tag}_up")
    y = _mm([(hmid, wd)], "nn", F32, f"{tag}_down", residual=x, alpha=0.5)
    return y, (x, n, a, b, hmid)


def _ffn_bwd(dy, dyb, saved, gain, wg_t, wu_t, wd, tag, grads, names, hook):
    x, n, a, b, hmid = saved
    da, db = _ffn_down_bwd(dyb, wd, a, b, f"{tag}_down_bwd")
    grads[names[0]] = _mm([(da, n)], "tn", BF16, f"{tag}_dwg")
    grads[names[1]] = _mm([(db, n)], "tn", BF16, f"{tag}_dwu")
    hook("gu", grads[names[1]])
    dx, dxb, g_gain = _mm_rms_bwd([(da, wg_t), (db, wu_t)], x, gain, dy, f"{tag}_dn")
    hook("dn", dx)
    grads[names[2]] = _mm([(hmid, dyb)], "tn", BF16, f"{tag}_dwd", alpha=0.5)
    return dx, dxb, g_gain


def _fold_halves(d):
    return d[:, :HEAD_DIM_A] + d[:, HEAD_DIM_A:]


def _local_step(x, positions, target, layers, smalls, at=None):
    at = at or (lambda point, l, token, grads: None)
    _Order.tokens = ()
    n_batch, s_len, d = x.shape
    t = n_batch * s_len
    xt = x.reshape(t, d)
    tabs = _rope_tables(positions)
    pos_col = positions.reshape(t, 1)
    pos_row = positions.reshape(t // BLOCK, 1, BLOCK)
    saved = []
    get_layer = layers if callable(layers) else layers.__getitem__
    for l in range(len(smalls)):
        w, s = get_layer(l), smalls[l]
        g1, gm, g2 = (s[k].reshape(1, d) for k in ("ffn1_norm", "mix_norm", "ffn2_norm"))
        qg2, kg2 = (jnp.tile(s[k].reshape(1, -1), (1, 2)) for k in ("swa_q_norm", "swa_k_norm"))
        sinks = _pad_lanes(s["swa_sinks"], HEAD_PAD)
        gq, gkv = s["mla_q_lora_norm"].reshape(1, -1), s["mla_kv_lora_norm"].reshape(1, -1)
        gqn, gkn = _pad_lanes(s["mla_q_norm"], HEAD_PAD), _pad_lanes(s["mla_k_norm"], HEAD_PAD)
        x1, sv1 = _ffn_fwd(xt, g1, w["ffn1_w_gate"], w["ffn1_w_up"], w["ffn1_w_down"], f"l{l}_ffn1")
        h, proj = _rms_mm(x1, gm, w["w_in"], f"l{l}_proj")
        at("proj", l, proj, None)
        oa = _swa_fwd(proj, pos_col, pos_row, qg2, kg2, sinks, n_batch, s_len, f"l{l}_swa")
        q, k, v = _mla_pre(proj, tabs, gq, gkv, gqn, gkn, w["mla_w_uq"], w["wk"], w["wv"], f"l{l}_mla_pre")
        ob, lse = _mla_flash_fwd(q, k, v, n_batch, s_len, f"l{l}_mla")
        merged, ya, yb = _merge_fwd(oa, ob, proj, w["w_branch_a"], w["w_branch_b"], f"l{l}_merge")
        x2 = _mm([(merged, w["w_out"])], "nn", F32, f"l{l}_out", residual=x1)
        at("out", l, x2, None)
        x3, sv2 = _ffn_fwd(x2, g2, w["ffn2_w_gate"], w["ffn2_w_up"], w["ffn2_w_down"], f"l{l}_ffn2")
        saved.append((w, sv1, sv2, x1, h, proj, oa, q, k, v, ob, lse, merged, ya, yb,
                      (g1, gm, g2, qg2, kg2, sinks, gq, gkv, gqn, gkn)))
        xt = x3

    dy, dyb, loss = _loss_head(xt, target.reshape(t, d))

    big_grads, small_grads = [None] * len(smalls), [None] * len(smalls)
    for l in reversed(range(len(smalls))):
        w, sv1, sv2, x1, h, proj, oa, q, k, v, ob, lse, merged, ya, yb, gains = saved[l]
        g1, gm, g2, qg2, kg2, sinks, gq, gkv, gqn, gkn = gains
        bg, sg = {}, {}
        dy, dyb, sg["ffn2_norm"] = _ffn_bwd(
            dy, dyb, sv2, g2, w["ffn2_w_gate"], w["ffn2_w_up"], w["ffn2_w_down"], f"l{l}_ffn2", bg, FFN2,
            lambda point, token, l=l, bg=bg: at("ffn2_" + point, l, token, bg))
        dya, dyb_, dga, dgb, doa, dob = _merge_bwd(dyb, w["w_out"], proj, ya, yb, w["w_branch_a"], w["w_branch_b"],
                                                   f"l{l}_merge_bwd")
        at("mixer", l, sg["ffn2_norm"], bg)
        bg["w_out"] = _mm([(merged, dyb)], "tn", BF16, f"l{l}_dwo")
        bg["w_branch_a"] = _mm([(dya, oa)], "tn", BF16, f"l{l}_dwa")
        bg["w_branch_b"] = _mm([(dyb_, ob)], "tn", BF16, f"l{l}_dwb")
        dqa, dkc, dkp, dvc, dvp, dqg, dsk = _swa_bwd(
            proj, pos_col, pos_row, qg2, kg2, sinks, doa, n_batch, s_len, f"l{l}_swa_bwd")
        sg["swa_q_norm"], sg["swa_sinks"] = _fold_halves(dqg), dsk[:, :N_HEADS_A]
        at("mixer_mid", l, dqa, bg)
        dka, dva, dkg = _swa_kv_bwd(proj, kg2, dkc, dkp, dvc, dvp, n_batch, s_len, f"l{l}_swa_kv_bwd")
        sg["swa_k_norm"] = _fold_halves(dkg)
        dq, dk, dv = _mla_flash_bwd(q, k, v, ob, dob, lse, n_batch, s_len, f"l{l}_mla_bwd")
        (dcq, dckv, dkr, g_uq, g_wk, g_wv, sg["mla_q_lora_norm"], sg["mla_kv_lora_norm"], dgqn, dgkn) = _mla_pre_bwd(
            proj, tabs, gq, gkv, gqn, gkn, w["mla_w_uq"], w["wk"], w["wv"], dq, dk, dv, f"l{l}_mla_pre_bwd")
        sg["mla_q_norm"], sg["mla_k_norm"] = dgqn[:, :QK_B], dgkn[:, :QK_B]
        bg["mla_w_uq"], bg["wk"], bg["wv"] = g_uq.astype(BF16), g_wk.astype(BF16), g_wv.astype(BF16)
        dproj = (dga, dgb, dqa, dcq, dka, dva, dckv, dkr)
        bg["w_in"] = _mm([(dproj, h)], "tn", BF16, f"l{l}_dwin")
        dy, dyb, sg["mix_norm"] = _mm_rms_bwd([(dproj, w["w_in"])], x1, gm, dy, f"l{l}_dh")
        at("ffn1", l, sg["mix_norm"], bg)
        dy, dyb, sg["ffn1_norm"] = _ffn_bwd(
            dy, dyb, sv1, g1, w["ffn1_w_gate"], w["ffn1_w_up"], w["ffn1_w_down"], f"l{l}_ffn1", bg, FFN1,
            lambda point, token, l=l, bg=bg: at("ffn1_" + point, l, token, bg))
        big_grads[l], small_grads[l] = bg, sg
        at("done", l, dy, bg)
    return loss, dy.reshape(n_batch, s_len, d), big_grads, small_grads


def _round_up(n, m):
    return (n + m - 1) // m * m


def _flat_layout(piece_shapes, members, row_tile):
    table, off = [], 0
    for l, piece in members:
        rows, k = piece_shapes[piece]
        pr = _round_up(rows * k // LANES, 16)
        table.append(((l, piece), off, pr, rows, k))
        off += pr
    return table, _round_up(off, row_tile)


def _pack_flat(stored, table, total):
    parts, off = [], 0
    for key, o, pr, rows, k in table:
        w = stored[key].reshape(rows * k // LANES, LANES)
        parts.append(jnp.pad(w, ((0, pr - w.shape[0]), (0, 0))))
        off = o + pr
    if total > off:
        parts.append(jnp.zeros((total - off, LANES), parts[0].dtype))
    return jnp.concatenate(parts, axis=0)


def _unpack_flat(flat, table):
    return {key: flat[o:o + rows * k // LANES].reshape(rows, k) for key, o, pr, rows, k in table}


def _gathered_mats(gathered, table, layer):
    return {piece: gathered[:, o:o + rows * k // LANES].reshape(N_DEV * rows, k)
            for (l, piece), o, pr, rows, k in table if l == layer}


def _pack_grads(grads, table, total):
    parts, off = [], 0
    for key, o, pr, rows, k in table:
        g = grads[key].reshape(N_DEV, rows * k // LANES, LANES)
        parts.append(jnp.pad(g, ((0, 0), (0, pr - g.shape[1]), (0, 0))))
        off = o + pr
    if total > off:
        parts.append(jnp.zeros((N_DEV, total - off, LANES), BF16))
    return jnp.concatenate(parts, axis=1)


def _pack_small(params, last=None):
    parts = [params[n][l].reshape(-1).astype(F32) for l in range(DEPTH) for n in SMALL]
    v = jnp.concatenate(parts)
    v = jnp.pad(v, (0, SMALL_ROWS * LANES - 1 - v.shape[0]))
    last = jnp.zeros((1,), F32) if last is None else last.reshape(1)
    return jnp.concatenate([v, last]).reshape(SMALL_ROWS, LANES)


def _unpack_small(flat, shapes):
    v, out, off = flat.reshape(-1), {}, 0
    for l in range(DEPTH):
        for n in SMALL:
            sz = math.prod(shapes[n][1:])
            out.setdefault(n, []).append(v[off:off + sz].reshape(shapes[n][1:]))
            off += sz
    return {n: jnp.stack(p) for n, p in out.items()}


_MESH = pl.DeviceIdType.MESH


def _place():
    return lax.axis_index("x"), lax.axis_index("y"), lax.axis_index("c")


def _handshake(peers):
    barrier = pltpu.get_barrier_semaphore()
    for peer in peers:
        pl.semaphore_signal(barrier, inc=1, device_id=peer, device_id_type=_MESH)
    pl.semaphore_wait(barrier, len(peers))


def _comm_call(body, out_shape, scratch, name, seq_id, spec=_ANY, cost_scale=LINK_COST_SCALE):
    if seq_id is None:
        return pl.pallas_call(body, name=name, out_shape=out_shape, in_specs=[spec, _ANY], out_specs=spec,
                              scratch_shapes=scratch)
    nbytes = cost_scale * math.prod(out_shape.shape) * out_shape.dtype.itemsize
    return pl.kernel(body, out_type=out_shape, mesh=plsc.ScalarSubcoreMesh(axis_name="sequencer", num_cores=1),
                     scratch_types=scratch, name=name, compiler_params=pltpu.CompilerParams(collective_id=seq_id),
                     cost_estimate=pl.CostEstimate(flops=0, transcendentals=0, bytes_accessed=nbytes))


def _all_gather(x_shard, name, vmem=False, seq_id=None, after=None):
    spec = pl.BlockSpec(memory_space=pltpu.VMEM) if vmem else _ANY

    def body(x_ref, after_ref, out_ref, send_sems, recv_sems, local_sem):
        x, y, c = _place()
        me, sibling = (x, y, c), (x, y, 1 - c)
        chips = [(1 - x, y), (x, 1 - y), (1 - x, 1 - y)]
        if seq_id is not None:
            _handshake([sibling] + [(*chip, c) for chip in chips])

        def rows(px, py, pc):
            return out_ref.at[4 * px + 2 * py + pc]

        def copy(k, block, to, src=None):
            return pltpu.make_async_remote_copy(
                src_ref=rows(*block) if src is None else src, dst_ref=rows(*block),
                send_sem=send_sems.at[k], recv_sem=recv_sems.at[k], device_id=to, device_id_type=_MESH)

        mine = pltpu.make_async_copy(x_ref, rows(*me), local_sem)
        mine.start()
        first = [copy(0, me, sibling, src=x_ref)]
        first += [copy(1 + j, me, (*chip, c), src=x_ref) for j, chip in enumerate(chips)]
        for cp in first:
            cp.start()
        passed = [copy(4 + j, (*chip, c), sibling) for j, chip in enumerate(chips)]
        for j, chip in enumerate(chips):
            copy(1 + j, (*chip, c), me).wait_recv()
            passed[j].start()
        copy(0, sibling, me).wait_recv()
        for j, chip in enumerate(chips):
            copy(4 + j, (*chip, 1 - c), me).wait_recv()
        for cp in first + passed:
            cp.wait_send()
        mine.wait()

    return _comm_call(
        body, jax.ShapeDtypeStruct((N_DEV,) + x_shard.shape, x_shard.dtype),
        [pltpu.SemaphoreType.DMA((7,)), pltpu.SemaphoreType.DMA((7,)), pltpu.SemaphoreType.DMA], name, seq_id,
        spec, cost_scale=GATHER_COST_SCALE)(x_shard, x_shard if after is None else after)


def _exchange_cores(g4, name, seq_id=None, after=None):
    n_chip, _, r, w = g4.shape

    def body(g_ref, after_ref, out_ref, send_sems, recv_sems):
        x, y, c = _place()
        if seq_id is not None:
            _handshake([(x, y, 1 - c)])
        copies = [pltpu.make_async_remote_copy(
            src_ref=g_ref.at[q, 1 - c], dst_ref=out_ref.at[q], send_sem=send_sems.at[q], recv_sem=recv_sems.at[q],
            device_id=(x, y, 1 - c), device_id_type=_MESH) for q in range(n_chip)]
        for cp in copies:
            cp.start()
        for cp in copies:
            cp.wait()

    return _comm_call(
        body, jax.ShapeDtypeStruct((n_chip, r, w), g4.dtype),
        [pltpu.SemaphoreType.DMA((n_chip,)), pltpu.SemaphoreType.DMA((n_chip,))], name, seq_id)(g4, g4 if after is None else after)


def _exchange_chips(s1, name, seq_id=None):
    _, r, w = s1.shape

    def body(s_ref, after_ref, out_ref, send_sems, recv_sems):
        x, y, c = _place()
        chips = [(1 - x, y), (x, 1 - y), (1 - x, 1 - y)]
        if seq_id is not None:
            _handshake([(*chip, c) for chip in chips])
        copies = []
        for k, (tx, ty) in enumerate(chips):
            copies.append(pltpu.make_async_remote_copy(
                src_ref=s_ref.at[2 * tx + ty], dst_ref=out_ref.at[k], send_sem=send_sems.at[k],
                recv_sem=recv_sems.at[k], device_id=(tx, ty, c), device_id_type=_MESH))
        for cp in copies:
            cp.start()
        for cp in copies:
            cp.wait()

    return _comm_call(
        body, jax.ShapeDtypeStruct((3, r, w), s1.dtype),
        [pltpu.SemaphoreType.DMA((3,)), pltpu.SemaphoreType.DMA((3,))], name, seq_id)(s1, s1)


def _chip_sum(g4, recv, core, after, name, tr):
    n_chip, _, r, w = g4.shape

    def body(c_ref, a_ref, b_ref, after_ref, o_ref):
        o_ref[...] = (a_ref[...].astype(F32) + b_ref[...].astype(F32)).astype(o_ref.dtype)

    grid_spec = pltpu.PrefetchScalarGridSpec(
        num_scalar_prefetch=1, grid=(n_chip, r // tr),
        in_specs=[pl.BlockSpec((None, None, tr, w), lambda q, i, c: (q, c[0], i, 0)),
                  pl.BlockSpec((None, tr, w), lambda q, i, c: (q, i, 0)), _ANY],
        out_specs=pl.BlockSpec((None, tr, w), lambda q, i, c: (q, i, 0)))
    return pl.pallas_call(
        body, name=name, grid_spec=grid_spec, out_shape=jax.ShapeDtypeStruct((n_chip, r, w), g4.dtype),
        compiler_params=_cp("parallel", "parallel"))(core, g4, recv, after)


def _adam(w, g, m, v):
    m = ADAM_B1 * m + (1.0 - ADAM_B1) * g
    v = ADAM_B2 * v + (1.0 - ADAM_B2) * (g * g)
    m_hat = m / (1.0 - ADAM_B1 ** ADAM_STEP)
    v_hat = v / (1.0 - ADAM_B2 ** ADAM_STEP)
    delta = -ADAM_LR * (m_hat / (jnp.sqrt(v_hat) + ADAM_EPS) + ADAM_WD * w)
    return delta, m, v


def _grad_sum(s1, r2, chip, name, tr):
    _, r, lanes = s1.shape

    def body(c_ref, s_ref, r0_ref, r1_ref, r2_ref, g_out):
        g_out[...] = ((s_ref[...].astype(F32) + r0_ref[...].astype(F32)) + r1_ref[...].astype(F32)) + r2_ref[
            ...].astype(F32)

    row = pl.BlockSpec((tr, lanes), lambda i, c: (i, 0))
    rel = lambda k: pl.BlockSpec((None, tr, lanes), lambda i, c: (k, i, 0))
    grid_spec = pltpu.PrefetchScalarGridSpec(
        num_scalar_prefetch=1, grid=(r // tr,),
        in_specs=[pl.BlockSpec((None, tr, lanes), lambda i, c: (c[0], i, 0)), rel(0), rel(1), rel(2)], out_specs=row)
    return pl.pallas_call(
        body, name=name, grid_spec=grid_spec, out_shape=jax.ShapeDtypeStruct((r, lanes), F32),
        compiler_params=_cp("parallel"))(chip, s1, r2, r2, r2)


def _adam_big(w, g, m, v, name):
    depth, k, n = w.shape
    tk = k if k <= 512 else _tile(k, (256, 128))

    def body(w_ref, g_ref, m_ref, v_ref, d_out, m_out, v_out):
        d, mn, vn = _adam(w_ref[...], g_ref[...], m_ref[...], v_ref[...])
        d_out[...] = d
        m_out[...] = mn
        v_out[...] = vn

    blk = pl.BlockSpec((None, tk, n), lambda l, i: (l, i, 0))
    return pl.pallas_call(
        body, name=name, grid=(depth, k // tk), in_specs=[blk] * 4, out_specs=[blk] * 3,
        out_shape=[jax.ShapeDtypeStruct(w.shape, F32)] * 3, compiler_params=_cp("parallel", "parallel"))(w, g, m, v)


def _adam_small(parts, w, m, v, name):
    rows, lanes = w.shape

    def body(p_ref, w_ref, m_ref, v_ref, g_out, d_out, m_out, v_out):
        g = p_ref[0:rows, :]
        for dev in range(1, N_DEV):
            g = g + p_ref[dev * rows:(dev + 1) * rows, :]
        d, mn, vn = _adam(w_ref[...], g, m_ref[...], v_ref[...])
        g_out[...] = g
        d_out[...] = d
        m_out[...] = mn
        v_out[...] = vn

    return pl.pallas_call(
        body, name=name, out_shape=[jax.ShapeDtypeStruct((rows, lanes), F32)] * 4)(parts, w, m, v)


def kernel(x, positions, ffn1_norm, ffn1_w_gate, ffn1_w_up, ffn1_w_down, mix_norm, w_in, swa_q_norm, swa_k_norm, swa_sinks, mla_q_lora_norm, mla_w_uq, mla_kv_lora_norm, mla_w_ukv, mla_q_norm, mla_k_norm, w_branch_a, w_branch_b, w_out, ffn2_norm, ffn2_w_gate, ffn2_w_up, ffn2_w_down, loss_target, m_ffn1_norm, m_ffn1_w_gate, m_ffn1_w_up, m_ffn1_w_down, m_mix_norm, m_w_in, m_swa_q_norm, m_swa_k_norm, m_swa_sinks, m_mla_q_lora_norm, m_mla_w_uq, m_mla_kv_lora_norm, m_mla_w_ukv, m_mla_q_norm, m_mla_k_norm, m_w_branch_a, m_w_branch_b, m_w_out, m_ffn2_norm, m_ffn2_w_gate, m_ffn2_w_up, m_ffn2_w_down, v_ffn1_norm, v_ffn1_w_gate, v_ffn1_w_up, v_ffn1_w_down, v_mix_norm, v_w_in, v_swa_q_norm, v_swa_k_norm, v_swa_sinks, v_mla_q_lora_norm, v_mla_w_uq, v_mla_kv_lora_norm, v_mla_w_ukv, v_mla_q_norm, v_mla_k_norm, v_w_branch_a, v_w_branch_b, v_w_out, v_ffn2_norm, v_ffn2_w_gate, v_ffn2_w_up, v_ffn2_w_down):
    given = dict(locals())
    params = {n: given[n] for n in WEIGHTS}
    mom1 = {n: given["m_" + n] for n in WEIGHTS}
    mom2 = {n: given["v_" + n] for n in WEIGHTS}
    assert N_HEADS_B == N_DEV
    stored = {(l, piece): w for l in range(DEPTH) for n, _ in BIG for piece, w in _store(n, params[n][l]).items()}
    piece_shapes = {piece: w.shape for (l, piece), w in stored.items() if l == 0}
    gsegs = [(members, tile) + _flat_layout(piece_shapes, members, tile) for members, tile in GATHER_SEGMENTS]
    rsegs = [(members, tile) + _flat_layout(piece_shapes, members, tile) for members, tile in SCATTER_SEGMENTS]

    def members_of(seg, l):
        return [n for sl, n in seg[0] if sl == l]

    cx, cy, cc = _place()
    core = jnp.reshape(cc, (1,)).astype(jnp.int32)
    chip = jnp.reshape(2 * cx + cy, (1,)).astype(jnp.int32)

    gathered = []
    for s, (_, _, table, total) in enumerate(gsegs):
        w_flat = _pack_flat(stored, table, total).astype(BF16)
        gathered.append(_all_gather(w_flat, f"gather_s{s}", seq_id=SEQ_IDS["gather", s]))

    def get_layer(l):
        def build(name):
            for seg, g in zip(gsegs, gathered):
                if name in members_of(seg, l):
                    return _layer_mats(_gathered_mats(g, seg[2], l))
            raise KeyError(name)
        return _LayerWeights(build)

    smalls = [{n: params[n][l] for n in SMALL} for l in range(DEPTH)]

    pending, big_out, layer_grads = {}, [None] * len(rsegs), {}

    def exchange_cores(s):
        _, _, table, total = rsegs[s]
        mine = {(l, n): _unlayer_w_in(layer_grads[l][n]) if n == "w_in" else layer_grads[l][n] for l, n in rsegs[s][0]}
        g_flat = _pack_grads(mine, table, total)
        g4 = g_flat.reshape(N_DEV // 2, 2, total, LANES)
        pending[s] = (g4, _exchange_cores(g4, f"scatter_cores_s{s}", seq_id=SEQ_IDS["cores", s]))

    def exchange_chips(s, after):
        g4, from_core = pending.pop(s)
        s1 = _chip_sum(g4, from_core, core, after, f"sum_cores_s{s}", rsegs[s][1])
        _Order.tokens = (s1,)
        pending[s] = (s1, _exchange_chips(s1, f"scatter_chips_s{s}", seq_id=SEQ_IDS["chips", s]))

    def finish(s):
        s1, from_chips = pending.pop(s)
        big_out[s] = _grad_sum(s1, from_chips, chip, f"grad_sum_s{s}", rsegs[s][1])

    plan = {("ffn1", 1): [("cores", 4)], ("ffn1_gu", 1): [("chips", 4)],
            ("mixer", 0): [("wait", 4), ("cores", 3)], ("mixer_mid", 0): [("chips", 3)],
            ("ffn1", 0): [("wait", 3), ("cores", 2)], ("ffn1_gu", 0): [("cores", 1), ("chips", 2)],
            ("ffn1_dn", 0): [("chips", 1)], ("done", 0): [("cores", 0)]}

    def at(point, l, token, grads):
        if grads is not None:
            layer_grads[l] = grads
        for what, s in plan.get((point, l), ()):
            if what == "cores":
                exchange_cores(s)
            elif what == "chips":
                exchange_chips(s, token)
            else:
                _Order.tokens += (pending[s][1],)

    loss, grad_x, _, small_grads = _local_step(x, positions, loss_target, get_layer, smalls, at)
    exchange_chips(0, grad_x)

    g_small = _pack_small({n: [small_grads[l][n] for l in range(DEPTH)] for n in SMALL}, loss)
    parts = _all_gather(g_small, "gather_small", vmem=True).reshape(N_DEV * SMALL_ROWS, LANES)
    small_out = _adam_small(parts, _pack_small(params), _pack_small(mom1), _pack_small(mom2), "adam_small")
    shapes = {n: params[n].shape for n in SMALL}
    outs = [_unpack_small(small, shapes) for small in small_out]
    loss = small_out[0].reshape(-1)[-1]

    pieces = {}
    for s in reversed(range(len(rsegs))):
        finish(s)
        pieces.update(_unpack_flat(big_out[s], rsegs[s][2]))
    last = SCATTER_SEGMENTS[0][0][0][1]
    for n, tr in sorted(BIG, key=lambda entry: entry[0] == last):
        view = (lambda a: jnp.swapaxes(a, 1, 2)) if tr else (lambda a: a)
        g = jnp.stack([_unstore(n, {p: pieces[l, p] for p in PIECES_OF.get(n, (n,))}) for l in range(DEPTH)])
        updated = _adam_big(view(params[n]), g, view(mom1[n]), view(mom2[n]), f"adam_{n}")
        for tree, leaf in zip(outs, (g,) + tuple(updated)):
            tree[n] = view(leaf)
    return (loss, grad_x, *[o[n] for o in outs for n in WEIGHTS])
```

```python
import math

import jax
import jax.numpy as jnp
from jax import lax
from jax.experimental import pallas as pl
from jax.experimental.pallas import tpu as pltpu
from jax.experimental.pallas import tpu_sc as plsc

F32 = jnp.float32
BF16 = jnp.bfloat16

N_DEV = 8
DEPTH = 2
D_MODEL = 1024
D_FF = 2816
HEAD_DIM_A = 64
N_HEADS_A = 8
N_KV_HEADS_A = 2
GROUP_A = N_HEADS_A // N_KV_HEADS_A
BLOCK = 128
N_HEADS_B = 8
Q_LORA = 256
KV_LORA = 128
NOPE = 64
ROPE = 32
QK_B = NOPE + ROPE
V_B = 64
HEAD_PAD = 128
WIDTH_A = N_HEADS_A * HEAD_DIM_A
WIDTH_B = N_HEADS_B * V_B
WIDTH_BP = N_HEADS_B * HEAD_PAD
KV_A = N_KV_HEADS_A * HEAD_DIM_A
IN_WIDTH = WIDTH_A + 2 * KV_A + Q_LORA + KV_LORA + ROPE + 2 * D_MODEL
ROPE_BASE = 10000.0
EPS = 1e-6
NEG = -1e30
LOG2E = 1.4426950408889634

P_GA, P_GB, P_QA, P_CQ, P_KA, P_VA, P_CKV, P_KR = 0, 1024, 2048, 2560, 2816, 2944, 3072, 3200
PROJ_W = 3328
SWA_HEAD_ORDER = (0, 4, 1, 5, 2, 6, 3, 7)
SWA_HEAD_INVERSE = tuple(SWA_HEAD_ORDER.index(h) for h in range(N_HEADS_A))

ADAM_LR, ADAM_B1, ADAM_B2, ADAM_EPS, ADAM_WD, ADAM_STEP = 0.001, 0.9, 0.999, 1e-08, 0.01, 10

VMEM_LIMIT = 56 * 1024 * 1024
LANES = 1024

BIG = (("ffn1_w_gate", True), ("ffn1_w_up", True), ("ffn1_w_down", False), ("w_in", True), ("mla_w_uq", True),
       ("mla_w_ukv", True), ("w_branch_a", True), ("w_branch_b", True), ("w_out", False),
       ("ffn2_w_gate", True), ("ffn2_w_up", True), ("ffn2_w_down", False))
SMALL = ("ffn1_norm", "mix_norm", "ffn2_norm", "swa_q_norm", "swa_k_norm", "swa_sinks", "mla_q_lora_norm",
         "mla_kv_lora_norm", "mla_q_norm", "mla_k_norm")
WEIGHTS = ("ffn1_norm", "ffn1_w_gate", "ffn1_w_up", "ffn1_w_down", "mix_norm", "w_in", "swa_q_norm", "swa_k_norm",
           "swa_sinks", "mla_q_lora_norm", "mla_w_uq", "mla_kv_lora_norm", "mla_w_ukv", "mla_q_norm", "mla_k_norm",
           "w_branch_a", "w_branch_b", "w_out", "ffn2_norm", "ffn2_w_gate", "ffn2_w_up", "ffn2_w_down")
SMALL_ROWS = 8
FFN1 = ("ffn1_w_gate", "ffn1_w_up", "ffn1_w_down")
FFN2 = ("ffn2_w_gate", "ffn2_w_up", "ffn2_w_down")
MIXER = ("w_in", "mla_w_uq", "wk", "wv", "w_branch_a", "w_branch_b", "w_out")
GATHER_SEGMENTS = (
    (tuple((0, n) for n in FFN1[:2]), 352),
    (((0, FFN1[2]),), 352),
    (tuple((0, n) for n in MIXER), 400),
    (tuple((0, n) for n in FFN2) + tuple((1, n) for n in FFN1), 352),
    (tuple((1, n) for n in MIXER + FFN2), 464),
)
SCATTER_SEGMENTS = (
    (((0, FFN1[2]),), 352),
    (tuple((0, n) for n in FFN1[:2]), 352),
    (tuple((0, n) for n in MIXER), 400),
    (tuple((0, n) for n in FFN2) + tuple((1, n) for n in FFN1), 352),
    (tuple((1, n) for n in MIXER + FFN2), 464),
)
LINK_COST_SCALE = 64
SEQ_IDS = {(kind, s): 1 + 5 * k + s for k, kind in enumerate(("gather", "cores", "chips")) for s in range(5)}


def _cp(*sem):
    return pltpu.CompilerParams(dimension_semantics=sem, vmem_limit_bytes=VMEM_LIMIT)


def _tile(n, prefs):
    for t in prefs:
        if n % t == 0:
            return t
    return n


def _dot(a, b, dims):
    return lax.dot_general(a, b, (dims, ((), ())), preferred_element_type=F32)


_NT = ((1,), (1,))
_NN = ((1,), (0,))
_TN = ((0,), (0,))


_ANY = pl.BlockSpec(memory_space=pl.ANY)


class _Order:
    tokens = ()


def _tc_call(body, *, in_specs, **kw):
    def run(*args):
        tokens, n = _Order.tokens, len(args)
        if not tokens:
            out = pl.pallas_call(body, in_specs=in_specs, **kw)(*args)
        else:
            def chained(*refs):
                return body(*refs[:n], *refs[n + len(tokens):])
            out = pl.pallas_call(chained, in_specs=list(in_specs) + [_ANY] * len(tokens), **kw)(*args, *tokens)
        _Order.tokens = (jax.tree.leaves(out)[0],)
        return out
    return run


def _sigmoid(x):
    return 0.5 * jnp.tanh(0.5 * x) + 0.5


def _chunks(n, width):
    return [(c, min(width, n - c)) for c in range(0, n, width)]


def _mm(pairs, mode, out_dtype, name, residual=None, alpha=1.0):
    for a, b in pairs:
        for piece in (a if isinstance(a, tuple) else (a,)):
            assert piece.dtype == BF16 and b.dtype == BF16, (name, piece.dtype, b.dtype)
    if mode == "tn":
        (a, b), = pairs
        return _mm_tokens(a, b, out_dtype, name, alpha)
    t = pairs[0][0].shape[0]
    n = pairs[0][1].shape[0] if mode == "nt" else pairs[0][1].shape[1]
    tm = _tile(t, (512, 256, 128))
    dims = _NT if mode == "nt" else _NN
    in_specs, args = [], []
    for a, w in pairs:
        in_specs.append(pl.BlockSpec((tm, a.shape[1]), lambda i: (i, 0)))
        in_specs.append(pl.BlockSpec(w.shape, lambda i: (0, 0)))
        args += [a, w]
    if residual is not None:
        in_specs.append(pl.BlockSpec((tm, n), lambda i: (i, 0)))
        args.append(residual)
    n_pairs = len(pairs)

    def body(*refs):
        o_ref = refs[-1]
        for c0, cw in _chunks(n, 512):
            acc = None
            for p in range(n_pairs):
                w_ref = refs[2 * p + 1]
                w = w_ref[c0:c0 + cw, :] if mode == "nt" else w_ref[:, c0:c0 + cw]
                d = _dot(refs[2 * p][...], w, dims)
                acc = d if acc is None else acc + d
            if alpha != 1.0:
                acc = acc * alpha
            if residual is not None:
                acc = refs[2 * n_pairs][:, c0:c0 + cw] + acc
            o_ref[:, c0:c0 + cw] = acc.astype(out_dtype)

    return _tc_call(
        body, name=name, grid=(t // tm,), in_specs=in_specs, out_specs=pl.BlockSpec((tm, n), lambda i: (i, 0)),
        out_shape=jax.ShapeDtypeStruct((t, n), out_dtype), compiler_params=_cp("parallel"))(*args)


def _mm_tokens(a, b, out_dtype, name, alpha):
    pieces = a if isinstance(a, tuple) else (a,)
    t = b.shape[0]
    widths = [p.shape[1] for p in pieces]
    m, n = sum(widths), b.shape[1]
    tk = _tile(t, (512, 256, 128))
    n_pieces = len(pieces)

    def body(*refs):
        b_ref, o_ref, acc_ref = refs[n_pieces:]
        k = pl.program_id(0)

        @pl.when(k == 0)
        def _():
            acc_ref[...] = jnp.zeros_like(acc_ref)

        off = 0
        for a_ref, width in zip(refs[:n_pieces], widths):
            for c0, cw in _chunks(width, 512):
                acc_ref[off + c0:off + c0 + cw, :] += _dot(a_ref[:, c0:c0 + cw], b_ref[...], _TN)
            off += width

        @pl.when(k == pl.num_programs(0) - 1)
        def _():
            o_ref[...] = (acc_ref[...] * alpha).astype(out_dtype)

    return _tc_call(
        body, name=name, grid=(t // tk,),
        in_specs=[pl.BlockSpec((tk, w), lambda k: (k, 0)) for w in widths] + [pl.BlockSpec((tk, n), lambda k: (k, 0))],
        out_specs=pl.BlockSpec((m, n), lambda k: (0, 0)), out_shape=jax.ShapeDtypeStruct((m, n), out_dtype),
        scratch_shapes=[pltpu.VMEM((m, n), F32)], compiler_params=_cp("arbitrary"))(*pieces, b)


def _rms_mm(x, gain, w, name):
    t, d = x.shape
    n = w.shape[0]
    tm = _tile(t, (512, 256, 128))

    def body(x_ref, g_ref, w_ref, h_ref, o_ref):
        xv = x_ref[...]
        r = lax.rsqrt(jnp.mean(xv * xv, axis=1, keepdims=True) + EPS)
        hv = (xv * r * g_ref[...]).astype(BF16)
        h_ref[...] = hv
        for c0, cw in _chunks(n, 512):
            o_ref[:, c0:c0 + cw] = _dot(hv, w_ref[c0:c0 + cw, :], _NT)

    row = pl.BlockSpec((tm, d), lambda i: (i, 0))
    return _tc_call(
        body, name=name, grid=(t // tm,),
        in_specs=[row, pl.BlockSpec((1, d), lambda i: (0, 0)), pl.BlockSpec(w.shape, lambda i: (0, 0))],
        out_specs=[row, pl.BlockSpec((tm, n), lambda i: (i, 0))],
        out_shape=[jax.ShapeDtypeStruct((t, d), BF16), jax.ShapeDtypeStruct((t, n), F32)],
        compiler_params=_cp("parallel"))(x, gain, w)


def _mm_rms_bwd(pairs, x, gain, res, name):
    t, d = x.shape
    tm = _tile(t, (512, 256, 128))
    acts, weights, entries = [], [], []
    for a, w in pairs:
        k0 = 0
        for piece in (a if isinstance(a, tuple) else (a,)):
            assert piece.dtype == BF16 and w.dtype == BF16, (name, piece.dtype, w.dtype)
            entries.append((len(acts), len(weights), k0, piece.shape[1]))
            acts.append(piece)
            k0 += piece.shape[1]
        assert k0 == w.shape[0], (name, k0, w.shape)
        weights.append(w)
    n_acts = len(acts)

    def body(*refs):
        x_ref, g_ref, res_ref, dx_ref, dxb_ref, dg_ref, dn_ref = refs[n_acts + len(weights):]
        for c0, cw in _chunks(d, 512):
            acc = None
            for ai, wi, k0, kw in entries:
                part = _dot(refs[ai][...], refs[n_acts + wi][k0:k0 + kw, c0:c0 + cw], _NN)
                acc = part if acc is None else acc + part
            dn_ref[:, c0:c0 + cw] = acc
        xv = x_ref[...]
        r = lax.rsqrt(jnp.mean(xv * xv, axis=1, keepdims=True) + EPS)
        xh = xv * r
        dnv = dn_ref[...]
        dxh = dnv * g_ref[...]
        dx = res_ref[...] + r * (dxh - xh * jnp.mean(dxh * xh, axis=1, keepdims=True))
        dx_ref[...] = dx
        dxb_ref[...] = dx.astype(BF16)

        @pl.when(pl.program_id(0) == 0)
        def _():
            dg_ref[...] = jnp.zeros_like(dg_ref)

        dg_ref[...] += jnp.sum(dnv * xh, axis=0, keepdims=True)

    in_specs = [pl.BlockSpec((tm, a.shape[1]), lambda i: (i, 0)) for a in acts]
    in_specs += [pl.BlockSpec(w.shape, lambda i: (0, 0)) for w in weights]
    row = pl.BlockSpec((tm, d), lambda i: (i, 0))
    one = pl.BlockSpec((1, d), lambda i: (0, 0))
    return _tc_call(
        body, name=name, grid=(t // tm,), in_specs=in_specs + [row, one, row], out_specs=[row, row, one],
        out_shape=[jax.ShapeDtypeStruct((t, d), F32), jax.ShapeDtypeStruct((t, d), BF16),
                   jax.ShapeDtypeStruct((1, d), F32)],
        scratch_shapes=[pltpu.VMEM((tm, d), F32)],
        compiler_params=_cp("arbitrary"))(*acts, *weights, x, gain, res)


def _ffn_up(x, gain, wg_t, wu_t, name):
    t, d = x.shape
    f = wg_t.shape[0]
    tm = _tile(t, (512, 256, 128))

    def body(x_ref, g_ref, wg_ref, wu_ref, n_ref, a_ref, b_ref, h_ref):
        xv = x_ref[...]
        r = lax.rsqrt(jnp.mean(xv * xv, axis=1, keepdims=True) + EPS)
        nv = (xv * r * g_ref[...]).astype(BF16)
        n_ref[...] = nv
        for c0, cw in _chunks(f, 256):
            a = _dot(nv, wg_ref[c0:c0 + cw, :], _NT)
            b = _dot(nv, wu_ref[c0:c0 + cw, :], _NT)
            a_ref[:, c0:c0 + cw] = a.astype(BF16)
            b_ref[:, c0:c0 + cw] = b.astype(BF16)
            h_ref[:, c0:c0 + cw] = (a * _sigmoid(a) * b).astype(BF16)

    w_spec = pl.BlockSpec((f, d), lambda i: (0, 0))
    x_spec = pl.BlockSpec((tm, d), lambda i: (i, 0))
    o_spec = pl.BlockSpec((tm, f), lambda i: (i, 0))
    o_shape = jax.ShapeDtypeStruct((t, f), BF16)
    return _tc_call(
        body, name=name, grid=(t // tm,), in_specs=[x_spec, pl.BlockSpec((1, d), lambda i: (0, 0)), w_spec, w_spec],
        out_specs=[x_spec] + [o_spec] * 3, out_shape=[jax.ShapeDtypeStruct((t, d), BF16)] + [o_shape] * 3,
        compiler_params=_cp("parallel"))(x, gain, wg_t, wu_t)


def _ffn_down_bwd(dxb, wd, a, b, name):
    t, d = dxb.shape
    f = wd.shape[0]
    tm = _tile(t, (512, 256, 128))

    def body(dx_ref, wd_ref, a_ref, b_ref, da_ref, db_ref):
        dxv = dx_ref[...]
        for c0, cw in _chunks(f, 256):
            dh = 0.5 * _dot(dxv, wd_ref[c0:c0 + cw, :], _NT)
            av = a_ref[:, c0:c0 + cw].astype(F32)
            bv = b_ref[:, c0:c0 + cw].astype(F32)
            sg = _sigmoid(av)
            da_ref[:, c0:c0 + cw] = (dh * bv * (sg * (1.0 + av * (1.0 - sg)))).astype(BF16)
            db_ref[:, c0:c0 + cw] = (dh * (av * sg)).astype(BF16)

    o_spec = pl.BlockSpec((tm, f), lambda i: (i, 0))
    o_shape = jax.ShapeDtypeStruct((t, f), BF16)
    return _tc_call(
        body, name=name, grid=(t // tm,),
        in_specs=[pl.BlockSpec((tm, d), lambda i: (i, 0)), pl.BlockSpec((f, d), lambda i: (0, 0)), o_spec, o_spec],
        out_specs=[o_spec] * 2, out_shape=[o_shape] * 2, compiler_params=_cp("parallel"))(dxb, wd, a, b)


def _loss_head(y, target):
    t, d = y.shape
    tm = _tile(t, (512, 256, 128))

    def body(y_ref, t_ref, dy_ref, dyb_ref, loss_ref, acc_ref):
        i = pl.program_id(0)
        e = y_ref[...] - t_ref[...]
        dy = e * (1.0 / d)
        dy_ref[...] = dy
        dyb_ref[...] = dy.astype(BF16)

        @pl.when(i == 0)
        def _():
            acc_ref[...] = jnp.zeros_like(acc_ref)

        acc_ref[...] += jnp.sum(e * e, axis=0, keepdims=True)

        @pl.when(i == pl.num_programs(0) - 1)
        def _():
            loss_ref[...] = jnp.sum(acc_ref[...], axis=1, keepdims=True) * (0.5 / d)

    row = pl.BlockSpec((tm, d), lambda i: (i, 0))
    return _tc_call(
        body, name="loss_head", grid=(t // tm,), in_specs=[row, row],
        out_specs=[row, row, pl.BlockSpec((1, 1), lambda i: (0, 0))],
        out_shape=[jax.ShapeDtypeStruct((t, d), F32), jax.ShapeDtypeStruct((t, d), BF16),
                   jax.ShapeDtypeStruct((1, 1), F32)],
        scratch_shapes=[pltpu.VMEM((1, d), F32)], compiler_params=_cp("arbitrary"))(y, target)


def _merge_fwd(oa, ob, proj, wa_t, wb_t, name):
    t = oa.shape[0]
    d = wa_t.shape[0]
    tm = _tile(t, (512, 256, 128))

    def body(oa_ref, ob_ref, ga_ref, gb_ref, wa_ref, wb_ref, mg_ref, ya_ref, yb_ref):
        oav, obv = oa_ref[...], ob_ref[...]
        for c0, cw in _chunks(d, 512):
            cs = slice(c0, c0 + cw)
            ya = _dot(oav, wa_ref[cs, :], _NT)
            yb = _dot(obv, wb_ref[cs, :], _NT)
            mg_ref[:, cs] = (_sigmoid(ga_ref[:, cs]) * ya + _sigmoid(gb_ref[:, cs]) * yb).astype(BF16)
            ya_ref[:, cs] = ya.astype(BF16)
            yb_ref[:, cs] = yb.astype(BF16)

    o_spec = pl.BlockSpec((tm, d), lambda i: (i, 0))
    o_shape = jax.ShapeDtypeStruct((t, d), BF16)
    return _tc_call(
        body, name=name, grid=(t // tm,),
        in_specs=[pl.BlockSpec((tm, oa.shape[1]), lambda i: (i, 0)), pl.BlockSpec((tm, ob.shape[1]), lambda i: (i, 0)),
                  pl.BlockSpec((tm, d), lambda i: (i, P_GA // d)), pl.BlockSpec((tm, d), lambda i: (i, P_GB // d)),
                  pl.BlockSpec(wa_t.shape, lambda i: (0, 0)), pl.BlockSpec(wb_t.shape, lambda i: (0, 0))],
        out_specs=[o_spec] * 3, out_shape=[o_shape] * 3,
        compiler_params=_cp("parallel"))(oa, ob, proj, proj, wa_t, wb_t)


def _merge_bwd(dxb, wo, proj, ya, yb, wa_t, wb_t, name):
    t, d = dxb.shape
    tm = _tile(t, (512, 256, 128))
    wa_in, wb_in = wa_t.shape[1], wb_t.shape[1]

    def body(dx_ref, wo_ref, ga_ref, gb_ref, ya_ref, yb_ref, wa_ref, wb_ref,
             dya_ref, dyb_ref, dga_ref, dgb_ref, doa_ref, dob_ref):
        dxv = dx_ref[...]
        doa = jnp.zeros((tm, wa_in), F32)
        dob = jnp.zeros((tm, wb_in), F32)
        for c0, cw in _chunks(d, 512):
            cs = slice(c0, c0 + cw)
            dm = _dot(dxv, wo_ref[cs, :], _NT)
            sa = _sigmoid(ga_ref[:, cs])
            sb = _sigmoid(gb_ref[:, cs])
            dya = (dm * sa).astype(BF16)
            dyb = (dm * sb).astype(BF16)
            dya_ref[:, cs] = dya
            dyb_ref[:, cs] = dyb
            dga_ref[:, cs] = (dm * ya_ref[:, cs].astype(F32) * (sa * (1.0 - sa))).astype(BF16)
            dgb_ref[:, cs] = (dm * yb_ref[:, cs].astype(F32) * (sb * (1.0 - sb))).astype(BF16)
            doa = doa + _dot(dya, wa_ref[cs, :], _NN)
            dob = dob + _dot(dyb, wb_ref[cs, :], _NN)
        doa_ref[...] = doa.astype(BF16)
        dob_ref[...] = dob.astype(BF16)

    o_spec = pl.BlockSpec((tm, d), lambda i: (i, 0))
    o_shape = jax.ShapeDtypeStruct((t, d), BF16)
    return _tc_call(
        body, name=name, grid=(t // tm,),
        in_specs=[o_spec, pl.BlockSpec((d, d), lambda i: (0, 0)),
                  pl.BlockSpec((tm, d), lambda i: (i, P_GA // d)), pl.BlockSpec((tm, d), lambda i: (i, P_GB // d)),
                  o_spec, o_spec, pl.BlockSpec(wa_t.shape, lambda i: (0, 0)), pl.BlockSpec(wb_t.shape, lambda i: (0, 0))],
        out_specs=[o_spec] * 4 + [pl.BlockSpec((tm, wa_in), lambda i: (i, 0)), pl.BlockSpec((tm, wb_in), lambda i: (i, 0))],
        out_shape=[o_shape] * 4 + [jax.ShapeDtypeStruct((t, wa_in), BF16), jax.ShapeDtypeStruct((t, wb_in), BF16)],
        compiler_params=_cp("parallel"))(dxb, wo, proj, proj, ya, yb, wa_t, wb_t)


def _swa_common(has_prev, pq, pk):
    dist = (pq - pk).astype(F32)
    row = lax.broadcasted_iota(jnp.int32, (BLOCK, 2 * BLOCK), 0)
    col = lax.broadcasted_iota(jnp.int32, (BLOCK, 2 * BLOCK), 1)
    diff = row + BLOCK - col
    valid = (diff >= 0) & (diff < BLOCK) & (has_prev | (col >= BLOCK))
    return jnp.concatenate([dist] * N_HEADS_A, axis=0), jnp.concatenate([valid] * N_HEADS_A, axis=0)


def _half_sum(x, lo):
    s_lo = jnp.sum(jnp.where(lo, x, 0.0), axis=1, keepdims=True)
    s_hi = jnp.sum(jnp.where(lo, 0.0, x), axis=1, keepdims=True)
    return jnp.where(lo, s_lo, s_hi)


def _norm2(x, gain2, lo):
    r = lax.rsqrt(_half_sum(x * x, lo) * (1.0 / HEAD_DIM_A) + EPS)
    xh = x * r
    return xh * gain2, xh, r


def _norm2_bwd(d, xh, r, gain2, lo):
    dxh = d * gain2
    return r * (dxh - xh * (_half_sum(dxh * xh, lo) * (1.0 / HEAD_DIM_A)))


def _swa_stack(tiles, lo):
    zero = jnp.zeros_like(tiles[0])
    return jnp.concatenate([jnp.where(lo, t, zero) for t in tiles] + [jnp.where(lo, zero, t) for t in tiles], axis=0)


def _swa_unstack(x8, j, lo):
    return jnp.where(lo, x8[j * BLOCK:(j + 1) * BLOCK], x8[(GROUP_A + j) * BLOCK:(GROUP_A + j + 1) * BLOCK])


def _swa_head_columns(sk_ref):
    slope = jnp.concatenate([jnp.full((BLOCK, 1), 2.0 ** (-(h + 1)), F32) for h in range(N_HEADS_A)], axis=0)
    sink = jnp.concatenate([jnp.broadcast_to(sk_ref[:, h:h + 1], (BLOCK, 1)) for h in range(N_HEADS_A)], axis=0)
    return slope, sink


def _swa_blocks_per_step(s_len):
    nb = s_len // BLOCK
    return 4 if nb % 4 == 0 and nb >= 8 else 2 if nb % 2 == 0 else 1


def _swa_specs(s_len):
    nb, qb = s_len // BLOCK, _swa_blocks_per_step(s_len)
    ns, rows = nb // qb, qb * BLOCK

    def step(b, j):
        return b * ns + j

    def prev(b, j):
        return b * nb + jnp.maximum(qb * j - 1, 0)

    q_spec = pl.BlockSpec((rows, WIDTH_A), lambda b, j: (step(b, j), P_QA // WIDTH_A))
    kc_spec = pl.BlockSpec((rows, KV_A), lambda b, j: (step(b, j), P_KA // KV_A))
    kp_spec = pl.BlockSpec((BLOCK, KV_A), lambda b, j: (prev(b, j), P_KA // KV_A))
    vc_spec = pl.BlockSpec((rows, KV_A), lambda b, j: (step(b, j), P_VA // KV_A))
    vp_spec = pl.BlockSpec((BLOCK, KV_A), lambda b, j: (prev(b, j), P_VA // KV_A))
    pq_spec = pl.BlockSpec((rows, 1), lambda b, j: (step(b, j), 0))
    pkc_spec = pl.BlockSpec((qb, 1, BLOCK), lambda b, j: (step(b, j), 0, 0))
    pkp_spec = pl.BlockSpec((1, 1, BLOCK), lambda b, j: (prev(b, j), 0, 0))
    return qb, ns, step, [q_spec, kc_spec, kp_spec, vc_spec, vp_spec, pq_spec, pkc_spec, pkp_spec]


def _swa_stage_probs(qb, q_ref, kc_ref, kp_ref, vc_ref, vp_ref, pq_ref, pkc_ref, pkp_ref, qg_ref, kg_ref, sk_ref, lo):
    first = pl.program_id(1) * qb
    kk_all = _norm2(jnp.concatenate([kp_ref[...], kc_ref[...]], axis=0), kg_ref[...], lo)[0].astype(BF16)
    vv_all = jnp.concatenate([vp_ref[...], vc_ref[...]], axis=0).astype(BF16)
    pk_all = jnp.concatenate([pkp_ref[0]] + [pkc_ref[s] for s in range(qb)], axis=1)
    rows = [slice(s * BLOCK, (s + 1) * BLOCK) for s in range(qb)]
    keys = [slice(s * BLOCK, (s + 2) * BLOCK) for s in range(qb)]
    masks = [_swa_common(first + s > 0, pq_ref[rows[s], :], pk_all[:, keys[s]]) for s in range(qb)]
    qs = [[_norm2(q_ref[r, j * HEAD_PAD:(j + 1) * HEAD_PAD], qg_ref[...], lo) for j in range(GROUP_A)] for r in rows]
    q8 = [_swa_stack([q[0] for q in tiles], lo).astype(BF16) for tiles in qs]
    kk = [kk_all[ks] for ks in keys]
    vv = [vv_all[ks] for ks in keys]
    slope, sink = _swa_head_columns(sk_ref)
    s = [_dot(q8[b], kk[b], _NT) * (HEAD_DIM_A ** -0.5) - slope * masks[b][0] for b in range(qb)]
    s = [jnp.where(masks[b][1], s[b], NEG) for b in range(qb)]
    m = [jnp.maximum(jnp.max(x, axis=1, keepdims=True), sink) for x in s]
    e = [jnp.exp(x - mx) for x, mx in zip(s, m)]
    es = [jnp.exp(sink - mx) for mx in m]
    inv = [1.0 / (jnp.sum(x, axis=1, keepdims=True) + y) for x, y in zip(e, es)]
    p = [x * i for x, i in zip(e, inv)]
    ps = [y * i for y, i in zip(es, inv)]
    return rows, qs, q8, kk, vv, p, ps


def _swa_fwd(proj, pos_col, pos_row, qg2, kg2, sinks, n_batch, s_len, name):
    t = proj.shape[0]
    qb, ns, step, specs = _swa_specs(s_len)
    small = pl.BlockSpec((1, HEAD_PAD), lambda b, j: (0, 0))

    def body(q_ref, kc_ref, kp_ref, vc_ref, vp_ref, pq_ref, pkc_ref, pkp_ref, qg_ref, kg_ref, sk_ref, o_ref):
        lo = lax.broadcasted_iota(jnp.int32, (1, HEAD_PAD), 1) < HEAD_DIM_A
        rows, _, _, _, vv, p, _ = _swa_stage_probs(qb, q_ref, kc_ref, kp_ref, vc_ref, vp_ref, pq_ref, pkc_ref, pkp_ref,
                                                   qg_ref, kg_ref, sk_ref, lo)
        o8 = [_dot(p[b].astype(BF16), vv[b], _NN) for b in range(qb)]
        for b in range(qb):
            for j in range(GROUP_A):
                o_ref[rows[b], j * HEAD_PAD:(j + 1) * HEAD_PAD] = _swa_unstack(o8[b], j, lo).astype(BF16)

    return _tc_call(
        body, name=name, grid=(n_batch, ns), in_specs=specs + [small, small, small],
        out_specs=pl.BlockSpec((qb * BLOCK, WIDTH_A), lambda b, j: (step(b, j), 0)),
        out_shape=jax.ShapeDtypeStruct((t, WIDTH_A), BF16),
        compiler_params=_cp("parallel", "parallel"))(proj, proj, proj, proj, proj, pos_col, pos_row, pos_row,
                                                     qg2, kg2, sinks)


def _swa_bwd(proj, pos_col, pos_row, qg2, kg2, sinks, do, n_batch, s_len, name):
    t = proj.shape[0]
    qb, ns, step, specs = _swa_specs(s_len)
    small = pl.BlockSpec((1, HEAD_PAD), lambda b, j: (0, 0))
    scale = HEAD_DIM_A ** -0.5

    def body(q_ref, kc_ref, kp_ref, vc_ref, vp_ref, pq_ref, pkc_ref, pkp_ref, qg_ref, kg_ref, sk_ref, do_ref,
             dq_ref, dkc_ref, dkp_ref, dvc_ref, dvp_ref, dqg_ref, dsk_ref):
        @pl.when((pl.program_id(0) == 0) & (pl.program_id(1) == 0))
        def _():
            dqg_ref[...] = jnp.zeros_like(dqg_ref)
            dsk_ref[...] = jnp.zeros_like(dsk_ref)

        lane = lax.broadcasted_iota(jnp.int32, (1, HEAD_PAD), 1)
        lo = lane < HEAD_DIM_A
        rows, qs, q8, kk, vv, p, ps = _swa_stage_probs(qb, q_ref, kc_ref, kp_ref, vc_ref, vp_ref, pq_ref, pkc_ref,
                                                       pkp_ref, qg_ref, kg_ref, sk_ref, lo)
        blocks = range(qb)
        do8 = [_swa_stack([do_ref[r, j * HEAD_PAD:(j + 1) * HEAD_PAD] for j in range(GROUP_A)], lo) for r in rows]
        dp = [_dot(do8[b], vv[b], _NT) for b in blocks]
        delta = [jnp.sum(p[b] * dp[b], axis=1, keepdims=True) for b in blocks]
        ds = [(p[b] * (dp[b] - delta[b]) * scale).astype(BF16) for b in blocks]
        dsink = [ps[b] * delta[b] for b in blocks]
        dvv = [_dot(p[b].astype(BF16), do8[b], _TN) for b in blocks]
        dkk = [_dot(ds[b], q8[b], _TN) for b in blocks]
        dq8 = [_dot(ds[b], kk[b], _NN) for b in blocks]
        dsk = jnp.zeros((1, HEAD_PAD), F32)
        dqg = jnp.zeros((1, HEAD_PAD), F32)
        for b in blocks:
            for h in range(N_HEADS_A):
                dsk = dsk + jnp.where(lane == h, -jnp.sum(dsink[b][h * BLOCK:(h + 1) * BLOCK]), 0.0)
            for j in range(GROUP_A):
                _, xh, r = qs[b][j]
                dqn = _swa_unstack(dq8[b], j, lo)
                dqg = dqg + jnp.sum(dqn * xh, axis=0, keepdims=True)
                dq_ref[rows[b], j * HEAD_PAD:(j + 1) * HEAD_PAD] = _norm2_bwd(dqn, xh, r, qg_ref[...], lo).astype(BF16)
            dkp_ref[rows[b], :] = dkk[b][:BLOCK]
            dkc_ref[rows[b], :] = dkk[b][BLOCK:]
            dvp_ref[rows[b], :] = dvv[b][:BLOCK]
            dvc_ref[rows[b], :] = dvv[b][BLOCK:]
        dqg_ref[...] += dqg
        dsk_ref[...] += dsk

    kv_out = pl.BlockSpec((qb * BLOCK, KV_A), lambda b, j: (step(b, j), 0))
    kv_shape = jax.ShapeDtypeStruct((t, KV_A), F32)
    wide = pl.BlockSpec((qb * BLOCK, WIDTH_A), lambda b, j: (step(b, j), 0))
    return _tc_call(
        body, name=name, grid=(n_batch, ns), in_specs=specs + [small, small, small, wide],
        out_specs=[wide, kv_out, kv_out, kv_out, kv_out, small, small],
        out_shape=[jax.ShapeDtypeStruct((t, WIDTH_A), BF16), kv_shape, kv_shape, kv_shape, kv_shape,
                   jax.ShapeDtypeStruct((1, HEAD_PAD), F32), jax.ShapeDtypeStruct((1, HEAD_PAD), F32)],
        compiler_params=_cp("arbitrary", "arbitrary"))(proj, proj, proj, proj, proj, pos_col, pos_row, pos_row,
                                                       qg2, kg2, sinks, do)


def _swa_kv_bwd(proj, kg2, dkc, dkp, dvc, dvp, n_batch, s_len, name):
    t = proj.shape[0]
    nb, qb = s_len // BLOCK, _swa_blocks_per_step(s_len)
    ns, rows = nb // qb, qb * BLOCK

    def step(b, j):
        return b * ns + j

    def edge(b, j):
        return b * nb + jnp.minimum(qb * (j + 1), nb - 1)

    def body(k_ref, kg_ref, dkc_ref, dkp_ref, dkp_edge, dvc_ref, dvp_ref, dvp_edge, dk_ref, dv_ref, dkg_ref):
        @pl.when((pl.program_id(0) == 0) & (pl.program_id(1) == 0))
        def _():
            dkg_ref[...] = jnp.zeros_like(dkg_ref)

        lo = lax.broadcasted_iota(jnp.int32, (1, HEAD_PAD), 1) < HEAD_DIM_A
        has_next = (pl.program_id(1) < ns - 1).astype(F32)

        def from_next(p_ref, edge_ref):
            return jnp.concatenate([p_ref[BLOCK:, :], has_next * edge_ref[...]], axis=0)

        dkn = dkc_ref[...] + from_next(dkp_ref, dkp_edge)
        dv_ref[...] = (dvc_ref[...] + from_next(dvp_ref, dvp_edge)).astype(BF16)
        _, xh, r = _norm2(k_ref[...], kg_ref[...], lo)
        dkg_ref[...] += jnp.sum(dkn * xh, axis=0, keepdims=True)
        dk_ref[...] = _norm2_bwd(dkn, xh, r, kg_ref[...], lo).astype(BF16)

    cur = pl.BlockSpec((rows, KV_A), lambda b, j: (step(b, j), 0))
    nxt = pl.BlockSpec((BLOCK, KV_A), lambda b, j: (edge(b, j), 0))
    small = pl.BlockSpec((1, HEAD_PAD), lambda b, j: (0, 0))
    return _tc_call(
        body, name=name, grid=(n_batch, ns),
        in_specs=[pl.BlockSpec((rows, KV_A), lambda b, j: (step(b, j), P_KA // KV_A)), small, cur, cur, nxt, cur, cur,
                  nxt],
        out_specs=[cur, cur, small],
        out_shape=[jax.ShapeDtypeStruct((t, KV_A), BF16), jax.ShapeDtypeStruct((t, KV_A), BF16),
                   jax.ShapeDtypeStruct((1, HEAD_PAD), F32)],
        compiler_params=_cp("arbitrary", "arbitrary"))(proj, kg2, dkc, dkp, dkp, dvc, dvp, dvp)


def _lora_norm(x, gain):
    r = lax.rsqrt(jnp.mean(x * x, axis=1, keepdims=True) + EPS)
    xh = x * r
    return xh * gain, xh, r


def _mla_in_specs(tm):
    row = lambda w, off: pl.BlockSpec((tm, w), lambda i: (i, off // w))
    one = lambda w: pl.BlockSpec((1, w), lambda i: (0, 0))
    full = lambda r, c: pl.BlockSpec((r, c), lambda i: (0, 0))
    tab = pl.BlockSpec((tm, HEAD_PAD), lambda i: (i, 0))
    return [row(Q_LORA, P_CQ), row(KV_LORA, P_CKV), row(HEAD_PAD, P_KR), tab, tab, tab,
            one(Q_LORA), one(KV_LORA), one(HEAD_PAD), one(HEAD_PAD),
            full(WIDTH_BP, Q_LORA), full(WIDTH_BP, KV_LORA), full(WIDTH_BP, KV_LORA)]


def _mla_pre(proj, tabs, gq, gkv, gqn, gkn, wuq, wk, wv, name):
    t = proj.shape[0]
    tm = _tile(t, (512, 256, 128))

    def body(cq_ref, ckv_ref, kr_ref, c_ref, sm_ref, sp_ref, gq_ref, gkv_ref, gqn_ref, gkn_ref,
             wuq_ref, wk_ref, wv_ref, q_ref, k_ref, v_ref):
        cqn = _lora_norm(cq_ref[...], gq_ref[...])[0].astype(BF16)
        ckvn = _lora_norm(ckv_ref[...], gkv_ref[...])[0].astype(BF16)
        q_raw = _dot(cqn, wuq_ref[...], _NT)
        k_raw = _dot(ckvn, wk_ref[...], _NT)
        v_ref[...] = _dot(ckvn, wv_ref[...], _NT).astype(BF16)
        kr = pltpu.roll(kr_ref[...], NOPE, 1)
        c, sm, sp = c_ref[...], sm_ref[...], sp_ref[...]
        heads = [slice(h * HEAD_PAD, (h + 1) * HEAD_PAD) for h in range(N_HEADS_B)]
        raw = [q_raw[:, hs] for hs in heads] + [k_raw[:, hs] + kr for hs in heads]
        gains = [gqn_ref[...]] * N_HEADS_B + [gkn_ref[...]] * N_HEADS_B
        sq = [jnp.sum(x * x, axis=1, keepdims=True) for x in raw]
        normed = [x * lax.rsqrt(s * (1.0 / QK_B) + EPS) * g for x, s, g in zip(raw, sq, gains)]
        up = [pltpu.roll(x, HEAD_PAD - ROPE // 2, 1) for x in normed]
        down = [pltpu.roll(x, ROPE // 2, 1) for x in normed]
        roped = [(x * c + u * sm + d * sp).astype(BF16) for x, u, d in zip(normed, up, down)]
        for h, hs in enumerate(heads):
            q_ref[:, hs] = roped[h]
            k_ref[:, hs] = roped[N_HEADS_B + h]

    o_spec = pl.BlockSpec((tm, WIDTH_BP), lambda i: (i, 0))
    o_shape = jax.ShapeDtypeStruct((t, WIDTH_BP), BF16)
    return _tc_call(
        body, name=name, grid=(t // tm,), in_specs=_mla_in_specs(tm), out_specs=[o_spec] * 3,
        out_shape=[o_shape] * 3, compiler_params=_cp("parallel"))(
            proj, proj, proj, *tabs, gq, gkv, gqn, gkn, wuq, wk, wv)


def _mla_pre_bwd(proj, tabs, gq, gkv, gqn, gkn, wuq, wk, wv, dq, dk, dv, name):
    t = proj.shape[0]
    tm = _tile(t, (512, 256, 128))

    def body(cq_ref, ckv_ref, kr_ref, c_ref, sm_ref, sp_ref, gq_ref, gkv_ref, gqn_ref, gkn_ref,
             wuq_ref, wk_ref, wv_ref, dq_ref, dk_ref, dv_ref,
             dcq_ref, dckv_ref, dkr_ref, dwuq_ref, dwk_ref, dwv_ref, dgq_ref, dgkv_ref, dgqn_ref, dgkn_ref,
             dqraw_ref, dkraw_ref):
        @pl.when(pl.program_id(0) == 0)
        def _():
            for r in (dwuq_ref, dwk_ref, dwv_ref, dgq_ref, dgkv_ref, dgqn_ref, dgkn_ref):
                r[...] = jnp.zeros_like(r)

        cqn_f, cq_xh, cq_r = _lora_norm(cq_ref[...], gq_ref[...])
        ckvn_f, ckv_xh, ckv_r = _lora_norm(ckv_ref[...], gkv_ref[...])
        cqn, ckvn = cqn_f.astype(BF16), ckvn_f.astype(BF16)
        q_raw = _dot(cqn, wuq_ref[...], _NT)
        k_raw = _dot(ckvn, wk_ref[...], _NT)
        kr = pltpu.roll(kr_ref[...], NOPE, 1)
        c, sm, sp = c_ref[...], sm_ref[...], sp_ref[...]
        heads = [slice(h * HEAD_PAD, (h + 1) * HEAD_PAD) for h in range(N_HEADS_B)]
        raw = [q_raw[:, hs] for hs in heads] + [k_raw[:, hs] + kr for hs in heads]
        d_out = [dq_ref[:, hs] for hs in heads] + [dk_ref[:, hs] for hs in heads]
        gains = [gqn_ref[...]] * N_HEADS_B + [gkn_ref[...]] * N_HEADS_B
        sq = [jnp.sum(x * x, axis=1, keepdims=True) for x in raw]
        rinv = [lax.rsqrt(s * (1.0 / QK_B) + EPS) for s in sq]
        xhat = [x * r for x, r in zip(raw, rinv)]
        down = [pltpu.roll(d * sm, ROPE // 2, 1) for d in d_out]
        up = [pltpu.roll(d * sp, HEAD_PAD - ROPE // 2, 1) for d in d_out]
        dn = [d * c + a + b for d, a, b in zip(d_out, down, up)]
        dgain = [jnp.sum(d * xh, axis=0, keepdims=True) for d, xh in zip(dn, xhat)]
        dxh = [d * g for d, g in zip(dn, gains)]
        inner = [jnp.sum(d * xh, axis=1, keepdims=True) * (1.0 / QK_B) for d, xh in zip(dxh, xhat)]
        d_raw = [r * (d - xh * s) for r, d, xh, s in zip(rinv, dxh, xhat, inner)]
        for h, hs in enumerate(heads):
            dqraw_ref[:, hs] = d_raw[h].astype(BF16)
            dkraw_ref[:, hs] = d_raw[N_HEADS_B + h].astype(BF16)
        dkr = sum(d_raw[N_HEADS_B + 1:], d_raw[N_HEADS_B])
        dgqn_ref[...] += sum(dgain[1:N_HEADS_B], dgain[0])
        dgkn_ref[...] += sum(dgain[N_HEADS_B + 1:], dgain[N_HEADS_B])
        lane = lax.broadcasted_iota(jnp.int32, (tm, HEAD_PAD), 1)
        dkr_ref[...] = jnp.where(lane < ROPE, pltpu.roll(dkr, HEAD_PAD - NOPE, 1), 0.0).astype(BF16)
        dqraw = dqraw_ref[...]
        dkraw = dkraw_ref[...]
        dvb = dv_ref[...].astype(BF16)
        dwuq_ref[...] += _dot(dqraw, cqn, _TN)
        dwk_ref[...] += _dot(dkraw, ckvn, _TN)
        dwv_ref[...] += _dot(dvb, ckvn, _TN)
        dcqn = _dot(dqraw, wuq_ref[...], _NN)
        dckvn = _dot(dkraw, wk_ref[...], _NN) + _dot(dvb, wv_ref[...], _NN)
        dgq_ref[...] += jnp.sum(dcqn * cq_xh, axis=0, keepdims=True)
        dgkv_ref[...] += jnp.sum(dckvn * ckv_xh, axis=0, keepdims=True)
        dxh = dcqn * gq_ref[...]
        dcq_ref[...] = (cq_r * (dxh - cq_xh * jnp.mean(dxh * cq_xh, axis=1, keepdims=True))).astype(BF16)
        dxh = dckvn * gkv_ref[...]
        dckv_ref[...] = (ckv_r * (dxh - ckv_xh * jnp.mean(dxh * ckv_xh, axis=1, keepdims=True))).astype(BF16)

    wide = pl.BlockSpec((tm, WIDTH_BP), lambda i: (i, 0))
    row = lambda w: pl.BlockSpec((tm, w), lambda i: (i, 0))
    full = lambda r, c: pl.BlockSpec((r, c), lambda i: (0, 0))
    return _tc_call(
        body, name=name, grid=(t // tm,), in_specs=_mla_in_specs(tm) + [wide, wide, wide],
        out_specs=[row(Q_LORA), row(KV_LORA), row(HEAD_PAD), full(WIDTH_BP, Q_LORA), full(WIDTH_BP, KV_LORA),
                   full(WIDTH_BP, KV_LORA), full(1, Q_LORA), full(1, KV_LORA), full(1, HEAD_PAD), full(1, HEAD_PAD)],
        out_shape=[jax.ShapeDtypeStruct((t, Q_LORA), BF16), jax.ShapeDtypeStruct((t, KV_LORA), BF16),
                   jax.ShapeDtypeStruct((t, HEAD_PAD), BF16), jax.ShapeDtypeStruct((WIDTH_BP, Q_LORA), F32),
                   jax.ShapeDtypeStruct((WIDTH_BP, KV_LORA), F32), jax.ShapeDtypeStruct((WIDTH_BP, KV_LORA), F32),
                   jax.ShapeDtypeStruct((1, Q_LORA), F32), jax.ShapeDtypeStruct((1, KV_LORA), F32),
                   jax.ShapeDtypeStruct((1, HEAD_PAD), F32), jax.ShapeDtypeStruct((1, HEAD_PAD), F32)],
        scratch_shapes=[pltpu.VMEM((tm, WIDTH_BP), BF16), pltpu.VMEM((tm, WIDTH_BP), BF16)],
        compiler_params=_cp("arbitrary"))(proj, proj, proj, *tabs, gq, gkv, gqn, gkn, wuq, wk, wv, dq, dk, dv)


def _mla_flash_specs(s_len):
    bh_spec = pl.BlockSpec((s_len, HEAD_PAD), lambda b, h: (b, h))
    lse_spec = pl.BlockSpec((1, s_len, 1), lambda b, h: (b * N_HEADS_B + h, 0, 0))
    return bh_spec, lse_spec


def _diag_mask(s):
    row = lax.broadcasted_iota(jnp.int32, s.shape, 0)
    col = lax.broadcasted_iota(jnp.int32, s.shape, 1)
    return jnp.where(row >= col, s, NEG)


def _mla_flash_fwd(q, k, v, n_batch, s_len, name):
    t = q.shape[0]
    tq = _tile(s_len, (256, 128))
    bh_spec, lse_spec = _mla_flash_specs(s_len)
    c = (QK_B ** -0.5) * LOG2E

    def body(q_ref, k_ref, v_ref, o_ref, lse_ref):
        nq = s_len // tq
        rows = [slice(i * tq, (i + 1) * tq) for i in range(nq)]
        below = [slice(0, i * tq) for i in range(nq)]
        qs = [q_ref[r, :] for r in rows]
        sd = [_diag_mask(_dot(qs[i], k_ref[rows[i], :], _NT)) for i in range(nq)]
        sb = [None] + [_dot(qs[i], k_ref[below[i], :], _NT) for i in range(1, nq)]
        m = [jnp.max(s, axis=1, keepdims=True) for s in sd]
        m = [m[0]] + [jnp.maximum(m[i], jnp.max(sb[i], axis=1, keepdims=True)) for i in range(1, nq)]
        pd = [jnp.exp2((sd[i] - m[i]) * c) for i in range(nq)]
        pb = [None] + [jnp.exp2((sb[i] - m[i]) * c) for i in range(1, nq)]
        l = [jnp.sum(p, axis=1, keepdims=True) for p in pd]
        l = [l[0]] + [l[i] + jnp.sum(pb[i], axis=1, keepdims=True) for i in range(1, nq)]
        acc = [_dot(pd[i].astype(BF16), v_ref[rows[i], :], _NN) for i in range(nq)]
        acc = [acc[0]] + [acc[i] + _dot(pb[i].astype(BF16), v_ref[below[i], :], _NN) for i in range(1, nq)]
        for i in range(nq):
            o_ref[rows[i], :] = (acc[i] * (1.0 / l[i])).astype(BF16)
            lse_ref[0, rows[i], :] = m[i] * c + jnp.log2(l[i])

    return _tc_call(
        body, name=name, grid=(n_batch, N_HEADS_B), in_specs=[bh_spec, bh_spec, bh_spec],
        out_specs=[bh_spec, lse_spec],
        out_shape=[jax.ShapeDtypeStruct((t, WIDTH_BP), BF16),
                   jax.ShapeDtypeStruct((n_batch * N_HEADS_B, s_len, 1), F32)],
        compiler_params=_cp("parallel", "parallel"))(q, k, v)


def _mla_flash_bwd(q, k, v, o, do, lse2, n_batch, s_len, name):
    t = q.shape[0]
    tq = _tile(s_len, (256, 128))
    bh_spec, lse_spec = _mla_flash_specs(s_len)
    scale = QK_B ** -0.5
    c = scale * LOG2E

    def body(q_ref, k_ref, v_ref, o_ref, do_ref, lse_ref, dq_ref, dk_ref, dv_ref):
        nq = s_len // tq
        rows = [slice(i * tq, (i + 1) * tq) for i in range(nq)]
        below = [slice(0, i * tq) for i in range(nq)]
        qs = [q_ref[r, :] for r in rows]
        dos = [do_ref[r, :] for r in rows]
        lse = [lse_ref[0, r, :] for r in rows]
        delta = [jnp.sum(dos[i].astype(F32) * o_ref[rows[i], :].astype(F32), axis=1, keepdims=True)
                 for i in range(nq)]

        def probs_and_ds(i, ks, diag):
            s = _dot(qs[i], k_ref[ks, :], _NT)
            if diag:
                s = _diag_mask(s)
            p = jnp.exp2(s * c - lse[i])
            dp = _dot(dos[i], v_ref[ks, :], _NT)
            return p.astype(BF16), (p * (dp - delta[i]) * scale).astype(BF16)

        diag = [probs_and_ds(i, rows[i], True) for i in range(nq)]
        rest = [None] + [probs_and_ds(i, below[i], False) for i in range(1, nq)]
        for i in range(nq):
            dq = _dot(diag[i][1], k_ref[rows[i], :], _NN)
            if i:
                dq = dq + _dot(rest[i][1], k_ref[below[i], :], _NN)
            dq_ref[rows[i], :] = dq
        for j in range(nq):
            later = slice(j * tq, s_len)
            p_j = jnp.concatenate([diag[j][0]] + [rest[i][0][:, rows[j]] for i in range(j + 1, nq)], axis=0)
            ds_j = jnp.concatenate([diag[j][1]] + [rest[i][1][:, rows[j]] for i in range(j + 1, nq)], axis=0)
            dk_ref[rows[j], :] = _dot(ds_j, q_ref[later, :], _TN)
            dv_ref[rows[j], :] = _dot(p_j, do_ref[later, :], _TN)

    f32_wide = jax.ShapeDtypeStruct((t, WIDTH_BP), F32)
    return _tc_call(
        body, name=name, grid=(n_batch, N_HEADS_B),
        in_specs=[bh_spec, bh_spec, bh_spec, bh_spec, bh_spec, lse_spec],
        out_specs=[bh_spec, bh_spec, bh_spec], out_shape=[f32_wide] * 3,
        compiler_params=_cp("parallel", "parallel"))(q, k, v, o, do, lse2)


def _swa_heads(w, axis, order):
    heads = [lax.slice_in_dim(w, h * HEAD_DIM_A, (h + 1) * HEAD_DIM_A, axis=axis) for h in order]
    return jnp.concatenate(heads, axis=axis)


class _LayerWeights:
    def __init__(self, build):
        self._build, self._mats = build, {}

    def __getitem__(self, name):
        if name not in self._mats:
            self._mats.update(self._build(name))
        return self._mats[name]


PIECES_OF = {"mla_w_ukv": ("wk", "wv")}


def _store(name, w):
    t = w.T if dict(BIG)[name] else w
    if name == "mla_w_uq":
        return {name: jnp.pad(t, ((0, HEAD_PAD - QK_B), (0, 0)))}
    if name == "mla_w_ukv":
        pad = ((0, HEAD_PAD - NOPE), (0, 0))
        return {"wk": jnp.pad(t[:NOPE], pad), "wv": jnp.pad(t[NOPE:], pad)}
    if name == "w_branch_b":
        t3 = jnp.pad(t.reshape(t.shape[0], N_HEADS_B, V_B), ((0, 0), (0, 0), (0, HEAD_PAD - V_B)))
        return {name: t3.reshape(t.shape[0], WIDTH_BP)}
    if name == "w_branch_a":
        return {name: _swa_heads(t, 1, SWA_HEAD_ORDER)}
    return {name: t}


def _unstore(name, pieces):
    if name == "mla_w_uq":
        return pieces[name][:QK_B]
    if name == "mla_w_ukv":
        return jnp.concatenate([pieces["wk"][:NOPE], pieces["wv"][:V_B]], axis=0)
    if name == "w_branch_b":
        g = pieces[name]
        return g.reshape(g.shape[0], N_HEADS_B, HEAD_PAD)[:, :, :V_B].reshape(g.shape[0], WIDTH_B)
    if name == "w_branch_a":
        return _swa_heads(pieces[name], 1, SWA_HEAD_INVERSE)
    return pieces[name]


def _layer_mats(w):
    if "w_in" not in w:
        return dict(w)
    w_in = w["w_in"]
    o = [0]
    for n in (WIDTH_A, KV_A, KV_A, Q_LORA, KV_LORA, ROPE, D_MODEL, D_MODEL):
        o.append(o[-1] + n)
    qa, ka, va, cq, ckv, kr, ga, gb = (w_in[o[i]:o[i + 1]] for i in range(8))
    pad = jnp.zeros((PROJ_W - IN_WIDTH, w_in.shape[1]), w_in.dtype)
    out = dict(w)
    out["w_in"] = jnp.concatenate([ga, gb, _swa_heads(qa, 0, SWA_HEAD_ORDER), cq, ka, va, ckv, kr, pad], axis=0)
    return out


def _unlayer_w_in(d):
    ga, gb, qa, cq, ka, va, ckv, kr = (d[a:b] for a, b in (
        (P_GA, P_GA + D_MODEL), (P_GB, P_GB + D_MODEL), (P_QA, P_QA + WIDTH_A), (P_CQ, P_CQ + Q_LORA),
        (P_KA, P_KA + KV_A), (P_VA, P_VA + KV_A), (P_CKV, P_CKV + KV_LORA), (P_KR, P_KR + ROPE)))
    return jnp.concatenate([_swa_heads(qa, 0, SWA_HEAD_INVERSE), ka, va, cq, ckv, kr, ga, gb], axis=0)


def _pad_lanes(v, width):
    return jnp.pad(v.reshape(1, -1), ((0, 0), (0, width - v.shape[-1])))


def _rope_tables(positions):
    half = ROPE // 2
    inv_freq = ROPE_BASE ** (-jnp.arange(half, dtype=F32) / half)
    ang = positions.astype(F32).reshape(-1, 1) * inv_freq
    cos, sin = jnp.cos(ang), jnp.sin(ang)
    t = cos.shape[0]
    one, zero = jnp.ones((t, NOPE), F32), jnp.zeros((t, NOPE), F32)
    tail = jnp.zeros((t, HEAD_PAD - QK_B), F32)
    z16 = jnp.zeros((t, half), F32)
    c = jnp.concatenate([one, cos, cos, tail], axis=1)
    sm = jnp.concatenate([zero, -sin, z16, tail], axis=1)
    sp = jnp.concatenate([zero, z16, sin, tail], axis=1)
    return c, sm, sp


def _ffn_fwd(x, gain, wg_t, wu_t, wd, tag):
    n, a, b, hmid = _ffn_up(x, gain, wg_t, wu_t, f"{tag}_up")
    y = _mm([(hmid, wd)], "nn", F32, f"{tag}_down", residual=x, alpha=0.5)
    return y, (x, n, a, b, hmid)


def _ffn_bwd(dy, dyb, saved, gain, wg_t, wu_t, wd, tag, grads, names, hook):
    x, n, a, b, hmid = saved
    da, db = _ffn_down_bwd(dyb, wd, a, b, f"{tag}_down_bwd")
    grads[names[0]] = _mm([(da, n)], "tn", BF16, f"{tag}_dwg")
    grads[names[1]] = _mm([(db, n)], "tn", BF16, f"{tag}_dwu")
    hook("gu", grads[names[1]])
    dx, dxb, g_gain = _mm_rms_bwd([(da, wg_t), (db, wu_t)], x, gain, dy, f"{tag}_dn")
    hook("dn", dx)
    grads[names[2]] = _mm([(hmid, dyb)], "tn", BF16, f"{tag}_dwd", alpha=0.5)
    return dx, dxb, g_gain


def _fold_halves(d):
    return d[:, :HEAD_DIM_A] + d[:, HEAD_DIM_A:]


def _local_step(x, positions, target, layers, smalls, at=None):
    at = at or (lambda point, l, token, grads: None)
    _Order.tokens = ()
    n_batch, s_len, d = x.shape
    t = n_batch * s_len
    xt = x.reshape(t, d)
    tabs = _rope_tables(positions)
    pos_col = positions.reshape(t, 1)
    pos_row = positions.reshape(t // BLOCK, 1, BLOCK)
    saved = []
    get_layer = layers if callable(layers) else layers.__getitem__
    for l in range(len(smalls)):
        w, s = get_layer(l), smalls[l]
        g1, gm, g2 = (s[k].reshape(1, d) for k in ("ffn1_norm", "mix_norm", "ffn2_norm"))
        qg2, kg2 = (jnp.tile(s[k].reshape(1, -1), (1, 2)) for k in ("swa_q_norm", "swa_k_norm"))
        sinks = _pad_lanes(s["swa_sinks"], HEAD_PAD)
        gq, gkv = s["mla_q_lora_norm"].reshape(1, -1), s["mla_kv_lora_norm"].reshape(1, -1)
        gqn, gkn = _pad_lanes(s["mla_q_norm"], HEAD_PAD), _pad_lanes(s["mla_k_norm"], HEAD_PAD)
        x1, sv1 = _ffn_fwd(xt, g1, w["ffn1_w_gate"], w["ffn1_w_up"], w["ffn1_w_down"], f"l{l}_ffn1")
        h, proj = _rms_mm(x1, gm, w["w_in"], f"l{l}_proj")
        at("proj", l, proj, None)
        oa = _swa_fwd(proj, pos_col, pos_row, qg2, kg2, sinks, n_batch, s_len, f"l{l}_swa")
        q, k, v = _mla_pre(proj, tabs, gq, gkv, gqn, gkn, w["mla_w_uq"], w["wk"], w["wv"], f"l{l}_mla_pre")
        ob, lse = _mla_flash_fwd(q, k, v, n_batch, s_len, f"l{l}_mla")
        merged, ya, yb = _merge_fwd(oa, ob, proj, w["w_branch_a"], w["w_branch_b"], f"l{l}_merge")
        x2 = _mm([(merged, w["w_out"])], "nn", F32, f"l{l}_out", residual=x1)
        at("out", l, x2, None)
        x3, sv2 = _ffn_fwd(x2, g2, w["ffn2_w_gate"], w["ffn2_w_up"], w["ffn2_w_down"], f"l{l}_ffn2")
        saved.append((w, sv1, sv2, x1, h, proj, oa, q, k, v, ob, lse, merged, ya, yb,
                      (g1, gm, g2, qg2, kg2, sinks, gq, gkv, gqn, gkn)))
        xt = x3

    dy, dyb, loss = _loss_head(xt, target.reshape(t, d))

    big_grads, small_grads = [None] * len(smalls), [None] * len(smalls)
    for l in reversed(range(len(smalls))):
        w, sv1, sv2, x1, h, proj, oa, q, k, v, ob, lse, merged, ya, yb, gains = saved[l]
        g1, gm, g2, qg2, kg2, sinks, gq, gkv, gqn, gkn = gains
        bg, sg = {}, {}
        dy, dyb, sg["ffn2_norm"] = _ffn_bwd(
            dy, dyb, sv2, g2, w["ffn2_w_gate"], w["ffn2_w_up"], w["ffn2_w_down"], f"l{l}_ffn2", bg, FFN2,
            lambda point, token, l=l, bg=bg: at("ffn2_" + point, l, token, bg))
        dya, dyb_, dga, dgb, doa, dob = _merge_bwd(dyb, w["w_out"], proj, ya, yb, w["w_branch_a"], w["w_branch_b"],
                                                   f"l{l}_merge_bwd")
        at("mixer", l, sg["ffn2_norm"], bg)
        bg["w_out"] = _mm([(merged, dyb)], "tn", BF16, f"l{l}_dwo")
        bg["w_branch_a"] = _mm([(dya, oa)], "tn", BF16, f"l{l}_dwa")
        bg["w_branch_b"] = _mm([(dyb_, ob)], "tn", BF16, f"l{l}_dwb")
        dqa, dkc, dkp, dvc, dvp, dqg, dsk = _swa_bwd(
            proj, pos_col, pos_row, qg2, kg2, sinks, doa, n_batch, s_len, f"l{l}_swa_bwd")
        sg["swa_q_norm"], sg["swa_sinks"] = _fold_halves(dqg), dsk[:, :N_HEADS_A]
        at("mixer_mid", l, dqa, bg)
        dka, dva, dkg = _swa_kv_bwd(proj, kg2, dkc, dkp, dvc, dvp, n_batch, s_len, f"l{l}_swa_kv_bwd")
        sg["swa_k_norm"] = _fold_halves(dkg)
        dq, dk, dv = _mla_flash_bwd(q, k, v, ob, dob, lse, n_batch, s_len, f"l{l}_mla_bwd")
        (dcq, dckv, dkr, g_uq, g_wk, g_wv, sg["mla_q_lora_norm"], sg["mla_kv_lora_norm"], dgqn, dgkn) = _mla_pre_bwd(
            proj, tabs, gq, gkv, gqn, gkn, w["mla_w_uq"], w["wk"], w["wv"], dq, dk, dv, f"l{l}_mla_pre_bwd")
        sg["mla_q_norm"], sg["mla_k_norm"] = dgqn[:, :QK_B], dgkn[:, :QK_B]
        bg["mla_w_uq"], bg["wk"], bg["wv"] = g_uq.astype(BF16), g_wk.astype(BF16), g_wv.astype(BF16)
        dproj = (dga, dgb, dqa, dcq, dka, dva, dckv, dkr)
        bg["w_in"] = _mm([(dproj, h)], "tn", BF16, f"l{l}_dwin")
        dy, dyb, sg["mix_norm"] = _mm_rms_bwd([(dproj, w["w_in"])], x1, gm, dy, f"l{l}_dh")
        at("ffn1", l, sg["mix_norm"], bg)
        dy, dyb, sg["ffn1_norm"] = _ffn_bwd(
            dy, dyb, sv1, g1, w["ffn1_w_gate"], w["ffn1_w_up"], w["ffn1_w_down"], f"l{l}_ffn1", bg, FFN1,
            lambda point, token, l=l, bg=bg: at("ffn1_" + point, l, token, bg))
        big_grads[l], small_grads[l] = bg, sg
        at("done", l, dy, bg)
    return loss, dy.reshape(n_batch, s_len, d), big_grads, small_grads


def _round_up(n, m):
    return (n + m - 1) // m * m


def _flat_layout(piece_shapes, members, row_tile):
    table, off = [], 0
    for l, piece in members:
        rows, k = piece_shapes[piece]
        pr = _round_up(rows * k // LANES, 16)
        table.append(((l, piece), off, pr, rows, k))
        off += pr
    return table, _round_up(off, row_tile)


def _pack_flat(stored, table, total):
    parts, off = [], 0
    for key, o, pr, rows, k in table:
        w = stored[key].reshape(rows * k // LANES, LANES)
        parts.append(jnp.pad(w, ((0, pr - w.shape[0]), (0, 0))))
        off = o + pr
    if total > off:
        parts.append(jnp.zeros((total - off, LANES), parts[0].dtype))
    return jnp.concatenate(parts, axis=0)


def _unpack_flat(flat, table):
    return {key: flat[o:o + rows * k // LANES].reshape(rows, k) for key, o, pr, rows, k in table}


def _gathered_mats(gathered, table, layer):
    return {piece: gathered[:, o:o + rows * k // LANES].reshape(N_DEV * rows, k)
            for (l, piece), o, pr, rows, k in table if l == layer}


def _pack_grads(grads, table, total):
    parts, off = [], 0
    for key, o, pr, rows, k in table:
        g = grads[key].reshape(N_DEV, rows * k // LANES, LANES)
        parts.append(jnp.pad(g, ((0, 0), (0, pr - g.shape[1]), (0, 0))))
        off = o + pr
    if total > off:
        parts.append(jnp.zeros((N_DEV, total - off, LANES), BF16))
    return jnp.concatenate(parts, axis=1)


def _pack_small(params, last=None):
    parts = [params[n][l].reshape(-1).astype(F32) for l in range(DEPTH) for n in SMALL]
    v = jnp.concatenate(parts)
    v = jnp.pad(v, (0, SMALL_ROWS * LANES - 1 - v.shape[0]))
    last = jnp.zeros((1,), F32) if last is None else last.reshape(1)
    return jnp.concatenate([v, last]).reshape(SMALL_ROWS, LANES)


def _unpack_small(flat, shapes):
    v, out, off = flat.reshape(-1), {}, 0
    for l in range(DEPTH):
        for n in SMALL:
            sz = math.prod(shapes[n][1:])
            out.setdefault(n, []).append(v[off:off + sz].reshape(shapes[n][1:]))
            off += sz
    return {n: jnp.stack(p) for n, p in out.items()}


_MESH = pl.DeviceIdType.MESH


def _place():
    return lax.axis_index("x"), lax.axis_index("y"), lax.axis_index("c")


def _handshake(peers):
    barrier = pltpu.get_barrier_semaphore()
    for peer in peers:
        pl.semaphore_signal(barrier, inc=1, device_id=peer, device_id_type=_MESH)
    pl.semaphore_wait(barrier, len(peers))


def _comm_call(body, out_shape, scratch, name, seq_id, spec=_ANY):
    if seq_id is None:
        return pl.pallas_call(body, name=name, out_shape=out_shape, in_specs=[spec, _ANY], out_specs=spec,
                              scratch_shapes=scratch)
    nbytes = LINK_COST_SCALE * math.prod(out_shape.shape) * out_shape.dtype.itemsize
    return pl.kernel(body, out_type=out_shape, mesh=plsc.ScalarSubcoreMesh(axis_name="sequencer", num_cores=1),
                     scratch_types=scratch, name=name, compiler_params=pltpu.CompilerParams(collective_id=seq_id),
                     cost_estimate=pl.CostEstimate(flops=0, transcendentals=0, bytes_accessed=nbytes))


def _all_gather(x_shard, name, vmem=False, seq_id=None, after=None):
    spec = pl.BlockSpec(memory_space=pltpu.VMEM) if vmem else _ANY

    def body(x_ref, after_ref, out_ref, send_sems, recv_sems, local_sem):
        x, y, c = _place()
        me, sibling = (x, y, c), (x, y, 1 - c)
        chips = [(1 - x, y), (x, 1 - y), (1 - x, 1 - y)]
        if seq_id is not None:
            _handshake([sibling] + [(*chip, c) for chip in chips])

        def rows(px, py, pc):
            return out_ref.at[4 * px + 2 * py + pc]

        def copy(k, block, to, src=None):
            return pltpu.make_async_remote_copy(
                src_ref=rows(*block) if src is None else src, dst_ref=rows(*block),
                send_sem=send_sems.at[k], recv_sem=recv_sems.at[k], device_id=to, device_id_type=_MESH)

        mine = pltpu.make_async_copy(x_ref, rows(*me), local_sem)
        mine.start()
        first = [copy(0, me, sibling, src=x_ref)]
        first += [copy(1 + j, me, (*chip, c), src=x_ref) for j, chip in enumerate(chips)]
        for cp in first:
            cp.start()
        passed = [copy(4 + j, (*chip, c), sibling) for j, chip in enumerate(chips)]
        for j, chip in enumerate(chips):
            copy(1 + j, (*chip, c), me).wait_recv()
            passed[j].start()
        copy(0, sibling, me).wait_recv()
        for j, chip in enumerate(chips):
            copy(4 + j, (*chip, 1 - c), me).wait_recv()
        for cp in first + passed:
            cp.wait_send()
        mine.wait()

    return _comm_call(
        body, jax.ShapeDtypeStruct((N_DEV,) + x_shard.shape, x_shard.dtype),
        [pltpu.SemaphoreType.DMA((7,)), pltpu.SemaphoreType.DMA((7,)), pltpu.SemaphoreType.DMA], name, seq_id,
        spec)(x_shard, x_shard if after is None else after)


def _exchange_cores(g4, name, seq_id=None, after=None):
    n_chip, _, r, w = g4.shape

    def body(g_ref, after_ref, out_ref, send_sems, recv_sems):
        x, y, c = _place()
        if seq_id is not None:
            _handshake([(x, y, 1 - c)])
        copies = [pltpu.make_async_remote_copy(
            src_ref=g_ref.at[q, 1 - c], dst_ref=out_ref.at[q], send_sem=send_sems.at[q], recv_sem=recv_sems.at[q],
            device_id=(x, y, 1 - c), device_id_type=_MESH) for q in range(n_chip)]
        for cp in copies:
            cp.start()
        for cp in copies:
            cp.wait()

    return _comm_call(
        body, jax.ShapeDtypeStruct((n_chip, r, w), g4.dtype),
        [pltpu.SemaphoreType.DMA((n_chip,)), pltpu.SemaphoreType.DMA((n_chip,))], name, seq_id)(g4, g4 if after is None else after)


def _exchange_chips(s1, name, seq_id=None):
    _, r, w = s1.shape

    def body(s_ref, after_ref, out_ref, send_sems, recv_sems):
        x, y, c = _place()
        chips = [(1 - x, y), (x, 1 - y), (1 - x, 1 - y)]
        if seq_id is not None:
            _handshake([(*chip, c) for chip in chips])
        copies = []
        for k, (tx, ty) in enumerate(chips):
            copies.append(pltpu.make_async_remote_copy(
                src_ref=s_ref.at[2 * tx + ty], dst_ref=out_ref.at[k], send_sem=send_sems.at[k],
                recv_sem=recv_sems.at[k], device_id=(tx, ty, c), device_id_type=_MESH))
        for cp in copies:
            cp.start()
        for cp in copies:
            cp.wait()

    return _comm_call(
        body, jax.ShapeDtypeStruct((3, r, w), s1.dtype),
        [pltpu.SemaphoreType.DMA((3,)), pltpu.SemaphoreType.DMA((3,))], name, seq_id)(s1, s1)


def _chip_sum(g4, recv, core, after, name, tr):
    n_chip, _, r, w = g4.shape

    def body(c_ref, a_ref, b_ref, after_ref, o_ref):
        o_ref[...] = (a_ref[...].astype(F32) + b_ref[...].astype(F32)).astype(o_ref.dtype)

    grid_spec = pltpu.PrefetchScalarGridSpec(
        num_scalar_prefetch=1, grid=(n_chip, r // tr),
        in_specs=[pl.BlockSpec((None, None, tr, w), lambda q, i, c: (q, c[0], i, 0)),
                  pl.BlockSpec((None, tr, w), lambda q, i, c: (q, i, 0)), _ANY],
        out_specs=pl.BlockSpec((None, tr, w), lambda q, i, c: (q, i, 0)))
    return pl.pallas_call(
        body, name=name, grid_spec=grid_spec, out_shape=jax.ShapeDtypeStruct((n_chip, r, w), g4.dtype),
        compiler_params=_cp("parallel", "parallel"))(core, g4, recv, after)


def _adam(w, g, m, v):
    m = ADAM_B1 * m + (1.0 - ADAM_B1) * g
    v = ADAM_B2 * v + (1.0 - ADAM_B2) * (g * g)
    m_hat = m / (1.0 - ADAM_B1 ** ADAM_STEP)
    v_hat = v / (1.0 - ADAM_B2 ** ADAM_STEP)
    delta = -ADAM_LR * (m_hat / (jnp.sqrt(v_hat) + ADAM_EPS) + ADAM_WD * w)
    return delta, m, v


def _grad_sum(s1, r2, chip, name, tr):
    _, r, lanes = s1.shape

    def body(c_ref, s_ref, r0_ref, r1_ref, r2_ref, g_out):
        g_out[...] = ((s_ref[...].astype(F32) + r0_ref[...].astype(F32)) + r1_ref[...].astype(F32)) + r2_ref[
            ...].astype(F32)

    row = pl.BlockSpec((tr, lanes), lambda i, c: (i, 0))
    rel = lambda k: pl.BlockSpec((None, tr, lanes), lambda i, c: (k, i, 0))
    grid_spec = pltpu.PrefetchScalarGridSpec(
        num_scalar_prefetch=1, grid=(r // tr,),
        in_specs=[pl.BlockSpec((None, tr, lanes), lambda i, c: (c[0], i, 0)), rel(0), rel(1), rel(2)], out_specs=row)
    return pl.pallas_call(
        body, name=name, grid_spec=grid_spec, out_shape=jax.ShapeDtypeStruct((r, lanes), F32),
        compiler_params=_cp("parallel"))(chip, s1, r2, r2, r2)


def _adam_big(w, g, m, v, name):
    depth, k, n = w.shape
    tk = k if k <= 512 else _tile(k, (256, 128))

    def body(w_ref, g_ref, m_ref, v_ref, d_out, m_out, v_out):
        d, mn, vn = _adam(w_ref[...], g_ref[...], m_ref[...], v_ref[...])
        d_out[...] = d
        m_out[...] = mn
        v_out[...] = vn

    blk = pl.BlockSpec((None, tk, n), lambda l, i: (l, i, 0))
    return pl.pallas_call(
        body, name=name, grid=(depth, k // tk), in_specs=[blk] * 4, out_specs=[blk] * 3,
        out_shape=[jax.ShapeDtypeStruct(w.shape, F32)] * 3, compiler_params=_cp("parallel", "parallel"))(w, g, m, v)


def _adam_small(parts, w, m, v, name):
    rows, lanes = w.shape

    def body(p_ref, w_ref, m_ref, v_ref, g_out, d_out, m_out, v_out):
        g = p_ref[0:rows, :]
        for dev in range(1, N_DEV):
            g = g + p_ref[dev * rows:(dev + 1) * rows, :]
        d, mn, vn = _adam(w_ref[...], g, m_ref[...], v_ref[...])
        g_out[...] = g
        d_out[...] = d
        m_out[...] = mn
        v_out[...] = vn

    return pl.pallas_call(
        body, name=name, out_shape=[jax.ShapeDtypeStruct((rows, lanes), F32)] * 4)(parts, w, m, v)


def kernel(x, positions, ffn1_norm, ffn1_w_gate, ffn1_w_up, ffn1_w_down, mix_norm, w_in, swa_q_norm, swa_k_norm, swa_sinks, mla_q_lora_norm, mla_w_uq, mla_kv_lora_norm, mla_w_ukv, mla_q_norm, mla_k_norm, w_branch_a, w_branch_b, w_out, ffn2_norm, ffn2_w_gate, ffn2_w_up, ffn2_w_down, loss_target, m_ffn1_norm, m_ffn1_w_gate, m_ffn1_w_up, m_ffn1_w_down, m_mix_norm, m_w_in, m_swa_q_norm, m_swa_k_norm, m_swa_sinks, m_mla_q_lora_norm, m_mla_w_uq, m_mla_kv_lora_norm, m_mla_w_ukv, m_mla_q_norm, m_mla_k_norm, m_w_branch_a, m_w_branch_b, m_w_out, m_ffn2_norm, m_ffn2_w_gate, m_ffn2_w_up, m_ffn2_w_down, v_ffn1_norm, v_ffn1_w_gate, v_ffn1_w_up, v_ffn1_w_down, v_mix_norm, v_w_in, v_swa_q_norm, v_swa_k_norm, v_swa_sinks, v_mla_q_lora_norm, v_mla_w_uq, v_mla_kv_lora_norm, v_mla_w_ukv, v_mla_q_norm, v_mla_k_norm, v_w_branch_a, v_w_branch_b, v_w_out, v_ffn2_norm, v_ffn2_w_gate, v_ffn2_w_up, v_ffn2_w_down):
    given = dict(locals())
    params = {n: given[n] for n in WEIGHTS}
    mom1 = {n: given["m_" + n] for n in WEIGHTS}
    mom2 = {n: given["v_" + n] for n in WEIGHTS}
    assert N_HEADS_B == N_DEV
    stored = {(l, piece): w for l in range(DEPTH) for n, _ in BIG for piece, w in _store(n, params[n][l]).items()}
    piece_shapes = {piece: w.shape for (l, piece), w in stored.items() if l == 0}
    gsegs = [(members, tile) + _flat_layout(piece_shapes, members, tile) for members, tile in GATHER_SEGMENTS]
    rsegs = [(members, tile) + _flat_layout(piece_shapes, members, tile) for members, tile in SCATTER_SEGMENTS]

    def members_of(seg, l):
        return [n for sl, n in seg[0] if sl == l]

    cx, cy, cc = _place()
    core = jnp.reshape(cc, (1,)).astype(jnp.int32)
    chip = jnp.reshape(2 * cx + cy, (1,)).astype(jnp.int32)

    gathered = []
    for s, (_, _, table, total) in enumerate(gsegs):
        w_flat = _pack_flat(stored, table, total).astype(BF16)
        gathered.append(_all_gather(w_flat, f"gather_s{s}", seq_id=SEQ_IDS["gather", s]))

    def get_layer(l):
        def build(name):
            for seg, g in zip(gsegs, gathered):
                if name in members_of(seg, l):
                    return _layer_mats(_gathered_mats(g, seg[2], l))
            raise KeyError(name)
        return _LayerWeights(build)

    smalls = [{n: params[n][l] for n in SMALL} for l in range(DEPTH)]

    pending, big_out, layer_grads = {}, [None] * len(rsegs), {}

    def exchange_cores(s):
        _, _, table, total = rsegs[s]
        mine = {(l, n): _unlayer_w_in(layer_grads[l][n]) if n == "w_in" else layer_grads[l][n] for l, n in rsegs[s][0]}
        g_flat = _pack_grads(mine, table, total)
        g4 = g_flat.reshape(N_DEV // 2, 2, total, LANES)
        pending[s] = (g4, _exchange_cores(g4, f"scatter_cores_s{s}", seq_id=SEQ_IDS["cores", s]))

    def exchange_chips(s, after):
        g4, from_core = pending.pop(s)
        s1 = _chip_sum(g4, from_core, core, after, f"sum_cores_s{s}", rsegs[s][3])
        _Order.tokens = (s1,)
        pending[s] = (s1, _exchange_chips(s1, f"scatter_chips_s{s}", seq_id=SEQ_IDS["chips", s]))

    def finish(s):
        s1, from_chips = pending.pop(s)
        big_out[s] = _grad_sum(s1, from_chips, chip, f"grad_sum_s{s}", rsegs[s][1])

    plan = {("ffn1", 1): [("cores", 4)], ("ffn1_gu", 1): [("chips", 4)],
            ("mixer", 0): [("wait", 4), ("cores", 3)], ("mixer_mid", 0): [("chips", 3)],
            ("ffn1", 0): [("wait", 3), ("cores", 2)], ("ffn1_gu", 0): [("cores", 1), ("chips", 2)],
            ("ffn1_dn", 0): [("chips", 1)], ("done", 0): [("cores", 0)]}

    def at(point, l, token, grads):
        if grads is not None:
            layer_grads[l] = grads
        for what, s in plan.get((point, l), ()):
            if what == "cores":
                exchange_cores(s)
            elif what == "chips":
                exchange_chips(s, token)
            else:
                _Order.tokens += (pending[s][1],)

    loss, grad_x, _, small_grads = _local_step(x, positions, loss_target, get_layer, smalls, at)
    exchange_chips(0, grad_x)

    g_small = _pack_small({n: [small_grads[l][n] for l in range(DEPTH)] for n in SMALL}, loss)
    parts = _all_gather(g_small, "gather_small", vmem=True).reshape(N_DEV * SMALL_ROWS, LANES)
    small_out = _adam_small(parts, _pack_small(params), _pack_small(mom1), _pack_small(mom2), "adam_small")
    shapes = {n: params[n].shape for n in SMALL}
    outs = [_unpack_small(small, shapes) for small in small_out]
    loss = small_out[0].reshape(-1)[-1]

    pieces = {}
    for s in reversed(range(len(rsegs))):
        finish(s)
        pieces.update(_unpack_flat(big_out[s], rsegs[s][2]))
    last = SCATTER_SEGMENTS[0][0][0][1]
    for n, tr in sorted(BIG, key=lambda entry: entry[0] == last):
        view = (lambda a: jnp.swapaxes(a, 1, 2)) if tr else (lambda a: a)
        g = jnp.stack([_unstore(n, {p: pieces[l, p] for p in PIECES_OF.get(n, (n,))}) for l in range(DEPTH)])
        updated = _adam_big(view(params[n]), g, view(mom1[n]), view(mom2[n]), f"adam_{n}")
        for tree, leaf in zip(outs, (g,) + tuple(updated)):
            tree[n] = view(leaf)
    return (loss, grad_x, *[o[n] for o in outs for n in WEIGHTS])
```
